```python
import jax, jax.numpy as jnp
from jax import lax
import numpy as np

D_MODEL = 1024
BATCH = 16
SEQ = 4096
DEPTH = 4

N_META = 16
EPS = 1e-6
CONV_WIDTH = D_MODEL // 2
CONV_GROUPS = 8
SHORT_CONV_K = 3
LRU_WIDTH = D_MODEL // 2
LRU_HEADS = 8
LRU_HEAD_DIM = LRU_WIDTH // LRU_HEADS
LRU_CONV_K = 4
LRU_C = 8.0
EVEN_IN = 3 * CONV_WIDTH + 2 * LRU_WIDTH
EVEN_MIX = CONV_WIDTH + LRU_WIDTH
MLA_HEADS = 16
QK_NOPE = 64
QK_ROPE = 32
QK_HEAD = QK_NOPE + QK_ROPE
V_HEAD = 64
Q_LORA = 384
KV_LORA = 256
ODD_IN = Q_LORA + KV_LORA + QK_ROPE
ROPE_BASE = 10000.0
ATTN_BLOCK = 128
D_FF = 2816
FFN_CONV_K = 3
N_EVEN = (DEPTH + 1) // 2
N_ODD = DEPTH // 2

kernel_name = "hybrid_conv_rglru_mla_convffn"


def rms_norm(x, g):
    xf = x.astype(jnp.float32)
    y = xf * lax.rsqrt(jnp.mean(xf * xf, axis=-1, keepdims=True) + EPS)
    return (y * g.astype(jnp.float32)).astype(x.dtype)


def causal_dwconv(x, w):
    k_width = w.shape[0]
    t_len = x.shape[1]
    xp = jnp.pad(x, ((0, 0), (k_width - 1, 0), (0, 0)))
    y = xp[:, 0:t_len] * w[0]
    for k in range(1, k_width):
        y = y + xp[:, k:k + t_len] * w[k]
    return y


def rope_tables(t_len):
    pos = jnp.arange(t_len, dtype=jnp.float32)
    inv_freq = ROPE_BASE ** (-jnp.arange(0, QK_ROPE, 2, dtype=jnp.float32) / QK_ROPE)
    ang = pos[:, None] * inv_freq[None, :]
    return jnp.cos(ang), jnp.sin(ang)


def apply_rope(x, cos, sin):
    xf = x.astype(jnp.float32)
    x1, x2 = jnp.split(xf, 2, axis=-1)
    out = jnp.concatenate([x1 * cos - x2 * sin, x2 * cos + x1 * sin], axis=-1)
    return out.astype(x.dtype)


def rg_lru(xc, r_w, r_b, i_w, i_b, lam):
    b, t, _ = xc.shape
    xh = xc.reshape(b, t, LRU_HEADS, LRU_HEAD_DIM)
    r = jax.nn.sigmoid(jnp.einsum('bthi,hij->bthj', xh, r_w).reshape(b, t, LRU_WIDTH) + r_b)
    i = jax.nn.sigmoid(jnp.einsum('bthi,hij->bthj', xh, i_w).reshape(b, t, LRU_WIDTH) + i_b)
    log_a = -LRU_C * r.astype(jnp.float32) * jax.nn.softplus(-lam.astype(jnp.float32))
    a = jnp.exp(log_a)
    mult = jnp.sqrt(-jnp.expm1(2.0 * log_a))
    u = mult * (i * xc).astype(jnp.float32)

    def combine(left, right):
        a1, b1 = left
        a2, b2 = right
        return a1 * a2, a2 * b1 + b2

    _, h = lax.associative_scan(combine, (a, u), axis=1)
    return h.astype(xc.dtype)


def even_layer(x, norm, w_in, conv_a, conv_b, conv_b_bias, r_w, r_b, i_w, i_b, lam, w_out):
    h = rms_norm(x, norm)
    u = h @ w_in
    gb, gc, xa, xb, gate = jnp.split(
        u, [CONV_WIDTH, 2 * CONV_WIDTH, 3 * CONV_WIDTH, 3 * CONV_WIDTH + LRU_WIDTH], axis=-1)
    y_a = gb * causal_dwconv(gc * xa, conv_a)
    xc = causal_dwconv(xb, conv_b) + conv_b_bias
    y_b = jax.nn.gelu(gate) * rg_lru(xc, r_w, r_b, i_w, i_b, lam)
    return x + jnp.concatenate([y_a, y_b], axis=-1) @ w_out


def causal_block_attention(q, k, v):
    b, t, nh, dq = q.shape
    nb = -(-t // ATTN_BLOCK)
    tp = nb * ATTN_BLOCK
    pad = ((0, 0), (0, tp - t), (0, 0), (0, 0))
    q, k, v = jnp.pad(q, pad), jnp.pad(k, pad), jnp.pad(v, pad)
    qb = q.reshape(b, nb, ATTN_BLOCK, nh, dq).transpose(1, 0, 2, 3, 4)
    key_pos = jnp.arange(tp)
    scale = QK_HEAD ** -0.5
    neg = jnp.finfo(jnp.float32).min

    def one_block(args):
        q_blk, blk = args
        s = jnp.einsum('bqhd,bkhd->bhqk', q_blk, k).astype(jnp.float32) * scale
        q_pos = blk * ATTN_BLOCK + jnp.arange(ATTN_BLOCK)
        mask = key_pos[None, :] <= q_pos[:, None]
        s = jnp.where(mask[None, None], s, neg)
        p = jax.nn.softmax(s, axis=-1).astype(v.dtype)
        return jnp.einsum('bhqk,bkhd->bqhd', p, v)

    out = lax.map(one_block, (qb, jnp.arange(nb)))
    out = out.transpose(1, 0, 2, 3, 4).reshape(b, tp, nh, V_HEAD)
    return out[:, :t]


def odd_layer(x, cos, sin, norm, w_in, q_norm, kv_norm, w_uq, w_ukv, w_out):
    b, t, _ = x.shape
    h = rms_norm(x, norm)
    u = h @ w_in
    cq, ckv, k_r = jnp.split(u, [Q_LORA, Q_LORA + KV_LORA], axis=-1)
    q = (rms_norm(cq, q_norm) @ w_uq).reshape(b, t, MLA_HEADS, QK_HEAD)
    q_nope, q_rope = jnp.split(q, [QK_NOPE], axis=-1)
    q_rope = apply_rope(q_rope, cos[:, None, :], sin[:, None, :])
    kv = (rms_norm(ckv, kv_norm) @ w_ukv).reshape(b, t, MLA_HEADS, QK_NOPE + V_HEAD)
    k_nope, v = jnp.split(kv, [QK_NOPE], axis=-1)
    k_rope = apply_rope(k_r, cos, sin)
    k_rope = jnp.broadcast_to(k_rope[:, :, None, :], (b, t, MLA_HEADS, QK_ROPE))
    qf = jnp.concatenate([q_nope, q_rope], axis=-1)
    kf = jnp.concatenate([k_nope, k_rope], axis=-1)
    o = causal_block_attention(qf, kf, v).reshape(b, t, MLA_HEADS * V_HEAD)
    return x + o @ w_out


def ffn_layer(x, norm, w_up, conv_w, conv_b, w_down):
    h = rms_norm(x, norm)
    u = causal_dwconv(h @ w_up, conv_w) + conv_b
    a, g = jnp.split(u, 2, axis=-1)
    return x + (jax.nn.silu(a) * g) @ w_down


def _fwd_setup_inputs(seed: int = 0) -> dict:
    key = jax.random.key(seed)
    ks = iter(jax.random.split(key, 40))

    def nrm(shape, scale):
        return jax.random.normal(next(ks), shape, jnp.float32) * scale

    def gain(shape):
        return 1.0 + nrm(shape, 0.01)

    u = jax.random.uniform(next(ks), (N_EVEN, LRU_WIDTH), jnp.float32, 0.9, 0.999)
    a_base = u ** (1.0 / LRU_C)
    lam = jnp.log(a_base) - jnp.log1p(-a_base)
    return {
        "x": nrm((BATCH, SEQ, D_MODEL), 1.0),
        "meta_tokens": nrm((N_META, D_MODEL), 1.0),
        "ev_norm": gain((N_EVEN, D_MODEL)),
        "ev_w_in": nrm((N_EVEN, D_MODEL, EVEN_IN), D_MODEL ** -0.5),
        "ev_conv_a": nrm((N_EVEN, SHORT_CONV_K, CONV_WIDTH), SHORT_CONV_K ** -0.5),
        "ev_conv_b": nrm((N_EVEN, LRU_CONV_K, LRU_WIDTH), LRU_CONV_K ** -0.5),
        "ev_conv_b_bias": nrm((N_EVEN, LRU_WIDTH), 0.02),
        "ev_gate_r_w": nrm((N_EVEN, LRU_HEADS, LRU_HEAD_DIM, LRU_HEAD_DIM), LRU_HEAD_DIM ** -0.5),
        "ev_gate_r_b": nrm((N_EVEN, LRU_WIDTH), 0.02),
        "ev_gate_i_w": nrm((N_EVEN, LRU_HEADS, LRU_HEAD_DIM, LRU_HEAD_DIM), LRU_HEAD_DIM ** -0.5),
        "ev_gate_i_b": nrm((N_EVEN, LRU_WIDTH), 0.02),
        "ev_lru_lambda": lam,
        "ev_w_out": nrm((N_EVEN, EVEN_MIX, D_MODEL), EVEN_MIX ** -0.5),
        "od_norm": gain((N_ODD, D_MODEL)),
        "od_w_in": nrm((N_ODD, D_MODEL, ODD_IN), D_MODEL ** -0.5),
        "od_q_norm": gain((N_ODD, Q_LORA)),
        "od_kv_norm": gain((N_ODD, KV_LORA)),
        "od_w_uq": nrm((N_ODD, Q_LORA, MLA_HEADS * QK_HEAD), Q_LORA ** -0.5),
        "od_w_ukv": nrm((N_ODD, KV_LORA, MLA_HEADS * (QK_NOPE + V_HEAD)), KV_LORA ** -0.5),
        "od_w_out": nrm((N_ODD, MLA_HEADS * V_HEAD, D_MODEL), (MLA_HEADS * V_HEAD) ** -0.5),
        "ffn_norm": gain((DEPTH, D_MODEL)),
        "ffn_w_up": nrm((DEPTH, D_MODEL, 2 * D_FF), D_MODEL ** -0.5),
        "ffn_conv_w": nrm((DEPTH, FFN_CONV_K, 2 * D_FF), FFN_CONV_K ** -0.5),
        "ffn_conv_b": nrm((DEPTH, 2 * D_FF), 0.02),
        "ffn_w_down": nrm((DEPTH, D_FF, D_MODEL), D_FF ** -0.5),
        "final_norm": gain((D_MODEL,)),
    }


def _fwd_reference(x, meta_tokens, ev_norm, ev_w_in, ev_conv_a, ev_conv_b, ev_conv_b_bias,
              ev_gate_r_w, ev_gate_r_b, ev_gate_i_w, ev_gate_i_b, ev_lru_lambda, ev_w_out,
              od_norm, od_w_in, od_q_norm, od_kv_norm, od_w_uq, od_w_ukv, od_w_out,
              ffn_norm, ffn_w_up, ffn_conv_w, ffn_conv_b, ffn_w_down, final_norm):
    b = x.shape[0]
    meta = jnp.broadcast_to(meta_tokens[None].astype(x.dtype), (b, N_META, D_MODEL))
    h = jnp.concatenate([meta, x], axis=1)
    cos, sin = rope_tables(h.shape[1])
    for layer in range(DEPTH):
        j = layer // 2
        if layer % 2 == 0:
            h = even_layer(h, ev_norm[j], ev_w_in[j], ev_conv_a[j], ev_conv_b[j], ev_conv_b_bias[j],
                           ev_gate_r_w[j], ev_gate_r_b[j], ev_gate_i_w[j], ev_gate_i_b[j],
                           ev_lru_lambda[j], ev_w_out[j])
        else:
            h = odd_layer(h, cos, sin, od_norm[j], od_w_in[j], od_q_norm[j], od_kv_norm[j],
                          od_w_uq[j], od_w_ukv[j], od_w_out[j])
        h = ffn_layer(h, ffn_norm[layer], ffn_w_up[layer], ffn_conv_w[layer], ffn_conv_b[layer],
                      ffn_w_down[layer])
    h = rms_norm(h, final_norm)
    return h[:, N_META:]


import jax as _jax
import jax.numpy as _jnp

TWIN_FORMAT = 'train_step'
FWD_PARAMS = ['x', 'meta_tokens', 'ev_norm', 'ev_w_in', 'ev_conv_a', 'ev_conv_b', 'ev_conv_b_bias', 'ev_gate_r_w', 'ev_gate_r_b', 'ev_gate_i_w', 'ev_gate_i_b', 'ev_lru_lambda', 'ev_w_out', 'od_norm', 'od_w_in', 'od_q_norm', 'od_kv_norm', 'od_w_uq', 'od_w_ukv', 'od_w_out', 'ffn_norm', 'ffn_w_up', 'ffn_conv_w', 'ffn_conv_b', 'ffn_w_down', 'final_norm']
TWIN_WEIGHTS = ['meta_tokens', 'ev_norm', 'ev_w_in', 'ev_conv_a', 'ev_conv_b', 'ev_conv_b_bias', 'ev_gate_r_w', 'ev_gate_r_b', 'ev_gate_i_w', 'ev_gate_i_b', 'ev_lru_lambda', 'ev_w_out', 'od_norm', 'od_w_in', 'od_q_norm', 'od_kv_norm', 'od_w_uq', 'od_w_ukv', 'od_w_out', 'ffn_norm', 'ffn_w_up', 'ffn_conv_w', 'ffn_conv_b', 'ffn_w_down', 'final_norm']
TWIN_DIFF_INPUT = 'x'
TWIN_INPUTS = ['x', 'meta_tokens', 'ev_norm', 'ev_w_in', 'ev_conv_a', 'ev_conv_b', 'ev_conv_b_bias', 'ev_gate_r_w', 'ev_gate_r_b', 'ev_gate_i_w', 'ev_gate_i_b', 'ev_lru_lambda', 'ev_w_out', 'od_norm', 'od_w_in', 'od_q_norm', 'od_kv_norm', 'od_w_uq', 'od_w_ukv', 'od_w_out', 'ffn_norm', 'ffn_w_up', 'ffn_conv_w', 'ffn_conv_b', 'ffn_w_down', 'final_norm', 'loss_target', 'm_meta_tokens', 'm_ev_norm', 'm_ev_w_in', 'm_ev_conv_a', 'm_ev_conv_b', 'm_ev_conv_b_bias', 'm_ev_gate_r_w', 'm_ev_gate_r_b', 'm_ev_gate_i_w', 'm_ev_gate_i_b', 'm_ev_lru_lambda', 'm_ev_w_out', 'm_od_norm', 'm_od_w_in', 'm_od_q_norm', 'm_od_kv_norm', 'm_od_w_uq', 'm_od_w_ukv', 'm_od_w_out', 'm_ffn_norm', 'm_ffn_w_up', 'm_ffn_conv_w', 'm_ffn_conv_b', 'm_ffn_w_down', 'm_final_norm', 'v_meta_tokens', 'v_ev_norm', 'v_ev_w_in', 'v_ev_conv_a', 'v_ev_conv_b', 'v_ev_conv_b_bias', 'v_ev_gate_r_w', 'v_ev_gate_r_b', 'v_ev_gate_i_w', 'v_ev_gate_i_b', 'v_ev_lru_lambda', 'v_ev_w_out', 'v_od_norm', 'v_od_w_in', 'v_od_q_norm', 'v_od_kv_norm', 'v_od_w_uq', 'v_od_w_ukv', 'v_od_w_out', 'v_ffn_norm', 'v_ffn_w_up', 'v_ffn_conv_w', 'v_ffn_conv_b', 'v_ffn_w_down', 'v_final_norm']
TWIN_OUTPUTS = ['loss', 'grad_x', 'grad_meta_tokens', 'grad_ev_norm', 'grad_ev_w_in', 'grad_ev_conv_a', 'grad_ev_conv_b', 'grad_ev_conv_b_bias', 'grad_ev_gate_r_w', 'grad_ev_gate_r_b', 'grad_ev_gate_i_w', 'grad_ev_gate_i_b', 'grad_ev_lru_lambda', 'grad_ev_w_out', 'grad_od_norm', 'grad_od_w_in', 'grad_od_q_norm', 'grad_od_kv_norm', 'grad_od_w_uq', 'grad_od_w_ukv', 'grad_od_w_out', 'grad_ffn_norm', 'grad_ffn_w_up', 'grad_ffn_conv_w', 'grad_ffn_conv_b', 'grad_ffn_w_down', 'grad_final_norm', 'delta_meta_tokens', 'delta_ev_norm', 'delta_ev_w_in', 'delta_ev_conv_a', 'delta_ev_conv_b', 'delta_ev_conv_b_bias', 'delta_ev_gate_r_w', 'delta_ev_gate_r_b', 'delta_ev_gate_i_w', 'delta_ev_gate_i_b', 'delta_ev_lru_lambda', 'delta_ev_w_out', 'delta_od_norm', 'delta_od_w_in', 'delta_od_q_norm', 'delta_od_kv_norm', 'delta_od_w_uq', 'delta_od_w_ukv', 'delta_od_w_out', 'delta_ffn_norm', 'delta_ffn_w_up', 'delta_ffn_conv_w', 'delta_ffn_conv_b', 'delta_ffn_w_down', 'delta_final_norm', 'new_m_meta_tokens', 'new_m_ev_norm', 'new_m_ev_w_in', 'new_m_ev_conv_a', 'new_m_ev_conv_b', 'new_m_ev_conv_b_bias', 'new_m_ev_gate_r_w', 'new_m_ev_gate_r_b', 'new_m_ev_gate_i_w', 'new_m_ev_gate_i_b', 'new_m_ev_lru_lambda', 'new_m_ev_w_out', 'new_m_od_norm', 'new_m_od_w_in', 'new_m_od_q_norm', 'new_m_od_kv_norm', 'new_m_od_w_uq', 'new_m_od_w_ukv', 'new_m_od_w_out', 'new_m_ffn_norm', 'new_m_ffn_w_up', 'new_m_ffn_conv_w', 'new_m_ffn_conv_b', 'new_m_ffn_w_down', 'new_m_final_norm', 'new_v_meta_tokens', 'new_v_ev_norm', 'new_v_ev_w_in', 'new_v_ev_conv_a', 'new_v_ev_conv_b', 'new_v_ev_conv_b_bias', 'new_v_ev_gate_r_w', 'new_v_ev_gate_r_b', 'new_v_ev_gate_i_w', 'new_v_ev_gate_i_b', 'new_v_ev_lru_lambda', 'new_v_ev_w_out', 'new_v_od_norm', 'new_v_od_w_in', 'new_v_od_q_norm', 'new_v_od_kv_norm', 'new_v_od_w_uq', 'new_v_od_w_ukv', 'new_v_od_w_out', 'new_v_ffn_norm', 'new_v_ffn_w_up', 'new_v_ffn_conv_w', 'new_v_ffn_conv_b', 'new_v_ffn_w_down', 'new_v_final_norm']
TWIN_LEAF_KINDS = {'loss': 'loss', 'grad_x': 'grad_x', 'grad_meta_tokens': 'grad_w', 'grad_ev_norm': 'grad_w', 'grad_ev_w_in': 'grad_w', 'grad_ev_conv_a': 'grad_w', 'grad_ev_conv_b': 'grad_w', 'grad_ev_conv_b_bias': 'grad_w', 'grad_ev_gate_r_w': 'grad_w', 'grad_ev_gate_r_b': 'grad_w', 'grad_ev_gate_i_w': 'grad_w', 'grad_ev_gate_i_b': 'grad_w', 'grad_ev_lru_lambda': 'grad_w', 'grad_ev_w_out': 'grad_w', 'grad_od_norm': 'grad_w', 'grad_od_w_in': 'grad_w', 'grad_od_q_norm': 'grad_w', 'grad_od_kv_norm': 'grad_w', 'grad_od_w_uq': 'grad_w', 'grad_od_w_ukv': 'grad_w', 'grad_od_w_out': 'grad_w', 'grad_ffn_norm': 'grad_w', 'grad_ffn_w_up': 'grad_w', 'grad_ffn_conv_w': 'grad_w', 'grad_ffn_conv_b': 'grad_w', 'grad_ffn_w_down': 'grad_w', 'grad_final_norm': 'grad_w', 'delta_meta_tokens': 'delta_w', 'delta_ev_norm': 'delta_w', 'delta_ev_w_in': 'delta_w', 'delta_ev_conv_a': 'delta_w', 'delta_ev_conv_b': 'delta_w', 'delta_ev_conv_b_bias': 'delta_w', 'delta_ev_gate_r_w': 'delta_w', 'delta_ev_gate_r_b': 'delta_w', 'delta_ev_gate_i_w': 'delta_w', 'delta_ev_gate_i_b': 'delta_w', 'delta_ev_lru_lambda': 'delta_w', 'delta_ev_w_out': 'delta_w', 'delta_od_norm': 'delta_w', 'delta_od_w_in': 'delta_w', 'delta_od_q_norm': 'delta_w', 'delta_od_kv_norm': 'delta_w', 'delta_od_w_uq': 'delta_w', 'delta_od_w_ukv': 'delta_w', 'delta_od_w_out': 'delta_w', 'delta_ffn_norm': 'delta_w', 'delta_ffn_w_up': 'delta_w', 'delta_ffn_conv_w': 'delta_w', 'delta_ffn_conv_b': 'delta_w', 'delta_ffn_w_down': 'delta_w', 'delta_final_norm': 'delta_w', 'new_m_meta_tokens': 'new_m', 'new_m_ev_norm': 'new_m', 'new_m_ev_w_in': 'new_m', 'new_m_ev_conv_a': 'new_m', 'new_m_ev_conv_b': 'new_m', 'new_m_ev_conv_b_bias': 'new_m', 'new_m_ev_gate_r_w': 'new_m', 'new_m_ev_gate_r_b': 'new_m', 'new_m_ev_gate_i_w': 'new_m', 'new_m_ev_gate_i_b': 'new_m', 'new_m_ev_lru_lambda': 'new_m', 'new_m_ev_w_out': 'new_m', 'new_m_od_norm': 'new_m', 'new_m_od_w_in': 'new_m', 'new_m_od_q_norm': 'new_m', 'new_m_od_kv_norm': 'new_m', 'new_m_od_w_uq': 'new_m', 'new_m_od_w_ukv': 'new_m', 'new_m_od_w_out': 'new_m', 'new_m_ffn_norm': 'new_m', 'new_m_ffn_w_up': 'new_m', 'new_m_ffn_conv_w': 'new_m', 'new_m_ffn_conv_b': 'new_m', 'new_m_ffn_w_down': 'new_m', 'new_m_final_norm': 'new_m', 'new_v_meta_tokens': 'new_v', 'new_v_ev_norm': 'new_v', 'new_v_ev_w_in': 'new_v', 'new_v_ev_conv_a': 'new_v', 'new_v_ev_conv_b': 'new_v', 'new_v_ev_conv_b_bias': 'new_v', 'new_v_ev_gate_r_w': 'new_v', 'new_v_ev_gate_r_b': 'new_v', 'new_v_ev_gate_i_w': 'new_v', 'new_v_ev_gate_i_b': 'new_v', 'new_v_ev_lru_lambda': 'new_v', 'new_v_ev_w_out': 'new_v', 'new_v_od_norm': 'new_v', 'new_v_od_w_in': 'new_v', 'new_v_od_q_norm': 'new_v', 'new_v_od_kv_norm': 'new_v', 'new_v_od_w_uq': 'new_v', 'new_v_od_w_ukv': 'new_v', 'new_v_od_w_out': 'new_v', 'new_v_ffn_norm': 'new_v', 'new_v_ffn_w_up': 'new_v', 'new_v_ffn_conv_w': 'new_v', 'new_v_ffn_conv_b': 'new_v', 'new_v_ffn_w_down': 'new_v', 'new_v_final_norm': 'new_v'}


def _forward(args):
    return _fwd_reference(*[args[k] for k in FWD_PARAMS])


def _output_shape():
    out = _jax.eval_shape(lambda: _forward(_fwd_setup_inputs(0)))
    return out.shape, out.dtype

N_MICROBATCH = 1
ADAM_LR = 0.001
ADAM_B1 = 0.9
ADAM_B2 = 0.999
ADAM_EPS = 1e-08
ADAM_WD = 0.01
ADAM_STEP = 10
PER_EXAMPLE_BATCH_AXIS = {'x': 0, 'loss_target': 0}
SHARED_INPUTS = []
_WEIGHT_DTYPES = {'meta_tokens': _jnp.float32, 'ev_norm': _jnp.float32, 'ev_w_in': _jnp.float32, 'ev_conv_a': _jnp.float32, 'ev_conv_b': _jnp.float32, 'ev_conv_b_bias': _jnp.float32, 'ev_gate_r_w': _jnp.float32, 'ev_gate_r_b': _jnp.float32, 'ev_gate_i_w': _jnp.float32, 'ev_gate_i_b': _jnp.float32, 'ev_lru_lambda': _jnp.float32, 'ev_w_out': _jnp.float32, 'od_norm': _jnp.float32, 'od_w_in': _jnp.float32, 'od_q_norm': _jnp.float32, 'od_kv_norm': _jnp.float32, 'od_w_uq': _jnp.float32, 'od_w_ukv': _jnp.float32, 'od_w_out': _jnp.float32, 'ffn_norm': _jnp.float32, 'ffn_w_up': _jnp.float32, 'ffn_conv_w': _jnp.float32, 'ffn_conv_b': _jnp.float32, 'ffn_w_down': _jnp.float32, 'final_norm': _jnp.float32}
MOMENT_SCALE = {'meta_tokens': 1.914530e-02, 'ev_norm': 3.325331e-01, 'ev_w_in': 2.033267e-01, 'ev_conv_a': 2.523596e-01, 'ev_conv_b': 1.364468e-01, 'ev_conv_b_bias': 1.925094e+00, 'ev_gate_r_w': 6.268623e-02, 'ev_gate_r_b': 3.797367e-02, 'ev_gate_i_w': 1.111682e-01, 'ev_gate_i_b': 3.916153e-02, 'ev_lru_lambda': 7.032467e-02, 'ev_w_out': 1.907023e-01, 'od_norm': 6.374564e-02, 'od_w_in': 7.689575e-02, 'od_q_norm': 5.538528e-02, 'od_kv_norm': 1.051871e-01, 'od_w_uq': 2.716988e-02, 'od_w_ukv': 3.653900e-02, 'od_w_out': 4.278910e-02, 'ffn_norm': 1.707361e-01, 'ffn_w_up': 7.239882e-02, 'ffn_conv_w': 7.315721e-02, 'ffn_conv_b': 7.314345e-02, 'ffn_w_down': 1.179027e-01, 'final_norm': 6.384703e+01}


def _to_microbatches(a, axis):
    t = _jnp.moveaxis(a, axis, 0)
    t = t.reshape((N_MICROBATCH, t.shape[0] // N_MICROBATCH) + t.shape[1:])
    return _jnp.moveaxis(t, 1, axis + 1)


def setup_inputs(seed: int = 0) -> dict:
    inp = _fwd_setup_inputs(seed)
    key = _jax.random.fold_in(_jax.random.key(seed), 7919)
    shape, _ = _output_shape()
    out = dict(inp)
    out["loss_target"] = _jax.random.normal(_jax.random.fold_in(key, 0), shape, _jnp.float32)
    for i, name in enumerate(TWIN_WEIGHTS):
        w = inp[name].astype(_jnp.float32)
        if MOMENT_SCALE is None:
            s = _jnp.sqrt(_jnp.mean(_jnp.square(w)) + 1e-30)
        else:
            s = MOMENT_SCALE[name]
        km, kv = _jax.random.split(_jax.random.fold_in(key, i + 1))
        out[name] = w
        out["m_" + name] = s * _jax.random.normal(km, w.shape, _jnp.float32)
        out["v_" + name] = (s * s) * _jax.random.uniform(kv, w.shape, _jnp.float32, 0.5, 1.5)
    if N_MICROBATCH > 1:
        for name, axis in PER_EXAMPLE_BATCH_AXIS.items():
            out[name] = _to_microbatches(out[name], axis)
    return {'x': out['x'], 'meta_tokens': out['meta_tokens'], 'ev_norm': out['ev_norm'], 'ev_w_in': out['ev_w_in'], 'ev_conv_a': out['ev_conv_a'], 'ev_conv_b': out['ev_conv_b'], 'ev_conv_b_bias': out['ev_conv_b_bias'], 'ev_gate_r_w': out['ev_gate_r_w'], 'ev_gate_r_b': out['ev_gate_r_b'], 'ev_gate_i_w': out['ev_gate_i_w'], 'ev_gate_i_b': out['ev_gate_i_b'], 'ev_lru_lambda': out['ev_lru_lambda'], 'ev_w_out': out['ev_w_out'], 'od_norm': out['od_norm'], 'od_w_in': out['od_w_in'], 'od_q_norm': out['od_q_norm'], 'od_kv_norm': out['od_kv_norm'], 'od_w_uq': out['od_w_uq'], 'od_w_ukv': out['od_w_ukv'], 'od_w_out': out['od_w_out'], 'ffn_norm': out['ffn_norm'], 'ffn_w_up': out['ffn_w_up'], 'ffn_conv_w': out['ffn_conv_w'], 'ffn_conv_b': out['ffn_conv_b'], 'ffn_w_down': out['ffn_w_down'], 'final_norm': out['final_norm'], 'loss_target': out['loss_target'], 'm_meta_tokens': out['m_meta_tokens'], 'm_ev_norm': out['m_ev_norm'], 'm_ev_w_in': out['m_ev_w_in'], 'm_ev_conv_a': out['m_ev_conv_a'], 'm_ev_conv_b': out['m_ev_conv_b'], 'm_ev_conv_b_bias': out['m_ev_conv_b_bias'], 'm_ev_gate_r_w': out['m_ev_gate_r_w'], 'm_ev_gate_r_b': out['m_ev_gate_r_b'], 'm_ev_gate_i_w': out['m_ev_gate_i_w'], 'm_ev_gate_i_b': out['m_ev_gate_i_b'], 'm_ev_lru_lambda': out['m_ev_lru_lambda'], 'm_ev_w_out': out['m_ev_w_out'], 'm_od_norm': out['m_od_norm'], 'm_od_w_in': out['m_od_w_in'], 'm_od_q_norm': out['m_od_q_norm'], 'm_od_kv_norm': out['m_od_kv_norm'], 'm_od_w_uq': out['m_od_w_uq'], 'm_od_w_ukv': out['m_od_w_ukv'], 'm_od_w_out': out['m_od_w_out'], 'm_ffn_norm': out['m_ffn_norm'], 'm_ffn_w_up': out['m_ffn_w_up'], 'm_ffn_conv_w': out['m_ffn_conv_w'], 'm_ffn_conv_b': out['m_ffn_conv_b'], 'm_ffn_w_down': out['m_ffn_w_down'], 'm_final_norm': out['m_final_norm'], 'v_meta_tokens': out['v_meta_tokens'], 'v_ev_norm': out['v_ev_norm'], 'v_ev_w_in': out['v_ev_w_in'], 'v_ev_conv_a': out['v_ev_conv_a'], 'v_ev_conv_b': out['v_ev_conv_b'], 'v_ev_conv_b_bias': out['v_ev_conv_b_bias'], 'v_ev_gate_r_w': out['v_ev_gate_r_w'], 'v_ev_gate_r_b': out['v_ev_gate_r_b'], 'v_ev_gate_i_w': out['v_ev_gate_i_w'], 'v_ev_gate_i_b': out['v_ev_gate_i_b'], 'v_ev_lru_lambda': out['v_ev_lru_lambda'], 'v_ev_w_out': out['v_ev_w_out'], 'v_od_norm': out['v_od_norm'], 'v_od_w_in': out['v_od_w_in'], 'v_od_q_norm': out['v_od_q_norm'], 'v_od_kv_norm': out['v_od_kv_norm'], 'v_od_w_uq': out['v_od_w_uq'], 'v_od_w_ukv': out['v_od_w_ukv'], 'v_od_w_out': out['v_od_w_out'], 'v_ffn_norm': out['v_ffn_norm'], 'v_ffn_w_up': out['v_ffn_w_up'], 'v_ffn_conv_w': out['v_ffn_conv_w'], 'v_ffn_conv_b': out['v_ffn_conv_b'], 'v_ffn_w_down': out['v_ffn_w_down'], 'v_final_norm': out['v_final_norm']}


def _loss(weights, diff, rest, loss_target):
    with _jax.named_scope("forward"):
        args = {**rest, TWIN_DIFF_INPUT: diff, **{k: w.astype(_WEIGHT_DTYPES[k]) for k, w in weights.items()}}
        y = _forward(args)
    with _jax.named_scope("loss_head"):
        err = _jnp.square(y.astype(_jnp.float32) - loss_target)
        return 0.5 * _jnp.sum(_jnp.mean(err, axis=-1)) if err.ndim else 0.5 * err


def _adamw(w, g, m, v):
    m = ADAM_B1 * m + (1.0 - ADAM_B1) * g
    v = ADAM_B2 * v + (1.0 - ADAM_B2) * _jnp.square(g)
    m_hat = m / (1.0 - ADAM_B1 ** ADAM_STEP)
    v_hat = v / (1.0 - ADAM_B2 ** ADAM_STEP)
    delta = -ADAM_LR * (m_hat / (_jnp.sqrt(v_hat) + ADAM_EPS) + ADAM_WD * w)
    return delta, m, v


def reference(x, meta_tokens, ev_norm, ev_w_in, ev_conv_a, ev_conv_b, ev_conv_b_bias, ev_gate_r_w, ev_gate_r_b, ev_gate_i_w, ev_gate_i_b, ev_lru_lambda, ev_w_out, od_norm, od_w_in, od_q_norm, od_kv_norm, od_w_uq, od_w_ukv, od_w_out, ffn_norm, ffn_w_up, ffn_conv_w, ffn_conv_b, ffn_w_down, final_norm, loss_target, m_meta_tokens, m_ev_norm, m_ev_w_in, m_ev_conv_a, m_ev_conv_b, m_ev_conv_b_bias, m_ev_gate_r_w, m_ev_gate_r_b, m_ev_gate_i_w, m_ev_gate_i_b, m_ev_lru_lambda, m_ev_w_out, m_od_norm, m_od_w_in, m_od_q_norm, m_od_kv_norm, m_od_w_uq, m_od_w_ukv, m_od_w_out, m_ffn_norm, m_ffn_w_up, m_ffn_conv_w, m_ffn_conv_b, m_ffn_w_down, m_final_norm, v_meta_tokens, v_ev_norm, v_ev_w_in, v_ev_conv_a, v_ev_conv_b, v_ev_conv_b_bias, v_ev_gate_r_w, v_ev_gate_r_b, v_ev_gate_i_w, v_ev_gate_i_b, v_ev_lru_lambda, v_ev_w_out, v_od_norm, v_od_w_in, v_od_q_norm, v_od_kv_norm, v_od_w_uq, v_od_w_ukv, v_od_w_out, v_ffn_norm, v_ffn_w_up, v_ffn_conv_w, v_ffn_conv_b, v_ffn_w_down, v_final_norm):
    given = dict(x=x, meta_tokens=meta_tokens, ev_norm=ev_norm, ev_w_in=ev_w_in, ev_conv_a=ev_conv_a, ev_conv_b=ev_conv_b, ev_conv_b_bias=ev_conv_b_bias, ev_gate_r_w=ev_gate_r_w, ev_gate_r_b=ev_gate_r_b, ev_gate_i_w=ev_gate_i_w, ev_gate_i_b=ev_gate_i_b, ev_lru_lambda=ev_lru_lambda, ev_w_out=ev_w_out, od_norm=od_norm, od_w_in=od_w_in, od_q_norm=od_q_norm, od_kv_norm=od_kv_norm, od_w_uq=od_w_uq, od_w_ukv=od_w_ukv, od_w_out=od_w_out, ffn_norm=ffn_norm, ffn_w_up=ffn_w_up, ffn_conv_w=ffn_conv_w, ffn_conv_b=ffn_conv_b, ffn_w_down=ffn_w_down, final_norm=final_norm, loss_target=loss_target, m_meta_tokens=m_meta_tokens, m_ev_norm=m_ev_norm, m_ev_w_in=m_ev_w_in, m_ev_conv_a=m_ev_conv_a, m_ev_conv_b=m_ev_conv_b, m_ev_conv_b_bias=m_ev_conv_b_bias, m_ev_gate_r_w=m_ev_gate_r_w, m_ev_gate_r_b=m_ev_gate_r_b, m_ev_gate_i_w=m_ev_gate_i_w, m_ev_gate_i_b=m_ev_gate_i_b, m_ev_lru_lambda=m_ev_lru_lambda, m_ev_w_out=m_ev_w_out, m_od_norm=m_od_norm, m_od_w_in=m_od_w_in, m_od_q_norm=m_od_q_norm, m_od_kv_norm=m_od_kv_norm, m_od_w_uq=m_od_w_uq, m_od_w_ukv=m_od_w_ukv, m_od_w_out=m_od_w_out, m_ffn_norm=m_ffn_norm, m_ffn_w_up=m_ffn_w_up, m_ffn_conv_w=m_ffn_conv_w, m_ffn_conv_b=m_ffn_conv_b, m_ffn_w_down=m_ffn_w_down, m_final_norm=m_final_norm, v_meta_tokens=v_meta_tokens, v_ev_norm=v_ev_norm, v_ev_w_in=v_ev_w_in, v_ev_conv_a=v_ev_conv_a, v_ev_conv_b=v_ev_conv_b, v_ev_conv_b_bias=v_ev_conv_b_bias, v_ev_gate_r_w=v_ev_gate_r_w, v_ev_gate_r_b=v_ev_gate_r_b, v_ev_gate_i_w=v_ev_gate_i_w, v_ev_gate_i_b=v_ev_gate_i_b, v_ev_lru_lambda=v_ev_lru_lambda, v_ev_w_out=v_ev_w_out, v_od_norm=v_od_norm, v_od_w_in=v_od_w_in, v_od_q_norm=v_od_q_norm, v_od_kv_norm=v_od_kv_norm, v_od_w_uq=v_od_w_uq, v_od_w_ukv=v_od_w_ukv, v_od_w_out=v_od_w_out, v_ffn_norm=v_ffn_norm, v_ffn_w_up=v_ffn_w_up, v_ffn_conv_w=v_ffn_conv_w, v_ffn_conv_b=v_ffn_conv_b, v_ffn_w_down=v_ffn_w_down, v_final_norm=v_final_norm)
    weights = {n: given[n] for n in TWIN_WEIGHTS}
    shared = {n: given[n] for n in SHARED_INPUTS}
    per_example = {n: given[n] for n in ['x']}
    grad_fn = _jax.value_and_grad(_loss, argnums=(0, 1))

    def one_microbatch(ex, loss_target):
        ex = dict(ex)
        diff = ex.pop(TWIN_DIFF_INPUT)
        return grad_fn(weights, diff, {**shared, **ex}, loss_target)

    if N_MICROBATCH == 1:
        loss, (grad_w, grad_x) = one_microbatch(per_example, given["loss_target"])
    else:
        def body(carry, xs):
            loss_sum, grad_sum = carry
            l_k, (gw_k, gx_k) = one_microbatch(xs[0], xs[1])
            with _jax.named_scope("update"):
                return (loss_sum + l_k, _jax.tree.map(_jnp.add, grad_sum, gw_k)), gx_k

        init = (_jnp.zeros((), _jnp.float32), _jax.tree.map(_jnp.zeros_like, weights))
        (loss, grad_w), grad_x = _jax.lax.scan(body, init, (per_example, given["loss_target"]))
    with _jax.named_scope("update"):
        delta_w, new_m, new_v = {}, {}, {}
        for n in TWIN_WEIGHTS:
            delta_w[n], new_m[n], new_v[n] = _adamw(weights[n], grad_w[n], given["m_" + n], given["v_" + n])
    return (loss, grad_x, *[grad_w[n] for n in TWIN_WEIGHTS], *[delta_w[n] for n in TWIN_WEIGHTS],
            *[new_m[n] for n in TWIN_WEIGHTS], *[new_v[n] for n in TWIN_WEIGHTS])
```

```python
import math

import jax
import jax.numpy as jnp
from jax import lax
from jax.experimental import pallas as pl
from jax.experimental.pallas import tpu as pltpu

F32 = jnp.float32
MXU_DT = jnp.bfloat16
S = jax.ShapeDtypeStruct
MESH = pl.DeviceIdType.MESH

EPS = 1e-6
D_MODEL = 1024
N_META = 16
DEPTH = 4
CONV_W = 512
LRU_W = 512
LRU_HEADS = 8
LRU_C = 8.0
EVEN_IN = 2560
MLA_HEADS = 16
QK_NOPE = 64
QK_ROPE = 32
QK_HEAD = 96
V_HEAD = 64
Q_LORA = 384
KV_LORA = 256
ROPE_BASE = 10000.0
D_FF = 2816
ODD_PAD = 896
ODD_CKV_COL = 2
ODD_KR_COL = 6
HP = 128
ATT_BLK = 384
FFN_CT = 256
LANE = 128
SUBLANE = 8
VMEM_LIMIT_MB = 52

ADAM_LR = 0.001
ADAM_B1 = 0.9
ADAM_B2 = 0.999
ADAM_EPS = 1e-08
ADAM_WD = 0.01
ADAM_STEP = 10

NT_DIMS = (((1,), (1,)), ((), ()))
TN_DIMS = (((0,), (0,)), ((), ()))


def _cp(sem):
    return pltpu.CompilerParams(dimension_semantics=sem, vmem_limit_bytes=VMEM_LIMIT_MB << 20)


def _div_tile(n, cap, mult):
    if n <= cap:
        return n
    best = None
    for t in range(mult, cap + 1, mult):
        if n % t == 0:
            best = t
    assert best is not None, (n, cap, mult)
    return best


def _round_up(n, m):
    return -(-n // m) * m


def mat_cols(arr):
    return arr.shape[1] if arr.ndim == 2 else arr.shape[0] * arr.shape[2]


def mat_width(arr):
    return arr.shape[-1]


def mat_spec(arr, tm, tw, rc):
    if arr.ndim == 2:
        return pl.BlockSpec((tm, tw), lambda *g: rc(*g))
    per = arr.shape[2] // tw
    assert arr.shape[2] % tw == 0

    def imap(*g):
        r, c = rc(*g)
        return (c // per, r, c % per)

    return pl.BlockSpec((None, tm, tw), imap)


def norm_matmul(x, xcol, kdim, gain, w, tm, tn, out_dtype, name, epi=None, epi_ops=(), epi_specs=()):
    rows, n = x.shape[0], w.shape[1]
    n_epi = len(epi_ops)

    def body(x_ref, g_ref, w_ref, *rest):
        epi_refs = rest[:n_epi]
        out_ref, xn_ref, xn_sc = rest[n_epi:]

        @pl.when(pl.program_id(1) == 0)
        def _():
            xv = x_ref[...]
            y = xv * lax.rsqrt(jnp.mean(xv * xv, axis=-1, keepdims=True) + EPS)
            xn = (y * g_ref[...]).astype(MXU_DT)
            xn_sc[...] = xn
            xn_ref[...] = xn

        acc = jnp.dot(xn_sc[...], w_ref[...], preferred_element_type=F32)
        if epi is not None:
            acc = epi(acc, *[r[...] for r in epi_refs])
        out_ref[...] = acc.astype(out_dtype)

    return pl.pallas_call(
        body, grid=(rows // tm, n // tn),
        in_specs=[pl.BlockSpec((tm, kdim), lambda i, j: (i, xcol)), pl.BlockSpec((1, kdim), lambda i, j: (0, 0)),
                  pl.BlockSpec((kdim, tn), lambda i, j: (0, j)), *epi_specs],
        out_specs=[pl.BlockSpec((tm, tn), lambda i, j: (i, j)), pl.BlockSpec((tm, kdim), lambda i, j: (i, 0))],
        out_shape=[S((rows, n), out_dtype), S((rows, kdim), MXU_DT)],
        scratch_shapes=[pltpu.VMEM((tm, kdim), MXU_DT)],
        compiler_params=_cp(("parallel", "arbitrary")), name=name)(x, gain, w, *epi_ops)


def matmul_res(a, w, res, tm, tn, name):
    grp, rows, k = a.shape
    n = w.shape[2]

    def body(a_ref, w_ref, r_ref, o_ref):
        acc = r_ref[...]
        for g in range(grp):
            acc = acc + jnp.dot(a_ref[g], w_ref[g], preferred_element_type=F32)
        o_ref[...] = acc

    return pl.pallas_call(
        body, grid=(rows // tm, n // tn),
        in_specs=[pl.BlockSpec((grp, tm, k), lambda i, j: (0, i, 0)), pl.BlockSpec((grp, k, tn), lambda i, j: (0, 0, j)),
                  pl.BlockSpec((tm, tn), lambda i, j: (i, j))],
        out_specs=pl.BlockSpec((tm, tn), lambda i, j: (i, j)),
        out_shape=S((rows, n), F32), compiler_params=_cp(("parallel", "parallel")), name=name)(a, w, res)


def matmul_nt(a, w, tm, tn, out_dtype, name):
    rows, k = a.shape
    n = w.shape[0]

    def body(a_ref, w_ref, o_ref):
        o_ref[...] = lax.dot_general(a_ref[...].astype(MXU_DT), w_ref[...], NT_DIMS,
                                     preferred_element_type=F32).astype(out_dtype)

    return pl.pallas_call(
        body, grid=(rows // tm, n // tn),
        in_specs=[pl.BlockSpec((tm, k), lambda i, j: (i, 0)), pl.BlockSpec((tn, k), lambda i, j: (j, 0))],
        out_specs=pl.BlockSpec((tm, tn), lambda i, j: (i, j)),
        out_shape=S((rows, n), out_dtype), compiler_params=_cp(("parallel", "parallel")), name=name)(a, w)


def matmul_nt_normbwd(du, w, x, xcol, gain, res, tm, tk, out_dtype, name):
    rows, kc = du.shape[-2], mat_cols(du)
    dn = w.shape[0]
    nk = kc // tk
    has_res = res is not None

    def body(du_ref, w_ref, x_ref, g_ref, *rest):
        if has_res:
            res_ref, dx_ref, dg_ref, acc = rest
        else:
            dx_ref, dg_ref, acc = rest
        i, k = pl.program_id(0), pl.program_id(1)

        @pl.when(k == 0)
        def _():
            acc[...] = jnp.zeros_like(acc)

        @pl.when((i == 0) & (k == 0))
        def _():
            dg_ref[...] = jnp.zeros_like(dg_ref)

        acc[...] += lax.dot_general(du_ref[...], w_ref[...], NT_DIMS, preferred_element_type=F32)

        @pl.when(k == nk - 1)
        def _():
            dhn = acc[...]
            xv = x_ref[...]
            rstd = lax.rsqrt(jnp.mean(xv * xv, axis=-1, keepdims=True) + EPS)
            xhat = xv * rstd
            dg_ref[...] += jnp.sum(dhn * xhat, axis=0, keepdims=True)
            dxh = dhn * g_ref[...]
            dx = rstd * (dxh - xhat * jnp.mean(dxh * xhat, axis=-1, keepdims=True))
            if has_res:
                dx = dx + res_ref[...]
            dx_ref[...] = dx.astype(out_dtype)

    in_specs = [mat_spec(du, tm, tk, lambda i, k: (i, k)), pl.BlockSpec((dn, tk), lambda i, k: (0, k)),
                pl.BlockSpec((tm, dn), lambda i, k: (i, xcol)), pl.BlockSpec((1, dn), lambda i, k: (0, 0))]
    ops = [du, w, x, gain]
    if has_res:
        in_specs.append(pl.BlockSpec((tm, dn), lambda i, k: (i, 0)))
        ops.append(res)
    return pl.pallas_call(
        body, grid=(rows // tm, nk), in_specs=in_specs,
        out_specs=[pl.BlockSpec((tm, dn), lambda i, k: (i, 0)), pl.BlockSpec((1, dn), lambda i, k: (0, 0))],
        out_shape=[S((rows, dn), out_dtype), S((1, dn), F32)],
        scratch_shapes=[pltpu.VMEM((tm, dn), F32)],
        compiler_params=_cp(("arbitrary", "arbitrary")), name=name)(*ops)


def matmul_tn(a, b, tr, name):
    rows, ka, nb = a.shape[-2], mat_cols(a), mat_cols(b)
    ta = _div_tile(mat_width(a), 1536, LANE)
    tb = _div_tile(mat_width(b), 1536 if ta <= 1024 else 1024, LANE)
    nr = rows // tr

    def body(a_ref, b_ref, o_ref, acc):
        r = pl.program_id(2)

        @pl.when(r == 0)
        def _():
            acc[...] = jnp.zeros_like(acc)

        acc[...] += lax.dot_general(a_ref[...].astype(MXU_DT), b_ref[...].astype(MXU_DT), TN_DIMS,
                                    preferred_element_type=F32)

        @pl.when(r == nr - 1)
        def _():
            o_ref[...] = acc[...]

    return pl.pallas_call(
        body, grid=(ka // ta, nb // tb, nr),
        in_specs=[mat_spec(a, tr, ta, lambda i, j, r: (r, i)), mat_spec(b, tr, tb, lambda i, j, r: (r, j))],
        out_specs=pl.BlockSpec((ta, tb), lambda i, j, r: (i, j)),
        out_shape=S((ka, nb), F32), scratch_shapes=[pltpu.VMEM((ta, tb), F32)],
        compiler_params=_cp(("parallel", "parallel", "arbitrary")), name=name)(a, b)


def _sigmoid(x):
    return 1.0 / (1.0 + jnp.exp(-x))


def _log1p(e):
    return jnp.where(e < 1e-3, e * (1.0 - e * (0.5 - e * (1.0 / 3.0 - 0.25 * e))), jnp.log(1.0 + e))


def _softplus(x):
    return jnp.maximum(x, 0.0) + _log1p(jnp.exp(-jnp.abs(x)))


def _expm1(x):
    series = x * (1.0 + x * (0.5 + x * (1.0 / 6.0 + x * (1.0 / 24.0 + x * (1.0 / 120.0)))))
    return jnp.where(jnp.abs(x) < 0.1, series, jnp.exp(x) - 1.0)


_GELU_K = math.sqrt(2.0 / math.pi)
_GELU_C = 0.044715


def _gelu_and_grad(x):
    th = jnp.tanh(_GELU_K * (x + _GELU_C * x * x * x))
    g = 0.5 * x * (1.0 + th)
    dg = 0.5 * (1.0 + th) + 0.5 * x * (1.0 - th * th) * _GELU_K * (1.0 + 3.0 * _GELU_C * x * x)
    return g, dg


def _row_iota(shape):
    return lax.broadcasted_iota(jnp.int32, shape, 0)


def _scan_chunk_fwd(a_sc, u_sc, out_ref, hcar, n, width):
    rowi = _row_iota((SUBLANE, width))

    def step(c, hprev):
        r0 = pl.multiple_of(c * SUBLANE, SUBLANE)
        a = a_sc[pl.ds(r0, SUBLANE), :]
        u = u_sc[pl.ds(r0, SUBLANE), :]
        for d in (1, 2, 4):
            a_s = jnp.where(rowi >= d, pltpu.roll(a, d, axis=0), 1.0)
            u_s = jnp.where(rowi >= d, pltpu.roll(u, d, axis=0), 0.0)
            u = u + a * u_s
            a = a * a_s
        h = u + a * hprev
        out_ref[pl.ds(r0, SUBLANE), :] = h
        return jnp.broadcast_to(h[SUBLANE - 1:SUBLANE, :], (SUBLANE, width))

    hcar[...] = lax.fori_loop(0, n // SUBLANE, step, hcar[...], unroll=4)


def _scan_chunk_bwd(b_sc, d_sc, out_ref, gcar, n, width):
    rowi = _row_iota((SUBLANE, width))
    nc = n // SUBLANE

    def step(c, gnext):
        r0 = pl.multiple_of((nc - 1 - c) * SUBLANE, SUBLANE)
        b = b_sc[pl.ds(r0, SUBLANE), :]
        d = d_sc[pl.ds(r0, SUBLANE), :]
        for s in (1, 2, 4):
            keep = rowi < SUBLANE - s
            b_s = jnp.where(keep, pltpu.roll(b, SUBLANE - s, axis=0), 1.0)
            d_s = jnp.where(keep, pltpu.roll(d, SUBLANE - s, axis=0), 0.0)
            d = d + b * d_s
            b = b * b_s
        g = d + b * gnext
        out_ref[pl.ds(r0, SUBLANE), :] = g
        return jnp.broadcast_to(g[0:1, :], (SUBLANE, width))

    gcar[...] = lax.fori_loop(0, nc, step, gcar[...], unroll=4)


def even_mid_fwd(u, conv_a, conv_b, conv_b_bias, rw, rb, iw, ib, lam, nb, tp, n, name):
    rows = u.shape[0]
    w = LANE
    nj = CONV_W // w
    nt = tp // n
    h8 = SUBLANE

    def body(gb_r, gc_r, xa_r, xb_r, gate_r, ca_w, cb_w, cb_b, rw_r, rb_r, iw_r, ib_r, lam_r,
             y_o, ca_o, xc_o, a_o, hs_o, pext, xext, hcar, a_sc, u_sc):
        t = pl.program_id(2)

        @pl.when(t == 0)
        def _():
            pext[0:h8, :] = jnp.zeros((h8, w), F32)
            xext[0:h8, :] = jnp.zeros((h8, w), F32)
            hcar[...] = jnp.zeros_like(hcar)

        p = gc_r[...] * xa_r[...]
        pext[h8:h8 + n, :] = p
        wa = ca_w[...]
        ca = wa[2:3, :] * p + wa[1:2, :] * pext[h8 - 1:h8 - 1 + n, :] + wa[0:1, :] * pext[h8 - 2:h8 - 2 + n, :]
        ca_o[...] = ca
        y_o[0] = (gb_r[...] * ca).astype(MXU_DT)
        pext[0:h8, :] = pext[n:n + h8, :]

        xb = xb_r[...]
        xext[h8:h8 + n, :] = xb
        wb = cb_w[...]
        xc = (wb[3:4, :] * xb + wb[2:3, :] * xext[h8 - 1:h8 - 1 + n, :] + wb[1:2, :] * xext[h8 - 2:h8 - 2 + n, :]
              + wb[0:1, :] * xext[h8 - 3:h8 - 3 + n, :]) + cb_b[...]
        xc_o[...] = xc
        xext[0:h8, :] = xext[n:n + h8, :]

        xcm = xc.astype(MXU_DT)
        r = _sigmoid(jnp.dot(xcm, rw_r[...], preferred_element_type=F32) + rb_r[...])
        ig = _sigmoid(jnp.dot(xcm, iw_r[...], preferred_element_type=F32) + ib_r[...])
        log_a = (-LRU_C) * r * _softplus(-lam_r[...])
        a = jnp.exp(log_a)
        mult = jnp.sqrt(-_expm1(2.0 * log_a))
        a_sc[...] = a
        a_o[...] = a
        u_sc[...] = mult * (ig * xc)
        _scan_chunk_fwd(a_sc, u_sc, hs_o, hcar, n, w)
        gel, _ = _gelu_and_grad(gate_r[...])
        y_o[1] = (gel * hs_o[...]).astype(MXU_DT)

    def ublk(off):
        return pl.BlockSpec((n, w), lambda j, b, t: (b * nt + t, off + j))

    def pblk(r_):
        return pl.BlockSpec((r_, w), lambda j, b, t: (0, j))

    act = pl.BlockSpec((n, w), lambda j, b, t: (b * nt + t, j))
    mat = pl.BlockSpec((w, w), lambda j, b, t: (j, j))
    return pl.pallas_call(
        body, grid=(nj, nb, nt),
        in_specs=[ublk(0), ublk(nj), ublk(2 * nj), ublk(3 * nj), ublk(4 * nj), pblk(3), pblk(4), pblk(1),
                  mat, pblk(1), mat, pblk(1), pblk(1)],
        out_specs=[pl.BlockSpec((2, n, w), lambda j, b, t: (0, b * nt + t, j)), act, act, act, act],
        out_shape=[S((2, rows, CONV_W), MXU_DT), S((rows, CONV_W), F32), S((rows, LRU_W), F32), S((rows, LRU_W), F32),
                   S((rows, LRU_W), F32)],
        scratch_shapes=[pltpu.VMEM((n + h8, w), F32), pltpu.VMEM((n + h8, w), F32), pltpu.VMEM((h8, w), F32),
                        pltpu.VMEM((n, w), F32), pltpu.VMEM((n, w), F32)],
        compiler_params=_cp(("parallel", "parallel", "arbitrary")), name=name,
    )(u, u, u, u, u, conv_a, conv_b, conv_b_bias, rw, rb, iw, ib, lam)


def even_mid_bwd(u, dycat, ca, xc, a_sv, hs, conv_a, conv_b, rw, rb, iw, ib, lam, nb, tp, n, name):
    rows = u.shape[0]
    w = LANE
    nj = CONV_W // w
    nt = tp // n
    h8 = SUBLANE

    def body(gb_r, gc_r, xa_r, xb_r, gate_r, dya_r, dyb_r, ca_r, xc_r, a_r, hs_r, hsp_r,
             ca_w, cb_w, rw_r, rb_r, iw_r, ib_r, lam_r,
             du_o, dca_w, dcb_w, dcb_b, drw, drb, diw, dib, dlam,
             aext, hext, dext, eext, gcar, b_sc, d_sc, g_sc):
        b, t = pl.program_id(1), pl.program_id(2)

        @pl.when((b == 0) & (t == 0))
        def _():
            for ref in (dca_w, dcb_w, dcb_b, drw, drb, diw, dib, dlam):
                ref[...] = jnp.zeros_like(ref)

        @pl.when(t == 0)
        def _():
            aext[n:n + h8, :] = jnp.zeros((h8, w), F32)
            dext[n:n + h8, :] = jnp.zeros((h8, w), F32)
            eext[n:n + h8, :] = jnp.zeros((h8, w), F32)
            gcar[...] = jnp.zeros_like(gcar)

        xc_v = xc_r[...]
        xcm = xc_v.astype(MXU_DT)
        r = _sigmoid(jnp.dot(xcm, rw_r[...], preferred_element_type=F32) + rb_r[...])
        ig = _sigmoid(jnp.dot(xcm, iw_r[...], preferred_element_type=F32) + ib_r[...])
        lam_v = lam_r[...]
        sp = _softplus(-lam_v)
        log_a = (-LRU_C) * r * sp
        a = a_r[...]
        mult = jnp.sqrt(-_expm1(2.0 * log_a))
        hs_v = hs_r[...]
        gel, dgel = _gelu_and_grad(gate_r[...])
        dyb = dyb_r[...]
        du_o[4] = (dyb * hs_v * dgel).astype(MXU_DT)

        aext[0:n, :] = a
        b_sc[...] = aext[1:1 + n, :]
        d_sc[...] = dyb * gel
        _scan_chunk_bwd(b_sc, d_sc, g_sc, gcar, n, w)
        aext[n:n + h8, :] = aext[0:h8, :]
        g = g_sc[...]

        hext[0:h8, :] = jnp.where(t == nt - 1, 0.0, hsp_r[...])
        hext[h8:h8 + n, :] = hs_v
        da = g * hext[h8 - 1:h8 - 1 + n, :]
        dmult = g * (ig * xc_v)
        di = g * mult * xc_v
        dxc = g * mult * ig
        dlog_a = da * a - dmult * (a * a) / mult
        dr = dlog_a * ((-LRU_C) * sp)
        dsp = jnp.sum(dlog_a * ((-LRU_C) * r), axis=0, keepdims=True)
        dlam[...] += dsp * (-_sigmoid(-lam_v))
        dzr = dr * r * (1.0 - r)
        dzi = di * ig * (1.0 - ig)
        dzr_m = dzr.astype(MXU_DT)
        dzi_m = dzi.astype(MXU_DT)
        dxc = (dxc + lax.dot_general(dzr_m, rw_r[...], NT_DIMS, preferred_element_type=F32)
               + lax.dot_general(dzi_m, iw_r[...], NT_DIMS, preferred_element_type=F32))
        drw[...] += lax.dot_general(xcm, dzr_m, TN_DIMS, preferred_element_type=F32)
        diw[...] += lax.dot_general(xcm, dzi_m, TN_DIMS, preferred_element_type=F32)
        drb[...] += jnp.sum(dzr, axis=0, keepdims=True)
        dib[...] += jnp.sum(dzi, axis=0, keepdims=True)
        dcb_b[...] += jnp.sum(dxc, axis=0, keepdims=True)

        xb = xb_r[...]
        dext[0:n, :] = dxc
        wb = cb_w[...]
        d1, d2, d3 = dext[1:1 + n, :], dext[2:2 + n, :], dext[3:3 + n, :]
        du_o[3] = (wb[3:4, :] * dxc + wb[2:3, :] * d1 + wb[1:2, :] * d2 + wb[0:1, :] * d3).astype(MXU_DT)
        dcb_w[3:4, :] += jnp.sum(xb * dxc, axis=0, keepdims=True)
        dcb_w[2:3, :] += jnp.sum(xb * d1, axis=0, keepdims=True)
        dcb_w[1:2, :] += jnp.sum(xb * d2, axis=0, keepdims=True)
        dcb_w[0:1, :] += jnp.sum(xb * d3, axis=0, keepdims=True)
        dext[n:n + h8, :] = dext[0:h8, :]

        gb, gc, xa = gb_r[...], gc_r[...], xa_r[...]
        dya = dya_r[...]
        du_o[0] = (dya * ca_r[...]).astype(MXU_DT)
        dca = dya * gb
        eext[0:n, :] = dca
        wa = ca_w[...]
        e1, e2 = eext[1:1 + n, :], eext[2:2 + n, :]
        dp = wa[2:3, :] * dca + wa[1:2, :] * e1 + wa[0:1, :] * e2
        p = gc * xa
        dca_w[2:3, :] += jnp.sum(p * dca, axis=0, keepdims=True)
        dca_w[1:2, :] += jnp.sum(p * e1, axis=0, keepdims=True)
        dca_w[0:1, :] += jnp.sum(p * e2, axis=0, keepdims=True)
        eext[n:n + h8, :] = eext[0:h8, :]
        du_o[1] = (dp * xa).astype(MXU_DT)
        du_o[2] = (dp * gc).astype(MXU_DT)

    def rt(b, t):
        return b * nt + (nt - 1 - t)

    def ublk(off):
        return pl.BlockSpec((n, w), lambda j, b, t: (rt(b, t), off + j))

    def pblk(r_):
        return pl.BlockSpec((r_, w), lambda j, b, t: (0, j))

    act = pl.BlockSpec((n, w), lambda j, b, t: (rt(b, t), j))
    n8 = n // h8
    hsp = pl.BlockSpec((h8, w), lambda j, b, t: (jnp.maximum(rt(b, t) * n8 - 1, 0), j))
    mat = pl.BlockSpec((w, w), lambda j, b, t: (j, j))
    return pl.pallas_call(
        body, grid=(nj, nb, nt),
        in_specs=[ublk(0), ublk(nj), ublk(2 * nj), ublk(3 * nj), ublk(4 * nj), ublk(0), ublk(nj), act, act, act, act,
                  hsp, pblk(3), pblk(4), mat, pblk(1), mat, pblk(1), pblk(1)],
        out_specs=[pl.BlockSpec((5, n, w), lambda j, b, t: (0, rt(b, t), j)), pblk(3), pblk(4), pblk(1),
                   mat, pblk(1), mat, pblk(1), pblk(1)],
        out_shape=[S((5, rows, CONV_W), MXU_DT), S((3, CONV_W), F32), S((4, LRU_W), F32), S((1, LRU_W), F32),
                   S((LRU_W, LRU_W), F32), S((1, LRU_W), F32), S((LRU_W, LRU_W), F32), S((1, LRU_W), F32),
                   S((1, LRU_W), F32)],
        scratch_shapes=[pltpu.VMEM((n + h8, w), F32)] * 4 + [pltpu.VMEM((h8, w), F32)] + [pltpu.VMEM((n, w), F32)] * 3,
        compiler_params=_cp(("arbitrary", "arbitrary", "arbitrary")), name=name,
    )(u, u, u, u, u, dycat, dycat, ca, xc, a_sv, hs, hs, conv_a, conv_b, rw, rb, iw, ib, lam)


def ffn_mid_fwd(up, cw, cb, nb, tp, n, name):
    rows = up.shape[0]
    w = FFN_CT
    nj = D_FF // w
    nt = tp // n
    h8 = SUBLANE

    def body(xa_r, xg_r, w_r, b_r, u_o, y_o, aext, gext):
        t = pl.program_id(2)

        @pl.when(t == 0)
        def _():
            aext[0:h8, :] = jnp.zeros((h8, w), F32)
            gext[0:h8, :] = jnp.zeros((h8, w), F32)

        def conv(x_r, ext, g):
            x = x_r[...]
            ext[h8:h8 + n, :] = x
            wv = w_r[g]
            y = (wv[2:3, :] * x + wv[1:2, :] * ext[h8 - 1:h8 - 1 + n, :] + wv[0:1, :] * ext[h8 - 2:h8 - 2 + n, :]) + b_r[g]
            ext[0:h8, :] = ext[n:n + h8, :]
            return y

        ua = conv(xa_r, aext, 0)
        ug = conv(xg_r, gext, 1)
        u_o[0] = ua
        u_o[1] = ug
        y_o[...] = (ua * _sigmoid(ua) * ug).astype(MXU_DT)

    def ublk(off):
        return pl.BlockSpec((n, w), lambda j, b, t: (b * nt + t, off + j))

    return pl.pallas_call(
        body, grid=(nj, nb, nt),
        in_specs=[ublk(0), ublk(nj), pl.BlockSpec((2, 3, w), lambda j, b, t: (0, 0, j)),
                  pl.BlockSpec((2, 1, w), lambda j, b, t: (0, 0, j))],
        out_specs=[pl.BlockSpec((2, n, w), lambda j, b, t: (0, b * nt + t, j)), ublk(0)],
        out_shape=[S((2, rows, D_FF), F32), S((rows, D_FF), MXU_DT)],
        scratch_shapes=[pltpu.VMEM((n + h8, w), F32)] * 2,
        compiler_params=_cp(("parallel", "parallel", "arbitrary")), name=name,
    )(up, up, cw, cb)


def ffn_mid_bwd(dy, u, up, cw, nb, tp, n, name):
    rows = up.shape[0]
    w = FFN_CT
    nj = D_FF // w
    nt = tp // n
    h8 = SUBLANE

    def body(dy_r, u_r, xa_r, xg_r, w_r, dx_o, dw, db, aext, gext):
        b, t = pl.program_id(1), pl.program_id(2)

        @pl.when((b == 0) & (t == 0))
        def _():
            dw[...] = jnp.zeros_like(dw)
            db[...] = jnp.zeros_like(db)

        @pl.when(t == 0)
        def _():
            aext[n:n + h8, :] = jnp.zeros((h8, w), F32)
            gext[n:n + h8, :] = jnp.zeros((h8, w), F32)

        dyv = dy_r[...]
        ua, ug = u_r[0], u_r[1]
        sg = _sigmoid(ua)
        dua = dyv * ug * (sg * (1.0 + ua * (1.0 - sg)))
        dug = dyv * (ua * sg)

        def conv_t(du, ext, x_r, g):
            db[g] += jnp.sum(du, axis=0, keepdims=True)
            ext[0:n, :] = du
            wv = w_r[g]
            x = x_r[...]
            d1, d2 = ext[1:1 + n, :], ext[2:2 + n, :]
            dx_o[g] = (wv[2:3, :] * du + wv[1:2, :] * d1 + wv[0:1, :] * d2).astype(MXU_DT)
            dw[g, 2:3, :] += jnp.sum(x * du, axis=0, keepdims=True)
            dw[g, 1:2, :] += jnp.sum(x * d1, axis=0, keepdims=True)
            dw[g, 0:1, :] += jnp.sum(x * d2, axis=0, keepdims=True)
            ext[n:n + h8, :] = ext[0:h8, :]

        conv_t(dua, aext, xa_r, 0)
        conv_t(dug, gext, xg_r, 1)

    def rt(b, t):
        return b * nt + (nt - 1 - t)

    def ublk(off):
        return pl.BlockSpec((n, w), lambda j, b, t: (rt(b, t), off + j))

    pair = pl.BlockSpec((2, n, w), lambda j, b, t: (0, rt(b, t), j))
    return pl.pallas_call(
        body, grid=(nj, nb, nt),
        in_specs=[ublk(0), pair, ublk(0), ublk(nj), pl.BlockSpec((2, 3, w), lambda j, b, t: (0, 0, j))],
        out_specs=[pair, pl.BlockSpec((2, 3, w), lambda j, b, t: (0, 0, j)),
                   pl.BlockSpec((2, 1, w), lambda j, b, t: (0, 0, j))],
        out_shape=[S((2, rows, D_FF), MXU_DT), S((2, 3, D_FF), F32), S((2, 1, D_FF), F32)],
        scratch_shapes=[pltpu.VMEM((n + h8, w), F32)] * 2,
        compiler_params=_cp(("arbitrary", "arbitrary", "arbitrary")), name=name,
    )(dy, u, up, up, cw)


def _lane_mod(shape):
    return lax.broadcasted_iota(jnp.int32, shape, 1) & (HP - 1)


def _q_rope_epi(acc, tab):
    reps = acc.shape[1] // HP
    a = acc * jnp.tile(tab, (1, reps))
    lane = _lane_mod(a.shape)
    shifted = pltpu.roll(a, a.shape[1] - QK_ROPE, axis=1)
    return jnp.where(lane < QK_NOPE, a, jnp.where(lane < QK_HEAD, a + shifted, 0.0))


def _k_rope_block(krblk, tabk):
    a = krblk * tabk
    lane = _lane_mod(a.shape)
    b = a + pltpu.roll(a, HP - QK_ROPE, axis=1)
    return jnp.where((lane >= QK_NOPE) & (lane < QK_HEAD), b, 0.0)


def _k_rope_epi(acc, krblk, tabk):
    reps = acc.shape[1] // HP
    return acc + jnp.tile(_k_rope_block(krblk, tabk), (1, reps))


def attn_fwd(q, k, v, nb, tp, name):
    rows = q.shape[0]
    blk = ATT_BLK
    nq = tp // blk
    npair = MLA_HEADS // 2
    scale = QK_HEAD ** -0.5

    def body(q_r, k_r, v_r, o_r, lse_r):
        qi = pl.program_id(2)
        even = lax.broadcasted_iota(jnp.int32, (blk, LANE), 1) < V_HEAD
        rowi = lax.broadcasted_iota(jnp.int32, (blk, blk), 0)
        coli = lax.broadcasted_iota(jnp.int32, (blk, blk), 1)
        qs = [q_r[:, h * HP:(h + 1) * HP] for h in range(2)]

        def kv_step(kb, carry):
            ms, ls, acc = carry
            k0 = pl.multiple_of(kb * blk, blk)
            vblk = v_r[pl.ds(k0, blk), :]
            new_ms, new_ls, alphas, pv = [], [], [], []
            for h in range(2):
                kh = k_r[pl.ds(k0, blk), h * HP:(h + 1) * HP]
                s = lax.dot_general(qs[h], kh, NT_DIMS, preferred_element_type=F32) * scale
                s = jnp.where((kb < qi) | (coli <= rowi), s, -jnp.inf)
                m_new = jnp.maximum(ms[h], jnp.max(s, axis=1, keepdims=True))
                alpha = jnp.exp(ms[h] - m_new)
                p = jnp.exp(s - m_new)
                new_ls.append(alpha * ls[h] + jnp.sum(p, axis=1, keepdims=True))
                new_ms.append(m_new)
                alphas.append(alpha)
                vh = jnp.where(even if h == 0 else ~even, vblk, jnp.zeros_like(vblk))
                pv.append(jnp.dot(p.astype(MXU_DT), vh, preferred_element_type=F32))
            acc = acc * jnp.where(even, alphas[0], alphas[1]) + pv[0] + pv[1]
            return tuple(new_ms), tuple(new_ls), acc

        neg = jnp.full((blk, 1), -jnp.inf, F32)
        zero = jnp.zeros((blk, 1), F32)
        ms, ls, acc = lax.fori_loop(0, qi + 1, kv_step, ((neg, neg), (zero, zero), jnp.zeros((blk, LANE), F32)))
        o_r[...] = (acc / jnp.where(even, ls[0], ls[1])).astype(MXU_DT)
        lse_r[...] = jnp.where(even, ms[0] + jnp.log(ls[0]), ms[1] + jnp.log(ls[1]))

    return pl.pallas_call(
        body, grid=(nb, npair, nq),
        in_specs=[pl.BlockSpec((blk, 2 * HP), lambda b, p, i: (b * nq + i, p)),
                  pl.BlockSpec((tp, 2 * HP), lambda b, p, i: (b, p)),
                  pl.BlockSpec((tp, LANE), lambda b, p, i: (b, p))],
        out_specs=[pl.BlockSpec((blk, LANE), lambda b, p, i: (b * nq + i, p)),
                   pl.BlockSpec((None, blk, LANE), lambda b, p, i: (p, b * nq + i, 0))],
        out_shape=[S((rows, MLA_HEADS * V_HEAD), MXU_DT), S((npair, rows, LANE), F32)],
        compiler_params=_cp(("parallel", "parallel", "arbitrary")), name=name)(q, k, v)


def attn_bwd(q, k, v, o, do, lse, nb, tp, name):
    rows = q.shape[0]
    blk = ATT_BLK
    nq = tp // blk
    npair = MLA_HEADS // 2
    scale = QK_HEAD ** -0.5

    def body(q_r, k_r, v_r, o_r, do_r, lse_r, dq_o, dk_o, dv_o, dq_acc, delta_sc):
        kb = pl.program_id(2)
        even = lax.broadcasted_iota(jnp.int32, (blk, LANE), 1) < V_HEAD
        rowi = lax.broadcasted_iota(jnp.int32, (blk, blk), 0)
        coli = lax.broadcasted_iota(jnp.int32, (blk, blk), 1)

        @pl.when(kb == 0)
        def _():
            dq_acc[...] = jnp.zeros_like(dq_acc)

            def dstep(i, c):
                r0 = pl.multiple_of(i * blk, blk)
                prod = do_r[pl.ds(r0, blk), :].astype(F32) * o_r[pl.ds(r0, blk), :].astype(F32)
                de = jnp.sum(jnp.where(even, prod, 0.0), axis=1, keepdims=True)
                dd = jnp.sum(jnp.where(even, 0.0, prod), axis=1, keepdims=True)
                delta_sc[pl.ds(r0, blk), :] = jnp.where(even, de, dd)
                return c

            lax.fori_loop(0, nq, dstep, 0)

        vblk = v_r[...]
        ks = [k_r[:, h * HP:(h + 1) * HP] for h in range(2)]

        def q_step(qb, carry):
            dk0, dk1, dv = carry
            r0 = pl.multiple_of(qb * blk, blk)
            dob = do_r[pl.ds(r0, blk), :]
            lse_b = lse_r[pl.ds(r0, blk), :]
            dl_b = delta_sc[pl.ds(r0, blk), :]
            dks = [dk0, dk1]
            for h in range(2):
                lo = 0 if h == 0 else V_HEAD
                qh = q_r[pl.ds(r0, blk), h * HP:(h + 1) * HP]
                s = lax.dot_general(qh, ks[h], NT_DIMS, preferred_element_type=F32) * scale
                p = jnp.exp(s - lse_b[:, lo:lo + 1])
                p = jnp.where((qb > kb) | (coli <= rowi), p, 0.0)
                doh = jnp.where(even if h == 0 else ~even, dob, jnp.zeros_like(dob))
                dp = lax.dot_general(doh, vblk, NT_DIMS, preferred_element_type=F32)
                ds = (p * (dp - dl_b[:, lo:lo + 1]) * scale).astype(MXU_DT)
                dv = dv + lax.dot_general(p.astype(MXU_DT), doh, TN_DIMS, preferred_element_type=F32)
                dks[h] = dks[h] + lax.dot_general(ds, qh, TN_DIMS, preferred_element_type=F32)
                dq_acc[pl.ds(r0, blk), h * HP:(h + 1) * HP] += jnp.dot(ds, ks[h], preferred_element_type=F32)
            return dks[0], dks[1], dv

        z = jnp.zeros((blk, HP), F32)
        dk0, dk1, dv = lax.fori_loop(kb, nq, q_step, (z, z, jnp.zeros((blk, LANE), F32)))
        dk_o[:, 0:HP] = dk0.astype(MXU_DT)
        dk_o[:, HP:2 * HP] = dk1.astype(MXU_DT)
        dv_o[...] = dv.astype(MXU_DT)

        @pl.when(kb == nq - 1)
        def _():
            dq_o[...] = dq_acc[...].astype(MXU_DT)

    seq_pair = pl.BlockSpec((tp, LANE), lambda b, p, kk: (b, p))
    return pl.pallas_call(
        body, grid=(nb, npair, nq),
        in_specs=[pl.BlockSpec((tp, 2 * HP), lambda b, p, kk: (b, p)),
                  pl.BlockSpec((blk, 2 * HP), lambda b, p, kk: (b * nq + kk, p)),
                  pl.BlockSpec((blk, LANE), lambda b, p, kk: (b * nq + kk, p)),
                  seq_pair, seq_pair, pl.BlockSpec((None, tp, LANE), lambda b, p, kk: (p, b, 0))],
        out_specs=[pl.BlockSpec((tp, 2 * HP), lambda b, p, kk: (b, p)),
                   pl.BlockSpec((blk, 2 * HP), lambda b, p, kk: (b * nq + kk, p)),
                   pl.BlockSpec((blk, LANE), lambda b, p, kk: (b * nq + kk, p))],
        out_shape=[S((rows, MLA_HEADS * HP), MXU_DT), S((rows, MLA_HEADS * HP), MXU_DT),
                   S((rows, MLA_HEADS * V_HEAD), MXU_DT)],
        scratch_shapes=[pltpu.VMEM((tp, 2 * HP), F32), pltpu.VMEM((tp, LANE), F32)],
        compiler_params=_cp(("parallel", "parallel", "arbitrary")), name=name)(q, k, v, o, do, lse)


def rope_bwd(dq, dk, dv, tabq, tabk, tp, tm, name):
    rows = dq.shape[0]
    nt = tp // tm
    wq = MLA_HEADS * HP

    def body(dq_r, dk_r, dv_r, tq_r, tk_r, dqa_o, dkv_o, dkr_o):
        dqv = dq_r[...].astype(F32)
        lane = _lane_mod(dqv.shape)
        in_rope = (lane >= QK_NOPE) & (lane < QK_HEAD)
        rope = jnp.where(in_rope, dqv, 0.0)
        da = jnp.where(lane < QK_HEAD, dqv, 0.0) + pltpu.roll(rope, QK_ROPE, axis=1)
        dqa_o[...] = (da * jnp.tile(tq_r[...], (1, MLA_HEADS))).astype(MXU_DT)
        dkf = dk_r[...].astype(F32)
        dkv_o[:, 0:wq] = jnp.where(lane < QK_NOPE, dkf, 0.0).astype(MXU_DT)
        dkv_o[:, wq:] = dv_r[...]
        kr = jnp.where(in_rope, dkf, 0.0)
        tot = kr[:, 0:HP]
        for h in range(1, MLA_HEADS):
            tot = tot + kr[:, h * HP:(h + 1) * HP]
        dkr_o[...] = ((tot + pltpu.roll(tot, QK_ROPE, axis=1)) * tk_r[...]).astype(MXU_DT)

    def rowblk(wd):
        return pl.BlockSpec((tm, wd), lambda i: (i, 0))

    tab = pl.BlockSpec((tm, HP), lambda i: (i % nt, 0))
    return pl.pallas_call(
        body, grid=(rows // tm,), in_specs=[rowblk(wq), rowblk(wq), rowblk(MLA_HEADS * V_HEAD), tab, tab],
        out_specs=[rowblk(wq), rowblk(wq + MLA_HEADS * V_HEAD), rowblk(HP)],
        out_shape=[S((rows, wq), MXU_DT), S((rows, wq + MLA_HEADS * V_HEAD), MXU_DT), S((rows, HP), MXU_DT)],
        compiler_params=_cp(("parallel",)), name=name)(dq, dk, dv, tabq, tabk)


def loss_head(h, target, gain, tp, t_real, tm, name):
    rows = h.shape[0]
    nt = tp // tm

    def body(h_r, t_r, g_r, dh_o, loss_o, dg_o):
        i = pl.program_id(0)

        @pl.when(i == 0)
        def _():
            loss_o[...] = jnp.zeros_like(loss_o)
            dg_o[...] = jnp.zeros_like(dg_o)

        xv = h_r[...]
        rstd = lax.rsqrt(jnp.mean(xv * xv, axis=-1, keepdims=True) + EPS)
        xhat = xv * rstd
        g = g_r[...]
        pos = (i % nt) * tm + lax.broadcasted_iota(jnp.int32, (tm, 1), 0)
        valid = (pos >= N_META) & (pos < t_real)
        err = jnp.where(valid, xhat * g - t_r[...], 0.0)
        loss_o[...] += 0.5 * jnp.sum(jnp.mean(err * err, axis=-1, keepdims=True))
        dy = err * (1.0 / D_MODEL)
        dg_o[...] += jnp.sum(dy * xhat, axis=0, keepdims=True)
        dxh = dy * g
        dh_o[...] = rstd * (dxh - xhat * jnp.mean(dxh * xhat, axis=-1, keepdims=True))

    blk = pl.BlockSpec((tm, D_MODEL), lambda i: (i, 0))
    return pl.pallas_call(
        body, grid=(rows // tm,), in_specs=[blk, blk, pl.BlockSpec((1, D_MODEL), lambda i: (0, 0))],
        out_specs=[blk, pl.BlockSpec((1, LANE), lambda i: (0, 0)), pl.BlockSpec((1, D_MODEL), lambda i: (0, 0))],
        out_shape=[S((rows, D_MODEL), F32), S((1, LANE), F32), S((1, D_MODEL), F32)],
        compiler_params=_cp(("arbitrary",)), name=name)(h, target, gain)


ADAM_TILE_ELEMS = 128 * 1024


def adamw(g, w, m, v, name):
    shape = w.shape
    cols = shape[-1]
    rws = max(1, math.prod(shape[:-1]))
    tr = rws if rws * cols <= ADAM_TILE_ELEMS else _div_tile(rws, max(SUBLANE, ADAM_TILE_ELEMS // cols), SUBLANE)
    bc1 = 1.0 - ADAM_B1 ** ADAM_STEP
    bc2 = 1.0 - ADAM_B2 ** ADAM_STEP

    def body(g_r, w_r, m_r, v_r, go, do, mo, vo):
        gv = g_r[...]
        mn = ADAM_B1 * m_r[...] + (1.0 - ADAM_B1) * gv
        vn = ADAM_B2 * v_r[...] + (1.0 - ADAM_B2) * (gv * gv)
        m_hat = mn / bc1
        v_hat = vn / bc2
        go[...] = gv
        do[...] = -ADAM_LR * (m_hat / (jnp.sqrt(v_hat) + ADAM_EPS) + ADAM_WD * w_r[...])
        mo[...] = mn
        vo[...] = vn

    blk = pl.BlockSpec((tr, cols), lambda i: (i, 0))
    outs = pl.pallas_call(
        body, grid=(rws // tr,), in_specs=[blk] * 4, out_specs=[blk] * 4, out_shape=[S((rws, cols), F32)] * 4,
        compiler_params=_cp(("parallel",)), name=name,
    )(*[a.reshape(rws, cols) for a in (g, w, m, v)])
    return tuple(o.reshape(shape) for o in outs)


HBM_SPEC = pl.BlockSpec(memory_space=pltpu.HBM)
SUM_ROWS = 2048


def _place():
    return lax.axis_index("x"), lax.axis_index("y"), lax.axis_index("c")


def _remote(src, dst, send_sems, recv_sems, k, to):
    return pltpu.make_async_remote_copy(src_ref=src, dst_ref=dst, send_sem=send_sems.at[k], recv_sem=recv_sems.at[k],
                                        device_id=to, device_id_type=MESH)


def all_gather_chips(xs, name):
    _, rws, _ = xs.shape

    def body(x_ref, out_ref, send_sems, recv_sems, local_sem):
        mx, my, mc = _place()
        sibling = (mx, my, 1 - mc)
        chips = [(1 - mx, my), (mx, 1 - my), (1 - mx, 1 - my)]

        def piece(cx, cy, h):
            return out_ref.at[2 * cx + cy, h]

        mine = pltpu.make_async_copy(x_ref, out_ref.at[2 * mx + my], local_sem)
        mine.start()
        first = [_remote(x_ref.at[mc], piece(mx, my, mc), send_sems, recv_sems, j, (cx, cy, mc))
                 for j, (cx, cy) in enumerate(chips)]
        for cp in first:
            cp.start()
        passed = [_remote(piece(cx, cy, mc), piece(cx, cy, mc), send_sems, recv_sems, 3 + j, sibling)
                  for j, (cx, cy) in enumerate(chips)]
        for j, (cx, cy) in enumerate(chips):
            _remote(x_ref.at[mc], piece(cx, cy, mc), send_sems, recv_sems, j, (cx, cy, mc)).wait_recv()
            passed[j].start()
        for j, (cx, cy) in enumerate(chips):
            _remote(x_ref.at[mc], piece(cx, cy, 1 - mc), send_sems, recv_sems, 3 + j, sibling).wait_recv()
        for cp in first + passed:
            cp.wait_send()
        mine.wait()

    return pl.pallas_call(
        body, out_shape=S((4, 2, rws, LANE), xs.dtype), in_specs=[HBM_SPEC], out_specs=HBM_SPEC,
        scratch_shapes=[pltpu.SemaphoreType.DMA((6,)), pltpu.SemaphoreType.DMA((6,)), pltpu.SemaphoreType.DMA],
        name=name)(xs)


def pair_exchange_halves(g4, name):
    _, _, rws, _ = g4.shape

    def body(g_ref, land_ref, send_sems, recv_sems):
        mx, my, mc = _place()
        sibling = (mx, my, 1 - mc)
        cps = [_remote(g_ref.at[s, 1 - mc], land_ref.at[s], send_sems, recv_sems, s, sibling) for s in range(4)]
        for cp in cps:
            cp.start()
        for cp in cps:
            cp.wait_recv()
        for cp in cps:
            cp.wait_send()

    return pl.pallas_call(
        body, out_shape=S((4, rws, LANE), g4.dtype), in_specs=[HBM_SPEC], out_specs=HBM_SPEC,
        scratch_shapes=[pltpu.SemaphoreType.DMA((4,)), pltpu.SemaphoreType.DMA((4,))], name=name)(g4)


def pair_sum(g4, land, c_idx, name):
    _, _, rws, _ = g4.shape
    th = SUM_ROWS

    def body(c_ref, a_ref, b_ref, o_ref):
        o_ref[...] = a_ref[...] + b_ref[...]

    return pl.pallas_call(
        body,
        grid_spec=pltpu.PrefetchScalarGridSpec(
            num_scalar_prefetch=1, grid=(4, rws // th),
            in_specs=[pl.BlockSpec((None, None, th, LANE), lambda s, i, c: (s, c[0], i, 0)),
                      pl.BlockSpec((None, th, LANE), lambda s, i, c: (s, i, 0))],
            out_specs=pl.BlockSpec((None, th, LANE), lambda s, i, c: (s, i, 0))),
        out_shape=S((4, rws, LANE), F32), compiler_params=_cp(("parallel", "parallel")), name=name)(c_idx, g4, land)


def chip_scatter(p4, name):
    _, rws, _ = p4.shape

    def body(p_ref, land_ref, send_sems, recv_sems, local_sem):
        mx, my, mc = _place()
        me = 2 * mx + my
        chips = [(1 - mx, my), (mx, 1 - my), (1 - mx, 1 - my)]
        mine = pltpu.make_async_copy(p_ref.at[me], land_ref.at[me], local_sem)
        mine.start()
        cps = [_remote(p_ref.at[2 * cx + cy], land_ref.at[me], send_sems, recv_sems, j, (cx, cy, mc))
               for j, (cx, cy) in enumerate(chips)]
        for cp in cps:
            cp.start()
        for j, (cx, cy) in enumerate(chips):
            _remote(p_ref.at[me], land_ref.at[2 * cx + cy], send_sems, recv_sems, j, (cx, cy, mc)).wait_recv()
        for cp in cps:
            cp.wait_send()
        mine.wait()

    return pl.pallas_call(
        body, out_shape=S((4, rws, LANE), p4.dtype), in_specs=[HBM_SPEC], out_specs=HBM_SPEC,
        scratch_shapes=[pltpu.SemaphoreType.DMA((3,)), pltpu.SemaphoreType.DMA((3,)), pltpu.SemaphoreType.DMA],
        name=name)(p4)


def chip_sum(l4, name):
    _, rws, _ = l4.shape
    th = SUM_ROWS

    def body(a, b, c, d, o_ref):
        o_ref[...] = ((a[...] + b[...]) + c[...]) + d[...]

    def blk(s):
        return pl.BlockSpec((None, th, LANE), lambda i: (s, i, 0))

    return pl.pallas_call(
        body, grid=(rws // th,), in_specs=[blk(0), blk(1), blk(2), blk(3)],
        out_specs=pl.BlockSpec((th, LANE), lambda i: (i, 0)), out_shape=S((rws, LANE), F32),
        compiler_params=_cp(("parallel",)), name=name)(l4, l4, l4, l4)


def pair_gather(rh, name):
    rws, _ = rh.shape

    def body(r_ref, out_ref, send_sems, recv_sems, local_sem):
        mx, my, mc = _place()
        sibling = (mx, my, 1 - mc)
        mine = pltpu.make_async_copy(r_ref, out_ref.at[mc], local_sem)
        mine.start()
        cp = _remote(r_ref, out_ref.at[mc], send_sems, recv_sems, 0, sibling)
        cp.start()
        _remote(r_ref, out_ref.at[1 - mc], send_sems, recv_sems, 0, sibling).wait_recv()
        cp.wait_send()
        mine.wait()

    return pl.pallas_call(
        body, out_shape=S((2, rws, LANE), rh.dtype), in_specs=[HBM_SPEC], out_specs=HBM_SPEC,
        scratch_shapes=[pltpu.SemaphoreType.DMA((1,)), pltpu.SemaphoreType.DMA((1,)), pltpu.SemaphoreType.DMA],
        name=name)(rh)


PACK_ELEMS = 16 * LANE


def pack_rows(arrays, lead, total_mult):
    parts, offs, r0 = [], [], 0
    for a in arrays:
        flat = a.reshape(a.shape[:lead] + (-1,))
        elems = _round_up(flat.shape[-1], PACK_ELEMS)
        flat = jnp.pad(flat, [(0, 0)] * lead + [(0, elems - flat.shape[-1])])
        parts.append(flat.reshape(flat.shape[:lead] + (elems // LANE, LANE)))
        offs.append((r0, elems // LANE))
        r0 += elems // LANE
    total = _round_up(r0, total_mult)
    if total > r0:
        parts.append(jnp.zeros(parts[0].shape[:lead] + (total - r0, LANE), parts[0].dtype))
    return jnp.concatenate(parts, axis=lead), offs


def unpack_rows(buf, off, shape):
    r0, nr = off
    lead = buf.shape[:-2]
    n = math.prod(shape)
    return buf[..., r0:r0 + nr, :].reshape(lead + (nr * LANE,))[..., :n].reshape(lead + tuple(shape))


def unshard(stacked, axis):
    x = jnp.moveaxis(stacked, 0, axis)
    return x.reshape(x.shape[:axis] + (4 * x.shape[axis + 1],) + x.shape[axis + 2:])


def to_shards(full, axis):
    n = full.shape[axis] // 4
    x = full.reshape(full.shape[:axis] + (4, n) + full.shape[axis + 1:])
    return jnp.moveaxis(x, axis, 0)


def _rot_cols(w):
    half = w.shape[-1] // 2
    return jnp.concatenate([-w[..., half:], w[..., :half]], axis=-1)


def _unrot_cols(dw):
    half = dw.shape[-1] // 2
    return jnp.concatenate([dw[..., half:], -dw[..., :half]], axis=-1)


def odd_w_in_padded(w_in):
    kr = w_in[:, Q_LORA + KV_LORA:]
    rows = w_in.shape[0]
    return jnp.concatenate([w_in[:, :Q_LORA], jnp.zeros((rows, 128), w_in.dtype), w_in[:, Q_LORA:Q_LORA + KV_LORA],
                            jnp.zeros((rows, 64), w_in.dtype), kr, _rot_cols(kr)], axis=1)


def odd_w_in_unpad(dwp):
    base = 512 + KV_LORA + 64
    dkr = dwp[:, base:base + QK_ROPE] + _unrot_cols(dwp[:, base + QK_ROPE:base + 2 * QK_ROPE])
    return jnp.concatenate([dwp[:, :Q_LORA], dwp[:, 512:512 + KV_LORA], dkr], axis=1)


def uq_padded(w_uq):
    w = w_uq.reshape(Q_LORA, MLA_HEADS, QK_HEAD)
    return jnp.concatenate([w, _rot_cols(w[:, :, QK_NOPE:])], axis=-1).reshape(Q_LORA, MLA_HEADS * HP)


def uq_unpad(dwp):
    d = dwp.reshape(Q_LORA, MLA_HEADS, HP)
    rope = d[:, :, QK_NOPE:QK_HEAD] + _unrot_cols(d[:, :, QK_HEAD:])
    return jnp.concatenate([d[:, :, :QK_NOPE], rope], axis=-1).reshape(Q_LORA, MLA_HEADS * QK_HEAD)


def ukv_padded(w_ukv):
    w = w_ukv.reshape(KV_LORA, MLA_HEADS, QK_NOPE + V_HEAD)
    wk = jnp.concatenate([w[:, :, :QK_NOPE], jnp.zeros((KV_LORA, MLA_HEADS, HP - QK_NOPE), w.dtype)], axis=-1)
    return jnp.concatenate([wk.reshape(KV_LORA, MLA_HEADS * HP), w[:, :, QK_NOPE:].reshape(KV_LORA, MLA_HEADS * V_HEAD)],
                           axis=1)


def ukv_unpad(dwp):
    dk = dwp[:, :MLA_HEADS * HP].reshape(KV_LORA, MLA_HEADS, HP)[:, :, :QK_NOPE]
    dv = dwp[:, MLA_HEADS * HP:].reshape(KV_LORA, MLA_HEADS, V_HEAD)
    return jnp.concatenate([dk, dv], axis=-1).reshape(KV_LORA, MLA_HEADS * (QK_NOPE + V_HEAD))


def block_diag(w):
    h, d, _ = w.shape
    eye = jnp.eye(h, dtype=w.dtype)
    return (eye[:, None, :, None] * w[:, :, None, :]).reshape(h * d, h * d)


def block_diag_part(dense, h):
    d = dense.shape[0] // h
    x = dense.reshape(h, d, h, d)
    return jnp.stack([x[i, :, i, :] for i in range(h)], axis=0)


def rope_tables(tp):
    pos = jnp.arange(tp, dtype=F32)
    inv_freq = ROPE_BASE ** (-jnp.arange(0, QK_ROPE, 2, dtype=F32) / QK_ROPE)
    ang = pos[:, None] * inv_freq[None, :]
    cos2 = jnp.tile(jnp.cos(ang), (1, 2))
    sin2 = jnp.tile(jnp.sin(ang), (1, 2))
    tabq = jnp.concatenate([jnp.ones((tp, QK_NOPE), F32), cos2, sin2], axis=1)
    tabk = jnp.concatenate([jnp.zeros((tp, QK_NOPE), F32), cos2, sin2], axis=1)
    return tabq, tabk


class Dims:
    def __init__(self, nb, seq):
        self.nb = nb
        self.t_real = seq + N_META
        self.tp = _round_up(self.t_real, ATT_BLK)
        self.n = self.tp // 4
        assert self.n % 16 == 0
        self.rows = nb * self.tp


def even_fwd(h, p, dm):
    u, hn = norm_matmul(h, 0, D_MODEL, p["norm"], p["w_in"], dm.n, 512, F32, "ev_in")
    y, ca, xc, a, hs = even_mid_fwd(u, p["conv_a"], p["conv_b"], p["conv_b_bias"], p["rw"], p["r_b"], p["iw"], p["i_b"],
                                    p["lam"], dm.nb, dm.tp, dm.n, "ev_mid")
    out = matmul_res(y, p["w_out"].reshape(2, CONV_W, D_MODEL), h, dm.n, 512, "ev_out")
    return out, (h, u, hn, ca, xc, a, hs, y)


def even_bwd(dout, saved, p, dm):
    h, u, hn, ca, xc, a, hs, y = saved
    g = {}
    dycat = matmul_nt(dout, p["w_out"], dm.n, 512, F32, "ev_dycat")
    g["w_out"] = matmul_tn(y, dout, dm.n, "ev_dw_out")
    outs = even_mid_bwd(u, dycat, ca, xc, a, hs, p["conv_a"], p["conv_b"], p["rw"], p["r_b"], p["iw"], p["i_b"],
                        p["lam"], dm.nb, dm.tp, dm.n, "ev_mid_bwd")
    du, g["conv_a"], g["conv_b"], g["conv_b_bias"], drw, g["r_b"], diw, g["i_b"], g["lam"] = outs
    g["r_w"] = block_diag_part(drw, LRU_HEADS)
    g["i_w"] = block_diag_part(diw, LRU_HEADS)
    g["w_in"] = matmul_tn(hn, du, dm.n, "ev_dw_in")
    dx, g["norm"] = matmul_nt_normbwd(du, p["w_in"], h, 0, p["norm"], dout, dm.n, 512, F32, "ev_dx")
    return dx, g


def odd_fwd(h, p, tabq, tabk, dm):
    nt = dm.tp // dm.n
    u, hn = norm_matmul(h, 0, D_MODEL, p["norm"], p["w_in_p"], dm.n, ODD_PAD, F32, "od_in")
    tab_spec = pl.BlockSpec((dm.n, HP), lambda i, j: (i % nt, 0))
    q, cqn = norm_matmul(u, 0, Q_LORA, p["q_norm"], p["w_uq_p"], dm.n, 512, MXU_DT, "od_q",
                         epi=_q_rope_epi, epi_ops=(tabq,), epi_specs=(tab_spec,))
    kr_spec = pl.BlockSpec((dm.n, HP), lambda i, j: (i, ODD_KR_COL))
    k, ckvn = norm_matmul(u, ODD_CKV_COL, KV_LORA, p["kv_norm"], p["w_uk_p"], dm.n, 512, MXU_DT, "od_k",
                          epi=_k_rope_epi, epi_ops=(u, tabk), epi_specs=(kr_spec, tab_spec))
    v, _ = norm_matmul(u, ODD_CKV_COL, KV_LORA, p["kv_norm"], p["w_uv_p"], dm.n, 512, MXU_DT, "od_v")
    o, lse = attn_fwd(q, k, v, dm.nb, dm.tp, "od_attn")
    out = matmul_res(o[None], p["w_out"][None], h, dm.n, 512, "od_out")
    return out, (h, u, hn, cqn, ckvn, q, k, v, o, lse)


def odd_bwd(dout, saved, p, tabq, tabk, dm):
    h, u, hn, cqn, ckvn, q, k, v, o, lse = saved
    g = {}
    do = matmul_nt(dout, p["w_out"], dm.n, 512, MXU_DT, "od_do")
    g["w_out"] = matmul_tn(o, dout, dm.n, "od_dw_out")
    dq, dk, dv = attn_bwd(q, k, v, o, do, lse, dm.nb, dm.tp, "od_attn_bwd")
    dqa, dkv, dkr = rope_bwd(dq, dk, dv, tabq, tabk, dm.tp, dm.n, "od_rope_bwd")
    g["w_uq_p"] = matmul_tn(cqn, dqa, dm.n, "od_dw_uq")
    g["w_ukv_p"] = matmul_tn(ckvn, dkv, dm.n, "od_dw_ukv")
    dcq, g["q_norm"] = matmul_nt_normbwd(dqa, p["w_uq_p"], u, 0, p["q_norm"], None, dm.n, 512, MXU_DT, "od_dcq")
    dckv, g["kv_norm"] = matmul_nt_normbwd(dkv, p["w_ukv_p"], u, ODD_CKV_COL, p["kv_norm"], None, dm.n, 512, MXU_DT,
                                           "od_dckv")
    du = jnp.concatenate([dcq, jnp.zeros((dm.rows, 128), MXU_DT), dckv, dkr], axis=1)
    g["w_in_p"] = matmul_tn(hn, du, dm.n, "od_dw_in")
    dx, g["norm"] = matmul_nt_normbwd(du, p["w_in_p"], h, 0, p["norm"], dout, dm.n, ODD_PAD, F32, "od_dx")
    return dx, g


def ffn_fwd(h, p, dm):
    up, hn = norm_matmul(h, 0, D_MODEL, p["norm"], p["w_up"], dm.n, 512, F32, "ffn_up")
    u, y = ffn_mid_fwd(up, p["cw"], p["cb"], dm.nb, dm.tp, dm.n, "ffn_mid")
    out = matmul_res(y[None], p["w_down"][None], h, dm.n, 512, "ffn_down")
    return out, (h, up, hn, u, y)


def ffn_bwd(dout, saved, p, dm):
    h, up, hn, u, y = saved
    g = {}
    dy = matmul_nt(dout, p["w_down"], dm.n, FFN_CT, F32, "ffn_dy")
    g["w_down"] = matmul_tn(y, dout, dm.n, "ffn_dw_down")
    dup, g["cw"], g["cb"] = ffn_mid_bwd(dy, u, up, p["cw"], dm.nb, dm.tp, dm.n, "ffn_mid_bwd")
    g["w_up"] = matmul_tn(hn, dup, dm.n, "ffn_dw_up")
    dx, g["norm"] = matmul_nt_normbwd(dup, p["w_up"], h, 0, p["norm"], dout, dm.n, D_FF // 2, F32, "ffn_dx")
    return dx, g


def _row(v):
    return v.reshape(1, -1)


def local_step(x, target, wf):
    nb, seq, _ = x.shape
    dm = Dims(nb, seq)
    tail = dm.tp - dm.t_real
    meta = jnp.broadcast_to(wf["meta_tokens"][None], (nb, N_META, D_MODEL))
    h = jnp.concatenate([meta, x, jnp.zeros((nb, tail, D_MODEL), F32)], axis=1).reshape(dm.rows, D_MODEL)
    tgt = jnp.pad(target, ((0, 0), (N_META, tail), (0, 0))).reshape(dm.rows, D_MODEL)
    tabq, tabk = rope_tables(dm.tp)

    ev, od, ffn = [], [], []
    for j in range(DEPTH // 2):
        ev.append(dict(norm=_row(wf["ev_norm"][j]), w_in=wf["ev_w_in"][j], conv_a=wf["ev_conv_a"][j],
                       conv_b=wf["ev_conv_b"][j], conv_b_bias=_row(wf["ev_conv_b_bias"][j]),
                       rw=block_diag(wf["ev_gate_r_w"][j]).astype(MXU_DT), r_b=_row(wf["ev_gate_r_b"][j]),
                       iw=block_diag(wf["ev_gate_i_w"][j]).astype(MXU_DT), i_b=_row(wf["ev_gate_i_b"][j]),
                       lam=_row(wf["ev_lru_lambda"][j]), w_out=wf["ev_w_out"][j]))
        wkv = ukv_padded(wf["od_w_ukv"][j])
        od.append(dict(norm=_row(wf["od_norm"][j]), w_in_p=odd_w_in_padded(wf["od_w_in"][j]),
                       q_norm=_row(wf["od_q_norm"][j]), kv_norm=_row(wf["od_kv_norm"][j]),
                       w_uq_p=uq_padded(wf["od_w_uq"][j]), w_ukv_p=wkv, w_uk_p=wkv[:, :MLA_HEADS * HP],
                       w_uv_p=wkv[:, MLA_HEADS * HP:], w_out=wf["od_w_out"][j]))
    for layer in range(DEPTH):
        ffn.append(dict(norm=_row(wf["ffn_norm"][layer]), w_up=wf["ffn_w_up"][layer],
                        cw=jnp.moveaxis(wf["ffn_conv_w"][layer].reshape(3, 2, D_FF), 1, 0),
                        cb=wf["ffn_conv_b"][layer].reshape(2, 1, D_FF), w_down=wf["ffn_w_down"][layer]))

    saved = []
    for layer in range(DEPTH):
        if layer % 2 == 0:
            h, sv = even_fwd(h, ev[layer // 2], dm)
        else:
            h, sv = odd_fwd(h, od[layer // 2], tabq, tabk, dm)
        saved.append(sv)
        h, sv = ffn_fwd(h, ffn[layer], dm)
        saved.append(sv)

    dh, loss, dfinal = loss_head(h, tgt, _row(wf["final_norm"]), dm.tp, dm.t_real, dm.n, "loss_head")

    gl = {k_: [None] * (DEPTH // 2) for k_ in
          ("ev_norm", "ev_w_in", "ev_conv_a", "ev_conv_b", "ev_conv_b_bias", "ev_gate_r_w", "ev_gate_r_b",
           "ev_gate_i_w", "ev_gate_i_b", "ev_lru_lambda", "ev_w_out", "od_norm", "od_w_in", "od_q_norm",
           "od_kv_norm", "od_w_uq", "od_w_ukv", "od_w_out")}
    gl.update({k_: [None] * DEPTH for k_ in ("ffn_norm", "ffn_w_up", "ffn_conv_w", "ffn_conv_b", "ffn_w_down")})
    for layer in reversed(range(DEPTH)):
        dh, g = ffn_bwd(dh, saved[2 * layer + 1], ffn[layer], dm)
        gl["ffn_norm"][layer] = g["norm"][0]
        gl["ffn_w_up"][layer] = g["w_up"]
        gl["ffn_conv_w"][layer] = jnp.moveaxis(g["cw"], 0, 1).reshape(3, 2 * D_FF)
        gl["ffn_conv_b"][layer] = g["cb"].reshape(2 * D_FF)
        gl["ffn_w_down"][layer] = g["w_down"]
        j = layer // 2
        if layer % 2 == 0:
            dh, g = even_bwd(dh, saved[2 * layer], ev[j], dm)
            for src, dst in (("norm", "ev_norm"), ("conv_b_bias", "ev_conv_b_bias"), ("r_b", "ev_gate_r_b"),
                             ("i_b", "ev_gate_i_b"), ("lam", "ev_lru_lambda")):
                gl[dst][j] = g[src][0]
            for src, dst in (("w_in", "ev_w_in"), ("conv_a", "ev_conv_a"), ("conv_b", "ev_conv_b"),
                             ("r_w", "ev_gate_r_w"), ("i_w", "ev_gate_i_w"), ("w_out", "ev_w_out")):
                gl[dst][j] = g[src]
        else:
            dh, g = odd_bwd(dh, saved[2 * layer], od[j], tabq, tabk, dm)
            gl["od_norm"][j] = g["norm"][0]
            gl["od_q_norm"][j] = g["q_norm"][0]
            gl["od_kv_norm"][j] = g["kv_norm"][0]
            gl["od_w_in"][j] = odd_w_in_unpad(g["w_in_p"])
            gl["od_w_uq"][j] = uq_unpad(g["w_uq_p"])
            gl["od_w_ukv"][j] = ukv_unpad(g["w_ukv_p"])
            gl["od_w_out"][j] = g["w_out"]
    grads = {k_: jnp.stack(v_, axis=0) for k_, v_ in gl.items()}
    grads["final_norm"] = dfinal[0]
    dh3 = dh.reshape(nb, dm.tp, D_MODEL)
    grads["meta_tokens"] = jnp.sum(dh3[:, :N_META], axis=0)
    return loss[0, 0], dh3[:, N_META:dm.t_real], grads


WEIGHTS = ["meta_tokens", "ev_norm", "ev_w_in", "ev_conv_a", "ev_conv_b", "ev_conv_b_bias", "ev_gate_r_w", "ev_gate_r_b",
           "ev_gate_i_w", "ev_gate_i_b", "ev_lru_lambda", "ev_w_out", "od_norm", "od_w_in", "od_q_norm", "od_kv_norm",
           "od_w_uq", "od_w_ukv", "od_w_out", "ffn_norm", "ffn_w_up", "ffn_conv_w", "ffn_conv_b", "ffn_w_down",
           "final_norm"]
SHARD_AXIS = {"meta_tokens": 1, "ev_w_in": 2, "ev_conv_a": 2, "ev_conv_b": 2, "ev_w_out": 1, "od_norm": 1, "od_w_in": 1,
              "od_q_norm": 1, "od_kv_norm": 1, "od_w_uq": 2, "od_w_ukv": 2, "od_w_out": 1, "ffn_w_up": 2,
              "ffn_conv_w": 2, "ffn_w_down": 1}
MATMUL_WEIGHTS = ["ev_w_in", "ev_w_out", "od_w_in", "od_w_uq", "od_w_ukv", "od_w_out", "ffn_w_up", "ffn_w_down"]


def gather_weights(w):
    full = {n: w[n] for n in WEIGHTS if n not in SHARD_AXIS}
    groups = ((MATMUL_WEIGHTS, MXU_DT, "gather_matmul_weights"),
              ([n for n in SHARD_AXIS if n not in MATMUL_WEIGHTS], F32, "gather_small_weights"))
    for names, dt, call_name in groups:
        buf, offs = pack_rows([w[n].astype(dt) for n in names], 0, 32)
        got = all_gather_chips(buf.reshape(2, buf.shape[0] // 2, LANE), call_name)
        got = got.reshape(4, buf.shape[0], LANE)
        for n, off in zip(names, offs):
            full[n] = unshard(unpack_rows(got, off, w[n].shape), SHARD_AXIS[n])
    return full


def reduce_gradients(grads, shard_shapes):
    sharded = [n for n in WEIGHTS if n in SHARD_AXIS]
    repl = [n for n in WEIGHTS if n not in SHARD_AXIS]
    gs, offs_s = pack_rows([to_shards(grads[n], SHARD_AXIS[n]) for n in sharded], 1, 16)
    gr, offs_r = pack_rows([grads[n] for n in repl], 0, 64)
    rs, rr = gs.shape[1] // 2, gr.shape[0] // 8
    rows = _round_up(rs + rr, SUM_ROWS)
    parts = [gs.reshape(4, 2, rs, LANE), gr.reshape(4, 2, rr, LANE)]
    if rows > rs + rr:
        parts.append(jnp.zeros((4, 2, rows - rs - rr, LANE), F32))
    g4 = jnp.concatenate(parts, axis=2)
    c_idx = lax.axis_index("c").astype(jnp.int32).reshape(1)
    land = pair_exchange_halves(g4, "grad_pair_exchange")
    part = pair_sum(g4, land, c_idx, "grad_pair_sum")
    land4 = chip_scatter(part, "grad_chip_scatter")
    mine = chip_sum(land4, "grad_chip_sum")
    both = pair_gather(mine, "grad_pair_gather")
    out = {}
    flat_s = both[:, :rs].reshape(2 * rs, LANE)
    for n, off in zip(sharded, offs_s):
        out[n] = unpack_rows(flat_s, off, shard_shapes[n])
    tails = all_gather_chips(both[:, rs:rs + rr], "grad_gather_replicated").reshape(8 * rr, LANE)
    for n, off in zip(repl, offs_r):
        out[n] = unpack_rows(tails, off, shard_shapes[n])
    return out


def kernel(x, meta_tokens, ev_norm, ev_w_in, ev_conv_a, ev_conv_b, ev_conv_b_bias, ev_gate_r_w, ev_gate_r_b, ev_gate_i_w, ev_gate_i_b, ev_lru_lambda, ev_w_out, od_norm, od_w_in, od_q_norm, od_kv_norm, od_w_uq, od_w_ukv, od_w_out, ffn_norm, ffn_w_up, ffn_conv_w, ffn_conv_b, ffn_w_down, final_norm, loss_target, m_meta_tokens, m_ev_norm, m_ev_w_in, m_ev_conv_a, m_ev_conv_b, m_ev_conv_b_bias, m_ev_gate_r_w, m_ev_gate_r_b, m_ev_gate_i_w, m_ev_gate_i_b, m_ev_lru_lambda, m_ev_w_out, m_od_norm, m_od_w_in, m_od_q_norm, m_od_kv_norm, m_od_w_uq, m_od_w_ukv, m_od_w_out, m_ffn_norm, m_ffn_w_up, m_ffn_conv_w, m_ffn_conv_b, m_ffn_w_down, m_final_norm, v_meta_tokens, v_ev_norm, v_ev_w_in, v_ev_conv_a, v_ev_conv_b, v_ev_conv_b_bias, v_ev_gate_r_w, v_ev_gate_r_b, v_ev_gate_i_w, v_ev_gate_i_b, v_ev_lru_lambda, v_ev_w_out, v_od_norm, v_od_w_in, v_od_q_norm, v_od_kv_norm, v_od_w_uq, v_od_w_ukv, v_od_w_out, v_ffn_norm, v_ffn_w_up, v_ffn_conv_w, v_ffn_conv_b, v_ffn_w_down, v_final_norm):
    given = dict(locals())
    w = {n: given[n] for n in WEIGHTS}
    full = gather_weights(w)
    loss, grad_x, grads = local_step(x, loss_target, full)
    loss = lax.psum(loss, ("x", "y", "c"))
    red = reduce_gradients(grads, {n: w[n].shape for n in WEIGHTS})
    g_out, d_out, m_out, v_out = [], [], [], []
    for n in WEIGHTS:
        g, d, m, v = adamw(red[n], w[n], given["m_" + n], given["v_" + n], "adamw_" + n)
        g_out.append(g)
        d_out.append(d)
        m_out.append(m)
        v_out.append(v)
    return (loss, grad_x, *g_out, *d_out, *m_out, *v_out)
```

```python
import math

import jax
import jax.numpy as jnp
from jax import lax
from jax.experimental import pallas as pl
from jax.experimental.pallas import tpu as pltpu

F32 = jnp.float32
MXU_DT = jnp.bfloat16
S = jax.ShapeDtypeStruct
MESH = pl.DeviceIdType.MESH

EPS = 1e-6
D_MODEL = 1024
N_META = 16
DEPTH = 4
CONV_W = 512
LRU_W = 512
LRU_HEADS = 8
LRU_C = 8.0
EVEN_IN = 2560
MLA_HEADS = 16
QK_NOPE = 64
QK_ROPE = 32
QK_HEAD = 96
V_HEAD = 64
Q_LORA = 384
KV_LORA = 256
ROPE_BASE = 10000.0
D_FF = 2816
ODD_PAD = 896
ODD_CKV_COL = 2
ODD_KR_COL = 6
HP = 128
ATT_BLK = 384
Q_PRESCALE = QK_HEAD ** -0.5 * math.log2(math.e)
FFN_CT = 256
LANE = 128
SUBLANE = 8
VMEM_LIMIT_MB = 52

ADAM_LR = 0.001
ADAM_B1 = 0.9
ADAM_B2 = 0.999
ADAM_EPS = 1e-08
ADAM_WD = 0.01
ADAM_STEP = 10

NT_DIMS = (((1,), (1,)), ((), ()))
TN_DIMS = (((0,), (0,)), ((), ()))


def _cp(sem):
    return pltpu.CompilerParams(dimension_semantics=sem, vmem_limit_bytes=VMEM_LIMIT_MB << 20)


def _div_tile(n, cap, mult):
    if n <= cap:
        return n
    best = None
    for t in range(mult, cap + 1, mult):
        if n % t == 0:
            best = t
    assert best is not None, (n, cap, mult)
    return best


def _round_up(n, m):
    return -(-n // m) * m


def mat_cols(arr):
    return arr.shape[1] if arr.ndim == 2 else arr.shape[0] * arr.shape[2]


def mat_width(arr):
    return arr.shape[-1]


def mat_spec(arr, tm, tw, rc):
    if arr.ndim == 2:
        return pl.BlockSpec((tm, tw), lambda *g: rc(*g))
    per = arr.shape[2] // tw
    assert arr.shape[2] % tw == 0

    def imap(*g):
        r, c = rc(*g)
        return (c // per, r, c % per)

    return pl.BlockSpec((None, tm, tw), imap)


def norm_matmul(x, xcol, kdim, gain, w, tm, tn, out_dtype, name, epi=None, epi_ops=(), epi_specs=()):
    rows, n = x.shape[0], w.shape[1]
    n_epi = len(epi_ops)

    def body(x_ref, g_ref, w_ref, *rest):
        epi_refs = rest[:n_epi]
        out_ref, xn_ref, xn_sc = rest[n_epi:]

        @pl.when(pl.program_id(1) == 0)
        def _():
            xv = x_ref[...]
            y = xv * lax.rsqrt(jnp.mean(xv * xv, axis=-1, keepdims=True) + EPS)
            xn = (y * g_ref[...]).astype(MXU_DT)
            xn_sc[...] = xn
            xn_ref[...] = xn

        acc = jnp.dot(xn_sc[...], w_ref[...], preferred_element_type=F32)
        if epi is not None:
            acc = epi(acc, *[r[...] for r in epi_refs])
        out_ref[...] = acc.astype(out_dtype)

    return pl.pallas_call(
        body, grid=(rows // tm, n // tn),
        in_specs=[pl.BlockSpec((tm, kdim), lambda i, j: (i, xcol)), pl.BlockSpec((1, kdim), lambda i, j: (0, 0)),
                  pl.BlockSpec((kdim, tn), lambda i, j: (0, j)), *epi_specs],
        out_specs=[pl.BlockSpec((tm, tn), lambda i, j: (i, j)), pl.BlockSpec((tm, kdim), lambda i, j: (i, 0))],
        out_shape=[S((rows, n), out_dtype), S((rows, kdim), MXU_DT)],
        scratch_shapes=[pltpu.VMEM((tm, kdim), MXU_DT)],
        compiler_params=_cp(("parallel", "arbitrary")), name=name)(x, gain, w, *epi_ops)


def matmul_res(a, w, res, tm, tn, name):
    grp, rows, k = a.shape
    n = w.shape[2]

    def body(a_ref, w_ref, r_ref, o_ref):
        acc = r_ref[...]
        for g in range(grp):
            acc = acc + jnp.dot(a_ref[g], w_ref[g], preferred_element_type=F32)
        o_ref[...] = acc

    return pl.pallas_call(
        body, grid=(rows // tm, n // tn),
        in_specs=[pl.BlockSpec((grp, tm, k), lambda i, j: (0, i, 0)), pl.BlockSpec((grp, k, tn), lambda i, j: (0, 0, j)),
                  pl.BlockSpec((tm, tn), lambda i, j: (i, j))],
        out_specs=pl.BlockSpec((tm, tn), lambda i, j: (i, j)),
        out_shape=S((rows, n), F32), compiler_params=_cp(("parallel", "parallel")), name=name)(a, w, res)


def matmul_nt(a, w, tm, tn, out_dtype, name):
    rows, k = a.shape
    n = w.shape[0]

    def body(a_ref, w_ref, o_ref):
        o_ref[...] = lax.dot_general(a_ref[...].astype(MXU_DT), w_ref[...], NT_DIMS,
                                     preferred_element_type=F32).astype(out_dtype)

    return pl.pallas_call(
        body, grid=(rows // tm, n // tn),
        in_specs=[pl.BlockSpec((tm, k), lambda i, j: (i, 0)), pl.BlockSpec((tn, k), lambda i, j: (j, 0))],
        out_specs=pl.BlockSpec((tm, tn), lambda i, j: (i, j)),
        out_shape=S((rows, n), out_dtype), compiler_params=_cp(("parallel", "parallel")), name=name)(a, w)


def matmul_nt_normbwd(du, w, x, xcol, gain, res, tm, tk, out_dtype, name):
    rows, kc = du.shape[-2], mat_cols(du)
    dn = w.shape[0]
    nk = kc // tk
    has_res = res is not None

    def body(du_ref, w_ref, x_ref, g_ref, *rest):
        if has_res:
            res_ref, dx_ref, dg_ref, acc = rest
        else:
            dx_ref, dg_ref, acc = rest
        i, k = pl.program_id(0), pl.program_id(1)

        @pl.when(k == 0)
        def _():
            acc[...] = jnp.zeros_like(acc)

        @pl.when((i == 0) & (k == 0))
        def _():
            dg_ref[...] = jnp.zeros_like(dg_ref)

        acc[...] += lax.dot_general(du_ref[...], w_ref[...], NT_DIMS, preferred_element_type=F32)

        @pl.when(k == nk - 1)
        def _():
            dhn = acc[...]
            xv = x_ref[...]
            rstd = lax.rsqrt(jnp.mean(xv * xv, axis=-1, keepdims=True) + EPS)
            xhat = xv * rstd
            dg_ref[...] += jnp.sum(dhn * xhat, axis=0, keepdims=True)
            dxh = dhn * g_ref[...]
            dx = rstd * (dxh - xhat * jnp.mean(dxh * xhat, axis=-1, keepdims=True))
            if has_res:
                dx = dx + res_ref[...]
            dx_ref[...] = dx.astype(out_dtype)

    in_specs = [mat_spec(du, tm, tk, lambda i, k: (i, k)), pl.BlockSpec((dn, tk), lambda i, k: (0, k)),
                pl.BlockSpec((tm, dn), lambda i, k: (i, xcol)), pl.BlockSpec((1, dn), lambda i, k: (0, 0))]
    ops = [du, w, x, gain]
    if has_res:
        in_specs.append(pl.BlockSpec((tm, dn), lambda i, k: (i, 0)))
        ops.append(res)
    return pl.pallas_call(
        body, grid=(rows // tm, nk), in_specs=in_specs,
        out_specs=[pl.BlockSpec((tm, dn), lambda i, k: (i, 0)), pl.BlockSpec((1, dn), lambda i, k: (0, 0))],
        out_shape=[S((rows, dn), out_dtype), S((1, dn), F32)],
        scratch_shapes=[pltpu.VMEM((tm, dn), F32)],
        compiler_params=_cp(("arbitrary", "arbitrary")), name=name)(*ops)


def matmul_tn(a, b, tr, name):
    rows, ka, nb = a.shape[-2], mat_cols(a), mat_cols(b)
    ta = _div_tile(mat_width(a), 1536, LANE)
    tb = _div_tile(mat_width(b), 1536 if ta <= 1024 else 1024, LANE)
    nr = rows // tr

    def body(a_ref, b_ref, o_ref, acc):
        r = pl.program_id(2)

        @pl.when(r == 0)
        def _():
            acc[...] = jnp.zeros_like(acc)

        acc[...] += lax.dot_general(a_ref[...].astype(MXU_DT), b_ref[...].astype(MXU_DT), TN_DIMS,
                                    preferred_element_type=F32)

        @pl.when(r == nr - 1)
        def _():
            o_ref[...] = acc[...]

    return pl.pallas_call(
        body, grid=(ka // ta, nb // tb, nr),
        in_specs=[mat_spec(a, tr, ta, lambda i, j, r: (r, i)), mat_spec(b, tr, tb, lambda i, j, r: (r, j))],
        out_specs=pl.BlockSpec((ta, tb), lambda i, j, r: (i, j)),
        out_shape=S((ka, nb), F32), scratch_shapes=[pltpu.VMEM((ta, tb), F32)],
        compiler_params=_cp(("parallel", "parallel", "arbitrary")), name=name)(a, b)


def _sigmoid(x):
    return 1.0 / (1.0 + jnp.exp(-x))


def _log1p(e):
    return jnp.where(e < 1e-3, e * (1.0 - e * (0.5 - e * (1.0 / 3.0 - 0.25 * e))), jnp.log(1.0 + e))


def _softplus(x):
    return jnp.maximum(x, 0.0) + _log1p(jnp.exp(-jnp.abs(x)))


def _expm1(x):
    series = x * (1.0 + x * (0.5 + x * (1.0 / 6.0 + x * (1.0 / 24.0 + x * (1.0 / 120.0)))))
    return jnp.where(jnp.abs(x) < 0.1, series, jnp.exp(x) - 1.0)


_GELU_K = math.sqrt(2.0 / math.pi)
_GELU_C = 0.044715


def _gelu_and_grad(x):
    th = jnp.tanh(_GELU_K * (x + _GELU_C * x * x * x))
    g = 0.5 * x * (1.0 + th)
    dg = 0.5 * (1.0 + th) + 0.5 * x * (1.0 - th * th) * _GELU_K * (1.0 + 3.0 * _GELU_C * x * x)
    return g, dg


def _row_iota(shape):
    return lax.broadcasted_iota(jnp.int32, shape, 0)


def _scan_chunk_fwd(a_sc, u_sc, out_ref, hcar, n, width):
    rowi = _row_iota((SUBLANE, width))

    def step(c, hprev):
        r0 = pl.multiple_of(c * SUBLANE, SUBLANE)
        a = a_sc[pl.ds(r0, SUBLANE), :]
        u = u_sc[pl.ds(r0, SUBLANE), :]
        for d in (1, 2, 4):
            a_s = jnp.where(rowi >= d, pltpu.roll(a, d, axis=0), 1.0)
            u_s = jnp.where(rowi >= d, pltpu.roll(u, d, axis=0), 0.0)
            u = u + a * u_s
            a = a * a_s
        h = u + a * hprev
        out_ref[pl.ds(r0, SUBLANE), :] = h
        return jnp.broadcast_to(h[SUBLANE - 1:SUBLANE, :], (SUBLANE, width))

    hcar[...] = lax.fori_loop(0, n // SUBLANE, step, hcar[...], unroll=4)


def _scan_chunk_bwd(b_sc, d_sc, out_ref, gcar, n, width):
    rowi = _row_iota((SUBLANE, width))
    nc = n // SUBLANE

    def step(c, gnext):
        r0 = pl.multiple_of((nc - 1 - c) * SUBLANE, SUBLANE)
        b = b_sc[pl.ds(r0, SUBLANE), :]
        d = d_sc[pl.ds(r0, SUBLANE), :]
        for s in (1, 2, 4):
            keep = rowi < SUBLANE - s
            b_s = jnp.where(keep, pltpu.roll(b, SUBLANE - s, axis=0), 1.0)
            d_s = jnp.where(keep, pltpu.roll(d, SUBLANE - s, axis=0), 0.0)
            d = d + b * d_s
            b = b * b_s
        g = d + b * gnext
        out_ref[pl.ds(r0, SUBLANE), :] = g
        return jnp.broadcast_to(g[0:1, :], (SUBLANE, width))

    gcar[...] = lax.fori_loop(0, nc, step, gcar[...], unroll=4)


def even_mid_fwd(u, conv_a, conv_b, conv_b_bias, rw, rb, iw, ib, lam, nb, tp, n, name):
    rows = u.shape[0]
    w = LANE
    nj = CONV_W // w
    nt = tp // n
    h8 = SUBLANE

    def body(gb_r, gc_r, xa_r, xb_r, gate_r, ca_w, cb_w, cb_b, rw_r, rb_r, iw_r, ib_r, lam_r,
             y_o, ca_o, xc_o, a_o, hs_o, pext, xext, hcar, a_sc, u_sc):
        t = pl.program_id(2)

        @pl.when(t == 0)
        def _():
            pext[0:h8, :] = jnp.zeros((h8, w), F32)
            xext[0:h8, :] = jnp.zeros((h8, w), F32)
            hcar[...] = jnp.zeros_like(hcar)

        p = gc_r[...] * xa_r[...]
        pext[h8:h8 + n, :] = p
        wa = ca_w[...]
        ca = wa[2:3, :] * p + wa[1:2, :] * pext[h8 - 1:h8 - 1 + n, :] + wa[0:1, :] * pext[h8 - 2:h8 - 2 + n, :]
        ca_o[...] = ca
        y_o[0] = (gb_r[...] * ca).astype(MXU_DT)
        pext[0:h8, :] = pext[n:n + h8, :]

        xb = xb_r[...]
        xext[h8:h8 + n, :] = xb
        wb = cb_w[...]
        xc = (wb[3:4, :] * xb + wb[2:3, :] * xext[h8 - 1:h8 - 1 + n, :] + wb[1:2, :] * xext[h8 - 2:h8 - 2 + n, :]
              + wb[0:1, :] * xext[h8 - 3:h8 - 3 + n, :]) + cb_b[...]
        xc_o[...] = xc
        xext[0:h8, :] = xext[n:n + h8, :]

        xcm = xc.astype(MXU_DT)
        r = _sigmoid(jnp.dot(xcm, rw_r[...], preferred_element_type=F32) + rb_r[...])
        ig = _sigmoid(jnp.dot(xcm, iw_r[...], preferred_element_type=F32) + ib_r[...])
        log_a = (-LRU_C) * r * _softplus(-lam_r[...])
        a = jnp.exp(log_a)
        mult = jnp.sqrt(-_expm1(2.0 * log_a))
        a_sc[...] = a
        a_o[...] = a
        u_sc[...] = mult * (ig * xc)
        _scan_chunk_fwd(a_sc, u_sc, hs_o, hcar, n, w)
        gel, _ = _gelu_and_grad(gate_r[...])
        y_o[1] = (gel * hs_o[...]).astype(MXU_DT)

    def ublk(off):
        return pl.BlockSpec((n, w), lambda j, b, t: (b * nt + t, off + j))

    def pblk(r_):
        return pl.BlockSpec((r_, w), lambda j, b, t: (0, j))

    act = pl.BlockSpec((n, w), lambda j, b, t: (b * nt + t, j))
    mat = pl.BlockSpec((w, w), lambda j, b, t: (j, j))
    return pl.pallas_call(
        body, grid=(nj, nb, nt),
        in_specs=[ublk(0), ublk(nj), ublk(2 * nj), ublk(3 * nj), ublk(4 * nj), pblk(3), pblk(4), pblk(1),
                  mat, pblk(1), mat, pblk(1), pblk(1)],
        out_specs=[pl.BlockSpec((2, n, w), lambda j, b, t: (0, b * nt + t, j)), act, act, act, act],
        out_shape=[S((2, rows, CONV_W), MXU_DT), S((rows, CONV_W), F32), S((rows, LRU_W), F32), S((rows, LRU_W), F32),
                   S((rows, LRU_W), F32)],
        scratch_shapes=[pltpu.VMEM((n + h8, w), F32), pltpu.VMEM((n + h8, w), F32), pltpu.VMEM((h8, w), F32),
                        pltpu.VMEM((n, w), F32), pltpu.VMEM((n, w), F32)],
        compiler_params=_cp(("parallel", "parallel", "arbitrary")), name=name,
    )(u, u, u, u, u, conv_a, conv_b, conv_b_bias, rw, rb, iw, ib, lam)


def even_mid_bwd(u, dycat, ca, xc, a_sv, hs, conv_a, conv_b, rw, rb, iw, ib, lam, nb, tp, n, name):
    rows = u.shape[0]
    w = LANE
    nj = CONV_W // w
    nt = tp // n
    h8 = SUBLANE

    def body(gb_r, gc_r, xa_r, xb_r, gate_r, dya_r, dyb_r, ca_r, xc_r, a_r, hs_r, hsp_r,
             ca_w, cb_w, rw_r, rb_r, iw_r, ib_r, lam_r,
             du_o, dca_w, dcb_w, dcb_b, drw, drb, diw, dib, dlam,
             aext, hext, dext, eext, gcar, b_sc, d_sc, g_sc):
        b, t = pl.program_id(1), pl.program_id(2)

        @pl.when((b == 0) & (t == 0))
        def _():
            for ref in (dca_w, dcb_w, dcb_b, drw, drb, diw, dib, dlam):
                ref[...] = jnp.zeros_like(ref)

        @pl.when(t == 0)
        def _():
            aext[n:n + h8, :] = jnp.zeros((h8, w), F32)
            dext[n:n + h8, :] = jnp.zeros((h8, w), F32)
            eext[n:n + h8, :] = jnp.zeros((h8, w), F32)
            gcar[...] = jnp.zeros_like(gcar)

        xc_v = xc_r[...]
        xcm = xc_v.astype(MXU_DT)
        r = _sigmoid(jnp.dot(xcm, rw_r[...], preferred_element_type=F32) + rb_r[...])
        ig = _sigmoid(jnp.dot(xcm, iw_r[...], preferred_element_type=F32) + ib_r[...])
        lam_v = lam_r[...]
        sp = _softplus(-lam_v)
        log_a = (-LRU_C) * r * sp
        a = a_r[...]
        mult = jnp.sqrt(-_expm1(2.0 * log_a))
        hs_v = hs_r[...]
        gel, dgel = _gelu_and_grad(gate_r[...])
        dyb = dyb_r[...]
        du_o[4] = (dyb * hs_v * dgel).astype(MXU_DT)

        aext[0:n, :] = a
        b_sc[...] = aext[1:1 + n, :]
        d_sc[...] = dyb * gel
        _scan_chunk_bwd(b_sc, d_sc, g_sc, gcar, n, w)
        aext[n:n + h8, :] = aext[0:h8, :]
        g = g_sc[...]

        hext[0:h8, :] = jnp.where(t == nt - 1, 0.0, hsp_r[...])
        hext[h8:h8 + n, :] = hs_v
        da = g * hext[h8 - 1:h8 - 1 + n, :]
        dmult = g * (ig * xc_v)
        di = g * mult * xc_v
        dxc = g * mult * ig
        dlog_a = da * a - dmult * (a * a) / mult
        dr = dlog_a * ((-LRU_C) * sp)
        dsp = jnp.sum(dlog_a * ((-LRU_C) * r), axis=0, keepdims=True)
        dlam[...] += dsp * (-_sigmoid(-lam_v))
        dzr = dr * r * (1.0 - r)
        dzi = di * ig * (1.0 - ig)
        dzr_m = dzr.astype(MXU_DT)
        dzi_m = dzi.astype(MXU_DT)
        dxc = (dxc + lax.dot_general(dzr_m, rw_r[...], NT_DIMS, preferred_element_type=F32)
               + lax.dot_general(dzi_m, iw_r[...], NT_DIMS, preferred_element_type=F32))
        drw[...] += lax.dot_general(xcm, dzr_m, TN_DIMS, preferred_element_type=F32)
        diw[...] += lax.dot_general(xcm, dzi_m, TN_DIMS, preferred_element_type=F32)
        drb[...] += jnp.sum(dzr, axis=0, keepdims=True)
        dib[...] += jnp.sum(dzi, axis=0, keepdims=True)
        dcb_b[...] += jnp.sum(dxc, axis=0, keepdims=True)

        xb = xb_r[...]
        dext[0:n, :] = dxc
        wb = cb_w[...]
        d1, d2, d3 = dext[1:1 + n, :], dext[2:2 + n, :], dext[3:3 + n, :]
        du_o[3] = (wb[3:4, :] * dxc + wb[2:3, :] * d1 + wb[1:2, :] * d2 + wb[0:1, :] * d3).astype(MXU_DT)
        dcb_w[3:4, :] += jnp.sum(xb * dxc, axis=0, keepdims=True)
        dcb_w[2:3, :] += jnp.sum(xb * d1, axis=0, keepdims=True)
        dcb_w[1:2, :] += jnp.sum(xb * d2, axis=0, keepdims=True)
        dcb_w[0:1, :] += jnp.sum(xb * d3, axis=0, keepdims=True)
        dext[n:n + h8, :] = dext[0:h8, :]

        gb, gc, xa = gb_r[...], gc_r[...], xa_r[...]
        dya = dya_r[...]
        du_o[0] = (dya * ca_r[...]).astype(MXU_DT)
        dca = dya * gb
        eext[0:n, :] = dca
        wa = ca_w[...]
        e1, e2 = eext[1:1 + n, :], eext[2:2 + n, :]
        dp = wa[2:3, :] * dca + wa[1:2, :] * e1 + wa[0:1, :] * e2
        p = gc * xa
        dca_w[2:3, :] += jnp.sum(p * dca, axis=0, keepdims=True)
        dca_w[1:2, :] += jnp.sum(p * e1, axis=0, keepdims=True)
        dca_w[0:1, :] += jnp.sum(p * e2, axis=0, keepdims=True)
        eext[n:n + h8, :] = eext[0:h8, :]
        du_o[1] = (dp * xa).astype(MXU_DT)
        du_o[2] = (dp * gc).astype(MXU_DT)

    def rt(b, t):
        return b * nt + (nt - 1 - t)

    def ublk(off):
        return pl.BlockSpec((n, w), lambda j, b, t: (rt(b, t), off + j))

    def pblk(r_):
        return pl.BlockSpec((r_, w), lambda j, b, t: (0, j))

    act = pl.BlockSpec((n, w), lambda j, b, t: (rt(b, t), j))
    n8 = n // h8
    hsp = pl.BlockSpec((h8, w), lambda j, b, t: (jnp.maximum(rt(b, t) * n8 - 1, 0), j))
    mat = pl.BlockSpec((w, w), lambda j, b, t: (j, j))
    return pl.pallas_call(
        body, grid=(nj, nb, nt),
        in_specs=[ublk(0), ublk(nj), ublk(2 * nj), ublk(3 * nj), ublk(4 * nj), ublk(0), ublk(nj), act, act, act, act,
                  hsp, pblk(3), pblk(4), mat, pblk(1), mat, pblk(1), pblk(1)],
        out_specs=[pl.BlockSpec((5, n, w), lambda j, b, t: (0, rt(b, t), j)), pblk(3), pblk(4), pblk(1),
                   mat, pblk(1), mat, pblk(1), pblk(1)],
        out_shape=[S((5, rows, CONV_W), MXU_DT), S((3, CONV_W), F32), S((4, LRU_W), F32), S((1, LRU_W), F32),
                   S((LRU_W, LRU_W), F32), S((1, LRU_W), F32), S((LRU_W, LRU_W), F32), S((1, LRU_W), F32),
                   S((1, LRU_W), F32)],
        scratch_shapes=[pltpu.VMEM((n + h8, w), F32)] * 4 + [pltpu.VMEM((h8, w), F32)] + [pltpu.VMEM((n, w), F32)] * 3,
        compiler_params=_cp(("arbitrary", "arbitrary", "arbitrary")), name=name,
    )(u, u, u, u, u, dycat, dycat, ca, xc, a_sv, hs, hs, conv_a, conv_b, rw, rb, iw, ib, lam)


def ffn_mid_fwd(up, cw, cb, nb, tp, n, name):
    rows = up.shape[0]
    w = FFN_CT
    nj = D_FF // w
    nt = tp // n
    h8 = SUBLANE

    def body(xa_r, xg_r, w_r, b_r, u_o, y_o, aext, gext):
        t = pl.program_id(2)

        @pl.when(t == 0)
        def _():
            aext[0:h8, :] = jnp.zeros((h8, w), F32)
            gext[0:h8, :] = jnp.zeros((h8, w), F32)

        def conv(x_r, ext, g):
            x = x_r[...].astype(F32)
            ext[h8:h8 + n, :] = x
            wv = w_r[g]
            y = (wv[2:3, :] * x + wv[1:2, :] * ext[h8 - 1:h8 - 1 + n, :] + wv[0:1, :] * ext[h8 - 2:h8 - 2 + n, :]) + b_r[g]
            ext[0:h8, :] = ext[n:n + h8, :]
            return y

        ua = conv(xa_r, aext, 0)
        ug = conv(xg_r, gext, 1)
        u_o[0] = ua.astype(MXU_DT)
        u_o[1] = ug.astype(MXU_DT)
        y_o[...] = (ua * _sigmoid(ua) * ug).astype(MXU_DT)

    def ublk(off):
        return pl.BlockSpec((n, w), lambda j, b, t: (b * nt + t, off + j))

    return pl.pallas_call(
        body, grid=(nj, nb, nt),
        in_specs=[ublk(0), ublk(nj), pl.BlockSpec((2, 3, w), lambda j, b, t: (0, 0, j)),
                  pl.BlockSpec((2, 1, w), lambda j, b, t: (0, 0, j))],
        out_specs=[pl.BlockSpec((2, n, w), lambda j, b, t: (0, b * nt + t, j)), ublk(0)],
        out_shape=[S((2, rows, D_FF), MXU_DT), S((rows, D_FF), MXU_DT)],
        scratch_shapes=[pltpu.VMEM((n + h8, w), F32)] * 2,
        compiler_params=_cp(("parallel", "parallel", "arbitrary")), name=name,
    )(up, up, cw, cb)


def ffn_mid_bwd(dy, u, up, cw, nb, tp, n, name):
    rows = up.shape[0]
    w = FFN_CT
    nj = D_FF // w
    nt = tp // n
    h8 = SUBLANE

    def body(dy_r, u_r, xa_r, xg_r, w_r, dx_o, dw, db, aext, gext):
        b, t = pl.program_id(1), pl.program_id(2)

        @pl.when((b == 0) & (t == 0))
        def _():
            dw[...] = jnp.zeros_like(dw)
            db[...] = jnp.zeros_like(db)

        @pl.when(t == 0)
        def _():
            aext[n:n + h8, :] = jnp.zeros((h8, w), F32)
            gext[n:n + h8, :] = jnp.zeros((h8, w), F32)

        dyv = dy_r[...].astype(F32)
        ua, ug = u_r[0].astype(F32), u_r[1].astype(F32)
        sg = _sigmoid(ua)
        dua = dyv * ug * (sg * (1.0 + ua * (1.0 - sg)))
        dug = dyv * (ua * sg)

        def conv_t(du, ext, x_r, g):
            db[g] += jnp.sum(du, axis=0, keepdims=True)
            ext[0:n, :] = du
            wv = w_r[g]
            x = x_r[...].astype(F32)
            d1, d2 = ext[1:1 + n, :], ext[2:2 + n, :]
            dx_o[g] = (wv[2:3, :] * du + wv[1:2, :] * d1 + wv[0:1, :] * d2).astype(MXU_DT)
            dw[g, 2:3, :] += jnp.sum(x * du, axis=0, keepdims=True)
            dw[g, 1:2, :] += jnp.sum(x * d1, axis=0, keepdims=True)
            dw[g, 0:1, :] += jnp.sum(x * d2, axis=0, keepdims=True)
            ext[n:n + h8, :] = ext[0:h8, :]

        conv_t(dua, aext, xa_r, 0)
        conv_t(dug, gext, xg_r, 1)

    def rt(b, t):
        return b * nt + (nt - 1 - t)

    def ublk(off):
        return pl.BlockSpec((n, w), lambda j, b, t: (rt(b, t), off + j))

    pair = pl.BlockSpec((2, n, w), lambda j, b, t: (0, rt(b, t), j))
    return pl.pallas_call(
        body, grid=(nj, nb, nt),
        in_specs=[ublk(0), pair, ublk(0), ublk(nj), pl.BlockSpec((2, 3, w), lambda j, b, t: (0, 0, j))],
        out_specs=[pair, pl.BlockSpec((2, 3, w), lambda j, b, t: (0, 0, j)),
                   pl.BlockSpec((2, 1, w), lambda j, b, t: (0, 0, j))],
        out_shape=[S((2, rows, D_FF), MXU_DT), S((2, 3, D_FF), F32), S((2, 1, D_FF), F32)],
        scratch_shapes=[pltpu.VMEM((n + h8, w), F32)] * 2,
        compiler_params=_cp(("arbitrary", "arbitrary", "arbitrary")), name=name,
    )(dy, u, up, up, cw)


def _lane_mod(shape):
    return lax.broadcasted_iota(jnp.int32, shape, 1) & (HP - 1)


def _q_rope_epi(acc, tab):
    reps = acc.shape[1] // HP
    a = acc * jnp.tile(tab, (1, reps))
    lane = _lane_mod(a.shape)
    shifted = pltpu.roll(a, a.shape[1] - QK_ROPE, axis=1)
    return jnp.where(lane < QK_NOPE, a, jnp.where(lane < QK_HEAD, a + shifted, 0.0)) * Q_PRESCALE


def _k_rope_block(krblk, tabk):
    a = krblk * tabk
    lane = _lane_mod(a.shape)
    b = a + pltpu.roll(a, HP - QK_ROPE, axis=1)
    return jnp.where((lane >= QK_NOPE) & (lane < QK_HEAD), b, 0.0)


def _k_rope_epi(acc, krblk, tabk):
    reps = acc.shape[1] // HP
    return acc + jnp.tile(_k_rope_block(krblk, tabk), (1, reps))


def attn_fwd(q, k, v, nb, tp, name):
    rows = q.shape[0]
    blk = ATT_BLK
    nq = tp // blk
    npair = MLA_HEADS // 2

    def body(q_r, k_r, v_r, o_r, lse_r):
        qi = pl.program_id(2)
        lane = lax.broadcasted_iota(jnp.int32, (blk, LANE), 1)
        even = lane < V_HEAD
        sum_lane = (V_HEAD, 0)
        rowi = lax.broadcasted_iota(jnp.int32, (blk, blk), 0)
        coli = lax.broadcasted_iota(jnp.int32, (blk, blk), 1)
        qs = [q_r[:, h * HP:(h + 1) * HP] for h in range(2)]

        def kv_block(k0, width, carry, diagonal):
            ms, accs = carry
            vblk = v_r[pl.ds(k0, width), :]
            one = jnp.ones_like(vblk)
            zero = jnp.zeros_like(vblk)
            vlane = lax.broadcasted_iota(jnp.int32, (width, LANE), 1)
            new_ms, new_accs = [], []
            for h in range(2):
                kh = k_r[pl.ds(k0, width), h * HP:(h + 1) * HP]
                s = lax.dot_general(qs[h], kh, NT_DIMS, preferred_element_type=F32)
                if diagonal:
                    s = jnp.where(coli <= rowi, s, -jnp.inf)
                m_new = jnp.maximum(ms[h], jnp.max(s, axis=1, keepdims=True))
                alpha = jnp.exp2(ms[h] - m_new)
                p = jnp.exp2(s - m_new).astype(MXU_DT)
                mine = (vlane < V_HEAD) if h == 0 else (vlane >= V_HEAD)
                vh = jnp.where(mine, vblk, jnp.where(vlane == sum_lane[h], one, zero))
                new_accs.append(alpha * accs[h] + jnp.dot(p, vh, preferred_element_type=F32))
                new_ms.append(m_new)
            return tuple(new_ms), tuple(new_accs)

        neg = jnp.full((blk, 1), -jnp.inf, F32)
        zacc = jnp.zeros((blk, LANE), F32)
        carry = lax.fori_loop(0, qi // 2, lambda i, c: kv_block(pl.multiple_of(i * 2 * blk, blk), 2 * blk, c, False),
                              ((neg, neg), (zacc, zacc)))
        carry = lax.cond(qi % 2 == 1, lambda c: kv_block(pl.multiple_of((qi - 1) * blk, blk), blk, c, False),
                         lambda c: c, carry)
        ms, accs = kv_block(pl.multiple_of(qi * blk, blk), blk, carry, True)
        ls = [accs[h][:, sum_lane[h]:sum_lane[h] + 1] for h in range(2)]
        o_r[...] = jnp.where(even, accs[0] / ls[0], accs[1] / ls[1]).astype(MXU_DT)
        lse_r[...] = jnp.where(even, ms[0] + jnp.log2(ls[0]), ms[1] + jnp.log2(ls[1]))

    return pl.pallas_call(
        body, grid=(nb, npair, nq),
        in_specs=[pl.BlockSpec((blk, 2 * HP), lambda b, p, i: (b * nq + i, p)),
                  pl.BlockSpec((tp, 2 * HP), lambda b, p, i: (b, p)),
                  pl.BlockSpec((tp, LANE), lambda b, p, i: (b, p))],
        out_specs=[pl.BlockSpec((blk, LANE), lambda b, p, i: (b * nq + i, p)),
                   pl.BlockSpec((None, blk, LANE), lambda b, p, i: (p, b * nq + i, 0))],
        out_shape=[S((rows, MLA_HEADS * V_HEAD), MXU_DT), S((npair, rows, LANE), F32)],
        compiler_params=_cp(("parallel", "parallel", "arbitrary")), name=name)(q, k, v)


def attn_bwd(q, k, v, o, do, lse, nb, tp, name):
    rows = q.shape[0]
    blk = ATT_BLK
    nq = tp // blk
    npair = MLA_HEADS // 2
    scale = QK_HEAD ** -0.5

    def body(q_r, k_r, v_r, o_r, do_r, lse_r, dq_o, dk_o, dv_o, dq_acc, delta_sc):
        kb = pl.program_id(2)
        even = lax.broadcasted_iota(jnp.int32, (blk, LANE), 1) < V_HEAD
        rowi = lax.broadcasted_iota(jnp.int32, (blk, blk), 0)
        coli = lax.broadcasted_iota(jnp.int32, (blk, blk), 1)

        @pl.when(kb == 0)
        def _():
            dq_acc[...] = jnp.zeros_like(dq_acc)

            def dstep(i, c):
                r0 = pl.multiple_of(i * blk, blk)
                prod = do_r[pl.ds(r0, blk), :].astype(F32) * o_r[pl.ds(r0, blk), :].astype(F32)
                de = jnp.sum(jnp.where(even, prod, 0.0), axis=1, keepdims=True)
                dd = jnp.sum(jnp.where(even, 0.0, prod), axis=1, keepdims=True)
                delta_sc[pl.ds(r0, blk), :] = jnp.where(even, de, dd)
                return c

            lax.fori_loop(0, nq, dstep, 0)

        vblk = v_r[...]
        ks = [k_r[:, h * HP:(h + 1) * HP] for h in range(2)]

        def q_block(r0, height, carry, diagonal):
            dk0, dk1, dv = carry
            dob = do_r[pl.ds(r0, height), :]
            lse_b = lse_r[pl.ds(r0, height), :]
            dl_b = delta_sc[pl.ds(r0, height), :]
            qlane = lax.broadcasted_iota(jnp.int32, (height, LANE), 1)
            dks = [dk0, dk1]
            for h in range(2):
                lo = 0 if h == 0 else V_HEAD
                qh = q_r[pl.ds(r0, height), h * HP:(h + 1) * HP]
                s = lax.dot_general(qh, ks[h], NT_DIMS, preferred_element_type=F32)
                p = jnp.exp2(s - lse_b[:, lo:lo + 1])
                if diagonal:
                    p = jnp.where(coli <= rowi, p, 0.0)
                mine = (qlane < V_HEAD) if h == 0 else (qlane >= V_HEAD)
                doh = jnp.where(mine, dob, jnp.zeros_like(dob))
                dp = lax.dot_general(doh, vblk, NT_DIMS, preferred_element_type=F32)
                ds = (p * (dp - dl_b[:, lo:lo + 1])).astype(MXU_DT)
                dv = dv + lax.dot_general(p.astype(MXU_DT), doh, TN_DIMS, preferred_element_type=F32)
                dks[h] = dks[h] + lax.dot_general(ds, qh, TN_DIMS, preferred_element_type=F32)
                dq_acc[pl.ds(r0, height), h * HP:(h + 1) * HP] += jnp.dot(ds, ks[h], preferred_element_type=F32)
            return dks[0], dks[1], dv

        z = jnp.zeros((blk, HP), F32)
        carry = q_block(pl.multiple_of(kb * blk, blk), blk, (z, z, jnp.zeros((blk, LANE), F32)), True)
        below = nq - 1 - kb
        carry = lax.fori_loop(
            0, below // 2, lambda i, c: q_block(pl.multiple_of((kb + 1 + 2 * i) * blk, blk), 2 * blk, c, False), carry)
        dk0, dk1, dv = lax.cond(below % 2 == 1, lambda c: q_block(pl.multiple_of((nq - 1) * blk, blk), blk, c, False),
                                lambda c: c, carry)
        dk_o[:, 0:HP] = (dk0 * (scale / Q_PRESCALE)).astype(MXU_DT)
        dk_o[:, HP:2 * HP] = (dk1 * (scale / Q_PRESCALE)).astype(MXU_DT)
        dv_o[...] = dv.astype(MXU_DT)

        @pl.when(kb == nq - 1)
        def _():
            dq_o[...] = (dq_acc[...] * scale).astype(MXU_DT)

    seq_pair = pl.BlockSpec((tp, LANE), lambda b, p, kk: (b, p))
    return pl.pallas_call(
        body, grid=(nb, npair, nq),
        in_specs=[pl.BlockSpec((tp, 2 * HP), lambda b, p, kk: (b, p)),
                  pl.BlockSpec((blk, 2 * HP), lambda b, p, kk: (b * nq + kk, p)),
                  pl.BlockSpec((blk, LANE), lambda b, p, kk: (b * nq + kk, p)),
                  seq_pair, seq_pair, pl.BlockSpec((None, tp, LANE), lambda b, p, kk: (p, b, 0))],
        out_specs=[pl.BlockSpec((tp, 2 * HP), lambda b, p, kk: (b, p)),
                   pl.BlockSpec((blk, 2 * HP), lambda b, p, kk: (b * nq + kk, p)),
                   pl.BlockSpec((blk, LANE), lambda b, p, kk: (b * nq + kk, p))],
        out_shape=[S((rows, MLA_HEADS * HP), MXU_DT), S((rows, MLA_HEADS * HP), MXU_DT),
                   S((rows, MLA_HEADS * V_HEAD), MXU_DT)],
        scratch_shapes=[pltpu.VMEM((tp, 2 * HP), F32), pltpu.VMEM((tp, LANE), F32)],
        compiler_params=_cp(("parallel", "parallel", "arbitrary")), name=name)(q, k, v, o, do, lse)


def rope_bwd(dq, dk, dv, tabq, tabk, tp, tm, name):
    rows = dq.shape[0]
    nt = tp // tm
    wq = MLA_HEADS * HP

    def body(dq_r, dk_r, dv_r, tq_r, tk_r, dqa_o, dkv_o, dkr_o):
        dqv = dq_r[...].astype(F32)
        lane = _lane_mod(dqv.shape)
        in_rope = (lane >= QK_NOPE) & (lane < QK_HEAD)
        rope = jnp.where(in_rope, dqv, 0.0)
        da = jnp.where(lane < QK_HEAD, dqv, 0.0) + pltpu.roll(rope, QK_ROPE, axis=1)
        dqa_o[...] = (da * jnp.tile(tq_r[...], (1, MLA_HEADS))).astype(MXU_DT)
        dkf = dk_r[...].astype(F32)
        dkv_o[:, 0:wq] = jnp.where(lane < QK_NOPE, dkf, 0.0).astype(MXU_DT)
        dkv_o[:, wq:] = dv_r[...]
        kr = jnp.where(in_rope, dkf, 0.0)
        tot = kr[:, 0:HP]
        for h in range(1, MLA_HEADS):
            tot = tot + kr[:, h * HP:(h + 1) * HP]
        dkr_o[...] = ((tot + pltpu.roll(tot, QK_ROPE, axis=1)) * tk_r[...]).astype(MXU_DT)

    def rowblk(wd):
        return pl.BlockSpec((tm, wd), lambda i: (i, 0))

    tab = pl.BlockSpec((tm, HP), lambda i: (i % nt, 0))
    return pl.pallas_call(
        body, grid=(rows // tm,), in_specs=[rowblk(wq), rowblk(wq), rowblk(MLA_HEADS * V_HEAD), tab, tab],
        out_specs=[rowblk(wq), rowblk(wq + MLA_HEADS * V_HEAD), rowblk(HP)],
        out_shape=[S((rows, wq), MXU_DT), S((rows, wq + MLA_HEADS * V_HEAD), MXU_DT), S((rows, HP), MXU_DT)],
        compiler_params=_cp(("parallel",)), name=name)(dq, dk, dv, tabq, tabk)


def loss_head(h, target, gain, tp, t_real, tm, name):
    rows = h.shape[0]
    nt = tp // tm

    def body(h_r, t_r, g_r, dh_o, loss_o, dg_o):
        i = pl.program_id(0)

        @pl.when(i == 0)
        def _():
            loss_o[...] = jnp.zeros_like(loss_o)
            dg_o[...] = jnp.zeros_like(dg_o)

        xv = h_r[...]
        rstd = lax.rsqrt(jnp.mean(xv * xv, axis=-1, keepdims=True) + EPS)
        xhat = xv * rstd
        g = g_r[...]
        pos = (i % nt) * tm + lax.broadcasted_iota(jnp.int32, (tm, 1), 0)
        valid = (pos >= N_META) & (pos < t_real)
        err = jnp.where(valid, xhat * g - t_r[...], 0.0)
        loss_o[...] += 0.5 * jnp.sum(jnp.mean(err * err, axis=-1, keepdims=True))
        dy = err * (1.0 / D_MODEL)
        dg_o[...] += jnp.sum(dy * xhat, axis=0, keepdims=True)
        dxh = dy * g
        dh_o[...] = rstd * (dxh - xhat * jnp.mean(dxh * xhat, axis=-1, keepdims=True))

    blk = pl.BlockSpec((tm, D_MODEL), lambda i: (i, 0))
    return pl.pallas_call(
        body, grid=(rows // tm,), in_specs=[blk, blk, pl.BlockSpec((1, D_MODEL), lambda i: (0, 0))],
        out_specs=[blk, pl.BlockSpec((1, LANE), lambda i: (0, 0)), pl.BlockSpec((1, D_MODEL), lambda i: (0, 0))],
        out_shape=[S((rows, D_MODEL), F32), S((1, LANE), F32), S((1, D_MODEL), F32)],
        compiler_params=_cp(("arbitrary",)), name=name)(h, target, gain)


ADAM_TILE_ELEMS = 128 * 1024


def adamw(g, w, m, v, name):
    shape = w.shape
    cols = shape[-1]
    rws = max(1, math.prod(shape[:-1]))
    tr = rws if rws * cols <= ADAM_TILE_ELEMS else _div_tile(rws, max(SUBLANE, ADAM_TILE_ELEMS // cols), SUBLANE)
    bc1 = 1.0 - ADAM_B1 ** ADAM_STEP
    bc2 = 1.0 - ADAM_B2 ** ADAM_STEP

    def body(g_r, w_r, m_r, v_r, go, do, mo, vo):
        gv = g_r[...]
        mn = ADAM_B1 * m_r[...] + (1.0 - ADAM_B1) * gv
        vn = ADAM_B2 * v_r[...] + (1.0 - ADAM_B2) * (gv * gv)
        m_hat = mn / bc1
        v_hat = vn / bc2
        go[...] = gv
        do[...] = -ADAM_LR * (m_hat / (jnp.sqrt(v_hat) + ADAM_EPS) + ADAM_WD * w_r[...])
        mo[...] = mn
        vo[...] = vn

    blk = pl.BlockSpec((tr, cols), lambda i: (i, 0))
    outs = pl.pallas_call(
        body, grid=(rws // tr,), in_specs=[blk] * 4, out_specs=[blk] * 4, out_shape=[S((rws, cols), F32)] * 4,
        compiler_params=_cp(("parallel",)), name=name,
    )(*[a.reshape(rws, cols) for a in (g, w, m, v)])
    return tuple(o.reshape(shape) for o in outs)


HBM_SPEC = pl.BlockSpec(memory_space=pltpu.HBM)
SUM_ROWS = 2048


def _place():
    return lax.axis_index("x"), lax.axis_index("y"), lax.axis_index("c")


def _remote(src, dst, send_sems, recv_sems, k, to):
    return pltpu.make_async_remote_copy(src_ref=src, dst_ref=dst, send_sem=send_sems.at[k], recv_sem=recv_sems.at[k],
                                        device_id=to, device_id_type=MESH)


def all_gather_chips(xs, name):
    _, rws, _ = xs.shape

    def body(x_ref, out_ref, send_sems, recv_sems, local_sem):
        mx, my, mc = _place()
        sibling = (mx, my, 1 - mc)
        chips = [(1 - mx, my), (mx, 1 - my), (1 - mx, 1 - my)]

        def piece(cx, cy, h):
            return out_ref.at[2 * cx + cy, h]

        mine = pltpu.make_async_copy(x_ref, out_ref.at[2 * mx + my], local_sem)
        mine.start()
        first = [_remote(x_ref.at[mc], piece(mx, my, mc), send_sems, recv_sems, j, (cx, cy, mc))
                 for j, (cx, cy) in enumerate(chips)]
        for cp in first:
            cp.start()
        passed = [_remote(piece(cx, cy, mc), piece(cx, cy, mc), send_sems, recv_sems, 3 + j, sibling)
                  for j, (cx, cy) in enumerate(chips)]
        for j, (cx, cy) in enumerate(chips):
            _remote(x_ref.at[mc], piece(cx, cy, mc), send_sems, recv_sems, j, (cx, cy, mc)).wait_recv()
            passed[j].start()
        for j, (cx, cy) in enumerate(chips):
            _remote(x_ref.at[mc], piece(cx, cy, 1 - mc), send_sems, recv_sems, 3 + j, sibling).wait_recv()
        for cp in first + passed:
            cp.wait_send()
        mine.wait()

    return pl.pallas_call(
        body, out_shape=S((4, 2, rws, LANE), xs.dtype), in_specs=[HBM_SPEC], out_specs=HBM_SPEC,
        scratch_shapes=[pltpu.SemaphoreType.DMA((6,)), pltpu.SemaphoreType.DMA((6,)), pltpu.SemaphoreType.DMA],
        name=name)(xs)


def pair_exchange_halves(g4, name):
    _, _, rws, _ = g4.shape

    def body(g_ref, land_ref, send_sems, recv_sems):
        mx, my, mc = _place()
        sibling = (mx, my, 1 - mc)
        cps = [_remote(g_ref.at[s, 1 - mc], land_ref.at[s], send_sems, recv_sems, s, sibling) for s in range(4)]
        for cp in cps:
            cp.start()
        for cp in cps:
            cp.wait_recv()
        for cp in cps:
            cp.wait_send()

    return pl.pallas_call(
        body, out_shape=S((4, rws, LANE), g4.dtype), in_specs=[HBM_SPEC], out_specs=HBM_SPEC,
        scratch_shapes=[pltpu.SemaphoreType.DMA((4,)), pltpu.SemaphoreType.DMA((4,))], name=name)(g4)


def pair_sum(g4, land, c_idx, name):
    _, _, rws, _ = g4.shape
    th = SUM_ROWS

    def body(c_ref, a_ref, b_ref, o_ref):
        o_ref[...] = a_ref[...] + b_ref[...]

    return pl.pallas_call(
        body,
        grid_spec=pltpu.PrefetchScalarGridSpec(
            num_scalar_prefetch=1, grid=(4, rws // th),
            in_specs=[pl.BlockSpec((None, None, th, LANE), lambda s, i, c: (s, c[0], i, 0)),
                      pl.BlockSpec((None, th, LANE), lambda s, i, c: (s, i, 0))],
            out_specs=pl.BlockSpec((None, th, LANE), lambda s, i, c: (s, i, 0))),
        out_shape=S((4, rws, LANE), F32), compiler_params=_cp(("parallel", "parallel")), name=name)(c_idx, g4, land)


def chip_scatter(p4, name):
    _, rws, _ = p4.shape

    def body(p_ref, land_ref, send_sems, recv_sems, local_sem):
        mx, my, mc = _place()
        me = 2 * mx + my
        chips = [(1 - mx, my), (mx, 1 - my), (1 - mx, 1 - my)]
        mine = pltpu.make_async_copy(p_ref.at[me], land_ref.at[me], local_sem)
        mine.start()
        cps = [_remote(p_ref.at[2 * cx + cy], land_ref.at[me], send_sems, recv_sems, j, (cx, cy, mc))
               for j, (cx, cy) in enumerate(chips)]
        for cp in cps:
            cp.start()
        for j, (cx, cy) in enumerate(chips):
            _remote(p_ref.at[me], land_ref.at[2 * cx + cy], send_sems, recv_sems, j, (cx, cy, mc)).wait_recv()
        for cp in cps:
            cp.wait_send()
        mine.wait()

    return pl.pallas_call(
        body, out_shape=S((4, rws, LANE), p4.dtype), in_specs=[HBM_SPEC], out_specs=HBM_SPEC,
        scratch_shapes=[pltpu.SemaphoreType.DMA((3,)), pltpu.SemaphoreType.DMA((3,)), pltpu.SemaphoreType.DMA],
        name=name)(p4)


def chip_sum(l4, name):
    _, rws, _ = l4.shape
    th = SUM_ROWS

    def body(a, b, c, d, o_ref):
        o_ref[...] = ((a[...] + b[...]) + c[...]) + d[...]

    def blk(s):
        return pl.BlockSpec((None, th, LANE), lambda i: (s, i, 0))

    return pl.pallas_call(
        body, grid=(rws // th,), in_specs=[blk(0), blk(1), blk(2), blk(3)],
        out_specs=pl.BlockSpec((th, LANE), lambda i: (i, 0)), out_shape=S((rws, LANE), F32),
        compiler_params=_cp(("parallel",)), name=name)(l4, l4, l4, l4)


def pair_gather(rh, name):
    rws, _ = rh.shape

    def body(r_ref, out_ref, send_sems, recv_sems, local_sem):
        mx, my, mc = _place()
        sibling = (mx, my, 1 - mc)
        mine = pltpu.make_async_copy(r_ref, out_ref.at[mc], local_sem)
        mine.start()
        cp = _remote(r_ref, out_ref.at[mc], send_sems, recv_sems, 0, sibling)
        cp.start()
        _remote(r_ref, out_ref.at[1 - mc], send_sems, recv_sems, 0, sibling).wait_recv()
        cp.wait_send()
        mine.wait()

    return pl.pallas_call(
        body, out_shape=S((2, rws, LANE), rh.dtype), in_specs=[HBM_SPEC], out_specs=HBM_SPEC,
        scratch_shapes=[pltpu.SemaphoreType.DMA((1,)), pltpu.SemaphoreType.DMA((1,)), pltpu.SemaphoreType.DMA],
        name=name)(rh)


PACK_ELEMS = 16 * LANE


def pack_rows(arrays, lead, total_mult):
    parts, offs, r0 = [], [], 0
    for a in arrays:
        flat = a.reshape(a.shape[:lead] + (-1,))
        elems = _round_up(flat.shape[-1], PACK_ELEMS)
        flat = jnp.pad(flat, [(0, 0)] * lead + [(0, elems - flat.shape[-1])])
        parts.append(flat.reshape(flat.shape[:lead] + (elems // LANE, LANE)))
        offs.append((r0, elems // LANE))
        r0 += elems // LANE
    total = _round_up(r0, total_mult)
    if total > r0:
        parts.append(jnp.zeros(parts[0].shape[:lead] + (total - r0, LANE), parts[0].dtype))
    return jnp.concatenate(parts, axis=lead), offs


def unpack_rows(buf, off, shape):
    r0, nr = off
    lead = buf.shape[:-2]
    n = math.prod(shape)
    return buf[..., r0:r0 + nr, :].reshape(lead + (nr * LANE,))[..., :n].reshape(lead + tuple(shape))


def unshard(stacked, axis):
    x = jnp.moveaxis(stacked, 0, axis)
    return x.reshape(x.shape[:axis] + (4 * x.shape[axis + 1],) + x.shape[axis + 2:])


def to_shards(full, axis):
    n = full.shape[axis] // 4
    x = full.reshape(full.shape[:axis] + (4, n) + full.shape[axis + 1:])
    return jnp.moveaxis(x, axis, 0)


def _rot_cols(w):
    half = w.shape[-1] // 2
    return jnp.concatenate([-w[..., half:], w[..., :half]], axis=-1)


def _unrot_cols(dw):
    half = dw.shape[-1] // 2
    return jnp.concatenate([dw[..., half:], -dw[..., :half]], axis=-1)


def odd_w_in_padded(w_in):
    kr = w_in[:, Q_LORA + KV_LORA:]
    rows = w_in.shape[0]
    return jnp.concatenate([w_in[:, :Q_LORA], jnp.zeros((rows, 128), w_in.dtype), w_in[:, Q_LORA:Q_LORA + KV_LORA],
                            jnp.zeros((rows, 64), w_in.dtype), kr, _rot_cols(kr)], axis=1)


def odd_w_in_unpad(dwp):
    base = 512 + KV_LORA + 64
    dkr = dwp[:, base:base + QK_ROPE] + _unrot_cols(dwp[:, base + QK_ROPE:base + 2 * QK_ROPE])
    return jnp.concatenate([dwp[:, :Q_LORA], dwp[:, 512:512 + KV_LORA], dkr], axis=1)


def uq_padded(w_uq):
    w = w_uq.reshape(Q_LORA, MLA_HEADS, QK_HEAD)
    return jnp.concatenate([w, _rot_cols(w[:, :, QK_NOPE:])], axis=-1).reshape(Q_LORA, MLA_HEADS * HP)


def uq_unpad(dwp):
    d = dwp.reshape(Q_LORA, MLA_HEADS, HP)
    rope = d[:, :, QK_NOPE:QK_HEAD] + _unrot_cols(d[:, :, QK_HEAD:])
    return jnp.concatenate([d[:, :, :QK_NOPE], rope], axis=-1).reshape(Q_LORA, MLA_HEADS * QK_HEAD)


def ukv_padded(w_ukv):
    w = w_ukv.reshape(KV_LORA, MLA_HEADS, QK_NOPE + V_HEAD)
    wk = jnp.concatenate([w[:, :, :QK_NOPE], jnp.zeros((KV_LORA, MLA_HEADS, HP - QK_NOPE), w.dtype)], axis=-1)
    return jnp.concatenate([wk.reshape(KV_LORA, MLA_HEADS * HP), w[:, :, QK_NOPE:].reshape(KV_LORA, MLA_HEADS * V_HEAD)],
                           axis=1)


def ukv_unpad(dwp):
    dk = dwp[:, :MLA_HEADS * HP].reshape(KV_LORA, MLA_HEADS, HP)[:, :, :QK_NOPE]
    dv = dwp[:, MLA_HEADS * HP:].reshape(KV_LORA, MLA_HEADS, V_HEAD)
    return jnp.concatenate([dk, dv], axis=-1).reshape(KV_LORA, MLA_HEADS * (QK_NOPE + V_HEAD))


def block_diag(w):
    h, d, _ = w.shape
    eye = jnp.eye(h, dtype=w.dtype)
    return (eye[:, None, :, None] * w[:, :, None, :]).reshape(h * d, h * d)


def block_diag_part(dense, h):
    d = dense.shape[0] // h
    x = dense.reshape(h, d, h, d)
    return jnp.stack([x[i, :, i, :] for i in range(h)], axis=0)


def rope_tables(tp):
    pos = jnp.arange(tp, dtype=F32)
    inv_freq = ROPE_BASE ** (-jnp.arange(0, QK_ROPE, 2, dtype=F32) / QK_ROPE)
    ang = pos[:, None] * inv_freq[None, :]
    cos2 = jnp.tile(jnp.cos(ang), (1, 2))
    sin2 = jnp.tile(jnp.sin(ang), (1, 2))
    tabq = jnp.concatenate([jnp.ones((tp, QK_NOPE), F32), cos2, sin2], axis=1)
    tabk = jnp.concatenate([jnp.zeros((tp, QK_NOPE), F32), cos2, sin2], axis=1)
    return tabq, tabk


class Dims:
    def __init__(self, nb, seq):
        self.nb = nb
        self.t_real = seq + N_META
        self.tp = _round_up(self.t_real, ATT_BLK)
        self.n = self.tp // 4
        assert self.n % 16 == 0
        self.rows = nb * self.tp


def even_fwd(h, p, dm):
    u, hn = norm_matmul(h, 0, D_MODEL, p["norm"], p["w_in"], dm.n, 512, F32, "ev_in")
    y, ca, xc, a, hs = even_mid_fwd(u, p["conv_a"], p["conv_b"], p["conv_b_bias"], p["rw"], p["r_b"], p["iw"], p["i_b"],
                                    p["lam"], dm.nb, dm.tp, dm.n, "ev_mid")
    out = matmul_res(y, p["w_out"].reshape(2, CONV_W, D_MODEL), h, dm.n, 512, "ev_out")
    return out, (h, u, hn, ca, xc, a, hs, y)


def even_bwd(dout, saved, p, dm):
    h, u, hn, ca, xc, a, hs, y = saved
    g = {}
    dycat = matmul_nt(dout, p["w_out"], dm.n, 512, F32, "ev_dycat")
    g["w_out"] = matmul_tn(y, dout, dm.n, "ev_dw_out")
    outs = even_mid_bwd(u, dycat, ca, xc, a, hs, p["conv_a"], p["conv_b"], p["rw"], p["r_b"], p["iw"], p["i_b"],
                        p["lam"], dm.nb, dm.tp, dm.n, "ev_mid_bwd")
    du, g["conv_a"], g["conv_b"], g["conv_b_bias"], drw, g["r_b"], diw, g["i_b"], g["lam"] = outs
    g["r_w"] = block_diag_part(drw, LRU_HEADS)
    g["i_w"] = block_diag_part(diw, LRU_HEADS)
    g["w_in"] = matmul_tn(hn, du, dm.n, "ev_dw_in")
    dx, g["norm"] = matmul_nt_normbwd(du, p["w_in"], h, 0, p["norm"], dout, dm.n, 512, F32, "ev_dx")
    return dx, g


def odd_fwd(h, p, tabq, tabk, dm):
    nt = dm.tp // dm.n
    u, hn = norm_matmul(h, 0, D_MODEL, p["norm"], p["w_in_p"], dm.n, ODD_PAD, F32, "od_in")
    tab_spec = pl.BlockSpec((dm.n, HP), lambda i, j: (i % nt, 0))
    q, cqn = norm_matmul(u, 0, Q_LORA, p["q_norm"], p["w_uq_p"], dm.n, 512, MXU_DT, "od_q",
                         epi=_q_rope_epi, epi_ops=(tabq,), epi_specs=(tab_spec,))
    kr_spec = pl.BlockSpec((dm.n, HP), lambda i, j: (i, ODD_KR_COL))
    k, ckvn = norm_matmul(u, ODD_CKV_COL, KV_LORA, p["kv_norm"], p["w_uk_p"], dm.n, 512, MXU_DT, "od_k",
                          epi=_k_rope_epi, epi_ops=(u, tabk), epi_specs=(kr_spec, tab_spec))
    v, _ = norm_matmul(u, ODD_CKV_COL, KV_LORA, p["kv_norm"], p["w_uv_p"], dm.n, 512, MXU_DT, "od_v")
    o, lse = attn_fwd(q, k, v, dm.nb, dm.tp, "od_attn")
    out = matmul_res(o[None], p["w_out"][None], h, dm.n, 512, "od_out")
    return out, (h, u, hn, cqn, ckvn, q, k, v, o, lse)


def odd_bwd(dout, saved, p, tabq, tabk, dm):
    h, u, hn, cqn, ckvn, q, k, v, o, lse = saved
    g = {}
    do = matmul_nt(dout, p["w_out"], dm.n, 512, MXU_DT, "od_do")
    g["w_out"] = matmul_tn(o, dout, dm.n, "od_dw_out")
    dq, dk, dv = attn_bwd(q, k, v, o, do, lse, dm.nb, dm.tp, "od_attn_bwd")
    dqa, dkv, dkr = rope_bwd(dq, dk, dv, tabq, tabk, dm.tp, dm.n, "od_rope_bwd")
    g["w_uq_p"] = matmul_tn(cqn, dqa, dm.n, "od_dw_uq")
    g["w_ukv_p"] = matmul_tn(ckvn, dkv, dm.n, "od_dw_ukv")
    dcq, g["q_norm"] = matmul_nt_normbwd(dqa, p["w_uq_p"], u, 0, p["q_norm"], None, dm.n, 512, MXU_DT, "od_dcq")
    dckv, g["kv_norm"] = matmul_nt_normbwd(dkv, p["w_ukv_p"], u, ODD_CKV_COL, p["kv_norm"], None, dm.n, 512, MXU_DT,
                                           "od_dckv")
    du = jnp.concatenate([dcq, jnp.zeros((dm.rows, 128), MXU_DT), dckv, dkr], axis=1)
    g["w_in_p"] = matmul_tn(hn, du, dm.n, "od_dw_in")
    dx, g["norm"] = matmul_nt_normbwd(du, p["w_in_p"], h, 0, p["norm"], dout, dm.n, ODD_PAD, F32, "od_dx")
    return dx, g


def ffn_fwd(h, p, dm):
    up, hn = norm_matmul(h, 0, D_MODEL, p["norm"], p["w_up"], dm.n, 512, MXU_DT, "ffn_up")
    u, y = ffn_mid_fwd(up, p["cw"], p["cb"], dm.nb, dm.tp, dm.n, "ffn_mid")
    out = matmul_res(y[None], p["w_down"][None], h, dm.n, 512, "ffn_down")
    return out, (h, up, hn, u, y)


def ffn_bwd(dout, saved, p, dm):
    h, up, hn, u, y = saved
    g = {}
    dy = matmul_nt(dout, p["w_down"], dm.n, D_FF // 2, MXU_DT, "ffn_dy")
    g["w_down"] = matmul_tn(y, dout, dm.n, "ffn_dw_down")
    dup, g["cw"], g["cb"] = ffn_mid_bwd(dy, u, up, p["cw"], dm.nb, dm.tp, dm.n, "ffn_mid_bwd")
    g["w_up"] = matmul_tn(hn, dup, dm.n, "ffn_dw_up")
    dx, g["norm"] = matmul_nt_normbwd(dup, p["w_up"], h, 0, p["norm"], dout, dm.n, D_FF // 2, F32, "ffn_dx")
    return dx, g


def _row(v):
    return v.reshape(1, -1)


def local_step(x, target, wf):
    nb, seq, _ = x.shape
    dm = Dims(nb, seq)
    tail = dm.tp - dm.t_real
    meta = jnp.broadcast_to(wf["meta_tokens"][None], (nb, N_META, D_MODEL))
    h = jnp.concatenate([meta, x, jnp.zeros((nb, tail, D_MODEL), F32)], axis=1).reshape(dm.rows, D_MODEL)
    tgt = jnp.pad(target, ((0, 0), (N_META, tail), (0, 0))).reshape(dm.rows, D_MODEL)
    tabq, tabk = rope_tables(dm.tp)

    ev, od, ffn = [], [], []
    for j in range(DEPTH // 2):
        ev.append(dict(norm=_row(wf["ev_norm"][j]), w_in=wf["ev_w_in"][j], conv_a=wf["ev_conv_a"][j],
                       conv_b=wf["ev_conv_b"][j], conv_b_bias=_row(wf["ev_conv_b_bias"][j]),
                       rw=block_diag(wf["ev_gate_r_w"][j]).astype(MXU_DT), r_b=_row(wf["ev_gate_r_b"][j]),
                       iw=block_diag(wf["ev_gate_i_w"][j]).astype(MXU_DT), i_b=_row(wf["ev_gate_i_b"][j]),
                       lam=_row(wf["ev_lru_lambda"][j]), w_out=wf["ev_w_out"][j]))
        wkv = ukv_padded(wf["od_w_ukv"][j])
        od.append(dict(norm=_row(wf["od_norm"][j]), w_in_p=odd_w_in_padded(wf["od_w_in"][j]),
                       q_norm=_row(wf["od_q_norm"][j]), kv_norm=_row(wf["od_kv_norm"][j]),
                       w_uq_p=uq_padded(wf["od_w_uq"][j]), w_ukv_p=wkv, w_uk_p=wkv[:, :MLA_HEADS * HP],
                       w_uv_p=wkv[:, MLA_HEADS * HP:], w_out=wf["od_w_out"][j]))
    for layer in range(DEPTH):
        ffn.append(dict(norm=_row(wf["ffn_norm"][layer]), w_up=wf["ffn_w_up"][layer],
                        cw=jnp.moveaxis(wf["ffn_conv_w"][layer].reshape(3, 2, D_FF), 1, 0),
                        cb=wf["ffn_conv_b"][layer].reshape(2, 1, D_FF), w_down=wf["ffn_w_down"][layer]))

    saved = []
    for layer in range(DEPTH):
        if layer % 2 == 0:
            h, sv = even_fwd(h, ev[layer // 2], dm)
        else:
            h, sv = odd_fwd(h, od[layer // 2], tabq, tabk, dm)
        saved.append(sv)
        h, sv = ffn_fwd(h, ffn[layer], dm)
        saved.append(sv)

    dh, loss, dfinal = loss_head(h, tgt, _row(wf["final_norm"]), dm.tp, dm.t_real, dm.n, "loss_head")

    gl = {k_: [None] * (DEPTH // 2) for k_ in
          ("ev_norm", "ev_w_in", "ev_conv_a", "ev_conv_b", "ev_conv_b_bias", "ev_gate_r_w", "ev_gate_r_b",
           "ev_gate_i_w", "ev_gate_i_b", "ev_lru_lambda", "ev_w_out", "od_norm", "od_w_in", "od_q_norm",
           "od_kv_norm", "od_w_uq", "od_w_ukv", "od_w_out")}
    gl.update({k_: [None] * DEPTH for k_ in ("ffn_norm", "ffn_w_up", "ffn_conv_w", "ffn_conv_b", "ffn_w_down")})
    for layer in reversed(range(DEPTH)):
        dh, g = ffn_bwd(dh, saved[2 * layer + 1], ffn[layer], dm)
        gl["ffn_norm"][layer] = g["norm"][0]
        gl["ffn_w_up"][layer] = g["w_up"]
        gl["ffn_conv_w"][layer] = jnp.moveaxis(g["cw"], 0, 1).reshape(3, 2 * D_FF)
        gl["ffn_conv_b"][layer] = g["cb"].reshape(2 * D_FF)
        gl["ffn_w_down"][layer] = g["w_down"]
        j = layer // 2
        if layer % 2 == 0:
            dh, g = even_bwd(dh, saved[2 * layer], ev[j], dm)
            for src, dst in (("norm", "ev_norm"), ("conv_b_bias", "ev_conv_b_bias"), ("r_b", "ev_gate_r_b"),
                             ("i_b", "ev_gate_i_b"), ("lam", "ev_lru_lambda")):
                gl[dst][j] = g[src][0]
            for src, dst in (("w_in", "ev_w_in"), ("conv_a", "ev_conv_a"), ("conv_b", "ev_conv_b"),
                             ("r_w", "ev_gate_r_w"), ("i_w", "ev_gate_i_w"), ("w_out", "ev_w_out")):
                gl[dst][j] = g[src]
        else:
            dh, g = odd_bwd(dh, saved[2 * layer], od[j], tabq, tabk, dm)
            gl["od_norm"][j] = g["norm"][0]
            gl["od_q_norm"][j] = g["q_norm"][0]
            gl["od_kv_norm"][j] = g["kv_norm"][0]
            gl["od_w_in"][j] = odd_w_in_unpad(g["w_in_p"])
            gl["od_w_uq"][j] = uq_unpad(g["w_uq_p"])
            gl["od_w_ukv"][j] = ukv_unpad(g["w_ukv_p"])
            gl["od_w_out"][j] = g["w_out"]
    grads = {k_: jnp.stack(v_, axis=0) for k_, v_ in gl.items()}
    grads["final_norm"] = dfinal[0]
    dh3 = dh.reshape(nb, dm.tp, D_MODEL)
    grads["meta_tokens"] = jnp.sum(dh3[:, :N_META], axis=0)
    return loss[0, 0], dh3[:, N_META:dm.t_real], grads


WEIGHTS = ["meta_tokens", "ev_norm", "ev_w_in", "ev_conv_a", "ev_conv_b", "ev_conv_b_bias", "ev_gate_r_w", "ev_gate_r_b",
           "ev_gate_i_w", "ev_gate_i_b", "ev_lru_lambda", "ev_w_out", "od_norm", "od_w_in", "od_q_norm", "od_kv_norm",
           "od_w_uq", "od_w_ukv", "od_w_out", "ffn_norm", "ffn_w_up", "ffn_conv_w", "ffn_conv_b", "ffn_w_down",
           "final_norm"]
SHARD_AXIS = {"meta_tokens": 1, "ev_w_in": 2, "ev_conv_a": 2, "ev_conv_b": 2, "ev_w_out": 1, "od_norm": 1, "od_w_in": 1,
              "od_q_norm": 1, "od_kv_norm": 1, "od_w_uq": 2, "od_w_ukv": 2, "od_w_out": 1, "ffn_w_up": 2,
              "ffn_conv_w": 2, "ffn_w_down": 1}
MATMUL_WEIGHTS = ["ev_w_in", "ev_w_out", "od_w_in", "od_w_uq", "od_w_ukv", "od_w_out", "ffn_w_up", "ffn_w_down"]


def gather_weights(w):
    full = {n: w[n] for n in WEIGHTS if n not in SHARD_AXIS}
    groups = ((MATMUL_WEIGHTS, MXU_DT, "gather_matmul_weights"),
              ([n for n in SHARD_AXIS if n not in MATMUL_WEIGHTS], F32, "gather_small_weights"))
    for names, dt, call_name in groups:
        buf, offs = pack_rows([w[n].astype(dt) for n in names], 0, 32)
        got = all_gather_chips(buf.reshape(2, buf.shape[0] // 2, LANE), call_name)
        got = got.reshape(4, buf.shape[0], LANE)
        for n, off in zip(names, offs):
            full[n] = unshard(unpack_rows(got, off, w[n].shape), SHARD_AXIS[n])
    return full


def reduce_gradients(grads, shard_shapes):
    sharded = [n for n in WEIGHTS if n in SHARD_AXIS]
    repl = [n for n in WEIGHTS if n not in SHARD_AXIS]
    gs, offs_s = pack_rows([to_shards(grads[n], SHARD_AXIS[n]) for n in sharded], 1, 16)
    gr, offs_r = pack_rows([grads[n] for n in repl], 0, 64)
    rs, rr = gs.shape[1] // 2, gr.shape[0] // 8
    rows = _round_up(rs + rr, SUM_ROWS)
    parts = [gs.reshape(4, 2, rs, LANE), gr.reshape(4, 2, rr, LANE)]
    if rows > rs + rr:
        parts.append(jnp.zeros((4, 2, rows - rs - rr, LANE), F32))
    g4 = jnp.concatenate(parts, axis=2)
    c_idx = lax.axis_index("c").astype(jnp.int32).reshape(1)
    land = pair_exchange_halves(g4, "grad_pair_exchange")
    part = pair_sum(g4, land, c_idx, "grad_pair_sum")
    land4 = chip_scatter(part, "grad_chip_scatter")
    mine = chip_sum(land4, "grad_chip_sum")
    both = pair_gather(mine, "grad_pair_gather")
    out = {}
    flat_s = both[:, :rs].reshape(2 * rs, LANE)
    for n, off in zip(sharded, offs_s):
        out[n] = unpack_rows(flat_s, off, shard_shapes[n])
    tails = all_gather_chips(both[:, rs:rs + rr], "grad_gather_replicated").reshape(8 * rr, LANE)
    for n, off in zip(repl, offs_r):
        out[n] = unpack_rows(tails, off, shard_shapes[n])
    return out


def kernel(x, meta_tokens, ev_norm, ev_w_in, ev_conv_a, ev_conv_b, ev_conv_b_bias, ev_gate_r_w, ev_gate_r_b, ev_gate_i_w, ev_gate_i_b, ev_lru_lambda, ev_w_out, od_norm, od_w_in, od_q_norm, od_kv_norm, od_w_uq, od_w_ukv, od_w_out, ffn_norm, ffn_w_up, ffn_conv_w, ffn_conv_b, ffn_w_down, final_norm, loss_target, m_meta_tokens, m_ev_norm, m_ev_w_in, m_ev_conv_a, m_ev_conv_b, m_ev_conv_b_bias, m_ev_gate_r_w, m_ev_gate_r_b, m_ev_gate_i_w, m_ev_gate_i_b, m_ev_lru_lambda, m_ev_w_out, m_od_norm, m_od_w_in, m_od_q_norm, m_od_kv_norm, m_od_w_uq, m_od_w_ukv, m_od_w_out, m_ffn_norm, m_ffn_w_up, m_ffn_conv_w, m_ffn_conv_b, m_ffn_w_down, m_final_norm, v_meta_tokens, v_ev_norm, v_ev_w_in, v_ev_conv_a, v_ev_conv_b, v_ev_conv_b_bias, v_ev_gate_r_w, v_ev_gate_r_b, v_ev_gate_i_w, v_ev_gate_i_b, v_ev_lru_lambda, v_ev_w_out, v_od_norm, v_od_w_in, v_od_q_norm, v_od_kv_norm, v_od_w_uq, v_od_w_ukv, v_od_w_out, v_ffn_norm, v_ffn_w_up, v_ffn_conv_w, v_ffn_conv_b, v_ffn_w_down, v_final_norm):
    given = dict(locals())
    w = {n: given[n] for n in WEIGHTS}
    full = gather_weights(w)
    loss, grad_x, grads = local_step(x, loss_target, full)
    loss = lax.psum(loss, ("x", "y", "c"))
    red = reduce_gradients(grads, {n: w[n].shape for n in WEIGHTS})
    g_out, d_out, m_out, v_out = [], [], [], []
    for n in WEIGHTS:
        g, d, m, v = adamw(red[n], w[n], given["m_" + n], given["v_" + n], "adamw_" + n)
        g_out.append(g)
        d_out.append(d)
        m_out.append(m)
        v_out.append(v)
    return (loss, grad_x, *g_out, *d_out, *m_out, *v_out)
```

```python
import math

import jax
import jax.numpy as jnp
from jax import lax
from jax.experimental import pallas as pl
from jax.experimental.pallas import tpu as pltpu

F32 = jnp.float32
MXU_DT = jnp.bfloat16
S = jax.ShapeDtypeStruct
MESH = pl.DeviceIdType.MESH

EPS = 1e-6
D_MODEL = 1024
N_META = 16
DEPTH = 4
CONV_W = 512
LRU_W = 512
LRU_HEADS = 8
LRU_C = 8.0
EVEN_IN = 2560
MLA_HEADS = 16
QK_NOPE = 64
QK_ROPE = 32
QK_HEAD = 96
V_HEAD = 64
Q_LORA = 384
KV_LORA = 256
ROPE_BASE = 10000.0
D_FF = 2816
ODD_PAD = 896
ODD_CKV_COL = 2
ODD_KR_COL = 6
HP = 128
ATT_BLK = 384
Q_PRESCALE = QK_HEAD ** -0.5 * math.log2(math.e)
FFN_CT = 256
LANE = 128
SUBLANE = 8
VMEM_LIMIT_MB = 52

ADAM_LR = 0.001
ADAM_B1 = 0.9
ADAM_B2 = 0.999
ADAM_EPS = 1e-08
ADAM_WD = 0.01
ADAM_STEP = 10

NT_DIMS = (((1,), (1,)), ((), ()))
TN_DIMS = (((0,), (0,)), ((), ()))


def _cp(sem):
    return pltpu.CompilerParams(dimension_semantics=sem, vmem_limit_bytes=VMEM_LIMIT_MB << 20)


def _div_tile(n, cap, mult):
    if n <= cap:
        return n
    best = None
    for t in range(mult, cap + 1, mult):
        if n % t == 0:
            best = t
    assert best is not None, (n, cap, mult)
    return best


def _round_up(n, m):
    return -(-n // m) * m


def mat_cols(arr):
    return arr.shape[1] if arr.ndim == 2 else arr.shape[0] * arr.shape[2]


def mat_width(arr):
    return arr.shape[-1]


def mat_spec(arr, tm, tw, rc):
    if arr.ndim == 2:
        return pl.BlockSpec((tm, tw), lambda *g: rc(*g))
    per = arr.shape[2] // tw
    assert arr.shape[2] % tw == 0

    def imap(*g):
        r, c = rc(*g)
        return (c // per, r, c % per)

    return pl.BlockSpec((None, tm, tw), imap)


HBM_SPEC = pl.BlockSpec(memory_space=pltpu.HBM)


class Stage:
    def __init__(self, inputs, out_shapes, sems, start, finish):
        self.inputs, self.out_shapes, self.sems, self.start, self.finish = inputs, out_shapes, sems, start, finish


def run_stage(stage, name):
    n_in, n_out = len(stage.inputs), len(stage.out_shapes)

    def body(*refs):
        ins, outs, sems = refs[:n_in], refs[n_in:n_in + n_out], refs[n_in + n_out:]
        stage.start(ins, outs, sems)
        stage.finish(ins, outs, sems)

    return pl.pallas_call(body, out_shape=list(stage.out_shapes), in_specs=[HBM_SPEC] * n_in,
                          out_specs=[HBM_SPEC] * n_out, scratch_shapes=list(stage.sems), name=name)(*stage.inputs)


def _call(body, ops, carry, *, grid, in_specs, out_specs, out_shape, scratch_shapes, sem, name):
    if carry is None:
        outs = pl.pallas_call(body, grid=grid, in_specs=in_specs, out_specs=out_specs, out_shape=out_shape,
                              scratch_shapes=scratch_shapes, compiler_params=_cp(sem), name=name)(*ops)
        return outs, None
    multi = isinstance(out_shape, (list, tuple))
    shapes = list(out_shape) if multi else [out_shape]
    ospecs = list(out_specs) if multi else [out_specs]
    n_in, n_out, n_sc = len(ops), len(shapes), len(scratch_shapes)
    c_in, c_out = len(carry.inputs), len(carry.out_shapes)

    def wrapped(*refs):
        ins, cin = refs[:n_in], refs[n_in:n_in + c_in]
        o0 = n_in + c_in
        outs, cout = refs[o0:o0 + n_out], refs[o0 + n_out:o0 + n_out + c_out]
        s0 = o0 + n_out + c_out
        scs, csems = refs[s0:s0 + n_sc], refs[s0 + n_sc:]
        first = pl.program_id(0) == 0
        last = pl.program_id(0) == grid[0] - 1
        for d in range(1, len(grid)):
            first = first & (pl.program_id(d) == 0)
            last = last & (pl.program_id(d) == grid[d] - 1)

        @pl.when(first)
        def _():
            carry.start(cin, cout, csems)

        body(*ins, *outs, *scs)

        @pl.when(last)
        def _():
            carry.finish(cin, cout, csems)

    res = pl.pallas_call(
        wrapped, grid=grid, in_specs=list(in_specs) + [HBM_SPEC] * c_in, out_specs=ospecs + [HBM_SPEC] * c_out,
        out_shape=shapes + list(carry.out_shapes), scratch_shapes=list(scratch_shapes) + list(carry.sems),
        compiler_params=_cp(("arbitrary",) * len(grid)), name=name)(*ops, *carry.inputs)
    main = res[:n_out]
    return (list(main) if multi else main[0]), list(res[n_out:])


def norm_matmul(x, xcol, kdim, gain, w, tm, tn, out_dtype, name, epi=None, epi_ops=(), epi_specs=(), carry=None):
    rows, n = x.shape[0], mat_cols(w) if w.ndim == 3 else w.shape[1]
    n_epi = len(epi_ops)
    w_spec = (pl.BlockSpec((kdim, tn), lambda i, j: (0, j)) if w.ndim == 2 else
              pl.BlockSpec((None, kdim, tn), lambda i, j: (j // (w.shape[2] // tn), 0, j % (w.shape[2] // tn))))

    def body(x_ref, g_ref, w_ref, *rest):
        epi_refs = rest[:n_epi]
        out_ref, xn_ref, xn_sc = rest[n_epi:]

        @pl.when(pl.program_id(1) == 0)
        def _():
            xv = x_ref[...]
            y = xv * lax.rsqrt(jnp.mean(xv * xv, axis=-1, keepdims=True) + EPS)
            xn = (y * g_ref[...]).astype(MXU_DT)
            xn_sc[...] = xn
            xn_ref[...] = xn

        acc = jnp.dot(xn_sc[...], w_ref[...], preferred_element_type=F32)
        if epi is not None:
            acc = epi(acc, *[r[...] for r in epi_refs])
        out_ref[...] = acc.astype(out_dtype)

    return _call(
        body, (x, gain, w, *epi_ops), carry, grid=(rows // tm, n // tn),
        in_specs=[pl.BlockSpec((tm, kdim), lambda i, j: (i, xcol)), pl.BlockSpec((1, kdim), lambda i, j: (0, 0)),
                  w_spec, *epi_specs],
        out_specs=[pl.BlockSpec((tm, tn), lambda i, j: (i, j)), pl.BlockSpec((tm, kdim), lambda i, j: (i, 0))],
        out_shape=[S((rows, n), out_dtype), S((rows, kdim), MXU_DT)],
        scratch_shapes=[pltpu.VMEM((tm, kdim), MXU_DT)], sem=("parallel", "arbitrary"), name=name)


def matmul_res(a, w, res, tm, tn, name):
    grp, rows, k = a.shape
    n = w.shape[2]

    def body(a_ref, w_ref, r_ref, o_ref):
        acc = r_ref[...]
        for g in range(grp):
            acc = acc + jnp.dot(a_ref[g], w_ref[g], preferred_element_type=F32)
        o_ref[...] = acc

    return pl.pallas_call(
        body, grid=(rows // tm, n // tn),
        in_specs=[pl.BlockSpec((grp, tm, k), lambda i, j: (0, i, 0)), pl.BlockSpec((grp, k, tn), lambda i, j: (0, 0, j)),
                  pl.BlockSpec((tm, tn), lambda i, j: (i, j))],
        out_specs=pl.BlockSpec((tm, tn), lambda i, j: (i, j)),
        out_shape=S((rows, n), F32), compiler_params=_cp(("parallel", "parallel")), name=name)(a, w, res)


def matmul_nt(a, w, tm, tn, out_dtype, name):
    rows, k = a.shape
    n = w.shape[0]

    def body(a_ref, w_ref, o_ref):
        o_ref[...] = lax.dot_general(a_ref[...].astype(MXU_DT), w_ref[...], NT_DIMS,
                                     preferred_element_type=F32).astype(out_dtype)

    return pl.pallas_call(
        body, grid=(rows // tm, n // tn),
        in_specs=[pl.BlockSpec((tm, k), lambda i, j: (i, 0)), pl.BlockSpec((tn, k), lambda i, j: (j, 0))],
        out_specs=pl.BlockSpec((tm, tn), lambda i, j: (i, j)),
        out_shape=S((rows, n), out_dtype), compiler_params=_cp(("parallel", "parallel")), name=name)(a, w)


def matmul_nt_normbwd(du, w, x, xcol, gain, res, tm, tk, out_dtype, name):
    rows, kc = du.shape[-2], mat_cols(du)
    dn = w.shape[-2]
    nk = kc // tk
    has_res = res is not None
    w_spec = (pl.BlockSpec((dn, tk), lambda i, k: (0, k)) if w.ndim == 2 else
              pl.BlockSpec((None, dn, tk), lambda i, k: (k // (w.shape[2] // tk), 0, k % (w.shape[2] // tk))))

    def body(du_ref, w_ref, x_ref, g_ref, *rest):
        if has_res:
            res_ref, dx_ref, dg_ref, acc = rest
        else:
            dx_ref, dg_ref, acc = rest
        i, k = pl.program_id(0), pl.program_id(1)

        @pl.when(k == 0)
        def _():
            acc[...] = jnp.zeros_like(acc)

        @pl.when((i == 0) & (k == 0))
        def _():
            dg_ref[...] = jnp.zeros_like(dg_ref)

        acc[...] += lax.dot_general(du_ref[...], w_ref[...], NT_DIMS, preferred_element_type=F32)

        @pl.when(k == nk - 1)
        def _():
            dhn = acc[...]
            xv = x_ref[...]
            rstd = lax.rsqrt(jnp.mean(xv * xv, axis=-1, keepdims=True) + EPS)
            xhat = xv * rstd
            dg_ref[...] += jnp.sum(dhn * xhat, axis=0, keepdims=True)
            dxh = dhn * g_ref[...]
            dx = rstd * (dxh - xhat * jnp.mean(dxh * xhat, axis=-1, keepdims=True))
            if has_res:
                dx = dx + res_ref[...]
            dx_ref[...] = dx.astype(out_dtype)

    in_specs = [mat_spec(du, tm, tk, lambda i, k: (i, k)), w_spec,
                pl.BlockSpec((tm, dn), lambda i, k: (i, xcol)), pl.BlockSpec((1, dn), lambda i, k: (0, 0))]
    ops = [du, w, x, gain]
    if has_res:
        in_specs.append(pl.BlockSpec((tm, dn), lambda i, k: (i, 0)))
        ops.append(res)
    return pl.pallas_call(
        body, grid=(rows // tm, nk), in_specs=in_specs,
        out_specs=[pl.BlockSpec((tm, dn), lambda i, k: (i, 0)), pl.BlockSpec((1, dn), lambda i, k: (0, 0))],
        out_shape=[S((rows, dn), out_dtype), S((1, dn), F32)],
        scratch_shapes=[pltpu.VMEM((tm, dn), F32)],
        compiler_params=_cp(("arbitrary", "arbitrary")), name=name)(*ops)


def matmul_tn(a, b, tr, name, carry=None, col_shards=1):
    rows, ka, nb = a.shape[-2], mat_cols(a), mat_cols(b)
    ta = _div_tile(mat_width(a), 1536, LANE)
    tb = _div_tile(mat_width(b), 1536 if ta <= 1024 else 1024, LANE)
    nr = rows // tr
    if col_shards == 1:
        out_spec, out_shape = pl.BlockSpec((ta, tb), lambda i, j, r: (i, j)), S((ka, nb), F32)
    else:
        per = nb // col_shards // tb
        assert per * tb * col_shards == nb
        out_spec = pl.BlockSpec((None, ta, tb), lambda i, j, r: (j // per, i, j % per))
        out_shape = S((col_shards, ka, nb // col_shards), F32)

    def body(a_ref, b_ref, o_ref, acc):
        r = pl.program_id(2)

        @pl.when(r == 0)
        def _():
            acc[...] = jnp.zeros_like(acc)

        acc[...] += lax.dot_general(a_ref[...].astype(MXU_DT), b_ref[...].astype(MXU_DT), TN_DIMS,
                                    preferred_element_type=F32)

        @pl.when(r == nr - 1)
        def _():
            o_ref[...] = acc[...]

    return _call(
        body, (a, b), carry, grid=(ka // ta, nb // tb, nr),
        in_specs=[mat_spec(a, tr, ta, lambda i, j, r: (r, i)), mat_spec(b, tr, tb, lambda i, j, r: (r, j))],
        out_specs=out_spec, out_shape=out_shape, scratch_shapes=[pltpu.VMEM((ta, tb), F32)],
        sem=("parallel", "parallel", "arbitrary"), name=name)


def _sigmoid(x):
    return 1.0 / (1.0 + jnp.exp(-x))


def _log1p(e):
    return jnp.where(e < 1e-3, e * (1.0 - e * (0.5 - e * (1.0 / 3.0 - 0.25 * e))), jnp.log(1.0 + e))


def _softplus(x):
    return jnp.maximum(x, 0.0) + _log1p(jnp.exp(-jnp.abs(x)))


def _expm1(x):
    series = x * (1.0 + x * (0.5 + x * (1.0 / 6.0 + x * (1.0 / 24.0 + x * (1.0 / 120.0)))))
    return jnp.where(jnp.abs(x) < 0.1, series, jnp.exp(x) - 1.0)


_GELU_K = math.sqrt(2.0 / math.pi)
_GELU_C = 0.044715


def _gelu_and_grad(x):
    th = jnp.tanh(_GELU_K * (x + _GELU_C * x * x * x))
    g = 0.5 * x * (1.0 + th)
    dg = 0.5 * (1.0 + th) + 0.5 * x * (1.0 - th * th) * _GELU_K * (1.0 + 3.0 * _GELU_C * x * x)
    return g, dg


def _row_iota(shape):
    return lax.broadcasted_iota(jnp.int32, shape, 0)


def _scan_chunk_fwd(a_sc, u_sc, out_ref, hcar, n, width):
    rowi = _row_iota((SUBLANE, width))

    def step(c, hprev):
        r0 = pl.multiple_of(c * SUBLANE, SUBLANE)
        a = a_sc[pl.ds(r0, SUBLANE), :]
        u = u_sc[pl.ds(r0, SUBLANE), :]
        for d in (1, 2, 4):
            a_s = jnp.where(rowi >= d, pltpu.roll(a, d, axis=0), 1.0)
            u_s = jnp.where(rowi >= d, pltpu.roll(u, d, axis=0), 0.0)
            u = u + a * u_s
            a = a * a_s
        h = u + a * hprev
        out_ref[pl.ds(r0, SUBLANE), :] = h
        return jnp.broadcast_to(h[SUBLANE - 1:SUBLANE, :], (SUBLANE, width))

    hcar[...] = lax.fori_loop(0, n // SUBLANE, step, hcar[...], unroll=4)


def _scan_chunk_bwd(b_sc, d_sc, out_ref, gcar, n, width):
    rowi = _row_iota((SUBLANE, width))
    nc = n // SUBLANE

    def step(c, gnext):
        r0 = pl.multiple_of((nc - 1 - c) * SUBLANE, SUBLANE)
        b = b_sc[pl.ds(r0, SUBLANE), :]
        d = d_sc[pl.ds(r0, SUBLANE), :]
        for s in (1, 2, 4):
            keep = rowi < SUBLANE - s
            b_s = jnp.where(keep, pltpu.roll(b, SUBLANE - s, axis=0), 1.0)
            d_s = jnp.where(keep, pltpu.roll(d, SUBLANE - s, axis=0), 0.0)
            d = d + b * d_s
            b = b * b_s
        g = d + b * gnext
        out_ref[pl.ds(r0, SUBLANE), :] = g
        return jnp.broadcast_to(g[0:1, :], (SUBLANE, width))

    gcar[...] = lax.fori_loop(0, nc, step, gcar[...], unroll=4)


def even_mid_fwd(u, conv_a, conv_b, conv_b_bias, rw, rb, iw, ib, lam, nb, tp, n, name, carry=None):
    rows = u.shape[0]
    w = LANE
    nj = CONV_W // w
    nt = tp // n
    h8 = SUBLANE

    def body(gb_r, gc_r, xa_r, xb_r, gate_r, ca_w, cb_w, cb_b, rw_r, rb_r, iw_r, ib_r, lam_r,
             y_o, ca_o, xc_o, a_o, hs_o, pext, xext, hcar, a_sc, u_sc):
        t = pl.program_id(2)

        @pl.when(t == 0)
        def _():
            pext[0:h8, :] = jnp.zeros((h8, w), F32)
            xext[0:h8, :] = jnp.zeros((h8, w), F32)
            hcar[...] = jnp.zeros_like(hcar)

        p = gc_r[...] * xa_r[...]
        pext[h8:h8 + n, :] = p
        wa = ca_w[...]
        ca = wa[2:3, :] * p + wa[1:2, :] * pext[h8 - 1:h8 - 1 + n, :] + wa[0:1, :] * pext[h8 - 2:h8 - 2 + n, :]
        ca_o[...] = ca
        y_o[0] = (gb_r[...] * ca).astype(MXU_DT)
        pext[0:h8, :] = pext[n:n + h8, :]

        xb = xb_r[...]
        xext[h8:h8 + n, :] = xb
        wb = cb_w[...]
        xc = (wb[3:4, :] * xb + wb[2:3, :] * xext[h8 - 1:h8 - 1 + n, :] + wb[1:2, :] * xext[h8 - 2:h8 - 2 + n, :]
              + wb[0:1, :] * xext[h8 - 3:h8 - 3 + n, :]) + cb_b[...]
        xc_o[...] = xc
        xext[0:h8, :] = xext[n:n + h8, :]

        xcm = xc.astype(MXU_DT)
        r = _sigmoid(jnp.dot(xcm, rw_r[...], preferred_element_type=F32) + rb_r[...])
        ig = _sigmoid(jnp.dot(xcm, iw_r[...], preferred_element_type=F32) + ib_r[...])
        log_a = (-LRU_C) * r * _softplus(-lam_r[...])
        a = jnp.exp(log_a)
        mult = jnp.sqrt(-_expm1(2.0 * log_a))
        a_sc[...] = a
        a_o[...] = a
        u_sc[...] = mult * (ig * xc)
        _scan_chunk_fwd(a_sc, u_sc, hs_o, hcar, n, w)
        gel, _ = _gelu_and_grad(gate_r[...])
        y_o[1] = (gel * hs_o[...]).astype(MXU_DT)

    def ublk(off):
        return pl.BlockSpec((n, w), lambda j, b, t: (b * nt + t, off + j))

    def pblk(r_):
        return pl.BlockSpec((r_, w), lambda j, b, t: (0, j))

    act = pl.BlockSpec((n, w), lambda j, b, t: (b * nt + t, j))
    mat = pl.BlockSpec((w, w), lambda j, b, t: (j, j))
    return _call(
        body, (u, u, u, u, u, conv_a, conv_b, conv_b_bias, rw, rb, iw, ib, lam), carry, grid=(nj, nb, nt),
        in_specs=[ublk(0), ublk(nj), ublk(2 * nj), ublk(3 * nj), ublk(4 * nj), pblk(3), pblk(4), pblk(1),
                  mat, pblk(1), mat, pblk(1), pblk(1)],
        out_specs=[pl.BlockSpec((2, n, w), lambda j, b, t: (0, b * nt + t, j)), act, act, act, act],
        out_shape=[S((2, rows, CONV_W), MXU_DT), S((rows, CONV_W), F32), S((rows, LRU_W), F32), S((rows, LRU_W), F32),
                   S((rows, LRU_W), F32)],
        scratch_shapes=[pltpu.VMEM((n + h8, w), F32), pltpu.VMEM((n + h8, w), F32), pltpu.VMEM((h8, w), F32),
                        pltpu.VMEM((n, w), F32), pltpu.VMEM((n, w), F32)],
        sem=("parallel", "parallel", "arbitrary"), name=name)


def even_mid_bwd(u, dycat, ca, xc, a_sv, hs, conv_a, conv_b, rw, rb, iw, ib, lam, nb, tp, n, name, carry=None):
    rows = u.shape[0]
    w = LANE
    nj = CONV_W // w
    nt = tp // n
    h8 = SUBLANE

    def body(gb_r, gc_r, xa_r, xb_r, gate_r, dya_r, dyb_r, ca_r, xc_r, a_r, hs_r, hsp_r,
             ca_w, cb_w, rw_r, rb_r, iw_r, ib_r, lam_r,
             du_o, dca_w, dcb_w, dcb_b, drw, drb, diw, dib, dlam,
             aext, hext, dext, eext, gcar, b_sc, d_sc, g_sc):
        b, t = pl.program_id(1), pl.program_id(2)

        @pl.when((b == 0) & (t == 0))
        def _():
            for ref in (dca_w, dcb_w, dcb_b, drw, drb, diw, dib, dlam):
                ref[...] = jnp.zeros_like(ref)

        @pl.when(t == 0)
        def _():
            aext[n:n + h8, :] = jnp.zeros((h8, w), F32)
            dext[n:n + h8, :] = jnp.zeros((h8, w), F32)
            eext[n:n + h8, :] = jnp.zeros((h8, w), F32)
            gcar[...] = jnp.zeros_like(gcar)

        xc_v = xc_r[...]
        xcm = xc_v.astype(MXU_DT)
        r = _sigmoid(jnp.dot(xcm, rw_r[...], preferred_element_type=F32) + rb_r[...])
        ig = _sigmoid(jnp.dot(xcm, iw_r[...], preferred_element_type=F32) + ib_r[...])
        lam_v = lam_r[...]
        sp = _softplus(-lam_v)
        log_a = (-LRU_C) * r * sp
        a = a_r[...]
        mult = jnp.sqrt(-_expm1(2.0 * log_a))
        hs_v = hs_r[...]
        gel, dgel = _gelu_and_grad(gate_r[...])
        dyb = dyb_r[...]
        du_o[4] = (dyb * hs_v * dgel).astype(MXU_DT)

        aext[0:n, :] = a
        b_sc[...] = aext[1:1 + n, :]
        d_sc[...] = dyb * gel
        _scan_chunk_bwd(b_sc, d_sc, g_sc, gcar, n, w)
        aext[n:n + h8, :] = aext[0:h8, :]
        g = g_sc[...]

        hext[0:h8, :] = jnp.where(t == nt - 1, 0.0, hsp_r[...])
        hext[h8:h8 + n, :] = hs_v
        da = g * hext[h8 - 1:h8 - 1 + n, :]
        dmult = g * (ig * xc_v)
        di = g * mult * xc_v
        dxc = g * mult * ig
        dlog_a = da * a - dmult * (a * a) / mult
        dr = dlog_a * ((-LRU_C) * sp)
        dsp = jnp.sum(dlog_a * ((-LRU_C) * r), axis=0, keepdims=True)
        dlam[...] += dsp * (-_sigmoid(-lam_v))
        dzr = dr * r * (1.0 - r)
        dzi = di * ig * (1.0 - ig)
        dzr_m = dzr.astype(MXU_DT)
        dzi_m = dzi.astype(MXU_DT)
        dxc = (dxc + lax.dot_general(dzr_m, rw_r[...], NT_DIMS, preferred_element_type=F32)
               + lax.dot_general(dzi_m, iw_r[...], NT_DIMS, preferred_element_type=F32))
        drw[...] += lax.dot_general(xcm, dzr_m, TN_DIMS, preferred_element_type=F32)
        diw[...] += lax.dot_general(xcm, dzi_m, TN_DIMS, preferred_element_type=F32)
        drb[...] += jnp.sum(dzr, axis=0, keepdims=True)
        dib[...] += jnp.sum(dzi, axis=0, keepdims=True)
        dcb_b[...] += jnp.sum(dxc, axis=0, keepdims=True)

        xb = xb_r[...]
        dext[0:n, :] = dxc
        wb = cb_w[...]
        d1, d2, d3 = dext[1:1 + n, :], dext[2:2 + n, :], dext[3:3 + n, :]
        du_o[3] = (wb[3:4, :] * dxc + wb[2:3, :] * d1 + wb[1:2, :] * d2 + wb[0:1, :] * d3).astype(MXU_DT)
        dcb_w[3:4, :] += jnp.sum(xb * dxc, axis=0, keepdims=True)
        dcb_w[2:3, :] += jnp.sum(xb * d1, axis=0, keepdims=True)
        dcb_w[1:2, :] += jnp.sum(xb * d2, axis=0, keepdims=True)
        dcb_w[0:1, :] += jnp.sum(xb * d3, axis=0, keepdims=True)
        dext[n:n + h8, :] = dext[0:h8, :]

        gb, gc, xa = gb_r[...], gc_r[...], xa_r[...]
        dya = dya_r[...]
        du_o[0] = (dya * ca_r[...]).astype(MXU_DT)
        dca = dya * gb
        eext[0:n, :] = dca
        wa = ca_w[...]
        e1, e2 = eext[1:1 + n, :], eext[2:2 + n, :]
        dp = wa[2:3, :] * dca + wa[1:2, :] * e1 + wa[0:1, :] * e2
        p = gc * xa
        dca_w[2:3, :] += jnp.sum(p * dca, axis=0, keepdims=True)
        dca_w[1:2, :] += jnp.sum(p * e1, axis=0, keepdims=True)
        dca_w[0:1, :] += jnp.sum(p * e2, axis=0, keepdims=True)
        eext[n:n + h8, :] = eext[0:h8, :]
        du_o[1] = (dp * xa).astype(MXU_DT)
        du_o[2] = (dp * gc).astype(MXU_DT)

    def rt(b, t):
        return b * nt + (nt - 1 - t)

    def ublk(off):
        return pl.BlockSpec((n, w), lambda j, b, t: (rt(b, t), off + j))

    def pblk(r_):
        return pl.BlockSpec((r_, w), lambda j, b, t: (0, j))

    act = pl.BlockSpec((n, w), lambda j, b, t: (rt(b, t), j))
    n8 = n // h8
    hsp = pl.BlockSpec((h8, w), lambda j, b, t: (jnp.maximum(rt(b, t) * n8 - 1, 0), j))
    mat = pl.BlockSpec((w, w), lambda j, b, t: (j, j))
    return _call(
        body, (u, u, u, u, u, dycat, dycat, ca, xc, a_sv, hs, hs, conv_a, conv_b, rw, rb, iw, ib, lam), carry,
        grid=(nj, nb, nt),
        in_specs=[ublk(0), ublk(nj), ublk(2 * nj), ublk(3 * nj), ublk(4 * nj), ublk(0), ublk(nj), act, act, act, act,
                  hsp, pblk(3), pblk(4), mat, pblk(1), mat, pblk(1), pblk(1)],
        out_specs=[pl.BlockSpec((5, n, w), lambda j, b, t: (0, rt(b, t), j)), pblk(3), pblk(4), pblk(1),
                   mat, pblk(1), mat, pblk(1), pblk(1)],
        out_shape=[S((5, rows, CONV_W), MXU_DT), S((3, CONV_W), F32), S((4, LRU_W), F32), S((1, LRU_W), F32),
                   S((LRU_W, LRU_W), F32), S((1, LRU_W), F32), S((LRU_W, LRU_W), F32), S((1, LRU_W), F32),
                   S((1, LRU_W), F32)],
        scratch_shapes=[pltpu.VMEM((n + h8, w), F32)] * 4 + [pltpu.VMEM((h8, w), F32)] + [pltpu.VMEM((n, w), F32)] * 3,
        sem=("arbitrary", "arbitrary", "arbitrary"), name=name)


def ffn_mid_fwd(up, cw, cb, nb, tp, n, name):
    rows = up.shape[0]
    w = FFN_CT
    nj = D_FF // w
    nt = tp // n
    h8 = SUBLANE

    def body(xa_r, xg_r, w_r, b_r, u_o, y_o, aext, gext):
        t = pl.program_id(2)

        @pl.when(t == 0)
        def _():
            aext[0:h8, :] = jnp.zeros((h8, w), F32)
            gext[0:h8, :] = jnp.zeros((h8, w), F32)

        def conv(x_r, ext, g):
            x = x_r[...].astype(F32)
            ext[h8:h8 + n, :] = x
            wv = w_r[g]
            y = (wv[2:3, :] * x + wv[1:2, :] * ext[h8 - 1:h8 - 1 + n, :] + wv[0:1, :] * ext[h8 - 2:h8 - 2 + n, :]) + b_r[g]
            ext[0:h8, :] = ext[n:n + h8, :]
            return y

        ua = conv(xa_r, aext, 0)
        ug = conv(xg_r, gext, 1)
        u_o[0] = ua.astype(MXU_DT)
        u_o[1] = ug.astype(MXU_DT)
        y_o[...] = (ua * _sigmoid(ua) * ug).astype(MXU_DT)

    def ublk(off):
        return pl.BlockSpec((n, w), lambda j, b, t: (b * nt + t, off + j))

    return pl.pallas_call(
        body, grid=(nj, nb, nt),
        in_specs=[ublk(0), ublk(nj), pl.BlockSpec((2, 3, w), lambda j, b, t: (0, 0, j)),
                  pl.BlockSpec((2, 1, w), lambda j, b, t: (0, 0, j))],
        out_specs=[pl.BlockSpec((2, n, w), lambda j, b, t: (0, b * nt + t, j)), ublk(0)],
        out_shape=[S((2, rows, D_FF), MXU_DT), S((rows, D_FF), MXU_DT)],
        scratch_shapes=[pltpu.VMEM((n + h8, w), F32)] * 2,
        compiler_params=_cp(("parallel", "parallel", "arbitrary")), name=name,
    )(up, up, cw, cb)


def ffn_mid_bwd(dy, u, up, cw, nb, tp, n, name, carry=None):
    rows = up.shape[0]
    w = FFN_CT
    nj = D_FF // w
    nt = tp // n
    h8 = SUBLANE

    def body(dy_r, u_r, xa_r, xg_r, w_r, dx_o, dw, db, aext, gext):
        b, t = pl.program_id(1), pl.program_id(2)

        @pl.when((b == 0) & (t == 0))
        def _():
            dw[...] = jnp.zeros_like(dw)
            db[...] = jnp.zeros_like(db)

        @pl.when(t == 0)
        def _():
            aext[n:n + h8, :] = jnp.zeros((h8, w), F32)
            gext[n:n + h8, :] = jnp.zeros((h8, w), F32)

        dyv = dy_r[...].astype(F32)
        ua, ug = u_r[0].astype(F32), u_r[1].astype(F32)
        sg = _sigmoid(ua)
        dua = dyv * ug * (sg * (1.0 + ua * (1.0 - sg)))
        dug = dyv * (ua * sg)

        def conv_t(du, ext, x_r, g):
            db[g] += jnp.sum(du, axis=0, keepdims=True)
            ext[0:n, :] = du
            wv = w_r[g]
            x = x_r[...].astype(F32)
            d1, d2 = ext[1:1 + n, :], ext[2:2 + n, :]
            dx_o[g] = (wv[2:3, :] * du + wv[1:2, :] * d1 + wv[0:1, :] * d2).astype(MXU_DT)
            dw[g, 2:3, :] += jnp.sum(x * du, axis=0, keepdims=True)
            dw[g, 1:2, :] += jnp.sum(x * d1, axis=0, keepdims=True)
            dw[g, 0:1, :] += jnp.sum(x * d2, axis=0, keepdims=True)
            ext[n:n + h8, :] = ext[0:h8, :]

        conv_t(dua, aext, xa_r, 0)
        conv_t(dug, gext, xg_r, 1)

    def rt(b, t):
        return b * nt + (nt - 1 - t)

    def ublk(off):
        return pl.BlockSpec((n, w), lambda j, b, t: (rt(b, t), off + j))

    pair = pl.BlockSpec((2, n, w), lambda j, b, t: (0, rt(b, t), j))
    return _call(
        body, (dy, u, up, up, cw), carry, grid=(nj, nb, nt),
        in_specs=[ublk(0), pair, ublk(0), ublk(nj), pl.BlockSpec((2, 3, w), lambda j, b, t: (0, 0, j))],
        out_specs=[pair, pl.BlockSpec((2, 3, w), lambda j, b, t: (0, 0, j)),
                   pl.BlockSpec((2, 1, w), lambda j, b, t: (0, 0, j))],
        out_shape=[S((2, rows, D_FF), MXU_DT), S((2, 3, D_FF), F32), S((2, 1, D_FF), F32)],
        scratch_shapes=[pltpu.VMEM((n + h8, w), F32)] * 2,
        sem=("arbitrary", "arbitrary", "arbitrary"), name=name)


def _lane_mod(shape):
    return lax.broadcasted_iota(jnp.int32, shape, 1) & (HP - 1)


def _q_rope_epi(acc, tab):
    reps = acc.shape[1] // HP
    a = acc * jnp.tile(tab, (1, reps))
    lane = _lane_mod(a.shape)
    shifted = pltpu.roll(a, a.shape[1] - QK_ROPE, axis=1)
    return jnp.where(lane < QK_NOPE, a, jnp.where(lane < QK_HEAD, a + shifted, 0.0)) * Q_PRESCALE


def _k_rope_block(krblk, tabk):
    a = krblk * tabk
    lane = _lane_mod(a.shape)
    b = a + pltpu.roll(a, HP - QK_ROPE, axis=1)
    return jnp.where((lane >= QK_NOPE) & (lane < QK_HEAD), b, 0.0)


def _k_rope_epi(acc, krblk, tabk):
    reps = acc.shape[1] // HP
    return acc + jnp.tile(_k_rope_block(krblk, tabk), (1, reps))


def attn_fwd(q, k, v, nb, tp, name, carry=None):
    rows = q.shape[0]
    blk = ATT_BLK
    nq = tp // blk
    npair = MLA_HEADS // 2

    def body(q_r, k_r, v_r, o_r, lse_r):
        qi = pl.program_id(2)
        lane = lax.broadcasted_iota(jnp.int32, (blk, LANE), 1)
        even = lane < V_HEAD
        sum_lane = (V_HEAD, 0)
        rowi = lax.broadcasted_iota(jnp.int32, (blk, blk), 0)
        coli = lax.broadcasted_iota(jnp.int32, (blk, blk), 1)
        qs = [q_r[:, h * HP:(h + 1) * HP] for h in range(2)]

        def kv_block(k0, width, carry, diagonal):
            ms, accs = carry
            vblk = v_r[pl.ds(k0, width), :]
            one = jnp.ones_like(vblk)
            zero = jnp.zeros_like(vblk)
            vlane = lax.broadcasted_iota(jnp.int32, (width, LANE), 1)
            new_ms, new_accs = [], []
            for h in range(2):
                kh = k_r[pl.ds(k0, width), h * HP:(h + 1) * HP]
                s = lax.dot_general(qs[h], kh, NT_DIMS, preferred_element_type=F32)
                if diagonal:
                    s = jnp.where(coli <= rowi, s, -jnp.inf)
                m_new = jnp.maximum(ms[h], jnp.max(s, axis=1, keepdims=True))
                alpha = jnp.exp2(ms[h] - m_new)
                p = jnp.exp2(s - m_new).astype(MXU_DT)
                mine = (vlane < V_HEAD) if h == 0 else (vlane >= V_HEAD)
                vh = jnp.where(mine, vblk, jnp.where(vlane == sum_lane[h], one, zero))
                new_accs.append(alpha * accs[h] + jnp.dot(p, vh, preferred_element_type=F32))
                new_ms.append(m_new)
            return tuple(new_ms), tuple(new_accs)

        neg = jnp.full((blk, 1), -jnp.inf, F32)
        zacc = jnp.zeros((blk, LANE), F32)
        carry = lax.fori_loop(0, qi // 2, lambda i, c: kv_block(pl.multiple_of(i * 2 * blk, blk), 2 * blk, c, False),
                              ((neg, neg), (zacc, zacc)))
        carry = lax.cond(qi % 2 == 1, lambda c: kv_block(pl.multiple_of((qi - 1) * blk, blk), blk, c, False),
                         lambda c: c, carry)
        ms, accs = kv_block(pl.multiple_of(qi * blk, blk), blk, carry, True)
        ls = [accs[h][:, sum_lane[h]:sum_lane[h] + 1] for h in range(2)]
        o_r[...] = jnp.where(even, accs[0] / ls[0], accs[1] / ls[1]).astype(MXU_DT)
        lse_r[...] = jnp.where(even, ms[0] + jnp.log2(ls[0]), ms[1] + jnp.log2(ls[1]))

    return _call(
        body, (q, k, v), carry, grid=(nb, npair, nq),
        in_specs=[pl.BlockSpec((blk, 2 * HP), lambda b, p, i: (b * nq + i, p)),
                  pl.BlockSpec((tp, 2 * HP), lambda b, p, i: (b, p)),
                  pl.BlockSpec((tp, LANE), lambda b, p, i: (b, p))],
        out_specs=[pl.BlockSpec((blk, LANE), lambda b, p, i: (b * nq + i, p)),
                   pl.BlockSpec((None, blk, LANE), lambda b, p, i: (p, b * nq + i, 0))],
        out_shape=[S((rows, MLA_HEADS * V_HEAD), MXU_DT), S((npair, rows, LANE), F32)], scratch_shapes=[],
        sem=("parallel", "parallel", "arbitrary"), name=name)


def attn_bwd(q, k, v, o, do, lse, nb, tp, name, carry=None):
    rows = q.shape[0]
    blk = ATT_BLK
    nq = tp // blk
    npair = MLA_HEADS // 2
    scale = QK_HEAD ** -0.5

    def body(q_r, k_r, v_r, o_r, do_r, lse_r, dq_o, dk_o, dv_o, dq_acc, delta_sc):
        kb = pl.program_id(2)
        even = lax.broadcasted_iota(jnp.int32, (blk, LANE), 1) < V_HEAD
        rowi = lax.broadcasted_iota(jnp.int32, (blk, blk), 0)
        coli = lax.broadcasted_iota(jnp.int32, (blk, blk), 1)

        @pl.when(kb == 0)
        def _():
            dq_acc[...] = jnp.zeros_like(dq_acc)

            def dstep(i, c):
                r0 = pl.multiple_of(i * blk, blk)
                prod = do_r[pl.ds(r0, blk), :].astype(F32) * o_r[pl.ds(r0, blk), :].astype(F32)
                de = jnp.sum(jnp.where(even, prod, 0.0), axis=1, keepdims=True)
                dd = jnp.sum(jnp.where(even, 0.0, prod), axis=1, keepdims=True)
                delta_sc[pl.ds(r0, blk), :] = jnp.where(even, de, dd)
                return c

            lax.fori_loop(0, nq, dstep, 0)

        vblk = v_r[...]
        ks = [k_r[:, h * HP:(h + 1) * HP] for h in range(2)]

        def q_block(r0, height, carry, diagonal):
            dk0, dk1, dv = carry
            dob = do_r[pl.ds(r0, height), :]
            lse_b = lse_r[pl.ds(r0, height), :]
            dl_b = delta_sc[pl.ds(r0, height), :]
            qlane = lax.broadcasted_iota(jnp.int32, (height, LANE), 1)
            dks = [dk0, dk1]
            for h in range(2):
                lo = 0 if h == 0 else V_HEAD
                qh = q_r[pl.ds(r0, height), h * HP:(h + 1) * HP]
                s = lax.dot_general(qh, ks[h], NT_DIMS, preferred_element_type=F32)
                p = jnp.exp2(s - lse_b[:, lo:lo + 1])
                if diagonal:
                    p = jnp.where(coli <= rowi, p, 0.0)
                mine = (qlane < V_HEAD) if h == 0 else (qlane >= V_HEAD)
                doh = jnp.where(mine, dob, jnp.zeros_like(dob))
                dp = lax.dot_general(doh, vblk, NT_DIMS, preferred_element_type=F32)
                ds = (p * (dp - dl_b[:, lo:lo + 1])).astype(MXU_DT)
                dv = dv + lax.dot_general(p.astype(MXU_DT), doh, TN_DIMS, preferred_element_type=F32)
                dks[h] = dks[h] + lax.dot_general(ds, qh, TN_DIMS, preferred_element_type=F32)
                dq_acc[pl.ds(r0, height), h * HP:(h + 1) * HP] += jnp.dot(ds, ks[h], preferred_element_type=F32)
            return dks[0], dks[1], dv

        z = jnp.zeros((blk, HP), F32)
        carry = q_block(pl.multiple_of(kb * blk, blk), blk, (z, z, jnp.zeros((blk, LANE), F32)), True)
        below = nq - 1 - kb
        carry = lax.fori_loop(
            0, below // 2, lambda i, c: q_block(pl.multiple_of((kb + 1 + 2 * i) * blk, blk), 2 * blk, c, False), carry)
        dk0, dk1, dv = lax.cond(below % 2 == 1, lambda c: q_block(pl.multiple_of((nq - 1) * blk, blk), blk, c, False),
                                lambda c: c, carry)
        dk_o[:, 0:HP] = (dk0 * (scale / Q_PRESCALE)).astype(MXU_DT)
        dk_o[:, HP:2 * HP] = (dk1 * (scale / Q_PRESCALE)).astype(MXU_DT)
        dv_o[...] = dv.astype(MXU_DT)

        @pl.when(kb == nq - 1)
        def _():
            dq_o[...] = (dq_acc[...] * scale).astype(MXU_DT)

    seq_pair = pl.BlockSpec((tp, LANE), lambda b, p, kk: (b, p))
    return _call(
        body, (q, k, v, o, do, lse), carry, grid=(nb, npair, nq),
        in_specs=[pl.BlockSpec((tp, 2 * HP), lambda b, p, kk: (b, p)),
                  pl.BlockSpec((blk, 2 * HP), lambda b, p, kk: (b * nq + kk, p)),
                  pl.BlockSpec((blk, LANE), lambda b, p, kk: (b * nq + kk, p)),
                  seq_pair, seq_pair, pl.BlockSpec((None, tp, LANE), lambda b, p, kk: (p, b, 0))],
        out_specs=[pl.BlockSpec((tp, 2 * HP), lambda b, p, kk: (b, p)),
                   pl.BlockSpec((blk, 2 * HP), lambda b, p, kk: (b * nq + kk, p)),
                   pl.BlockSpec((blk, LANE), lambda b, p, kk: (b * nq + kk, p))],
        out_shape=[S((rows, MLA_HEADS * HP), MXU_DT), S((rows, MLA_HEADS * HP), MXU_DT),
                   S((rows, MLA_HEADS * V_HEAD), MXU_DT)],
        scratch_shapes=[pltpu.VMEM((tp, 2 * HP), F32), pltpu.VMEM((tp, LANE), F32)],
        sem=("parallel", "parallel", "arbitrary"), name=name)


def rope_bwd(dq, dk, dv, tabq, tabk, tp, tm, name):
    rows = dq.shape[0]
    nt = tp // tm
    wq = MLA_HEADS * HP

    def body(dq_r, dk_r, dv_r, tq_r, tk_r, dqa_o, dkv_o, dkr_o):
        dqv = dq_r[...].astype(F32)
        lane = _lane_mod(dqv.shape)
        in_rope = (lane >= QK_NOPE) & (lane < QK_HEAD)
        rope = jnp.where(in_rope, dqv, 0.0)
        da = jnp.where(lane < QK_HEAD, dqv, 0.0) + pltpu.roll(rope, QK_ROPE, axis=1)
        dqa_o[...] = (da * jnp.tile(tq_r[...], (1, MLA_HEADS))).astype(MXU_DT)
        dkf = dk_r[...].astype(F32)
        dkv_o[:, 0:wq] = jnp.where(lane < QK_NOPE, dkf, 0.0).astype(MXU_DT)
        dkv_o[:, wq:] = dv_r[...]
        kr = jnp.where(in_rope, dkf, 0.0)
        tot = kr[:, 0:HP]
        for h in range(1, MLA_HEADS):
            tot = tot + kr[:, h * HP:(h + 1) * HP]
        dkr_o[...] = ((tot + pltpu.roll(tot, QK_ROPE, axis=1)) * tk_r[...]).astype(MXU_DT)

    def rowblk(wd):
        return pl.BlockSpec((tm, wd), lambda i: (i, 0))

    tab = pl.BlockSpec((tm, HP), lambda i: (i % nt, 0))
    return pl.pallas_call(
        body, grid=(rows // tm,), in_specs=[rowblk(wq), rowblk(wq), rowblk(MLA_HEADS * V_HEAD), tab, tab],
        out_specs=[rowblk(wq), rowblk(wq + MLA_HEADS * V_HEAD), rowblk(HP)],
        out_shape=[S((rows, wq), MXU_DT), S((rows, wq + MLA_HEADS * V_HEAD), MXU_DT), S((rows, HP), MXU_DT)],
        compiler_params=_cp(("parallel",)), name=name)(dq, dk, dv, tabq, tabk)


def loss_head(h, target, gain, tp, t_real, tm, name):
    rows = h.shape[0]
    nt = tp // tm

    def body(h_r, t_r, g_r, dh_o, loss_o, dg_o):
        i = pl.program_id(0)

        @pl.when(i == 0)
        def _():
            loss_o[...] = jnp.zeros_like(loss_o)
            dg_o[...] = jnp.zeros_like(dg_o)

        xv = h_r[...]
        rstd = lax.rsqrt(jnp.mean(xv * xv, axis=-1, keepdims=True) + EPS)
        xhat = xv * rstd
        g = g_r[...]
        pos = (i % nt) * tm + lax.broadcasted_iota(jnp.int32, (tm, 1), 0)
        valid = (pos >= N_META) & (pos < t_real)
        err = jnp.where(valid, xhat * g - t_r[...], 0.0)
        loss_o[...] += 0.5 * jnp.sum(jnp.mean(err * err, axis=-1, keepdims=True))
        dy = err * (1.0 / D_MODEL)
        dg_o[...] += jnp.sum(dy * xhat, axis=0, keepdims=True)
        dxh = dy * g
        dh_o[...] = rstd * (dxh - xhat * jnp.mean(dxh * xhat, axis=-1, keepdims=True))

    blk = pl.BlockSpec((tm, D_MODEL), lambda i: (i, 0))
    return pl.pallas_call(
        body, grid=(rows // tm,), in_specs=[blk, blk, pl.BlockSpec((1, D_MODEL), lambda i: (0, 0))],
        out_specs=[blk, pl.BlockSpec((1, LANE), lambda i: (0, 0)), pl.BlockSpec((1, D_MODEL), lambda i: (0, 0))],
        out_shape=[S((rows, D_MODEL), F32), S((1, LANE), F32), S((1, D_MODEL), F32)],
        compiler_params=_cp(("arbitrary",)), name=name)(h, target, gain)


ADAM_TILE_ELEMS = 128 * 1024


def adamw(g, w, m, v, name):
    shape = w.shape
    cols = shape[-1]
    rws = max(1, math.prod(shape[:-1]))
    tr = rws if rws * cols <= ADAM_TILE_ELEMS else _div_tile(rws, max(SUBLANE, ADAM_TILE_ELEMS // cols), SUBLANE)
    bc1 = 1.0 - ADAM_B1 ** ADAM_STEP
    bc2 = 1.0 - ADAM_B2 ** ADAM_STEP

    def body(g_r, w_r, m_r, v_r, go, do, mo, vo):
        gv = g_r[...]
        mn = ADAM_B1 * m_r[...] + (1.0 - ADAM_B1) * gv
        vn = ADAM_B2 * v_r[...] + (1.0 - ADAM_B2) * (gv * gv)
        m_hat = mn / bc1
        v_hat = vn / bc2
        go[...] = gv
        do[...] = -ADAM_LR * (m_hat / (jnp.sqrt(v_hat) + ADAM_EPS) + ADAM_WD * w_r[...])
        mo[...] = mn
        vo[...] = vn

    blk = pl.BlockSpec((tr, cols), lambda i: (i, 0))
    outs = pl.pallas_call(
        body, grid=(rws // tr,), in_specs=[blk] * 4, out_specs=[blk] * 4, out_shape=[S((rws, cols), F32)] * 4,
        compiler_params=_cp(("parallel",)), name=name,
    )(*[a.reshape(rws, cols) for a in (g, w, m, v)])
    return tuple(o.reshape(shape) for o in outs)


SUM_ROWS = 512


def _place():
    return lax.axis_index("x"), lax.axis_index("y"), lax.axis_index("c")


def _remote(src, dst, send_sems, recv_sems, k, to):
    return pltpu.make_async_remote_copy(src_ref=src, dst_ref=dst, send_sem=send_sems.at[k], recv_sem=recv_sems.at[k],
                                        device_id=to, device_id_type=MESH)


def stage_gather_chips(xs):
    _, rws, _ = xs.shape

    def copies(ins, outs, sems):
        (x_ref,), (out_ref,), (send_sems, recv_sems, local_sem) = ins, outs, sems
        mx, my, mc = _place()
        sibling = (mx, my, 1 - mc)
        chips = [(1 - mx, my), (mx, 1 - my), (1 - mx, 1 - my)]

        def piece(cx, cy, h):
            return out_ref.at[2 * cx + cy, h]

        mine = pltpu.make_async_copy(x_ref, out_ref.at[2 * mx + my], local_sem)
        first = [_remote(x_ref.at[mc], piece(mx, my, mc), send_sems, recv_sems, j, (cx, cy, mc))
                 for j, (cx, cy) in enumerate(chips)]
        landed = [_remote(x_ref.at[mc], piece(cx, cy, mc), send_sems, recv_sems, j, (cx, cy, mc))
                  for j, (cx, cy) in enumerate(chips)]
        passed = [_remote(piece(cx, cy, mc), piece(cx, cy, mc), send_sems, recv_sems, 3 + j, sibling)
                  for j, (cx, cy) in enumerate(chips)]
        from_sibling = [_remote(x_ref.at[mc], piece(cx, cy, 1 - mc), send_sems, recv_sems, 3 + j, sibling)
                        for j, (cx, cy) in enumerate(chips)]
        return mine, first, landed, passed, from_sibling

    def start(ins, outs, sems):
        mine, first, _, _, _ = copies(ins, outs, sems)
        mine.start()
        for cp in first:
            cp.start()

    def finish(ins, outs, sems):
        mine, first, landed, passed, from_sibling = copies(ins, outs, sems)
        for j in range(3):
            landed[j].wait_recv()
            passed[j].start()
        for cp in from_sibling:
            cp.wait_recv()
        for cp in first + passed:
            cp.wait_send()
        mine.wait()

    return Stage([xs], [S((4, 2, rws, LANE), xs.dtype)],
                 [pltpu.SemaphoreType.DMA((6,)), pltpu.SemaphoreType.DMA((6,)), pltpu.SemaphoreType.DMA], start, finish)


def stage_pair_exchange(g4):
    _, _, rws, _ = g4.shape

    def copies(ins, outs, sems):
        (g_ref,), (land_ref,), (send_sems, recv_sems) = ins, outs, sems
        mx, my, mc = _place()
        return [_remote(g_ref.at[s, 1 - mc], land_ref.at[s], send_sems, recv_sems, s, (mx, my, 1 - mc))
                for s in range(4)]

    def start(ins, outs, sems):
        for cp in copies(ins, outs, sems):
            cp.start()

    def finish(ins, outs, sems):
        cps = copies(ins, outs, sems)
        for cp in cps:
            cp.wait_recv()
        for cp in cps:
            cp.wait_send()

    return Stage([g4], [S((4, rws, LANE), g4.dtype)],
                 [pltpu.SemaphoreType.DMA((4,)), pltpu.SemaphoreType.DMA((4,))], start, finish)


def pair_sum(g4, land, c_idx, name):
    _, _, rws, _ = g4.shape
    th = SUM_ROWS

    def body(c_ref, a_ref, b_ref, o_ref):
        o_ref[...] = a_ref[...] + b_ref[...]

    return pl.pallas_call(
        body,
        grid_spec=pltpu.PrefetchScalarGridSpec(
            num_scalar_prefetch=1, grid=(4, rws // th),
            in_specs=[pl.BlockSpec((None, None, th, LANE), lambda s, i, c: (s, c[0], i, 0)),
                      pl.BlockSpec((None, th, LANE), lambda s, i, c: (s, i, 0))],
            out_specs=pl.BlockSpec((None, th, LANE), lambda s, i, c: (s, i, 0))),
        out_shape=S((4, rws, LANE), F32), compiler_params=_cp(("parallel", "parallel")), name=name)(c_idx, g4, land)


def stage_chip_scatter(p4):
    _, rws, _ = p4.shape

    def copies(ins, outs, sems):
        (p_ref,), (land_ref,), (send_sems, recv_sems, local_sem) = ins, outs, sems
        mx, my, mc = _place()
        me = 2 * mx + my
        chips = [(1 - mx, my), (mx, 1 - my), (1 - mx, 1 - my)]
        mine = pltpu.make_async_copy(p_ref.at[me], land_ref.at[me], local_sem)
        sent = [_remote(p_ref.at[2 * cx + cy], land_ref.at[me], send_sems, recv_sems, j, (cx, cy, mc))
                for j, (cx, cy) in enumerate(chips)]
        landed = [_remote(p_ref.at[me], land_ref.at[2 * cx + cy], send_sems, recv_sems, j, (cx, cy, mc))
                  for j, (cx, cy) in enumerate(chips)]
        return mine, sent, landed

    def start(ins, outs, sems):
        mine, sent, _ = copies(ins, outs, sems)
        mine.start()
        for cp in sent:
            cp.start()

    def finish(ins, outs, sems):
        mine, sent, landed = copies(ins, outs, sems)
        for cp in landed:
            cp.wait_recv()
        for cp in sent:
            cp.wait_send()
        mine.wait()

    return Stage([p4], [S((4, rws, LANE), p4.dtype)],
                 [pltpu.SemaphoreType.DMA((3,)), pltpu.SemaphoreType.DMA((3,)), pltpu.SemaphoreType.DMA], start, finish)


def chip_sum(l4, name):
    _, rws, _ = l4.shape
    th = SUM_ROWS

    def body(a, b, c, d, o_ref):
        o_ref[...] = ((a[...] + b[...]) + c[...]) + d[...]

    def blk(s):
        return pl.BlockSpec((None, th, LANE), lambda i: (s, i, 0))

    return pl.pallas_call(
        body, grid=(rws // th,), in_specs=[blk(0), blk(1), blk(2), blk(3)],
        out_specs=pl.BlockSpec((th, LANE), lambda i: (i, 0)), out_shape=S((rws, LANE), F32),
        compiler_params=_cp(("parallel",)), name=name)(l4, l4, l4, l4)


def stage_pair_gather(rh):
    rws, _ = rh.shape

    def copies(ins, outs, sems):
        (r_ref,), (out_ref,), (send_sems, recv_sems, local_sem) = ins, outs, sems
        mx, my, mc = _place()
        sibling = (mx, my, 1 - mc)
        mine = pltpu.make_async_copy(r_ref, out_ref.at[mc], local_sem)
        sent = _remote(r_ref, out_ref.at[mc], send_sems, recv_sems, 0, sibling)
        landed = _remote(r_ref, out_ref.at[1 - mc], send_sems, recv_sems, 0, sibling)
        return mine, sent, landed

    def start(ins, outs, sems):
        mine, sent, _ = copies(ins, outs, sems)
        mine.start()
        sent.start()

    def finish(ins, outs, sems):
        mine, sent, landed = copies(ins, outs, sems)
        landed.wait_recv()
        sent.wait_send()
        mine.wait()

    return Stage([rh], [S((2, rws, LANE), rh.dtype)],
                 [pltpu.SemaphoreType.DMA((1,)), pltpu.SemaphoreType.DMA((1,)), pltpu.SemaphoreType.DMA], start, finish)


PACK_ELEMS = 16 * LANE


def pack_rows(arrays, lead, total_mult):
    parts, offs, r0 = [], [], 0
    for a in arrays:
        flat = a.reshape(a.shape[:lead] + (-1,))
        elems = _round_up(flat.shape[-1], PACK_ELEMS)
        flat = jnp.pad(flat, [(0, 0)] * lead + [(0, elems - flat.shape[-1])])
        parts.append(flat.reshape(flat.shape[:lead] + (elems // LANE, LANE)))
        offs.append((r0, elems // LANE))
        r0 += elems // LANE
    total = _round_up(r0, total_mult)
    if total > r0:
        parts.append(jnp.zeros(parts[0].shape[:lead] + (total - r0, LANE), parts[0].dtype))
    return jnp.concatenate(parts, axis=lead), offs


def unpack_rows(buf, off, shape):
    r0, nr = off
    lead = buf.shape[:-2]
    n = math.prod(shape)
    return buf[..., r0:r0 + nr, :].reshape(lead + (nr * LANE,))[..., :n].reshape(lead + tuple(shape))


def unshard(stacked, axis):
    x = jnp.moveaxis(stacked, 0, axis)
    return x.reshape(x.shape[:axis] + (4 * x.shape[axis + 1],) + x.shape[axis + 2:])


def to_shards(full, axis):
    n = full.shape[axis] // 4
    x = full.reshape(full.shape[:axis] + (4, n) + full.shape[axis + 1:])
    return jnp.moveaxis(x, axis, 0)


def _rot_cols(w):
    half = w.shape[-1] // 2
    return jnp.concatenate([-w[..., half:], w[..., :half]], axis=-1)


def _unrot_cols(dw):
    half = dw.shape[-1] // 2
    return jnp.concatenate([dw[..., half:], -dw[..., :half]], axis=-1)


def odd_w_in_padded(w_in):
    kr = w_in[:, Q_LORA + KV_LORA:]
    rows = w_in.shape[0]
    return jnp.concatenate([w_in[:, :Q_LORA], jnp.zeros((rows, 128), w_in.dtype), w_in[:, Q_LORA:Q_LORA + KV_LORA],
                            jnp.zeros((rows, 64), w_in.dtype), kr, _rot_cols(kr)], axis=1)


def odd_w_in_unpad(dwp):
    base = 512 + KV_LORA + 64
    dkr = dwp[:, base:base + QK_ROPE] + _unrot_cols(dwp[:, base + QK_ROPE:base + 2 * QK_ROPE])
    return jnp.concatenate([dwp[:, :Q_LORA], dwp[:, 512:512 + KV_LORA], dkr], axis=1)


def uq_padded(w_uq):
    w = w_uq.reshape(Q_LORA, MLA_HEADS, QK_HEAD)
    return jnp.concatenate([w, _rot_cols(w[:, :, QK_NOPE:])], axis=-1).reshape(Q_LORA, MLA_HEADS * HP)


def uq_unpad(dwp):
    d = dwp.reshape(Q_LORA, MLA_HEADS, HP)
    rope = d[:, :, QK_NOPE:QK_HEAD] + _unrot_cols(d[:, :, QK_HEAD:])
    return jnp.concatenate([d[:, :, :QK_NOPE], rope], axis=-1).reshape(Q_LORA, MLA_HEADS * QK_HEAD)


def ukv_padded(w_ukv):
    w = w_ukv.reshape(KV_LORA, MLA_HEADS, QK_NOPE + V_HEAD)
    wk = jnp.concatenate([w[:, :, :QK_NOPE], jnp.zeros((KV_LORA, MLA_HEADS, HP - QK_NOPE), w.dtype)], axis=-1)
    return jnp.concatenate([wk.reshape(KV_LORA, MLA_HEADS * HP), w[:, :, QK_NOPE:].reshape(KV_LORA, MLA_HEADS * V_HEAD)],
                           axis=1)


def ukv_unpad(dwp):
    dk = dwp[:, :MLA_HEADS * HP].reshape(KV_LORA, MLA_HEADS, HP)[:, :, :QK_NOPE]
    dv = dwp[:, MLA_HEADS * HP:].reshape(KV_LORA, MLA_HEADS, V_HEAD)
    return jnp.concatenate([dk, dv], axis=-1).reshape(KV_LORA, MLA_HEADS * (QK_NOPE + V_HEAD))


def block_diag(w):
    h, d, _ = w.shape
    eye = jnp.eye(h, dtype=w.dtype)
    return (eye[:, None, :, None] * w[:, :, None, :]).reshape(h * d, h * d)


def block_diag_part(dense, h):
    d = dense.shape[0] // h
    x = dense.reshape(h, d, h, d)
    return jnp.stack([x[i, :, i, :] for i in range(h)], axis=0)


def rope_tables(tp):
    pos = jnp.arange(tp, dtype=F32)
    inv_freq = ROPE_BASE ** (-jnp.arange(0, QK_ROPE, 2, dtype=F32) / QK_ROPE)
    ang = pos[:, None] * inv_freq[None, :]
    cos2 = jnp.tile(jnp.cos(ang), (1, 2))
    sin2 = jnp.tile(jnp.sin(ang), (1, 2))
    tabq = jnp.concatenate([jnp.ones((tp, QK_NOPE), F32), cos2, sin2], axis=1)
    tabk = jnp.concatenate([jnp.zeros((tp, QK_NOPE), F32), cos2, sin2], axis=1)
    return tabq, tabk


class Dims:
    def __init__(self, nb, seq):
        self.nb = nb
        self.t_real = seq + N_META
        self.tp = _round_up(self.t_real, ATT_BLK)
        self.n = self.tp // 4
        assert self.n % 16 == 0
        self.rows = nb * self.tp


class NoComm:
    def advance(self, carried):
        return None


def even_fwd(h, p, dm, comm):
    (u, hn), _ = norm_matmul(h, 0, D_MODEL, p["norm"], p["w_in"], dm.n, 512, F32, "ev_in")
    (y, ca, xc, a, hs), got = even_mid_fwd(u, p["conv_a"], p["conv_b"], p["conv_b_bias"], p["rw"], p["r_b"], p["iw"],
                                           p["i_b"], p["lam"], dm.nb, dm.tp, dm.n, "ev_mid", carry=comm.advance(None))
    comm.advance(got)
    out = matmul_res(y, p["w_out"].reshape(2, CONV_W, D_MODEL), h, dm.n, 512, "ev_out")
    return out, (h, u, hn, ca, xc, a, hs, y)


def even_bwd(dout, saved, p, dm, comm):
    h, u, hn, ca, xc, a, hs, y = saved
    g = {}
    dycat = matmul_nt(dout, p["w_out"], dm.n, 512, F32, "ev_dycat")
    g["w_out"], got = matmul_tn(y, dout, dm.n, "ev_dw_out", carry=comm.advance(None))
    outs, got = even_mid_bwd(u, dycat, ca, xc, a, hs, p["conv_a"], p["conv_b"], p["rw"], p["r_b"], p["iw"], p["i_b"],
                             p["lam"], dm.nb, dm.tp, dm.n, "ev_mid_bwd", carry=comm.advance(got))
    du, g["conv_a"], g["conv_b"], g["conv_b_bias"], drw, g["r_b"], diw, g["i_b"], g["lam"] = outs
    g["r_w"] = block_diag_part(drw, LRU_HEADS)
    g["i_w"] = block_diag_part(diw, LRU_HEADS)
    g["w_in"], got = matmul_tn(hn, du, dm.n, "ev_dw_in", carry=comm.advance(got))
    comm.advance(got)
    dx, g["norm"] = matmul_nt_normbwd(du, p["w_in"], h, 0, p["norm"], dout, dm.n, 512, F32, "ev_dx")
    return dx, g


def odd_fwd(h, p, tabq, tabk, dm, comm):
    nt = dm.tp // dm.n
    (u, hn), _ = norm_matmul(h, 0, D_MODEL, p["norm"], p["w_in_p"], dm.n, ODD_PAD, F32, "od_in")
    tab_spec = pl.BlockSpec((dm.n, HP), lambda i, j: (i % nt, 0))
    (q, cqn), _ = norm_matmul(u, 0, Q_LORA, p["q_norm"], p["w_uq_p"], dm.n, 512, MXU_DT, "od_q",
                              epi=_q_rope_epi, epi_ops=(tabq,), epi_specs=(tab_spec,))
    kr_spec = pl.BlockSpec((dm.n, HP), lambda i, j: (i, ODD_KR_COL))
    (k, ckvn), _ = norm_matmul(u, ODD_CKV_COL, KV_LORA, p["kv_norm"], p["w_uk_p"], dm.n, 512, MXU_DT, "od_k",
                               epi=_k_rope_epi, epi_ops=(u, tabk), epi_specs=(kr_spec, tab_spec))
    (v, _), _ = norm_matmul(u, ODD_CKV_COL, KV_LORA, p["kv_norm"], p["w_uv_p"], dm.n, 512, MXU_DT, "od_v")
    (o, lse), got = attn_fwd(q, k, v, dm.nb, dm.tp, "od_attn", carry=comm.advance(None))
    comm.advance(got)
    out = matmul_res(o[None], p["w_out"][None], h, dm.n, 512, "od_out")
    return out, (h, u, hn, cqn, ckvn, q, k, v, o, lse)


def odd_bwd(dout, saved, p, tabq, tabk, dm, comm):
    h, u, hn, cqn, ckvn, q, k, v, o, lse = saved
    g = {}
    do = matmul_nt(dout, p["w_out"], dm.n, 512, MXU_DT, "od_do")
    g["w_out"], got = matmul_tn(o, dout, dm.n, "od_dw_out", carry=comm.advance(None))
    (dq, dk, dv), got = attn_bwd(q, k, v, o, do, lse, dm.nb, dm.tp, "od_attn_bwd", carry=comm.advance(got))
    dqa, dkv, dkr = rope_bwd(dq, dk, dv, tabq, tabk, dm.tp, dm.n, "od_rope_bwd")
    g["w_uq_p"], got = matmul_tn(cqn, dqa, dm.n, "od_dw_uq", carry=comm.advance(got))
    comm.advance(got)
    g["w_ukv_p"], _ = matmul_tn(ckvn, dkv, dm.n, "od_dw_ukv")
    dcq, g["q_norm"] = matmul_nt_normbwd(dqa, p["w_uq_p"], u, 0, p["q_norm"], None, dm.n, 512, MXU_DT, "od_dcq")
    dckv, g["kv_norm"] = matmul_nt_normbwd(dkv, p["w_ukv_p"], u, ODD_CKV_COL, p["kv_norm"], None, dm.n, 512, MXU_DT,
                                           "od_dckv")
    du = jnp.concatenate([dcq, jnp.zeros((dm.rows, 128), MXU_DT), dckv, dkr], axis=1)
    g["w_in_p"], _ = matmul_tn(hn, du, dm.n, "od_dw_in")
    dx, g["norm"] = matmul_nt_normbwd(du, p["w_in_p"], h, 0, p["norm"], dout, dm.n, ODD_PAD, F32, "od_dx")
    return dx, g


def ffn_fwd(h, p, dm, comm):
    (up, hn), got = norm_matmul(h, 0, D_MODEL, p["norm"], p["w_up"], dm.n, D_FF // 2, MXU_DT, "ffn_up",
                                carry=comm.advance(None))
    comm.advance(got)
    u, y = ffn_mid_fwd(up, p["cw"], p["cb"], dm.nb, dm.tp, dm.n, "ffn_mid")
    out = matmul_res(y[None], p["w_down"][None], h, dm.n, 512, "ffn_down")
    return out, (h, up, hn, u, y)


def ffn_bwd(dout, saved, p, dm, comm):
    h, up, hn, u, y = saved
    g = {}
    dy = matmul_nt(dout, p["w_down"], dm.n, D_FF // 2, MXU_DT, "ffn_dy")
    g["w_down"], got = matmul_tn(y, dout, dm.n, "ffn_dw_down", carry=comm.advance(None))
    (dup, g["cw"], g["cb"]), got = ffn_mid_bwd(dy, u, up, p["cw"], dm.nb, dm.tp, dm.n, "ffn_mid_bwd",
                                               carry=comm.advance(got))
    g["w_up"], got = matmul_tn(hn, dup, dm.n, "ffn_dw_up", carry=comm.advance(got), col_shards=4)
    comm.advance(got)
    dx, g["norm"] = matmul_nt_normbwd(dup, p["w_up"], h, 0, p["norm"], dout, dm.n, D_FF // 2, F32, "ffn_dx")
    return dx, g


def _row(v):
    return v.reshape(1, -1)


def even_params(wf, j):
    return dict(norm=_row(wf["ev_norm"][j]), w_in=wf["ev_w_in"], conv_a=wf["ev_conv_a"][j], conv_b=wf["ev_conv_b"][j],
                conv_b_bias=_row(wf["ev_conv_b_bias"][j]), rw=block_diag(wf["ev_gate_r_w"][j]).astype(MXU_DT),
                r_b=_row(wf["ev_gate_r_b"][j]), iw=block_diag(wf["ev_gate_i_w"][j]).astype(MXU_DT),
                i_b=_row(wf["ev_gate_i_b"][j]), lam=_row(wf["ev_lru_lambda"][j]), w_out=wf["ev_w_out"])


def odd_params(wf, j):
    wkv = ukv_padded(wf["od_w_ukv"])
    return dict(norm=_row(wf["od_norm"][j]), w_in_p=odd_w_in_padded(wf["od_w_in"]), q_norm=_row(wf["od_q_norm"][j]),
                kv_norm=_row(wf["od_kv_norm"][j]), w_uq_p=uq_padded(wf["od_w_uq"]), w_ukv_p=wkv,
                w_uk_p=wkv[:, :MLA_HEADS * HP], w_uv_p=wkv[:, MLA_HEADS * HP:], w_out=wf["od_w_out"])


def ffn_params(wf, layer):
    return dict(norm=_row(wf["ffn_norm"][layer]), w_up=wf["ffn_w_up"],
                cw=jnp.moveaxis(wf["ffn_conv_w"][layer].reshape(3, 2, D_FF), 1, 0),
                cb=wf["ffn_conv_b"][layer].reshape(2, 1, D_FF), w_down=wf["ffn_w_down"])


def even_grads(g):
    out = {"ev_" + k_: g[k_] for k_ in ("w_in", "conv_a", "conv_b", "w_out")}
    out.update({"ev_norm": g["norm"][0], "ev_conv_b_bias": g["conv_b_bias"][0], "ev_gate_r_w": g["r_w"],
                "ev_gate_r_b": g["r_b"][0], "ev_gate_i_w": g["i_w"], "ev_gate_i_b": g["i_b"][0],
                "ev_lru_lambda": g["lam"][0]})
    return out


def odd_grads(g):
    return {"od_norm": g["norm"][0], "od_q_norm": g["q_norm"][0], "od_kv_norm": g["kv_norm"][0],
            "od_w_in": odd_w_in_unpad(g["w_in_p"]), "od_w_uq": uq_unpad(g["w_uq_p"]),
            "od_w_ukv": ukv_unpad(g["w_ukv_p"]), "od_w_out": g["w_out"]}


def ffn_grads(g):
    return {"ffn_norm": g["norm"][0], "ffn_w_up": g["w_up"], "ffn_conv_w": jnp.moveaxis(g["cw"], 0, 1).reshape(3, 2 * D_FF),
            "ffn_conv_b": g["cb"].reshape(2 * D_FF), "ffn_w_down": g["w_down"]}


WEIGHTS = ["meta_tokens", "ev_norm", "ev_w_in", "ev_conv_a", "ev_conv_b", "ev_conv_b_bias", "ev_gate_r_w", "ev_gate_r_b",
           "ev_gate_i_w", "ev_gate_i_b", "ev_lru_lambda", "ev_w_out", "od_norm", "od_w_in", "od_q_norm", "od_kv_norm",
           "od_w_uq", "od_w_ukv", "od_w_out", "ffn_norm", "ffn_w_up", "ffn_conv_w", "ffn_conv_b", "ffn_w_down",
           "final_norm"]
SHARD_AXIS = {"meta_tokens": 1, "ev_w_in": 2, "ev_conv_a": 2, "ev_conv_b": 2, "ev_w_out": 1, "od_norm": 1, "od_w_in": 1,
              "od_q_norm": 1, "od_kv_norm": 1, "od_w_uq": 2, "od_w_ukv": 2, "od_w_out": 1, "ffn_w_up": 2,
              "ffn_conv_w": 2, "ffn_w_down": 1}
MATMUL_WEIGHTS = ["ev_w_in", "ev_w_out", "od_w_in", "od_w_uq", "od_w_ukv", "od_w_out", "ffn_w_up", "ffn_w_down"]


LAYER_ORDER = [("ev", 0), ("ffn", 0), ("od", 0), ("ffn", 1), ("ev", 1), ("ffn", 2), ("od", 1), ("ffn", 3)]
LAYER_MATMUL = {"ev": ["ev_w_in", "ev_w_out"], "od": ["od_w_in", "od_w_uq", "od_w_ukv", "od_w_out"],
                "ffn": ["ffn_w_up", "ffn_w_down"]}
LAYER_SHARDED = {"ev": ["ev_w_in", "ev_conv_a", "ev_conv_b", "ev_w_out"],
                 "od": ["od_norm", "od_w_in", "od_q_norm", "od_kv_norm", "od_w_uq", "od_w_ukv", "od_w_out"],
                 "ffn": ["ffn_w_up", "ffn_conv_w", "ffn_w_down"]}
STACKED_SHARDS = "ffn_w_up"


def gather_all_layers(w, names, name):
    buf, offs = pack_rows([w[n] for n in names], 0, 32)
    got = run_stage(stage_gather_chips(buf.reshape(2, buf.shape[0] // 2, LANE)), name)[0]
    got = got.reshape(4, buf.shape[0], LANE)
    return {n: unshard(unpack_rows(got, off, w[n].shape), SHARD_AXIS[n]) for n, off in zip(names, offs)}


class GatherComm:
    def __init__(self, w, kind, idx):
        self.names = LAYER_MATMUL[kind]
        arrs = [w[n][idx].astype(MXU_DT) for n in self.names]
        self.shapes = [a.shape for a in arrs]
        buf, self.offs = pack_rows(arrs, 0, 32)
        self.rows = buf.shape[0]
        self.stage = stage_gather_chips(buf.reshape(2, self.rows // 2, LANE))
        self.step, self.got = 0, None

    def advance(self, carried):
        self.step += 1
        if self.step == 1:
            return self.stage
        if self.step == 2:
            self.got = carried[0]
        return None

    def run_alone(self, name):
        self.advance(run_stage(self.advance(None), name))

    def weights(self):
        got = self.got.reshape(4, self.rows, LANE)
        out = {}
        for n, off, shape in zip(self.names, self.offs, self.shapes):
            stacked = unpack_rows(got, off, shape)
            out[n] = stacked if n == STACKED_SHARDS else unshard(stacked, SHARD_AXIS[n] - 1)
        return out


class ReduceComm:
    def __init__(self, grads, axes, c_idx, tag, tail=None):
        self.names = list(grads)
        shards = [grads[n] if n == STACKED_SHARDS else to_shards(grads[n], axes[n]) for n in self.names]
        self.shapes = [s.shape[1:] for s in shards]
        gs, self.offs = pack_rows(shards, 1, 16)
        self.rs = gs.shape[1] // 2
        parts = [gs.reshape(4, 2, self.rs, LANE)]
        self.rr = 0
        if tail is not None:
            self.rr = tail.shape[0] // 8
            parts.append(tail.reshape(4, 2, self.rr, LANE))
        rows = _round_up(self.rs + self.rr, SUM_ROWS)
        if rows > self.rs + self.rr:
            parts.append(jnp.zeros((4, 2, rows - self.rs - self.rr, LANE), F32))
        self.g4 = jnp.concatenate(parts, axis=2)
        self.c_idx, self.tag, self.step, self.both = c_idx, tag, 0, None

    def advance(self, carried):
        self.step += 1
        if self.step == 1:
            return stage_pair_exchange(self.g4)
        if self.step == 2:
            return stage_chip_scatter(pair_sum(self.g4, carried[0], self.c_idx, "grad_pair_sum_" + self.tag))
        if self.step == 3:
            return stage_pair_gather(chip_sum(carried[0], "grad_chip_sum_" + self.tag))
        if self.step == 4:
            self.both = carried[0]
        return None

    def run_alone(self, name):
        stage = self.advance(None)
        while stage is not None:
            stage = self.advance(run_stage(stage, name + "_%d" % self.step))

    def results(self):
        flat = self.both[:, :self.rs].reshape(2 * self.rs, LANE)
        out = {n: unpack_rows(flat, off, shape) for n, off, shape in zip(self.names, self.offs, self.shapes)}
        return out, self.both[:, self.rs:self.rs + self.rr]


def kernel(x, meta_tokens, ev_norm, ev_w_in, ev_conv_a, ev_conv_b, ev_conv_b_bias, ev_gate_r_w, ev_gate_r_b, ev_gate_i_w, ev_gate_i_b, ev_lru_lambda, ev_w_out, od_norm, od_w_in, od_q_norm, od_kv_norm, od_w_uq, od_w_ukv, od_w_out, ffn_norm, ffn_w_up, ffn_conv_w, ffn_conv_b, ffn_w_down, final_norm, loss_target, m_meta_tokens, m_ev_norm, m_ev_w_in, m_ev_conv_a, m_ev_conv_b, m_ev_conv_b_bias, m_ev_gate_r_w, m_ev_gate_r_b, m_ev_gate_i_w, m_ev_gate_i_b, m_ev_lru_lambda, m_ev_w_out, m_od_norm, m_od_w_in, m_od_q_norm, m_od_kv_norm, m_od_w_uq, m_od_w_ukv, m_od_w_out, m_ffn_norm, m_ffn_w_up, m_ffn_conv_w, m_ffn_conv_b, m_ffn_w_down, m_final_norm, v_meta_tokens, v_ev_norm, v_ev_w_in, v_ev_conv_a, v_ev_conv_b, v_ev_conv_b_bias, v_ev_gate_r_w, v_ev_gate_r_b, v_ev_gate_i_w, v_ev_gate_i_b, v_ev_lru_lambda, v_ev_w_out, v_od_norm, v_od_w_in, v_od_q_norm, v_od_kv_norm, v_od_w_uq, v_od_w_ukv, v_od_w_out, v_ffn_norm, v_ffn_w_up, v_ffn_conv_w, v_ffn_conv_b, v_ffn_w_down, v_final_norm):
    given = dict(locals())
    w = {n: given[n] for n in WEIGHTS}
    nb, seq, _ = x.shape
    dm = Dims(nb, seq)
    n_layers = len(LAYER_ORDER)

    wf = {n: w[n] for n in WEIGHTS if n not in SHARD_AXIS}
    wf.update(gather_all_layers(w, [n for n in SHARD_AXIS if n not in MATMUL_WEIGHTS], "gather_small_weights"))
    gathers = [GatherComm(w, kind, idx) for kind, idx in LAYER_ORDER]
    gathers[0].run_alone("gather_first_layer")

    tail = dm.tp - dm.t_real
    meta = jnp.broadcast_to(wf["meta_tokens"][None], (nb, N_META, D_MODEL))
    h = jnp.concatenate([meta, x, jnp.zeros((nb, tail, D_MODEL), F32)], axis=1).reshape(dm.rows, D_MODEL)
    tgt = jnp.pad(loss_target, ((0, 0), (N_META, tail), (0, 0))).reshape(dm.rows, D_MODEL)
    tabq, tabk = rope_tables(dm.tp)

    params, saved = [], []
    for i, (kind, idx) in enumerate(LAYER_ORDER):
        wl = dict(wf)
        wl.update(gathers[i].weights())
        comm = gathers[i + 1] if i + 1 < n_layers else NoComm()
        if kind == "ev":
            p = even_params(wl, idx)
            h, sv = even_fwd(h, p, dm, comm)
        elif kind == "od":
            p = odd_params(wl, idx)
            h, sv = odd_fwd(h, p, tabq, tabk, dm, comm)
        else:
            p = ffn_params(wl, idx)
            h, sv = ffn_fwd(h, p, dm, comm)
        params.append(p)
        saved.append(sv)

    dh, loss, dfinal = loss_head(h, tgt, _row(wf["final_norm"]), dm.tp, dm.t_real, dm.n, "loss_head")
    loss = lax.psum(loss[0, 0], ("x", "y", "c"))

    c_idx = lax.axis_index("c").astype(jnp.int32).reshape(1)
    layer_grads = {n: {} for n in WEIGHTS}
    pending, reduces = NoComm(), []
    for i in reversed(range(n_layers)):
        kind, idx = LAYER_ORDER[i]
        if kind == "ev":
            dh, g = even_bwd(dh, saved[i], params[i], dm, pending)
            g = even_grads(g)
        elif kind == "od":
            dh, g = odd_bwd(dh, saved[i], params[i], tabq, tabk, dm, pending)
            g = odd_grads(g)
        else:
            dh, g = ffn_bwd(dh, saved[i], params[i], dm, pending)
            g = ffn_grads(g)
        for n in g:
            if n not in SHARD_AXIS:
                layer_grads[n][idx] = g[n]
        if i > 0:
            pending = ReduceComm({n: g[n] for n in LAYER_SHARDED[kind]}, {n: SHARD_AXIS[n] - 1 for n in SHARD_AXIS},
                                 c_idx, "%s%d" % (kind, idx))
            reduces.append((pending, idx))
    dh3 = dh.reshape(nb, dm.tp, D_MODEL)
    grad_x = dh3[:, N_META:dm.t_real]

    repl = [n for n in WEIGHTS if n not in SHARD_AXIS]
    layer_grads["final_norm"] = {0: dfinal[0]}
    repl_full = {n: (layer_grads[n][0] if n == "final_norm" else
                     jnp.stack([layer_grads[n][j] for j in range(w[n].shape[0])], axis=0)) for n in repl}
    tail_buf, tail_offs = pack_rows([repl_full[n] for n in repl], 0, 64)
    first = {n: g[n] for n in LAYER_SHARDED["ev"]}
    first["meta_tokens"] = jnp.sum(dh3[:, :N_META], axis=0)
    axes = {n: SHARD_AXIS[n] - 1 for n in SHARD_AXIS}
    axes["meta_tokens"] = SHARD_AXIS["meta_tokens"]
    last = ReduceComm(first, axes, c_idx, "first_layer", tail=tail_buf)
    last.run_alone("grad_first_layer")
    reduces.append((last, 0))

    red = {}
    for comm, idx in reduces:
        got, tail_piece = comm.results()
        for n, v_ in got.items():
            if n == "meta_tokens":
                red[n] = v_
            else:
                layer_grads[n][idx] = v_
    tails = run_stage(stage_gather_chips(tail_piece), "grad_gather_replicated")[0].reshape(tail_buf.shape[0], LANE)
    for n, off in zip(repl, tail_offs):
        red[n] = unpack_rows(tails, off, w[n].shape)
    for n in SHARD_AXIS:
        if n != "meta_tokens":
            red[n] = jnp.stack([layer_grads[n][j] for j in range(w[n].shape[0])], axis=0)

    g_out, d_out, m_out, v_out = [], [], [], []
    for n in WEIGHTS:
        g, d, m, v = adamw(red[n], w[n], given["m_" + n], given["v_" + n], "adamw_" + n)
        g_out.append(g)
        d_out.append(d)
        m_out.append(m)
        v_out.append(v)
    return (loss, grad_x, *g_out, *d_out, *m_out, *v_out)
```

```python
import math

import jax
import jax.numpy as jnp
from jax import lax
from jax.experimental import pallas as pl
from jax.experimental.pallas import tpu as pltpu

F32 = jnp.float32
MXU_DT = jnp.bfloat16
S = jax.ShapeDtypeStruct
MESH = pl.DeviceIdType.MESH

EPS = 1e-6
D_MODEL = 1024
N_META = 16
DEPTH = 4
CONV_W = 512
LRU_W = 512
LRU_HEADS = 8
LRU_C = 8.0
EVEN_IN = 2560
MLA_HEADS = 16
QK_NOPE = 64
QK_ROPE = 32
QK_HEAD = 96
V_HEAD = 64
Q_LORA = 384
KV_LORA = 256
ROPE_BASE = 10000.0
D_FF = 2816
ODD_PAD = 896
ODD_CKV_COL = 2
ODD_KR_COL = 6
HP = 128
ATT_BLK = 384
Q_PRESCALE = QK_HEAD ** -0.5 * math.log2(math.e)
FFN_CT = 256
STRIP_ROWS = 16
LANE = 128
SUBLANE = 8
VMEM_LIMIT_MB = 52

ADAM_LR = 0.001
ADAM_B1 = 0.9
ADAM_B2 = 0.999
ADAM_EPS = 1e-08
ADAM_WD = 0.01
ADAM_STEP = 10

NT_DIMS = (((1,), (1,)), ((), ()))
TN_DIMS = (((0,), (0,)), ((), ()))


def _cp(sem):
    return pltpu.CompilerParams(dimension_semantics=sem, vmem_limit_bytes=VMEM_LIMIT_MB << 20)


def _div_tile(n, cap, mult):
    if n <= cap:
        return n
    best = None
    for t in range(mult, cap + 1, mult):
        if n % t == 0:
            best = t
    assert best is not None, (n, cap, mult)
    return best


def _round_up(n, m):
    return -(-n // m) * m


def mat_cols(arr):
    return arr.shape[1] if arr.ndim == 2 else arr.shape[0] * arr.shape[2]


def mat_width(arr):
    return arr.shape[-1]


def mat_spec(arr, tm, tw, rc):
    if arr.ndim == 2:
        return pl.BlockSpec((tm, tw), lambda *g: rc(*g))
    per = arr.shape[2] // tw
    assert arr.shape[2] % tw == 0

    def imap(*g):
        r, c = rc(*g)
        return (c // per, r, c % per)

    return pl.BlockSpec((None, tm, tw), imap)


HBM_SPEC = pl.BlockSpec(memory_space=pltpu.HBM)


class Stage:
    def __init__(self, inputs, out_shapes, sems, start, finish):
        self.inputs, self.out_shapes, self.sems, self.start, self.finish = inputs, out_shapes, sems, start, finish


def run_stage(stage, name):
    n_in, n_out = len(stage.inputs), len(stage.out_shapes)

    def body(*refs):
        ins, outs, sems = refs[:n_in], refs[n_in:n_in + n_out], refs[n_in + n_out:]
        stage.start(ins, outs, sems)
        stage.finish(ins, outs, sems)

    return pl.pallas_call(body, out_shape=list(stage.out_shapes), in_specs=[HBM_SPEC] * n_in,
                          out_specs=[HBM_SPEC] * n_out, scratch_shapes=list(stage.sems), name=name)(*stage.inputs)


def _call(body, ops, carry, *, grid, in_specs, out_specs, out_shape, scratch_shapes, sem, name):
    if carry is None:
        outs = pl.pallas_call(body, grid=grid, in_specs=in_specs, out_specs=out_specs, out_shape=out_shape,
                              scratch_shapes=scratch_shapes, compiler_params=_cp(sem), name=name)(*ops)
        return outs, None
    multi = isinstance(out_shape, (list, tuple))
    shapes = list(out_shape) if multi else [out_shape]
    ospecs = list(out_specs) if multi else [out_specs]
    n_in, n_out, n_sc = len(ops), len(shapes), len(scratch_shapes)
    c_in, c_out = len(carry.inputs), len(carry.out_shapes)

    def wrapped(*refs):
        ins, cin = refs[:n_in], refs[n_in:n_in + c_in]
        o0 = n_in + c_in
        outs, cout = refs[o0:o0 + n_out], refs[o0 + n_out:o0 + n_out + c_out]
        s0 = o0 + n_out + c_out
        scs, csems = refs[s0:s0 + n_sc], refs[s0 + n_sc:]
        first = pl.program_id(0) == 0
        last = pl.program_id(0) == grid[0] - 1
        for d in range(1, len(grid)):
            first = first & (pl.program_id(d) == 0)
            last = last & (pl.program_id(d) == grid[d] - 1)

        @pl.when(first)
        def _():
            carry.start(cin, cout, csems)

        body(*ins, *outs, *scs)

        @pl.when(last)
        def _():
            carry.finish(cin, cout, csems)

    res = pl.pallas_call(
        wrapped, grid=grid, in_specs=list(in_specs) + [HBM_SPEC] * c_in, out_specs=ospecs + [HBM_SPEC] * c_out,
        out_shape=shapes + list(carry.out_shapes), scratch_shapes=list(scratch_shapes) + list(carry.sems),
        compiler_params=_cp(("arbitrary",) * len(grid)), name=name)(*ops, *carry.inputs)
    main = res[:n_out]
    return (list(main) if multi else main[0]), list(res[n_out:])


def norm_matmul(x, xcol, kdim, gain, w, tm, tn, out_dtype, name, epi=None, epi_ops=(), epi_specs=(), carry=None):
    rows, n = x.shape[0], mat_cols(w) if w.ndim == 3 else w.shape[1]
    n_epi = len(epi_ops)
    w_spec = (pl.BlockSpec((kdim, tn), lambda i, j: (0, j)) if w.ndim == 2 else
              pl.BlockSpec((None, kdim, tn), lambda i, j: (j // (w.shape[2] // tn), 0, j % (w.shape[2] // tn))))

    def body(x_ref, g_ref, w_ref, *rest):
        epi_refs = rest[:n_epi]
        out_ref, xn_ref, xn_sc = rest[n_epi:]

        @pl.when(pl.program_id(1) == 0)
        def _():
            xv = x_ref[...]
            y = xv * lax.rsqrt(jnp.mean(xv * xv, axis=-1, keepdims=True) + EPS)
            xn = (y * g_ref[...]).astype(MXU_DT)
            xn_sc[...] = xn
            xn_ref[...] = xn

        acc = jnp.dot(xn_sc[...], w_ref[...], preferred_element_type=F32)
        if epi is not None:
            acc = epi(acc, *[r[...] for r in epi_refs])
        out_ref[...] = acc.astype(out_dtype)

    return _call(
        body, (x, gain, w, *epi_ops), carry, grid=(rows // tm, n // tn),
        in_specs=[pl.BlockSpec((tm, kdim), lambda i, j: (i, xcol)), pl.BlockSpec((1, kdim), lambda i, j: (0, 0)),
                  w_spec, *epi_specs],
        out_specs=[pl.BlockSpec((tm, tn), lambda i, j: (i, j)), pl.BlockSpec((tm, kdim), lambda i, j: (i, 0))],
        out_shape=[S((rows, n), out_dtype), S((rows, kdim), MXU_DT)],
        scratch_shapes=[pltpu.VMEM((tm, kdim), MXU_DT)], sem=("parallel", "arbitrary"), name=name)


def matmul_res(a, w, res, tm, tn, name):
    grp, rows, k = a.shape
    n = w.shape[2]

    def body(a_ref, w_ref, r_ref, o_ref):
        acc = r_ref[...]
        for g in range(grp):
            acc = acc + jnp.dot(a_ref[g], w_ref[g], preferred_element_type=F32)
        o_ref[...] = acc

    return pl.pallas_call(
        body, grid=(rows // tm, n // tn),
        in_specs=[pl.BlockSpec((grp, tm, k), lambda i, j: (0, i, 0)), pl.BlockSpec((grp, k, tn), lambda i, j: (0, 0, j)),
                  pl.BlockSpec((tm, tn), lambda i, j: (i, j))],
        out_specs=pl.BlockSpec((tm, tn), lambda i, j: (i, j)),
        out_shape=S((rows, n), F32), compiler_params=_cp(("parallel", "parallel")), name=name)(a, w, res)


def matmul_nt(a, w, tm, tn, out_dtype, name):
    rows, k = a.shape
    n = w.shape[0]

    def body(a_ref, w_ref, o_ref):
        o_ref[...] = lax.dot_general(a_ref[...].astype(MXU_DT), w_ref[...], NT_DIMS,
                                     preferred_element_type=F32).astype(out_dtype)

    return pl.pallas_call(
        body, grid=(rows // tm, n // tn),
        in_specs=[pl.BlockSpec((tm, k), lambda i, j: (i, 0)), pl.BlockSpec((tn, k), lambda i, j: (j, 0))],
        out_specs=pl.BlockSpec((tm, tn), lambda i, j: (i, j)),
        out_shape=S((rows, n), out_dtype), compiler_params=_cp(("parallel", "parallel")), name=name)(a, w)


def matmul_nt_normbwd(du, w, x, xcol, gain, res, tm, tk, out_dtype, name):
    rows, kc = du.shape[-2], mat_cols(du)
    dn = w.shape[-2]
    nk = kc // tk
    has_res = res is not None
    w_spec = (pl.BlockSpec((dn, tk), lambda i, k: (0, k)) if w.ndim == 2 else
              pl.BlockSpec((None, dn, tk), lambda i, k: (k // (w.shape[2] // tk), 0, k % (w.shape[2] // tk))))

    def body(du_ref, w_ref, x_ref, g_ref, *rest):
        if has_res:
            res_ref, dx_ref, dg_ref, acc = rest
        else:
            dx_ref, dg_ref, acc = rest
        i, k = pl.program_id(0), pl.program_id(1)

        @pl.when(k == 0)
        def _():
            acc[...] = jnp.zeros_like(acc)

        @pl.when((i == 0) & (k == 0))
        def _():
            dg_ref[...] = jnp.zeros_like(dg_ref)

        acc[...] += lax.dot_general(du_ref[...], w_ref[...], NT_DIMS, preferred_element_type=F32)

        @pl.when(k == nk - 1)
        def _():
            dhn = acc[...]
            xv = x_ref[...]
            rstd = lax.rsqrt(jnp.mean(xv * xv, axis=-1, keepdims=True) + EPS)
            xhat = xv * rstd
            dg_ref[...] += jnp.sum(dhn * xhat, axis=0, keepdims=True)
            dxh = dhn * g_ref[...]
            dx = rstd * (dxh - xhat * jnp.mean(dxh * xhat, axis=-1, keepdims=True))
            if has_res:
                dx = dx + res_ref[...]
            dx_ref[...] = dx.astype(out_dtype)

    in_specs = [mat_spec(du, tm, tk, lambda i, k: (i, k)), w_spec,
                pl.BlockSpec((tm, dn), lambda i, k: (i, xcol)), pl.BlockSpec((1, dn), lambda i, k: (0, 0))]
    ops = [du, w, x, gain]
    if has_res:
        in_specs.append(pl.BlockSpec((tm, dn), lambda i, k: (i, 0)))
        ops.append(res)
    return pl.pallas_call(
        body, grid=(rows // tm, nk), in_specs=in_specs,
        out_specs=[pl.BlockSpec((tm, dn), lambda i, k: (i, 0)), pl.BlockSpec((1, dn), lambda i, k: (0, 0))],
        out_shape=[S((rows, dn), out_dtype), S((1, dn), F32)],
        scratch_shapes=[pltpu.VMEM((tm, dn), F32)],
        compiler_params=_cp(("arbitrary", "arbitrary")), name=name)(*ops)


def matmul_tn(a, b, tr, name, carry=None, col_shards=1):
    rows, ka, nb = a.shape[-2], mat_cols(a), mat_cols(b)
    ta = _div_tile(mat_width(a), 1536, LANE)
    tb = _div_tile(mat_width(b), 1536 if ta <= 1024 else 1024, LANE)
    nr = rows // tr
    if col_shards == 1:
        out_spec, out_shape = pl.BlockSpec((ta, tb), lambda i, j, r: (i, j)), S((ka, nb), F32)
    else:
        per = nb // col_shards // tb
        assert per * tb * col_shards == nb
        out_spec = pl.BlockSpec((None, ta, tb), lambda i, j, r: (j // per, i, j % per))
        out_shape = S((col_shards, ka, nb // col_shards), F32)

    def body(a_ref, b_ref, o_ref, acc):
        r = pl.program_id(2)

        @pl.when(r == 0)
        def _():
            acc[...] = jnp.zeros_like(acc)

        acc[...] += lax.dot_general(a_ref[...].astype(MXU_DT), b_ref[...].astype(MXU_DT), TN_DIMS,
                                    preferred_element_type=F32)

        @pl.when(r == nr - 1)
        def _():
            o_ref[...] = acc[...]

    return _call(
        body, (a, b), carry, grid=(ka // ta, nb // tb, nr),
        in_specs=[mat_spec(a, tr, ta, lambda i, j, r: (r, i)), mat_spec(b, tr, tb, lambda i, j, r: (r, j))],
        out_specs=out_spec, out_shape=out_shape, scratch_shapes=[pltpu.VMEM((ta, tb), F32)],
        sem=("parallel", "parallel", "arbitrary"), name=name)


def _sigmoid(x):
    return 1.0 / (1.0 + jnp.exp(-x))


def _log1p(e):
    return jnp.where(e < 1e-3, e * (1.0 - e * (0.5 - e * (1.0 / 3.0 - 0.25 * e))), jnp.log(1.0 + e))


def _softplus(x):
    return jnp.maximum(x, 0.0) + _log1p(jnp.exp(-jnp.abs(x)))


def _expm1(x):
    series = x * (1.0 + x * (0.5 + x * (1.0 / 6.0 + x * (1.0 / 24.0 + x * (1.0 / 120.0)))))
    return jnp.where(jnp.abs(x) < 0.1, series, jnp.exp(x) - 1.0)


_GELU_K = math.sqrt(2.0 / math.pi)
_GELU_C = 0.044715


def _gelu_and_grad(x):
    th = jnp.tanh(_GELU_K * (x + _GELU_C * x * x * x))
    g = 0.5 * x * (1.0 + th)
    dg = 0.5 * (1.0 + th) + 0.5 * x * (1.0 - th * th) * _GELU_K * (1.0 + 3.0 * _GELU_C * x * x)
    return g, dg


def _row_iota(shape):
    return lax.broadcasted_iota(jnp.int32, shape, 0)


def _scan_chunk_fwd(a_sc, u_sc, out_ref, hcar, n, width):
    rowi = _row_iota((SUBLANE, width))

    def step(c, hprev):
        r0 = pl.multiple_of(c * SUBLANE, SUBLANE)
        a = a_sc[pl.ds(r0, SUBLANE), :]
        u = u_sc[pl.ds(r0, SUBLANE), :]
        for d in (1, 2, 4):
            a_s = jnp.where(rowi >= d, pltpu.roll(a, d, axis=0), 1.0)
            u_s = jnp.where(rowi >= d, pltpu.roll(u, d, axis=0), 0.0)
            u = u + a * u_s
            a = a * a_s
        h = u + a * hprev
        out_ref[pl.ds(r0, SUBLANE), :] = h
        return jnp.broadcast_to(h[SUBLANE - 1:SUBLANE, :], (SUBLANE, width))

    hcar[...] = lax.fori_loop(0, n // SUBLANE, step, hcar[...], unroll=4)


def _scan_chunk_bwd(b_sc, d_sc, out_ref, gcar, n, width):
    rowi = _row_iota((SUBLANE, width))
    nc = n // SUBLANE

    def step(c, gnext):
        r0 = pl.multiple_of((nc - 1 - c) * SUBLANE, SUBLANE)
        b = b_sc[pl.ds(r0, SUBLANE), :]
        d = d_sc[pl.ds(r0, SUBLANE), :]
        for s in (1, 2, 4):
            keep = rowi < SUBLANE - s
            b_s = jnp.where(keep, pltpu.roll(b, SUBLANE - s, axis=0), 1.0)
            d_s = jnp.where(keep, pltpu.roll(d, SUBLANE - s, axis=0), 0.0)
            d = d + b * d_s
            b = b * b_s
        g = d + b * gnext
        out_ref[pl.ds(r0, SUBLANE), :] = g
        return jnp.broadcast_to(g[0:1, :], (SUBLANE, width))

    gcar[...] = lax.fori_loop(0, nc, step, gcar[...], unroll=4)


def even_mid_fwd(u, conv_a, conv_b, conv_b_bias, rw, rb, iw, ib, lam, nb, tp, n, name, carry=None):
    rows = u.shape[0]
    w = LANE
    nj = CONV_W // w
    nt = tp // n
    h8 = SUBLANE

    def body(gb_r, gc_r, xa_r, xb_r, gate_r, ca_w, cb_w, cb_b, rw_r, rb_r, iw_r, ib_r, lam_r,
             y_o, ca_o, xc_o, a_o, hs_o, pext, xext, hcar, a_sc, u_sc):
        t = pl.program_id(2)

        @pl.when(t == 0)
        def _():
            pext[0:h8, :] = jnp.zeros((h8, w), F32)
            xext[0:h8, :] = jnp.zeros((h8, w), F32)
            hcar[...] = jnp.zeros_like(hcar)

        p = gc_r[...] * xa_r[...]
        pext[h8:h8 + n, :] = p
        wa = ca_w[...]
        ca = wa[2:3, :] * p + wa[1:2, :] * pext[h8 - 1:h8 - 1 + n, :] + wa[0:1, :] * pext[h8 - 2:h8 - 2 + n, :]
        ca_o[...] = ca
        y_o[0] = (gb_r[...] * ca).astype(MXU_DT)
        pext[0:h8, :] = pext[n:n + h8, :]

        xb = xb_r[...]
        xext[h8:h8 + n, :] = xb
        wb = cb_w[...]
        xc = (wb[3:4, :] * xb + wb[2:3, :] * xext[h8 - 1:h8 - 1 + n, :] + wb[1:2, :] * xext[h8 - 2:h8 - 2 + n, :]
              + wb[0:1, :] * xext[h8 - 3:h8 - 3 + n, :]) + cb_b[...]
        xc_o[...] = xc
        xext[0:h8, :] = xext[n:n + h8, :]

        xcm = xc.astype(MXU_DT)
        r = _sigmoid(jnp.dot(xcm, rw_r[...], preferred_element_type=F32) + rb_r[...])
        ig = _sigmoid(jnp.dot(xcm, iw_r[...], preferred_element_type=F32) + ib_r[...])
        log_a = (-LRU_C) * r * _softplus(-lam_r[...])
        a = jnp.exp(log_a)
        mult = jnp.sqrt(-_expm1(2.0 * log_a))
        a_sc[...] = a
        a_o[...] = a
        u_sc[...] = mult * (ig * xc)
        _scan_chunk_fwd(a_sc, u_sc, hs_o, hcar, n, w)
        gel, _ = _gelu_and_grad(gate_r[...])
        y_o[1] = (gel * hs_o[...]).astype(MXU_DT)

    def ublk(off):
        return pl.BlockSpec((n, w), lambda j, b, t: (b * nt + t, off + j))

    def pblk(r_):
        return pl.BlockSpec((r_, w), lambda j, b, t: (0, j))

    act = pl.BlockSpec((n, w), lambda j, b, t: (b * nt + t, j))
    mat = pl.BlockSpec((w, w), lambda j, b, t: (j, j))
    return _call(
        body, (u, u, u, u, u, conv_a, conv_b, conv_b_bias, rw, rb, iw, ib, lam), carry, grid=(nj, nb, nt),
        in_specs=[ublk(0), ublk(nj), ublk(2 * nj), ublk(3 * nj), ublk(4 * nj), pblk(3), pblk(4), pblk(1),
                  mat, pblk(1), mat, pblk(1), pblk(1)],
        out_specs=[pl.BlockSpec((2, n, w), lambda j, b, t: (0, b * nt + t, j)), act, act, act, act],
        out_shape=[S((2, rows, CONV_W), MXU_DT), S((rows, CONV_W), F32), S((rows, LRU_W), F32), S((rows, LRU_W), F32),
                   S((rows, LRU_W), F32)],
        scratch_shapes=[pltpu.VMEM((n + h8, w), F32), pltpu.VMEM((n + h8, w), F32), pltpu.VMEM((h8, w), F32),
                        pltpu.VMEM((n, w), F32), pltpu.VMEM((n, w), F32)],
        sem=("parallel", "parallel", "arbitrary"), name=name)


def even_mid_bwd(u, dycat, ca, xc, a_sv, hs, conv_a, conv_b, rw, rb, iw, ib, lam, nb, tp, n, name, carry=None):
    rows = u.shape[0]
    w = LANE
    nj = CONV_W // w
    nt = tp // n
    h8 = SUBLANE

    def body(gb_r, gc_r, xa_r, xb_r, gate_r, dya_r, dyb_r, ca_r, xc_r, a_r, hs_r, hsp_r,
             ca_w, cb_w, rw_r, rb_r, iw_r, ib_r, lam_r,
             du_o, dca_w, dcb_w, dcb_b, drw, drb, diw, dib, dlam,
             aext, hext, dext, eext, gcar, b_sc, d_sc, g_sc):
        b, t = pl.program_id(1), pl.program_id(2)

        @pl.when((b == 0) & (t == 0))
        def _():
            for ref in (dca_w, dcb_w, dcb_b, drw, drb, diw, dib, dlam):
                ref[...] = jnp.zeros_like(ref)

        @pl.when(t == 0)
        def _():
            aext[n:n + h8, :] = jnp.zeros((h8, w), F32)
            dext[n:n + h8, :] = jnp.zeros((h8, w), F32)
            eext[n:n + h8, :] = jnp.zeros((h8, w), F32)
            gcar[...] = jnp.zeros_like(gcar)

        xc_v = xc_r[...]
        xcm = xc_v.astype(MXU_DT)
        r = _sigmoid(jnp.dot(xcm, rw_r[...], preferred_element_type=F32) + rb_r[...])
        ig = _sigmoid(jnp.dot(xcm, iw_r[...], preferred_element_type=F32) + ib_r[...])
        lam_v = lam_r[...]
        sp = _softplus(-lam_v)
        log_a = (-LRU_C) * r * sp
        a = a_r[...]
        mult = jnp.sqrt(-_expm1(2.0 * log_a))
        hs_v = hs_r[...]
        gel, dgel = _gelu_and_grad(gate_r[...])
        dyb = dyb_r[...]
        du_o[4] = (dyb * hs_v * dgel).astype(MXU_DT)

        aext[0:n, :] = a
        b_sc[...] = aext[1:1 + n, :]
        d_sc[...] = dyb * gel
        _scan_chunk_bwd(b_sc, d_sc, g_sc, gcar, n, w)
        aext[n:n + h8, :] = aext[0:h8, :]
        g = g_sc[...]

        hext[0:h8, :] = jnp.where(t == nt - 1, 0.0, hsp_r[...])
        hext[h8:h8 + n, :] = hs_v
        da = g * hext[h8 - 1:h8 - 1 + n, :]
        dmult = g * (ig * xc_v)
        di = g * mult * xc_v
        dxc = g * mult * ig
        dlog_a = da * a - dmult * (a * a) / mult
        dr = dlog_a * ((-LRU_C) * sp)
        dsp = jnp.sum(dlog_a * ((-LRU_C) * r), axis=0, keepdims=True)
        dlam[...] += dsp * (-_sigmoid(-lam_v))
        dzr = dr * r * (1.0 - r)
        dzi = di * ig * (1.0 - ig)
        dzr_m = dzr.astype(MXU_DT)
        dzi_m = dzi.astype(MXU_DT)
        dxc = (dxc + lax.dot_general(dzr_m, rw_r[...], NT_DIMS, preferred_element_type=F32)
               + lax.dot_general(dzi_m, iw_r[...], NT_DIMS, preferred_element_type=F32))
        drw[...] += lax.dot_general(xcm, dzr_m, TN_DIMS, preferred_element_type=F32)
        diw[...] += lax.dot_general(xcm, dzi_m, TN_DIMS, preferred_element_type=F32)
        drb[...] += jnp.sum(dzr, axis=0, keepdims=True)
        dib[...] += jnp.sum(dzi, axis=0, keepdims=True)
        dcb_b[...] += jnp.sum(dxc, axis=0, keepdims=True)

        xb = xb_r[...]
        dext[0:n, :] = dxc
        wb = cb_w[...]
        d1, d2, d3 = dext[1:1 + n, :], dext[2:2 + n, :], dext[3:3 + n, :]
        du_o[3] = (wb[3:4, :] * dxc + wb[2:3, :] * d1 + wb[1:2, :] * d2 + wb[0:1, :] * d3).astype(MXU_DT)
        dcb_w[3:4, :] += jnp.sum(xb * dxc, axis=0, keepdims=True)
        dcb_w[2:3, :] += jnp.sum(xb * d1, axis=0, keepdims=True)
        dcb_w[1:2, :] += jnp.sum(xb * d2, axis=0, keepdims=True)
        dcb_w[0:1, :] += jnp.sum(xb * d3, axis=0, keepdims=True)
        dext[n:n + h8, :] = dext[0:h8, :]

        gb, gc, xa = gb_r[...], gc_r[...], xa_r[...]
        dya = dya_r[...]
        du_o[0] = (dya * ca_r[...]).astype(MXU_DT)
        dca = dya * gb
        eext[0:n, :] = dca
        wa = ca_w[...]
        e1, e2 = eext[1:1 + n, :], eext[2:2 + n, :]
        dp = wa[2:3, :] * dca + wa[1:2, :] * e1 + wa[0:1, :] * e2
        p = gc * xa
        dca_w[2:3, :] += jnp.sum(p * dca, axis=0, keepdims=True)
        dca_w[1:2, :] += jnp.sum(p * e1, axis=0, keepdims=True)
        dca_w[0:1, :] += jnp.sum(p * e2, axis=0, keepdims=True)
        eext[n:n + h8, :] = eext[0:h8, :]
        du_o[1] = (dp * xa).astype(MXU_DT)
        du_o[2] = (dp * gc).astype(MXU_DT)

    def rt(b, t):
        return b * nt + (nt - 1 - t)

    def ublk(off):
        return pl.BlockSpec((n, w), lambda j, b, t: (rt(b, t), off + j))

    def pblk(r_):
        return pl.BlockSpec((r_, w), lambda j, b, t: (0, j))

    act = pl.BlockSpec((n, w), lambda j, b, t: (rt(b, t), j))
    n8 = n // h8
    hsp = pl.BlockSpec((h8, w), lambda j, b, t: (jnp.maximum(rt(b, t) * n8 - 1, 0), j))
    mat = pl.BlockSpec((w, w), lambda j, b, t: (j, j))
    return _call(
        body, (u, u, u, u, u, dycat, dycat, ca, xc, a_sv, hs, hs, conv_a, conv_b, rw, rb, iw, ib, lam), carry,
        grid=(nj, nb, nt),
        in_specs=[ublk(0), ublk(nj), ublk(2 * nj), ublk(3 * nj), ublk(4 * nj), ublk(0), ublk(nj), act, act, act, act,
                  hsp, pblk(3), pblk(4), mat, pblk(1), mat, pblk(1), pblk(1)],
        out_specs=[pl.BlockSpec((5, n, w), lambda j, b, t: (0, rt(b, t), j)), pblk(3), pblk(4), pblk(1),
                   mat, pblk(1), mat, pblk(1), pblk(1)],
        out_shape=[S((5, rows, CONV_W), MXU_DT), S((3, CONV_W), F32), S((4, LRU_W), F32), S((1, LRU_W), F32),
                   S((LRU_W, LRU_W), F32), S((1, LRU_W), F32), S((LRU_W, LRU_W), F32), S((1, LRU_W), F32),
                   S((1, LRU_W), F32)],
        scratch_shapes=[pltpu.VMEM((n + h8, w), F32)] * 4 + [pltpu.VMEM((h8, w), F32)] + [pltpu.VMEM((n, w), F32)] * 3,
        sem=("arbitrary", "arbitrary", "arbitrary"), name=name)


def ffn_mid_fwd(up, cw, cb, nb, tp, n, name):
    rows = up.shape[0]
    w = FFN_CT
    nj = D_FF // w
    nt = tp // n
    h8 = SUBLANE

    sr = STRIP_ROWS

    def body(xa_r, xg_r, w_r, b_r, u_o, y_o, halo):
        t = pl.program_id(2)

        @pl.when(t == 0)
        def _():
            halo[...] = jnp.zeros_like(halo)

        wv = (w_r[0], w_r[1])
        bv = (b_r[0], b_r[1])

        def strip(s, carry):
            r0 = pl.multiple_of(s * sr, sr)
            us, new = [], []
            for g, x_r in enumerate((xa_r, xg_r)):
                x = x_r[pl.ds(r0, sr), :].astype(F32)
                win = jnp.concatenate([carry[g], x], axis=0)
                x1 = pltpu.roll(win, 1, axis=0)[h8:, :]
                x2 = pltpu.roll(win, 2, axis=0)[h8:, :]
                u = (wv[g][2:3, :] * x + wv[g][1:2, :] * x1 + wv[g][0:1, :] * x2) + bv[g]
                u_o[g, pl.ds(r0, sr), :] = u.astype(MXU_DT)
                us.append(u)
                new.append(x[sr - h8:, :])
            y_o[pl.ds(r0, sr), :] = (us[0] * _sigmoid(us[0]) * us[1]).astype(MXU_DT)
            return tuple(new)

        ha, hg = lax.fori_loop(0, n // sr, strip, (halo[0], halo[1]))
        halo[0] = ha
        halo[1] = hg

    def ublk(off):
        return pl.BlockSpec((n, w), lambda j, b, t: (b * nt + t, off + j))

    return pl.pallas_call(
        body, grid=(nj, nb, nt),
        in_specs=[ublk(0), ublk(nj), pl.BlockSpec((2, 3, w), lambda j, b, t: (0, 0, j)),
                  pl.BlockSpec((2, 1, w), lambda j, b, t: (0, 0, j))],
        out_specs=[pl.BlockSpec((2, n, w), lambda j, b, t: (0, b * nt + t, j)), ublk(0)],
        out_shape=[S((2, rows, D_FF), MXU_DT), S((rows, D_FF), MXU_DT)],
        scratch_shapes=[pltpu.VMEM((2, h8, w), F32)],
        compiler_params=_cp(("parallel", "parallel", "arbitrary")), name=name,
    )(up, up, cw, cb)


def ffn_mid_bwd(dy, u, up, cw, nb, tp, n, name, carry=None):
    rows = up.shape[0]
    w = FFN_CT
    nj = D_FF // w
    nt = tp // n
    h8 = SUBLANE

    sr = STRIP_ROWS
    ns = n // sr

    def fold(v):
        acc = v[0:h8, :]
        for k in range(1, sr // h8):
            acc = acc + v[k * h8:(k + 1) * h8, :]
        return acc

    def body(dy_r, u_r, xa_r, xg_r, w_r, dx_o, dw, db, halo):
        b, t = pl.program_id(1), pl.program_id(2)

        @pl.when((b == 0) & (t == 0))
        def _():
            dw[...] = jnp.zeros_like(dw)
            db[...] = jnp.zeros_like(db)

        @pl.when(t == 0)
        def _():
            halo[...] = jnp.zeros_like(halo)

        wv = (w_r[0], w_r[1])

        def strip(s, carry):
            halos, sums = carry
            r0 = pl.multiple_of((ns - 1 - s) * sr, sr)
            dyv = dy_r[pl.ds(r0, sr), :].astype(F32)
            ua = u_r[0, pl.ds(r0, sr), :].astype(F32)
            ug = u_r[1, pl.ds(r0, sr), :].astype(F32)
            sg = _sigmoid(ua)
            dus = (dyv * ug * (sg * (1.0 + ua * (1.0 - sg))), dyv * (ua * sg))
            new_halos, new_sums = [], []
            for g, x_r in enumerate((xa_r, xg_r)):
                du = dus[g]
                win = jnp.concatenate([du, halos[g]], axis=0)
                d1 = pltpu.roll(win, sr + h8 - 1, axis=0)[0:sr, :]
                d2 = pltpu.roll(win, sr + h8 - 2, axis=0)[0:sr, :]
                dx_o[g, pl.ds(r0, sr), :] = (wv[g][2:3, :] * du + wv[g][1:2, :] * d1 + wv[g][0:1, :] * d2).astype(MXU_DT)
                x = x_r[pl.ds(r0, sr), :].astype(F32)
                s2, s1, s0, sb = sums[g]
                new_sums.append((s2 + fold(x * du), s1 + fold(x * d1), s0 + fold(x * d2), sb + fold(du)))
                new_halos.append(du[0:h8, :])
            return tuple(new_halos), tuple(new_sums)

        z = jnp.zeros((h8, w), F32)
        halos, sums = lax.fori_loop(0, ns, strip, ((halo[0], halo[1]), ((z, z, z, z), (z, z, z, z))))
        halo[0] = halos[0]
        halo[1] = halos[1]
        for g in range(2):
            s2, s1, s0, sb = sums[g]
            dw[g, 2:3, :] += jnp.sum(s2, axis=0, keepdims=True)
            dw[g, 1:2, :] += jnp.sum(s1, axis=0, keepdims=True)
            dw[g, 0:1, :] += jnp.sum(s0, axis=0, keepdims=True)
            db[g] += jnp.sum(sb, axis=0, keepdims=True)

    def rt(b, t):
        return b * nt + (nt - 1 - t)

    def ublk(off):
        return pl.BlockSpec((n, w), lambda j, b, t: (rt(b, t), off + j))

    pair = pl.BlockSpec((2, n, w), lambda j, b, t: (0, rt(b, t), j))
    return _call(
        body, (dy, u, up, up, cw), carry, grid=(nj, nb, nt),
        in_specs=[ublk(0), pair, ublk(0), ublk(nj), pl.BlockSpec((2, 3, w), lambda j, b, t: (0, 0, j))],
        out_specs=[pair, pl.BlockSpec((2, 3, w), lambda j, b, t: (0, 0, j)),
                   pl.BlockSpec((2, 1, w), lambda j, b, t: (0, 0, j))],
        out_shape=[S((2, rows, D_FF), MXU_DT), S((2, 3, D_FF), F32), S((2, 1, D_FF), F32)],
        scratch_shapes=[pltpu.VMEM((2, h8, w), F32)],
        sem=("arbitrary", "arbitrary", "arbitrary"), name=name)


def _lane_mod(shape):
    return lax.broadcasted_iota(jnp.int32, shape, 1) & (HP - 1)


def _q_rope_epi(acc, tab):
    reps = acc.shape[1] // HP
    a = acc * jnp.tile(tab, (1, reps))
    lane = _lane_mod(a.shape)
    shifted = pltpu.roll(a, a.shape[1] - QK_ROPE, axis=1)
    return jnp.where(lane < QK_NOPE, a, jnp.where(lane < QK_HEAD, a + shifted, 0.0)) * Q_PRESCALE


def _k_rope_block(krblk, tabk):
    a = krblk * tabk
    lane = _lane_mod(a.shape)
    b = a + pltpu.roll(a, HP - QK_ROPE, axis=1)
    return jnp.where((lane >= QK_NOPE) & (lane < QK_HEAD), b, 0.0)


def _k_rope_epi(acc, krblk, tabk):
    reps = acc.shape[1] // HP
    return acc + jnp.tile(_k_rope_block(krblk, tabk), (1, reps))


def attn_fwd(q, k, v, nb, tp, name, carry=None):
    rows = q.shape[0]
    blk = ATT_BLK
    nq = tp // blk
    npair = MLA_HEADS // 2

    def body(q_r, k_r, v_r, o_r, lse_r):
        qi = pl.program_id(2)
        lane = lax.broadcasted_iota(jnp.int32, (blk, LANE), 1)
        even = lane < V_HEAD
        sum_lane = (V_HEAD, 0)
        rowi = lax.broadcasted_iota(jnp.int32, (blk, blk), 0)
        coli = lax.broadcasted_iota(jnp.int32, (blk, blk), 1)
        qs = [q_r[:, h * HP:(h + 1) * HP] for h in range(2)]

        def kv_block(k0, width, carry, diagonal):
            ms, accs = carry
            vblk = v_r[pl.ds(k0, width), :]
            one = jnp.ones_like(vblk)
            zero = jnp.zeros_like(vblk)
            vlane = lax.broadcasted_iota(jnp.int32, (width, LANE), 1)
            new_ms, new_accs = [], []
            for h in range(2):
                kh = k_r[pl.ds(k0, width), h * HP:(h + 1) * HP]
                s = lax.dot_general(qs[h], kh, NT_DIMS, preferred_element_type=F32)
                if diagonal:
                    s = jnp.where(coli <= rowi, s, -jnp.inf)
                m_new = jnp.maximum(ms[h], jnp.max(s, axis=1, keepdims=True))
                alpha = jnp.exp2(ms[h] - m_new)
                p = jnp.exp2(s - m_new).astype(MXU_DT)
                mine = (vlane < V_HEAD) if h == 0 else (vlane >= V_HEAD)
                vh = jnp.where(mine, vblk, jnp.where(vlane == sum_lane[h], one, zero))
                new_accs.append(alpha * accs[h] + jnp.dot(p, vh, preferred_element_type=F32))
                new_ms.append(m_new)
            return tuple(new_ms), tuple(new_accs)

        neg = jnp.full((blk, 1), -jnp.inf, F32)
        zacc = jnp.zeros((blk, LANE), F32)
        carry = lax.fori_loop(0, qi // 2, lambda i, c: kv_block(pl.multiple_of(i * 2 * blk, blk), 2 * blk, c, False),
                              ((neg, neg), (zacc, zacc)))
        carry = lax.cond(qi % 2 == 1, lambda c: kv_block(pl.multiple_of((qi - 1) * blk, blk), blk, c, False),
                         lambda c: c, carry)
        ms, accs = kv_block(pl.multiple_of(qi * blk, blk), blk, carry, True)
        ls = [accs[h][:, sum_lane[h]:sum_lane[h] + 1] for h in range(2)]
        o_r[...] = jnp.where(even, accs[0] / ls[0], accs[1] / ls[1]).astype(MXU_DT)
        lse_r[...] = jnp.where(even, ms[0] + jnp.log2(ls[0]), ms[1] + jnp.log2(ls[1]))

    return _call(
        body, (q, k, v), carry, grid=(nb, npair, nq),
        in_specs=[pl.BlockSpec((blk, 2 * HP), lambda b, p, i: (b * nq + i, p)),
                  pl.BlockSpec((tp, 2 * HP), lambda b, p, i: (b, p)),
                  pl.BlockSpec((tp, LANE), lambda b, p, i: (b, p))],
        out_specs=[pl.BlockSpec((blk, LANE), lambda b, p, i: (b * nq + i, p)),
                   pl.BlockSpec((None, blk, LANE), lambda b, p, i: (p, b * nq + i, 0))],
        out_shape=[S((rows, MLA_HEADS * V_HEAD), MXU_DT), S((npair, rows, LANE), F32)], scratch_shapes=[],
        sem=("parallel", "parallel", "arbitrary"), name=name)


def attn_bwd(q, k, v, o, do, lse, nb, tp, name, carry=None):
    rows = q.shape[0]
    blk = ATT_BLK
    nq = tp // blk
    npair = MLA_HEADS // 2
    scale = QK_HEAD ** -0.5

    def body(q_r, k_r, v_r, o_r, do_r, lse_r, dq_o, dk_o, dv_o, dq_acc, delta_sc):
        kb = pl.program_id(2)
        even = lax.broadcasted_iota(jnp.int32, (blk, LANE), 1) < V_HEAD
        rowi = lax.broadcasted_iota(jnp.int32, (blk, blk), 0)
        coli = lax.broadcasted_iota(jnp.int32, (blk, blk), 1)

        @pl.when(kb == 0)
        def _():
            dq_acc[...] = jnp.zeros_like(dq_acc)

            def dstep(i, c):
                r0 = pl.multiple_of(i * blk, blk)
                prod = do_r[pl.ds(r0, blk), :].astype(F32) * o_r[pl.ds(r0, blk), :].astype(F32)
                de = jnp.sum(jnp.where(even, prod, 0.0), axis=1, keepdims=True)
                dd = jnp.sum(jnp.where(even, 0.0, prod), axis=1, keepdims=True)
                delta_sc[pl.ds(r0, blk), :] = jnp.where(even, de, dd)
                return c

            lax.fori_loop(0, nq, dstep, 0)

        kblk, vblk = k_r[...], v_r[...]
        klane = lax.broadcasted_iota(jnp.int32, (blk, 2 * HP), 1)
        kdiag = jnp.concatenate([jnp.where(klane < HP, kblk, jnp.zeros_like(kblk)),
                                 jnp.where(klane >= HP, kblk, jnp.zeros_like(kblk))], axis=0)
        vdiag = jnp.concatenate([jnp.where(even, vblk, jnp.zeros_like(vblk)),
                                 jnp.where(even, jnp.zeros_like(vblk), vblk)], axis=0)

        def q_block(r0, height, carry, diagonal):
            dk0, dk1, dv = carry
            dob = do_r[pl.ds(r0, height), :]
            lse_b = lse_r[pl.ds(r0, height), :]
            dl_b = delta_sc[pl.ds(r0, height), :]
            qlane = lax.broadcasted_iota(jnp.int32, (height, LANE), 1)
            qpair = q_r[pl.ds(r0, height), :]
            s2 = lax.dot_general(qpair, kdiag, NT_DIMS, preferred_element_type=F32)
            dp2 = lax.dot_general(dob, vdiag, NT_DIMS, preferred_element_type=F32)
            dks, dss = [dk0, dk1], []
            for h in range(2):
                lo = 0 if h == 0 else V_HEAD
                p = jnp.exp2(s2[:, h * blk:(h + 1) * blk] - lse_b[:, lo:lo + 1])
                if diagonal:
                    p = jnp.where(coli <= rowi, p, 0.0)
                ds = (p * (dp2[:, h * blk:(h + 1) * blk] - dl_b[:, lo:lo + 1])).astype(MXU_DT)
                mine = (qlane < V_HEAD) if h == 0 else (qlane >= V_HEAD)
                doh = jnp.where(mine, dob, jnp.zeros_like(dob))
                dv = dv + lax.dot_general(p.astype(MXU_DT), doh, TN_DIMS, preferred_element_type=F32)
                dks[h] = dks[h] + lax.dot_general(ds, qpair[:, h * HP:(h + 1) * HP], TN_DIMS, preferred_element_type=F32)
                dss.append(ds)
            dq_acc[pl.ds(r0, height), :] += jnp.dot(jnp.concatenate(dss, axis=1), kdiag, preferred_element_type=F32)
            return dks[0], dks[1], dv

        z = jnp.zeros((blk, HP), F32)
        carry = q_block(pl.multiple_of(kb * blk, blk), blk, (z, z, jnp.zeros((blk, LANE), F32)), True)
        below = nq - 1 - kb
        carry = lax.fori_loop(
            0, below // 2, lambda i, c: q_block(pl.multiple_of((kb + 1 + 2 * i) * blk, blk), 2 * blk, c, False), carry)
        dk0, dk1, dv = lax.cond(below % 2 == 1, lambda c: q_block(pl.multiple_of((nq - 1) * blk, blk), blk, c, False),
                                lambda c: c, carry)
        dk_o[:, 0:HP] = (dk0 * (scale / Q_PRESCALE)).astype(MXU_DT)
        dk_o[:, HP:2 * HP] = (dk1 * (scale / Q_PRESCALE)).astype(MXU_DT)
        dv_o[...] = dv.astype(MXU_DT)

        @pl.when(kb == nq - 1)
        def _():
            dq_o[...] = (dq_acc[...] * scale).astype(MXU_DT)

    seq_pair = pl.BlockSpec((tp, LANE), lambda b, p, kk: (b, p))
    return _call(
        body, (q, k, v, o, do, lse), carry, grid=(nb, npair, nq),
        in_specs=[pl.BlockSpec((tp, 2 * HP), lambda b, p, kk: (b, p)),
                  pl.BlockSpec((blk, 2 * HP), lambda b, p, kk: (b * nq + kk, p)),
                  pl.BlockSpec((blk, LANE), lambda b, p, kk: (b * nq + kk, p)),
                  seq_pair, seq_pair, pl.BlockSpec((None, tp, LANE), lambda b, p, kk: (p, b, 0))],
        out_specs=[pl.BlockSpec((tp, 2 * HP), lambda b, p, kk: (b, p)),
                   pl.BlockSpec((blk, 2 * HP), lambda b, p, kk: (b * nq + kk, p)),
                   pl.BlockSpec((blk, LANE), lambda b, p, kk: (b * nq + kk, p))],
        out_shape=[S((rows, MLA_HEADS * HP), MXU_DT), S((rows, MLA_HEADS * HP), MXU_DT),
                   S((rows, MLA_HEADS * V_HEAD), MXU_DT)],
        scratch_shapes=[pltpu.VMEM((tp, 2 * HP), F32), pltpu.VMEM((tp, LANE), F32)],
        sem=("parallel", "parallel", "arbitrary"), name=name)


def rope_bwd(dq, dk, dv, tabq, tabk, tp, tm, name):
    rows = dq.shape[0]
    nt = tp // tm
    wq = MLA_HEADS * HP

    def body(dq_r, dk_r, dv_r, tq_r, tk_r, dqa_o, dkv_o, dkr_o):
        dqv = dq_r[...].astype(F32)
        lane = _lane_mod(dqv.shape)
        in_rope = (lane >= QK_NOPE) & (lane < QK_HEAD)
        rope = jnp.where(in_rope, dqv, 0.0)
        da = jnp.where(lane < QK_HEAD, dqv, 0.0) + pltpu.roll(rope, QK_ROPE, axis=1)
        dqa_o[...] = (da * jnp.tile(tq_r[...], (1, MLA_HEADS))).astype(MXU_DT)
        dkf = dk_r[...].astype(F32)
        dkv_o[:, 0:wq] = jnp.where(lane < QK_NOPE, dkf, 0.0).astype(MXU_DT)
        dkv_o[:, wq:] = dv_r[...]
        kr = jnp.where(in_rope, dkf, 0.0)
        tot = kr[:, 0:HP]
        for h in range(1, MLA_HEADS):
            tot = tot + kr[:, h * HP:(h + 1) * HP]
        dkr_o[...] = ((tot + pltpu.roll(tot, QK_ROPE, axis=1)) * tk_r[...]).astype(MXU_DT)

    def rowblk(wd):
        return pl.BlockSpec((tm, wd), lambda i: (i, 0))

    tab = pl.BlockSpec((tm, HP), lambda i: (i % nt, 0))
    return pl.pallas_call(
        body, grid=(rows // tm,), in_specs=[rowblk(wq), rowblk(wq), rowblk(MLA_HEADS * V_HEAD), tab, tab],
        out_specs=[rowblk(wq), rowblk(wq + MLA_HEADS * V_HEAD), rowblk(HP)],
        out_shape=[S((rows, wq), MXU_DT), S((rows, wq + MLA_HEADS * V_HEAD), MXU_DT), S((rows, HP), MXU_DT)],
        compiler_params=_cp(("parallel",)), name=name)(dq, dk, dv, tabq, tabk)


def loss_head(h, target, gain, tp, t_real, tm, name):
    rows = h.shape[0]
    nt = tp // tm

    def body(h_r, t_r, g_r, dh_o, loss_o, dg_o):
        i = pl.program_id(0)

        @pl.when(i == 0)
        def _():
            loss_o[...] = jnp.zeros_like(loss_o)
            dg_o[...] = jnp.zeros_like(dg_o)

        xv = h_r[...]
        rstd = lax.rsqrt(jnp.mean(xv * xv, axis=-1, keepdims=True) + EPS)
        xhat = xv * rstd
        g = g_r[...]
        pos = (i % nt) * tm + lax.broadcasted_iota(jnp.int32, (tm, 1), 0)
        valid = (pos >= N_META) & (pos < t_real)
        err = jnp.where(valid, xhat * g - t_r[...], 0.0)
        loss_o[...] += 0.5 * jnp.sum(jnp.mean(err * err, axis=-1, keepdims=True))
        dy = err * (1.0 / D_MODEL)
        dg_o[...] += jnp.sum(dy * xhat, axis=0, keepdims=True)
        dxh = dy * g
        dh_o[...] = rstd * (dxh - xhat * jnp.mean(dxh * xhat, axis=-1, keepdims=True))

    blk = pl.BlockSpec((tm, D_MODEL), lambda i: (i, 0))
    return pl.pallas_call(
        body, grid=(rows // tm,), in_specs=[blk, blk, pl.BlockSpec((1, D_MODEL), lambda i: (0, 0))],
        out_specs=[blk, pl.BlockSpec((1, LANE), lambda i: (0, 0)), pl.BlockSpec((1, D_MODEL), lambda i: (0, 0))],
        out_shape=[S((rows, D_MODEL), F32), S((1, LANE), F32), S((1, D_MODEL), F32)],
        compiler_params=_cp(("arbitrary",)), name=name)(h, target, gain)


ADAM_TILE_ELEMS = 128 * 1024


def adamw(g, w, m, v, name, carry=None):
    shape = w.shape
    cols = shape[-1]
    rws = max(1, math.prod(shape[:-1]))
    tr = rws if rws * cols <= ADAM_TILE_ELEMS else _div_tile(rws, max(SUBLANE, ADAM_TILE_ELEMS // cols), SUBLANE)
    bc1 = 1.0 - ADAM_B1 ** ADAM_STEP
    bc2 = 1.0 - ADAM_B2 ** ADAM_STEP

    def body(g_r, w_r, m_r, v_r, go, do, mo, vo):
        gv = g_r[...]
        mn = ADAM_B1 * m_r[...] + (1.0 - ADAM_B1) * gv
        vn = ADAM_B2 * v_r[...] + (1.0 - ADAM_B2) * (gv * gv)
        m_hat = mn / bc1
        v_hat = vn / bc2
        go[...] = gv
        do[...] = -ADAM_LR * (m_hat / (jnp.sqrt(v_hat) + ADAM_EPS) + ADAM_WD * w_r[...])
        mo[...] = mn
        vo[...] = vn

    blk = pl.BlockSpec((tr, cols), lambda i: (i, 0))
    outs, got = _call(
        body, [a.reshape(rws, cols) for a in (g, w, m, v)], carry, grid=(rws // tr,), in_specs=[blk] * 4,
        out_specs=[blk] * 4, out_shape=[S((rws, cols), F32)] * 4, scratch_shapes=[], sem=("parallel",), name=name)
    return tuple(o.reshape(shape) for o in outs), got


SUM_ROWS = 512


def _place():
    return lax.axis_index("x"), lax.axis_index("y"), lax.axis_index("c")


def _remote(src, dst, send_sems, recv_sems, k, to):
    return pltpu.make_async_remote_copy(src_ref=src, dst_ref=dst, send_sem=send_sems.at[k], recv_sem=recv_sems.at[k],
                                        device_id=to, device_id_type=MESH)


def stage_gather_chips(xs):
    _, rws, _ = xs.shape

    def copies(ins, outs, sems):
        (x_ref,), (out_ref,), (send_sems, recv_sems, local_sem) = ins, outs, sems
        mx, my, mc = _place()
        sibling = (mx, my, 1 - mc)
        chips = [(1 - mx, my), (mx, 1 - my), (1 - mx, 1 - my)]

        def piece(cx, cy, h):
            return out_ref.at[2 * cx + cy, h]

        mine = pltpu.make_async_copy(x_ref, out_ref.at[2 * mx + my], local_sem)
        first = [_remote(x_ref.at[mc], piece(mx, my, mc), send_sems, recv_sems, j, (cx, cy, mc))
                 for j, (cx, cy) in enumerate(chips)]
        landed = [_remote(x_ref.at[mc], piece(cx, cy, mc), send_sems, recv_sems, j, (cx, cy, mc))
                  for j, (cx, cy) in enumerate(chips)]
        passed = [_remote(piece(cx, cy, mc), piece(cx, cy, mc), send_sems, recv_sems, 3 + j, sibling)
                  for j, (cx, cy) in enumerate(chips)]
        from_sibling = [_remote(x_ref.at[mc], piece(cx, cy, 1 - mc), send_sems, recv_sems, 3 + j, sibling)
                        for j, (cx, cy) in enumerate(chips)]
        return mine, first, landed, passed, from_sibling

    def start(ins, outs, sems):
        mine, first, _, _, _ = copies(ins, outs, sems)
        mine.start()
        for cp in first:
            cp.start()

    def finish(ins, outs, sems):
        mine, first, landed, passed, from_sibling = copies(ins, outs, sems)
        for j in range(3):
            landed[j].wait_recv()
            passed[j].start()
        for cp in from_sibling:
            cp.wait_recv()
        for cp in first + passed:
            cp.wait_send()
        mine.wait()

    return Stage([xs], [S((4, 2, rws, LANE), xs.dtype)],
                 [pltpu.SemaphoreType.DMA((6,)), pltpu.SemaphoreType.DMA((6,)), pltpu.SemaphoreType.DMA], start, finish)


def stage_pair_exchange(g4):
    _, _, rws, _ = g4.shape

    def copies(ins, outs, sems):
        (g_ref,), (land_ref,), (send_sems, recv_sems) = ins, outs, sems
        mx, my, mc = _place()
        return [_remote(g_ref.at[s, 1 - mc], land_ref.at[s], send_sems, recv_sems, s, (mx, my, 1 - mc))
                for s in range(4)]

    def start(ins, outs, sems):
        for cp in copies(ins, outs, sems):
            cp.start()

    def finish(ins, outs, sems):
        cps = copies(ins, outs, sems)
        for cp in cps:
            cp.wait_recv()
        for cp in cps:
            cp.wait_send()

    return Stage([g4], [S((4, rws, LANE), g4.dtype)],
                 [pltpu.SemaphoreType.DMA((4,)), pltpu.SemaphoreType.DMA((4,))], start, finish)


def pair_sum(g4, land, c_idx, name):
    _, _, rws, _ = g4.shape
    th = SUM_ROWS

    def body(c_ref, a_ref, b_ref, o_ref):
        o_ref[...] = a_ref[...] + b_ref[...]

    return pl.pallas_call(
        body,
        grid_spec=pltpu.PrefetchScalarGridSpec(
            num_scalar_prefetch=1, grid=(4, rws // th),
            in_specs=[pl.BlockSpec((None, None, th, LANE), lambda s, i, c: (s, c[0], i, 0)),
                      pl.BlockSpec((None, th, LANE), lambda s, i, c: (s, i, 0))],
            out_specs=pl.BlockSpec((None, th, LANE), lambda s, i, c: (s, i, 0))),
        out_shape=S((4, rws, LANE), F32), compiler_params=_cp(("parallel", "parallel")), name=name)(c_idx, g4, land)


def stage_chip_scatter(p4):
    _, rws, _ = p4.shape

    def copies(ins, outs, sems):
        (p_ref,), (land_ref,), (send_sems, recv_sems, local_sem) = ins, outs, sems
        mx, my, mc = _place()
        me = 2 * mx + my
        chips = [(1 - mx, my), (mx, 1 - my), (1 - mx, 1 - my)]
        mine = pltpu.make_async_copy(p_ref.at[me], land_ref.at[me], local_sem)
        sent = [_remote(p_ref.at[2 * cx + cy], land_ref.at[me], send_sems, recv_sems, j, (cx, cy, mc))
                for j, (cx, cy) in enumerate(chips)]
        landed = [_remote(p_ref.at[me], land_ref.at[2 * cx + cy], send_sems, recv_sems, j, (cx, cy, mc))
                  for j, (cx, cy) in enumerate(chips)]
        return mine, sent, landed

    def start(ins, outs, sems):
        mine, sent, _ = copies(ins, outs, sems)
        mine.start()
        for cp in sent:
            cp.start()

    def finish(ins, outs, sems):
        mine, sent, landed = copies(ins, outs, sems)
        for cp in landed:
            cp.wait_recv()
        for cp in sent:
            cp.wait_send()
        mine.wait()

    return Stage([p4], [S((4, rws, LANE), p4.dtype)],
                 [pltpu.SemaphoreType.DMA((3,)), pltpu.SemaphoreType.DMA((3,)), pltpu.SemaphoreType.DMA], start, finish)


def chip_sum(l4, name):
    _, rws, _ = l4.shape
    th = SUM_ROWS

    def body(a, b, c, d, o_ref):
        o_ref[...] = ((a[...] + b[...]) + c[...]) + d[...]

    def blk(s):
        return pl.BlockSpec((None, th, LANE), lambda i: (s, i, 0))

    return pl.pallas_call(
        body, grid=(rws // th,), in_specs=[blk(0), blk(1), blk(2), blk(3)],
        out_specs=pl.BlockSpec((th, LANE), lambda i: (i, 0)), out_shape=S((rws, LANE), F32),
        compiler_params=_cp(("parallel",)), name=name)(l4, l4, l4, l4)


def stage_pair_gather(rh):
    rws, _ = rh.shape

    def copies(ins, outs, sems):
        (r_ref,), (out_ref,), (send_sems, recv_sems, local_sem) = ins, outs, sems
        mx, my, mc = _place()
        sibling = (mx, my, 1 - mc)
        mine = pltpu.make_async_copy(r_ref, out_ref.at[mc], local_sem)
        sent = _remote(r_ref, out_ref.at[mc], send_sems, recv_sems, 0, sibling)
        landed = _remote(r_ref, out_ref.at[1 - mc], send_sems, recv_sems, 0, sibling)
        return mine, sent, landed

    def start(ins, outs, sems):
        mine, sent, _ = copies(ins, outs, sems)
        mine.start()
        sent.start()

    def finish(ins, outs, sems):
        mine, sent, landed = copies(ins, outs, sems)
        landed.wait_recv()
        sent.wait_send()
        mine.wait()

    return Stage([rh], [S((2, rws, LANE), rh.dtype)],
                 [pltpu.SemaphoreType.DMA((1,)), pltpu.SemaphoreType.DMA((1,)), pltpu.SemaphoreType.DMA], start, finish)


PACK_ELEMS = 16 * LANE


def pack_rows(arrays, lead, total_mult):
    parts, offs, r0 = [], [], 0
    for a in arrays:
        flat = a.reshape(a.shape[:lead] + (-1,))
        elems = _round_up(flat.shape[-1], PACK_ELEMS)
        flat = jnp.pad(flat, [(0, 0)] * lead + [(0, elems - flat.shape[-1])])
        parts.append(flat.reshape(flat.shape[:lead] + (elems // LANE, LANE)))
        offs.append((r0, elems // LANE))
        r0 += elems // LANE
    total = _round_up(r0, total_mult)
    if total > r0:
        parts.append(jnp.zeros(parts[0].shape[:lead] + (total - r0, LANE), parts[0].dtype))
    return jnp.concatenate(parts, axis=lead), offs


def unpack_rows(buf, off, shape):
    r0, nr = off
    lead = buf.shape[:-2]
    n = math.prod(shape)
    return buf[..., r0:r0 + nr, :].reshape(lead + (nr * LANE,))[..., :n].reshape(lead + tuple(shape))


def unshard(stacked, axis):
    x = jnp.moveaxis(stacked, 0, axis)
    return x.reshape(x.shape[:axis] + (4 * x.shape[axis + 1],) + x.shape[axis + 2:])


def to_shards(full, axis):
    n = full.shape[axis] // 4
    x = full.reshape(full.shape[:axis] + (4, n) + full.shape[axis + 1:])
    return jnp.moveaxis(x, axis, 0)


def _rot_cols(w):
    half = w.shape[-1] // 2
    return jnp.concatenate([-w[..., half:], w[..., :half]], axis=-1)


def _unrot_cols(dw):
    half = dw.shape[-1] // 2
    return jnp.concatenate([dw[..., half:], -dw[..., :half]], axis=-1)


def odd_w_in_padded(w_in):
    kr = w_in[:, Q_LORA + KV_LORA:]
    rows = w_in.shape[0]
    return jnp.concatenate([w_in[:, :Q_LORA], jnp.zeros((rows, 128), w_in.dtype), w_in[:, Q_LORA:Q_LORA + KV_LORA],
                            jnp.zeros((rows, 64), w_in.dtype), kr, _rot_cols(kr)], axis=1)


def odd_w_in_unpad(dwp):
    base = 512 + KV_LORA + 64
    dkr = dwp[:, base:base + QK_ROPE] + _unrot_cols(dwp[:, base + QK_ROPE:base + 2 * QK_ROPE])
    return jnp.concatenate([dwp[:, :Q_LORA], dwp[:, 512:512 + KV_LORA], dkr], axis=1)


def uq_padded(w_uq):
    w = w_uq.reshape(Q_LORA, MLA_HEADS, QK_HEAD)
    return jnp.concatenate([w, _rot_cols(w[:, :, QK_NOPE:])], axis=-1).reshape(Q_LORA, MLA_HEADS * HP)


def uq_unpad(dwp):
    d = dwp.reshape(Q_LORA, MLA_HEADS, HP)
    rope = d[:, :, QK_NOPE:QK_HEAD] + _unrot_cols(d[:, :, QK_HEAD:])
    return jnp.concatenate([d[:, :, :QK_NOPE], rope], axis=-1).reshape(Q_LORA, MLA_HEADS * QK_HEAD)


def ukv_padded(w_ukv):
    w = w_ukv.reshape(KV_LORA, MLA_HEADS, QK_NOPE + V_HEAD)
    wk = jnp.concatenate([w[:, :, :QK_NOPE], jnp.zeros((KV_LORA, MLA_HEADS, HP - QK_NOPE), w.dtype)], axis=-1)
    return jnp.concatenate([wk.reshape(KV_LORA, MLA_HEADS * HP), w[:, :, QK_NOPE:].reshape(KV_LORA, MLA_HEADS * V_HEAD)],
                           axis=1)


def ukv_unpad(dwp):
    dk = dwp[:, :MLA_HEADS * HP].reshape(KV_LORA, MLA_HEADS, HP)[:, :, :QK_NOPE]
    dv = dwp[:, MLA_HEADS * HP:].reshape(KV_LORA, MLA_HEADS, V_HEAD)
    return jnp.concatenate([dk, dv], axis=-1).reshape(KV_LORA, MLA_HEADS * (QK_NOPE + V_HEAD))


def block_diag(w):
    h, d, _ = w.shape
    eye = jnp.eye(h, dtype=w.dtype)
    return (eye[:, None, :, None] * w[:, :, None, :]).reshape(h * d, h * d)


def block_diag_part(dense, h):
    d = dense.shape[0] // h
    x = dense.reshape(h, d, h, d)
    return jnp.stack([x[i, :, i, :] for i in range(h)], axis=0)


def rope_tables(tp):
    pos = jnp.arange(tp, dtype=F32)
    inv_freq = ROPE_BASE ** (-jnp.arange(0, QK_ROPE, 2, dtype=F32) / QK_ROPE)
    ang = pos[:, None] * inv_freq[None, :]
    cos2 = jnp.tile(jnp.cos(ang), (1, 2))
    sin2 = jnp.tile(jnp.sin(ang), (1, 2))
    tabq = jnp.concatenate([jnp.ones((tp, QK_NOPE), F32), cos2, sin2], axis=1)
    tabk = jnp.concatenate([jnp.zeros((tp, QK_NOPE), F32), cos2, sin2], axis=1)
    return tabq, tabk


class Dims:
    def __init__(self, nb, seq):
        self.nb = nb
        self.t_real = seq + N_META
        self.tp = _round_up(self.t_real, ATT_BLK)
        self.n = self.tp // 4
        assert self.n % 16 == 0
        self.rows = nb * self.tp


class NoComm:
    def advance(self, carried):
        return None


def even_fwd(h, p, dm, comm):
    (u, hn), _ = norm_matmul(h, 0, D_MODEL, p["norm"], p["w_in"], dm.n, 512, F32, "ev_in")
    (y, ca, xc, a, hs), got = even_mid_fwd(u, p["conv_a"], p["conv_b"], p["conv_b_bias"], p["rw"], p["r_b"], p["iw"],
                                           p["i_b"], p["lam"], dm.nb, dm.tp, dm.n, "ev_mid", carry=comm.advance(None))
    comm.advance(got)
    out = matmul_res(y, p["w_out"].reshape(2, CONV_W, D_MODEL), h, dm.n, 512, "ev_out")
    return out, (h, u, hn, ca, xc, a, hs, y)


def even_bwd(dout, saved, p, dm, comm):
    h, u, hn, ca, xc, a, hs, y = saved
    g = {}
    dycat = matmul_nt(dout, p["w_out"], dm.n, 512, F32, "ev_dycat")
    g["w_out"], got = matmul_tn(y, dout, dm.n, "ev_dw_out", carry=comm.advance(None))
    outs, got = even_mid_bwd(u, dycat, ca, xc, a, hs, p["conv_a"], p["conv_b"], p["rw"], p["r_b"], p["iw"], p["i_b"],
                             p["lam"], dm.nb, dm.tp, dm.n, "ev_mid_bwd", carry=comm.advance(got))
    du, g["conv_a"], g["conv_b"], g["conv_b_bias"], drw, g["r_b"], diw, g["i_b"], g["lam"] = outs
    g["r_w"] = block_diag_part(drw, LRU_HEADS)
    g["i_w"] = block_diag_part(diw, LRU_HEADS)
    g["w_in"], got = matmul_tn(hn, du, dm.n, "ev_dw_in", carry=comm.advance(got))
    comm.advance(got)
    dx, g["norm"] = matmul_nt_normbwd(du, p["w_in"], h, 0, p["norm"], dout, dm.n, 512, F32, "ev_dx")
    return dx, g


def odd_fwd(h, p, tabq, tabk, dm, comm):
    nt = dm.tp // dm.n
    (u, hn), _ = norm_matmul(h, 0, D_MODEL, p["norm"], p["w_in_p"], dm.n, ODD_PAD, F32, "od_in")
    tab_spec = pl.BlockSpec((dm.n, HP), lambda i, j: (i % nt, 0))
    (q, cqn), _ = norm_matmul(u, 0, Q_LORA, p["q_norm"], p["w_uq_p"], dm.n, 512, MXU_DT, "od_q",
                              epi=_q_rope_epi, epi_ops=(tabq,), epi_specs=(tab_spec,))
    kr_spec = pl.BlockSpec((dm.n, HP), lambda i, j: (i, ODD_KR_COL))
    (k, ckvn), _ = norm_matmul(u, ODD_CKV_COL, KV_LORA, p["kv_norm"], p["w_uk_p"], dm.n, 512, MXU_DT, "od_k",
                               epi=_k_rope_epi, epi_ops=(u, tabk), epi_specs=(kr_spec, tab_spec))
    (v, _), _ = norm_matmul(u, ODD_CKV_COL, KV_LORA, p["kv_norm"], p["w_uv_p"], dm.n, 512, MXU_DT, "od_v")
    (o, lse), got = attn_fwd(q, k, v, dm.nb, dm.tp, "od_attn", carry=comm.advance(None))
    comm.advance(got)
    out = matmul_res(o[None], p["w_out"][None], h, dm.n, 512, "od_out")
    return out, (h, u, hn, cqn, ckvn, q, k, v, o, lse)


def odd_bwd(dout, saved, p, tabq, tabk, dm, comm):
    h, u, hn, cqn, ckvn, q, k, v, o, lse = saved
    g = {}
    do = matmul_nt(dout, p["w_out"], dm.n, 512, MXU_DT, "od_do")
    g["w_out"], got = matmul_tn(o, dout, dm.n, "od_dw_out", carry=comm.advance(None))
    (dq, dk, dv), got = attn_bwd(q, k, v, o, do, lse, dm.nb, dm.tp, "od_attn_bwd", carry=comm.advance(got))
    dqa, dkv, dkr = rope_bwd(dq, dk, dv, tabq, tabk, dm.tp, dm.n, "od_rope_bwd")
    g["w_uq_p"], got = matmul_tn(cqn, dqa, dm.n, "od_dw_uq", carry=comm.advance(got))
    comm.advance(got)
    g["w_ukv_p"], _ = matmul_tn(ckvn, dkv, dm.n, "od_dw_ukv")
    dcq, g["q_norm"] = matmul_nt_normbwd(dqa, p["w_uq_p"], u, 0, p["q_norm"], None, dm.n, 512, MXU_DT, "od_dcq")
    dckv, g["kv_norm"] = matmul_nt_normbwd(dkv, p["w_ukv_p"], u, ODD_CKV_COL, p["kv_norm"], None, dm.n, 512, MXU_DT,
                                           "od_dckv")
    du = jnp.concatenate([dcq, jnp.zeros((dm.rows, 128), MXU_DT), dckv, dkr], axis=1)
    g["w_in_p"], _ = matmul_tn(hn, du, dm.n, "od_dw_in")
    dx, g["norm"] = matmul_nt_normbwd(du, p["w_in_p"], h, 0, p["norm"], dout, dm.n, ODD_PAD, F32, "od_dx")
    return dx, g


def ffn_fwd(h, p, dm, comm):
    (up, hn), got = norm_matmul(h, 0, D_MODEL, p["norm"], p["w_up"], dm.n, D_FF // 2, MXU_DT, "ffn_up",
                                carry=comm.advance(None))
    comm.advance(got)
    u, y = ffn_mid_fwd(up, p["cw"], p["cb"], dm.nb, dm.tp, dm.n, "ffn_mid")
    out = matmul_res(y[None], p["w_down"][None], h, dm.n, 512, "ffn_down")
    return out, (h, up, hn, u, y)


def ffn_bwd(dout, saved, p, dm, comm):
    h, up, hn, u, y = saved
    g = {}
    dy = matmul_nt(dout, p["w_down"], dm.n, D_FF // 2, MXU_DT, "ffn_dy")
    g["w_down"], got = matmul_tn(y, dout, dm.n, "ffn_dw_down", carry=comm.advance(None))
    (dup, g["cw"], g["cb"]), got = ffn_mid_bwd(dy, u, up, p["cw"], dm.nb, dm.tp, dm.n, "ffn_mid_bwd",
                                               carry=comm.advance(got))
    g["w_up"], got = matmul_tn(hn, dup, dm.n, "ffn_dw_up", carry=comm.advance(got), col_shards=4)
    comm.advance(got)
    dx, g["norm"] = matmul_nt_normbwd(dup, p["w_up"], h, 0, p["norm"], dout, dm.n, D_FF // 2, F32, "ffn_dx")
    return dx, g


def _row(v):
    return v.reshape(1, -1)


def even_params(wf, j):
    return dict(norm=_row(wf["ev_norm"][j]), w_in=wf["ev_w_in"], conv_a=wf["ev_conv_a"][j], conv_b=wf["ev_conv_b"][j],
                conv_b_bias=_row(wf["ev_conv_b_bias"][j]), rw=block_diag(wf["ev_gate_r_w"][j]).astype(MXU_DT),
                r_b=_row(wf["ev_gate_r_b"][j]), iw=block_diag(wf["ev_gate_i_w"][j]).astype(MXU_DT),
                i_b=_row(wf["ev_gate_i_b"][j]), lam=_row(wf["ev_lru_lambda"][j]), w_out=wf["ev_w_out"])


def odd_params(wf, j):
    wkv = ukv_padded(wf["od_w_ukv"])
    return dict(norm=_row(wf["od_norm"][j]), w_in_p=odd_w_in_padded(wf["od_w_in"]), q_norm=_row(wf["od_q_norm"][j]),
                kv_norm=_row(wf["od_kv_norm"][j]), w_uq_p=uq_padded(wf["od_w_uq"]), w_ukv_p=wkv,
                w_uk_p=wkv[:, :MLA_HEADS * HP], w_uv_p=wkv[:, MLA_HEADS * HP:], w_out=wf["od_w_out"])


def ffn_params(wf, layer):
    return dict(norm=_row(wf["ffn_norm"][layer]), w_up=wf["ffn_w_up"],
                cw=jnp.moveaxis(wf["ffn_conv_w"][layer].reshape(3, 2, D_FF), 1, 0),
                cb=wf["ffn_conv_b"][layer].reshape(2, 1, D_FF), w_down=wf["ffn_w_down"])


def even_grads(g):
    out = {"ev_" + k_: g[k_] for k_ in ("w_in", "conv_a", "conv_b", "w_out")}
    out.update({"ev_norm": g["norm"][0], "ev_conv_b_bias": g["conv_b_bias"][0], "ev_gate_r_w": g["r_w"],
                "ev_gate_r_b": g["r_b"][0], "ev_gate_i_w": g["i_w"], "ev_gate_i_b": g["i_b"][0],
                "ev_lru_lambda": g["lam"][0]})
    return out


def odd_grads(g):
    return {"od_norm": g["norm"][0], "od_q_norm": g["q_norm"][0], "od_kv_norm": g["kv_norm"][0],
            "od_w_in": odd_w_in_unpad(g["w_in_p"]), "od_w_uq": uq_unpad(g["w_uq_p"]),
            "od_w_ukv": ukv_unpad(g["w_ukv_p"]), "od_w_out": g["w_out"]}


def ffn_grads(g):
    return {"ffn_norm": g["norm"][0], "ffn_w_up": g["w_up"], "ffn_conv_w": jnp.moveaxis(g["cw"], 0, 1).reshape(3, 2 * D_FF),
            "ffn_conv_b": g["cb"].reshape(2 * D_FF), "ffn_w_down": g["w_down"]}


WEIGHTS = ["meta_tokens", "ev_norm", "ev_w_in", "ev_conv_a", "ev_conv_b", "ev_conv_b_bias", "ev_gate_r_w", "ev_gate_r_b",
           "ev_gate_i_w", "ev_gate_i_b", "ev_lru_lambda", "ev_w_out", "od_norm", "od_w_in", "od_q_norm", "od_kv_norm",
           "od_w_uq", "od_w_ukv", "od_w_out", "ffn_norm", "ffn_w_up", "ffn_conv_w", "ffn_conv_b", "ffn_w_down",
           "final_norm"]
SHARD_AXIS = {"meta_tokens": 1, "ev_w_in": 2, "ev_conv_a": 2, "ev_conv_b": 2, "ev_w_out": 1, "od_norm": 1, "od_w_in": 1,
              "od_q_norm": 1, "od_kv_norm": 1, "od_w_uq": 2, "od_w_ukv": 2, "od_w_out": 1, "ffn_w_up": 2,
              "ffn_conv_w": 2, "ffn_w_down": 1}
MATMUL_WEIGHTS = ["ev_w_in", "ev_w_out", "od_w_in", "od_w_uq", "od_w_ukv", "od_w_out", "ffn_w_up", "ffn_w_down"]


LAYER_ORDER = [("ev", 0), ("ffn", 0), ("od", 0), ("ffn", 1), ("ev", 1), ("ffn", 2), ("od", 1), ("ffn", 3)]
LAYER_MATMUL = {"ev": ["ev_w_in", "ev_w_out"], "od": ["od_w_in", "od_w_uq", "od_w_ukv", "od_w_out"],
                "ffn": ["ffn_w_up", "ffn_w_down"]}
LAYER_SHARDED = {"ev": ["ev_w_in", "ev_conv_a", "ev_conv_b", "ev_w_out"],
                 "od": ["od_norm", "od_w_in", "od_q_norm", "od_kv_norm", "od_w_uq", "od_w_ukv", "od_w_out"],
                 "ffn": ["ffn_w_up", "ffn_conv_w", "ffn_w_down"]}
STACKED_SHARDS = "ffn_w_up"


def gather_all_layers(w, names, name):
    buf, offs = pack_rows([w[n] for n in names], 0, 32)
    got = run_stage(stage_gather_chips(buf.reshape(2, buf.shape[0] // 2, LANE)), name)[0]
    got = got.reshape(4, buf.shape[0], LANE)
    return {n: unshard(unpack_rows(got, off, w[n].shape), SHARD_AXIS[n]) for n, off in zip(names, offs)}


class GatherComm:
    def __init__(self, w, kind, idx):
        self.names = LAYER_MATMUL[kind]
        arrs = [w[n][idx].astype(MXU_DT) for n in self.names]
        self.shapes = [a.shape for a in arrs]
        buf, self.offs = pack_rows(arrs, 0, 32)
        self.rows = buf.shape[0]
        self.stage = stage_gather_chips(buf.reshape(2, self.rows // 2, LANE))
        self.step, self.got = 0, None

    def advance(self, carried):
        self.step += 1
        if self.step == 1:
            return self.stage
        if self.step == 2:
            self.got = carried[0]
        return None

    def run_alone(self, name):
        self.advance(run_stage(self.advance(None), name))

    def weights(self):
        got = self.got.reshape(4, self.rows, LANE)
        out = {}
        for n, off, shape in zip(self.names, self.offs, self.shapes):
            stacked = unpack_rows(got, off, shape)
            out[n] = stacked if n == STACKED_SHARDS else unshard(stacked, SHARD_AXIS[n] - 1)
        return out


class ReduceComm:
    def __init__(self, grads, axes, c_idx, tag, tail=None):
        self.names = list(grads)
        shards = [grads[n] if n == STACKED_SHARDS else to_shards(grads[n], axes[n]) for n in self.names]
        self.shapes = [s.shape[1:] for s in shards]
        gs, self.offs = pack_rows(shards, 1, 16)
        self.rs = gs.shape[1] // 2
        parts = [gs.reshape(4, 2, self.rs, LANE)]
        self.rr = 0
        if tail is not None:
            self.rr = tail.shape[0] // 8
            parts.append(tail.reshape(4, 2, self.rr, LANE))
        rows = _round_up(self.rs + self.rr, SUM_ROWS)
        if rows > self.rs + self.rr:
            parts.append(jnp.zeros((4, 2, rows - self.rs - self.rr, LANE), F32))
        self.g4 = jnp.concatenate(parts, axis=2)
        self.c_idx, self.tag, self.step, self.both = c_idx, tag, 0, None

    def advance(self, carried):
        self.step += 1
        if self.step == 1:
            return stage_pair_exchange(self.g4)
        if self.step == 2:
            return stage_chip_scatter(pair_sum(self.g4, carried[0], self.c_idx, "grad_pair_sum_" + self.tag))
        if self.step == 3:
            return stage_pair_gather(chip_sum(carried[0], "grad_chip_sum_" + self.tag))
        if self.step == 4:
            self.both = carried[0]
        return None

    def run_alone(self, name):
        stage = self.advance(None)
        while stage is not None:
            stage = self.advance(run_stage(stage, name + "_%d" % self.step))

    def results(self):
        flat = self.both[:, :self.rs].reshape(2 * self.rs, LANE)
        out = {n: unpack_rows(flat, off, shape) for n, off, shape in zip(self.names, self.offs, self.shapes)}
        return out, self.both[:, self.rs:self.rs + self.rr]


def kernel(x, meta_tokens, ev_norm, ev_w_in, ev_conv_a, ev_conv_b, ev_conv_b_bias, ev_gate_r_w, ev_gate_r_b, ev_gate_i_w, ev_gate_i_b, ev_lru_lambda, ev_w_out, od_norm, od_w_in, od_q_norm, od_kv_norm, od_w_uq, od_w_ukv, od_w_out, ffn_norm, ffn_w_up, ffn_conv_w, ffn_conv_b, ffn_w_down, final_norm, loss_target, m_meta_tokens, m_ev_norm, m_ev_w_in, m_ev_conv_a, m_ev_conv_b, m_ev_conv_b_bias, m_ev_gate_r_w, m_ev_gate_r_b, m_ev_gate_i_w, m_ev_gate_i_b, m_ev_lru_lambda, m_ev_w_out, m_od_norm, m_od_w_in, m_od_q_norm, m_od_kv_norm, m_od_w_uq, m_od_w_ukv, m_od_w_out, m_ffn_norm, m_ffn_w_up, m_ffn_conv_w, m_ffn_conv_b, m_ffn_w_down, m_final_norm, v_meta_tokens, v_ev_norm, v_ev_w_in, v_ev_conv_a, v_ev_conv_b, v_ev_conv_b_bias, v_ev_gate_r_w, v_ev_gate_r_b, v_ev_gate_i_w, v_ev_gate_i_b, v_ev_lru_lambda, v_ev_w_out, v_od_norm, v_od_w_in, v_od_q_norm, v_od_kv_norm, v_od_w_uq, v_od_w_ukv, v_od_w_out, v_ffn_norm, v_ffn_w_up, v_ffn_conv_w, v_ffn_conv_b, v_ffn_w_down, v_final_norm):
    given = dict(locals())
    w = {n: given[n] for n in WEIGHTS}
    nb, seq, _ = x.shape
    dm = Dims(nb, seq)
    n_layers = len(LAYER_ORDER)

    wf = {n: w[n] for n in WEIGHTS if n not in SHARD_AXIS}
    wf.update(gather_all_layers(w, [n for n in SHARD_AXIS if n not in MATMUL_WEIGHTS], "gather_small_weights"))
    gathers = [GatherComm(w, kind, idx) for kind, idx in LAYER_ORDER]
    gathers[0].run_alone("gather_first_layer")

    tail = dm.tp - dm.t_real
    meta = jnp.broadcast_to(wf["meta_tokens"][None], (nb, N_META, D_MODEL))
    h = jnp.concatenate([meta, x, jnp.zeros((nb, tail, D_MODEL), F32)], axis=1).reshape(dm.rows, D_MODEL)
    tgt = jnp.pad(loss_target, ((0, 0), (N_META, tail), (0, 0))).reshape(dm.rows, D_MODEL)
    tabq, tabk = rope_tables(dm.tp)

    params, saved = [], []
    for i, (kind, idx) in enumerate(LAYER_ORDER):
        wl = dict(wf)
        wl.update(gathers[i].weights())
        comm = gathers[i + 1] if i + 1 < n_layers else NoComm()
        if kind == "ev":
            p = even_params(wl, idx)
            h, sv = even_fwd(h, p, dm, comm)
        elif kind == "od":
            p = odd_params(wl, idx)
            h, sv = odd_fwd(h, p, tabq, tabk, dm, comm)
        else:
            p = ffn_params(wl, idx)
            h, sv = ffn_fwd(h, p, dm, comm)
        params.append(p)
        saved.append(sv)

    dh, loss, dfinal = loss_head(h, tgt, _row(wf["final_norm"]), dm.tp, dm.t_real, dm.n, "loss_head")
    loss = lax.psum(loss[0, 0], ("x", "y", "c"))

    c_idx = lax.axis_index("c").astype(jnp.int32).reshape(1)
    layer_grads = {n: {} for n in WEIGHTS}
    pending, reduces = NoComm(), []
    for i in reversed(range(n_layers)):
        kind, idx = LAYER_ORDER[i]
        if kind == "ev":
            dh, g = even_bwd(dh, saved[i], params[i], dm, pending)
            g = even_grads(g)
        elif kind == "od":
            dh, g = odd_bwd(dh, saved[i], params[i], tabq, tabk, dm, pending)
            g = odd_grads(g)
        else:
            dh, g = ffn_bwd(dh, saved[i], params[i], dm, pending)
            g = ffn_grads(g)
        for n in g:
            if n not in SHARD_AXIS:
                layer_grads[n][idx] = g[n]
        if i > 0:
            pending = ReduceComm({n: g[n] for n in LAYER_SHARDED[kind]}, {n: SHARD_AXIS[n] - 1 for n in SHARD_AXIS},
                                 c_idx, "%s%d" % (kind, idx))
            reduces.append((pending, idx))
    dh3 = dh.reshape(nb, dm.tp, D_MODEL)
    grad_x = dh3[:, N_META:dm.t_real]

    repl = [n for n in WEIGHTS if n not in SHARD_AXIS]
    layer_grads["final_norm"] = {0: dfinal[0]}
    repl_full = {n: (layer_grads[n][0] if n == "final_norm" else
                     jnp.stack([layer_grads[n][j] for j in range(w[n].shape[0])], axis=0)) for n in repl}
    tail_buf, tail_offs = pack_rows([repl_full[n] for n in repl], 0, 64)
    first = {n: g[n] for n in LAYER_SHARDED["ev"]}
    first["meta_tokens"] = jnp.sum(dh3[:, :N_META], axis=0)
    axes = {n: SHARD_AXIS[n] - 1 for n in SHARD_AXIS}
    axes["meta_tokens"] = SHARD_AXIS["meta_tokens"]
    last = ReduceComm(first, axes, c_idx, "first_layer", tail=tail_buf)

    red = {}
    for comm, idx in reduces:
        got, _ = comm.results()
        for n, v_ in got.items():
            layer_grads[n][idx] = v_
    results = {}

    def update(n, carry=None):
        results[n], carried = adamw(red[n], w[n], given["m_" + n], given["v_" + n], "adamw_" + n, carry=carry)
        return carried

    done_early = LAYER_SHARDED["od"] + LAYER_SHARDED["ffn"]
    for n in done_early:
        red[n] = jnp.stack([layer_grads[n][j] for j in range(w[n].shape[0])], axis=0)
    carriers = ["ffn_w_down", "ffn_w_up", "od_w_out"]
    carried = None
    for n in carriers:
        carried = update(n, last.advance(carried))
    last.advance(carried)
    for n in done_early:
        if n not in carriers:
            update(n)

    got, tail_piece = last.results()
    red["meta_tokens"] = got.pop("meta_tokens")
    for n, v_ in got.items():
        layer_grads[n][0] = v_
    for n in LAYER_SHARDED["ev"]:
        red[n] = jnp.stack([layer_grads[n][j] for j in range(w[n].shape[0])], axis=0)
    carried = update("ev_w_in", stage_gather_chips(tail_piece))
    tails = carried[0].reshape(tail_buf.shape[0], LANE)
    for n, off in zip(repl, tail_offs):
        red[n] = unpack_rows(tails, off, w[n].shape)
    for n in WEIGHTS:
        if n not in results:
            update(n)
    outs = [results[n] for n in WEIGHTS]
    return (loss, grad_x, *[o[0] for o in outs], *[o[1] for o in outs], *[o[2] for o in outs], *[o[3] for o in outs])
```

```python
import math

import jax
import jax.numpy as jnp
from jax import lax
from jax.experimental import pallas as pl
from jax.experimental.pallas import tpu as pltpu

F32 = jnp.float32
MXU_DT = jnp.bfloat16
S = jax.ShapeDtypeStruct
MESH = pl.DeviceIdType.MESH

EPS = 1e-6
D_MODEL = 1024
N_META = 16
DEPTH = 4
CONV_W = 512
LRU_W = 512
LRU_HEADS = 8
LRU_C = 8.0
EVEN_IN = 2560
MLA_HEADS = 16
QK_NOPE = 64
QK_ROPE = 32
QK_HEAD = 96
V_HEAD = 64
Q_LORA = 384
KV_LORA = 256
ROPE_BASE = 10000.0
D_FF = 2816
ODD_PAD = 896
ODD_CKV_COL = 2
ODD_KR_COL = 6
HP = 128
ATT_BLK = 384
Q_PRESCALE = QK_HEAD ** -0.5 * math.log2(math.e)
FFN_CT = 256
STRIP_ROWS = 16
LANE = 128
SUBLANE = 8
VMEM_LIMIT_MB = 52

ADAM_LR = 0.001
ADAM_B1 = 0.9
ADAM_B2 = 0.999
ADAM_EPS = 1e-08
ADAM_WD = 0.01
ADAM_STEP = 10

NT_DIMS = (((1,), (1,)), ((), ()))
TN_DIMS = (((0,), (0,)), ((), ()))


def _cp(sem):
    return pltpu.CompilerParams(dimension_semantics=sem, vmem_limit_bytes=VMEM_LIMIT_MB << 20)


def _div_tile(n, cap, mult):
    if n <= cap:
        return n
    best = None
    for t in range(mult, cap + 1, mult):
        if n % t == 0:
            best = t
    assert best is not None, (n, cap, mult)
    return best


def _round_up(n, m):
    return -(-n // m) * m


def mat_cols(arr):
    return arr.shape[1] if arr.ndim == 2 else arr.shape[0] * arr.shape[2]


def mat_width(arr):
    return arr.shape[-1]


def mat_spec(arr, tm, tw, rc):
    if arr.ndim == 2:
        return pl.BlockSpec((tm, tw), lambda *g: rc(*g))
    per = arr.shape[2] // tw
    assert arr.shape[2] % tw == 0

    def imap(*g):
        r, c = rc(*g)
        return (c // per, r, c % per)

    return pl.BlockSpec((None, tm, tw), imap)


HBM_SPEC = pl.BlockSpec(memory_space=pltpu.HBM)


class Stage:
    def __init__(self, inputs, out_shapes, sems, start, finish):
        self.inputs, self.out_shapes, self.sems, self.start, self.finish = inputs, out_shapes, sems, start, finish


def run_stage(stage, name):
    n_in, n_out = len(stage.inputs), len(stage.out_shapes)

    def body(*refs):
        ins, outs, sems = refs[:n_in], refs[n_in:n_in + n_out], refs[n_in + n_out:]
        stage.start(ins, outs, sems)
        stage.finish(ins, outs, sems)

    return pl.pallas_call(body, out_shape=list(stage.out_shapes), in_specs=[HBM_SPEC] * n_in,
                          out_specs=[HBM_SPEC] * n_out, scratch_shapes=list(stage.sems), name=name)(*stage.inputs)


def _call(body, ops, carry, *, grid, in_specs, out_specs, out_shape, scratch_shapes, sem, name):
    if carry is None:
        outs = pl.pallas_call(body, grid=grid, in_specs=in_specs, out_specs=out_specs, out_shape=out_shape,
                              scratch_shapes=scratch_shapes, compiler_params=_cp(sem), name=name)(*ops)
        return outs, None
    multi = isinstance(out_shape, (list, tuple))
    shapes = list(out_shape) if multi else [out_shape]
    ospecs = list(out_specs) if multi else [out_specs]
    n_in, n_out, n_sc = len(ops), len(shapes), len(scratch_shapes)
    c_in, c_out = len(carry.inputs), len(carry.out_shapes)

    def wrapped(*refs):
        ins, cin = refs[:n_in], refs[n_in:n_in + c_in]
        o0 = n_in + c_in
        outs, cout = refs[o0:o0 + n_out], refs[o0 + n_out:o0 + n_out + c_out]
        s0 = o0 + n_out + c_out
        scs, csems = refs[s0:s0 + n_sc], refs[s0 + n_sc:]
        first = pl.program_id(0) == 0
        last = pl.program_id(0) == grid[0] - 1
        for d in range(1, len(grid)):
            first = first & (pl.program_id(d) == 0)
            last = last & (pl.program_id(d) == grid[d] - 1)

        @pl.when(first)
        def _():
            carry.start(cin, cout, csems)

        body(*ins, *outs, *scs)

        @pl.when(last)
        def _():
            carry.finish(cin, cout, csems)

    res = pl.pallas_call(
        wrapped, grid=grid, in_specs=list(in_specs) + [HBM_SPEC] * c_in, out_specs=ospecs + [HBM_SPEC] * c_out,
        out_shape=shapes + list(carry.out_shapes), scratch_shapes=list(scratch_shapes) + list(carry.sems),
        compiler_params=_cp(("arbitrary",) * len(grid)), name=name)(*ops, *carry.inputs)
    main = res[:n_out]
    return (list(main) if multi else main[0]), list(res[n_out:])


def norm_matmul(x, xcol, kdim, gain, w, tm, tn, out_dtype, name, epi=None, epi_ops=(), epi_specs=(), carry=None):
    rows, n = x.shape[0], mat_cols(w) if w.ndim == 3 else w.shape[1]
    n_epi = len(epi_ops)
    w_spec = (pl.BlockSpec((kdim, tn), lambda i, j: (0, j)) if w.ndim == 2 else
              pl.BlockSpec((None, kdim, tn), lambda i, j: (j // (w.shape[2] // tn), 0, j % (w.shape[2] // tn))))

    def body(x_ref, g_ref, w_ref, *rest):
        epi_refs = rest[:n_epi]
        out_ref, xn_ref, xn_sc = rest[n_epi:]

        @pl.when(pl.program_id(1) == 0)
        def _():
            xv = x_ref[...]
            y = xv * lax.rsqrt(jnp.mean(xv * xv, axis=-1, keepdims=True) + EPS)
            xn = (y * g_ref[...]).astype(MXU_DT)
            xn_sc[...] = xn
            xn_ref[...] = xn

        acc = jnp.dot(xn_sc[...], w_ref[...], preferred_element_type=F32)
        if epi is not None:
            acc = epi(acc, *[r[...] for r in epi_refs])
        out_ref[...] = acc.astype(out_dtype)

    return _call(
        body, (x, gain, w, *epi_ops), carry, grid=(rows // tm, n // tn),
        in_specs=[pl.BlockSpec((tm, kdim), lambda i, j: (i, xcol)), pl.BlockSpec((1, kdim), lambda i, j: (0, 0)),
                  w_spec, *epi_specs],
        out_specs=[pl.BlockSpec((tm, tn), lambda i, j: (i, j)), pl.BlockSpec((tm, kdim), lambda i, j: (i, 0))],
        out_shape=[S((rows, n), out_dtype), S((rows, kdim), MXU_DT)],
        scratch_shapes=[pltpu.VMEM((tm, kdim), MXU_DT)], sem=("parallel", "arbitrary"), name=name)


def matmul_res(a, w, res, tm, tn, name):
    grp, rows, k = a.shape
    n = w.shape[2]

    def body(a_ref, w_ref, r_ref, o_ref):
        acc = r_ref[...]
        for g in range(grp):
            acc = acc + jnp.dot(a_ref[g], w_ref[g], preferred_element_type=F32)
        o_ref[...] = acc

    return pl.pallas_call(
        body, grid=(rows // tm, n // tn),
        in_specs=[pl.BlockSpec((grp, tm, k), lambda i, j: (0, i, 0)), pl.BlockSpec((grp, k, tn), lambda i, j: (0, 0, j)),
                  pl.BlockSpec((tm, tn), lambda i, j: (i, j))],
        out_specs=pl.BlockSpec((tm, tn), lambda i, j: (i, j)),
        out_shape=S((rows, n), F32), compiler_params=_cp(("parallel", "parallel")), name=name)(a, w, res)


def matmul_nt(a, w, tm, tn, out_dtype, name):
    rows, k = a.shape
    n = w.shape[0]

    def body(a_ref, w_ref, o_ref):
        o_ref[...] = lax.dot_general(a_ref[...].astype(MXU_DT), w_ref[...], NT_DIMS,
                                     preferred_element_type=F32).astype(out_dtype)

    return pl.pallas_call(
        body, grid=(rows // tm, n // tn),
        in_specs=[pl.BlockSpec((tm, k), lambda i, j: (i, 0)), pl.BlockSpec((tn, k), lambda i, j: (j, 0))],
        out_specs=pl.BlockSpec((tm, tn), lambda i, j: (i, j)),
        out_shape=S((rows, n), out_dtype), compiler_params=_cp(("parallel", "parallel")), name=name)(a, w)


def matmul_nt_normbwd(du, w, x, xcol, gain, res, tm, tk, out_dtype, name):
    rows, kc = du.shape[-2], mat_cols(du)
    dn = w.shape[-2]
    nk = kc // tk
    has_res = res is not None
    w_spec = (pl.BlockSpec((dn, tk), lambda i, k: (0, k)) if w.ndim == 2 else
              pl.BlockSpec((None, dn, tk), lambda i, k: (k // (w.shape[2] // tk), 0, k % (w.shape[2] // tk))))

    def body(du_ref, w_ref, x_ref, g_ref, *rest):
        if has_res:
            res_ref, dx_ref, dg_ref, acc = rest
        else:
            dx_ref, dg_ref, acc = rest
        i, k = pl.program_id(0), pl.program_id(1)

        @pl.when(k == 0)
        def _():
            acc[...] = jnp.zeros_like(acc)

        @pl.when((i == 0) & (k == 0))
        def _():
            dg_ref[...] = jnp.zeros_like(dg_ref)

        acc[...] += lax.dot_general(du_ref[...], w_ref[...], NT_DIMS, preferred_element_type=F32)

        @pl.when(k == nk - 1)
        def _():
            dhn = acc[...]
            xv = x_ref[...]
            rstd = lax.rsqrt(jnp.mean(xv * xv, axis=-1, keepdims=True) + EPS)
            xhat = xv * rstd
            dg_ref[...] += jnp.sum(dhn * xhat, axis=0, keepdims=True)
            dxh = dhn * g_ref[...]
            dx = rstd * (dxh - xhat * jnp.mean(dxh * xhat, axis=-1, keepdims=True))
            if has_res:
                dx = dx + res_ref[...]
            dx_ref[...] = dx.astype(out_dtype)

    in_specs = [mat_spec(du, tm, tk, lambda i, k: (i, k)), w_spec,
                pl.BlockSpec((tm, dn), lambda i, k: (i, xcol)), pl.BlockSpec((1, dn), lambda i, k: (0, 0))]
    ops = [du, w, x, gain]
    if has_res:
        in_specs.append(pl.BlockSpec((tm, dn), lambda i, k: (i, 0)))
        ops.append(res)
    return pl.pallas_call(
        body, grid=(rows // tm, nk), in_specs=in_specs,
        out_specs=[pl.BlockSpec((tm, dn), lambda i, k: (i, 0)), pl.BlockSpec((1, dn), lambda i, k: (0, 0))],
        out_shape=[S((rows, dn), out_dtype), S((1, dn), F32)],
        scratch_shapes=[pltpu.VMEM((tm, dn), F32)],
        compiler_params=_cp(("arbitrary", "arbitrary")), name=name)(*ops)


def matmul_tn(a, b, tr, name, carry=None, col_shards=1):
    rows, ka, nb = a.shape[-2], mat_cols(a), mat_cols(b)
    ta = _div_tile(mat_width(a), 1536, LANE)
    tb = _div_tile(mat_width(b), 1536 if ta <= 1024 else 1024, LANE)
    nr = rows // tr
    if col_shards == 1:
        out_spec, out_shape = pl.BlockSpec((ta, tb), lambda i, j, r: (i, j)), S((ka, nb), F32)
    else:
        per = nb // col_shards // tb
        assert per * tb * col_shards == nb
        out_spec = pl.BlockSpec((None, ta, tb), lambda i, j, r: (j // per, i, j % per))
        out_shape = S((col_shards, ka, nb // col_shards), F32)

    def body(a_ref, b_ref, o_ref, acc):
        r = pl.program_id(2)

        @pl.when(r == 0)
        def _():
            acc[...] = jnp.zeros_like(acc)

        acc[...] += lax.dot_general(a_ref[...].astype(MXU_DT), b_ref[...].astype(MXU_DT), TN_DIMS,
                                    preferred_element_type=F32)

        @pl.when(r == nr - 1)
        def _():
            o_ref[...] = acc[...]

    return _call(
        body, (a, b), carry, grid=(ka // ta, nb // tb, nr),
        in_specs=[mat_spec(a, tr, ta, lambda i, j, r: (r, i)), mat_spec(b, tr, tb, lambda i, j, r: (r, j))],
        out_specs=out_spec, out_shape=out_shape, scratch_shapes=[pltpu.VMEM((ta, tb), F32)],
        sem=("parallel", "parallel", "arbitrary"), name=name)


def _sigmoid(x):
    return 1.0 / (1.0 + jnp.exp(-x))


def _log1p(e):
    return jnp.where(e < 1e-3, e * (1.0 - e * (0.5 - e * (1.0 / 3.0 - 0.25 * e))), jnp.log(1.0 + e))


def _softplus(x):
    return jnp.maximum(x, 0.0) + _log1p(jnp.exp(-jnp.abs(x)))


def _expm1(x):
    series = x * (1.0 + x * (0.5 + x * (1.0 / 6.0 + x * (1.0 / 24.0 + x * (1.0 / 120.0)))))
    return jnp.where(jnp.abs(x) < 0.1, series, jnp.exp(x) - 1.0)


_GELU_K = math.sqrt(2.0 / math.pi)
_GELU_C = 0.044715


def _gelu_and_grad(x):
    th = jnp.tanh(_GELU_K * (x + _GELU_C * x * x * x))
    g = 0.5 * x * (1.0 + th)
    dg = 0.5 * (1.0 + th) + 0.5 * x * (1.0 - th * th) * _GELU_K * (1.0 + 3.0 * _GELU_C * x * x)
    return g, dg


def _row_iota(shape):
    return lax.broadcasted_iota(jnp.int32, shape, 0)


def _scan_chunk_fwd(a_sc, u_sc, out_ref, hcar, n, width):
    rowi = _row_iota((SUBLANE, width))

    def step(c, hprev):
        r0 = pl.multiple_of(c * SUBLANE, SUBLANE)
        a = a_sc[pl.ds(r0, SUBLANE), :]
        u = u_sc[pl.ds(r0, SUBLANE), :]
        for d in (1, 2, 4):
            a_s = jnp.where(rowi >= d, pltpu.roll(a, d, axis=0), 1.0)
            u_s = jnp.where(rowi >= d, pltpu.roll(u, d, axis=0), 0.0)
            u = u + a * u_s
            a = a * a_s
        h = u + a * hprev
        out_ref[pl.ds(r0, SUBLANE), :] = h
        return jnp.broadcast_to(h[SUBLANE - 1:SUBLANE, :], (SUBLANE, width))

    hcar[...] = lax.fori_loop(0, n // SUBLANE, step, hcar[...], unroll=4)


def _scan_chunk_bwd(b_sc, d_sc, out_ref, gcar, n, width):
    rowi = _row_iota((SUBLANE, width))
    nc = n // SUBLANE

    def step(c, gnext):
        r0 = pl.multiple_of((nc - 1 - c) * SUBLANE, SUBLANE)
        b = b_sc[pl.ds(r0, SUBLANE), :]
        d = d_sc[pl.ds(r0, SUBLANE), :]
        for s in (1, 2, 4):
            keep = rowi < SUBLANE - s
            b_s = jnp.where(keep, pltpu.roll(b, SUBLANE - s, axis=0), 1.0)
            d_s = jnp.where(keep, pltpu.roll(d, SUBLANE - s, axis=0), 0.0)
            d = d + b * d_s
            b = b * b_s
        g = d + b * gnext
        out_ref[pl.ds(r0, SUBLANE), :] = g
        return jnp.broadcast_to(g[0:1, :], (SUBLANE, width))

    gcar[...] = lax.fori_loop(0, nc, step, gcar[...], unroll=4)


def even_mid_fwd(u, conv_a, conv_b, conv_b_bias, rw, rb, iw, ib, lam, nb, tp, n, name, carry=None):
    rows = u.shape[0]
    w = LANE
    nj = CONV_W // w
    nt = tp // n
    h8 = SUBLANE

    def body(gb_r, gc_r, xa_r, xb_r, gate_r, ca_w, cb_w, cb_b, rw_r, rb_r, iw_r, ib_r, lam_r,
             y_o, ca_o, xc_o, a_o, hs_o, pext, xext, hcar, a_sc, u_sc):
        t = pl.program_id(2)

        @pl.when(t == 0)
        def _():
            pext[0:h8, :] = jnp.zeros((h8, w), F32)
            xext[0:h8, :] = jnp.zeros((h8, w), F32)
            hcar[...] = jnp.zeros_like(hcar)

        p = gc_r[...] * xa_r[...]
        pext[h8:h8 + n, :] = p
        wa = ca_w[...]
        ca = wa[2:3, :] * p + wa[1:2, :] * pext[h8 - 1:h8 - 1 + n, :] + wa[0:1, :] * pext[h8 - 2:h8 - 2 + n, :]
        ca_o[...] = ca
        y_o[0] = (gb_r[...] * ca).astype(MXU_DT)
        pext[0:h8, :] = pext[n:n + h8, :]

        xb = xb_r[...]
        xext[h8:h8 + n, :] = xb
        wb = cb_w[...]
        xc = (wb[3:4, :] * xb + wb[2:3, :] * xext[h8 - 1:h8 - 1 + n, :] + wb[1:2, :] * xext[h8 - 2:h8 - 2 + n, :]
              + wb[0:1, :] * xext[h8 - 3:h8 - 3 + n, :]) + cb_b[...]
        xc_o[...] = xc
        xext[0:h8, :] = xext[n:n + h8, :]

        xcm = xc.astype(MXU_DT)
        r = _sigmoid(jnp.dot(xcm, rw_r[...], preferred_element_type=F32) + rb_r[...])
        ig = _sigmoid(jnp.dot(xcm, iw_r[...], preferred_element_type=F32) + ib_r[...])
        log_a = (-LRU_C) * r * _softplus(-lam_r[...])
        a = jnp.exp(log_a)
        mult = jnp.sqrt(-_expm1(2.0 * log_a))
        a_sc[...] = a
        a_o[...] = a
        u_sc[...] = mult * (ig * xc)
        _scan_chunk_fwd(a_sc, u_sc, hs_o, hcar, n, w)
        gel, _ = _gelu_and_grad(gate_r[...])
        y_o[1] = (gel * hs_o[...]).astype(MXU_DT)

    def ublk(off):
        return pl.BlockSpec((n, w), lambda j, b, t: (b * nt + t, off + j))

    def pblk(r_):
        return pl.BlockSpec((r_, w), lambda j, b, t: (0, j))

    act = pl.BlockSpec((n, w), lambda j, b, t: (b * nt + t, j))
    mat = pl.BlockSpec((w, w), lambda j, b, t: (j, j))
    return _call(
        body, (u, u, u, u, u, conv_a, conv_b, conv_b_bias, rw, rb, iw, ib, lam), carry, grid=(nj, nb, nt),
        in_specs=[ublk(0), ublk(nj), ublk(2 * nj), ublk(3 * nj), ublk(4 * nj), pblk(3), pblk(4), pblk(1),
                  mat, pblk(1), mat, pblk(1), pblk(1)],
        out_specs=[pl.BlockSpec((2, n, w), lambda j, b, t: (0, b * nt + t, j)), act, act, act, act],
        out_shape=[S((2, rows, CONV_W), MXU_DT), S((rows, CONV_W), F32), S((rows, LRU_W), F32), S((rows, LRU_W), F32),
                   S((rows, LRU_W), F32)],
        scratch_shapes=[pltpu.VMEM((n + h8, w), F32), pltpu.VMEM((n + h8, w), F32), pltpu.VMEM((h8, w), F32),
                        pltpu.VMEM((n, w), F32), pltpu.VMEM((n, w), F32)],
        sem=("parallel", "parallel", "arbitrary"), name=name)


def even_mid_bwd(u, dycat, ca, xc, a_sv, hs, conv_a, conv_b, rw, rb, iw, ib, lam, nb, tp, n, name, carry=None):
    rows = u.shape[0]
    w = LANE
    nj = CONV_W // w
    nt = tp // n
    h8 = SUBLANE

    def body(gb_r, gc_r, xa_r, xb_r, gate_r, dya_r, dyb_r, ca_r, xc_r, a_r, hs_r, hsp_r,
             ca_w, cb_w, rw_r, rb_r, iw_r, ib_r, lam_r,
             du_o, dca_w, dcb_w, dcb_b, drw, drb, diw, dib, dlam,
             aext, hext, dext, eext, gcar, b_sc, d_sc, g_sc):
        b, t = pl.program_id(1), pl.program_id(2)

        @pl.when((b == 0) & (t == 0))
        def _():
            for ref in (dca_w, dcb_w, dcb_b, drw, drb, diw, dib, dlam):
                ref[...] = jnp.zeros_like(ref)

        @pl.when(t == 0)
        def _():
            aext[n:n + h8, :] = jnp.zeros((h8, w), F32)
            dext[n:n + h8, :] = jnp.zeros((h8, w), F32)
            eext[n:n + h8, :] = jnp.zeros((h8, w), F32)
            gcar[...] = jnp.zeros_like(gcar)

        xc_v = xc_r[...]
        xcm = xc_v.astype(MXU_DT)
        r = _sigmoid(jnp.dot(xcm, rw_r[...], preferred_element_type=F32) + rb_r[...])
        ig = _sigmoid(jnp.dot(xcm, iw_r[...], preferred_element_type=F32) + ib_r[...])
        lam_v = lam_r[...]
        sp = _softplus(-lam_v)
        log_a = (-LRU_C) * r * sp
        a = a_r[...]
        mult = jnp.sqrt(-_expm1(2.0 * log_a))
        hs_v = hs_r[...]
        gel, dgel = _gelu_and_grad(gate_r[...])
        dyb = dyb_r[...]
        du_o[4] = (dyb * hs_v * dgel).astype(MXU_DT)

        aext[0:n, :] = a
        b_sc[...] = aext[1:1 + n, :]
        d_sc[...] = dyb * gel
        _scan_chunk_bwd(b_sc, d_sc, g_sc, gcar, n, w)
        aext[n:n + h8, :] = aext[0:h8, :]
        g = g_sc[...]

        hext[0:h8, :] = jnp.where(t == nt - 1, 0.0, hsp_r[...])
        hext[h8:h8 + n, :] = hs_v
        da = g * hext[h8 - 1:h8 - 1 + n, :]
        dmult = g * (ig * xc_v)
        di = g * mult * xc_v
        dxc = g * mult * ig
        dlog_a = da * a - dmult * (a * a) / mult
        dr = dlog_a * ((-LRU_C) * sp)
        dsp = jnp.sum(dlog_a * ((-LRU_C) * r), axis=0, keepdims=True)
        dlam[...] += dsp * (-_sigmoid(-lam_v))
        dzr = dr * r * (1.0 - r)
        dzi = di * ig * (1.0 - ig)
        dzr_m = dzr.astype(MXU_DT)
        dzi_m = dzi.astype(MXU_DT)
        dxc = (dxc + lax.dot_general(dzr_m, rw_r[...], NT_DIMS, preferred_element_type=F32)
               + lax.dot_general(dzi_m, iw_r[...], NT_DIMS, preferred_element_type=F32))
        drw[...] += lax.dot_general(xcm, dzr_m, TN_DIMS, preferred_element_type=F32)
        diw[...] += lax.dot_general(xcm, dzi_m, TN_DIMS, preferred_element_type=F32)
        drb[...] += jnp.sum(dzr, axis=0, keepdims=True)
        dib[...] += jnp.sum(dzi, axis=0, keepdims=True)
        dcb_b[...] += jnp.sum(dxc, axis=0, keepdims=True)

        xb = xb_r[...]
        dext[0:n, :] = dxc
        wb = cb_w[...]
        d1, d2, d3 = dext[1:1 + n, :], dext[2:2 + n, :], dext[3:3 + n, :]
        du_o[3] = (wb[3:4, :] * dxc + wb[2:3, :] * d1 + wb[1:2, :] * d2 + wb[0:1, :] * d3).astype(MXU_DT)
        dcb_w[3:4, :] += jnp.sum(xb * dxc, axis=0, keepdims=True)
        dcb_w[2:3, :] += jnp.sum(xb * d1, axis=0, keepdims=True)
        dcb_w[1:2, :] += jnp.sum(xb * d2, axis=0, keepdims=True)
        dcb_w[0:1, :] += jnp.sum(xb * d3, axis=0, keepdims=True)
        dext[n:n + h8, :] = dext[0:h8, :]

        gb, gc, xa = gb_r[...], gc_r[...], xa_r[...]
        dya = dya_r[...]
        du_o[0] = (dya * ca_r[...]).astype(MXU_DT)
        dca = dya * gb
        eext[0:n, :] = dca
        wa = ca_w[...]
        e1, e2 = eext[1:1 + n, :], eext[2:2 + n, :]
        dp = wa[2:3, :] * dca + wa[1:2, :] * e1 + wa[0:1, :] * e2
        p = gc * xa
        dca_w[2:3, :] += jnp.sum(p * dca, axis=0, keepdims=True)
        dca_w[1:2, :] += jnp.sum(p * e1, axis=0, keepdims=True)
        dca_w[0:1, :] += jnp.sum(p * e2, axis=0, keepdims=True)
        eext[n:n + h8, :] = eext[0:h8, :]
        du_o[1] = (dp * xa).astype(MXU_DT)
        du_o[2] = (dp * gc).astype(MXU_DT)

    def rt(b, t):
        return b * nt + (nt - 1 - t)

    def ublk(off):
        return pl.BlockSpec((n, w), lambda j, b, t: (rt(b, t), off + j))

    def pblk(r_):
        return pl.BlockSpec((r_, w), lambda j, b, t: (0, j))

    act = pl.BlockSpec((n, w), lambda j, b, t: (rt(b, t), j))
    n8 = n // h8
    hsp = pl.BlockSpec((h8, w), lambda j, b, t: (jnp.maximum(rt(b, t) * n8 - 1, 0), j))
    mat = pl.BlockSpec((w, w), lambda j, b, t: (j, j))
    return _call(
        body, (u, u, u, u, u, dycat, dycat, ca, xc, a_sv, hs, hs, conv_a, conv_b, rw, rb, iw, ib, lam), carry,
        grid=(nj, nb, nt),
        in_specs=[ublk(0), ublk(nj), ublk(2 * nj), ublk(3 * nj), ublk(4 * nj), ublk(0), ublk(nj), act, act, act, act,
                  hsp, pblk(3), pblk(4), mat, pblk(1), mat, pblk(1), pblk(1)],
        out_specs=[pl.BlockSpec((5, n, w), lambda j, b, t: (0, rt(b, t), j)), pblk(3), pblk(4), pblk(1),
                   mat, pblk(1), mat, pblk(1), pblk(1)],
        out_shape=[S((5, rows, CONV_W), MXU_DT), S((3, CONV_W), F32), S((4, LRU_W), F32), S((1, LRU_W), F32),
                   S((LRU_W, LRU_W), F32), S((1, LRU_W), F32), S((LRU_W, LRU_W), F32), S((1, LRU_W), F32),
                   S((1, LRU_W), F32)],
        scratch_shapes=[pltpu.VMEM((n + h8, w), F32)] * 4 + [pltpu.VMEM((h8, w), F32)] + [pltpu.VMEM((n, w), F32)] * 3,
        sem=("arbitrary", "arbitrary", "arbitrary"), name=name)


def ffn_mid_fwd(up, cw, cb, nb, tp, n, name):
    rows = up.shape[0]
    w = FFN_CT
    nj = D_FF // w
    nt = tp // n
    h8 = SUBLANE

    sr = STRIP_ROWS

    def body(xa_r, xg_r, w_r, b_r, u_o, y_o, halo):
        t = pl.program_id(2)

        @pl.when(t == 0)
        def _():
            halo[...] = jnp.zeros_like(halo)

        wv = (w_r[0], w_r[1])
        bv = (b_r[0], b_r[1])

        def strip(s, carry):
            r0 = pl.multiple_of(s * sr, sr)
            us, new = [], []
            for g, x_r in enumerate((xa_r, xg_r)):
                x = x_r[pl.ds(r0, sr), :].astype(F32)
                win = jnp.concatenate([carry[g], x], axis=0)
                x1 = pltpu.roll(win, 1, axis=0)[h8:, :]
                x2 = pltpu.roll(win, 2, axis=0)[h8:, :]
                u = (wv[g][2:3, :] * x + wv[g][1:2, :] * x1 + wv[g][0:1, :] * x2) + bv[g]
                u_o[g, pl.ds(r0, sr), :] = u.astype(MXU_DT)
                us.append(u)
                new.append(x[sr - h8:, :])
            y_o[pl.ds(r0, sr), :] = (us[0] * _sigmoid(us[0]) * us[1]).astype(MXU_DT)
            return tuple(new)

        ha, hg = lax.fori_loop(0, n // sr, strip, (halo[0], halo[1]))
        halo[0] = ha
        halo[1] = hg

    def ublk(off):
        return pl.BlockSpec((n, w), lambda j, b, t: (b * nt + t, off + j))

    return pl.pallas_call(
        body, grid=(nj, nb, nt),
        in_specs=[ublk(0), ublk(nj), pl.BlockSpec((2, 3, w), lambda j, b, t: (0, 0, j)),
                  pl.BlockSpec((2, 1, w), lambda j, b, t: (0, 0, j))],
        out_specs=[pl.BlockSpec((2, n, w), lambda j, b, t: (0, b * nt + t, j)), ublk(0)],
        out_shape=[S((2, rows, D_FF), MXU_DT), S((rows, D_FF), MXU_DT)],
        scratch_shapes=[pltpu.VMEM((2, h8, w), F32)],
        compiler_params=_cp(("parallel", "parallel", "arbitrary")), name=name,
    )(up, up, cw, cb)


def ffn_mid_bwd(dy, u, up, cw, nb, tp, n, name, carry=None):
    rows = up.shape[0]
    w = FFN_CT
    nj = D_FF // w
    nt = tp // n
    h8 = SUBLANE

    sr = STRIP_ROWS
    ns = n // sr

    def fold(v):
        acc = v[0:h8, :]
        for k in range(1, sr // h8):
            acc = acc + v[k * h8:(k + 1) * h8, :]
        return acc

    def body(dy_r, u_r, xa_r, xg_r, w_r, dx_o, dw, db, halo):
        b, t = pl.program_id(1), pl.program_id(2)

        @pl.when((b == 0) & (t == 0))
        def _():
            dw[...] = jnp.zeros_like(dw)
            db[...] = jnp.zeros_like(db)

        @pl.when(t == 0)
        def _():
            halo[...] = jnp.zeros_like(halo)

        wv = (w_r[0], w_r[1])

        def strip(s, carry):
            halos, sums = carry
            r0 = pl.multiple_of((ns - 1 - s) * sr, sr)
            dyv = dy_r[pl.ds(r0, sr), :].astype(F32)
            ua = u_r[0, pl.ds(r0, sr), :].astype(F32)
            ug = u_r[1, pl.ds(r0, sr), :].astype(F32)
            sg = _sigmoid(ua)
            dus = (dyv * ug * (sg * (1.0 + ua * (1.0 - sg))), dyv * (ua * sg))
            new_halos, new_sums = [], []
            for g, x_r in enumerate((xa_r, xg_r)):
                du = dus[g]
                win = jnp.concatenate([du, halos[g]], axis=0)
                d1 = pltpu.roll(win, sr + h8 - 1, axis=0)[0:sr, :]
                d2 = pltpu.roll(win, sr + h8 - 2, axis=0)[0:sr, :]
                dx_o[g, pl.ds(r0, sr), :] = (wv[g][2:3, :] * du + wv[g][1:2, :] * d1 + wv[g][0:1, :] * d2).astype(MXU_DT)
                x = x_r[pl.ds(r0, sr), :].astype(F32)
                s2, s1, s0, sb = sums[g]
                new_sums.append((s2 + fold(x * du), s1 + fold(x * d1), s0 + fold(x * d2), sb + fold(du)))
                new_halos.append(du[0:h8, :])
            return tuple(new_halos), tuple(new_sums)

        z = jnp.zeros((h8, w), F32)
        halos, sums = lax.fori_loop(0, ns, strip, ((halo[0], halo[1]), ((z, z, z, z), (z, z, z, z))))
        halo[0] = halos[0]
        halo[1] = halos[1]
        for g in range(2):
            s2, s1, s0, sb = sums[g]
            dw[g, 2:3, :] += jnp.sum(s2, axis=0, keepdims=True)
            dw[g, 1:2, :] += jnp.sum(s1, axis=0, keepdims=True)
            dw[g, 0:1, :] += jnp.sum(s0, axis=0, keepdims=True)
            db[g] += jnp.sum(sb, axis=0, keepdims=True)

    def rt(b, t):
        return b * nt + (nt - 1 - t)

    def ublk(off):
        return pl.BlockSpec((n, w), lambda j, b, t: (rt(b, t), off + j))

    pair = pl.BlockSpec((2, n, w), lambda j, b, t: (0, rt(b, t), j))
    return _call(
        body, (dy, u, up, up, cw), carry, grid=(nj, nb, nt),
        in_specs=[ublk(0), pair, ublk(0), ublk(nj), pl.BlockSpec((2, 3, w), lambda j, b, t: (0, 0, j))],
        out_specs=[pair, pl.BlockSpec((2, 3, w), lambda j, b, t: (0, 0, j)),
                   pl.BlockSpec((2, 1, w), lambda j, b, t: (0, 0, j))],
        out_shape=[S((2, rows, D_FF), MXU_DT), S((2, 3, D_FF), F32), S((2, 1, D_FF), F32)],
        scratch_shapes=[pltpu.VMEM((2, h8, w), F32)],
        sem=("arbitrary", "arbitrary", "arbitrary"), name=name)


def _lane_mod(shape):
    return lax.broadcasted_iota(jnp.int32, shape, 1) & (HP - 1)


def _q_rope_epi(acc, tab):
    reps = acc.shape[1] // HP
    a = acc * jnp.tile(tab, (1, reps))
    lane = _lane_mod(a.shape)
    shifted = pltpu.roll(a, a.shape[1] - QK_ROPE, axis=1)
    return jnp.where(lane < QK_NOPE, a, jnp.where(lane < QK_HEAD, a + shifted, 0.0)) * Q_PRESCALE


def _k_rope_block(krblk, tabk):
    a = krblk * tabk
    lane = _lane_mod(a.shape)
    b = a + pltpu.roll(a, HP - QK_ROPE, axis=1)
    return jnp.where((lane >= QK_NOPE) & (lane < QK_HEAD), b, 0.0)


def _k_rope_epi(acc, krblk, tabk):
    reps = acc.shape[1] // HP
    return acc + jnp.tile(_k_rope_block(krblk, tabk), (1, reps))


def attn_fwd(q, k, v, nb, tp, name, carry=None):
    rows = q.shape[0]
    blk = ATT_BLK
    nq = tp // blk
    npair = MLA_HEADS // 2

    def body(q_r, k_r, v_r, o_r, lse_r):
        qi = pl.program_id(2)
        lane = lax.broadcasted_iota(jnp.int32, (blk, LANE), 1)
        even = lane < V_HEAD
        sum_lane = (V_HEAD, 0)
        rowi = lax.broadcasted_iota(jnp.int32, (blk, blk), 0)
        coli = lax.broadcasted_iota(jnp.int32, (blk, blk), 1)
        qs = [q_r[:, h * HP:(h + 1) * HP] for h in range(2)]

        def kv_block(k0, width, carry, diagonal):
            ms, accs = carry
            vblk = v_r[pl.ds(k0, width), :]
            one = jnp.ones_like(vblk)
            zero = jnp.zeros_like(vblk)
            vlane = lax.broadcasted_iota(jnp.int32, (width, LANE), 1)
            new_ms, new_accs = [], []
            for h in range(2):
                kh = k_r[pl.ds(k0, width), h * HP:(h + 1) * HP]
                s = lax.dot_general(qs[h], kh, NT_DIMS, preferred_element_type=F32)
                if diagonal:
                    s = jnp.where(coli <= rowi, s, -jnp.inf)
                m_new = jnp.maximum(ms[h], jnp.max(s, axis=1, keepdims=True))
                alpha = jnp.exp2(ms[h] - m_new)
                p = jnp.exp2(s - m_new).astype(MXU_DT)
                mine = (vlane < V_HEAD) if h == 0 else (vlane >= V_HEAD)
                vh = jnp.where(mine, vblk, jnp.where(vlane == sum_lane[h], one, zero))
                new_accs.append(alpha * accs[h] + jnp.dot(p, vh, preferred_element_type=F32))
                new_ms.append(m_new)
            return tuple(new_ms), tuple(new_accs)

        neg = jnp.full((blk, 1), -jnp.inf, F32)
        zacc = jnp.zeros((blk, LANE), F32)
        carry = lax.fori_loop(0, qi // 2, lambda i, c: kv_block(pl.multiple_of(i * 2 * blk, blk), 2 * blk, c, False),
                              ((neg, neg), (zacc, zacc)))
        carry = lax.cond(qi % 2 == 1, lambda c: kv_block(pl.multiple_of((qi - 1) * blk, blk), blk, c, False),
                         lambda c: c, carry)
        ms, accs = kv_block(pl.multiple_of(qi * blk, blk), blk, carry, True)
        ls = [accs[h][:, sum_lane[h]:sum_lane[h] + 1] for h in range(2)]
        o_r[...] = jnp.where(even, accs[0] / ls[0], accs[1] / ls[1]).astype(MXU_DT)
        lse_r[...] = jnp.where(even, ms[0] + jnp.log2(ls[0]), ms[1] + jnp.log2(ls[1]))

    return _call(
        body, (q, k, v), carry, grid=(nb, npair, nq),
        in_specs=[pl.BlockSpec((blk, 2 * HP), lambda b, p, i: (b * nq + i, p)),
                  pl.BlockSpec((tp, 2 * HP), lambda b, p, i: (b, p)),
                  pl.BlockSpec((tp, LANE), lambda b, p, i: (b, p))],
        out_specs=[pl.BlockSpec((blk, LANE), lambda b, p, i: (b * nq + i, p)),
                   pl.BlockSpec((None, blk, LANE), lambda b, p, i: (p, b * nq + i, 0))],
        out_shape=[S((rows, MLA_HEADS * V_HEAD), MXU_DT), S((npair, rows, LANE), F32)], scratch_shapes=[],
        sem=("parallel", "parallel", "arbitrary"), name=name)


def attn_bwd(q, k, v, o, do, lse, nb, tp, name, carry=None):
    rows = q.shape[0]
    blk = ATT_BLK
    nq = tp // blk
    npair = MLA_HEADS // 2
    scale = QK_HEAD ** -0.5

    def body(q_r, k_r, v_r, o_r, do_r, lse_r, dq_o, dk_o, dv_o, dq_acc, delta_sc):
        kb = pl.program_id(2)
        even = lax.broadcasted_iota(jnp.int32, (blk, LANE), 1) < V_HEAD
        rowi = lax.broadcasted_iota(jnp.int32, (blk, blk), 0)
        coli = lax.broadcasted_iota(jnp.int32, (blk, blk), 1)

        @pl.when(kb == 0)
        def _():
            dq_acc[...] = jnp.zeros_like(dq_acc)

            def dstep(i, c):
                r0 = pl.multiple_of(i * blk, blk)
                prod = do_r[pl.ds(r0, blk), :].astype(F32) * o_r[pl.ds(r0, blk), :].astype(F32)
                de = jnp.sum(jnp.where(even, prod, 0.0), axis=1, keepdims=True)
                dd = jnp.sum(jnp.where(even, 0.0, prod), axis=1, keepdims=True)
                delta_sc[pl.ds(r0, blk), :] = jnp.where(even, de, dd)
                return c

            lax.fori_loop(0, nq, dstep, 0)

        vblk = v_r[...]
        ks = [k_r[:, h * HP:(h + 1) * HP] for h in range(2)]

        def q_block(r0, height, carry, diagonal):
            dk0, dk1, dv = carry
            dob = do_r[pl.ds(r0, height), :]
            lse_b = lse_r[pl.ds(r0, height), :]
            dl_b = delta_sc[pl.ds(r0, height), :]
            qlane = lax.broadcasted_iota(jnp.int32, (height, LANE), 1)
            dks = [dk0, dk1]
            for h in range(2):
                lo = 0 if h == 0 else V_HEAD
                qh = q_r[pl.ds(r0, height), h * HP:(h + 1) * HP]
                s = lax.dot_general(qh, ks[h], NT_DIMS, preferred_element_type=F32)
                p = jnp.exp2(s - lse_b[:, lo:lo + 1])
                if diagonal:
                    p = jnp.where(coli <= rowi, p, 0.0)
                mine = (qlane < V_HEAD) if h == 0 else (qlane >= V_HEAD)
                doh = jnp.where(mine, dob, jnp.zeros_like(dob))
                dp = lax.dot_general(doh, vblk, NT_DIMS, preferred_element_type=F32)
                ds = (p * (dp - dl_b[:, lo:lo + 1])).astype(MXU_DT)
                dv = dv + lax.dot_general(p.astype(MXU_DT), doh, TN_DIMS, preferred_element_type=F32)
                dks[h] = dks[h] + lax.dot_general(ds, qh, TN_DIMS, preferred_element_type=F32)
                dq_acc[pl.ds(r0, height), h * HP:(h + 1) * HP] += jnp.dot(ds, ks[h], preferred_element_type=F32)
            return dks[0], dks[1], dv

        z = jnp.zeros((blk, HP), F32)
        carry = q_block(pl.multiple_of(kb * blk, blk), blk, (z, z, jnp.zeros((blk, LANE), F32)), True)
        below = nq - 1 - kb
        carry = lax.fori_loop(
            0, below // 2, lambda i, c: q_block(pl.multiple_of((kb + 1 + 2 * i) * blk, blk), 2 * blk, c, False), carry)
        dk0, dk1, dv = lax.cond(below % 2 == 1, lambda c: q_block(pl.multiple_of((nq - 1) * blk, blk), blk, c, False),
                                lambda c: c, carry)
        dk_o[:, 0:HP] = (dk0 * (scale / Q_PRESCALE)).astype(MXU_DT)
        dk_o[:, HP:2 * HP] = (dk1 * (scale / Q_PRESCALE)).astype(MXU_DT)
        dv_o[...] = dv.astype(MXU_DT)

        @pl.when(kb == nq - 1)
        def _():
            dq_o[...] = (dq_acc[...] * scale).astype(MXU_DT)

    seq_pair = pl.BlockSpec((tp, LANE), lambda b, p, kk: (b, p))
    return _call(
        body, (q, k, v, o, do, lse), carry, grid=(nb, npair, nq),
        in_specs=[pl.BlockSpec((tp, 2 * HP), lambda b, p, kk: (b, p)),
                  pl.BlockSpec((blk, 2 * HP), lambda b, p, kk: (b * nq + kk, p)),
                  pl.BlockSpec((blk, LANE), lambda b, p, kk: (b * nq + kk, p)),
                  seq_pair, seq_pair, pl.BlockSpec((None, tp, LANE), lambda b, p, kk: (p, b, 0))],
        out_specs=[pl.BlockSpec((tp, 2 * HP), lambda b, p, kk: (b, p)),
                   pl.BlockSpec((blk, 2 * HP), lambda b, p, kk: (b * nq + kk, p)),
                   pl.BlockSpec((blk, LANE), lambda b, p, kk: (b * nq + kk, p))],
        out_shape=[S((rows, MLA_HEADS * HP), MXU_DT), S((rows, MLA_HEADS * HP), MXU_DT),
                   S((rows, MLA_HEADS * V_HEAD), MXU_DT)],
        scratch_shapes=[pltpu.VMEM((tp, 2 * HP), F32), pltpu.VMEM((tp, LANE), F32)],
        sem=("parallel", "parallel", "arbitrary"), name=name)


def rope_bwd(dq, dk, dv, tabq, tabk, tp, tm, name):
    rows = dq.shape[0]
    nt = tp // tm
    wq = MLA_HEADS * HP

    def body(dq_r, dk_r, dv_r, tq_r, tk_r, dqa_o, dkv_o, dkr_o):
        dqv = dq_r[...].astype(F32)
        lane = _lane_mod(dqv.shape)
        in_rope = (lane >= QK_NOPE) & (lane < QK_HEAD)
        rope = jnp.where(in_rope, dqv, 0.0)
        da = jnp.where(lane < QK_HEAD, dqv, 0.0) + pltpu.roll(rope, QK_ROPE, axis=1)
        dqa_o[...] = (da * jnp.tile(tq_r[...], (1, MLA_HEADS))).astype(MXU_DT)
        dkf = dk_r[...].astype(F32)
        dkv_o[:, 0:wq] = jnp.where(lane < QK_NOPE, dkf, 0.0).astype(MXU_DT)
        dkv_o[:, wq:] = dv_r[...]
        kr = jnp.where(in_rope, dkf, 0.0)
        tot = kr[:, 0:HP]
        for h in range(1, MLA_HEADS):
            tot = tot + kr[:, h * HP:(h + 1) * HP]
        dkr_o[...] = ((tot + pltpu.roll(tot, QK_ROPE, axis=1)) * tk_r[...]).astype(MXU_DT)

    def rowblk(wd):
        return pl.BlockSpec((tm, wd), lambda i: (i, 0))

    tab = pl.BlockSpec((tm, HP), lambda i: (i % nt, 0))
    return pl.pallas_call(
        body, grid=(rows // tm,), in_specs=[rowblk(wq), rowblk(wq), rowblk(MLA_HEADS * V_HEAD), tab, tab],
        out_specs=[rowblk(wq), rowblk(wq + MLA_HEADS * V_HEAD), rowblk(HP)],
        out_shape=[S((rows, wq), MXU_DT), S((rows, wq + MLA_HEADS * V_HEAD), MXU_DT), S((rows, HP), MXU_DT)],
        compiler_params=_cp(("parallel",)), name=name)(dq, dk, dv, tabq, tabk)


def loss_head(h, target, gain, tp, t_real, tm, name):
    rows = h.shape[0]
    nt = tp // tm

    def body(h_r, t_r, g_r, dh_o, loss_o, dg_o):
        i = pl.program_id(0)

        @pl.when(i == 0)
        def _():
            loss_o[...] = jnp.zeros_like(loss_o)
            dg_o[...] = jnp.zeros_like(dg_o)

        xv = h_r[...]
        rstd = lax.rsqrt(jnp.mean(xv * xv, axis=-1, keepdims=True) + EPS)
        xhat = xv * rstd
        g = g_r[...]
        pos = (i % nt) * tm + lax.broadcasted_iota(jnp.int32, (tm, 1), 0)
        valid = (pos >= N_META) & (pos < t_real)
        err = jnp.where(valid, xhat * g - t_r[...], 0.0)
        loss_o[...] += 0.5 * jnp.sum(jnp.mean(err * err, axis=-1, keepdims=True))
        dy = err * (1.0 / D_MODEL)
        dg_o[...] += jnp.sum(dy * xhat, axis=0, keepdims=True)
        dxh = dy * g
        dh_o[...] = rstd * (dxh - xhat * jnp.mean(dxh * xhat, axis=-1, keepdims=True))

    blk = pl.BlockSpec((tm, D_MODEL), lambda i: (i, 0))
    return pl.pallas_call(
        body, grid=(rows // tm,), in_specs=[blk, blk, pl.BlockSpec((1, D_MODEL), lambda i: (0, 0))],
        out_specs=[blk, pl.BlockSpec((1, LANE), lambda i: (0, 0)), pl.BlockSpec((1, D_MODEL), lambda i: (0, 0))],
        out_shape=[S((rows, D_MODEL), F32), S((1, LANE), F32), S((1, D_MODEL), F32)],
        compiler_params=_cp(("arbitrary",)), name=name)(h, target, gain)


ADAM_TILE_ELEMS = 128 * 1024


def adamw(g, w, m, v, name, carry=None):
    shape = w.shape
    cols = shape[-1]
    rws = max(1, math.prod(shape[:-1]))
    tr = rws if rws * cols <= ADAM_TILE_ELEMS else _div_tile(rws, max(SUBLANE, ADAM_TILE_ELEMS // cols), SUBLANE)
    bc1 = 1.0 - ADAM_B1 ** ADAM_STEP
    bc2 = 1.0 - ADAM_B2 ** ADAM_STEP

    def body(g_r, w_r, m_r, v_r, go, do, mo, vo):
        gv = g_r[...]
        mn = ADAM_B1 * m_r[...] + (1.0 - ADAM_B1) * gv
        vn = ADAM_B2 * v_r[...] + (1.0 - ADAM_B2) * (gv * gv)
        m_hat = mn / bc1
        v_hat = vn / bc2
        go[...] = gv
        do[...] = -ADAM_LR * (m_hat / (jnp.sqrt(v_hat) + ADAM_EPS) + ADAM_WD * w_r[...])
        mo[...] = mn
        vo[...] = vn

    blk = pl.BlockSpec((tr, cols), lambda i: (i, 0))
    outs, got = _call(
        body, [a.reshape(rws, cols) for a in (g, w, m, v)], carry, grid=(rws // tr,), in_specs=[blk] * 4,
        out_specs=[blk] * 4, out_shape=[S((rws, cols), F32)] * 4, scratch_shapes=[], sem=("parallel",), name=name)
    return tuple(o.reshape(shape) for o in outs), got


SUM_ROWS = 512


def _place():
    return lax.axis_index("x"), lax.axis_index("y"), lax.axis_index("c")


def _remote(src, dst, send_sems, recv_sems, k, to):
    return pltpu.make_async_remote_copy(src_ref=src, dst_ref=dst, send_sem=send_sems.at[k], recv_sem=recv_sems.at[k],
                                        device_id=to, device_id_type=MESH)


def chip_index():
    return 2 * lax.axis_index("x") + lax.axis_index("y")


def stage_gather_chips(xs):
    _, rws, _ = xs.shape

    def copies(ins, outs, sems):
        (x_ref,), (out_ref,), (send_sems, recv_sems) = ins, outs, sems
        mx, my, mc = _place()
        sibling = (mx, my, 1 - mc)
        chips = [(1 - mx, my), (mx, 1 - my), (1 - mx, 1 - my)]

        def piece(cx, cy, h):
            return out_ref.at[2 * cx + cy, h]

        first = [_remote(x_ref.at[mc], piece(mx, my, mc), send_sems, recv_sems, j, (cx, cy, mc))
                 for j, (cx, cy) in enumerate(chips)]
        landed = [_remote(x_ref.at[mc], piece(cx, cy, mc), send_sems, recv_sems, j, (cx, cy, mc))
                  for j, (cx, cy) in enumerate(chips)]
        passed = [_remote(piece(cx, cy, mc), piece(cx, cy, mc), send_sems, recv_sems, 3 + j, sibling)
                  for j, (cx, cy) in enumerate(chips)]
        from_sibling = [_remote(x_ref.at[mc], piece(cx, cy, 1 - mc), send_sems, recv_sems, 3 + j, sibling)
                        for j, (cx, cy) in enumerate(chips)]
        return first, landed, passed, from_sibling

    def start(ins, outs, sems):
        for cp in copies(ins, outs, sems)[0]:
            cp.start()

    def finish(ins, outs, sems):
        first, landed, passed, from_sibling = copies(ins, outs, sems)
        for j in range(3):
            landed[j].wait_recv()
            passed[j].start()
        for cp in from_sibling:
            cp.wait_recv()
        for cp in first + passed:
            cp.wait_send()

    return Stage([xs], [S((4, 2, rws, LANE), xs.dtype)],
                 [pltpu.SemaphoreType.DMA((6,)), pltpu.SemaphoreType.DMA((6,))], start, finish)


def own_block(gathered, xs):
    return lax.dynamic_update_slice(gathered, xs[None], (chip_index(), 0, 0, 0))


def stage_pair_exchange(g4):
    _, _, rws, _ = g4.shape

    def copies(ins, outs, sems):
        (g_ref,), (land_ref,), (send_sems, recv_sems) = ins, outs, sems
        mx, my, mc = _place()
        return [_remote(g_ref.at[s, 1 - mc], land_ref.at[s], send_sems, recv_sems, s, (mx, my, 1 - mc))
                for s in range(4)]

    def start(ins, outs, sems):
        for cp in copies(ins, outs, sems):
            cp.start()

    def finish(ins, outs, sems):
        cps = copies(ins, outs, sems)
        for cp in cps:
            cp.wait_recv()
        for cp in cps:
            cp.wait_send()

    return Stage([g4], [S((4, rws, LANE), g4.dtype)],
                 [pltpu.SemaphoreType.DMA((4,)), pltpu.SemaphoreType.DMA((4,))], start, finish)


def pair_sum(g4, land, c_idx, name):
    _, _, rws, _ = g4.shape
    th = SUM_ROWS

    def body(c_ref, a_ref, b_ref, o_ref):
        o_ref[...] = a_ref[...] + b_ref[...]

    return pl.pallas_call(
        body,
        grid_spec=pltpu.PrefetchScalarGridSpec(
            num_scalar_prefetch=1, grid=(4, rws // th),
            in_specs=[pl.BlockSpec((None, None, th, LANE), lambda s, i, c: (s, c[0], i, 0)),
                      pl.BlockSpec((None, th, LANE), lambda s, i, c: (s, i, 0))],
            out_specs=pl.BlockSpec((None, th, LANE), lambda s, i, c: (s, i, 0))),
        out_shape=S((4, rws, LANE), F32), compiler_params=_cp(("parallel", "parallel")), name=name)(c_idx, g4, land)


def stage_chip_scatter(p4):
    _, rws, _ = p4.shape

    def copies(ins, outs, sems):
        (p_ref,), (land_ref,), (send_sems, recv_sems) = ins, outs, sems
        mx, my, mc = _place()
        me = 2 * mx + my
        chips = [(1 - mx, my), (mx, 1 - my), (1 - mx, 1 - my)]
        sent = [_remote(p_ref.at[2 * cx + cy], land_ref.at[me], send_sems, recv_sems, j, (cx, cy, mc))
                for j, (cx, cy) in enumerate(chips)]
        landed = [_remote(p_ref.at[me], land_ref.at[2 * cx + cy], send_sems, recv_sems, j, (cx, cy, mc))
                  for j, (cx, cy) in enumerate(chips)]
        return sent, landed

    def start(ins, outs, sems):
        for cp in copies(ins, outs, sems)[0]:
            cp.start()

    def finish(ins, outs, sems):
        sent, landed = copies(ins, outs, sems)
        for cp in landed:
            cp.wait_recv()
        for cp in sent:
            cp.wait_send()

    return Stage([p4], [S((4, rws, LANE), p4.dtype)],
                 [pltpu.SemaphoreType.DMA((3,)), pltpu.SemaphoreType.DMA((3,))], start, finish)


def chip_sum(l4, p4, me_idx, name):
    _, rws, _ = l4.shape
    th = SUM_ROWS

    def body(me_ref, a, b, c, d, own, o_ref):
        me = me_ref[0]
        parts = [jnp.where(me == s, own[...], r[...]) for s, r in enumerate((a, b, c, d))]
        o_ref[...] = ((parts[0] + parts[1]) + parts[2]) + parts[3]

    def blk(s):
        return pl.BlockSpec((None, th, LANE), lambda i, me: (jnp.where(me[0] == s, (s + 1) % 4, s), i, 0))

    return pl.pallas_call(
        body,
        grid_spec=pltpu.PrefetchScalarGridSpec(
            num_scalar_prefetch=1, grid=(rws // th,),
            in_specs=[blk(0), blk(1), blk(2), blk(3), pl.BlockSpec((None, th, LANE), lambda i, me: (me[0], i, 0))],
            out_specs=pl.BlockSpec((th, LANE), lambda i, me: (i, 0))),
        out_shape=S((rws, LANE), F32), compiler_params=_cp(("parallel",)), name=name)(me_idx, l4, l4, l4, l4, p4)


def stage_pair_gather(rh):
    rws, _ = rh.shape

    def copy(ins, outs, sems):
        (r_ref,), (out_ref,), (send_sems, recv_sems) = ins, outs, sems
        mx, my, mc = _place()
        return _remote(r_ref, out_ref, send_sems, recv_sems, 0, (mx, my, 1 - mc))

    def start(ins, outs, sems):
        copy(ins, outs, sems).start()

    def finish(ins, outs, sems):
        copy(ins, outs, sems).wait()

    return Stage([rh], [S((rws, LANE), rh.dtype)],
                 [pltpu.SemaphoreType.DMA((1,)), pltpu.SemaphoreType.DMA((1,))], start, finish)


PACK_ELEMS = 16 * LANE


def pack_rows(arrays, lead, total_mult):
    parts, offs, r0 = [], [], 0
    for a in arrays:
        flat = a.reshape(a.shape[:lead] + (-1,))
        elems = _round_up(flat.shape[-1], PACK_ELEMS)
        flat = jnp.pad(flat, [(0, 0)] * lead + [(0, elems - flat.shape[-1])])
        parts.append(flat.reshape(flat.shape[:lead] + (elems // LANE, LANE)))
        offs.append((r0, elems // LANE))
        r0 += elems // LANE
    total = _round_up(r0, total_mult)
    if total > r0:
        parts.append(jnp.zeros(parts[0].shape[:lead] + (total - r0, LANE), parts[0].dtype))
    return jnp.concatenate(parts, axis=lead), offs


def unpack_rows(buf, off, shape):
    r0, nr = off
    lead = buf.shape[:-2]
    n = math.prod(shape)
    return buf[..., r0:r0 + nr, :].reshape(lead + (nr * LANE,))[..., :n].reshape(lead + tuple(shape))


def unshard(stacked, axis):
    x = jnp.moveaxis(stacked, 0, axis)
    return x.reshape(x.shape[:axis] + (4 * x.shape[axis + 1],) + x.shape[axis + 2:])


def to_shards(full, axis):
    n = full.shape[axis] // 4
    x = full.reshape(full.shape[:axis] + (4, n) + full.shape[axis + 1:])
    return jnp.moveaxis(x, axis, 0)


def _rot_cols(w):
    half = w.shape[-1] // 2
    return jnp.concatenate([-w[..., half:], w[..., :half]], axis=-1)


def _unrot_cols(dw):
    half = dw.shape[-1] // 2
    return jnp.concatenate([dw[..., half:], -dw[..., :half]], axis=-1)


def odd_w_in_padded(w_in):
    kr = w_in[:, Q_LORA + KV_LORA:]
    rows = w_in.shape[0]
    return jnp.concatenate([w_in[:, :Q_LORA], jnp.zeros((rows, 128), w_in.dtype), w_in[:, Q_LORA:Q_LORA + KV_LORA],
                            jnp.zeros((rows, 64), w_in.dtype), kr, _rot_cols(kr)], axis=1)


def odd_w_in_unpad(dwp):
    base = 512 + KV_LORA + 64
    dkr = dwp[:, base:base + QK_ROPE] + _unrot_cols(dwp[:, base + QK_ROPE:base + 2 * QK_ROPE])
    return jnp.concatenate([dwp[:, :Q_LORA], dwp[:, 512:512 + KV_LORA], dkr], axis=1)


def uq_padded(w_uq):
    w = w_uq.reshape(Q_LORA, MLA_HEADS, QK_HEAD)
    return jnp.concatenate([w, _rot_cols(w[:, :, QK_NOPE:])], axis=-1).reshape(Q_LORA, MLA_HEADS * HP)


def uq_unpad(dwp):
    d = dwp.reshape(Q_LORA, MLA_HEADS, HP)
    rope = d[:, :, QK_NOPE:QK_HEAD] + _unrot_cols(d[:, :, QK_HEAD:])
    return jnp.concatenate([d[:, :, :QK_NOPE], rope], axis=-1).reshape(Q_LORA, MLA_HEADS * QK_HEAD)


def ukv_padded(w_ukv):
    w = w_ukv.reshape(KV_LORA, MLA_HEADS, QK_NOPE + V_HEAD)
    wk = jnp.concatenate([w[:, :, :QK_NOPE], jnp.zeros((KV_LORA, MLA_HEADS, HP - QK_NOPE), w.dtype)], axis=-1)
    return jnp.concatenate([wk.reshape(KV_LORA, MLA_HEADS * HP), w[:, :, QK_NOPE:].reshape(KV_LORA, MLA_HEADS * V_HEAD)],
                           axis=1)


def ukv_unpad(dwp):
    dk = dwp[:, :MLA_HEADS * HP].reshape(KV_LORA, MLA_HEADS, HP)[:, :, :QK_NOPE]
    dv = dwp[:, MLA_HEADS * HP:].reshape(KV_LORA, MLA_HEADS, V_HEAD)
    return jnp.concatenate([dk, dv], axis=-1).reshape(KV_LORA, MLA_HEADS * (QK_NOPE + V_HEAD))


def block_diag(w):
    h, d, _ = w.shape
    eye = jnp.eye(h, dtype=w.dtype)
    return (eye[:, None, :, None] * w[:, :, None, :]).reshape(h * d, h * d)


def block_diag_part(dense, h):
    d = dense.shape[0] // h
    x = dense.reshape(h, d, h, d)
    return jnp.stack([x[i, :, i, :] for i in range(h)], axis=0)


def rope_tables(tp):
    pos = jnp.arange(tp, dtype=F32)
    inv_freq = ROPE_BASE ** (-jnp.arange(0, QK_ROPE, 2, dtype=F32) / QK_ROPE)
    ang = pos[:, None] * inv_freq[None, :]
    cos2 = jnp.tile(jnp.cos(ang), (1, 2))
    sin2 = jnp.tile(jnp.sin(ang), (1, 2))
    tabq = jnp.concatenate([jnp.ones((tp, QK_NOPE), F32), cos2, sin2], axis=1)
    tabk = jnp.concatenate([jnp.zeros((tp, QK_NOPE), F32), cos2, sin2], axis=1)
    return tabq, tabk


class Dims:
    def __init__(self, nb, seq):
        self.nb = nb
        self.t_real = seq + N_META
        self.tp = _round_up(self.t_real, ATT_BLK)
        self.n = self.tp // 4
        assert self.n % 16 == 0
        self.rows = nb * self.tp


class NoComm:
    def advance(self, carried):
        return None


def even_fwd(h, p, dm, comm):
    (u, hn), _ = norm_matmul(h, 0, D_MODEL, p["norm"], p["w_in"], dm.n, 512, F32, "ev_in")
    (y, ca, xc, a, hs), got = even_mid_fwd(u, p["conv_a"], p["conv_b"], p["conv_b_bias"], p["rw"], p["r_b"], p["iw"],
                                           p["i_b"], p["lam"], dm.nb, dm.tp, dm.n, "ev_mid", carry=comm.advance(None))
    comm.advance(got)
    out = matmul_res(y, p["w_out"].reshape(2, CONV_W, D_MODEL), h, dm.n, 512, "ev_out")
    return out, (h, u, hn, ca, xc, a, hs, y)


def even_bwd(dout, saved, p, dm, comm):
    h, u, hn, ca, xc, a, hs, y = saved
    g = {}
    dycat = matmul_nt(dout, p["w_out"], dm.n, 512, F32, "ev_dycat")
    g["w_out"], got = matmul_tn(y, dout, dm.n, "ev_dw_out", carry=comm.advance(None))
    outs, got = even_mid_bwd(u, dycat, ca, xc, a, hs, p["conv_a"], p["conv_b"], p["rw"], p["r_b"], p["iw"], p["i_b"],
                             p["lam"], dm.nb, dm.tp, dm.n, "ev_mid_bwd", carry=comm.advance(got))
    du, g["conv_a"], g["conv_b"], g["conv_b_bias"], drw, g["r_b"], diw, g["i_b"], g["lam"] = outs
    g["r_w"] = block_diag_part(drw, LRU_HEADS)
    g["i_w"] = block_diag_part(diw, LRU_HEADS)
    g["w_in"], got = matmul_tn(hn, du, dm.n, "ev_dw_in", carry=comm.advance(got))
    comm.advance(got)
    dx, g["norm"] = matmul_nt_normbwd(du, p["w_in"], h, 0, p["norm"], dout, dm.n, 512, F32, "ev_dx")
    return dx, g


def odd_fwd(h, p, tabq, tabk, dm, comm):
    nt = dm.tp // dm.n
    (u, hn), _ = norm_matmul(h, 0, D_MODEL, p["norm"], p["w_in_p"], dm.n, ODD_PAD, F32, "od_in")
    tab_spec = pl.BlockSpec((dm.n, HP), lambda i, j: (i % nt, 0))
    (q, cqn), _ = norm_matmul(u, 0, Q_LORA, p["q_norm"], p["w_uq_p"], dm.n, 512, MXU_DT, "od_q",
                              epi=_q_rope_epi, epi_ops=(tabq,), epi_specs=(tab_spec,))
    kr_spec = pl.BlockSpec((dm.n, HP), lambda i, j: (i, ODD_KR_COL))
    (k, ckvn), _ = norm_matmul(u, ODD_CKV_COL, KV_LORA, p["kv_norm"], p["w_uk_p"], dm.n, 512, MXU_DT, "od_k",
                               epi=_k_rope_epi, epi_ops=(u, tabk), epi_specs=(kr_spec, tab_spec))
    (v, _), _ = norm_matmul(u, ODD_CKV_COL, KV_LORA, p["kv_norm"], p["w_uv_p"], dm.n, 512, MXU_DT, "od_v")
    (o, lse), got = attn_fwd(q, k, v, dm.nb, dm.tp, "od_attn", carry=comm.advance(None))
    comm.advance(got)
    out = matmul_res(o[None], p["w_out"][None], h, dm.n, 512, "od_out")
    return out, (h, u, hn, cqn, ckvn, q, k, v, o, lse)


def odd_bwd(dout, saved, p, tabq, tabk, dm, comm):
    h, u, hn, cqn, ckvn, q, k, v, o, lse = saved
    g = {}
    do = matmul_nt(dout, p["w_out"], dm.n, 512, MXU_DT, "od_do")
    g["w_out"], got = matmul_tn(o, dout, dm.n, "od_dw_out", carry=comm.advance(None))
    (dq, dk, dv), got = attn_bwd(q, k, v, o, do, lse, dm.nb, dm.tp, "od_attn_bwd", carry=comm.advance(got))
    dqa, dkv, dkr = rope_bwd(dq, dk, dv, tabq, tabk, dm.tp, dm.n, "od_rope_bwd")
    g["w_uq_p"], got = matmul_tn(cqn, dqa, dm.n, "od_dw_uq", carry=comm.advance(got))
    comm.advance(got)
    g["w_ukv_p"], _ = matmul_tn(ckvn, dkv, dm.n, "od_dw_ukv")
    dcq, g["q_norm"] = matmul_nt_normbwd(dqa, p["w_uq_p"], u, 0, p["q_norm"], None, dm.n, 512, MXU_DT, "od_dcq")
    dckv, g["kv_norm"] = matmul_nt_normbwd(dkv, p["w_ukv_p"], u, ODD_CKV_COL, p["kv_norm"], None, dm.n, 512, MXU_DT,
                                           "od_dckv")
    du = jnp.concatenate([dcq, jnp.zeros((dm.rows, 128), MXU_DT), dckv, dkr], axis=1)
    g["w_in_p"], _ = matmul_tn(hn, du, dm.n, "od_dw_in")
    dx, g["norm"] = matmul_nt_normbwd(du, p["w_in_p"], h, 0, p["norm"], dout, dm.n, ODD_PAD, F32, "od_dx")
    return dx, g


def ffn_fwd(h, p, dm, comm):
    (up, hn), got = norm_matmul(h, 0, D_MODEL, p["norm"], p["w_up"], dm.n, D_FF // 2, MXU_DT, "ffn_up",
                                carry=comm.advance(None))
    comm.advance(got)
    u, y = ffn_mid_fwd(up, p["cw"], p["cb"], dm.nb, dm.tp, dm.n, "ffn_mid")
    out = matmul_res(y[None], p["w_down"][None], h, dm.n, 512, "ffn_down")
    return out, (h, up, hn, u, y)


def ffn_bwd(dout, saved, p, dm, comm):
    h, up, hn, u, y = saved
    g = {}
    dy = matmul_nt(dout, p["w_down"], dm.n, D_FF // 2, MXU_DT, "ffn_dy")
    g["w_down"], got = matmul_tn(y, dout, dm.n, "ffn_dw_down", carry=comm.advance(None))
    (dup, g["cw"], g["cb"]), got = ffn_mid_bwd(dy, u, up, p["cw"], dm.nb, dm.tp, dm.n, "ffn_mid_bwd",
                                               carry=comm.advance(got))
    g["w_up"], got = matmul_tn(hn, dup, dm.n, "ffn_dw_up", carry=comm.advance(got), col_shards=4)
    comm.advance(got)
    dx, g["norm"] = matmul_nt_normbwd(dup, p["w_up"], h, 0, p["norm"], dout, dm.n, D_FF // 2, F32, "ffn_dx")
    return dx, g


def _row(v):
    return v.reshape(1, -1)


def even_params(wf, j):
    return dict(norm=_row(wf["ev_norm"][j]), w_in=wf["ev_w_in"], conv_a=wf["ev_conv_a"][j], conv_b=wf["ev_conv_b"][j],
                conv_b_bias=_row(wf["ev_conv_b_bias"][j]), rw=block_diag(wf["ev_gate_r_w"][j]).astype(MXU_DT),
                r_b=_row(wf["ev_gate_r_b"][j]), iw=block_diag(wf["ev_gate_i_w"][j]).astype(MXU_DT),
                i_b=_row(wf["ev_gate_i_b"][j]), lam=_row(wf["ev_lru_lambda"][j]), w_out=wf["ev_w_out"])


def odd_params(wf, j):
    wkv = ukv_padded(wf["od_w_ukv"])
    return dict(norm=_row(wf["od_norm"][j]), w_in_p=odd_w_in_padded(wf["od_w_in"]), q_norm=_row(wf["od_q_norm"][j]),
                kv_norm=_row(wf["od_kv_norm"][j]), w_uq_p=uq_padded(wf["od_w_uq"]), w_ukv_p=wkv,
                w_uk_p=wkv[:, :MLA_HEADS * HP], w_uv_p=wkv[:, MLA_HEADS * HP:], w_out=wf["od_w_out"])


def ffn_params(wf, layer):
    return dict(norm=_row(wf["ffn_norm"][layer]), w_up=wf["ffn_w_up"],
                cw=jnp.moveaxis(wf["ffn_conv_w"][layer].reshape(3, 2, D_FF), 1, 0),
                cb=wf["ffn_conv_b"][layer].reshape(2, 1, D_FF), w_down=wf["ffn_w_down"])


def even_grads(g):
    out = {"ev_" + k_: g[k_] for k_ in ("w_in", "conv_a", "conv_b", "w_out")}
    out.update({"ev_norm": g["norm"][0], "ev_conv_b_bias": g["conv_b_bias"][0], "ev_gate_r_w": g["r_w"],
                "ev_gate_r_b": g["r_b"][0], "ev_gate_i_w": g["i_w"], "ev_gate_i_b": g["i_b"][0],
                "ev_lru_lambda": g["lam"][0]})
    return out


def odd_grads(g):
    return {"od_norm": g["norm"][0], "od_q_norm": g["q_norm"][0], "od_kv_norm": g["kv_norm"][0],
            "od_w_in": odd_w_in_unpad(g["w_in_p"]), "od_w_uq": uq_unpad(g["w_uq_p"]),
            "od_w_ukv": ukv_unpad(g["w_ukv_p"]), "od_w_out": g["w_out"]}


def ffn_grads(g):
    return {"ffn_norm": g["norm"][0], "ffn_w_up": g["w_up"], "ffn_conv_w": jnp.moveaxis(g["cw"], 0, 1).reshape(3, 2 * D_FF),
            "ffn_conv_b": g["cb"].reshape(2 * D_FF), "ffn_w_down": g["w_down"]}


WEIGHTS = ["meta_tokens", "ev_norm", "ev_w_in", "ev_conv_a", "ev_conv_b", "ev_conv_b_bias", "ev_gate_r_w", "ev_gate_r_b",
           "ev_gate_i_w", "ev_gate_i_b", "ev_lru_lambda", "ev_w_out", "od_norm", "od_w_in", "od_q_norm", "od_kv_norm",
           "od_w_uq", "od_w_ukv", "od_w_out", "ffn_norm", "ffn_w_up", "ffn_conv_w", "ffn_conv_b", "ffn_w_down",
           "final_norm"]
SHARD_AXIS = {"meta_tokens": 1, "ev_w_in": 2, "ev_conv_a": 2, "ev_conv_b": 2, "ev_w_out": 1, "od_norm": 1, "od_w_in": 1,
              "od_q_norm": 1, "od_kv_norm": 1, "od_w_uq": 2, "od_w_ukv": 2, "od_w_out": 1, "ffn_w_up": 2,
              "ffn_conv_w": 2, "ffn_w_down": 1}
MATMUL_WEIGHTS = ["ev_w_in", "ev_w_out", "od_w_in", "od_w_uq", "od_w_ukv", "od_w_out", "ffn_w_up", "ffn_w_down"]


LAYER_ORDER = [("ev", 0), ("ffn", 0), ("od", 0), ("ffn", 1), ("ev", 1), ("ffn", 2), ("od", 1), ("ffn", 3)]
LAYER_MATMUL = {"ev": ["ev_w_in", "ev_w_out"], "od": ["od_w_in", "od_w_uq", "od_w_ukv", "od_w_out"],
                "ffn": ["ffn_w_up", "ffn_w_down"]}
LAYER_SHARDED = {"ev": ["ev_w_in", "ev_conv_a", "ev_conv_b", "ev_w_out"],
                 "od": ["od_norm", "od_w_in", "od_q_norm", "od_kv_norm", "od_w_uq", "od_w_ukv", "od_w_out"],
                 "ffn": ["ffn_w_up", "ffn_conv_w", "ffn_w_down"]}
STACKED_SHARDS = "ffn_w_up"


def gather_all_layers(w, names, name):
    buf, offs = pack_rows([w[n] for n in names], 0, 32)
    halves = buf.reshape(2, buf.shape[0] // 2, LANE)
    got = own_block(run_stage(stage_gather_chips(halves), name)[0], halves).reshape(4, buf.shape[0], LANE)
    return {n: unshard(unpack_rows(got, off, w[n].shape), SHARD_AXIS[n]) for n, off in zip(names, offs)}


class GatherComm:
    def __init__(self, w, kind, idx):
        self.names = LAYER_MATMUL[kind]
        arrs = [w[n][idx].astype(MXU_DT) for n in self.names]
        self.shapes = [a.shape for a in arrs]
        buf, self.offs = pack_rows(arrs, 0, 32)
        self.rows = buf.shape[0]
        self.halves = buf.reshape(2, self.rows // 2, LANE)
        self.stage = stage_gather_chips(self.halves)
        self.step, self.got = 0, None

    def advance(self, carried):
        self.step += 1
        if self.step == 1:
            return self.stage
        if self.step == 2:
            self.got = carried[0]
        return None

    def run_alone(self, name):
        self.advance(run_stage(self.advance(None), name))

    def weights(self):
        got = own_block(self.got, self.halves).reshape(4, self.rows, LANE)
        out = {}
        for n, off, shape in zip(self.names, self.offs, self.shapes):
            stacked = unpack_rows(got, off, shape)
            out[n] = stacked if n == STACKED_SHARDS else unshard(stacked, SHARD_AXIS[n] - 1)
        return out


class ReduceComm:
    def __init__(self, grads, axes, c_idx, tag, tail=None):
        self.names = list(grads)
        shards = [grads[n] if n == STACKED_SHARDS else to_shards(grads[n], axes[n]) for n in self.names]
        self.shapes = [s.shape[1:] for s in shards]
        gs, self.offs = pack_rows(shards, 1, 16)
        self.rs = gs.shape[1] // 2
        parts = [gs.reshape(4, 2, self.rs, LANE)]
        self.rr = 0
        if tail is not None:
            self.rr = tail.shape[0] // 8
            parts.append(tail.reshape(4, 2, self.rr, LANE))
        rows = _round_up(self.rs + self.rr, SUM_ROWS)
        if rows > self.rs + self.rr:
            parts.append(jnp.zeros((4, 2, rows - self.rs - self.rr, LANE), F32))
        self.g4 = jnp.concatenate(parts, axis=2)
        self.c_idx, self.tag, self.step = c_idx, tag, 0
        self.part = self.mine = self.theirs = None

    def advance(self, carried):
        self.step += 1
        if self.step == 1:
            return stage_pair_exchange(self.g4)
        if self.step == 2:
            self.part = pair_sum(self.g4, carried[0], self.c_idx, "grad_pair_sum_" + self.tag)
            return stage_chip_scatter(self.part)
        if self.step == 3:
            me_idx = chip_index().astype(jnp.int32).reshape(1)
            self.mine = chip_sum(carried[0], self.part, me_idx, "grad_chip_sum_" + self.tag)
            return stage_pair_gather(self.mine)
        if self.step == 4:
            self.theirs = carried[0]
        return None

    def run_alone(self, name):
        stage = self.advance(None)
        while stage is not None:
            stage = self.advance(run_stage(stage, name + "_%d" % self.step))

    def results(self):
        south = self.c_idx[0] == 0
        both = jnp.stack([jnp.where(south, self.mine, self.theirs), jnp.where(south, self.theirs, self.mine)], axis=0)
        flat = both[:, :self.rs].reshape(2 * self.rs, LANE)
        out = {n: unpack_rows(flat, off, shape) for n, off, shape in zip(self.names, self.offs, self.shapes)}
        return out, both[:, self.rs:self.rs + self.rr]


def kernel(x, meta_tokens, ev_norm, ev_w_in, ev_conv_a, ev_conv_b, ev_conv_b_bias, ev_gate_r_w, ev_gate_r_b, ev_gate_i_w, ev_gate_i_b, ev_lru_lambda, ev_w_out, od_norm, od_w_in, od_q_norm, od_kv_norm, od_w_uq, od_w_ukv, od_w_out, ffn_norm, ffn_w_up, ffn_conv_w, ffn_conv_b, ffn_w_down, final_norm, loss_target, m_meta_tokens, m_ev_norm, m_ev_w_in, m_ev_conv_a, m_ev_conv_b, m_ev_conv_b_bias, m_ev_gate_r_w, m_ev_gate_r_b, m_ev_gate_i_w, m_ev_gate_i_b, m_ev_lru_lambda, m_ev_w_out, m_od_norm, m_od_w_in, m_od_q_norm, m_od_kv_norm, m_od_w_uq, m_od_w_ukv, m_od_w_out, m_ffn_norm, m_ffn_w_up, m_ffn_conv_w, m_ffn_conv_b, m_ffn_w_down, m_final_norm, v_meta_tokens, v_ev_norm, v_ev_w_in, v_ev_conv_a, v_ev_conv_b, v_ev_conv_b_bias, v_ev_gate_r_w, v_ev_gate_r_b, v_ev_gate_i_w, v_ev_gate_i_b, v_ev_lru_lambda, v_ev_w_out, v_od_norm, v_od_w_in, v_od_q_norm, v_od_kv_norm, v_od_w_uq, v_od_w_ukv, v_od_w_out, v_ffn_norm, v_ffn_w_up, v_ffn_conv_w, v_ffn_conv_b, v_ffn_w_down, v_final_norm):
    given = dict(locals())
    w = {n: given[n] for n in WEIGHTS}
    nb, seq, _ = x.shape
    dm = Dims(nb, seq)
    n_layers = len(LAYER_ORDER)

    wf = {n: w[n] for n in WEIGHTS if n not in SHARD_AXIS}
    wf.update(gather_all_layers(w, [n for n in SHARD_AXIS if n not in MATMUL_WEIGHTS], "gather_small_weights"))
    gathers = [GatherComm(w, kind, idx) for kind, idx in LAYER_ORDER]
    gathers[0].run_alone("gather_first_layer")

    tail = dm.tp - dm.t_real
    meta = jnp.broadcast_to(wf["meta_tokens"][None], (nb, N_META, D_MODEL))
    h = jnp.concatenate([meta, x, jnp.zeros((nb, tail, D_MODEL), F32)], axis=1).reshape(dm.rows, D_MODEL)
    tgt = jnp.pad(loss_target, ((0, 0), (N_META, tail), (0, 0))).reshape(dm.rows, D_MODEL)
    tabq, tabk = rope_tables(dm.tp)

    params, saved = [], []
    for i, (kind, idx) in enumerate(LAYER_ORDER):
        wl = dict(wf)
        wl.update(gathers[i].weights())
        comm = gathers[i + 1] if i + 1 < n_layers else NoComm()
        if kind == "ev":
            p = even_params(wl, idx)
            h, sv = even_fwd(h, p, dm, comm)
        elif kind == "od":
            p = odd_params(wl, idx)
            h, sv = odd_fwd(h, p, tabq, tabk, dm, comm)
        else:
            p = ffn_params(wl, idx)
            h, sv = ffn_fwd(h, p, dm, comm)
        params.append(p)
        saved.append(sv)

    dh, loss, dfinal = loss_head(h, tgt, _row(wf["final_norm"]), dm.tp, dm.t_real, dm.n, "loss_head")
    loss = lax.psum(loss[0, 0], ("x", "y", "c"))

    c_idx = lax.axis_index("c").astype(jnp.int32).reshape(1)
    layer_grads = {n: {} for n in WEIGHTS}
    pending, reduces = NoComm(), []
    for i in reversed(range(n_layers)):
        kind, idx = LAYER_ORDER[i]
        if kind == "ev":
            dh, g = even_bwd(dh, saved[i], params[i], dm, pending)
            g = even_grads(g)
        elif kind == "od":
            dh, g = odd_bwd(dh, saved[i], params[i], tabq, tabk, dm, pending)
            g = odd_grads(g)
        else:
            dh, g = ffn_bwd(dh, saved[i], params[i], dm, pending)
            g = ffn_grads(g)
        for n in g:
            if n not in SHARD_AXIS:
                layer_grads[n][idx] = g[n]
        if i > 0:
            pending = ReduceComm({n: g[n] for n in LAYER_SHARDED[kind]}, {n: SHARD_AXIS[n] - 1 for n in SHARD_AXIS},
                                 c_idx, "%s%d" % (kind, idx))
            reduces.append((pending, idx))
    dh3 = dh.reshape(nb, dm.tp, D_MODEL)
    grad_x = dh3[:, N_META:dm.t_real]

    repl = [n for n in WEIGHTS if n not in SHARD_AXIS]
    layer_grads["final_norm"] = {0: dfinal[0]}
    repl_full = {n: (layer_grads[n][0] if n == "final_norm" else
                     jnp.stack([layer_grads[n][j] for j in range(w[n].shape[0])], axis=0)) for n in repl}
    tail_buf, tail_offs = pack_rows([repl_full[n] for n in repl], 0, 64)
    first = {n: g[n] for n in LAYER_SHARDED["ev"]}
    first["meta_tokens"] = jnp.sum(dh3[:, :N_META], axis=0)
    axes = {n: SHARD_AXIS[n] - 1 for n in SHARD_AXIS}
    axes["meta_tokens"] = SHARD_AXIS["meta_tokens"]
    last = ReduceComm(first, axes, c_idx, "first_layer", tail=tail_buf)
    last.run_alone("grad_first_layer")
    reduces.append((last, 0))

    red = {}
    for comm, idx in reduces:
        got, tail_piece = comm.results()
        for n, v_ in got.items():
            if n == "meta_tokens":
                red[n] = v_
            else:
                layer_grads[n][idx] = v_
    tails = own_block(run_stage(stage_gather_chips(tail_piece), "grad_gather_replicated")[0], tail_piece)
    tails = tails.reshape(tail_buf.shape[0], LANE)
    for n, off in zip(repl, tail_offs):
        red[n] = unpack_rows(tails, off, w[n].shape)
    for n in SHARD_AXIS:
        if n != "meta_tokens":
            red[n] = jnp.stack([layer_grads[n][j] for j in range(w[n].shape[0])], axis=0)

    outs = [adamw(red[n], w[n], given["m_" + n], given["v_" + n], "adamw_" + n)[0] for n in WEIGHTS]
    return (loss, grad_x, *[o[0] for o in outs], *[o[1] for o in outs], *[o[2] for o in outs], *[o[3] for o in outs])
```

```python
import math

import jax
import jax.numpy as jnp
from jax import lax
from jax.experimental import pallas as pl
from jax.experimental.pallas import tpu as pltpu

F32 = jnp.float32
MXU_DT = jnp.bfloat16
S = jax.ShapeDtypeStruct
MESH = pl.DeviceIdType.MESH

EPS = 1e-6
D_MODEL = 1024
N_META = 16
DEPTH = 4
CONV_W = 512
LRU_W = 512
LRU_HEADS = 8
LRU_C = 8.0
EVEN_IN = 2560
MLA_HEADS = 16
QK_NOPE = 64
QK_ROPE = 32
QK_HEAD = 96
V_HEAD = 64
Q_LORA = 384
KV_LORA = 256
ROPE_BASE = 10000.0
D_FF = 2816
ODD_PAD = 896
ODD_CKV_COL = 2
ODD_KR_COL = 6
HP = 128
ATT_BLK = 384
Q_PRESCALE = QK_HEAD ** -0.5 * math.log2(math.e)
FFN_CT = 256
STRIP_ROWS = 16
LANE = 128
SUBLANE = 8
VMEM_LIMIT_MB = 52

ADAM_LR = 0.001
ADAM_B1 = 0.9
ADAM_B2 = 0.999
ADAM_EPS = 1e-08
ADAM_WD = 0.01
ADAM_STEP = 10

NT_DIMS = (((1,), (1,)), ((), ()))
TN_DIMS = (((0,), (0,)), ((), ()))


def _cp(sem):
    return pltpu.CompilerParams(dimension_semantics=sem, vmem_limit_bytes=VMEM_LIMIT_MB << 20)


def _div_tile(n, cap, mult):
    if n <= cap:
        return n
    best = None
    for t in range(mult, cap + 1, mult):
        if n % t == 0:
            best = t
    assert best is not None, (n, cap, mult)
    return best


def _round_up(n, m):
    return -(-n // m) * m


def mat_cols(arr):
    return arr.shape[1] if arr.ndim == 2 else arr.shape[0] * arr.shape[2]


def mat_width(arr):
    return arr.shape[-1]


def mat_spec(arr, tm, tw, rc):
    if arr.ndim == 2:
        return pl.BlockSpec((tm, tw), lambda *g: rc(*g))
    per = arr.shape[2] // tw
    assert arr.shape[2] % tw == 0

    def imap(*g):
        r, c = rc(*g)
        return (c // per, r, c % per)

    return pl.BlockSpec((None, tm, tw), imap)


HBM_SPEC = pl.BlockSpec(memory_space=pltpu.HBM)


class Stage:
    def __init__(self, inputs, out_shapes, sems, start, finish):
        self.inputs, self.out_shapes, self.sems, self.start, self.finish = inputs, out_shapes, sems, start, finish


def run_stage(stage, name):
    n_in, n_out = len(stage.inputs), len(stage.out_shapes)

    def body(*refs):
        ins, outs, sems = refs[:n_in], refs[n_in:n_in + n_out], refs[n_in + n_out:]
        stage.start(ins, outs, sems)
        stage.finish(ins, outs, sems)

    return pl.pallas_call(body, out_shape=list(stage.out_shapes), in_specs=[HBM_SPEC] * n_in,
                          out_specs=[HBM_SPEC] * n_out, scratch_shapes=list(stage.sems), name=name)(*stage.inputs)


def _call(body, ops, carry, *, grid, in_specs, out_specs, out_shape, scratch_shapes, sem, name):
    if carry is None:
        outs = pl.pallas_call(body, grid=grid, in_specs=in_specs, out_specs=out_specs, out_shape=out_shape,
                              scratch_shapes=scratch_shapes, compiler_params=_cp(sem), name=name)(*ops)
        return outs, None
    multi = isinstance(out_shape, (list, tuple))
    shapes = list(out_shape) if multi else [out_shape]
    ospecs = list(out_specs) if multi else [out_specs]
    n_in, n_out, n_sc = len(ops), len(shapes), len(scratch_shapes)
    c_in, c_out = len(carry.inputs), len(carry.out_shapes)

    def wrapped(*refs):
        ins, cin = refs[:n_in], refs[n_in:n_in + c_in]
        o0 = n_in + c_in
        outs, cout = refs[o0:o0 + n_out], refs[o0 + n_out:o0 + n_out + c_out]
        s0 = o0 + n_out + c_out
        scs, csems = refs[s0:s0 + n_sc], refs[s0 + n_sc:]
        first = pl.program_id(0) == 0
        last = pl.program_id(0) == grid[0] - 1
        for d in range(1, len(grid)):
            first = first & (pl.program_id(d) == 0)
            last = last & (pl.program_id(d) == grid[d] - 1)

        @pl.when(first)
        def _():
            carry.start(cin, cout, csems)

        body(*ins, *outs, *scs)

        @pl.when(last)
        def _():
            carry.finish(cin, cout, csems)

    res = pl.pallas_call(
        wrapped, grid=grid, in_specs=list(in_specs) + [HBM_SPEC] * c_in, out_specs=ospecs + [HBM_SPEC] * c_out,
        out_shape=shapes + list(carry.out_shapes), scratch_shapes=list(scratch_shapes) + list(carry.sems),
        compiler_params=_cp(("arbitrary",) * len(grid)), name=name)(*ops, *carry.inputs)
    main = res[:n_out]
    return (list(main) if multi else main[0]), list(res[n_out:])


def norm_matmul(x, xcol, kdim, gain, w, tm, tn, out_dtype, name, epi=None, epi_ops=(), epi_specs=(), carry=None):
    rows, n = x.shape[0], mat_cols(w) if w.ndim == 3 else w.shape[1]
    n_epi = len(epi_ops)
    w_spec = (pl.BlockSpec((kdim, tn), lambda i, j: (0, j)) if w.ndim == 2 else
              pl.BlockSpec((None, kdim, tn), lambda i, j: (j // (w.shape[2] // tn), 0, j % (w.shape[2] // tn))))

    def body(x_ref, g_ref, w_ref, *rest):
        epi_refs = rest[:n_epi]
        out_ref, xn_ref, xn_sc = rest[n_epi:]

        @pl.when(pl.program_id(1) == 0)
        def _():
            xv = x_ref[...]
            y = xv * lax.rsqrt(jnp.mean(xv * xv, axis=-1, keepdims=True) + EPS)
            xn = (y * g_ref[...]).astype(MXU_DT)
            xn_sc[...] = xn
            xn_ref[...] = xn

        acc = jnp.dot(xn_sc[...], w_ref[...], preferred_element_type=F32)
        if epi is not None:
            acc = epi(acc, *[r[...] for r in epi_refs])
        out_ref[...] = acc.astype(out_dtype)

    return _call(
        body, (x, gain, w, *epi_ops), carry, grid=(rows // tm, n // tn),
        in_specs=[pl.BlockSpec((tm, kdim), lambda i, j: (i, xcol)), pl.BlockSpec((1, kdim), lambda i, j: (0, 0)),
                  w_spec, *epi_specs],
        out_specs=[pl.BlockSpec((tm, tn), lambda i, j: (i, j)), pl.BlockSpec((tm, kdim), lambda i, j: (i, 0))],
        out_shape=[S((rows, n), out_dtype), S((rows, kdim), MXU_DT)],
        scratch_shapes=[pltpu.VMEM((tm, kdim), MXU_DT)], sem=("parallel", "arbitrary"), name=name)


def matmul_res(a, w, res, tm, tn, name):
    grp, rows, k = a.shape
    n = w.shape[2]

    def body(a_ref, w_ref, r_ref, o_ref):
        acc = r_ref[...]
        for g in range(grp):
            acc = acc + jnp.dot(a_ref[g], w_ref[g], preferred_element_type=F32)
        o_ref[...] = acc

    return pl.pallas_call(
        body, grid=(rows // tm, n // tn),
        in_specs=[pl.BlockSpec((grp, tm, k), lambda i, j: (0, i, 0)), pl.BlockSpec((grp, k, tn), lambda i, j: (0, 0, j)),
                  pl.BlockSpec((tm, tn), lambda i, j: (i, j))],
        out_specs=pl.BlockSpec((tm, tn), lambda i, j: (i, j)),
        out_shape=S((rows, n), F32), compiler_params=_cp(("parallel", "parallel")), name=name)(a, w, res)


def matmul_nt(a, w, tm, tn, out_dtype, name):
    rows, k = a.shape
    n = w.shape[0]

    def body(a_ref, w_ref, o_ref):
        o_ref[...] = lax.dot_general(a_ref[...].astype(MXU_DT), w_ref[...], NT_DIMS,
                                     preferred_element_type=F32).astype(out_dtype)

    return pl.pallas_call(
        body, grid=(rows // tm, n // tn),
        in_specs=[pl.BlockSpec((tm, k), lambda i, j: (i, 0)), pl.BlockSpec((tn, k), lambda i, j: (j, 0))],
        out_specs=pl.BlockSpec((tm, tn), lambda i, j: (i, j)),
        out_shape=S((rows, n), out_dtype), compiler_params=_cp(("parallel", "parallel")), name=name)(a, w)


def matmul_nt_normbwd(du, w, x, xcol, gain, res, tm, tk, out_dtype, name):
    rows, kc = du.shape[-2], mat_cols(du)
    dn = w.shape[-2]
    nk = kc // tk
    has_res = res is not None
    w_spec = (pl.BlockSpec((dn, tk), lambda i, k: (0, k)) if w.ndim == 2 else
              pl.BlockSpec((None, dn, tk), lambda i, k: (k // (w.shape[2] // tk), 0, k % (w.shape[2] // tk))))

    def body(du_ref, w_ref, x_ref, g_ref, *rest):
        if has_res:
            res_ref, dx_ref, dg_ref, acc = rest
        else:
            dx_ref, dg_ref, acc = rest
        i, k = pl.program_id(0), pl.program_id(1)

        @pl.when(k == 0)
        def _():
            acc[...] = jnp.zeros_like(acc)

        @pl.when((i == 0) & (k == 0))
        def _():
            dg_ref[...] = jnp.zeros_like(dg_ref)

        acc[...] += lax.dot_general(du_ref[...], w_ref[...], NT_DIMS, preferred_element_type=F32)

        @pl.when(k == nk - 1)
        def _():
            dhn = acc[...]
            xv = x_ref[...]
            rstd = lax.rsqrt(jnp.mean(xv * xv, axis=-1, keepdims=True) + EPS)
            xhat = xv * rstd
            dg_ref[...] += jnp.sum(dhn * xhat, axis=0, keepdims=True)
            dxh = dhn * g_ref[...]
            dx = rstd * (dxh - xhat * jnp.mean(dxh * xhat, axis=-1, keepdims=True))
            if has_res:
                dx = dx + res_ref[...]
            dx_ref[...] = dx.astype(out_dtype)

    in_specs = [mat_spec(du, tm, tk, lambda i, k: (i, k)), w_spec,
                pl.BlockSpec((tm, dn), lambda i, k: (i, xcol)), pl.BlockSpec((1, dn), lambda i, k: (0, 0))]
    ops = [du, w, x, gain]
    if has_res:
        in_specs.append(pl.BlockSpec((tm, dn), lambda i, k: (i, 0)))
        ops.append(res)
    return pl.pallas_call(
        body, grid=(rows // tm, nk), in_specs=in_specs,
        out_specs=[pl.BlockSpec((tm, dn), lambda i, k: (i, 0)), pl.BlockSpec((1, dn), lambda i, k: (0, 0))],
        out_shape=[S((rows, dn), out_dtype), S((1, dn), F32)],
        scratch_shapes=[pltpu.VMEM((tm, dn), F32)],
        compiler_params=_cp(("arbitrary", "arbitrary")), name=name)(*ops)


def matmul_tn(a, b, tr, name, carry=None, col_shards=1):
    rows, ka, nb = a.shape[-2], mat_cols(a), mat_cols(b)
    ta = _div_tile(mat_width(a), 1536, LANE)
    tb = _div_tile(mat_width(b), 1536 if ta <= 1024 else 1024, LANE)
    nr = rows // tr
    if col_shards == 1:
        out_spec, out_shape = pl.BlockSpec((ta, tb), lambda i, j, r: (i, j)), S((ka, nb), F32)
    else:
        per = nb // col_shards // tb
        assert per * tb * col_shards == nb
        out_spec = pl.BlockSpec((None, ta, tb), lambda i, j, r: (j // per, i, j % per))
        out_shape = S((col_shards, ka, nb // col_shards), F32)

    def body(a_ref, b_ref, o_ref, acc):
        r = pl.program_id(2)

        @pl.when(r == 0)
        def _():
            acc[...] = jnp.zeros_like(acc)

        acc[...] += lax.dot_general(a_ref[...].astype(MXU_DT), b_ref[...].astype(MXU_DT), TN_DIMS,
                                    preferred_element_type=F32)

        @pl.when(r == nr - 1)
        def _():
            o_ref[...] = acc[...]

    return _call(
        body, (a, b), carry, grid=(ka // ta, nb // tb, nr),
        in_specs=[mat_spec(a, tr, ta, lambda i, j, r: (r, i)), mat_spec(b, tr, tb, lambda i, j, r: (r, j))],
        out_specs=out_spec, out_shape=out_shape, scratch_shapes=[pltpu.VMEM((ta, tb), F32)],
        sem=("parallel", "parallel", "arbitrary"), name=name)


def _sigmoid(x):
    return 1.0 / (1.0 + jnp.exp(-x))


def _log1p(e):
    return jnp.where(e < 1e-3, e * (1.0 - e * (0.5 - e * (1.0 / 3.0 - 0.25 * e))), jnp.log(1.0 + e))


def _softplus(x):
    return jnp.maximum(x, 0.0) + _log1p(jnp.exp(-jnp.abs(x)))


def _expm1(x):
    series = x * (1.0 + x * (0.5 + x * (1.0 / 6.0 + x * (1.0 / 24.0 + x * (1.0 / 120.0)))))
    return jnp.where(jnp.abs(x) < 0.1, series, jnp.exp(x) - 1.0)


_GELU_K = math.sqrt(2.0 / math.pi)
_GELU_C = 0.044715


def _gelu_and_grad(x):
    th = jnp.tanh(_GELU_K * (x + _GELU_C * x * x * x))
    g = 0.5 * x * (1.0 + th)
    dg = 0.5 * (1.0 + th) + 0.5 * x * (1.0 - th * th) * _GELU_K * (1.0 + 3.0 * _GELU_C * x * x)
    return g, dg


def _row_iota(shape):
    return lax.broadcasted_iota(jnp.int32, shape, 0)


def _scan_chunk_fwd(a_sc, u_sc, out_ref, hcar, n, width):
    rowi = _row_iota((SUBLANE, width))

    def step(c, hprev):
        r0 = pl.multiple_of(c * SUBLANE, SUBLANE)
        a = a_sc[pl.ds(r0, SUBLANE), :]
        u = u_sc[pl.ds(r0, SUBLANE), :]
        for d in (1, 2, 4):
            a_s = jnp.where(rowi >= d, pltpu.roll(a, d, axis=0), 1.0)
            u_s = jnp.where(rowi >= d, pltpu.roll(u, d, axis=0), 0.0)
            u = u + a * u_s
            a = a * a_s
        h = u + a * hprev
        out_ref[pl.ds(r0, SUBLANE), :] = h
        return jnp.broadcast_to(h[SUBLANE - 1:SUBLANE, :], (SUBLANE, width))

    hcar[...] = lax.fori_loop(0, n // SUBLANE, step, hcar[...], unroll=4)


def _scan_chunk_bwd(b_sc, d_sc, out_ref, gcar, n, width):
    rowi = _row_iota((SUBLANE, width))
    nc = n // SUBLANE

    def step(c, gnext):
        r0 = pl.multiple_of((nc - 1 - c) * SUBLANE, SUBLANE)
        b = b_sc[pl.ds(r0, SUBLANE), :]
        d = d_sc[pl.ds(r0, SUBLANE), :]
        for s in (1, 2, 4):
            keep = rowi < SUBLANE - s
            b_s = jnp.where(keep, pltpu.roll(b, SUBLANE - s, axis=0), 1.0)
            d_s = jnp.where(keep, pltpu.roll(d, SUBLANE - s, axis=0), 0.0)
            d = d + b * d_s
            b = b * b_s
        g = d + b * gnext
        out_ref[pl.ds(r0, SUBLANE), :] = g
        return jnp.broadcast_to(g[0:1, :], (SUBLANE, width))

    gcar[...] = lax.fori_loop(0, nc, step, gcar[...], unroll=4)


def even_mid_fwd(u, conv_a, conv_b, conv_b_bias, rw, rb, iw, ib, lam, nb, tp, n, name, carry=None):
    rows = u.shape[0]
    w = LANE
    nj = CONV_W // w
    nt = tp // n
    h8 = SUBLANE

    def body(gb_r, gc_r, xa_r, xb_r, gate_r, ca_w, cb_w, cb_b, rw_r, rb_r, iw_r, ib_r, lam_r,
             y_o, ca_o, xc_o, a_o, hs_o, pext, xext, hcar, a_sc, u_sc):
        t = pl.program_id(2)

        @pl.when(t == 0)
        def _():
            pext[0:h8, :] = jnp.zeros((h8, w), F32)
            xext[0:h8, :] = jnp.zeros((h8, w), F32)
            hcar[...] = jnp.zeros_like(hcar)

        p = gc_r[...] * xa_r[...]
        pext[h8:h8 + n, :] = p
        wa = ca_w[...]
        ca = wa[2:3, :] * p + wa[1:2, :] * pext[h8 - 1:h8 - 1 + n, :] + wa[0:1, :] * pext[h8 - 2:h8 - 2 + n, :]
        ca_o[...] = ca
        y_o[0] = (gb_r[...] * ca).astype(MXU_DT)
        pext[0:h8, :] = pext[n:n + h8, :]

        xb = xb_r[...]
        xext[h8:h8 + n, :] = xb
        wb = cb_w[...]
        xc = (wb[3:4, :] * xb + wb[2:3, :] * xext[h8 - 1:h8 - 1 + n, :] + wb[1:2, :] * xext[h8 - 2:h8 - 2 + n, :]
              + wb[0:1, :] * xext[h8 - 3:h8 - 3 + n, :]) + cb_b[...]
        xc_o[...] = xc
        xext[0:h8, :] = xext[n:n + h8, :]

        xcm = xc.astype(MXU_DT)
        r = _sigmoid(jnp.dot(xcm, rw_r[...], preferred_element_type=F32) + rb_r[...])
        ig = _sigmoid(jnp.dot(xcm, iw_r[...], preferred_element_type=F32) + ib_r[...])
        log_a = (-LRU_C) * r * _softplus(-lam_r[...])
        a = jnp.exp(log_a)
        mult = jnp.sqrt(-_expm1(2.0 * log_a))
        a_sc[...] = a
        a_o[...] = a
        u_sc[...] = mult * (ig * xc)
        _scan_chunk_fwd(a_sc, u_sc, hs_o, hcar, n, w)
        gel, _ = _gelu_and_grad(gate_r[...])
        y_o[1] = (gel * hs_o[...]).astype(MXU_DT)

    def ublk(off):
        return pl.BlockSpec((n, w), lambda j, b, t: (b * nt + t, off + j))

    def pblk(r_):
        return pl.BlockSpec((r_, w), lambda j, b, t: (0, j))

    act = pl.BlockSpec((n, w), lambda j, b, t: (b * nt + t, j))
    mat = pl.BlockSpec((w, w), lambda j, b, t: (j, j))
    return _call(
        body, (u, u, u, u, u, conv_a, conv_b, conv_b_bias, rw, rb, iw, ib, lam), carry, grid=(nj, nb, nt),
        in_specs=[ublk(0), ublk(nj), ublk(2 * nj), ublk(3 * nj), ublk(4 * nj), pblk(3), pblk(4), pblk(1),
                  mat, pblk(1), mat, pblk(1), pblk(1)],
        out_specs=[pl.BlockSpec((2, n, w), lambda j, b, t: (0, b * nt + t, j)), act, act, act, act],
        out_shape=[S((2, rows, CONV_W), MXU_DT), S((rows, CONV_W), F32), S((rows, LRU_W), F32), S((rows, LRU_W), F32),
                   S((rows, LRU_W), F32)],
        scratch_shapes=[pltpu.VMEM((n + h8, w), F32), pltpu.VMEM((n + h8, w), F32), pltpu.VMEM((h8, w), F32),
                        pltpu.VMEM((n, w), F32), pltpu.VMEM((n, w), F32)],
        sem=("parallel", "parallel", "arbitrary"), name=name)


def even_mid_bwd(u, dycat, ca, xc, a_sv, hs, conv_a, conv_b, rw, rb, iw, ib, lam, nb, tp, n, name, carry=None):
    rows = u.shape[0]
    w = LANE
    nj = CONV_W // w
    nt = tp // n
    h8 = SUBLANE

    def body(gb_r, gc_r, xa_r, xb_r, gate_r, dya_r, dyb_r, ca_r, xc_r, a_r, hs_r, hsp_r,
             ca_w, cb_w, rw_r, rb_r, iw_r, ib_r, lam_r,
             du_o, dca_w, dcb_w, dcb_b, drw, drb, diw, dib, dlam,
             aext, hext, dext, eext, gcar, b_sc, d_sc, g_sc):
        b, t = pl.program_id(1), pl.program_id(2)

        @pl.when((b == 0) & (t == 0))
        def _():
            for ref in (dca_w, dcb_w, dcb_b, drw, drb, diw, dib, dlam):
                ref[...] = jnp.zeros_like(ref)

        @pl.when(t == 0)
        def _():
            aext[n:n + h8, :] = jnp.zeros((h8, w), F32)
            dext[n:n + h8, :] = jnp.zeros((h8, w), F32)
            eext[n:n + h8, :] = jnp.zeros((h8, w), F32)
            gcar[...] = jnp.zeros_like(gcar)

        xc_v = xc_r[...]
        xcm = xc_v.astype(MXU_DT)
        r = _sigmoid(jnp.dot(xcm, rw_r[...], preferred_element_type=F32) + rb_r[...])
        ig = _sigmoid(jnp.dot(xcm, iw_r[...], preferred_element_type=F32) + ib_r[...])
        lam_v = lam_r[...]
        sp = _softplus(-lam_v)
        log_a = (-LRU_C) * r * sp
        a = a_r[...]
        mult = jnp.sqrt(-_expm1(2.0 * log_a))
        hs_v = hs_r[...]
        gel, dgel = _gelu_and_grad(gate_r[...])
        dyb = dyb_r[...]
        du_o[4] = (dyb * hs_v * dgel).astype(MXU_DT)

        aext[0:n, :] = a
        b_sc[...] = aext[1:1 + n, :]
        d_sc[...] = dyb * gel
        _scan_chunk_bwd(b_sc, d_sc, g_sc, gcar, n, w)
        aext[n:n + h8, :] = aext[0:h8, :]
        g = g_sc[...]

        hext[0:h8, :] = jnp.where(t == nt - 1, 0.0, hsp_r[...])
        hext[h8:h8 + n, :] = hs_v
        da = g * hext[h8 - 1:h8 - 1 + n, :]
        dmult = g * (ig * xc_v)
        di = g * mult * xc_v
        dxc = g * mult * ig
        dlog_a = da * a - dmult * (a * a) / mult
        dr = dlog_a * ((-LRU_C) * sp)
        dsp = jnp.sum(dlog_a * ((-LRU_C) * r), axis=0, keepdims=True)
        dlam[...] += dsp * (-_sigmoid(-lam_v))
        dzr = dr * r * (1.0 - r)
        dzi = di * ig * (1.0 - ig)
        dzr_m = dzr.astype(MXU_DT)
        dzi_m = dzi.astype(MXU_DT)
        dxc = (dxc + lax.dot_general(dzr_m, rw_r[...], NT_DIMS, preferred_element_type=F32)
               + lax.dot_general(dzi_m, iw_r[...], NT_DIMS, preferred_element_type=F32))
        drw[...] += lax.dot_general(xcm, dzr_m, TN_DIMS, preferred_element_type=F32)
        diw[...] += lax.dot_general(xcm, dzi_m, TN_DIMS, preferred_element_type=F32)
        drb[...] += jnp.sum(dzr, axis=0, keepdims=True)
        dib[...] += jnp.sum(dzi, axis=0, keepdims=True)
        dcb_b[...] += jnp.sum(dxc, axis=0, keepdims=True)

        xb = xb_r[...]
        dext[0:n, :] = dxc
        wb = cb_w[...]
        d1, d2, d3 = dext[1:1 + n, :], dext[2:2 + n, :], dext[3:3 + n, :]
        du_o[3] = (wb[3:4, :] * dxc + wb[2:3, :] * d1 + wb[1:2, :] * d2 + wb[0:1, :] * d3).astype(MXU_DT)
        dcb_w[3:4, :] += jnp.sum(xb * dxc, axis=0, keepdims=True)
        dcb_w[2:3, :] += jnp.sum(xb * d1, axis=0, keepdims=True)
        dcb_w[1:2, :] += jnp.sum(xb * d2, axis=0, keepdims=True)
        dcb_w[0:1, :] += jnp.sum(xb * d3, axis=0, keepdims=True)
        dext[n:n + h8, :] = dext[0:h8, :]

        gb, gc, xa = gb_r[...], gc_r[...], xa_r[...]
        dya = dya_r[...]
        du_o[0] = (dya * ca_r[...]).astype(MXU_DT)
        dca = dya * gb
        eext[0:n, :] = dca
        wa = ca_w[...]
        e1, e2 = eext[1:1 + n, :], eext[2:2 + n, :]
        dp = wa[2:3, :] * dca + wa[1:2, :] * e1 + wa[0:1, :] * e2
        p = gc * xa
        dca_w[2:3, :] += jnp.sum(p * dca, axis=0, keepdims=True)
        dca_w[1:2, :] += jnp.sum(p * e1, axis=0, keepdims=True)
        dca_w[0:1, :] += jnp.sum(p * e2, axis=0, keepdims=True)
        eext[n:n + h8, :] = eext[0:h8, :]
        du_o[1] = (dp * xa).astype(MXU_DT)
        du_o[2] = (dp * gc).astype(MXU_DT)

    def rt(b, t):
        return b * nt + (nt - 1 - t)

    def ublk(off):
        return pl.BlockSpec((n, w), lambda j, b, t: (rt(b, t), off + j))

    def pblk(r_):
        return pl.BlockSpec((r_, w), lambda j, b, t: (0, j))

    act = pl.BlockSpec((n, w), lambda j, b, t: (rt(b, t), j))
    n8 = n // h8
    hsp = pl.BlockSpec((h8, w), lambda j, b, t: (jnp.maximum(rt(b, t) * n8 - 1, 0), j))
    mat = pl.BlockSpec((w, w), lambda j, b, t: (j, j))
    return _call(
        body, (u, u, u, u, u, dycat, dycat, ca, xc, a_sv, hs, hs, conv_a, conv_b, rw, rb, iw, ib, lam), carry,
        grid=(nj, nb, nt),
        in_specs=[ublk(0), ublk(nj), ublk(2 * nj), ublk(3 * nj), ublk(4 * nj), ublk(0), ublk(nj), act, act, act, act,
                  hsp, pblk(3), pblk(4), mat, pblk(1), mat, pblk(1), pblk(1)],
        out_specs=[pl.BlockSpec((5, n, w), lambda j, b, t: (0, rt(b, t), j)), pblk(3), pblk(4), pblk(1),
                   mat, pblk(1), mat, pblk(1), pblk(1)],
        out_shape=[S((5, rows, CONV_W), MXU_DT), S((3, CONV_W), F32), S((4, LRU_W), F32), S((1, LRU_W), F32),
                   S((LRU_W, LRU_W), F32), S((1, LRU_W), F32), S((LRU_W, LRU_W), F32), S((1, LRU_W), F32),
                   S((1, LRU_W), F32)],
        scratch_shapes=[pltpu.VMEM((n + h8, w), F32)] * 4 + [pltpu.VMEM((h8, w), F32)] + [pltpu.VMEM((n, w), F32)] * 3,
        sem=("arbitrary", "arbitrary", "arbitrary"), name=name)


def ffn_mid_fwd(up, cw, cb, nb, tp, n, name):
    rows = up.shape[0]
    w = FFN_CT
    nj = D_FF // w
    nt = tp // n
    h8 = SUBLANE

    sr = STRIP_ROWS

    def body(xa_r, xg_r, w_r, b_r, u_o, y_o, halo):
        t = pl.program_id(2)

        @pl.when(t == 0)
        def _():
            halo[...] = jnp.zeros_like(halo)

        wv = (w_r[0], w_r[1])
        bv = (b_r[0], b_r[1])

        def strip(s, carry):
            r0 = pl.multiple_of(s * sr, sr)
            us, new = [], []
            for g, x_r in enumerate((xa_r, xg_r)):
                x = x_r[pl.ds(r0, sr), :].astype(F32)
                win = jnp.concatenate([carry[g], x], axis=0)
                x1 = pltpu.roll(win, 1, axis=0)[h8:, :]
                x2 = pltpu.roll(win, 2, axis=0)[h8:, :]
                u = (wv[g][2:3, :] * x + wv[g][1:2, :] * x1 + wv[g][0:1, :] * x2) + bv[g]
                u_o[g, pl.ds(r0, sr), :] = u.astype(MXU_DT)
                us.append(u)
                new.append(x[sr - h8:, :])
            y_o[pl.ds(r0, sr), :] = (us[0] * _sigmoid(us[0]) * us[1]).astype(MXU_DT)
            return tuple(new)

        ha, hg = lax.fori_loop(0, n // sr, strip, (halo[0], halo[1]))
        halo[0] = ha
        halo[1] = hg

    def ublk(off):
        return pl.BlockSpec((n, w), lambda j, b, t: (b * nt + t, off + j))

    return pl.pallas_call(
        body, grid=(nj, nb, nt),
        in_specs=[ublk(0), ublk(nj), pl.BlockSpec((2, 3, w), lambda j, b, t: (0, 0, j)),
                  pl.BlockSpec((2, 1, w), lambda j, b, t: (0, 0, j))],
        out_specs=[pl.BlockSpec((2, n, w), lambda j, b, t: (0, b * nt + t, j)), ublk(0)],
        out_shape=[S((2, rows, D_FF), MXU_DT), S((rows, D_FF), MXU_DT)],
        scratch_shapes=[pltpu.VMEM((2, h8, w), F32)],
        compiler_params=_cp(("parallel", "parallel", "arbitrary")), name=name,
    )(up, up, cw, cb)


def ffn_mid_bwd(dy, u, up, cw, nb, tp, n, name, carry=None):
    rows = up.shape[0]
    w = FFN_CT
    nj = D_FF // w
    nt = tp // n
    h8 = SUBLANE

    sr = STRIP_ROWS
    ns = n // sr

    def fold(v):
        acc = v[0:h8, :]
        for k in range(1, sr // h8):
            acc = acc + v[k * h8:(k + 1) * h8, :]
        return acc

    def body(dy_r, u_r, xa_r, xg_r, w_r, dx_o, dw, db, halo):
        b, t = pl.program_id(1), pl.program_id(2)

        @pl.when((b == 0) & (t == 0))
        def _():
            dw[...] = jnp.zeros_like(dw)
            db[...] = jnp.zeros_like(db)

        @pl.when(t == 0)
        def _():
            halo[...] = jnp.zeros_like(halo)

        wv = (w_r[0], w_r[1])

        def strip(s, carry):
            halos, sums = carry
            r0 = pl.multiple_of((ns - 1 - s) * sr, sr)
            dyv = dy_r[pl.ds(r0, sr), :].astype(F32)
            ua = u_r[0, pl.ds(r0, sr), :].astype(F32)
            ug = u_r[1, pl.ds(r0, sr), :].astype(F32)
            sg = _sigmoid(ua)
            dus = (dyv * ug * (sg * (1.0 + ua * (1.0 - sg))), dyv * (ua * sg))
            new_halos, new_sums = [], []
            for g, x_r in enumerate((xa_r, xg_r)):
                du = dus[g]
                win = jnp.concatenate([du, halos[g]], axis=0)
                d1 = pltpu.roll(win, sr + h8 - 1, axis=0)[0:sr, :]
                d2 = pltpu.roll(win, sr + h8 - 2, axis=0)[0:sr, :]
                dx_o[g, pl.ds(r0, sr), :] = (wv[g][2:3, :] * du + wv[g][1:2, :] * d1 + wv[g][0:1, :] * d2).astype(MXU_DT)
                x = x_r[pl.ds(r0, sr), :].astype(F32)
                s2, s1, s0, sb = sums[g]
                new_sums.append((s2 + fold(x * du), s1 + fold(x * d1), s0 + fold(x * d2), sb + fold(du)))
                new_halos.append(du[0:h8, :])
            return tuple(new_halos), tuple(new_sums)

        z = jnp.zeros((h8, w), F32)
        halos, sums = lax.fori_loop(0, ns, strip, ((halo[0], halo[1]), ((z, z, z, z), (z, z, z, z))))
        halo[0] = halos[0]
        halo[1] = halos[1]
        for g in range(2):
            s2, s1, s0, sb = sums[g]
            dw[g, 2:3, :] += jnp.sum(s2, axis=0, keepdims=True)
            dw[g, 1:2, :] += jnp.sum(s1, axis=0, keepdims=True)
            dw[g, 0:1, :] += jnp.sum(s0, axis=0, keepdims=True)
            db[g] += jnp.sum(sb, axis=0, keepdims=True)

    def rt(b, t):
        return b * nt + (nt - 1 - t)

    def ublk(off):
        return pl.BlockSpec((n, w), lambda j, b, t: (rt(b, t), off + j))

    pair = pl.BlockSpec((2, n, w), lambda j, b, t: (0, rt(b, t), j))
    return _call(
        body, (dy, u, up, up, cw), carry, grid=(nj, nb, nt),
        in_specs=[ublk(0), pair, ublk(0), ublk(nj), pl.BlockSpec((2, 3, w), lambda j, b, t: (0, 0, j))],
        out_specs=[pair, pl.BlockSpec((2, 3, w), lambda j, b, t: (0, 0, j)),
                   pl.BlockSpec((2, 1, w), lambda j, b, t: (0, 0, j))],
        out_shape=[S((2, rows, D_FF), MXU_DT), S((2, 3, D_FF), F32), S((2, 1, D_FF), F32)],
        scratch_shapes=[pltpu.VMEM((2, h8, w), F32)],
        sem=("arbitrary", "arbitrary", "arbitrary"), name=name)


def _lane_mod(shape):
    return lax.broadcasted_iota(jnp.int32, shape, 1) & (HP - 1)


def _q_rope_epi(acc, tab):
    reps = acc.shape[1] // HP
    a = acc * jnp.tile(tab, (1, reps))
    lane = _lane_mod(a.shape)
    shifted = pltpu.roll(a, a.shape[1] - QK_ROPE, axis=1)
    return jnp.where(lane < QK_NOPE, a, jnp.where(lane < QK_HEAD, a + shifted, 0.0)) * Q_PRESCALE


def _k_rope_block(krblk, tabk):
    a = krblk * tabk
    lane = _lane_mod(a.shape)
    b = a + pltpu.roll(a, HP - QK_ROPE, axis=1)
    return jnp.where((lane >= QK_NOPE) & (lane < QK_HEAD), b, 0.0)


def _k_rope_epi(acc, krblk, tabk):
    reps = acc.shape[1] // HP
    return acc + jnp.tile(_k_rope_block(krblk, tabk), (1, reps))


def attn_fwd(q, k, v, nb, tp, name, carry=None):
    rows = q.shape[0]
    blk = ATT_BLK
    nq = tp // blk
    npair = MLA_HEADS // 2

    def body(q_r, k_r, v_r, o_r, lse_r):
        qi = pl.program_id(2)
        lane = lax.broadcasted_iota(jnp.int32, (blk, LANE), 1)
        even = lane < V_HEAD
        sum_lane = (V_HEAD, 0)
        rowi = lax.broadcasted_iota(jnp.int32, (blk, blk), 0)
        coli = lax.broadcasted_iota(jnp.int32, (blk, blk), 1)
        qs = [q_r[:, h * HP:(h + 1) * HP] for h in range(2)]

        def kv_block(k0, width, carry, diagonal):
            ms, accs = carry
            vblk = v_r[pl.ds(k0, width), :]
            one = jnp.ones_like(vblk)
            zero = jnp.zeros_like(vblk)
            vlane = lax.broadcasted_iota(jnp.int32, (width, LANE), 1)
            new_ms, new_accs = [], []
            for h in range(2):
                kh = k_r[pl.ds(k0, width), h * HP:(h + 1) * HP]
                s = lax.dot_general(qs[h], kh, NT_DIMS, preferred_element_type=F32)
                if diagonal:
                    s = jnp.where(coli <= rowi, s, -jnp.inf)
                m_new = jnp.maximum(ms[h], jnp.max(s, axis=1, keepdims=True))
                alpha = jnp.exp2(ms[h] - m_new)
                p = jnp.exp2(s - m_new).astype(MXU_DT)
                mine = (vlane < V_HEAD) if h == 0 else (vlane >= V_HEAD)
                vh = jnp.where(mine, vblk, jnp.where(vlane == sum_lane[h], one, zero))
                new_accs.append(alpha * accs[h] + jnp.dot(p, vh, preferred_element_type=F32))
                new_ms.append(m_new)
            return tuple(new_ms), tuple(new_accs)

        neg = jnp.full((blk, 1), -jnp.inf, F32)
        zacc = jnp.zeros((blk, LANE), F32)
        carry = lax.fori_loop(0, qi // 2, lambda i, c: kv_block(pl.multiple_of(i * 2 * blk, blk), 2 * blk, c, False),
                              ((neg, neg), (zacc, zacc)))
        carry = lax.cond(qi % 2 == 1, lambda c: kv_block(pl.multiple_of((qi - 1) * blk, blk), blk, c, False),
                         lambda c: c, carry)
        ms, accs = kv_block(pl.multiple_of(qi * blk, blk), blk, carry, True)
        ls = [accs[h][:, sum_lane[h]:sum_lane[h] + 1] for h in range(2)]
        o_r[...] = jnp.where(even, accs[0] / ls[0], accs[1] / ls[1]).astype(MXU_DT)
        lse_r[...] = jnp.where(even, ms[0] + jnp.log2(ls[0]), ms[1] + jnp.log2(ls[1]))

    return _call(
        body, (q, k, v), carry, grid=(nb, npair, nq),
        in_specs=[pl.BlockSpec((blk, 2 * HP), lambda b, p, i: (b * nq + i, p)),
                  pl.BlockSpec((tp, 2 * HP), lambda b, p, i: (b, p)),
                  pl.BlockSpec((tp, LANE), lambda b, p, i: (b, p))],
        out_specs=[pl.BlockSpec((blk, LANE), lambda b, p, i: (b * nq + i, p)),
                   pl.BlockSpec((None, blk, LANE), lambda b, p, i: (p, b * nq + i, 0))],
        out_shape=[S((rows, MLA_HEADS * V_HEAD), MXU_DT), S((npair, rows, LANE), F32)], scratch_shapes=[],
        sem=("parallel", "parallel", "arbitrary"), name=name)


def attn_bwd(q, k, v, o, do, lse, nb, tp, name, carry=None):
    rows = q.shape[0]
    blk = ATT_BLK
    nq = tp // blk
    npair = MLA_HEADS // 2
    scale = QK_HEAD ** -0.5

    def body(q_r, k_r, v_r, o_r, do_r, lse_r, dq_o, dk_o, dv_o, dq_acc, delta_sc):
        kb = pl.program_id(2)
        even = lax.broadcasted_iota(jnp.int32, (blk, LANE), 1) < V_HEAD
        rowi = lax.broadcasted_iota(jnp.int32, (blk, blk), 0)
        coli = lax.broadcasted_iota(jnp.int32, (blk, blk), 1)

        @pl.when(kb == 0)
        def _():
            dq_acc[...] = jnp.zeros_like(dq_acc)

            def dstep(i, c):
                r0 = pl.multiple_of(i * blk, blk)
                prod = do_r[pl.ds(r0, blk), :].astype(F32) * o_r[pl.ds(r0, blk), :].astype(F32)
                de = jnp.sum(jnp.where(even, prod, 0.0), axis=1, keepdims=True)
                dd = jnp.sum(jnp.where(even, 0.0, prod), axis=1, keepdims=True)
                delta_sc[pl.ds(r0, blk), :] = jnp.where(even, de, dd)
                return c

            lax.fori_loop(0, nq, dstep, 0)

        vblk = v_r[...]
        ks = [k_r[:, h * HP:(h + 1) * HP] for h in range(2)]

        def q_block(r0, height, carry, diagonal):
            dk0, dk1, dv = carry
            dob = do_r[pl.ds(r0, height), :]
            lse_b = lse_r[pl.ds(r0, height), :]
            dl_b = delta_sc[pl.ds(r0, height), :]
            qlane = lax.broadcasted_iota(jnp.int32, (height, LANE), 1)
            dks = [dk0, dk1]
            for h in range(2):
                lo = 0 if h == 0 else V_HEAD
                qh = q_r[pl.ds(r0, height), h * HP:(h + 1) * HP]
                s = lax.dot_general(qh, ks[h], NT_DIMS, preferred_element_type=F32)
                p = jnp.exp2(s - lse_b[:, lo:lo + 1])
                if diagonal:
                    p = jnp.where(coli <= rowi, p, 0.0)
                mine = (qlane < V_HEAD) if h == 0 else (qlane >= V_HEAD)
                doh = jnp.where(mine, dob, jnp.zeros_like(dob))
                dp = lax.dot_general(doh, vblk, NT_DIMS, preferred_element_type=F32)
                ds = (p * (dp - dl_b[:, lo:lo + 1])).astype(MXU_DT)
                dv = dv + lax.dot_general(p.astype(MXU_DT), doh, TN_DIMS, preferred_element_type=F32)
                dks[h] = dks[h] + lax.dot_general(ds, qh, TN_DIMS, preferred_element_type=F32)
                dq_acc[pl.ds(r0, height), h * HP:(h + 1) * HP] += jnp.dot(ds, ks[h], preferred_element_type=F32)
            return dks[0], dks[1], dv

        z = jnp.zeros((blk, HP), F32)
        carry = q_block(pl.multiple_of(kb * blk, blk), blk, (z, z, jnp.zeros((blk, LANE), F32)), True)
        below = nq - 1 - kb
        carry = lax.fori_loop(
            0, below // 2, lambda i, c: q_block(pl.multiple_of((kb + 1 + 2 * i) * blk, blk), 2 * blk, c, False), carry)
        dk0, dk1, dv = lax.cond(below % 2 == 1, lambda c: q_block(pl.multiple_of((nq - 1) * blk, blk), blk, c, False),
                                lambda c: c, carry)
        dk_o[:, 0:HP] = (dk0 * (scale / Q_PRESCALE)).astype(MXU_DT)
        dk_o[:, HP:2 * HP] = (dk1 * (scale / Q_PRESCALE)).astype(MXU_DT)
        dv_o[...] = dv.astype(MXU_DT)

        @pl.when(kb == nq - 1)
        def _():
            dq_o[...] = (dq_acc[...] * scale).astype(MXU_DT)

    seq_pair = pl.BlockSpec((tp, LANE), lambda b, p, kk: (b, p))
    return _call(
        body, (q, k, v, o, do, lse), carry, grid=(nb, npair, nq),
        in_specs=[pl.BlockSpec((tp, 2 * HP), lambda b, p, kk: (b, p)),
                  pl.BlockSpec((blk, 2 * HP), lambda b, p, kk: (b * nq + kk, p)),
                  pl.BlockSpec((blk, LANE), lambda b, p, kk: (b * nq + kk, p)),
                  seq_pair, seq_pair, pl.BlockSpec((None, tp, LANE), lambda b, p, kk: (p, b, 0))],
        out_specs=[pl.BlockSpec((tp, 2 * HP), lambda b, p, kk: (b, p)),
                   pl.BlockSpec((blk, 2 * HP), lambda b, p, kk: (b * nq + kk, p)),
                   pl.BlockSpec((blk, LANE), lambda b, p, kk: (b * nq + kk, p))],
        out_shape=[S((rows, MLA_HEADS * HP), MXU_DT), S((rows, MLA_HEADS * HP), MXU_DT),
                   S((rows, MLA_HEADS * V_HEAD), MXU_DT)],
        scratch_shapes=[pltpu.VMEM((tp, 2 * HP), F32), pltpu.VMEM((tp, LANE), F32)],
        sem=("parallel", "parallel", "arbitrary"), name=name)


def rope_bwd(dq, dk, dv, tabq, tabk, tp, tm, name):
    rows = dq.shape[0]
    nt = tp // tm
    wq = MLA_HEADS * HP

    def body(dq_r, dk_r, dv_r, tq_r, tk_r, dqa_o, dkv_o, dkr_o):
        dqv = dq_r[...].astype(F32)
        lane = _lane_mod(dqv.shape)
        in_rope = (lane >= QK_NOPE) & (lane < QK_HEAD)
        rope = jnp.where(in_rope, dqv, 0.0)
        da = jnp.where(lane < QK_HEAD, dqv, 0.0) + pltpu.roll(rope, QK_ROPE, axis=1)
        dqa_o[...] = (da * jnp.tile(tq_r[...], (1, MLA_HEADS))).astype(MXU_DT)
        dkf = dk_r[...].astype(F32)
        dkv_o[:, 0:wq] = jnp.where(lane < QK_NOPE, dkf, 0.0).astype(MXU_DT)
        dkv_o[:, wq:] = dv_r[...]
        kr = jnp.where(in_rope, dkf, 0.0)
        tot = kr[:, 0:HP]
        for h in range(1, MLA_HEADS):
            tot = tot + kr[:, h * HP:(h + 1) * HP]
        dkr_o[...] = ((tot + pltpu.roll(tot, QK_ROPE, axis=1)) * tk_r[...]).astype(MXU_DT)

    def rowblk(wd):
        return pl.BlockSpec((tm, wd), lambda i: (i, 0))

    tab = pl.BlockSpec((tm, HP), lambda i: (i % nt, 0))
    return pl.pallas_call(
        body, grid=(rows // tm,), in_specs=[rowblk(wq), rowblk(wq), rowblk(MLA_HEADS * V_HEAD), tab, tab],
        out_specs=[rowblk(wq), rowblk(wq + MLA_HEADS * V_HEAD), rowblk(HP)],
        out_shape=[S((rows, wq), MXU_DT), S((rows, wq + MLA_HEADS * V_HEAD), MXU_DT), S((rows, HP), MXU_DT)],
        compiler_params=_cp(("parallel",)), name=name)(dq, dk, dv, tabq, tabk)


def loss_head(h, target, gain, tp, t_real, tm, name):
    rows = h.shape[0]
    nt = tp // tm

    def body(h_r, t_r, g_r, dh_o, loss_o, dg_o):
        i = pl.program_id(0)

        @pl.when(i == 0)
        def _():
            loss_o[...] = jnp.zeros_like(loss_o)
            dg_o[...] = jnp.zeros_like(dg_o)

        xv = h_r[...]
        rstd = lax.rsqrt(jnp.mean(xv * xv, axis=-1, keepdims=True) + EPS)
        xhat = xv * rstd
        g = g_r[...]
        pos = (i % nt) * tm + lax.broadcasted_iota(jnp.int32, (tm, 1), 0)
        valid = (pos >= N_META) & (pos < t_real)
        err = jnp.where(valid, xhat * g - t_r[...], 0.0)
        loss_o[...] += 0.5 * jnp.sum(jnp.mean(err * err, axis=-1, keepdims=True))
        dy = err * (1.0 / D_MODEL)
        dg_o[...] += jnp.sum(dy * xhat, axis=0, keepdims=True)
        dxh = dy * g
        dh_o[...] = rstd * (dxh - xhat * jnp.mean(dxh * xhat, axis=-1, keepdims=True))

    blk = pl.BlockSpec((tm, D_MODEL), lambda i: (i, 0))
    return pl.pallas_call(
        body, grid=(rows // tm,), in_specs=[blk, blk, pl.BlockSpec((1, D_MODEL), lambda i: (0, 0))],
        out_specs=[blk, pl.BlockSpec((1, LANE), lambda i: (0, 0)), pl.BlockSpec((1, D_MODEL), lambda i: (0, 0))],
        out_shape=[S((rows, D_MODEL), F32), S((1, LANE), F32), S((1, D_MODEL), F32)],
        compiler_params=_cp(("arbitrary",)), name=name)(h, target, gain)


ADAM_TILE_ELEMS = 128 * 1024


def adamw(g, w, m, v, name, carry=None):
    shape = w.shape
    cols = shape[-1]
    rws = max(1, math.prod(shape[:-1]))
    tr = rws if rws * cols <= ADAM_TILE_ELEMS else _div_tile(rws, max(SUBLANE, ADAM_TILE_ELEMS // cols), SUBLANE)
    bc1 = 1.0 - ADAM_B1 ** ADAM_STEP
    bc2 = 1.0 - ADAM_B2 ** ADAM_STEP

    def body(g_r, w_r, m_r, v_r, go, do, mo, vo):
        gv = g_r[...]
        mn = ADAM_B1 * m_r[...] + (1.0 - ADAM_B1) * gv
        vn = ADAM_B2 * v_r[...] + (1.0 - ADAM_B2) * (gv * gv)
        m_hat = mn / bc1
        v_hat = vn / bc2
        go[...] = gv
        do[...] = -ADAM_LR * (m_hat / (jnp.sqrt(v_hat) + ADAM_EPS) + ADAM_WD * w_r[...])
        mo[...] = mn
        vo[...] = vn

    blk = pl.BlockSpec((tr, cols), lambda i: (i, 0))
    outs, got = _call(
        body, [a.reshape(rws, cols) for a in (g, w, m, v)], carry, grid=(rws // tr,), in_specs=[blk] * 4,
        out_specs=[blk] * 4, out_shape=[S((rws, cols), F32)] * 4, scratch_shapes=[], sem=("parallel",), name=name)
    return tuple(o.reshape(shape) for o in outs), got


SUM_TILE_ELEMS = 128 * 1024


def _place():
    return lax.axis_index("x"), lax.axis_index("y"), lax.axis_index("c")


def _remote(src, dst, send_sems, recv_sems, k, to):
    return pltpu.make_async_remote_copy(src_ref=src, dst_ref=dst, send_sem=send_sems.at[k], recv_sem=recv_sems.at[k],
                                        device_id=to, device_id_type=MESH)


def chip_index():
    return 2 * lax.axis_index("x") + lax.axis_index("y")


def _sem_pair(n):
    return [pltpu.SemaphoreType.DMA((n,)), pltpu.SemaphoreType.DMA((n,))]


def stage_gather_chips(xs):
    def copies(ins, outs, sems):
        send_sems, recv_sems = sems
        mx, my, mc = _place()
        sibling = (mx, my, 1 - mc)
        chips = [(1 - mx, my), (mx, 1 - my), (1 - mx, 1 - my)]
        first, landed, passed, from_sibling = [], [], [], []
        for i, (x_ref, out_ref) in enumerate(zip(ins, outs)):
            def piece(cx, cy, h, out_ref=out_ref):
                return out_ref.at[2 * cx + cy, h]

            for j, (cx, cy) in enumerate(chips):
                k = 6 * i + j
                first.append(_remote(x_ref.at[mc], piece(mx, my, mc), send_sems, recv_sems, k, (cx, cy, mc)))
                landed.append(_remote(x_ref.at[mc], piece(cx, cy, mc), send_sems, recv_sems, k, (cx, cy, mc)))
                passed.append(_remote(piece(cx, cy, mc), piece(cx, cy, mc), send_sems, recv_sems, k + 3, sibling))
                from_sibling.append(_remote(x_ref.at[mc], piece(cx, cy, 1 - mc), send_sems, recv_sems, k + 3, sibling))
        return first, landed, passed, from_sibling

    def start(ins, outs, sems):
        for cp in copies(ins, outs, sems)[0]:
            cp.start()

    def finish(ins, outs, sems):
        first, landed, passed, from_sibling = copies(ins, outs, sems)
        for arrived, onward in zip(landed, passed):
            arrived.wait_recv()
            onward.start()
        for cp in from_sibling:
            cp.wait_recv()
        for cp in first + passed:
            cp.wait_send()

    return Stage(list(xs), [S((4,) + x.shape, x.dtype) for x in xs], _sem_pair(6 * len(xs)), start, finish)


def own_block(gathered, xs):
    return lax.dynamic_update_slice(gathered, xs[None], (chip_index(), 0, 0, 0))


def stage_pair_exchange(gs):
    def copies(ins, outs, sems):
        send_sems, recv_sems = sems
        mx, my, mc = _place()
        return [_remote(g_ref.at[s, 1 - mc], land_ref.at[s], send_sems, recv_sems, 4 * i + s, (mx, my, 1 - mc))
                for i, (g_ref, land_ref) in enumerate(zip(ins, outs)) for s in range(4)]

    def start(ins, outs, sems):
        for cp in copies(ins, outs, sems):
            cp.start()

    def finish(ins, outs, sems):
        cps = copies(ins, outs, sems)
        for cp in cps:
            cp.wait_recv()
        for cp in cps:
            cp.wait_send()

    return Stage(list(gs), [S((4,) + g.shape[2:], g.dtype) for g in gs], _sem_pair(4 * len(gs)), start, finish)


def _sum_rows(rws, width):
    return _div_tile(rws, max(SUBLANE, SUM_TILE_ELEMS // width), SUBLANE)


def pair_sum(g4, land, c_idx, name):
    _, _, rws, wd = g4.shape
    th = _sum_rows(rws, wd)

    def body(c_ref, a_ref, b_ref, o_ref):
        o_ref[...] = a_ref[...] + b_ref[...]

    return pl.pallas_call(
        body,
        grid_spec=pltpu.PrefetchScalarGridSpec(
            num_scalar_prefetch=1, grid=(4, rws // th),
            in_specs=[pl.BlockSpec((None, None, th, wd), lambda s, i, c: (s, c[0], i, 0)),
                      pl.BlockSpec((None, th, wd), lambda s, i, c: (s, i, 0))],
            out_specs=pl.BlockSpec((None, th, wd), lambda s, i, c: (s, i, 0))),
        out_shape=S((4, rws, wd), F32), compiler_params=_cp(("parallel", "parallel")), name=name)(c_idx, g4, land)


def stage_chip_scatter(ps):
    def copies(ins, outs, sems):
        send_sems, recv_sems = sems
        mx, my, mc = _place()
        me = 2 * mx + my
        chips = [(1 - mx, my), (mx, 1 - my), (1 - mx, 1 - my)]
        sent, landed = [], []
        for i, (p_ref, land_ref) in enumerate(zip(ins, outs)):
            for j, (cx, cy) in enumerate(chips):
                k = 3 * i + j
                sent.append(_remote(p_ref.at[2 * cx + cy], land_ref.at[me], send_sems, recv_sems, k, (cx, cy, mc)))
                landed.append(_remote(p_ref.at[me], land_ref.at[2 * cx + cy], send_sems, recv_sems, k, (cx, cy, mc)))
        return sent, landed

    def start(ins, outs, sems):
        for cp in copies(ins, outs, sems)[0]:
            cp.start()

    def finish(ins, outs, sems):
        sent, landed = copies(ins, outs, sems)
        for cp in landed:
            cp.wait_recv()
        for cp in sent:
            cp.wait_send()

    return Stage(list(ps), [S(p.shape, p.dtype) for p in ps], _sem_pair(3 * len(ps)), start, finish)


def chip_sum(l4, p4, me_idx, name):
    _, rws, wd = l4.shape
    th = _sum_rows(rws, wd)

    def body(me_ref, a, b, c, d, own, o_ref):
        me = me_ref[0]
        parts = [jnp.where(me == s, own[...], r[...]) for s, r in enumerate((a, b, c, d))]
        o_ref[...] = ((parts[0] + parts[1]) + parts[2]) + parts[3]

    def blk(s):
        return pl.BlockSpec((None, th, wd), lambda i, me: (jnp.where(me[0] == s, (s + 1) % 4, s), i, 0))

    return pl.pallas_call(
        body,
        grid_spec=pltpu.PrefetchScalarGridSpec(
            num_scalar_prefetch=1, grid=(rws // th,),
            in_specs=[blk(0), blk(1), blk(2), blk(3), pl.BlockSpec((None, th, wd), lambda i, me: (me[0], i, 0))],
            out_specs=pl.BlockSpec((th, wd), lambda i, me: (i, 0))),
        out_shape=S((rws, wd), F32), compiler_params=_cp(("parallel",)), name=name)(me_idx, l4, l4, l4, l4, p4)


def stage_pair_gather(rs):
    def copies(ins, outs, sems):
        send_sems, recv_sems = sems
        mx, my, mc = _place()
        return [_remote(r_ref, out_ref, send_sems, recv_sems, i, (mx, my, 1 - mc))
                for i, (r_ref, out_ref) in enumerate(zip(ins, outs))]

    def start(ins, outs, sems):
        for cp in copies(ins, outs, sems):
            cp.start()

    def finish(ins, outs, sems):
        for cp in copies(ins, outs, sems):
            cp.wait()

    return Stage(list(rs), [S(r.shape, r.dtype) for r in rs], _sem_pair(len(rs)), start, finish)


PACK_ELEMS = 16 * LANE


def pack_rows(arrays, lead, total_mult):
    parts, offs, r0 = [], [], 0
    for a in arrays:
        flat = a.reshape(a.shape[:lead] + (-1,))
        elems = _round_up(flat.shape[-1], PACK_ELEMS)
        flat = jnp.pad(flat, [(0, 0)] * lead + [(0, elems - flat.shape[-1])])
        parts.append(flat.reshape(flat.shape[:lead] + (elems // LANE, LANE)))
        offs.append((r0, elems // LANE))
        r0 += elems // LANE
    total = _round_up(r0, total_mult)
    if total > r0:
        parts.append(jnp.zeros(parts[0].shape[:lead] + (total - r0, LANE), parts[0].dtype))
    return jnp.concatenate(parts, axis=lead), offs


def unpack_rows(buf, off, shape):
    r0, nr = off
    lead = buf.shape[:-2]
    n = math.prod(shape)
    return buf[..., r0:r0 + nr, :].reshape(lead + (nr * LANE,))[..., :n].reshape(lead + tuple(shape))


def unshard(stacked, axis):
    x = jnp.moveaxis(stacked, 0, axis)
    return x.reshape(x.shape[:axis] + (4 * x.shape[axis + 1],) + x.shape[axis + 2:])


def to_shards(full, axis):
    n = full.shape[axis] // 4
    x = full.reshape(full.shape[:axis] + (4, n) + full.shape[axis + 1:])
    return jnp.moveaxis(x, axis, 0)


def _rot_cols(w):
    half = w.shape[-1] // 2
    return jnp.concatenate([-w[..., half:], w[..., :half]], axis=-1)


def _unrot_cols(dw):
    half = dw.shape[-1] // 2
    return jnp.concatenate([dw[..., half:], -dw[..., :half]], axis=-1)


def odd_w_in_padded(w_in):
    kr = w_in[:, Q_LORA + KV_LORA:]
    rows = w_in.shape[0]
    return jnp.concatenate([w_in[:, :Q_LORA], jnp.zeros((rows, 128), w_in.dtype), w_in[:, Q_LORA:Q_LORA + KV_LORA],
                            jnp.zeros((rows, 64), w_in.dtype), kr, _rot_cols(kr)], axis=1)


def odd_w_in_unpad(dwp):
    base = 512 + KV_LORA + 64
    dkr = dwp[:, base:base + QK_ROPE] + _unrot_cols(dwp[:, base + QK_ROPE:base + 2 * QK_ROPE])
    return jnp.concatenate([dwp[:, :Q_LORA], dwp[:, 512:512 + KV_LORA], dkr], axis=1)


def uq_padded(w_uq):
    w = w_uq.reshape(Q_LORA, MLA_HEADS, QK_HEAD)
    return jnp.concatenate([w, _rot_cols(w[:, :, QK_NOPE:])], axis=-1).reshape(Q_LORA, MLA_HEADS * HP)


def uq_unpad(dwp):
    d = dwp.reshape(Q_LORA, MLA_HEADS, HP)
    rope = d[:, :, QK_NOPE:QK_HEAD] + _unrot_cols(d[:, :, QK_HEAD:])
    return jnp.concatenate([d[:, :, :QK_NOPE], rope], axis=-1).reshape(Q_LORA, MLA_HEADS * QK_HEAD)


def ukv_padded(w_ukv):
    w = w_ukv.reshape(KV_LORA, MLA_HEADS, QK_NOPE + V_HEAD)
    wk = jnp.concatenate([w[:, :, :QK_NOPE], jnp.zeros((KV_LORA, MLA_HEADS, HP - QK_NOPE), w.dtype)], axis=-1)
    return jnp.concatenate([wk.reshape(KV_LORA, MLA_HEADS * HP), w[:, :, QK_NOPE:].reshape(KV_LORA, MLA_HEADS * V_HEAD)],
                           axis=1)


def ukv_unpad(dwp):
    dk = dwp[:, :MLA_HEADS * HP].reshape(KV_LORA, MLA_HEADS, HP)[:, :, :QK_NOPE]
    dv = dwp[:, MLA_HEADS * HP:].reshape(KV_LORA, MLA_HEADS, V_HEAD)
    return jnp.concatenate([dk, dv], axis=-1).reshape(KV_LORA, MLA_HEADS * (QK_NOPE + V_HEAD))


def block_diag(w):
    h, d, _ = w.shape
    eye = jnp.eye(h, dtype=w.dtype)
    return (eye[:, None, :, None] * w[:, :, None, :]).reshape(h * d, h * d)


def block_diag_part(dense, h):
    d = dense.shape[0] // h
    x = dense.reshape(h, d, h, d)
    return jnp.stack([x[i, :, i, :] for i in range(h)], axis=0)


def rope_tables(tp):
    pos = jnp.arange(tp, dtype=F32)
    inv_freq = ROPE_BASE ** (-jnp.arange(0, QK_ROPE, 2, dtype=F32) / QK_ROPE)
    ang = pos[:, None] * inv_freq[None, :]
    cos2 = jnp.tile(jnp.cos(ang), (1, 2))
    sin2 = jnp.tile(jnp.sin(ang), (1, 2))
    tabq = jnp.concatenate([jnp.ones((tp, QK_NOPE), F32), cos2, sin2], axis=1)
    tabk = jnp.concatenate([jnp.zeros((tp, QK_NOPE), F32), cos2, sin2], axis=1)
    return tabq, tabk


class Dims:
    def __init__(self, nb, seq):
        self.nb = nb
        self.t_real = seq + N_META
        self.tp = _round_up(self.t_real, ATT_BLK)
        self.n = self.tp // 4
        assert self.n % 16 == 0
        self.rows = nb * self.tp


class NoComm:
    def advance(self, carried):
        return None


def even_fwd(h, p, dm, comm):
    (u, hn), _ = norm_matmul(h, 0, D_MODEL, p["norm"], p["w_in"], dm.n, 512, F32, "ev_in")
    (y, ca, xc, a, hs), got = even_mid_fwd(u, p["conv_a"], p["conv_b"], p["conv_b_bias"], p["rw"], p["r_b"], p["iw"],
                                           p["i_b"], p["lam"], dm.nb, dm.tp, dm.n, "ev_mid", carry=comm.advance(None))
    comm.advance(got)
    out = matmul_res(y, p["w_out"].reshape(2, CONV_W, D_MODEL), h, dm.n, 512, "ev_out")
    return out, (h, u, hn, ca, xc, a, hs, y)


def even_bwd(dout, saved, p, dm, comm):
    h, u, hn, ca, xc, a, hs, y = saved
    g = {}
    dycat = matmul_nt(dout, p["w_out"], dm.n, 512, F32, "ev_dycat")
    g["w_out"], got = matmul_tn(y, dout, dm.n, "ev_dw_out", carry=comm.advance(None))
    outs, got = even_mid_bwd(u, dycat, ca, xc, a, hs, p["conv_a"], p["conv_b"], p["rw"], p["r_b"], p["iw"], p["i_b"],
                             p["lam"], dm.nb, dm.tp, dm.n, "ev_mid_bwd", carry=comm.advance(got))
    du, g["conv_a"], g["conv_b"], g["conv_b_bias"], drw, g["r_b"], diw, g["i_b"], g["lam"] = outs
    g["r_w"] = block_diag_part(drw, LRU_HEADS)
    g["i_w"] = block_diag_part(diw, LRU_HEADS)
    g["w_in"], got = matmul_tn(hn, du, dm.n, "ev_dw_in", carry=comm.advance(got))
    comm.advance(got)
    dx, g["norm"] = matmul_nt_normbwd(du, p["w_in"], h, 0, p["norm"], dout, dm.n, 512, F32, "ev_dx")
    return dx, g


def odd_fwd(h, p, tabq, tabk, dm, comm):
    nt = dm.tp // dm.n
    (u, hn), _ = norm_matmul(h, 0, D_MODEL, p["norm"], p["w_in_p"], dm.n, ODD_PAD, F32, "od_in")
    tab_spec = pl.BlockSpec((dm.n, HP), lambda i, j: (i % nt, 0))
    (q, cqn), _ = norm_matmul(u, 0, Q_LORA, p["q_norm"], p["w_uq_p"], dm.n, 512, MXU_DT, "od_q",
                              epi=_q_rope_epi, epi_ops=(tabq,), epi_specs=(tab_spec,))
    kr_spec = pl.BlockSpec((dm.n, HP), lambda i, j: (i, ODD_KR_COL))
    (k, ckvn), _ = norm_matmul(u, ODD_CKV_COL, KV_LORA, p["kv_norm"], p["w_uk_p"], dm.n, 512, MXU_DT, "od_k",
                               epi=_k_rope_epi, epi_ops=(u, tabk), epi_specs=(kr_spec, tab_spec))
    (v, _), _ = norm_matmul(u, ODD_CKV_COL, KV_LORA, p["kv_norm"], p["w_uv_p"], dm.n, 512, MXU_DT, "od_v")
    (o, lse), got = attn_fwd(q, k, v, dm.nb, dm.tp, "od_attn", carry=comm.advance(None))
    comm.advance(got)
    out = matmul_res(o[None], p["w_out"][None], h, dm.n, 512, "od_out")
    return out, (h, u, hn, cqn, ckvn, q, k, v, o, lse)


def odd_bwd(dout, saved, p, tabq, tabk, dm, comm):
    h, u, hn, cqn, ckvn, q, k, v, o, lse = saved
    g = {}
    do = matmul_nt(dout, p["w_out"], dm.n, 512, MXU_DT, "od_do")
    g["w_out"], got = matmul_tn(o, dout, dm.n, "od_dw_out", carry=comm.advance(None))
    (dq, dk, dv), got = attn_bwd(q, k, v, o, do, lse, dm.nb, dm.tp, "od_attn_bwd", carry=comm.advance(got))
    dqa, dkv, dkr = rope_bwd(dq, dk, dv, tabq, tabk, dm.tp, dm.n, "od_rope_bwd")
    g["w_uq_p"], got = matmul_tn(cqn, dqa, dm.n, "od_dw_uq", carry=comm.advance(got))
    comm.advance(got)
    g["w_ukv_p"], _ = matmul_tn(ckvn, dkv, dm.n, "od_dw_ukv")
    dcq, g["q_norm"] = matmul_nt_normbwd(dqa, p["w_uq_p"], u, 0, p["q_norm"], None, dm.n, 512, MXU_DT, "od_dcq")
    dckv, g["kv_norm"] = matmul_nt_normbwd(dkv, p["w_ukv_p"], u, ODD_CKV_COL, p["kv_norm"], None, dm.n, 512, MXU_DT,
                                           "od_dckv")
    du = jnp.concatenate([dcq, jnp.zeros((dm.rows, 128), MXU_DT), dckv, dkr], axis=1)
    g["w_in_p"], _ = matmul_tn(hn, du, dm.n, "od_dw_in")
    dx, g["norm"] = matmul_nt_normbwd(du, p["w_in_p"], h, 0, p["norm"], dout, dm.n, ODD_PAD, F32, "od_dx")
    return dx, g


def ffn_fwd(h, p, dm, comm):
    (up, hn), got = norm_matmul(h, 0, D_MODEL, p["norm"], p["w_up"], dm.n, D_FF // 2, MXU_DT, "ffn_up",
                                carry=comm.advance(None))
    comm.advance(got)
    u, y = ffn_mid_fwd(up, p["cw"], p["cb"], dm.nb, dm.tp, dm.n, "ffn_mid")
    out = matmul_res(y[None], p["w_down"][None], h, dm.n, 512, "ffn_down")
    return out, (h, up, hn, u, y)


def ffn_bwd(dout, saved, p, dm, comm):
    h, up, hn, u, y = saved
    g = {}
    dy = matmul_nt(dout, p["w_down"], dm.n, D_FF // 2, MXU_DT, "ffn_dy")
    g["w_down"], got = matmul_tn(y, dout, dm.n, "ffn_dw_down", carry=comm.advance(None))
    (dup, g["cw"], g["cb"]), got = ffn_mid_bwd(dy, u, up, p["cw"], dm.nb, dm.tp, dm.n, "ffn_mid_bwd",
                                               carry=comm.advance(got))
    g["w_up"], got = matmul_tn(hn, dup, dm.n, "ffn_dw_up", carry=comm.advance(got), col_shards=4)
    comm.advance(got)
    dx, g["norm"] = matmul_nt_normbwd(dup, p["w_up"], h, 0, p["norm"], dout, dm.n, D_FF // 2, F32, "ffn_dx")
    return dx, g


def _row(v):
    return v.reshape(1, -1)


def even_params(wf, j):
    return dict(norm=_row(wf["ev_norm"][j]), w_in=wf["ev_w_in"], conv_a=wf["ev_conv_a"][j], conv_b=wf["ev_conv_b"][j],
                conv_b_bias=_row(wf["ev_conv_b_bias"][j]), rw=block_diag(wf["ev_gate_r_w"][j]).astype(MXU_DT),
                r_b=_row(wf["ev_gate_r_b"][j]), iw=block_diag(wf["ev_gate_i_w"][j]).astype(MXU_DT),
                i_b=_row(wf["ev_gate_i_b"][j]), lam=_row(wf["ev_lru_lambda"][j]), w_out=wf["ev_w_out"])


def odd_params(wf, j):
    wkv = ukv_padded(wf["od_w_ukv"])
    return dict(norm=_row(wf["od_norm"][j]), w_in_p=odd_w_in_padded(wf["od_w_in"]), q_norm=_row(wf["od_q_norm"][j]),
                kv_norm=_row(wf["od_kv_norm"][j]), w_uq_p=uq_padded(wf["od_w_uq"]), w_ukv_p=wkv,
                w_uk_p=wkv[:, :MLA_HEADS * HP], w_uv_p=wkv[:, MLA_HEADS * HP:], w_out=wf["od_w_out"])


def ffn_params(wf, layer):
    return dict(norm=_row(wf["ffn_norm"][layer]), w_up=wf["ffn_w_up"],
                cw=jnp.moveaxis(wf["ffn_conv_w"][layer].reshape(3, 2, D_FF), 1, 0),
                cb=wf["ffn_conv_b"][layer].reshape(2, 1, D_FF), w_down=wf["ffn_w_down"])


def even_grads(g):
    out = {"ev_" + k_: g[k_] for k_ in ("w_in", "conv_a", "conv_b", "w_out")}
    out.update({"ev_norm": g["norm"][0], "ev_conv_b_bias": g["conv_b_bias"][0], "ev_gate_r_w": g["r_w"],
                "ev_gate_r_b": g["r_b"][0], "ev_gate_i_w": g["i_w"], "ev_gate_i_b": g["i_b"][0],
                "ev_lru_lambda": g["lam"][0]})
    return out


def odd_grads(g):
    return {"od_norm": g["norm"][0], "od_q_norm": g["q_norm"][0], "od_kv_norm": g["kv_norm"][0],
            "od_w_in": odd_w_in_unpad(g["w_in_p"]), "od_w_uq": uq_unpad(g["w_uq_p"]),
            "od_w_ukv": ukv_unpad(g["w_ukv_p"]), "od_w_out": g["w_out"]}


def ffn_grads(g):
    return {"ffn_norm": g["norm"][0], "ffn_w_up": g["w_up"], "ffn_conv_w": jnp.moveaxis(g["cw"], 0, 1).reshape(3, 2 * D_FF),
            "ffn_conv_b": g["cb"].reshape(2 * D_FF), "ffn_w_down": g["w_down"]}


WEIGHTS = ["meta_tokens", "ev_norm", "ev_w_in", "ev_conv_a", "ev_conv_b", "ev_conv_b_bias", "ev_gate_r_w", "ev_gate_r_b",
           "ev_gate_i_w", "ev_gate_i_b", "ev_lru_lambda", "ev_w_out", "od_norm", "od_w_in", "od_q_norm", "od_kv_norm",
           "od_w_uq", "od_w_ukv", "od_w_out", "ffn_norm", "ffn_w_up", "ffn_conv_w", "ffn_conv_b", "ffn_w_down",
           "final_norm"]
SHARD_AXIS = {"meta_tokens": 1, "ev_w_in": 2, "ev_conv_a": 2, "ev_conv_b": 2, "ev_w_out": 1, "od_norm": 1, "od_w_in": 1,
              "od_q_norm": 1, "od_kv_norm": 1, "od_w_uq": 2, "od_w_ukv": 2, "od_w_out": 1, "ffn_w_up": 2,
              "ffn_conv_w": 2, "ffn_w_down": 1}
MATMUL_WEIGHTS = ["ev_w_in", "ev_w_out", "od_w_in", "od_w_uq", "od_w_ukv", "od_w_out", "ffn_w_up", "ffn_w_down"]


LAYER_ORDER = [("ev", 0), ("ffn", 0), ("od", 0), ("ffn", 1), ("ev", 1), ("ffn", 2), ("od", 1), ("ffn", 3)]
LAYER_MATMUL = {"ev": ["ev_w_in", "ev_w_out"], "od": ["od_w_in", "od_w_uq", "od_w_ukv", "od_w_out"],
                "ffn": ["ffn_w_up", "ffn_w_down"]}
LAYER_SHARDED = {"ev": ["ev_w_in", "ev_conv_a", "ev_conv_b", "ev_w_out"],
                 "od": ["od_norm", "od_w_in", "od_q_norm", "od_kv_norm", "od_w_uq", "od_w_ukv", "od_w_out"],
                 "ffn": ["ffn_w_up", "ffn_conv_w", "ffn_w_down"]}
STACKED_SHARDS = "ffn_w_up"


def gather_all_layers(w, names, name):
    buf, offs = pack_rows([w[n] for n in names], 0, 32)
    halves = buf.reshape(2, buf.shape[0] // 2, LANE)
    got = own_block(run_stage(stage_gather_chips([halves]), name)[0], halves).reshape(4, buf.shape[0], LANE)
    return {n: unshard(unpack_rows(got, off, w[n].shape), SHARD_AXIS[n]) for n, off in zip(names, offs)}


def _halves(a):
    return a.reshape(2, a.shape[0] // 2, a.shape[1])


class GatherComm:
    def __init__(self, w, kind, idx):
        self.names = LAYER_MATMUL[kind]
        self.halves = [_halves(w[n][idx].astype(MXU_DT)) for n in self.names]
        self.stage = stage_gather_chips(self.halves)
        self.step, self.got = 0, None

    def advance(self, carried):
        self.step += 1
        if self.step == 1:
            return self.stage
        if self.step == 2:
            self.got = carried
        return None

    def run_alone(self, name):
        self.advance(run_stage(self.advance(None), name))

    def weights(self):
        out = {}
        for n, got, own in zip(self.names, self.got, self.halves):
            stacked = own_block(got, own).reshape(4, 2 * own.shape[1], own.shape[2])
            out[n] = stacked if n == STACKED_SHARDS else unshard(stacked, SHARD_AXIS[n] - 1)
        return out


class ReduceComm:
    def __init__(self, grads, axes, c_idx, tag, tail=None):
        shards = {n: grads[n] if n == STACKED_SHARDS else to_shards(grads[n], axes[n]) for n in grads}
        self.big = [n for n in grads if n in MATMUL_WEIGHTS]
        self.small = [n for n in grads if n not in MATMUL_WEIGHTS]
        self.shapes = {n: shards[n].shape[1:] for n in grads}
        arrays = [shards[n].reshape(4, 2, shards[n].shape[1] // 2, shards[n].shape[2]) for n in self.big]
        gs, self.offs = pack_rows([shards[n] for n in self.small], 1, 16)
        self.rs = gs.shape[1] // 2
        parts = [gs.reshape(4, 2, self.rs, LANE)]
        self.rr = 0
        if tail is not None:
            self.rr = tail.shape[0] // 8
            parts.append(tail.reshape(4, 2, self.rr, LANE))
        arrays.append(jnp.concatenate(parts, axis=2) if len(parts) > 1 else parts[0])
        self.arrays, self.c_idx, self.tag, self.step = arrays, c_idx, tag, 0
        self.part = self.mine = self.theirs = None

    def advance(self, carried):
        self.step += 1
        if self.step == 1:
            return stage_pair_exchange(self.arrays)
        if self.step == 2:
            self.part = [pair_sum(g, land, self.c_idx, "grad_pair_sum_%s_%d" % (self.tag, i))
                         for i, (g, land) in enumerate(zip(self.arrays, carried))]
            return stage_chip_scatter(self.part)
        if self.step == 3:
            me_idx = chip_index().astype(jnp.int32).reshape(1)
            self.mine = [chip_sum(land, part, me_idx, "grad_chip_sum_%s_%d" % (self.tag, i))
                         for i, (land, part) in enumerate(zip(carried, self.part))]
            return stage_pair_gather(self.mine)
        if self.step == 4:
            self.theirs = carried
        return None

    def run_alone(self, name):
        stage = self.advance(None)
        while stage is not None:
            stage = self.advance(run_stage(stage, name + "_%d" % self.step))

    def results(self):
        south = self.c_idx[0] == 0
        boths = [jnp.stack([jnp.where(south, m, t), jnp.where(south, t, m)], axis=0)
                 for m, t in zip(self.mine, self.theirs)]
        out = {n: b.reshape(self.shapes[n]) for n, b in zip(self.big, boths)}
        packed = boths[-1]
        flat = packed[:, :self.rs].reshape(2 * self.rs, LANE)
        out.update({n: unpack_rows(flat, off, self.shapes[n]) for n, off in zip(self.small, self.offs)})
        return out, packed[:, self.rs:self.rs + self.rr]


def kernel(x, meta_tokens, ev_norm, ev_w_in, ev_conv_a, ev_conv_b, ev_conv_b_bias, ev_gate_r_w, ev_gate_r_b, ev_gate_i_w, ev_gate_i_b, ev_lru_lambda, ev_w_out, od_norm, od_w_in, od_q_norm, od_kv_norm, od_w_uq, od_w_ukv, od_w_out, ffn_norm, ffn_w_up, ffn_conv_w, ffn_conv_b, ffn_w_down, final_norm, loss_target, m_meta_tokens, m_ev_norm, m_ev_w_in, m_ev_conv_a, m_ev_conv_b, m_ev_conv_b_bias, m_ev_gate_r_w, m_ev_gate_r_b, m_ev_gate_i_w, m_ev_gate_i_b, m_ev_lru_lambda, m_ev_w_out, m_od_norm, m_od_w_in, m_od_q_norm, m_od_kv_norm, m_od_w_uq, m_od_w_ukv, m_od_w_out, m_ffn_norm, m_ffn_w_up, m_ffn_conv_w, m_ffn_conv_b, m_ffn_w_down, m_final_norm, v_meta_tokens, v_ev_norm, v_ev_w_in, v_ev_conv_a, v_ev_conv_b, v_ev_conv_b_bias, v_ev_gate_r_w, v_ev_gate_r_b, v_ev_gate_i_w, v_ev_gate_i_b, v_ev_lru_lambda, v_ev_w_out, v_od_norm, v_od_w_in, v_od_q_norm, v_od_kv_norm, v_od_w_uq, v_od_w_ukv, v_od_w_out, v_ffn_norm, v_ffn_w_up, v_ffn_conv_w, v_ffn_conv_b, v_ffn_w_down, v_final_norm):
    given = dict(locals())
    w = {n: given[n] for n in WEIGHTS}
    nb, seq, _ = x.shape
    dm = Dims(nb, seq)
    n_layers = len(LAYER_ORDER)

    wf = {n: w[n] for n in WEIGHTS if n not in SHARD_AXIS}
    wf.update(gather_all_layers(w, [n for n in SHARD_AXIS if n not in MATMUL_WEIGHTS], "gather_small_weights"))
    gathers = [GatherComm(w, kind, idx) for kind, idx in LAYER_ORDER]
    gathers[0].run_alone("gather_first_layer")

    tail = dm.tp - dm.t_real
    meta = jnp.broadcast_to(wf["meta_tokens"][None], (nb, N_META, D_MODEL))
    h = jnp.concatenate([meta, x, jnp.zeros((nb, tail, D_MODEL), F32)], axis=1).reshape(dm.rows, D_MODEL)
    tgt = jnp.pad(loss_target, ((0, 0), (N_META, tail), (0, 0))).reshape(dm.rows, D_MODEL)
    tabq, tabk = rope_tables(dm.tp)

    params, saved = [], []
    for i, (kind, idx) in enumerate(LAYER_ORDER):
        wl = dict(wf)
        wl.update(gathers[i].weights())
        comm = gathers[i + 1] if i + 1 < n_layers else NoComm()
        if kind == "ev":
            p = even_params(wl, idx)
            h, sv = even_fwd(h, p, dm, comm)
        elif kind == "od":
            p = odd_params(wl, idx)
            h, sv = odd_fwd(h, p, tabq, tabk, dm, comm)
        else:
            p = ffn_params(wl, idx)
            h, sv = ffn_fwd(h, p, dm, comm)
        params.append(p)
        saved.append(sv)

    dh, loss, dfinal = loss_head(h, tgt, _row(wf["final_norm"]), dm.tp, dm.t_real, dm.n, "loss_head")
    loss = lax.psum(loss[0, 0], ("x", "y", "c"))

    c_idx = lax.axis_index("c").astype(jnp.int32).reshape(1)
    layer_grads = {n: {} for n in WEIGHTS}
    pending, reduces = NoComm(), []
    for i in reversed(range(n_layers)):
        kind, idx = LAYER_ORDER[i]
        if kind == "ev":
            dh, g = even_bwd(dh, saved[i], params[i], dm, pending)
            g = even_grads(g)
        elif kind == "od":
            dh, g = odd_bwd(dh, saved[i], params[i], tabq, tabk, dm, pending)
            g = odd_grads(g)
        else:
            dh, g = ffn_bwd(dh, saved[i], params[i], dm, pending)
            g = ffn_grads(g)
        for n in g:
            if n not in SHARD_AXIS:
                layer_grads[n][idx] = g[n]
        if i > 0:
            pending = ReduceComm({n: g[n] for n in LAYER_SHARDED[kind]}, {n: SHARD_AXIS[n] - 1 for n in SHARD_AXIS},
                                 c_idx, "%s%d" % (kind, idx))
            reduces.append((pending, idx))
    dh3 = dh.reshape(nb, dm.tp, D_MODEL)
    grad_x = dh3[:, N_META:dm.t_real]

    repl = [n for n in WEIGHTS if n not in SHARD_AXIS]
    layer_grads["final_norm"] = {0: dfinal[0]}
    repl_full = {n: (layer_grads[n][0] if n == "final_norm" else
                     jnp.stack([layer_grads[n][j] for j in range(w[n].shape[0])], axis=0)) for n in repl}
    tail_buf, tail_offs = pack_rows([repl_full[n] for n in repl], 0, 64)
    first = {n: g[n] for n in LAYER_SHARDED["ev"]}
    first["meta_tokens"] = jnp.sum(dh3[:, :N_META], axis=0)
    axes = {n: SHARD_AXIS[n] - 1 for n in SHARD_AXIS}
    axes["meta_tokens"] = SHARD_AXIS["meta_tokens"]
    last = ReduceComm(first, axes, c_idx, "first_layer", tail=tail_buf)
    last.run_alone("grad_first_layer")
    reduces.append((last, 0))

    red = {}
    for comm, idx in reduces:
        got, tail_piece = comm.results()
        for n, v_ in got.items():
            if n == "meta_tokens":
                red[n] = v_
            else:
                layer_grads[n][idx] = v_
    tails = own_block(run_stage(stage_gather_chips([tail_piece]), "grad_gather_replicated")[0], tail_piece)
    tails = tails.reshape(tail_buf.shape[0], LANE)
    for n, off in zip(repl, tail_offs):
        red[n] = unpack_rows(tails, off, w[n].shape)
    for n in SHARD_AXIS:
        if n != "meta_tokens":
            red[n] = jnp.stack([layer_grads[n][j] for j in range(w[n].shape[0])], axis=0)

    outs = [adamw(red[n], w[n], given["m_" + n], given["v_" + n], "adamw_" + n)[0] for n in WEIGHTS]
    return (loss, grad_x, *[o[0] for o in outs], *[o[1] for o in outs], *[o[2] for o in outs], *[o[3] for o in outs])
```

```python
import math

import jax
import jax.numpy as jnp
from jax import lax
from jax.experimental import pallas as pl
from jax.experimental.pallas import tpu as pltpu

F32 = jnp.float32
MXU_DT = jnp.bfloat16
S = jax.ShapeDtypeStruct
MESH = pl.DeviceIdType.MESH

EPS = 1e-6
D_MODEL = 1024
N_META = 16
DEPTH = 4
CONV_W = 512
LRU_W = 512
LRU_HEADS = 8
LRU_C = 8.0
EVEN_IN = 2560
MLA_HEADS = 16
QK_NOPE = 64
QK_ROPE = 32
QK_HEAD = 96
V_HEAD = 64
Q_LORA = 384
KV_LORA = 256
ROPE_BASE = 10000.0
D_FF = 2816
ODD_PAD = 896
ODD_CKV_COL = 2
ODD_KR_COL = 6
HP = 128
ATT_BLK = 384
Q_PRESCALE = QK_HEAD ** -0.5 * math.log2(math.e)
FFN_CT = 256
STRIP_ROWS = 16
LANE = 128
SUBLANE = 8
VMEM_LIMIT_MB = 52

ADAM_LR = 0.001
ADAM_B1 = 0.9
ADAM_B2 = 0.999
ADAM_EPS = 1e-08
ADAM_WD = 0.01
ADAM_STEP = 10

NT_DIMS = (((1,), (1,)), ((), ()))
TN_DIMS = (((0,), (0,)), ((), ()))


def _cp(sem):
    return pltpu.CompilerParams(dimension_semantics=sem, vmem_limit_bytes=VMEM_LIMIT_MB << 20)


def _div_tile(n, cap, mult):
    if n <= cap:
        return n
    best = None
    for t in range(mult, cap + 1, mult):
        if n % t == 0:
            best = t
    assert best is not None, (n, cap, mult)
    return best


def _round_up(n, m):
    return -(-n // m) * m


def mat_cols(arr):
    return arr.shape[1] if arr.ndim == 2 else arr.shape[0] * arr.shape[2]


def mat_width(arr):
    return arr.shape[-1]


def mat_spec(arr, tm, tw, rc):
    if arr.ndim == 2:
        return pl.BlockSpec((tm, tw), lambda *g: rc(*g))
    per = arr.shape[2] // tw
    assert arr.shape[2] % tw == 0

    def imap(*g):
        r, c = rc(*g)
        return (c // per, r, c % per)

    return pl.BlockSpec((None, tm, tw), imap)


HBM_SPEC = pl.BlockSpec(memory_space=pltpu.HBM)


class Stage:
    def __init__(self, inputs, out_shapes, sems, start, finish):
        self.inputs, self.out_shapes, self.sems, self.start, self.finish = inputs, out_shapes, sems, start, finish


def run_stage(stage, name):
    n_in, n_out = len(stage.inputs), len(stage.out_shapes)

    def body(*refs):
        ins, outs, sems = refs[:n_in], refs[n_in:n_in + n_out], refs[n_in + n_out:]
        stage.start(ins, outs, sems)
        stage.finish(ins, outs, sems)

    return pl.pallas_call(body, out_shape=list(stage.out_shapes), in_specs=[HBM_SPEC] * n_in,
                          out_specs=[HBM_SPEC] * n_out, scratch_shapes=list(stage.sems), name=name)(*stage.inputs)


def _call(body, ops, carry, *, grid, in_specs, out_specs, out_shape, scratch_shapes, sem, name):
    if carry is None:
        outs = pl.pallas_call(body, grid=grid, in_specs=in_specs, out_specs=out_specs, out_shape=out_shape,
                              scratch_shapes=scratch_shapes, compiler_params=_cp(sem), name=name)(*ops)
        return outs, None
    multi = isinstance(out_shape, (list, tuple))
    shapes = list(out_shape) if multi else [out_shape]
    ospecs = list(out_specs) if multi else [out_specs]
    n_in, n_out, n_sc = len(ops), len(shapes), len(scratch_shapes)
    c_in, c_out = len(carry.inputs), len(carry.out_shapes)

    def wrapped(*refs):
        ins, cin = refs[:n_in], refs[n_in:n_in + c_in]
        o0 = n_in + c_in
        outs, cout = refs[o0:o0 + n_out], refs[o0 + n_out:o0 + n_out + c_out]
        s0 = o0 + n_out + c_out
        scs, csems = refs[s0:s0 + n_sc], refs[s0 + n_sc:]
        first = pl.program_id(0) == 0
        last = pl.program_id(0) == grid[0] - 1
        for d in range(1, len(grid)):
            first = first & (pl.program_id(d) == 0)
            last = last & (pl.program_id(d) == grid[d] - 1)

        @pl.when(first)
        def _():
            carry.start(cin, cout, csems)

        body(*ins, *outs, *scs)

        @pl.when(last)
        def _():
            carry.finish(cin, cout, csems)

    res = pl.pallas_call(
        wrapped, grid=grid, in_specs=list(in_specs) + [HBM_SPEC] * c_in, out_specs=ospecs + [HBM_SPEC] * c_out,
        out_shape=shapes + list(carry.out_shapes), scratch_shapes=list(scratch_shapes) + list(carry.sems),
        compiler_params=_cp(("arbitrary",) * len(grid)), name=name)(*ops, *carry.inputs)
    main = res[:n_out]
    return (list(main) if multi else main[0]), list(res[n_out:])


def norm_matmul(x, xcol, kdim, gain, w, tm, tn, out_dtype, name, epi=None, epi_ops=(), epi_specs=(), carry=None):
    rows, n = x.shape[0], mat_cols(w) if w.ndim == 3 else w.shape[1]
    n_epi = len(epi_ops)
    w_spec = (pl.BlockSpec((kdim, tn), lambda i, j: (0, j)) if w.ndim == 2 else
              pl.BlockSpec((None, kdim, tn), lambda i, j: (j // (w.shape[2] // tn), 0, j % (w.shape[2] // tn))))

    def body(x_ref, g_ref, w_ref, *rest):
        epi_refs = rest[:n_epi]
        out_ref, xn_ref, xn_sc = rest[n_epi:]

        @pl.when(pl.program_id(1) == 0)
        def _():
            xv = x_ref[...]
            y = xv * lax.rsqrt(jnp.mean(xv * xv, axis=-1, keepdims=True) + EPS)
            xn = (y * g_ref[...]).astype(MXU_DT)
            xn_sc[...] = xn
            xn_ref[...] = xn

        acc = jnp.dot(xn_sc[...], w_ref[...], preferred_element_type=F32)
        if epi is not None:
            acc = epi(acc, *[r[...] for r in epi_refs])
        out_ref[...] = acc.astype(out_dtype)

    return _call(
        body, (x, gain, w, *epi_ops), carry, grid=(rows // tm, n // tn),
        in_specs=[pl.BlockSpec((tm, kdim), lambda i, j: (i, xcol)), pl.BlockSpec((1, kdim), lambda i, j: (0, 0)),
                  w_spec, *epi_specs],
        out_specs=[pl.BlockSpec((tm, tn), lambda i, j: (i, j)), pl.BlockSpec((tm, kdim), lambda i, j: (i, 0))],
        out_shape=[S((rows, n), out_dtype), S((rows, kdim), MXU_DT)],
        scratch_shapes=[pltpu.VMEM((tm, kdim), MXU_DT)], sem=("parallel", "arbitrary"), name=name)


def matmul_res(a, w, res, tm, tn, name):
    grp, rows, k = a.shape
    n = w.shape[2]

    def body(a_ref, w_ref, r_ref, o_ref):
        acc = r_ref[...]
        for g in range(grp):
            acc = acc + jnp.dot(a_ref[g], w_ref[g], preferred_element_type=F32)
        o_ref[...] = acc

    return pl.pallas_call(
        body, grid=(rows // tm, n // tn),
        in_specs=[pl.BlockSpec((grp, tm, k), lambda i, j: (0, i, 0)), pl.BlockSpec((grp, k, tn), lambda i, j: (0, 0, j)),
                  pl.BlockSpec((tm, tn), lambda i, j: (i, j))],
        out_specs=pl.BlockSpec((tm, tn), lambda i, j: (i, j)),
        out_shape=S((rows, n), F32), compiler_params=_cp(("parallel", "parallel")), name=name)(a, w, res)


def matmul_nt(a, w, tm, tn, out_dtype, name):
    rows, k = a.shape
    n = w.shape[0]

    def body(a_ref, w_ref, o_ref):
        o_ref[...] = lax.dot_general(a_ref[...].astype(MXU_DT), w_ref[...], NT_DIMS,
                                     preferred_element_type=F32).astype(out_dtype)

    return pl.pallas_call(
        body, grid=(rows // tm, n // tn),
        in_specs=[pl.BlockSpec((tm, k), lambda i, j: (i, 0)), pl.BlockSpec((tn, k), lambda i, j: (j, 0))],
        out_specs=pl.BlockSpec((tm, tn), lambda i, j: (i, j)),
        out_shape=S((rows, n), out_dtype), compiler_params=_cp(("parallel", "parallel")), name=name)(a, w)


def matmul_nt_normbwd(du, w, x, xcol, gain, res, tm, tk, out_dtype, name):
    rows, kc = du.shape[-2], mat_cols(du)
    dn = w.shape[-2]
    nk = kc // tk
    has_res = res is not None
    w_spec = (pl.BlockSpec((dn, tk), lambda i, k: (0, k)) if w.ndim == 2 else
              pl.BlockSpec((None, dn, tk), lambda i, k: (k // (w.shape[2] // tk), 0, k % (w.shape[2] // tk))))

    def body(du_ref, w_ref, x_ref, g_ref, *rest):
        if has_res:
            res_ref, dx_ref, dg_ref, acc = rest
        else:
            dx_ref, dg_ref, acc = rest
        i, k = pl.program_id(0), pl.program_id(1)

        @pl.when(k == 0)
        def _():
            acc[...] = jnp.zeros_like(acc)

        @pl.when((i == 0) & (k == 0))
        def _():
            dg_ref[...] = jnp.zeros_like(dg_ref)

        acc[...] += lax.dot_general(du_ref[...], w_ref[...], NT_DIMS, preferred_element_type=F32)

        @pl.when(k == nk - 1)
        def _():
            dhn = acc[...]
            xv = x_ref[...]
            rstd = lax.rsqrt(jnp.mean(xv * xv, axis=-1, keepdims=True) + EPS)
            xhat = xv * rstd
            dg_ref[...] += jnp.sum(dhn * xhat, axis=0, keepdims=True)
            dxh = dhn * g_ref[...]
            dx = rstd * (dxh - xhat * jnp.mean(dxh * xhat, axis=-1, keepdims=True))
            if has_res:
                dx = dx + res_ref[...]
            dx_ref[...] = dx.astype(out_dtype)

    in_specs = [mat_spec(du, tm, tk, lambda i, k: (i, k)), w_spec,
                pl.BlockSpec((tm, dn), lambda i, k: (i, xcol)), pl.BlockSpec((1, dn), lambda i, k: (0, 0))]
    ops = [du, w, x, gain]
    if has_res:
        in_specs.append(pl.BlockSpec((tm, dn), lambda i, k: (i, 0)))
        ops.append(res)
    return pl.pallas_call(
        body, grid=(rows // tm, nk), in_specs=in_specs,
        out_specs=[pl.BlockSpec((tm, dn), lambda i, k: (i, 0)), pl.BlockSpec((1, dn), lambda i, k: (0, 0))],
        out_shape=[S((rows, dn), out_dtype), S((1, dn), F32)],
        scratch_shapes=[pltpu.VMEM((tm, dn), F32)],
        compiler_params=_cp(("arbitrary", "arbitrary")), name=name)(*ops)


def matmul_tn(a, b, tr, name, carry=None, col_shards=1):
    rows, ka, nb = a.shape[-2], mat_cols(a), mat_cols(b)
    ta = _div_tile(mat_width(a), 1536, LANE)
    tb = _div_tile(mat_width(b), 1536 if ta <= 1024 else 1024, LANE)
    nr = rows // tr
    if col_shards == 1:
        out_spec, out_shape = pl.BlockSpec((ta, tb), lambda i, j, r: (i, j)), S((ka, nb), F32)
    else:
        per = nb // col_shards // tb
        assert per * tb * col_shards == nb
        out_spec = pl.BlockSpec((None, ta, tb), lambda i, j, r: (j // per, i, j % per))
        out_shape = S((col_shards, ka, nb // col_shards), F32)

    def body(a_ref, b_ref, o_ref, acc):
        r = pl.program_id(2)

        @pl.when(r == 0)
        def _():
            acc[...] = jnp.zeros_like(acc)

        acc[...] += lax.dot_general(a_ref[...].astype(MXU_DT), b_ref[...].astype(MXU_DT), TN_DIMS,
                                    preferred_element_type=F32)

        @pl.when(r == nr - 1)
        def _():
            o_ref[...] = acc[...]

    return _call(
        body, (a, b), carry, grid=(ka // ta, nb // tb, nr),
        in_specs=[mat_spec(a, tr, ta, lambda i, j, r: (r, i)), mat_spec(b, tr, tb, lambda i, j, r: (r, j))],
        out_specs=out_spec, out_shape=out_shape, scratch_shapes=[pltpu.VMEM((ta, tb), F32)],
        sem=("parallel", "parallel", "arbitrary"), name=name)


def _sigmoid(x):
    return 1.0 / (1.0 + jnp.exp(-x))


def _log1p(e):
    return jnp.where(e < 1e-3, e * (1.0 - e * (0.5 - e * (1.0 / 3.0 - 0.25 * e))), jnp.log(1.0 + e))


def _softplus(x):
    return jnp.maximum(x, 0.0) + _log1p(jnp.exp(-jnp.abs(x)))


def _expm1(x):
    series = x * (1.0 + x * (0.5 + x * (1.0 / 6.0 + x * (1.0 / 24.0 + x * (1.0 / 120.0)))))
    return jnp.where(jnp.abs(x) < 0.1, series, jnp.exp(x) - 1.0)


_GELU_K = math.sqrt(2.0 / math.pi)
_GELU_C = 0.044715


def _gelu_and_grad(x):
    th = jnp.tanh(_GELU_K * (x + _GELU_C * x * x * x))
    g = 0.5 * x * (1.0 + th)
    dg = 0.5 * (1.0 + th) + 0.5 * x * (1.0 - th * th) * _GELU_K * (1.0 + 3.0 * _GELU_C * x * x)
    return g, dg


def _row_iota(shape):
    return lax.broadcasted_iota(jnp.int32, shape, 0)


def _scan_chunk_fwd(a_sc, u_sc, out_ref, hcar, n, width):
    rowi = _row_iota((SUBLANE, width))

    def step(c, hprev):
        r0 = pl.multiple_of(c * SUBLANE, SUBLANE)
        a = a_sc[pl.ds(r0, SUBLANE), :]
        u = u_sc[pl.ds(r0, SUBLANE), :]
        for d in (1, 2, 4):
            a_s = jnp.where(rowi >= d, pltpu.roll(a, d, axis=0), 1.0)
            u_s = jnp.where(rowi >= d, pltpu.roll(u, d, axis=0), 0.0)
            u = u + a * u_s
            a = a * a_s
        h = u + a * hprev
        out_ref[pl.ds(r0, SUBLANE), :] = h
        return jnp.broadcast_to(h[SUBLANE - 1:SUBLANE, :], (SUBLANE, width))

    hcar[...] = lax.fori_loop(0, n // SUBLANE, step, hcar[...], unroll=4)


def _scan_chunk_bwd(b_sc, d_sc, out_ref, gcar, n, width):
    rowi = _row_iota((SUBLANE, width))
    nc = n // SUBLANE

    def step(c, gnext):
        r0 = pl.multiple_of((nc - 1 - c) * SUBLANE, SUBLANE)
        b = b_sc[pl.ds(r0, SUBLANE), :]
        d = d_sc[pl.ds(r0, SUBLANE), :]
        for s in (1, 2, 4):
            keep = rowi < SUBLANE - s
            b_s = jnp.where(keep, pltpu.roll(b, SUBLANE - s, axis=0), 1.0)
            d_s = jnp.where(keep, pltpu.roll(d, SUBLANE - s, axis=0), 0.0)
            d = d + b * d_s
            b = b * b_s
        g = d + b * gnext
        out_ref[pl.ds(r0, SUBLANE), :] = g
        return jnp.broadcast_to(g[0:1, :], (SUBLANE, width))

    gcar[...] = lax.fori_loop(0, nc, step, gcar[...], unroll=4)


def even_mid_fwd(u, conv_a, conv_b, conv_b_bias, rw, rb, iw, ib, lam, nb, tp, n, name, carry=None):
    rows = u.shape[0]
    w = LANE
    nj = CONV_W // w
    nt = tp // n
    h8 = SUBLANE

    def body(gb_r, gc_r, xa_r, xb_r, gate_r, ca_w, cb_w, cb_b, rw_r, rb_r, iw_r, ib_r, lam_r,
             y_o, ca_o, xc_o, a_o, hs_o, pext, xext, hcar, a_sc, u_sc):
        t = pl.program_id(2)

        @pl.when(t == 0)
        def _():
            pext[0:h8, :] = jnp.zeros((h8, w), F32)
            xext[0:h8, :] = jnp.zeros((h8, w), F32)
            hcar[...] = jnp.zeros_like(hcar)

        p = gc_r[...] * xa_r[...]
        pext[h8:h8 + n, :] = p
        wa = ca_w[...]
        ca = wa[2:3, :] * p + wa[1:2, :] * pext[h8 - 1:h8 - 1 + n, :] + wa[0:1, :] * pext[h8 - 2:h8 - 2 + n, :]
        ca_o[...] = ca
        y_o[0] = (gb_r[...] * ca).astype(MXU_DT)
        pext[0:h8, :] = pext[n:n + h8, :]

        xb = xb_r[...]
        xext[h8:h8 + n, :] = xb
        wb = cb_w[...]
        xc = (wb[3:4, :] * xb + wb[2:3, :] * xext[h8 - 1:h8 - 1 + n, :] + wb[1:2, :] * xext[h8 - 2:h8 - 2 + n, :]
              + wb[0:1, :] * xext[h8 - 3:h8 - 3 + n, :]) + cb_b[...]
        xc_o[...] = xc
        xext[0:h8, :] = xext[n:n + h8, :]

        xcm = xc.astype(MXU_DT)
        r = _sigmoid(jnp.dot(xcm, rw_r[...], preferred_element_type=F32) + rb_r[...])
        ig = _sigmoid(jnp.dot(xcm, iw_r[...], preferred_element_type=F32) + ib_r[...])
        log_a = (-LRU_C) * r * _softplus(-lam_r[...])
        a = jnp.exp(log_a)
        mult = jnp.sqrt(-_expm1(2.0 * log_a))
        a_sc[...] = a
        a_o[...] = a
        u_sc[...] = mult * (ig * xc)
        _scan_chunk_fwd(a_sc, u_sc, hs_o, hcar, n, w)
        gel, _ = _gelu_and_grad(gate_r[...])
        y_o[1] = (gel * hs_o[...]).astype(MXU_DT)

    def ublk(off):
        return pl.BlockSpec((n, w), lambda j, b, t: (b * nt + t, off + j))

    def pblk(r_):
        return pl.BlockSpec((r_, w), lambda j, b, t: (0, j))

    act = pl.BlockSpec((n, w), lambda j, b, t: (b * nt + t, j))
    mat = pl.BlockSpec((w, w), lambda j, b, t: (j, j))
    return _call(
        body, (u, u, u, u, u, conv_a, conv_b, conv_b_bias, rw, rb, iw, ib, lam), carry, grid=(nj, nb, nt),
        in_specs=[ublk(0), ublk(nj), ublk(2 * nj), ublk(3 * nj), ublk(4 * nj), pblk(3), pblk(4), pblk(1),
                  mat, pblk(1), mat, pblk(1), pblk(1)],
        out_specs=[pl.BlockSpec((2, n, w), lambda j, b, t: (0, b * nt + t, j)), act, act, act, act],
        out_shape=[S((2, rows, CONV_W), MXU_DT), S((rows, CONV_W), F32), S((rows, LRU_W), F32), S((rows, LRU_W), F32),
                   S((rows, LRU_W), F32)],
        scratch_shapes=[pltpu.VMEM((n + h8, w), F32), pltpu.VMEM((n + h8, w), F32), pltpu.VMEM((h8, w), F32),
                        pltpu.VMEM((n, w), F32), pltpu.VMEM((n, w), F32)],
        sem=("parallel", "parallel", "arbitrary"), name=name)


def even_mid_bwd(u, dycat, ca, xc, a_sv, hs, conv_a, conv_b, rw, rb, iw, ib, lam, nb, tp, n, name, carry=None):
    rows = u.shape[0]
    w = LANE
    nj = CONV_W // w
    nt = tp // n
    h8 = SUBLANE

    def body(gb_r, gc_r, xa_r, xb_r, gate_r, dya_r, dyb_r, ca_r, xc_r, a_r, hs_r, hsp_r,
             ca_w, cb_w, rw_r, rb_r, iw_r, ib_r, lam_r,
             du_o, dca_w, dcb_w, dcb_b, drw, drb, diw, dib, dlam,
             aext, hext, dext, eext, gcar, b_sc, d_sc, g_sc):
        b, t = pl.program_id(1), pl.program_id(2)

        @pl.when((b == 0) & (t == 0))
        def _():
            for ref in (dca_w, dcb_w, dcb_b, drw, drb, diw, dib, dlam):
                ref[...] = jnp.zeros_like(ref)

        @pl.when(t == 0)
        def _():
            aext[n:n + h8, :] = jnp.zeros((h8, w), F32)
            dext[n:n + h8, :] = jnp.zeros((h8, w), F32)
            eext[n:n + h8, :] = jnp.zeros((h8, w), F32)
            gcar[...] = jnp.zeros_like(gcar)

        xc_v = xc_r[...]
        xcm = xc_v.astype(MXU_DT)
        r = _sigmoid(jnp.dot(xcm, rw_r[...], preferred_element_type=F32) + rb_r[...])
        ig = _sigmoid(jnp.dot(xcm, iw_r[...], preferred_element_type=F32) + ib_r[...])
        lam_v = lam_r[...]
        sp = _softplus(-lam_v)
        log_a = (-LRU_C) * r * sp
        a = a_r[...]
        mult = jnp.sqrt(-_expm1(2.0 * log_a))
        hs_v = hs_r[...]
        gel, dgel = _gelu_and_grad(gate_r[...])
        dyb = dyb_r[...]
        du_o[4] = (dyb * hs_v * dgel).astype(MXU_DT)

        aext[0:n, :] = a
        b_sc[...] = aext[1:1 + n, :]
        d_sc[...] = dyb * gel
        _scan_chunk_bwd(b_sc, d_sc, g_sc, gcar, n, w)
        aext[n:n + h8, :] = aext[0:h8, :]
        g = g_sc[...]

        hext[0:h8, :] = jnp.where(t == nt - 1, 0.0, hsp_r[...])
        hext[h8:h8 + n, :] = hs_v
        da = g * hext[h8 - 1:h8 - 1 + n, :]
        dmult = g * (ig * xc_v)
        di = g * mult * xc_v
        dxc = g * mult * ig
        dlog_a = da * a - dmult * (a * a) / mult
        dr = dlog_a * ((-LRU_C) * sp)
        dsp = jnp.sum(dlog_a * ((-LRU_C) * r), axis=0, keepdims=True)
        dlam[...] += dsp * (-_sigmoid(-lam_v))
        dzr = dr * r * (1.0 - r)
        dzi = di * ig * (1.0 - ig)
        dzr_m = dzr.astype(MXU_DT)
        dzi_m = dzi.astype(MXU_DT)
        dxc = (dxc + lax.dot_general(dzr_m, rw_r[...], NT_DIMS, preferred_element_type=F32)
               + lax.dot_general(dzi_m, iw_r[...], NT_DIMS, preferred_element_type=F32))
        drw[...] += lax.dot_general(xcm, dzr_m, TN_DIMS, preferred_element_type=F32)
        diw[...] += lax.dot_general(xcm, dzi_m, TN_DIMS, preferred_element_type=F32)
        drb[...] += jnp.sum(dzr, axis=0, keepdims=True)
        dib[...] += jnp.sum(dzi, axis=0, keepdims=True)
        dcb_b[...] += jnp.sum(dxc, axis=0, keepdims=True)

        xb = xb_r[...]
        dext[0:n, :] = dxc
        wb = cb_w[...]
        d1, d2, d3 = dext[1:1 + n, :], dext[2:2 + n, :], dext[3:3 + n, :]
        du_o[3] = (wb[3:4, :] * dxc + wb[2:3, :] * d1 + wb[1:2, :] * d2 + wb[0:1, :] * d3).astype(MXU_DT)
        dcb_w[3:4, :] += jnp.sum(xb * dxc, axis=0, keepdims=True)
        dcb_w[2:3, :] += jnp.sum(xb * d1, axis=0, keepdims=True)
        dcb_w[1:2, :] += jnp.sum(xb * d2, axis=0, keepdims=True)
        dcb_w[0:1, :] += jnp.sum(xb * d3, axis=0, keepdims=True)
        dext[n:n + h8, :] = dext[0:h8, :]

        gb, gc, xa = gb_r[...], gc_r[...], xa_r[...]
        dya = dya_r[...]
        du_o[0] = (dya * ca_r[...]).astype(MXU_DT)
        dca = dya * gb
        eext[0:n, :] = dca
        wa = ca_w[...]
        e1, e2 = eext[1:1 + n, :], eext[2:2 + n, :]
        dp = wa[2:3, :] * dca + wa[1:2, :] * e1 + wa[0:1, :] * e2
        p = gc * xa
        dca_w[2:3, :] += jnp.sum(p * dca, axis=0, keepdims=True)
        dca_w[1:2, :] += jnp.sum(p * e1, axis=0, keepdims=True)
        dca_w[0:1, :] += jnp.sum(p * e2, axis=0, keepdims=True)
        eext[n:n + h8, :] = eext[0:h8, :]
        du_o[1] = (dp * xa).astype(MXU_DT)
        du_o[2] = (dp * gc).astype(MXU_DT)

    def rt(b, t):
        return b * nt + (nt - 1 - t)

    def ublk(off):
        return pl.BlockSpec((n, w), lambda j, b, t: (rt(b, t), off + j))

    def pblk(r_):
        return pl.BlockSpec((r_, w), lambda j, b, t: (0, j))

    act = pl.BlockSpec((n, w), lambda j, b, t: (rt(b, t), j))
    n8 = n // h8
    hsp = pl.BlockSpec((h8, w), lambda j, b, t: (jnp.maximum(rt(b, t) * n8 - 1, 0), j))
    mat = pl.BlockSpec((w, w), lambda j, b, t: (j, j))
    return _call(
        body, (u, u, u, u, u, dycat, dycat, ca, xc, a_sv, hs, hs, conv_a, conv_b, rw, rb, iw, ib, lam), carry,
        grid=(nj, nb, nt),
        in_specs=[ublk(0), ublk(nj), ublk(2 * nj), ublk(3 * nj), ublk(4 * nj), ublk(0), ublk(nj), act, act, act, act,
                  hsp, pblk(3), pblk(4), mat, pblk(1), mat, pblk(1), pblk(1)],
        out_specs=[pl.BlockSpec((5, n, w), lambda j, b, t: (0, rt(b, t), j)), pblk(3), pblk(4), pblk(1),
                   mat, pblk(1), mat, pblk(1), pblk(1)],
        out_shape=[S((5, rows, CONV_W), MXU_DT), S((3, CONV_W), F32), S((4, LRU_W), F32), S((1, LRU_W), F32),
                   S((LRU_W, LRU_W), F32), S((1, LRU_W), F32), S((LRU_W, LRU_W), F32), S((1, LRU_W), F32),
                   S((1, LRU_W), F32)],
        scratch_shapes=[pltpu.VMEM((n + h8, w), F32)] * 4 + [pltpu.VMEM((h8, w), F32)] + [pltpu.VMEM((n, w), F32)] * 3,
        sem=("arbitrary", "arbitrary", "arbitrary"), name=name)


def ffn_mid_fwd(up, cw, cb, nb, tp, n, name):
    rows = up.shape[0]
    w = FFN_CT
    nj = D_FF // w
    nt = tp // n
    h8 = SUBLANE

    sr = STRIP_ROWS

    def body(xa_r, xg_r, w_r, b_r, u_o, y_o, halo):
        t = pl.program_id(2)

        @pl.when(t == 0)
        def _():
            halo[...] = jnp.zeros_like(halo)

        wv = (w_r[0], w_r[1])
        bv = (b_r[0], b_r[1])

        def strip(s, carry):
            r0 = pl.multiple_of(s * sr, sr)
            us, new = [], []
            for g, x_r in enumerate((xa_r, xg_r)):
                x = x_r[pl.ds(r0, sr), :].astype(F32)
                win = jnp.concatenate([carry[g], x], axis=0)
                x1 = pltpu.roll(win, 1, axis=0)[h8:, :]
                x2 = pltpu.roll(win, 2, axis=0)[h8:, :]
                u = (wv[g][2:3, :] * x + wv[g][1:2, :] * x1 + wv[g][0:1, :] * x2) + bv[g]
                u_o[g, pl.ds(r0, sr), :] = u.astype(MXU_DT)
                us.append(u)
                new.append(x[sr - h8:, :])
            y_o[pl.ds(r0, sr), :] = (us[0] * _sigmoid(us[0]) * us[1]).astype(MXU_DT)
            return tuple(new)

        ha, hg = lax.fori_loop(0, n // sr, strip, (halo[0], halo[1]))
        halo[0] = ha
        halo[1] = hg

    def ublk(off):
        return pl.BlockSpec((n, w), lambda j, b, t: (b * nt + t, off + j))

    return pl.pallas_call(
        body, grid=(nj, nb, nt),
        in_specs=[ublk(0), ublk(nj), pl.BlockSpec((2, 3, w), lambda j, b, t: (0, 0, j)),
                  pl.BlockSpec((2, 1, w), lambda j, b, t: (0, 0, j))],
        out_specs=[pl.BlockSpec((2, n, w), lambda j, b, t: (0, b * nt + t, j)), ublk(0)],
        out_shape=[S((2, rows, D_FF), MXU_DT), S((rows, D_FF), MXU_DT)],
        scratch_shapes=[pltpu.VMEM((2, h8, w), F32)],
        compiler_params=_cp(("parallel", "parallel", "arbitrary")), name=name,
    )(up, up, cw, cb)


def ffn_mid_bwd(dy, u, up, cw, nb, tp, n, name, carry=None):
    rows = up.shape[0]
    w = FFN_CT
    nj = D_FF // w
    nt = tp // n
    h8 = SUBLANE

    sr = STRIP_ROWS
    ns = n // sr

    def fold(v):
        acc = v[0:h8, :]
        for k in range(1, sr // h8):
            acc = acc + v[k * h8:(k + 1) * h8, :]
        return acc

    def body(dy_r, u_r, xa_r, xg_r, w_r, dx_o, dw, db, halo):
        b, t = pl.program_id(1), pl.program_id(2)

        @pl.when((b == 0) & (t == 0))
        def _():
            dw[...] = jnp.zeros_like(dw)
            db[...] = jnp.zeros_like(db)

        @pl.when(t == 0)
        def _():
            halo[...] = jnp.zeros_like(halo)

        wv = (w_r[0], w_r[1])

        def strip(s, carry):
            halos, sums = carry
            r0 = pl.multiple_of((ns - 1 - s) * sr, sr)
            dyv = dy_r[pl.ds(r0, sr), :].astype(F32)
            ua = u_r[0, pl.ds(r0, sr), :].astype(F32)
            ug = u_r[1, pl.ds(r0, sr), :].astype(F32)
            sg = _sigmoid(ua)
            dus = (dyv * ug * (sg * (1.0 + ua * (1.0 - sg))), dyv * (ua * sg))
            new_halos, new_sums = [], []
            for g, x_r in enumerate((xa_r, xg_r)):
                du = dus[g]
                win = jnp.concatenate([du, halos[g]], axis=0)
                d1 = pltpu.roll(win, sr + h8 - 1, axis=0)[0:sr, :]
                d2 = pltpu.roll(win, sr + h8 - 2, axis=0)[0:sr, :]
                dx_o[g, pl.ds(r0, sr), :] = (wv[g][2:3, :] * du + wv[g][1:2, :] * d1 + wv[g][0:1, :] * d2).astype(MXU_DT)
                x = x_r[pl.ds(r0, sr), :].astype(F32)
                s2, s1, s0, sb = sums[g]
                new_sums.append((s2 + fold(x * du), s1 + fold(x * d1), s0 + fold(x * d2), sb + fold(du)))
                new_halos.append(du[0:h8, :])
            return tuple(new_halos), tuple(new_sums)

        z = jnp.zeros((h8, w), F32)
        halos, sums = lax.fori_loop(0, ns, strip, ((halo[0], halo[1]), ((z, z, z, z), (z, z, z, z))))
        halo[0] = halos[0]
        halo[1] = halos[1]
        for g in range(2):
            s2, s1, s0, sb = sums[g]
            dw[g, 2:3, :] += jnp.sum(s2, axis=0, keepdims=True)
            dw[g, 1:2, :] += jnp.sum(s1, axis=0, keepdims=True)
            dw[g, 0:1, :] += jnp.sum(s0, axis=0, keepdims=True)
            db[g] += jnp.sum(sb, axis=0, keepdims=True)

    def rt(b, t):
        return b * nt + (nt - 1 - t)

    def ublk(off):
        return pl.BlockSpec((n, w), lambda j, b, t: (rt(b, t), off + j))

    pair = pl.BlockSpec((2, n, w), lambda j, b, t: (0, rt(b, t), j))
    return _call(
        body, (dy, u, up, up, cw), carry, grid=(nj, nb, nt),
        in_specs=[ublk(0), pair, ublk(0), ublk(nj), pl.BlockSpec((2, 3, w), lambda j, b, t: (0, 0, j))],
        out_specs=[pair, pl.BlockSpec((2, 3, w), lambda j, b, t: (0, 0, j)),
                   pl.BlockSpec((2, 1, w), lambda j, b, t: (0, 0, j))],
        out_shape=[S((2, rows, D_FF), MXU_DT), S((2, 3, D_FF), F32), S((2, 1, D_FF), F32)],
        scratch_shapes=[pltpu.VMEM((2, h8, w), F32)],
        sem=("arbitrary", "arbitrary", "arbitrary"), name=name)


def _lane_mod(shape):
    return lax.broadcasted_iota(jnp.int32, shape, 1) & (HP - 1)


def _q_rope_epi(acc, tab):
    reps = acc.shape[1] // HP
    a = acc * jnp.tile(tab, (1, reps))
    lane = _lane_mod(a.shape)
    shifted = pltpu.roll(a, a.shape[1] - QK_ROPE, axis=1)
    return jnp.where(lane < QK_NOPE, a, jnp.where(lane < QK_HEAD, a + shifted, 0.0)) * Q_PRESCALE


def _k_rope_block(krblk, tabk):
    a = krblk * tabk
    lane = _lane_mod(a.shape)
    b = a + pltpu.roll(a, HP - QK_ROPE, axis=1)
    return jnp.where((lane >= QK_NOPE) & (lane < QK_HEAD), b, 0.0)


def _k_rope_epi(acc, krblk, tabk):
    reps = acc.shape[1] // HP
    return acc + jnp.tile(_k_rope_block(krblk, tabk), (1, reps))


def attn_fwd(q, k, v, nb, tp, name, carry=None):
    rows = q.shape[0]
    blk = ATT_BLK
    nq = tp // blk
    npair = MLA_HEADS // 2

    def body(q_r, k_r, v_r, o_r, lse_r):
        qi = pl.program_id(2)
        lane = lax.broadcasted_iota(jnp.int32, (blk, LANE), 1)
        even = lane < V_HEAD
        sum_lane = (V_HEAD, 0)
        rowi = lax.broadcasted_iota(jnp.int32, (blk, blk), 0)
        coli = lax.broadcasted_iota(jnp.int32, (blk, blk), 1)
        qs = [q_r[:, h * HP:(h + 1) * HP] for h in range(2)]

        def kv_block(k0, width, carry, diagonal):
            ms, accs = carry
            vblk = v_r[pl.ds(k0, width), :]
            one = jnp.ones_like(vblk)
            zero = jnp.zeros_like(vblk)
            vlane = lax.broadcasted_iota(jnp.int32, (width, LANE), 1)
            ss = [lax.dot_general(qs[h], k_r[pl.ds(k0, width), h * HP:(h + 1) * HP], NT_DIMS,
                                  preferred_element_type=F32) for h in range(2)]
            new_ms, new_accs = [], []
            for h in range(2):
                s = ss[h]
                if diagonal:
                    s = jnp.where(coli <= rowi, s, -jnp.inf)
                m_new = jnp.maximum(ms[h], jnp.max(s, axis=1, keepdims=True))
                alpha = jnp.exp2(ms[h] - m_new)
                p = jnp.exp2(s - m_new).astype(MXU_DT)
                mine = (vlane < V_HEAD) if h == 0 else (vlane >= V_HEAD)
                vh = jnp.where(mine, vblk, jnp.where(vlane == sum_lane[h], one, zero))
                new_accs.append(alpha * accs[h] + jnp.dot(p, vh, preferred_element_type=F32))
                new_ms.append(m_new)
            return tuple(new_ms), tuple(new_accs)

        neg = jnp.full((blk, 1), -jnp.inf, F32)
        zacc = jnp.zeros((blk, LANE), F32)
        carry = lax.fori_loop(0, qi // 2, lambda i, c: kv_block(pl.multiple_of(i * 2 * blk, blk), 2 * blk, c, False),
                              ((neg, neg), (zacc, zacc)))
        carry = lax.cond(qi % 2 == 1, lambda c: kv_block(pl.multiple_of((qi - 1) * blk, blk), blk, c, False),
                         lambda c: c, carry)
        ms, accs = kv_block(pl.multiple_of(qi * blk, blk), blk, carry, True)
        ls = [accs[h][:, sum_lane[h]:sum_lane[h] + 1] for h in range(2)]
        o_r[...] = jnp.where(even, accs[0] / ls[0], accs[1] / ls[1]).astype(MXU_DT)
        lse_r[...] = jnp.where(even, ms[0] + jnp.log2(ls[0]), ms[1] + jnp.log2(ls[1]))

    return _call(
        body, (q, k, v), carry, grid=(nb, npair, nq),
        in_specs=[pl.BlockSpec((blk, 2 * HP), lambda b, p, i: (b * nq + i, p)),
                  pl.BlockSpec((tp, 2 * HP), lambda b, p, i: (b, p)),
                  pl.BlockSpec((tp, LANE), lambda b, p, i: (b, p))],
        out_specs=[pl.BlockSpec((blk, LANE), lambda b, p, i: (b * nq + i, p)),
                   pl.BlockSpec((None, blk, LANE), lambda b, p, i: (p, b * nq + i, 0))],
        out_shape=[S((rows, MLA_HEADS * V_HEAD), MXU_DT), S((npair, rows, LANE), F32)], scratch_shapes=[],
        sem=("parallel", "parallel", "arbitrary"), name=name)


def attn_bwd(q, k, v, o, do, lse, nb, tp, name, carry=None):
    rows = q.shape[0]
    blk = ATT_BLK
    nq = tp // blk
    npair = MLA_HEADS // 2
    scale = QK_HEAD ** -0.5

    def body(q_r, k_r, v_r, o_r, do_r, lse_r, dq_o, dk_o, dv_o, dq_acc, delta_sc):
        kb = pl.program_id(2)
        even = lax.broadcasted_iota(jnp.int32, (blk, LANE), 1) < V_HEAD
        rowi = lax.broadcasted_iota(jnp.int32, (blk, blk), 0)
        coli = lax.broadcasted_iota(jnp.int32, (blk, blk), 1)

        @pl.when(kb == 0)
        def _():
            dq_acc[...] = jnp.zeros_like(dq_acc)

            def dstep(i, c):
                r0 = pl.multiple_of(i * blk, blk)
                prod = do_r[pl.ds(r0, blk), :].astype(F32) * o_r[pl.ds(r0, blk), :].astype(F32)
                de = jnp.sum(jnp.where(even, prod, 0.0), axis=1, keepdims=True)
                dd = jnp.sum(jnp.where(even, 0.0, prod), axis=1, keepdims=True)
                delta_sc[pl.ds(r0, blk), :] = jnp.where(even, de, dd)
                return c

            lax.fori_loop(0, nq, dstep, 0)

        vblk = v_r[...]
        ks = [k_r[:, h * HP:(h + 1) * HP] for h in range(2)]

        def q_block(r0, height, carry, diagonal):
            dk0, dk1, dv = carry
            dob = do_r[pl.ds(r0, height), :]
            lse_b = lse_r[pl.ds(r0, height), :]
            dl_b = delta_sc[pl.ds(r0, height), :]
            qlane = lax.broadcasted_iota(jnp.int32, (height, LANE), 1)
            dks = [dk0, dk1]
            qhs = [q_r[pl.ds(r0, height), h * HP:(h + 1) * HP] for h in range(2)]
            dohs = [jnp.where((qlane < V_HEAD) if h == 0 else (qlane >= V_HEAD), dob, jnp.zeros_like(dob))
                    for h in range(2)]
            ss = [lax.dot_general(qhs[h], ks[h], NT_DIMS, preferred_element_type=F32) for h in range(2)]
            dps = [lax.dot_general(dohs[h], vblk, NT_DIMS, preferred_element_type=F32) for h in range(2)]
            for h in range(2):
                lo = 0 if h == 0 else V_HEAD
                p = jnp.exp2(ss[h] - lse_b[:, lo:lo + 1])
                if diagonal:
                    p = jnp.where(coli <= rowi, p, 0.0)
                ds = (p * (dps[h] - dl_b[:, lo:lo + 1])).astype(MXU_DT)
                dv = dv + lax.dot_general(p.astype(MXU_DT), dohs[h], TN_DIMS, preferred_element_type=F32)
                dks[h] = dks[h] + lax.dot_general(ds, qhs[h], TN_DIMS, preferred_element_type=F32)
                dq_acc[pl.ds(r0, height), h * HP:(h + 1) * HP] += jnp.dot(ds, ks[h], preferred_element_type=F32)
            return dks[0], dks[1], dv

        z = jnp.zeros((blk, HP), F32)
        carry = q_block(pl.multiple_of(kb * blk, blk), blk, (z, z, jnp.zeros((blk, LANE), F32)), True)
        below = nq - 1 - kb
        carry = lax.fori_loop(
            0, below // 2, lambda i, c: q_block(pl.multiple_of((kb + 1 + 2 * i) * blk, blk), 2 * blk, c, False), carry)
        dk0, dk1, dv = lax.cond(below % 2 == 1, lambda c: q_block(pl.multiple_of((nq - 1) * blk, blk), blk, c, False),
                                lambda c: c, carry)
        dk_o[:, 0:HP] = (dk0 * (scale / Q_PRESCALE)).astype(MXU_DT)
        dk_o[:, HP:2 * HP] = (dk1 * (scale / Q_PRESCALE)).astype(MXU_DT)
        dv_o[...] = dv.astype(MXU_DT)

        @pl.when(kb == nq - 1)
        def _():
            dq_o[...] = (dq_acc[...] * scale).astype(MXU_DT)

    seq_pair = pl.BlockSpec((tp, LANE), lambda b, p, kk: (b, p))
    return _call(
        body, (q, k, v, o, do, lse), carry, grid=(nb, npair, nq),
        in_specs=[pl.BlockSpec((tp, 2 * HP), lambda b, p, kk: (b, p)),
                  pl.BlockSpec((blk, 2 * HP), lambda b, p, kk: (b * nq + kk, p)),
                  pl.BlockSpec((blk, LANE), lambda b, p, kk: (b * nq + kk, p)),
                  seq_pair, seq_pair, pl.BlockSpec((None, tp, LANE), lambda b, p, kk: (p, b, 0))],
        out_specs=[pl.BlockSpec((tp, 2 * HP), lambda b, p, kk: (b, p)),
                   pl.BlockSpec((blk, 2 * HP), lambda b, p, kk: (b * nq + kk, p)),
                   pl.BlockSpec((blk, LANE), lambda b, p, kk: (b * nq + kk, p))],
        out_shape=[S((rows, MLA_HEADS * HP), MXU_DT), S((rows, MLA_HEADS * HP), MXU_DT),
                   S((rows, MLA_HEADS * V_HEAD), MXU_DT)],
        scratch_shapes=[pltpu.VMEM((tp, 2 * HP), F32), pltpu.VMEM((tp, LANE), F32)],
        sem=("parallel", "parallel", "arbitrary"), name=name)


def rope_bwd(dq, dk, dv, tabq, tabk, tp, tm, name):
    rows = dq.shape[0]
    nt = tp // tm
    wq = MLA_HEADS * HP

    def body(dq_r, dk_r, dv_r, tq_r, tk_r, dqa_o, dkv_o, dkr_o):
        dqv = dq_r[...].astype(F32)
        lane = _lane_mod(dqv.shape)
        in_rope = (lane >= QK_NOPE) & (lane < QK_HEAD)
        rope = jnp.where(in_rope, dqv, 0.0)
        da = jnp.where(lane < QK_HEAD, dqv, 0.0) + pltpu.roll(rope, QK_ROPE, axis=1)
        dqa_o[...] = (da * jnp.tile(tq_r[...], (1, MLA_HEADS))).astype(MXU_DT)
        dkf = dk_r[...].astype(F32)
        dkv_o[:, 0:wq] = jnp.where(lane < QK_NOPE, dkf, 0.0).astype(MXU_DT)
        dkv_o[:, wq:] = dv_r[...]
        kr = jnp.where(in_rope, dkf, 0.0)
        tot = kr[:, 0:HP]
        for h in range(1, MLA_HEADS):
            tot = tot + kr[:, h * HP:(h + 1) * HP]
        dkr_o[...] = ((tot + pltpu.roll(tot, QK_ROPE, axis=1)) * tk_r[...]).astype(MXU_DT)

    def rowblk(wd):
        return pl.BlockSpec((tm, wd), lambda i: (i, 0))

    tab = pl.BlockSpec((tm, HP), lambda i: (i % nt, 0))
    return pl.pallas_call(
        body, grid=(rows // tm,), in_specs=[rowblk(wq), rowblk(wq), rowblk(MLA_HEADS * V_HEAD), tab, tab],
        out_specs=[rowblk(wq), rowblk(wq + MLA_HEADS * V_HEAD), rowblk(HP)],
        out_shape=[S((rows, wq), MXU_DT), S((rows, wq + MLA_HEADS * V_HEAD), MXU_DT), S((rows, HP), MXU_DT)],
        compiler_params=_cp(("parallel",)), name=name)(dq, dk, dv, tabq, tabk)


def loss_head(h, target, gain, tp, t_real, tm, name):
    rows = h.shape[0]
    nt = tp // tm

    def body(h_r, t_r, g_r, dh_o, loss_o, dg_o):
        i = pl.program_id(0)

        @pl.when(i == 0)
        def _():
            loss_o[...] = jnp.zeros_like(loss_o)
            dg_o[...] = jnp.zeros_like(dg_o)

        xv = h_r[...]
        rstd = lax.rsqrt(jnp.mean(xv * xv, axis=-1, keepdims=True) + EPS)
        xhat = xv * rstd
        g = g_r[...]
        pos = (i % nt) * tm + lax.broadcasted_iota(jnp.int32, (tm, 1), 0)
        valid = (pos >= N_META) & (pos < t_real)
        err = jnp.where(valid, xhat * g - t_r[...], 0.0)
        loss_o[...] += 0.5 * jnp.sum(jnp.mean(err * err, axis=-1, keepdims=True))
        dy = err * (1.0 / D_MODEL)
        dg_o[...] += jnp.sum(dy * xhat, axis=0, keepdims=True)
        dxh = dy * g
        dh_o[...] = rstd * (dxh - xhat * jnp.mean(dxh * xhat, axis=-1, keepdims=True))

    blk = pl.BlockSpec((tm, D_MODEL), lambda i: (i, 0))
    return pl.pallas_call(
        body, grid=(rows // tm,), in_specs=[blk, blk, pl.BlockSpec((1, D_MODEL), lambda i: (0, 0))],
        out_specs=[blk, pl.BlockSpec((1, LANE), lambda i: (0, 0)), pl.BlockSpec((1, D_MODEL), lambda i: (0, 0))],
        out_shape=[S((rows, D_MODEL), F32), S((1, LANE), F32), S((1, D_MODEL), F32)],
        compiler_params=_cp(("arbitrary",)), name=name)(h, target, gain)


ADAM_TILE_ELEMS = 128 * 1024


def adamw(g, w, m, v, name, carry=None):
    shape = w.shape
    cols = shape[-1]
    rws = max(1, math.prod(shape[:-1]))
    tr = rws if rws * cols <= ADAM_TILE_ELEMS else _div_tile(rws, max(SUBLANE, ADAM_TILE_ELEMS // cols), SUBLANE)
    bc1 = 1.0 - ADAM_B1 ** ADAM_STEP
    bc2 = 1.0 - ADAM_B2 ** ADAM_STEP

    def body(g_r, w_r, m_r, v_r, go, do, mo, vo):
        gv = g_r[...]
        mn = ADAM_B1 * m_r[...] + (1.0 - ADAM_B1) * gv
        vn = ADAM_B2 * v_r[...] + (1.0 - ADAM_B2) * (gv * gv)
        m_hat = mn / bc1
        v_hat = vn / bc2
        go[...] = gv
        do[...] = -ADAM_LR * (m_hat / (jnp.sqrt(v_hat) + ADAM_EPS) + ADAM_WD * w_r[...])
        mo[...] = mn
        vo[...] = vn

    blk = pl.BlockSpec((tr, cols), lambda i: (i, 0))
    outs, got = _call(
        body, [a.reshape(rws, cols) for a in (g, w, m, v)], carry, grid=(rws // tr,), in_specs=[blk] * 4,
        out_specs=[blk] * 4, out_shape=[S((rws, cols), F32)] * 4, scratch_shapes=[], sem=("parallel",), name=name)
    return tuple(o.reshape(shape) for o in outs), got


SUM_TILE_ELEMS = 128 * 1024


def _place():
    return lax.axis_index("x"), lax.axis_index("y"), lax.axis_index("c")


def _remote(src, dst, send_sems, recv_sems, k, to):
    return pltpu.make_async_remote_copy(src_ref=src, dst_ref=dst, send_sem=send_sems.at[k], recv_sem=recv_sems.at[k],
                                        device_id=to, device_id_type=MESH)


def chip_index():
    return 2 * lax.axis_index("x") + lax.axis_index("y")


def _sem_pair(n):
    return [pltpu.SemaphoreType.DMA((n,)), pltpu.SemaphoreType.DMA((n,))]


def stage_gather_chips(xs):
    def copies(ins, outs, sems):
        send_sems, recv_sems = sems
        mx, my, mc = _place()
        sibling = (mx, my, 1 - mc)
        chips = [(1 - mx, my), (mx, 1 - my), (1 - mx, 1 - my)]
        first, landed, passed, from_sibling = [], [], [], []
        for i, (x_ref, out_ref) in enumerate(zip(ins, outs)):
            def piece(cx, cy, h, out_ref=out_ref):
                return out_ref.at[2 * cx + cy, h]

            for j, (cx, cy) in enumerate(chips):
                k = 6 * i + j
                first.append(_remote(x_ref.at[mc], piece(mx, my, mc), send_sems, recv_sems, k, (cx, cy, mc)))
                landed.append(_remote(x_ref.at[mc], piece(cx, cy, mc), send_sems, recv_sems, k, (cx, cy, mc)))
                passed.append(_remote(piece(cx, cy, mc), piece(cx, cy, mc), send_sems, recv_sems, k + 3, sibling))
                from_sibling.append(_remote(x_ref.at[mc], piece(cx, cy, 1 - mc), send_sems, recv_sems, k + 3, sibling))
        return first, landed, passed, from_sibling

    def start(ins, outs, sems):
        for cp in copies(ins, outs, sems)[0]:
            cp.start()

    def finish(ins, outs, sems):
        first, landed, passed, from_sibling = copies(ins, outs, sems)
        for arrived, onward in zip(landed, passed):
            arrived.wait_recv()
            onward.start()
        for cp in from_sibling:
            cp.wait_recv()
        for cp in first + passed:
            cp.wait_send()

    return Stage(list(xs), [S((4,) + x.shape, x.dtype) for x in xs], _sem_pair(6 * len(xs)), start, finish)


def own_block(gathered, xs):
    return lax.dynamic_update_slice(gathered, xs[None], (chip_index(), 0, 0, 0))


def stage_pair_exchange(gs):
    def copies(ins, outs, sems):
        send_sems, recv_sems = sems
        mx, my, mc = _place()
        return [_remote(g_ref.at[s, 1 - mc], land_ref.at[s], send_sems, recv_sems, 4 * i + s, (mx, my, 1 - mc))
                for i, (g_ref, land_ref) in enumerate(zip(ins, outs)) for s in range(4)]

    def start(ins, outs, sems):
        for cp in copies(ins, outs, sems):
            cp.start()

    def finish(ins, outs, sems):
        cps = copies(ins, outs, sems)
        for cp in cps:
            cp.wait_recv()
        for cp in cps:
            cp.wait_send()

    return Stage(list(gs), [S((4,) + g.shape[2:], g.dtype) for g in gs], _sem_pair(4 * len(gs)), start, finish)


def _sum_rows(rws, width):
    return _div_tile(rws, max(SUBLANE, SUM_TILE_ELEMS // width), SUBLANE)


def pair_sum(g4, land, c_idx, name):
    _, _, rws, wd = g4.shape
    th = _sum_rows(rws, wd)

    def body(c_ref, a_ref, b_ref, o_ref):
        o_ref[...] = a_ref[...] + b_ref[...]

    return pl.pallas_call(
        body,
        grid_spec=pltpu.PrefetchScalarGridSpec(
            num_scalar_prefetch=1, grid=(4, rws // th),
            in_specs=[pl.BlockSpec((None, None, th, wd), lambda s, i, c: (s, c[0], i, 0)),
                      pl.BlockSpec((None, th, wd), lambda s, i, c: (s, i, 0))],
            out_specs=pl.BlockSpec((None, th, wd), lambda s, i, c: (s, i, 0))),
        out_shape=S((4, rws, wd), F32), compiler_params=_cp(("parallel", "parallel")), name=name)(c_idx, g4, land)


def stage_chip_scatter(ps):
    def copies(ins, outs, sems):
        send_sems, recv_sems = sems
        mx, my, mc = _place()
        me = 2 * mx + my
        chips = [(1 - mx, my), (mx, 1 - my), (1 - mx, 1 - my)]
        sent, landed = [], []
        for i, (p_ref, land_ref) in enumerate(zip(ins, outs)):
            for j, (cx, cy) in enumerate(chips):
                k = 3 * i + j
                sent.append(_remote(p_ref.at[2 * cx + cy], land_ref.at[me], send_sems, recv_sems, k, (cx, cy, mc)))
                landed.append(_remote(p_ref.at[me], land_ref.at[2 * cx + cy], send_sems, recv_sems, k, (cx, cy, mc)))
        return sent, landed

    def start(ins, outs, sems):
        for cp in copies(ins, outs, sems)[0]:
            cp.start()

    def finish(ins, outs, sems):
        sent, landed = copies(ins, outs, sems)
        for cp in landed:
            cp.wait_recv()
        for cp in sent:
            cp.wait_send()

    return Stage(list(ps), [S(p.shape, p.dtype) for p in ps], _sem_pair(3 * len(ps)), start, finish)


def chip_sum(l4, p4, me_idx, name):
    _, rws, wd = l4.shape
    th = _sum_rows(rws, wd)

    def body(me_ref, a, b, c, d, own, o_ref):
        me = me_ref[0]
        parts = [jnp.where(me == s, own[...], r[...]) for s, r in enumerate((a, b, c, d))]
        o_ref[...] = ((parts[0] + parts[1]) + parts[2]) + parts[3]

    def blk(s):
        return pl.BlockSpec((None, th, wd), lambda i, me: (jnp.where(me[0] == s, (s + 1) % 4, s), i, 0))

    return pl.pallas_call(
        body,
        grid_spec=pltpu.PrefetchScalarGridSpec(
            num_scalar_prefetch=1, grid=(rws // th,),
            in_specs=[blk(0), blk(1), blk(2), blk(3), pl.BlockSpec((None, th, wd), lambda i, me: (me[0], i, 0))],
            out_specs=pl.BlockSpec((th, wd), lambda i, me: (i, 0))),
        out_shape=S((rws, wd), F32), compiler_params=_cp(("parallel",)), name=name)(me_idx, l4, l4, l4, l4, p4)


def stage_pair_gather(rs):
    def copies(ins, outs, sems):
        send_sems, recv_sems = sems
        mx, my, mc = _place()
        return [_remote(r_ref, out_ref, send_sems, recv_sems, i, (mx, my, 1 - mc))
                for i, (r_ref, out_ref) in enumerate(zip(ins, outs))]

    def start(ins, outs, sems):
        for cp in copies(ins, outs, sems):
            cp.start()

    def finish(ins, outs, sems):
        for cp in copies(ins, outs, sems):
            cp.wait()

    return Stage(list(rs), [S(r.shape, r.dtype) for r in rs], _sem_pair(len(rs)), start, finish)


PACK_ELEMS = 16 * LANE


def pack_rows(arrays, lead, total_mult):
    parts, offs, r0 = [], [], 0
    for a in arrays:
        flat = a.reshape(a.shape[:lead] + (-1,))
        elems = _round_up(flat.shape[-1], PACK_ELEMS)
        flat = jnp.pad(flat, [(0, 0)] * lead + [(0, elems - flat.shape[-1])])
        parts.append(flat.reshape(flat.shape[:lead] + (elems // LANE, LANE)))
        offs.append((r0, elems // LANE))
        r0 += elems // LANE
    total = _round_up(r0, total_mult)
    if total > r0:
        parts.append(jnp.zeros(parts[0].shape[:lead] + (total - r0, LANE), parts[0].dtype))
    return jnp.concatenate(parts, axis=lead), offs


def unpack_rows(buf, off, shape):
    r0, nr = off
    lead = buf.shape[:-2]
    n = math.prod(shape)
    return buf[..., r0:r0 + nr, :].reshape(lead + (nr * LANE,))[..., :n].reshape(lead + tuple(shape))


def unshard(stacked, axis):
    x = jnp.moveaxis(stacked, 0, axis)
    return x.reshape(x.shape[:axis] + (4 * x.shape[axis + 1],) + x.shape[axis + 2:])


def to_shards(full, axis):
    n = full.shape[axis] // 4
    x = full.reshape(full.shape[:axis] + (4, n) + full.shape[axis + 1:])
    return jnp.moveaxis(x, axis, 0)


def _rot_cols(w):
    half = w.shape[-1] // 2
    return jnp.concatenate([-w[..., half:], w[..., :half]], axis=-1)


def _unrot_cols(dw):
    half = dw.shape[-1] // 2
    return jnp.concatenate([dw[..., half:], -dw[..., :half]], axis=-1)


def odd_w_in_padded(w_in):
    kr = w_in[:, Q_LORA + KV_LORA:]
    rows = w_in.shape[0]
    return jnp.concatenate([w_in[:, :Q_LORA], jnp.zeros((rows, 128), w_in.dtype), w_in[:, Q_LORA:Q_LORA + KV_LORA],
                            jnp.zeros((rows, 64), w_in.dtype), kr, _rot_cols(kr)], axis=1)


def odd_w_in_unpad(dwp):
    base = 512 + KV_LORA + 64
    dkr = dwp[:, base:base + QK_ROPE] + _unrot_cols(dwp[:, base + QK_ROPE:base + 2 * QK_ROPE])
    return jnp.concatenate([dwp[:, :Q_LORA], dwp[:, 512:512 + KV_LORA], dkr], axis=1)


def uq_padded(w_uq):
    w = w_uq.reshape(Q_LORA, MLA_HEADS, QK_HEAD)
    return jnp.concatenate([w, _rot_cols(w[:, :, QK_NOPE:])], axis=-1).reshape(Q_LORA, MLA_HEADS * HP)


def uq_unpad(dwp):
    d = dwp.reshape(Q_LORA, MLA_HEADS, HP)
    rope = d[:, :, QK_NOPE:QK_HEAD] + _unrot_cols(d[:, :, QK_HEAD:])
    return jnp.concatenate([d[:, :, :QK_NOPE], rope], axis=-1).reshape(Q_LORA, MLA_HEADS * QK_HEAD)


def ukv_padded(w_ukv):
    w = w_ukv.reshape(KV_LORA, MLA_HEADS, QK_NOPE + V_HEAD)
    wk = jnp.concatenate([w[:, :, :QK_NOPE], jnp.zeros((KV_LORA, MLA_HEADS, HP - QK_NOPE), w.dtype)], axis=-1)
    return jnp.concatenate([wk.reshape(KV_LORA, MLA_HEADS * HP), w[:, :, QK_NOPE:].reshape(KV_LORA, MLA_HEADS * V_HEAD)],
                           axis=1)


def ukv_unpad(dwp):
    dk = dwp[:, :MLA_HEADS * HP].reshape(KV_LORA, MLA_HEADS, HP)[:, :, :QK_NOPE]
    dv = dwp[:, MLA_HEADS * HP:].reshape(KV_LORA, MLA_HEADS, V_HEAD)
    return jnp.concatenate([dk, dv], axis=-1).reshape(KV_LORA, MLA_HEADS * (QK_NOPE + V_HEAD))


def block_diag(w):
    h, d, _ = w.shape
    eye = jnp.eye(h, dtype=w.dtype)
    return (eye[:, None, :, None] * w[:, :, None, :]).reshape(h * d, h * d)


def block_diag_part(dense, h):
    d = dense.shape[0] // h
    x = dense.reshape(h, d, h, d)
    return jnp.stack([x[i, :, i, :] for i in range(h)], axis=0)


def rope_tables(tp):
    pos = jnp.arange(tp, dtype=F32)
    inv_freq = ROPE_BASE ** (-jnp.arange(0, QK_ROPE, 2, dtype=F32) / QK_ROPE)
    ang = pos[:, None] * inv_freq[None, :]
    cos2 = jnp.tile(jnp.cos(ang), (1, 2))
    sin2 = jnp.tile(jnp.sin(ang), (1, 2))
    tabq = jnp.concatenate([jnp.ones((tp, QK_NOPE), F32), cos2, sin2], axis=1)
    tabk = jnp.concatenate([jnp.zeros((tp, QK_NOPE), F32), cos2, sin2], axis=1)
    return tabq, tabk


class Dims:
    def __init__(self, nb, seq):
        self.nb = nb
        self.t_real = seq + N_META
        self.tp = _round_up(self.t_real, ATT_BLK)
        self.n = self.tp // 4
        assert self.n % 16 == 0
        self.rows = nb * self.tp


class NoComm:
    def advance(self, carried):
        return None


def even_fwd(h, p, dm, comm):
    (u, hn), _ = norm_matmul(h, 0, D_MODEL, p["norm"], p["w_in"], dm.n, 512, F32, "ev_in")
    (y, ca, xc, a, hs), got = even_mid_fwd(u, p["conv_a"], p["conv_b"], p["conv_b_bias"], p["rw"], p["r_b"], p["iw"],
                                           p["i_b"], p["lam"], dm.nb, dm.tp, dm.n, "ev_mid", carry=comm.advance(None))
    comm.advance(got)
    out = matmul_res(y, p["w_out"].reshape(2, CONV_W, D_MODEL), h, dm.n, 512, "ev_out")
    return out, (h, u, hn, ca, xc, a, hs, y)


def even_bwd(dout, saved, p, dm, comm):
    h, u, hn, ca, xc, a, hs, y = saved
    g = {}
    dycat = matmul_nt(dout, p["w_out"], dm.n, 512, F32, "ev_dycat")
    g["w_out"], got = matmul_tn(y, dout, dm.n, "ev_dw_out", carry=comm.advance(None))
    outs, got = even_mid_bwd(u, dycat, ca, xc, a, hs, p["conv_a"], p["conv_b"], p["rw"], p["r_b"], p["iw"], p["i_b"],
                             p["lam"], dm.nb, dm.tp, dm.n, "ev_mid_bwd", carry=comm.advance(got))
    du, g["conv_a"], g["conv_b"], g["conv_b_bias"], drw, g["r_b"], diw, g["i_b"], g["lam"] = outs
    g["r_w"] = block_diag_part(drw, LRU_HEADS)
    g["i_w"] = block_diag_part(diw, LRU_HEADS)
    g["w_in"], got = matmul_tn(hn, du, dm.n, "ev_dw_in", carry=comm.advance(got))
    comm.advance(got)
    dx, g["norm"] = matmul_nt_normbwd(du, p["w_in"], h, 0, p["norm"], dout, dm.n, 512, F32, "ev_dx")
    return dx, g


def odd_fwd(h, p, tabq, tabk, dm, comm):
    nt = dm.tp // dm.n
    (u, hn), _ = norm_matmul(h, 0, D_MODEL, p["norm"], p["w_in_p"], dm.n, ODD_PAD, F32, "od_in")
    tab_spec = pl.BlockSpec((dm.n, HP), lambda i, j: (i % nt, 0))
    (q, cqn), _ = norm_matmul(u, 0, Q_LORA, p["q_norm"], p["w_uq_p"], dm.n, 512, MXU_DT, "od_q",
                              epi=_q_rope_epi, epi_ops=(tabq,), epi_specs=(tab_spec,))
    kr_spec = pl.BlockSpec((dm.n, HP), lambda i, j: (i, ODD_KR_COL))
    (k, ckvn), _ = norm_matmul(u, ODD_CKV_COL, KV_LORA, p["kv_norm"], p["w_uk_p"], dm.n, 512, MXU_DT, "od_k",
                               epi=_k_rope_epi, epi_ops=(u, tabk), epi_specs=(kr_spec, tab_spec))
    (v, _), _ = norm_matmul(u, ODD_CKV_COL, KV_LORA, p["kv_norm"], p["w_uv_p"], dm.n, 512, MXU_DT, "od_v")
    (o, lse), got = attn_fwd(q, k, v, dm.nb, dm.tp, "od_attn", carry=comm.advance(None))
    comm.advance(got)
    out = matmul_res(o[None], p["w_out"][None], h, dm.n, 512, "od_out")
    return out, (h, u, hn, cqn, ckvn, q, k, v, o, lse)


def odd_bwd(dout, saved, p, tabq, tabk, dm, comm):
    h, u, hn, cqn, ckvn, q, k, v, o, lse = saved
    g = {}
    do = matmul_nt(dout, p["w_out"], dm.n, 512, MXU_DT, "od_do")
    g["w_out"], got = matmul_tn(o, dout, dm.n, "od_dw_out", carry=comm.advance(None))
    (dq, dk, dv), got = attn_bwd(q, k, v, o, do, lse, dm.nb, dm.tp, "od_attn_bwd", carry=comm.advance(got))
    dqa, dkv, dkr = rope_bwd(dq, dk, dv, tabq, tabk, dm.tp, dm.n, "od_rope_bwd")
    g["w_uq_p"], got = matmul_tn(cqn, dqa, dm.n, "od_dw_uq", carry=comm.advance(got))
    comm.advance(got)
    g["w_ukv_p"], _ = matmul_tn(ckvn, dkv, dm.n, "od_dw_ukv")
    dcq, g["q_norm"] = matmul_nt_normbwd(dqa, p["w_uq_p"], u, 0, p["q_norm"], None, dm.n, 512, MXU_DT, "od_dcq")
    dckv, g["kv_norm"] = matmul_nt_normbwd(dkv, p["w_ukv_p"], u, ODD_CKV_COL, p["kv_norm"], None, dm.n, 512, MXU_DT,
                                           "od_dckv")
    du = jnp.concatenate([dcq, jnp.zeros((dm.rows, 128), MXU_DT), dckv, dkr], axis=1)
    g["w_in_p"], _ = matmul_tn(hn, du, dm.n, "od_dw_in")
    dx, g["norm"] = matmul_nt_normbwd(du, p["w_in_p"], h, 0, p["norm"], dout, dm.n, ODD_PAD, F32, "od_dx")
    return dx, g


def ffn_fwd(h, p, dm, comm):
    (up, hn), got = norm_matmul(h, 0, D_MODEL, p["norm"], p["w_up"], dm.n, D_FF // 2, MXU_DT, "ffn_up",
                                carry=comm.advance(None))
    comm.advance(got)
    u, y = ffn_mid_fwd(up, p["cw"], p["cb"], dm.nb, dm.tp, dm.n, "ffn_mid")
    out = matmul_res(y[None], p["w_down"][None], h, dm.n, 512, "ffn_down")
    return out, (h, up, hn, u, y)


def ffn_bwd(dout, saved, p, dm, comm):
    h, up, hn, u, y = saved
    g = {}
    dy = matmul_nt(dout, p["w_down"], dm.n, D_FF // 2, MXU_DT, "ffn_dy")
    g["w_down"], got = matmul_tn(y, dout, dm.n, "ffn_dw_down", carry=comm.advance(None))
    (dup, g["cw"], g["cb"]), got = ffn_mid_bwd(dy, u, up, p["cw"], dm.nb, dm.tp, dm.n, "ffn_mid_bwd",
                                               carry=comm.advance(got))
    g["w_up"], got = matmul_tn(hn, dup, dm.n, "ffn_dw_up", carry=comm.advance(got), col_shards=4)
    comm.advance(got)
    dx, g["norm"] = matmul_nt_normbwd(dup, p["w_up"], h, 0, p["norm"], dout, dm.n, D_FF // 2, F32, "ffn_dx")
    return dx, g


def _row(v):
    return v.reshape(1, -1)


def even_params(wf, j):
    return dict(norm=_row(wf["ev_norm"][j]), w_in=wf["ev_w_in"], conv_a=wf["ev_conv_a"][j], conv_b=wf["ev_conv_b"][j],
                conv_b_bias=_row(wf["ev_conv_b_bias"][j]), rw=block_diag(wf["ev_gate_r_w"][j]).astype(MXU_DT),
                r_b=_row(wf["ev_gate_r_b"][j]), iw=block_diag(wf["ev_gate_i_w"][j]).astype(MXU_DT),
                i_b=_row(wf["ev_gate_i_b"][j]), lam=_row(wf["ev_lru_lambda"][j]), w_out=wf["ev_w_out"])


def odd_params(wf, j):
    wkv = ukv_padded(wf["od_w_ukv"])
    return dict(norm=_row(wf["od_norm"][j]), w_in_p=odd_w_in_padded(wf["od_w_in"]), q_norm=_row(wf["od_q_norm"][j]),
                kv_norm=_row(wf["od_kv_norm"][j]), w_uq_p=uq_padded(wf["od_w_uq"]), w_ukv_p=wkv,
                w_uk_p=wkv[:, :MLA_HEADS * HP], w_uv_p=wkv[:, MLA_HEADS * HP:], w_out=wf["od_w_out"])


def ffn_params(wf, layer):
    return dict(norm=_row(wf["ffn_norm"][layer]), w_up=wf["ffn_w_up"],
                cw=jnp.moveaxis(wf["ffn_conv_w"][layer].reshape(3, 2, D_FF), 1, 0),
                cb=wf["ffn_conv_b"][layer].reshape(2, 1, D_FF), w_down=wf["ffn_w_down"])


def even_grads(g):
    out = {"ev_" + k_: g[k_] for k_ in ("w_in", "conv_a", "conv_b", "w_out")}
    out.update({"ev_norm": g["norm"][0], "ev_conv_b_bias": g["conv_b_bias"][0], "ev_gate_r_w": g["r_w"],
                "ev_gate_r_b": g["r_b"][0], "ev_gate_i_w": g["i_w"], "ev_gate_i_b": g["i_b"][0],
                "ev_lru_lambda": g["lam"][0]})
    return out


def odd_grads(g):
    return {"od_norm": g["norm"][0], "od_q_norm": g["q_norm"][0], "od_kv_norm": g["kv_norm"][0],
            "od_w_in": odd_w_in_unpad(g["w_in_p"]), "od_w_uq": uq_unpad(g["w_uq_p"]),
            "od_w_ukv": ukv_unpad(g["w_ukv_p"]), "od_w_out": g["w_out"]}


def ffn_grads(g):
    return {"ffn_norm": g["norm"][0], "ffn_w_up": g["w_up"], "ffn_conv_w": jnp.moveaxis(g["cw"], 0, 1).reshape(3, 2 * D_FF),
            "ffn_conv_b": g["cb"].reshape(2 * D_FF), "ffn_w_down": g["w_down"]}


WEIGHTS = ["meta_tokens", "ev_norm", "ev_w_in", "ev_conv_a", "ev_conv_b", "ev_conv_b_bias", "ev_gate_r_w", "ev_gate_r_b",
           "ev_gate_i_w", "ev_gate_i_b", "ev_lru_lambda", "ev_w_out", "od_norm", "od_w_in", "od_q_norm", "od_kv_norm",
           "od_w_uq", "od_w_ukv", "od_w_out", "ffn_norm", "ffn_w_up", "ffn_conv_w", "ffn_conv_b", "ffn_w_down",
           "final_norm"]
SHARD_AXIS = {"meta_tokens": 1, "ev_w_in": 2, "ev_conv_a": 2, "ev_conv_b": 2, "ev_w_out": 1, "od_norm": 1, "od_w_in": 1,
              "od_q_norm": 1, "od_kv_norm": 1, "od_w_uq": 2, "od_w_ukv": 2, "od_w_out": 1, "ffn_w_up": 2,
              "ffn_conv_w": 2, "ffn_w_down": 1}
MATMUL_WEIGHTS = ["ev_w_in", "ev_w_out", "od_w_in", "od_w_uq", "od_w_ukv", "od_w_out", "ffn_w_up", "ffn_w_down"]


LAYER_ORDER = [("ev", 0), ("ffn", 0), ("od", 0), ("ffn", 1), ("ev", 1), ("ffn", 2), ("od", 1), ("ffn", 3)]
LAYER_MATMUL = {"ev": ["ev_w_in", "ev_w_out"], "od": ["od_w_in", "od_w_uq", "od_w_ukv", "od_w_out"],
                "ffn": ["ffn_w_up", "ffn_w_down"]}
LAYER_SHARDED = {"ev": ["ev_w_in", "ev_conv_a", "ev_conv_b", "ev_w_out"],
                 "od": ["od_norm", "od_w_in", "od_q_norm", "od_kv_norm", "od_w_uq", "od_w_ukv", "od_w_out"],
                 "ffn": ["ffn_w_up", "ffn_conv_w", "ffn_w_down"]}
STACKED_SHARDS = "ffn_w_up"


def gather_all_layers(w, names, name):
    buf, offs = pack_rows([w[n] for n in names], 0, 32)
    halves = buf.reshape(2, buf.shape[0] // 2, LANE)
    got = own_block(run_stage(stage_gather_chips([halves]), name)[0], halves).reshape(4, buf.shape[0], LANE)
    return {n: unshard(unpack_rows(got, off, w[n].shape), SHARD_AXIS[n]) for n, off in zip(names, offs)}


def _halves(a):
    return a.reshape(2, a.shape[0] // 2, a.shape[1])


class GatherComm:
    def __init__(self, w, kind, idx):
        self.names = LAYER_MATMUL[kind]
        self.halves = [_halves(w[n][idx].astype(MXU_DT)) for n in self.names]
        self.stage = stage_gather_chips(self.halves)
        self.step, self.got = 0, None

    def advance(self, carried):
        self.step += 1
        if self.step == 1:
            return self.stage
        if self.step == 2:
            self.got = carried
        return None

    def run_alone(self, name):
        self.advance(run_stage(self.advance(None), name))

    def weights(self):
        out = {}
        for n, got, own in zip(self.names, self.got, self.halves):
            stacked = own_block(got, own).reshape(4, 2 * own.shape[1], own.shape[2])
            out[n] = stacked if n == STACKED_SHARDS else unshard(stacked, SHARD_AXIS[n] - 1)
        return out


class ReduceComm:
    def __init__(self, grads, axes, c_idx, tag, tail=None):
        shards = {n: grads[n] if n == STACKED_SHARDS else to_shards(grads[n], axes[n]) for n in grads}
        self.big = [n for n in grads if n in MATMUL_WEIGHTS]
        self.small = [n for n in grads if n not in MATMUL_WEIGHTS]
        self.shapes = {n: shards[n].shape[1:] for n in grads}
        arrays = [shards[n].reshape(4, 2, shards[n].shape[1] // 2, shards[n].shape[2]) for n in self.big]
        gs, self.offs = pack_rows([shards[n] for n in self.small], 1, 16)
        self.rs = gs.shape[1] // 2
        parts = [gs.reshape(4, 2, self.rs, LANE)]
        self.rr = 0
        if tail is not None:
            self.rr = tail.shape[0] // 8
            parts.append(tail.reshape(4, 2, self.rr, LANE))
        arrays.append(jnp.concatenate(parts, axis=2) if len(parts) > 1 else parts[0])
        self.arrays, self.c_idx, self.tag, self.step = arrays, c_idx, tag, 0
        self.part = self.mine = self.theirs = None

    def advance(self, carried):
        self.step += 1
        if self.step == 1:
            return stage_pair_exchange(self.arrays)
        if self.step == 2:
            self.part = [pair_sum(g, land, self.c_idx, "grad_pair_sum_%s_%d" % (self.tag, i))
                         for i, (g, land) in enumerate(zip(self.arrays, carried))]
            return stage_chip_scatter(self.part)
        if self.step == 3:
            me_idx = chip_index().astype(jnp.int32).reshape(1)
            self.mine = [chip_sum(land, part, me_idx, "grad_chip_sum_%s_%d" % (self.tag, i))
                         for i, (land, part) in enumerate(zip(carried, self.part))]
            return stage_pair_gather(self.mine)
        if self.step == 4:
            self.theirs = carried
        return None

    def run_alone(self, name):
        stage = self.advance(None)
        while stage is not None:
            stage = self.advance(run_stage(stage, name + "_%d" % self.step))

    def results(self):
        south = self.c_idx[0] == 0
        boths = [jnp.stack([jnp.where(south, m, t), jnp.where(south, t, m)], axis=0)
                 for m, t in zip(self.mine, self.theirs)]
        out = {n: b.reshape(self.shapes[n]) for n, b in zip(self.big, boths)}
        packed = boths[-1]
        flat = packed[:, :self.rs].reshape(2 * self.rs, LANE)
        out.update({n: unpack_rows(flat, off, self.shapes[n]) for n, off in zip(self.small, self.offs)})
        return out, packed[:, self.rs:self.rs + self.rr]


def kernel(x, meta_tokens, ev_norm, ev_w_in, ev_conv_a, ev_conv_b, ev_conv_b_bias, ev_gate_r_w, ev_gate_r_b, ev_gate_i_w, ev_gate_i_b, ev_lru_lambda, ev_w_out, od_norm, od_w_in, od_q_norm, od_kv_norm, od_w_uq, od_w_ukv, od_w_out, ffn_norm, ffn_w_up, ffn_conv_w, ffn_conv_b, ffn_w_down, final_norm, loss_target, m_meta_tokens, m_ev_norm, m_ev_w_in, m_ev_conv_a, m_ev_conv_b, m_ev_conv_b_bias, m_ev_gate_r_w, m_ev_gate_r_b, m_ev_gate_i_w, m_ev_gate_i_b, m_ev_lru_lambda, m_ev_w_out, m_od_norm, m_od_w_in, m_od_q_norm, m_od_kv_norm, m_od_w_uq, m_od_w_ukv, m_od_w_out, m_ffn_norm, m_ffn_w_up, m_ffn_conv_w, m_ffn_conv_b, m_ffn_w_down, m_final_norm, v_meta_tokens, v_ev_norm, v_ev_w_in, v_ev_conv_a, v_ev_conv_b, v_ev_conv_b_bias, v_ev_gate_r_w, v_ev_gate_r_b, v_ev_gate_i_w, v_ev_gate_i_b, v_ev_lru_lambda, v_ev_w_out, v_od_norm, v_od_w_in, v_od_q_norm, v_od_kv_norm, v_od_w_uq, v_od_w_ukv, v_od_w_out, v_ffn_norm, v_ffn_w_up, v_ffn_conv_w, v_ffn_conv_b, v_ffn_w_down, v_final_norm):
    given = dict(locals())
    w = {n: given[n] for n in WEIGHTS}
    nb, seq, _ = x.shape
    dm = Dims(nb, seq)
    n_layers = len(LAYER_ORDER)

    wf = {n: w[n] for n in WEIGHTS if n not in SHARD_AXIS}
    wf.update(gather_all_layers(w, [n for n in SHARD_AXIS if n not in MATMUL_WEIGHTS], "gather_small_weights"))
    gathers = [GatherComm(w, kind, idx) for kind, idx in LAYER_ORDER]
    gathers[0].run_alone("gather_first_layer")

    tail = dm.tp - dm.t_real
    meta = jnp.broadcast_to(wf["meta_tokens"][None], (nb, N_META, D_MODEL))
    h = jnp.concatenate([meta, x, jnp.zeros((nb, tail, D_MODEL), F32)], axis=1).reshape(dm.rows, D_MODEL)
    tgt = jnp.pad(loss_target, ((0, 0), (N_META, tail), (0, 0))).reshape(dm.rows, D_MODEL)
    tabq, tabk = rope_tables(dm.tp)

    params, saved = [], []
    for i, (kind, idx) in enumerate(LAYER_ORDER):
        wl = dict(wf)
        wl.update(gathers[i].weights())
        comm = gathers[i + 1] if i + 1 < n_layers else NoComm()
        if kind == "ev":
            p = even_params(wl, idx)
            h, sv = even_fwd(h, p, dm, comm)
        elif kind == "od":
            p = odd_params(wl, idx)
            h, sv = odd_fwd(h, p, tabq, tabk, dm, comm)
        else:
            p = ffn_params(wl, idx)
            h, sv = ffn_fwd(h, p, dm, comm)
        params.append(p)
        saved.append(sv)

    dh, loss, dfinal = loss_head(h, tgt, _row(wf["final_norm"]), dm.tp, dm.t_real, dm.n, "loss_head")
    loss = lax.psum(loss[0, 0], ("x", "y", "c"))

    c_idx = lax.axis_index("c").astype(jnp.int32).reshape(1)
    layer_grads = {n: {} for n in WEIGHTS}
    pending, reduces = NoComm(), []
    for i in reversed(range(n_layers)):
        kind, idx = LAYER_ORDER[i]
        if kind == "ev":
            dh, g = even_bwd(dh, saved[i], params[i], dm, pending)
            g = even_grads(g)
        elif kind == "od":
            dh, g = odd_bwd(dh, saved[i], params[i], tabq, tabk, dm, pending)
            g = odd_grads(g)
        else:
            dh, g = ffn_bwd(dh, saved[i], params[i], dm, pending)
            g = ffn_grads(g)
        for n in g:
            if n not in SHARD_AXIS:
                layer_grads[n][idx] = g[n]
        if i > 0:
            pending = ReduceComm({n: g[n] for n in LAYER_SHARDED[kind]}, {n: SHARD_AXIS[n] - 1 for n in SHARD_AXIS},
                                 c_idx, "%s%d" % (kind, idx))
            reduces.append((pending, idx))
    dh3 = dh.reshape(nb, dm.tp, D_MODEL)
    grad_x = dh3[:, N_META:dm.t_real]

    repl = [n for n in WEIGHTS if n not in SHARD_AXIS]
    layer_grads["final_norm"] = {0: dfinal[0]}
    repl_full = {n: (layer_grads[n][0] if n == "final_norm" else
                     jnp.stack([layer_grads[n][j] for j in range(w[n].shape[0])], axis=0)) for n in repl}
    tail_buf, tail_offs = pack_rows([repl_full[n] for n in repl], 0, 64)
    first = {n: g[n] for n in LAYER_SHARDED["ev"]}
    first["meta_tokens"] = jnp.sum(dh3[:, :N_META], axis=0)
    axes = {n: SHARD_AXIS[n] - 1 for n in SHARD_AXIS}
    axes["meta_tokens"] = SHARD_AXIS["meta_tokens"]
    last = ReduceComm(first, axes, c_idx, "first_layer", tail=tail_buf)
    last.run_alone("grad_first_layer")
    reduces.append((last, 0))

    red = {}
    for comm, idx in reduces:
        got, tail_piece = comm.results()
        for n, v_ in got.items():
            if n == "meta_tokens":
                red[n] = v_
            else:
                layer_grads[n][idx] = v_
    tails = own_block(run_stage(stage_gather_chips([tail_piece]), "grad_gather_replicated")[0], tail_piece)
    tails = tails.reshape(tail_buf.shape[0], LANE)
    for n, off in zip(repl, tail_offs):
        red[n] = unpack_rows(tails, off, w[n].shape)
    for n in SHARD_AXIS:
        if n != "meta_tokens":
            red[n] = jnp.stack([layer_grads[n][j] for j in range(w[n].shape[0])], axis=0)

    outs = [adamw(red[n], w[n], given["m_" + n], given["v_" + n], "adamw_" + n)[0] for n in WEIGHTS]
    return (loss, grad_x, *[o[0] for o in outs], *[o[1] for o in outs], *[o[2] for o in outs], *[o[3] for o in outs])
```

```python
import math

import jax
import jax.numpy as jnp
from jax import lax
from jax.experimental import pallas as pl
from jax.experimental.pallas import tpu as pltpu

F32 = jnp.float32
MXU_DT = jnp.bfloat16
S = jax.ShapeDtypeStruct
MESH = pl.DeviceIdType.MESH

EPS = 1e-6
D_MODEL = 1024
N_META = 16
DEPTH = 4
CONV_W = 512
LRU_W = 512
LRU_HEADS = 8
LRU_C = 8.0
EVEN_IN = 2560
MLA_HEADS = 16
QK_NOPE = 64
QK_ROPE = 32
QK_HEAD = 96
V_HEAD = 64
Q_LORA = 384
KV_LORA = 256
ROPE_BASE = 10000.0
D_FF = 2816
ODD_PAD = 896
ODD_CKV_COL = 2
ODD_KR_COL = 6
HP = 128
ATT_BLK = 384
Q_PRESCALE = QK_HEAD ** -0.5 * math.log2(math.e)
FFN_CT = 256
STRIP_ROWS = 16
LANE = 128
SUBLANE = 8
VMEM_LIMIT_MB = 52

ADAM_LR = 0.001
ADAM_B1 = 0.9
ADAM_B2 = 0.999
ADAM_EPS = 1e-08
ADAM_WD = 0.01
ADAM_STEP = 10

NT_DIMS = (((1,), (1,)), ((), ()))
TN_DIMS = (((0,), (0,)), ((), ()))


def _cp(sem):
    return pltpu.CompilerParams(dimension_semantics=sem, vmem_limit_bytes=VMEM_LIMIT_MB << 20)


def _div_tile(n, cap, mult):
    if n <= cap:
        return n
    best = None
    for t in range(mult, cap + 1, mult):
        if n % t == 0:
            best = t
    assert best is not None, (n, cap, mult)
    return best


def _round_up(n, m):
    return -(-n // m) * m


def mat_cols(arr):
    return arr.shape[1] if arr.ndim == 2 else arr.shape[0] * arr.shape[2]


def mat_width(arr):
    return arr.shape[-1]


def mat_spec(arr, tm, tw, rc):
    if arr.ndim == 2:
        return pl.BlockSpec((tm, tw), lambda *g: rc(*g))
    per = arr.shape[2] // tw
    assert arr.shape[2] % tw == 0

    def imap(*g):
        r, c = rc(*g)
        return (c // per, r, c % per)

    return pl.BlockSpec((None, tm, tw), imap)


HBM_SPEC = pl.BlockSpec(memory_space=pltpu.HBM)


class Stage:
    def __init__(self, inputs, out_shapes, sems, start, finish):
        self.inputs, self.out_shapes, self.sems, self.start, self.finish = inputs, out_shapes, sems, start, finish


def run_stage(stage, name):
    n_in, n_out = len(stage.inputs), len(stage.out_shapes)

    def body(*refs):
        ins, outs, sems = refs[:n_in], refs[n_in:n_in + n_out], refs[n_in + n_out:]
        stage.start(ins, outs, sems)
        stage.finish(ins, outs, sems)

    return pl.pallas_call(body, out_shape=list(stage.out_shapes), in_specs=[HBM_SPEC] * n_in,
                          out_specs=[HBM_SPEC] * n_out, scratch_shapes=list(stage.sems), name=name)(*stage.inputs)


def _call(body, ops, carry, *, grid, in_specs, out_specs, out_shape, scratch_shapes, sem, name):
    if carry is None:
        outs = pl.pallas_call(body, grid=grid, in_specs=in_specs, out_specs=out_specs, out_shape=out_shape,
                              scratch_shapes=scratch_shapes, compiler_params=_cp(sem), name=name)(*ops)
        return outs, None
    multi = isinstance(out_shape, (list, tuple))
    shapes = list(out_shape) if multi else [out_shape]
    ospecs = list(out_specs) if multi else [out_specs]
    n_in, n_out, n_sc = len(ops), len(shapes), len(scratch_shapes)
    c_in, c_out = len(carry.inputs), len(carry.out_shapes)

    def wrapped(*refs):
        ins, cin = refs[:n_in], refs[n_in:n_in + c_in]
        o0 = n_in + c_in
        outs, cout = refs[o0:o0 + n_out], refs[o0 + n_out:o0 + n_out + c_out]
        s0 = o0 + n_out + c_out
        scs, csems = refs[s0:s0 + n_sc], refs[s0 + n_sc:]
        first = pl.program_id(0) == 0
        last = pl.program_id(0) == grid[0] - 1
        for d in range(1, len(grid)):
            first = first & (pl.program_id(d) == 0)
            last = last & (pl.program_id(d) == grid[d] - 1)

        @pl.when(first)
        def _():
            carry.start(cin, cout, csems)

        body(*ins, *outs, *scs)

        @pl.when(last)
        def _():
            carry.finish(cin, cout, csems)

    res = pl.pallas_call(
        wrapped, grid=grid, in_specs=list(in_specs) + [HBM_SPEC] * c_in, out_specs=ospecs + [HBM_SPEC] * c_out,
        out_shape=shapes + list(carry.out_shapes), scratch_shapes=list(scratch_shapes) + list(carry.sems),
        compiler_params=_cp(("arbitrary",) * len(grid)), name=name)(*ops, *carry.inputs)
    main = res[:n_out]
    return (list(main) if multi else main[0]), list(res[n_out:])


def norm_matmul(x, xcol, kdim, gain, w, tm, tn, out_dtype, name, epi=None, epi_ops=(), epi_specs=(), carry=None):
    rows, n = x.shape[0], mat_cols(w) if w.ndim == 3 else w.shape[1]
    n_epi = len(epi_ops)
    w_spec = (pl.BlockSpec((kdim, tn), lambda i, j: (0, j)) if w.ndim == 2 else
              pl.BlockSpec((None, kdim, tn), lambda i, j: (j // (w.shape[2] // tn), 0, j % (w.shape[2] // tn))))

    def body(x_ref, g_ref, w_ref, *rest):
        epi_refs = rest[:n_epi]
        out_ref, xn_ref, xn_sc = rest[n_epi:]

        @pl.when(pl.program_id(1) == 0)
        def _():
            xv = x_ref[...]
            y = xv * lax.rsqrt(jnp.mean(xv * xv, axis=-1, keepdims=True) + EPS)
            xn = (y * g_ref[...]).astype(MXU_DT)
            xn_sc[...] = xn
            xn_ref[...] = xn

        acc = jnp.dot(xn_sc[...], w_ref[...], preferred_element_type=F32)
        if epi is not None:
            acc = epi(acc, *[r[...] for r in epi_refs])
        out_ref[...] = acc.astype(out_dtype)

    return _call(
        body, (x, gain, w, *epi_ops), carry, grid=(rows // tm, n // tn),
        in_specs=[pl.BlockSpec((tm, kdim), lambda i, j: (i, xcol)), pl.BlockSpec((1, kdim), lambda i, j: (0, 0)),
                  w_spec, *epi_specs],
        out_specs=[pl.BlockSpec((tm, tn), lambda i, j: (i, j)), pl.BlockSpec((tm, kdim), lambda i, j: (i, 0))],
        out_shape=[S((rows, n), out_dtype), S((rows, kdim), MXU_DT)],
        scratch_shapes=[pltpu.VMEM((tm, kdim), MXU_DT)], sem=("parallel", "arbitrary"), name=name)


def matmul_res(a, w, res, tm, tn, name):
    grp, rows, k = a.shape
    n = w.shape[2]

    def body(a_ref, w_ref, r_ref, o_ref):
        acc = r_ref[...]
        for g in range(grp):
            acc = acc + jnp.dot(a_ref[g], w_ref[g], preferred_element_type=F32)
        o_ref[...] = acc

    return pl.pallas_call(
        body, grid=(rows // tm, n // tn),
        in_specs=[pl.BlockSpec((grp, tm, k), lambda i, j: (0, i, 0)), pl.BlockSpec((grp, k, tn), lambda i, j: (0, 0, j)),
                  pl.BlockSpec((tm, tn), lambda i, j: (i, j))],
        out_specs=pl.BlockSpec((tm, tn), lambda i, j: (i, j)),
        out_shape=S((rows, n), F32), compiler_params=_cp(("parallel", "parallel")), name=name)(a, w, res)


def matmul_nt(a, w, tm, tn, out_dtype, name):
    rows, k = a.shape
    n = w.shape[0]

    def body(a_ref, w_ref, o_ref):
        o_ref[...] = lax.dot_general(a_ref[...].astype(MXU_DT), w_ref[...], NT_DIMS,
                                     preferred_element_type=F32).astype(out_dtype)

    return pl.pallas_call(
        body, grid=(rows // tm, n // tn),
        in_specs=[pl.BlockSpec((tm, k), lambda i, j: (i, 0)), pl.BlockSpec((tn, k), lambda i, j: (j, 0))],
        out_specs=pl.BlockSpec((tm, tn), lambda i, j: (i, j)),
        out_shape=S((rows, n), out_dtype), compiler_params=_cp(("parallel", "parallel")), name=name)(a, w)


def matmul_nt_normbwd(du, w, x, xcol, gain, res, tm, tk, out_dtype, name):
    rows, kc = du.shape[-2], mat_cols(du)
    dn = w.shape[-2]
    nk = kc // tk
    has_res = res is not None
    w_spec = (pl.BlockSpec((dn, tk), lambda i, k: (0, k)) if w.ndim == 2 else
              pl.BlockSpec((None, dn, tk), lambda i, k: (k // (w.shape[2] // tk), 0, k % (w.shape[2] // tk))))

    def body(du_ref, w_ref, x_ref, g_ref, *rest):
        if has_res:
            res_ref, dx_ref, dg_ref, acc = rest
        else:
            dx_ref, dg_ref, acc = rest
        i, k = pl.program_id(0), pl.program_id(1)

        @pl.when(k == 0)
        def _():
            acc[...] = jnp.zeros_like(acc)

        @pl.when((i == 0) & (k == 0))
        def _():
            dg_ref[...] = jnp.zeros_like(dg_ref)

        acc[...] += lax.dot_general(du_ref[...], w_ref[...], NT_DIMS, preferred_element_type=F32)

        @pl.when(k == nk - 1)
        def _():
            dhn = acc[...]
            xv = x_ref[...]
            rstd = lax.rsqrt(jnp.mean(xv * xv, axis=-1, keepdims=True) + EPS)
            xhat = xv * rstd
            dg_ref[...] += jnp.sum(dhn * xhat, axis=0, keepdims=True)
            dxh = dhn * g_ref[...]
            dx = rstd * (dxh - xhat * jnp.mean(dxh * xhat, axis=-1, keepdims=True))
            if has_res:
                dx = dx + res_ref[...]
            dx_ref[...] = dx.astype(out_dtype)

    in_specs = [mat_spec(du, tm, tk, lambda i, k: (i, k)), w_spec,
                pl.BlockSpec((tm, dn), lambda i, k: (i, xcol)), pl.BlockSpec((1, dn), lambda i, k: (0, 0))]
    ops = [du, w, x, gain]
    if has_res:
        in_specs.append(pl.BlockSpec((tm, dn), lambda i, k: (i, 0)))
        ops.append(res)
    return pl.pallas_call(
        body, grid=(rows // tm, nk), in_specs=in_specs,
        out_specs=[pl.BlockSpec((tm, dn), lambda i, k: (i, 0)), pl.BlockSpec((1, dn), lambda i, k: (0, 0))],
        out_shape=[S((rows, dn), out_dtype), S((1, dn), F32)],
        scratch_shapes=[pltpu.VMEM((tm, dn), F32)],
        compiler_params=_cp(("arbitrary", "arbitrary")), name=name)(*ops)


def matmul_tn(a, b, tr, name, carry=None, col_shards=1):
    rows, ka, nb = a.shape[-2], mat_cols(a), mat_cols(b)
    ta = _div_tile(mat_width(a), 1536, LANE)
    tb = _div_tile(mat_width(b), 1536 if ta <= 1024 else 1024, LANE)
    nr = rows // tr
    if col_shards == 1:
        out_spec, out_shape = pl.BlockSpec((ta, tb), lambda i, j, r: (i, j)), S((ka, nb), F32)
    else:
        per = nb // col_shards // tb
        assert per * tb * col_shards == nb
        out_spec = pl.BlockSpec((None, ta, tb), lambda i, j, r: (j // per, i, j % per))
        out_shape = S((col_shards, ka, nb // col_shards), F32)

    def body(a_ref, b_ref, o_ref, acc):
        r = pl.program_id(2)

        @pl.when(r == 0)
        def _():
            acc[...] = jnp.zeros_like(acc)

        acc[...] += lax.dot_general(a_ref[...].astype(MXU_DT), b_ref[...].astype(MXU_DT), TN_DIMS,
                                    preferred_element_type=F32)

        @pl.when(r == nr - 1)
        def _():
            o_ref[...] = acc[...]

    return _call(
        body, (a, b), carry, grid=(ka // ta, nb // tb, nr),
        in_specs=[mat_spec(a, tr, ta, lambda i, j, r: (r, i)), mat_spec(b, tr, tb, lambda i, j, r: (r, j))],
        out_specs=out_spec, out_shape=out_shape, scratch_shapes=[pltpu.VMEM((ta, tb), F32)],
        sem=("parallel", "parallel", "arbitrary"), name=name)


def _sigmoid(x):
    return 1.0 / (1.0 + jnp.exp(-x))


def _log1p(e):
    return jnp.where(e < 1e-3, e * (1.0 - e * (0.5 - e * (1.0 / 3.0 - 0.25 * e))), jnp.log(1.0 + e))


def _softplus(x):
    return jnp.maximum(x, 0.0) + _log1p(jnp.exp(-jnp.abs(x)))


def _expm1(x):
    series = x * (1.0 + x * (0.5 + x * (1.0 / 6.0 + x * (1.0 / 24.0 + x * (1.0 / 120.0)))))
    return jnp.where(jnp.abs(x) < 0.1, series, jnp.exp(x) - 1.0)


_GELU_K = math.sqrt(2.0 / math.pi)
_GELU_C = 0.044715


def _gelu_and_grad(x):
    th = jnp.tanh(_GELU_K * (x + _GELU_C * x * x * x))
    g = 0.5 * x * (1.0 + th)
    dg = 0.5 * (1.0 + th) + 0.5 * x * (1.0 - th * th) * _GELU_K * (1.0 + 3.0 * _GELU_C * x * x)
    return g, dg


def _row_iota(shape):
    return lax.broadcasted_iota(jnp.int32, shape, 0)


def _scan_chunk_fwd(a_sc, u_sc, out_ref, hcar, n, width):
    rowi = _row_iota((SUBLANE, width))

    def step(c, hprev):
        r0 = pl.multiple_of(c * SUBLANE, SUBLANE)
        a = a_sc[pl.ds(r0, SUBLANE), :]
        u = u_sc[pl.ds(r0, SUBLANE), :]
        for d in (1, 2, 4):
            a_s = jnp.where(rowi >= d, pltpu.roll(a, d, axis=0), 1.0)
            u_s = jnp.where(rowi >= d, pltpu.roll(u, d, axis=0), 0.0)
            u = u + a * u_s
            a = a * a_s
        h = u + a * hprev
        out_ref[pl.ds(r0, SUBLANE), :] = h
        return jnp.broadcast_to(h[SUBLANE - 1:SUBLANE, :], (SUBLANE, width))

    hcar[...] = lax.fori_loop(0, n // SUBLANE, step, hcar[...], unroll=4)


def _scan_chunk_bwd(b_sc, d_sc, out_ref, gcar, n, width):
    rowi = _row_iota((SUBLANE, width))
    nc = n // SUBLANE

    def step(c, gnext):
        r0 = pl.multiple_of((nc - 1 - c) * SUBLANE, SUBLANE)
        b = b_sc[pl.ds(r0, SUBLANE), :]
        d = d_sc[pl.ds(r0, SUBLANE), :]
        for s in (1, 2, 4):
            keep = rowi < SUBLANE - s
            b_s = jnp.where(keep, pltpu.roll(b, SUBLANE - s, axis=0), 1.0)
            d_s = jnp.where(keep, pltpu.roll(d, SUBLANE - s, axis=0), 0.0)
            d = d + b * d_s
            b = b * b_s
        g = d + b * gnext
        out_ref[pl.ds(r0, SUBLANE), :] = g
        return jnp.broadcast_to(g[0:1, :], (SUBLANE, width))

    gcar[...] = lax.fori_loop(0, nc, step, gcar[...], unroll=4)


def even_mid_fwd(u, conv_a, conv_b, conv_b_bias, rw, rb, iw, ib, lam, nb, tp, n, name, carry=None):
    rows = u.shape[0]
    w = LANE
    nj = CONV_W // w
    nt = tp // n
    h8 = SUBLANE

    def body(gb_r, gc_r, xa_r, xb_r, gate_r, ca_w, cb_w, cb_b, rw_r, rb_r, iw_r, ib_r, lam_r,
             y_o, ca_o, xc_o, a_o, hs_o, pext, xext, hcar, a_sc, u_sc):
        t = pl.program_id(2)

        @pl.when(t == 0)
        def _():
            pext[0:h8, :] = jnp.zeros((h8, w), F32)
            xext[0:h8, :] = jnp.zeros((h8, w), F32)
            hcar[...] = jnp.zeros_like(hcar)

        p = gc_r[...] * xa_r[...]
        pext[h8:h8 + n, :] = p
        wa = ca_w[...]
        ca = wa[2:3, :] * p + wa[1:2, :] * pext[h8 - 1:h8 - 1 + n, :] + wa[0:1, :] * pext[h8 - 2:h8 - 2 + n, :]
        ca_o[...] = ca
        y_o[0] = (gb_r[...] * ca).astype(MXU_DT)
        pext[0:h8, :] = pext[n:n + h8, :]

        xb = xb_r[...]
        xext[h8:h8 + n, :] = xb
        wb = cb_w[...]
        xc = (wb[3:4, :] * xb + wb[2:3, :] * xext[h8 - 1:h8 - 1 + n, :] + wb[1:2, :] * xext[h8 - 2:h8 - 2 + n, :]
              + wb[0:1, :] * xext[h8 - 3:h8 - 3 + n, :]) + cb_b[...]
        xc_o[...] = xc
        xext[0:h8, :] = xext[n:n + h8, :]

        xcm = xc.astype(MXU_DT)
        r = _sigmoid(jnp.dot(xcm, rw_r[...], preferred_element_type=F32) + rb_r[...])
        ig = _sigmoid(jnp.dot(xcm, iw_r[...], preferred_element_type=F32) + ib_r[...])
        log_a = (-LRU_C) * r * _softplus(-lam_r[...])
        a = jnp.exp(log_a)
        mult = jnp.sqrt(-_expm1(2.0 * log_a))
        a_sc[...] = a
        a_o[...] = a
        u_sc[...] = mult * (ig * xc)
        _scan_chunk_fwd(a_sc, u_sc, hs_o, hcar, n, w)
        gel, _ = _gelu_and_grad(gate_r[...])
        y_o[1] = (gel * hs_o[...]).astype(MXU_DT)

    def ublk(off):
        return pl.BlockSpec((n, w), lambda j, b, t: (b * nt + t, off + j))

    def pblk(r_):
        return pl.BlockSpec((r_, w), lambda j, b, t: (0, j))

    act = pl.BlockSpec((n, w), lambda j, b, t: (b * nt + t, j))
    mat = pl.BlockSpec((w, w), lambda j, b, t: (j, j))
    return _call(
        body, (u, u, u, u, u, conv_a, conv_b, conv_b_bias, rw, rb, iw, ib, lam), carry, grid=(nj, nb, nt),
        in_specs=[ublk(0), ublk(nj), ublk(2 * nj), ublk(3 * nj), ublk(4 * nj), pblk(3), pblk(4), pblk(1),
                  mat, pblk(1), mat, pblk(1), pblk(1)],
        out_specs=[pl.BlockSpec((2, n, w), lambda j, b, t: (0, b * nt + t, j)), act, act, act, act],
        out_shape=[S((2, rows, CONV_W), MXU_DT), S((rows, CONV_W), F32), S((rows, LRU_W), F32), S((rows, LRU_W), F32),
                   S((rows, LRU_W), F32)],
        scratch_shapes=[pltpu.VMEM((n + h8, w), F32), pltpu.VMEM((n + h8, w), F32), pltpu.VMEM((h8, w), F32),
                        pltpu.VMEM((n, w), F32), pltpu.VMEM((n, w), F32)],
        sem=("parallel", "parallel", "arbitrary"), name=name)


def even_mid_bwd(u, dycat, ca, xc, a_sv, hs, conv_a, conv_b, rw, rb, iw, ib, lam, nb, tp, n, name, carry=None):
    rows = u.shape[0]
    w = LANE
    nj = CONV_W // w
    nt = tp // n
    h8 = SUBLANE

    def body(gb_r, gc_r, xa_r, xb_r, gate_r, dya_r, dyb_r, ca_r, xc_r, a_r, hs_r, hsp_r,
             ca_w, cb_w, rw_r, rb_r, iw_r, ib_r, lam_r,
             du_o, dca_w, dcb_w, dcb_b, drw, drb, diw, dib, dlam,
             aext, hext, dext, eext, gcar, b_sc, d_sc, g_sc):
        b, t = pl.program_id(1), pl.program_id(2)

        @pl.when((b == 0) & (t == 0))
        def _():
            for ref in (dca_w, dcb_w, dcb_b, drw, drb, diw, dib, dlam):
                ref[...] = jnp.zeros_like(ref)

        @pl.when(t == 0)
        def _():
            aext[n:n + h8, :] = jnp.zeros((h8, w), F32)
            dext[n:n + h8, :] = jnp.zeros((h8, w), F32)
            eext[n:n + h8, :] = jnp.zeros((h8, w), F32)
            gcar[...] = jnp.zeros_like(gcar)

        xc_v = xc_r[...]
        xcm = xc_v.astype(MXU_DT)
        r = _sigmoid(jnp.dot(xcm, rw_r[...], preferred_element_type=F32) + rb_r[...])
        ig = _sigmoid(jnp.dot(xcm, iw_r[...], preferred_element_type=F32) + ib_r[...])
        lam_v = lam_r[...]
        sp = _softplus(-lam_v)
        log_a = (-LRU_C) * r * sp
        a = a_r[...]
        mult = jnp.sqrt(-_expm1(2.0 * log_a))
        hs_v = hs_r[...]
        gel, dgel = _gelu_and_grad(gate_r[...])
        dyb = dyb_r[...]
        du_o[4] = (dyb * hs_v * dgel).astype(MXU_DT)

        aext[0:n, :] = a
        b_sc[...] = aext[1:1 + n, :]
        d_sc[...] = dyb * gel
        _scan_chunk_bwd(b_sc, d_sc, g_sc, gcar, n, w)
        aext[n:n + h8, :] = aext[0:h8, :]
        g = g_sc[...]

        hext[0:h8, :] = jnp.where(t == nt - 1, 0.0, hsp_r[...])
        hext[h8:h8 + n, :] = hs_v
        da = g * hext[h8 - 1:h8 - 1 + n, :]
        dmult = g * (ig * xc_v)
        di = g * mult * xc_v
        dxc = g * mult * ig
        dlog_a = da * a - dmult * (a * a) / mult
        dr = dlog_a * ((-LRU_C) * sp)
        dsp = jnp.sum(dlog_a * ((-LRU_C) * r), axis=0, keepdims=True)
        dlam[...] += dsp * (-_sigmoid(-lam_v))
        dzr = dr * r * (1.0 - r)
        dzi = di * ig * (1.0 - ig)
        dzr_m = dzr.astype(MXU_DT)
        dzi_m = dzi.astype(MXU_DT)
        dxc = (dxc + lax.dot_general(dzr_m, rw_r[...], NT_DIMS, preferred_element_type=F32)
               + lax.dot_general(dzi_m, iw_r[...], NT_DIMS, preferred_element_type=F32))
        drw[...] += lax.dot_general(xcm, dzr_m, TN_DIMS, preferred_element_type=F32)
        diw[...] += lax.dot_general(xcm, dzi_m, TN_DIMS, preferred_element_type=F32)
        drb[...] += jnp.sum(dzr, axis=0, keepdims=True)
        dib[...] += jnp.sum(dzi, axis=0, keepdims=True)
        dcb_b[...] += jnp.sum(dxc, axis=0, keepdims=True)

        xb = xb_r[...]
        dext[0:n, :] = dxc
        wb = cb_w[...]
        d1, d2, d3 = dext[1:1 + n, :], dext[2:2 + n, :], dext[3:3 + n, :]
        du_o[3] = (wb[3:4, :] * dxc + wb[2:3, :] * d1 + wb[1:2, :] * d2 + wb[0:1, :] * d3).astype(MXU_DT)
        dcb_w[3:4, :] += jnp.sum(xb * dxc, axis=0, keepdims=True)
        dcb_w[2:3, :] += jnp.sum(xb * d1, axis=0, keepdims=True)
        dcb_w[1:2, :] += jnp.sum(xb * d2, axis=0, keepdims=True)
        dcb_w[0:1, :] += jnp.sum(xb * d3, axis=0, keepdims=True)
        dext[n:n + h8, :] = dext[0:h8, :]

        gb, gc, xa = gb_r[...], gc_r[...], xa_r[...]
        dya = dya_r[...]
        du_o[0] = (dya * ca_r[...]).astype(MXU_DT)
        dca = dya * gb
        eext[0:n, :] = dca
        wa = ca_w[...]
        e1, e2 = eext[1:1 + n, :], eext[2:2 + n, :]
        dp = wa[2:3, :] * dca + wa[1:2, :] * e1 + wa[0:1, :] * e2
        p = gc * xa
        dca_w[2:3, :] += jnp.sum(p * dca, axis=0, keepdims=True)
        dca_w[1:2, :] += jnp.sum(p * e1, axis=0, keepdims=True)
        dca_w[0:1, :] += jnp.sum(p * e2, axis=0, keepdims=True)
        eext[n:n + h8, :] = eext[0:h8, :]
        du_o[1] = (dp * xa).astype(MXU_DT)
        du_o[2] = (dp * gc).astype(MXU_DT)

    def rt(b, t):
        return b * nt + (nt - 1 - t)

    def ublk(off):
        return pl.BlockSpec((n, w), lambda j, b, t: (rt(b, t), off + j))

    def pblk(r_):
        return pl.BlockSpec((r_, w), lambda j, b, t: (0, j))

    act = pl.BlockSpec((n, w), lambda j, b, t: (rt(b, t), j))
    n8 = n // h8
    hsp = pl.BlockSpec((h8, w), lambda j, b, t: (jnp.maximum(rt(b, t) * n8 - 1, 0), j))
    mat = pl.BlockSpec((w, w), lambda j, b, t: (j, j))
    return _call(
        body, (u, u, u, u, u, dycat, dycat, ca, xc, a_sv, hs, hs, conv_a, conv_b, rw, rb, iw, ib, lam), carry,
        grid=(nj, nb, nt),
        in_specs=[ublk(0), ublk(nj), ublk(2 * nj), ublk(3 * nj), ublk(4 * nj), ublk(0), ublk(nj), act, act, act, act,
                  hsp, pblk(3), pblk(4), mat, pblk(1), mat, pblk(1), pblk(1)],
        out_specs=[pl.BlockSpec((5, n, w), lambda j, b, t: (0, rt(b, t), j)), pblk(3), pblk(4), pblk(1),
                   mat, pblk(1), mat, pblk(1), pblk(1)],
        out_shape=[S((5, rows, CONV_W), MXU_DT), S((3, CONV_W), F32), S((4, LRU_W), F32), S((1, LRU_W), F32),
                   S((LRU_W, LRU_W), F32), S((1, LRU_W), F32), S((LRU_W, LRU_W), F32), S((1, LRU_W), F32),
                   S((1, LRU_W), F32)],
        scratch_shapes=[pltpu.VMEM((n + h8, w), F32)] * 4 + [pltpu.VMEM((h8, w), F32)] + [pltpu.VMEM((n, w), F32)] * 3,
        sem=("arbitrary", "arbitrary", "arbitrary"), name=name)


def ffn_mid_fwd(up, cw, cb, nb, tp, n, name):
    rows = up.shape[0]
    w = FFN_CT
    nj = D_FF // w
    nt = tp // n
    h8 = SUBLANE

    sr = STRIP_ROWS

    def body(xa_r, xg_r, w_r, b_r, u_o, y_o, halo):
        t = pl.program_id(2)

        @pl.when(t == 0)
        def _():
            halo[...] = jnp.zeros_like(halo)

        wv = (w_r[0], w_r[1])
        bv = (b_r[0], b_r[1])

        def strip(s, carry):
            r0 = pl.multiple_of(s * sr, sr)
            us, new = [], []
            for g, x_r in enumerate((xa_r, xg_r)):
                x = x_r[pl.ds(r0, sr), :].astype(F32)
                win = jnp.concatenate([carry[g], x], axis=0)
                x1 = pltpu.roll(win, 1, axis=0)[h8:, :]
                x2 = pltpu.roll(win, 2, axis=0)[h8:, :]
                u = (wv[g][2:3, :] * x + wv[g][1:2, :] * x1 + wv[g][0:1, :] * x2) + bv[g]
                u_o[g, pl.ds(r0, sr), :] = u.astype(MXU_DT)
                us.append(u)
                new.append(x[sr - h8:, :])
            y_o[pl.ds(r0, sr), :] = (us[0] * _sigmoid(us[0]) * us[1]).astype(MXU_DT)
            return tuple(new)

        ha, hg = lax.fori_loop(0, n // sr, strip, (halo[0], halo[1]))
        halo[0] = ha
        halo[1] = hg

    def ublk(off):
        return pl.BlockSpec((n, w), lambda j, b, t: (b * nt + t, off + j))

    return pl.pallas_call(
        body, grid=(nj, nb, nt),
        in_specs=[ublk(0), ublk(nj), pl.BlockSpec((2, 3, w), lambda j, b, t: (0, 0, j)),
                  pl.BlockSpec((2, 1, w), lambda j, b, t: (0, 0, j))],
        out_specs=[pl.BlockSpec((2, n, w), lambda j, b, t: (0, b * nt + t, j)), ublk(0)],
        out_shape=[S((2, rows, D_FF), MXU_DT), S((rows, D_FF), MXU_DT)],
        scratch_shapes=[pltpu.VMEM((2, h8, w), F32)],
        compiler_params=_cp(("parallel", "parallel", "arbitrary")), name=name,
    )(up, up, cw, cb)


def ffn_mid_bwd(dy, u, up, cw, nb, tp, n, name, carry=None):
    rows = up.shape[0]
    w = FFN_CT
    nj = D_FF // w
    nt = tp // n
    h8 = SUBLANE

    sr = STRIP_ROWS
    ns = n // sr

    def fold(v):
        acc = v[0:h8, :]
        for k in range(1, sr // h8):
            acc = acc + v[k * h8:(k + 1) * h8, :]
        return acc

    def body(dy_r, u_r, xa_r, xg_r, w_r, dx_o, dw, db, halo):
        b, t = pl.program_id(1), pl.program_id(2)

        @pl.when((b == 0) & (t == 0))
        def _():
            dw[...] = jnp.zeros_like(dw)
            db[...] = jnp.zeros_like(db)

        @pl.when(t == 0)
        def _():
            halo[...] = jnp.zeros_like(halo)

        wv = (w_r[0], w_r[1])

        def strip(s, carry):
            halos, sums = carry
            r0 = pl.multiple_of((ns - 1 - s) * sr, sr)
            dyv = dy_r[pl.ds(r0, sr), :].astype(F32)
            ua = u_r[0, pl.ds(r0, sr), :].astype(F32)
            ug = u_r[1, pl.ds(r0, sr), :].astype(F32)
            sg = _sigmoid(ua)
            dus = (dyv * ug * (sg * (1.0 + ua * (1.0 - sg))), dyv * (ua * sg))
            new_halos, new_sums = [], []
            for g, x_r in enumerate((xa_r, xg_r)):
                du = dus[g]
                win = jnp.concatenate([du, halos[g]], axis=0)
                d1 = pltpu.roll(win, sr + h8 - 1, axis=0)[0:sr, :]
                d2 = pltpu.roll(win, sr + h8 - 2, axis=0)[0:sr, :]
                dx_o[g, pl.ds(r0, sr), :] = (wv[g][2:3, :] * du + wv[g][1:2, :] * d1 + wv[g][0:1, :] * d2).astype(MXU_DT)
                x = x_r[pl.ds(r0, sr), :].astype(F32)
                s2, s1, s0, sb = sums[g]
                new_sums.append((s2 + fold(x * du), s1 + fold(x * d1), s0 + fold(x * d2), sb + fold(du)))
                new_halos.append(du[0:h8, :])
            return tuple(new_halos), tuple(new_sums)

        z = jnp.zeros((h8, w), F32)
        halos, sums = lax.fori_loop(0, ns, strip, ((halo[0], halo[1]), ((z, z, z, z), (z, z, z, z))))
        halo[0] = halos[0]
        halo[1] = halos[1]
        for g in range(2):
            s2, s1, s0, sb = sums[g]
            dw[g, 2:3, :] += jnp.sum(s2, axis=0, keepdims=True)
            dw[g, 1:2, :] += jnp.sum(s1, axis=0, keepdims=True)
            dw[g, 0:1, :] += jnp.sum(s0, axis=0, keepdims=True)
            db[g] += jnp.sum(sb, axis=0, keepdims=True)

    def rt(b, t):
        return b * nt + (nt - 1 - t)

    def ublk(off):
        return pl.BlockSpec((n, w), lambda j, b, t: (rt(b, t), off + j))

    pair = pl.BlockSpec((2, n, w), lambda j, b, t: (0, rt(b, t), j))
    return _call(
        body, (dy, u, up, up, cw), carry, grid=(nj, nb, nt),
        in_specs=[ublk(0), pair, ublk(0), ublk(nj), pl.BlockSpec((2, 3, w), lambda j, b, t: (0, 0, j))],
        out_specs=[pair, pl.BlockSpec((2, 3, w), lambda j, b, t: (0, 0, j)),
                   pl.BlockSpec((2, 1, w), lambda j, b, t: (0, 0, j))],
        out_shape=[S((2, rows, D_FF), MXU_DT), S((2, 3, D_FF), F32), S((2, 1, D_FF), F32)],
        scratch_shapes=[pltpu.VMEM((2, h8, w), F32)],
        sem=("arbitrary", "arbitrary", "arbitrary"), name=name)


def _lane_mod(shape):
    return lax.broadcasted_iota(jnp.int32, shape, 1) & (HP - 1)


def _q_rope_epi(acc, tab):
    reps = acc.shape[1] // HP
    a = acc * jnp.tile(tab, (1, reps))
    lane = _lane_mod(a.shape)
    shifted = pltpu.roll(a, a.shape[1] - QK_ROPE, axis=1)
    return jnp.where(lane < QK_NOPE, a, jnp.where(lane < QK_HEAD, a + shifted, 0.0)) * Q_PRESCALE


def _k_rope_block(krblk, tabk):
    a = krblk * tabk
    lane = _lane_mod(a.shape)
    b = a + pltpu.roll(a, HP - QK_ROPE, axis=1)
    return jnp.where((lane >= QK_NOPE) & (lane < QK_HEAD), b, 0.0)


def _k_rope_epi(acc, krblk, tabk):
    reps = acc.shape[1] // HP
    return acc + jnp.tile(_k_rope_block(krblk, tabk), (1, reps))


def attn_fwd(q, k, v, nb, tp, name, carry=None):
    rows = q.shape[0]
    blk = ATT_BLK
    nq = tp // blk
    npair = MLA_HEADS // 2

    def body(q_r, k_r, v_r, o_r, lse_r):
        qi = pl.program_id(2)
        lane = lax.broadcasted_iota(jnp.int32, (blk, LANE), 1)
        even = lane < V_HEAD
        sum_lane = (V_HEAD, 0)
        rowi = lax.broadcasted_iota(jnp.int32, (blk, blk), 0)
        coli = lax.broadcasted_iota(jnp.int32, (blk, blk), 1)
        qs = [q_r[:, h * HP:(h + 1) * HP] for h in range(2)]

        def kv_block(k0, width, carry, visible):
            ms, accs = carry
            vblk = v_r[pl.ds(k0, width), :]
            one = jnp.ones_like(vblk)
            zero = jnp.zeros_like(vblk)
            vlane = lax.broadcasted_iota(jnp.int32, (width, LANE), 1)
            ss = [lax.dot_general(qs[h], k_r[pl.ds(k0, width), h * HP:(h + 1) * HP], NT_DIMS,
                                  preferred_element_type=F32) for h in range(2)]
            new_ms, new_accs = [], []
            for h in range(2):
                s = ss[h]
                if visible is not None:
                    s = jnp.where(visible, s, -jnp.inf)
                m_new = jnp.maximum(ms[h], jnp.max(s, axis=1, keepdims=True))
                alpha = jnp.exp2(ms[h] - m_new)
                p = jnp.exp2(s - m_new).astype(MXU_DT)
                mine = (vlane < V_HEAD) if h == 0 else (vlane >= V_HEAD)
                vh = jnp.where(mine, vblk, jnp.where(vlane == sum_lane[h], one, zero))
                new_accs.append(alpha * accs[h] + jnp.dot(p, vh, preferred_element_type=F32))
                new_ms.append(m_new)
            return tuple(new_ms), tuple(new_accs)

        neg = jnp.full((blk, 1), -jnp.inf, F32)
        zacc = jnp.zeros((blk, LANE), F32)
        carry = lax.fori_loop(0, qi // 2, lambda i, c: kv_block(pl.multiple_of(i * 2 * blk, blk), 2 * blk, c, None),
                              ((neg, neg), (zacc, zacc)))
        rowi2 = lax.broadcasted_iota(jnp.int32, (blk, 2 * blk), 0)
        coli2 = lax.broadcasted_iota(jnp.int32, (blk, 2 * blk), 1)
        ms, accs = lax.cond(
            qi % 2 == 1,
            lambda c: kv_block(pl.multiple_of((qi - 1) * blk, blk), 2 * blk, c, coli2 - blk <= rowi2),
            lambda c: kv_block(pl.multiple_of(qi * blk, blk), blk, c, coli <= rowi), carry)
        ls = [accs[h][:, sum_lane[h]:sum_lane[h] + 1] for h in range(2)]
        o_r[...] = jnp.where(even, accs[0] / ls[0], accs[1] / ls[1]).astype(MXU_DT)
        lse_r[...] = jnp.where(even, ms[0] + jnp.log2(ls[0]), ms[1] + jnp.log2(ls[1]))

    return _call(
        body, (q, k, v), carry, grid=(nb, npair, nq),
        in_specs=[pl.BlockSpec((blk, 2 * HP), lambda b, p, i: (b * nq + i, p)),
                  pl.BlockSpec((tp, 2 * HP), lambda b, p, i: (b, p)),
                  pl.BlockSpec((tp, LANE), lambda b, p, i: (b, p))],
        out_specs=[pl.BlockSpec((blk, LANE), lambda b, p, i: (b * nq + i, p)),
                   pl.BlockSpec((None, blk, LANE), lambda b, p, i: (p, b * nq + i, 0))],
        out_shape=[S((rows, MLA_HEADS * V_HEAD), MXU_DT), S((npair, rows, LANE), F32)], scratch_shapes=[],
        sem=("parallel", "parallel", "arbitrary"), name=name)


def attn_bwd(q, k, v, o, do, lse, nb, tp, name, carry=None):
    rows = q.shape[0]
    blk = ATT_BLK
    nq = tp // blk
    npair = MLA_HEADS // 2
    scale = QK_HEAD ** -0.5

    def body(q_r, k_r, v_r, o_r, do_r, lse_r, dq_o, dk_o, dv_o, dq_acc, delta_sc):
        kb = pl.program_id(2)
        even = lax.broadcasted_iota(jnp.int32, (blk, LANE), 1) < V_HEAD
        rowi = lax.broadcasted_iota(jnp.int32, (blk, blk), 0)
        coli = lax.broadcasted_iota(jnp.int32, (blk, blk), 1)

        @pl.when(kb == 0)
        def _():
            dq_acc[...] = jnp.zeros_like(dq_acc)

            def dstep(i, c):
                r0 = pl.multiple_of(i * blk, blk)
                prod = do_r[pl.ds(r0, blk), :].astype(F32) * o_r[pl.ds(r0, blk), :].astype(F32)
                de = jnp.sum(jnp.where(even, prod, 0.0), axis=1, keepdims=True)
                dd = jnp.sum(jnp.where(even, 0.0, prod), axis=1, keepdims=True)
                delta_sc[pl.ds(r0, blk), :] = jnp.where(even, de, dd)
                return c

            lax.fori_loop(0, nq, dstep, 0)

        vblk = v_r[...]
        ks = [k_r[:, h * HP:(h + 1) * HP] for h in range(2)]

        def q_block(r0, height, carry, visible):
            dk0, dk1, dv = carry
            dob = do_r[pl.ds(r0, height), :]
            lse_b = lse_r[pl.ds(r0, height), :]
            dl_b = delta_sc[pl.ds(r0, height), :]
            qlane = lax.broadcasted_iota(jnp.int32, (height, LANE), 1)
            dks = [dk0, dk1]
            qhs = [q_r[pl.ds(r0, height), h * HP:(h + 1) * HP] for h in range(2)]
            dohs = [jnp.where((qlane < V_HEAD) if h == 0 else (qlane >= V_HEAD), dob, jnp.zeros_like(dob))
                    for h in range(2)]
            ss = [lax.dot_general(qhs[h], ks[h], NT_DIMS, preferred_element_type=F32) for h in range(2)]
            dps = [lax.dot_general(dohs[h], vblk, NT_DIMS, preferred_element_type=F32) for h in range(2)]
            for h in range(2):
                lo = 0 if h == 0 else V_HEAD
                p = jnp.exp2(ss[h] - lse_b[:, lo:lo + 1])
                if visible is not None:
                    p = jnp.where(visible, p, 0.0)
                ds = (p * (dps[h] - dl_b[:, lo:lo + 1])).astype(MXU_DT)
                dv = dv + lax.dot_general(p.astype(MXU_DT), dohs[h], TN_DIMS, preferred_element_type=F32)
                dks[h] = dks[h] + lax.dot_general(ds, qhs[h], TN_DIMS, preferred_element_type=F32)
                dq_acc[pl.ds(r0, height), h * HP:(h + 1) * HP] += jnp.dot(ds, ks[h], preferred_element_type=F32)
            return dks[0], dks[1], dv

        z = jnp.zeros((blk, HP), F32)
        below = nq - 1 - kb
        odd = below % 2
        rowi2 = lax.broadcasted_iota(jnp.int32, (2 * blk, blk), 0)
        coli2 = lax.broadcasted_iota(jnp.int32, (2 * blk, blk), 1)
        first = pl.multiple_of(kb * blk, blk)
        carry = lax.cond(odd == 1, lambda c: q_block(first, 2 * blk, c, coli2 <= rowi2),
                         lambda c: q_block(first, blk, c, coli <= rowi), (z, z, jnp.zeros((blk, LANE), F32)))
        dk0, dk1, dv = lax.fori_loop(
            0, below // 2, lambda i, c: q_block(pl.multiple_of((kb + 1 + odd + 2 * i) * blk, blk), 2 * blk, c, None),
            carry)
        dk_o[:, 0:HP] = (dk0 * (scale / Q_PRESCALE)).astype(MXU_DT)
        dk_o[:, HP:2 * HP] = (dk1 * (scale / Q_PRESCALE)).astype(MXU_DT)
        dv_o[...] = dv.astype(MXU_DT)

        @pl.when(kb == nq - 1)
        def _():
            dq_o[...] = (dq_acc[...] * scale).astype(MXU_DT)

    seq_pair = pl.BlockSpec((tp, LANE), lambda b, p, kk: (b, p))
    return _call(
        body, (q, k, v, o, do, lse), carry, grid=(nb, npair, nq),
        in_specs=[pl.BlockSpec((tp, 2 * HP), lambda b, p, kk: (b, p)),
                  pl.BlockSpec((blk, 2 * HP), lambda b, p, kk: (b * nq + kk, p)),
                  pl.BlockSpec((blk, LANE), lambda b, p, kk: (b * nq + kk, p)),
                  seq_pair, seq_pair, pl.BlockSpec((None, tp, LANE), lambda b, p, kk: (p, b, 0))],
        out_specs=[pl.BlockSpec((tp, 2 * HP), lambda b, p, kk: (b, p)),
                   pl.BlockSpec((blk, 2 * HP), lambda b, p, kk: (b * nq + kk, p)),
                   pl.BlockSpec((blk, LANE), lambda b, p, kk: (b * nq + kk, p))],
        out_shape=[S((rows, MLA_HEADS * HP), MXU_DT), S((rows, MLA_HEADS * HP), MXU_DT),
                   S((rows, MLA_HEADS * V_HEAD), MXU_DT)],
        scratch_shapes=[pltpu.VMEM((tp, 2 * HP), F32), pltpu.VMEM((tp, LANE), F32)],
        sem=("parallel", "parallel", "arbitrary"), name=name)


def rope_bwd(dq, dk, dv, tabq, tabk, tp, tm, name):
    rows = dq.shape[0]
    nt = tp // tm
    wq = MLA_HEADS * HP

    def body(dq_r, dk_r, dv_r, tq_r, tk_r, dqa_o, dkv_o, dkr_o):
        dqv = dq_r[...].astype(F32)
        lane = _lane_mod(dqv.shape)
        in_rope = (lane >= QK_NOPE) & (lane < QK_HEAD)
        rope = jnp.where(in_rope, dqv, 0.0)
        da = jnp.where(lane < QK_HEAD, dqv, 0.0) + pltpu.roll(rope, QK_ROPE, axis=1)
        dqa_o[...] = (da * jnp.tile(tq_r[...], (1, MLA_HEADS))).astype(MXU_DT)
        dkf = dk_r[...].astype(F32)
        dkv_o[:, 0:wq] = jnp.where(lane < QK_NOPE, dkf, 0.0).astype(MXU_DT)
        dkv_o[:, wq:] = dv_r[...]
        kr = jnp.where(in_rope, dkf, 0.0)
        tot = kr[:, 0:HP]
        for h in range(1, MLA_HEADS):
            tot = tot + kr[:, h * HP:(h + 1) * HP]
        dkr_o[...] = ((tot + pltpu.roll(tot, QK_ROPE, axis=1)) * tk_r[...]).astype(MXU_DT)

    def rowblk(wd):
        return pl.BlockSpec((tm, wd), lambda i: (i, 0))

    tab = pl.BlockSpec((tm, HP), lambda i: (i % nt, 0))
    return pl.pallas_call(
        body, grid=(rows // tm,), in_specs=[rowblk(wq), rowblk(wq), rowblk(MLA_HEADS * V_HEAD), tab, tab],
        out_specs=[rowblk(wq), rowblk(wq + MLA_HEADS * V_HEAD), rowblk(HP)],
        out_shape=[S((rows, wq), MXU_DT), S((rows, wq + MLA_HEADS * V_HEAD), MXU_DT), S((rows, HP), MXU_DT)],
        compiler_params=_cp(("parallel",)), name=name)(dq, dk, dv, tabq, tabk)


def loss_head(h, target, gain, tp, t_real, tm, name):
    rows = h.shape[0]
    nt = tp // tm

    def body(h_r, t_r, g_r, dh_o, loss_o, dg_o):
        i = pl.program_id(0)

        @pl.when(i == 0)
        def _():
            loss_o[...] = jnp.zeros_like(loss_o)
            dg_o[...] = jnp.zeros_like(dg_o)

        xv = h_r[...]
        rstd = lax.rsqrt(jnp.mean(xv * xv, axis=-1, keepdims=True) + EPS)
        xhat = xv * rstd
        g = g_r[...]
        pos = (i % nt) * tm + lax.broadcasted_iota(jnp.int32, (tm, 1), 0)
        valid = (pos >= N_META) & (pos < t_real)
        err = jnp.where(valid, xhat * g - t_r[...], 0.0)
        loss_o[...] += 0.5 * jnp.sum(jnp.mean(err * err, axis=-1, keepdims=True))
        dy = err * (1.0 / D_MODEL)
        dg_o[...] += jnp.sum(dy * xhat, axis=0, keepdims=True)
        dxh = dy * g
        dh_o[...] = rstd * (dxh - xhat * jnp.mean(dxh * xhat, axis=-1, keepdims=True))

    blk = pl.BlockSpec((tm, D_MODEL), lambda i: (i, 0))
    return pl.pallas_call(
        body, grid=(rows // tm,), in_specs=[blk, blk, pl.BlockSpec((1, D_MODEL), lambda i: (0, 0))],
        out_specs=[blk, pl.BlockSpec((1, LANE), lambda i: (0, 0)), pl.BlockSpec((1, D_MODEL), lambda i: (0, 0))],
        out_shape=[S((rows, D_MODEL), F32), S((1, LANE), F32), S((1, D_MODEL), F32)],
        compiler_params=_cp(("arbitrary",)), name=name)(h, target, gain)


ADAM_TILE_ELEMS = 128 * 1024


def adamw(g, w, m, v, name, carry=None):
    shape = w.shape
    cols = shape[-1]
    rws = max(1, math.prod(shape[:-1]))
    tr = rws if rws * cols <= ADAM_TILE_ELEMS else _div_tile(rws, max(SUBLANE, ADAM_TILE_ELEMS // cols), SUBLANE)
    bc1 = 1.0 - ADAM_B1 ** ADAM_STEP
    bc2 = 1.0 - ADAM_B2 ** ADAM_STEP

    def body(g_r, w_r, m_r, v_r, go, do, mo, vo):
        gv = g_r[...]
        mn = ADAM_B1 * m_r[...] + (1.0 - ADAM_B1) * gv
        vn = ADAM_B2 * v_r[...] + (1.0 - ADAM_B2) * (gv * gv)
        m_hat = mn / bc1
        v_hat = vn / bc2
        go[...] = gv
        do[...] = -ADAM_LR * (m_hat / (jnp.sqrt(v_hat) + ADAM_EPS) + ADAM_WD * w_r[...])
        mo[...] = mn
        vo[...] = vn

    blk = pl.BlockSpec((tr, cols), lambda i: (i, 0))
    outs, got = _call(
        body, [a.reshape(rws, cols) for a in (g, w, m, v)], carry, grid=(rws // tr,), in_specs=[blk] * 4,
        out_specs=[blk] * 4, out_shape=[S((rws, cols), F32)] * 4, scratch_shapes=[], sem=("parallel",), name=name)
    return tuple(o.reshape(shape) for o in outs), got


SUM_TILE_ELEMS = 128 * 1024


def _place():
    return lax.axis_index("x"), lax.axis_index("y"), lax.axis_index("c")


def _remote(src, dst, send_sems, recv_sems, k, to):
    return pltpu.make_async_remote_copy(src_ref=src, dst_ref=dst, send_sem=send_sems.at[k], recv_sem=recv_sems.at[k],
                                        device_id=to, device_id_type=MESH)


def chip_index():
    return 2 * lax.axis_index("x") + lax.axis_index("y")


def _sem_pair(n):
    return [pltpu.SemaphoreType.DMA((n,)), pltpu.SemaphoreType.DMA((n,))]


def stage_gather_chips(xs):
    def copies(ins, outs, sems):
        send_sems, recv_sems = sems
        mx, my, mc = _place()
        sibling = (mx, my, 1 - mc)
        chips = [(1 - mx, my), (mx, 1 - my), (1 - mx, 1 - my)]
        first, landed, passed, from_sibling = [], [], [], []
        for i, (x_ref, out_ref) in enumerate(zip(ins, outs)):
            def piece(cx, cy, h, out_ref=out_ref):
                return out_ref.at[2 * cx + cy, h]

            for j, (cx, cy) in enumerate(chips):
                k = 6 * i + j
                first.append(_remote(x_ref.at[mc], piece(mx, my, mc), send_sems, recv_sems, k, (cx, cy, mc)))
                landed.append(_remote(x_ref.at[mc], piece(cx, cy, mc), send_sems, recv_sems, k, (cx, cy, mc)))
                passed.append(_remote(piece(cx, cy, mc), piece(cx, cy, mc), send_sems, recv_sems, k + 3, sibling))
                from_sibling.append(_remote(x_ref.at[mc], piece(cx, cy, 1 - mc), send_sems, recv_sems, k + 3, sibling))
        return first, landed, passed, from_sibling

    def start(ins, outs, sems):
        for cp in copies(ins, outs, sems)[0]:
            cp.start()

    def finish(ins, outs, sems):
        first, landed, passed, from_sibling = copies(ins, outs, sems)
        for arrived, onward in zip(landed, passed):
            arrived.wait_recv()
            onward.start()
        for cp in from_sibling:
            cp.wait_recv()
        for cp in first + passed:
            cp.wait_send()

    return Stage(list(xs), [S((4,) + x.shape, x.dtype) for x in xs], _sem_pair(6 * len(xs)), start, finish)


def own_block(gathered, xs):
    return lax.dynamic_update_slice(gathered, xs[None], (chip_index(), 0, 0, 0))


def stage_pair_exchange(gs):
    def copies(ins, outs, sems):
        send_sems, recv_sems = sems
        mx, my, mc = _place()
        return [_remote(g_ref.at[s, 1 - mc], land_ref.at[s], send_sems, recv_sems, 4 * i + s, (mx, my, 1 - mc))
                for i, (g_ref, land_ref) in enumerate(zip(ins, outs)) for s in range(4)]

    def start(ins, outs, sems):
        for cp in copies(ins, outs, sems):
            cp.start()

    def finish(ins, outs, sems):
        cps = copies(ins, outs, sems)
        for cp in cps:
            cp.wait_recv()
        for cp in cps:
            cp.wait_send()

    return Stage(list(gs), [S((4,) + g.shape[2:], g.dtype) for g in gs], _sem_pair(4 * len(gs)), start, finish)


def _sum_rows(rws, width):
    return _div_tile(rws, max(SUBLANE, SUM_TILE_ELEMS // width), SUBLANE)


def pair_sum(g4, land, c_idx, name):
    _, _, rws, wd = g4.shape
    th = _sum_rows(rws, wd)

    def body(c_ref, a_ref, b_ref, o_ref):
        o_ref[...] = a_ref[...] + b_ref[...]

    return pl.pallas_call(
        body,
        grid_spec=pltpu.PrefetchScalarGridSpec(
            num_scalar_prefetch=1, grid=(4, rws // th),
            in_specs=[pl.BlockSpec((None, None, th, wd), lambda s, i, c: (s, c[0], i, 0)),
                      pl.BlockSpec((None, th, wd), lambda s, i, c: (s, i, 0))],
            out_specs=pl.BlockSpec((None, th, wd), lambda s, i, c: (s, i, 0))),
        out_shape=S((4, rws, wd), F32), compiler_params=_cp(("parallel", "parallel")), name=name)(c_idx, g4, land)


def stage_chip_scatter(ps):
    def copies(ins, outs, sems):
        send_sems, recv_sems = sems
        mx, my, mc = _place()
        me = 2 * mx + my
        chips = [(1 - mx, my), (mx, 1 - my), (1 - mx, 1 - my)]
        sent, landed = [], []
        for i, (p_ref, land_ref) in enumerate(zip(ins, outs)):
            for j, (cx, cy) in enumerate(chips):
                k = 3 * i + j
                sent.append(_remote(p_ref.at[2 * cx + cy], land_ref.at[me], send_sems, recv_sems, k, (cx, cy, mc)))
                landed.append(_remote(p_ref.at[me], land_ref.at[2 * cx + cy], send_sems, recv_sems, k, (cx, cy, mc)))
        return sent, landed

    def start(ins, outs, sems):
        for cp in copies(ins, outs, sems)[0]:
            cp.start()

    def finish(ins, outs, sems):
        sent, landed = copies(ins, outs, sems)
        for cp in landed:
            cp.wait_recv()
        for cp in sent:
            cp.wait_send()

    return Stage(list(ps), [S(p.shape, p.dtype) for p in ps], _sem_pair(3 * len(ps)), start, finish)


def chip_sum(l4, p4, me_idx, name):
    _, rws, wd = l4.shape
    th = _sum_rows(rws, wd)

    def body(me_ref, a, b, c, d, own, o_ref):
        me = me_ref[0]
        parts = [jnp.where(me == s, own[...], r[...]) for s, r in enumerate((a, b, c, d))]
        o_ref[...] = ((parts[0] + parts[1]) + parts[2]) + parts[3]

    def blk(s):
        return pl.BlockSpec((None, th, wd), lambda i, me: (jnp.where(me[0] == s, (s + 1) % 4, s), i, 0))

    return pl.pallas_call(
        body,
        grid_spec=pltpu.PrefetchScalarGridSpec(
            num_scalar_prefetch=1, grid=(rws // th,),
            in_specs=[blk(0), blk(1), blk(2), blk(3), pl.BlockSpec((None, th, wd), lambda i, me: (me[0], i, 0))],
            out_specs=pl.BlockSpec((th, wd), lambda i, me: (i, 0))),
        out_shape=S((rws, wd), F32), compiler_params=_cp(("parallel",)), name=name)(me_idx, l4, l4, l4, l4, p4)


def stage_pair_gather(rs):
    def copies(ins, outs, sems):
        send_sems, recv_sems = sems
        mx, my, mc = _place()
        return [_remote(r_ref, out_ref, send_sems, recv_sems, i, (mx, my, 1 - mc))
                for i, (r_ref, out_ref) in enumerate(zip(ins, outs))]

    def start(ins, outs, sems):
        for cp in copies(ins, outs, sems):
            cp.start()

    def finish(ins, outs, sems):
        for cp in copies(ins, outs, sems):
            cp.wait()

    return Stage(list(rs), [S(r.shape, r.dtype) for r in rs], _sem_pair(len(rs)), start, finish)


PACK_ELEMS = 16 * LANE


def pack_rows(arrays, lead, total_mult):
    parts, offs, r0 = [], [], 0
    for a in arrays:
        flat = a.reshape(a.shape[:lead] + (-1,))
        elems = _round_up(flat.shape[-1], PACK_ELEMS)
        flat = jnp.pad(flat, [(0, 0)] * lead + [(0, elems - flat.shape[-1])])
        parts.append(flat.reshape(flat.shape[:lead] + (elems // LANE, LANE)))
        offs.append((r0, elems // LANE))
        r0 += elems // LANE
    total = _round_up(r0, total_mult)
    if total > r0:
        parts.append(jnp.zeros(parts[0].shape[:lead] + (total - r0, LANE), parts[0].dtype))
    return jnp.concatenate(parts, axis=lead), offs


def unpack_rows(buf, off, shape):
    r0, nr = off
    lead = buf.shape[:-2]
    n = math.prod(shape)
    return buf[..., r0:r0 + nr, :].reshape(lead + (nr * LANE,))[..., :n].reshape(lead + tuple(shape))


def unshard(stacked, axis):
    x = jnp.moveaxis(stacked, 0, axis)
    return x.reshape(x.shape[:axis] + (4 * x.shape[axis + 1],) + x.shape[axis + 2:])


def to_shards(full, axis):
    n = full.shape[axis] // 4
    x = full.reshape(full.shape[:axis] + (4, n) + full.shape[axis + 1:])
    return jnp.moveaxis(x, axis, 0)


def _rot_cols(w):
    half = w.shape[-1] // 2
    return jnp.concatenate([-w[..., half:], w[..., :half]], axis=-1)


def _unrot_cols(dw):
    half = dw.shape[-1] // 2
    return jnp.concatenate([dw[..., half:], -dw[..., :half]], axis=-1)


def odd_w_in_padded(w_in):
    kr = w_in[:, Q_LORA + KV_LORA:]
    rows = w_in.shape[0]
    return jnp.concatenate([w_in[:, :Q_LORA], jnp.zeros((rows, 128), w_in.dtype), w_in[:, Q_LORA:Q_LORA + KV_LORA],
                            jnp.zeros((rows, 64), w_in.dtype), kr, _rot_cols(kr)], axis=1)


def odd_w_in_unpad(dwp):
    base = 512 + KV_LORA + 64
    dkr = dwp[:, base:base + QK_ROPE] + _unrot_cols(dwp[:, base + QK_ROPE:base + 2 * QK_ROPE])
    return jnp.concatenate([dwp[:, :Q_LORA], dwp[:, 512:512 + KV_LORA], dkr], axis=1)


def uq_padded(w_uq):
    w = w_uq.reshape(Q_LORA, MLA_HEADS, QK_HEAD)
    return jnp.concatenate([w, _rot_cols(w[:, :, QK_NOPE:])], axis=-1).reshape(Q_LORA, MLA_HEADS * HP)


def uq_unpad(dwp):
    d = dwp.reshape(Q_LORA, MLA_HEADS, HP)
    rope = d[:, :, QK_NOPE:QK_HEAD] + _unrot_cols(d[:, :, QK_HEAD:])
    return jnp.concatenate([d[:, :, :QK_NOPE], rope], axis=-1).reshape(Q_LORA, MLA_HEADS * QK_HEAD)


def ukv_padded(w_ukv):
    w = w_ukv.reshape(KV_LORA, MLA_HEADS, QK_NOPE + V_HEAD)
    wk = jnp.concatenate([w[:, :, :QK_NOPE], jnp.zeros((KV_LORA, MLA_HEADS, HP - QK_NOPE), w.dtype)], axis=-1)
    return jnp.concatenate([wk.reshape(KV_LORA, MLA_HEADS * HP), w[:, :, QK_NOPE:].reshape(KV_LORA, MLA_HEADS * V_HEAD)],
                           axis=1)


def ukv_unpad(dwp):
    dk = dwp[:, :MLA_HEADS * HP].reshape(KV_LORA, MLA_HEADS, HP)[:, :, :QK_NOPE]
    dv = dwp[:, MLA_HEADS * HP:].reshape(KV_LORA, MLA_HEADS, V_HEAD)
    return jnp.concatenate([dk, dv], axis=-1).reshape(KV_LORA, MLA_HEADS * (QK_NOPE + V_HEAD))


def block_diag(w):
    h, d, _ = w.shape
    eye = jnp.eye(h, dtype=w.dtype)
    return (eye[:, None, :, None] * w[:, :, None, :]).reshape(h * d, h * d)


def block_diag_part(dense, h):
    d = dense.shape[0] // h
    x = dense.reshape(h, d, h, d)
    return jnp.stack([x[i, :, i, :] for i in range(h)], axis=0)


def rope_tables(tp):
    pos = jnp.arange(tp, dtype=F32)
    inv_freq = ROPE_BASE ** (-jnp.arange(0, QK_ROPE, 2, dtype=F32) / QK_ROPE)
    ang = pos[:, None] * inv_freq[None, :]
    cos2 = jnp.tile(jnp.cos(ang), (1, 2))
    sin2 = jnp.tile(jnp.sin(ang), (1, 2))
    tabq = jnp.concatenate([jnp.ones((tp, QK_NOPE), F32), cos2, sin2], axis=1)
    tabk = jnp.concatenate([jnp.zeros((tp, QK_NOPE), F32), cos2, sin2], axis=1)
    return tabq, tabk


class Dims:
    def __init__(self, nb, seq):
        self.nb = nb
        self.t_real = seq + N_META
        self.tp = _round_up(self.t_real, ATT_BLK)
        self.n = self.tp // 4
        assert self.n % 16 == 0
        self.rows = nb * self.tp


class NoComm:
    def advance(self, carried):
        return None


def even_fwd(h, p, dm, comm):
    (u, hn), _ = norm_matmul(h, 0, D_MODEL, p["norm"], p["w_in"], dm.n, 512, F32, "ev_in")
    (y, ca, xc, a, hs), got = even_mid_fwd(u, p["conv_a"], p["conv_b"], p["conv_b_bias"], p["rw"], p["r_b"], p["iw"],
                                           p["i_b"], p["lam"], dm.nb, dm.tp, dm.n, "ev_mid", carry=comm.advance(None))
    comm.advance(got)
    out = matmul_res(y, p["w_out"].reshape(2, CONV_W, D_MODEL), h, dm.n, 512, "ev_out")
    return out, (h, u, hn, ca, xc, a, hs, y)


def even_bwd(dout, saved, p, dm, comm):
    h, u, hn, ca, xc, a, hs, y = saved
    g = {}
    dycat = matmul_nt(dout, p["w_out"], dm.n, 512, F32, "ev_dycat")
    g["w_out"], got = matmul_tn(y, dout, dm.n, "ev_dw_out", carry=comm.advance(None))
    outs, got = even_mid_bwd(u, dycat, ca, xc, a, hs, p["conv_a"], p["conv_b"], p["rw"], p["r_b"], p["iw"], p["i_b"],
                             p["lam"], dm.nb, dm.tp, dm.n, "ev_mid_bwd", carry=comm.advance(got))
    du, g["conv_a"], g["conv_b"], g["conv_b_bias"], drw, g["r_b"], diw, g["i_b"], g["lam"] = outs
    g["r_w"] = block_diag_part(drw, LRU_HEADS)
    g["i_w"] = block_diag_part(diw, LRU_HEADS)
    g["w_in"], got = matmul_tn(hn, du, dm.n, "ev_dw_in", carry=comm.advance(got))
    comm.advance(got)
    dx, g["norm"] = matmul_nt_normbwd(du, p["w_in"], h, 0, p["norm"], dout, dm.n, 512, F32, "ev_dx")
    return dx, g


def odd_fwd(h, p, tabq, tabk, dm, comm):
    nt = dm.tp // dm.n
    (u, hn), _ = norm_matmul(h, 0, D_MODEL, p["norm"], p["w_in_p"], dm.n, ODD_PAD, F32, "od_in")
    tab_spec = pl.BlockSpec((dm.n, HP), lambda i, j: (i % nt, 0))
    (q, cqn), _ = norm_matmul(u, 0, Q_LORA, p["q_norm"], p["w_uq_p"], dm.n, 512, MXU_DT, "od_q",
                              epi=_q_rope_epi, epi_ops=(tabq,), epi_specs=(tab_spec,))
    kr_spec = pl.BlockSpec((dm.n, HP), lambda i, j: (i, ODD_KR_COL))
    (k, ckvn), _ = norm_matmul(u, ODD_CKV_COL, KV_LORA, p["kv_norm"], p["w_uk_p"], dm.n, 512, MXU_DT, "od_k",
                               epi=_k_rope_epi, epi_ops=(u, tabk), epi_specs=(kr_spec, tab_spec))
    (v, _), _ = norm_matmul(u, ODD_CKV_COL, KV_LORA, p["kv_norm"], p["w_uv_p"], dm.n, 512, MXU_DT, "od_v")
    (o, lse), got = attn_fwd(q, k, v, dm.nb, dm.tp, "od_attn", carry=comm.advance(None))
    comm.advance(got)
    out = matmul_res(o[None], p["w_out"][None], h, dm.n, 512, "od_out")
    return out, (h, u, hn, cqn, ckvn, q, k, v, o, lse)


def odd_bwd(dout, saved, p, tabq, tabk, dm, comm):
    h, u, hn, cqn, ckvn, q, k, v, o, lse = saved
    g = {}
    do = matmul_nt(dout, p["w_out"], dm.n, 512, MXU_DT, "od_do")
    g["w_out"], got = matmul_tn(o, dout, dm.n, "od_dw_out", carry=comm.advance(None))
    (dq, dk, dv), got = attn_bwd(q, k, v, o, do, lse, dm.nb, dm.tp, "od_attn_bwd", carry=comm.advance(got))
    dqa, dkv, dkr = rope_bwd(dq, dk, dv, tabq, tabk, dm.tp, dm.n, "od_rope_bwd")
    g["w_uq_p"], got = matmul_tn(cqn, dqa, dm.n, "od_dw_uq", carry=comm.advance(got))
    comm.advance(got)
    g["w_ukv_p"], _ = matmul_tn(ckvn, dkv, dm.n, "od_dw_ukv")
    dcq, g["q_norm"] = matmul_nt_normbwd(dqa, p["w_uq_p"], u, 0, p["q_norm"], None, dm.n, 512, MXU_DT, "od_dcq")
    dckv, g["kv_norm"] = matmul_nt_normbwd(dkv, p["w_ukv_p"], u, ODD_CKV_COL, p["kv_norm"], None, dm.n, 512, MXU_DT,
                                           "od_dckv")
    du = jnp.concatenate([dcq, jnp.zeros((dm.rows, 128), MXU_DT), dckv, dkr], axis=1)
    g["w_in_p"], _ = matmul_tn(hn, du, dm.n, "od_dw_in")
    dx, g["norm"] = matmul_nt_normbwd(du, p["w_in_p"], h, 0, p["norm"], dout, dm.n, ODD_PAD, F32, "od_dx")
    return dx, g


def ffn_fwd(h, p, dm, comm):
    (up, hn), got = norm_matmul(h, 0, D_MODEL, p["norm"], p["w_up"], dm.n, D_FF // 2, MXU_DT, "ffn_up",
                                carry=comm.advance(None))
    comm.advance(got)
    u, y = ffn_mid_fwd(up, p["cw"], p["cb"], dm.nb, dm.tp, dm.n, "ffn_mid")
    out = matmul_res(y[None], p["w_down"][None], h, dm.n, 512, "ffn_down")
    return out, (h, up, hn, u, y)


def ffn_bwd(dout, saved, p, dm, comm):
    h, up, hn, u, y = saved
    g = {}
    dy = matmul_nt(dout, p["w_down"], dm.n, D_FF // 2, MXU_DT, "ffn_dy")
    g["w_down"], got = matmul_tn(y, dout, dm.n, "ffn_dw_down", carry=comm.advance(None))
    (dup, g["cw"], g["cb"]), got = ffn_mid_bwd(dy, u, up, p["cw"], dm.nb, dm.tp, dm.n, "ffn_mid_bwd",
                                               carry=comm.advance(got))
    g["w_up"], got = matmul_tn(hn, dup, dm.n, "ffn_dw_up", carry=comm.advance(got), col_shards=4)
    comm.advance(got)
    dx, g["norm"] = matmul_nt_normbwd(dup, p["w_up"], h, 0, p["norm"], dout, dm.n, D_FF // 2, F32, "ffn_dx")
    return dx, g


def _row(v):
    return v.reshape(1, -1)


def even_params(wf, j):
    return dict(norm=_row(wf["ev_norm"][j]), w_in=wf["ev_w_in"], conv_a=wf["ev_conv_a"][j], conv_b=wf["ev_conv_b"][j],
                conv_b_bias=_row(wf["ev_conv_b_bias"][j]), rw=block_diag(wf["ev_gate_r_w"][j]).astype(MXU_DT),
                r_b=_row(wf["ev_gate_r_b"][j]), iw=block_diag(wf["ev_gate_i_w"][j]).astype(MXU_DT),
                i_b=_row(wf["ev_gate_i_b"][j]), lam=_row(wf["ev_lru_lambda"][j]), w_out=wf["ev_w_out"])


def odd_params(wf, j):
    wkv = ukv_padded(wf["od_w_ukv"])
    return dict(norm=_row(wf["od_norm"][j]), w_in_p=odd_w_in_padded(wf["od_w_in"]), q_norm=_row(wf["od_q_norm"][j]),
                kv_norm=_row(wf["od_kv_norm"][j]), w_uq_p=uq_padded(wf["od_w_uq"]), w_ukv_p=wkv,
                w_uk_p=wkv[:, :MLA_HEADS * HP], w_uv_p=wkv[:, MLA_HEADS * HP:], w_out=wf["od_w_out"])


def ffn_params(wf, layer):
    return dict(norm=_row(wf["ffn_norm"][layer]), w_up=wf["ffn_w_up"],
                cw=jnp.moveaxis(wf["ffn_conv_w"][layer].reshape(3, 2, D_FF), 1, 0),
                cb=wf["ffn_conv_b"][layer].reshape(2, 1, D_FF), w_down=wf["ffn_w_down"])


def even_grads(g):
    out = {"ev_" + k_: g[k_] for k_ in ("w_in", "conv_a", "conv_b", "w_out")}
    out.update({"ev_norm": g["norm"][0], "ev_conv_b_bias": g["conv_b_bias"][0], "ev_gate_r_w": g["r_w"],
                "ev_gate_r_b": g["r_b"][0], "ev_gate_i_w": g["i_w"], "ev_gate_i_b": g["i_b"][0],
                "ev_lru_lambda": g["lam"][0]})
    return out


def odd_grads(g):
    return {"od_norm": g["norm"][0], "od_q_norm": g["q_norm"][0], "od_kv_norm": g["kv_norm"][0],
            "od_w_in": odd_w_in_unpad(g["w_in_p"]), "od_w_uq": uq_unpad(g["w_uq_p"]),
            "od_w_ukv": ukv_unpad(g["w_ukv_p"]), "od_w_out": g["w_out"]}


def ffn_grads(g):
    return {"ffn_norm": g["norm"][0], "ffn_w_up": g["w_up"], "ffn_conv_w": jnp.moveaxis(g["cw"], 0, 1).reshape(3, 2 * D_FF),
            "ffn_conv_b": g["cb"].reshape(2 * D_FF), "ffn_w_down": g["w_down"]}


WEIGHTS = ["meta_tokens", "ev_norm", "ev_w_in", "ev_conv_a", "ev_conv_b", "ev_conv_b_bias", "ev_gate_r_w", "ev_gate_r_b",
           "ev_gate_i_w", "ev_gate_i_b", "ev_lru_lambda", "ev_w_out", "od_norm", "od_w_in", "od_q_norm", "od_kv_norm",
           "od_w_uq", "od_w_ukv", "od_w_out", "ffn_norm", "ffn_w_up", "ffn_conv_w", "ffn_conv_b", "ffn_w_down",
           "final_norm"]
SHARD_AXIS = {"meta_tokens": 1, "ev_w_in": 2, "ev_conv_a": 2, "ev_conv_b": 2, "ev_w_out": 1, "od_norm": 1, "od_w_in": 1,
              "od_q_norm": 1, "od_kv_norm": 1, "od_w_uq": 2, "od_w_ukv": 2, "od_w_out": 1, "ffn_w_up": 2,
              "ffn_conv_w": 2, "ffn_w_down": 1}
MATMUL_WEIGHTS = ["ev_w_in", "ev_w_out", "od_w_in", "od_w_uq", "od_w_ukv", "od_w_out", "ffn_w_up", "ffn_w_down"]


LAYER_ORDER = [("ev", 0), ("ffn", 0), ("od", 0), ("ffn", 1), ("ev", 1), ("ffn", 2), ("od", 1), ("ffn", 3)]
LAYER_MATMUL = {"ev": ["ev_w_in", "ev_w_out"], "od": ["od_w_in", "od_w_uq", "od_w_ukv", "od_w_out"],
                "ffn": ["ffn_w_up", "ffn_w_down"]}
LAYER_SHARDED = {"ev": ["ev_w_in", "ev_conv_a", "ev_conv_b", "ev_w_out"],
                 "od": ["od_norm", "od_w_in", "od_q_norm", "od_kv_norm", "od_w_uq", "od_w_ukv", "od_w_out"],
                 "ffn": ["ffn_w_up", "ffn_conv_w", "ffn_w_down"]}
STACKED_SHARDS = "ffn_w_up"


def gather_all_layers(w, names, name):
    buf, offs = pack_rows([w[n] for n in names], 0, 32)
    halves = buf.reshape(2, buf.shape[0] // 2, LANE)
    got = own_block(run_stage(stage_gather_chips([halves]), name)[0], halves).reshape(4, buf.shape[0], LANE)
    return {n: unshard(unpack_rows(got, off, w[n].shape), SHARD_AXIS[n]) for n, off in zip(names, offs)}


def _halves(a):
    return a.reshape(2, a.shape[0] // 2, a.shape[1])


class GatherComm:
    def __init__(self, w, kind, idx):
        self.names = LAYER_MATMUL[kind]
        self.halves = [_halves(w[n][idx].astype(MXU_DT)) for n in self.names]
        self.stage = stage_gather_chips(self.halves)
        self.step, self.got = 0, None

    def advance(self, carried):
        self.step += 1
        if self.step == 1:
            return self.stage
        if self.step == 2:
            self.got = carried
        return None

    def run_alone(self, name):
        self.advance(run_stage(self.advance(None), name))

    def weights(self):
        out = {}
        for n, got, own in zip(self.names, self.got, self.halves):
            stacked = own_block(got, own).reshape(4, 2 * own.shape[1], own.shape[2])
            out[n] = stacked if n == STACKED_SHARDS else unshard(stacked, SHARD_AXIS[n] - 1)
        return out


class ReduceComm:
    def __init__(self, grads, axes, c_idx, tag, tail=None):
        shards = {n: grads[n] if n == STACKED_SHARDS else to_shards(grads[n], axes[n]) for n in grads}
        self.big = [n for n in grads if n in MATMUL_WEIGHTS]
        self.small = [n for n in grads if n not in MATMUL_WEIGHTS]
        self.shapes = {n: shards[n].shape[1:] for n in grads}
        arrays = [shards[n].reshape(4, 2, shards[n].shape[1] // 2, shards[n].shape[2]) for n in self.big]
        gs, self.offs = pack_rows([shards[n] for n in self.small], 1, 16)
        self.rs = gs.shape[1] // 2
        parts = [gs.reshape(4, 2, self.rs, LANE)]
        self.rr = 0
        if tail is not None:
            self.rr = tail.shape[0] // 8
            parts.append(tail.reshape(4, 2, self.rr, LANE))
        arrays.append(jnp.concatenate(parts, axis=2) if len(parts) > 1 else parts[0])
        self.arrays, self.c_idx, self.tag, self.step = arrays, c_idx, tag, 0
        self.part = self.mine = self.theirs = None

    def advance(self, carried):
        self.step += 1
        if self.step == 1:
            return stage_pair_exchange(self.arrays)
        if self.step == 2:
            self.part = [pair_sum(g, land, self.c_idx, "grad_pair_sum_%s_%d" % (self.tag, i))
                         for i, (g, land) in enumerate(zip(self.arrays, carried))]
            return stage_chip_scatter(self.part)
        if self.step == 3:
            me_idx = chip_index().astype(jnp.int32).reshape(1)
            self.mine = [chip_sum(land, part, me_idx, "grad_chip_sum_%s_%d" % (self.tag, i))
                         for i, (land, part) in enumerate(zip(carried, self.part))]
            return stage_pair_gather(self.mine)
        if self.step == 4:
            self.theirs = carried
        return None

    def run_alone(self, name):
        stage = self.advance(None)
        while stage is not None:
            stage = self.advance(run_stage(stage, name + "_%d" % self.step))

    def results(self):
        south = self.c_idx[0] == 0
        boths = [jnp.stack([jnp.where(south, m, t), jnp.where(south, t, m)], axis=0)
                 for m, t in zip(self.mine, self.theirs)]
        out = {n: b.reshape(self.shapes[n]) for n, b in zip(self.big, boths)}
        packed = boths[-1]
        flat = packed[:, :self.rs].reshape(2 * self.rs, LANE)
        out.update({n: unpack_rows(flat, off, self.shapes[n]) for n, off in zip(self.small, self.offs)})
        return out, packed[:, self.rs:self.rs + self.rr]


def kernel(x, meta_tokens, ev_norm, ev_w_in, ev_conv_a, ev_conv_b, ev_conv_b_bias, ev_gate_r_w, ev_gate_r_b, ev_gate_i_w, ev_gate_i_b, ev_lru_lambda, ev_w_out, od_norm, od_w_in, od_q_norm, od_kv_norm, od_w_uq, od_w_ukv, od_w_out, ffn_norm, ffn_w_up, ffn_conv_w, ffn_conv_b, ffn_w_down, final_norm, loss_target, m_meta_tokens, m_ev_norm, m_ev_w_in, m_ev_conv_a, m_ev_conv_b, m_ev_conv_b_bias, m_ev_gate_r_w, m_ev_gate_r_b, m_ev_gate_i_w, m_ev_gate_i_b, m_ev_lru_lambda, m_ev_w_out, m_od_norm, m_od_w_in, m_od_q_norm, m_od_kv_norm, m_od_w_uq, m_od_w_ukv, m_od_w_out, m_ffn_norm, m_ffn_w_up, m_ffn_conv_w, m_ffn_conv_b, m_ffn_w_down, m_final_norm, v_meta_tokens, v_ev_norm, v_ev_w_in, v_ev_conv_a, v_ev_conv_b, v_ev_conv_b_bias, v_ev_gate_r_w, v_ev_gate_r_b, v_ev_gate_i_w, v_ev_gate_i_b, v_ev_lru_lambda, v_ev_w_out, v_od_norm, v_od_w_in, v_od_q_norm, v_od_kv_norm, v_od_w_uq, v_od_w_ukv, v_od_w_out, v_ffn_norm, v_ffn_w_up, v_ffn_conv_w, v_ffn_conv_b, v_ffn_w_down, v_final_norm):
    given = dict(locals())
    w = {n: given[n] for n in WEIGHTS}
    nb, seq, _ = x.shape
    dm = Dims(nb, seq)
    n_layers = len(LAYER_ORDER)

    wf = {n: w[n] for n in WEIGHTS if n not in SHARD_AXIS}
    wf.update(gather_all_layers(w, [n for n in SHARD_AXIS if n not in MATMUL_WEIGHTS], "gather_small_weights"))
    gathers = [GatherComm(w, kind, idx) for kind, idx in LAYER_ORDER]
    gathers[0].run_alone("gather_first_layer")

    tail = dm.tp - dm.t_real
    meta = jnp.broadcast_to(wf["meta_tokens"][None], (nb, N_META, D_MODEL))
    h = jnp.concatenate([meta, x, jnp.zeros((nb, tail, D_MODEL), F32)], axis=1).reshape(dm.rows, D_MODEL)
    tgt = jnp.pad(loss_target, ((0, 0), (N_META, tail), (0, 0))).reshape(dm.rows, D_MODEL)
    tabq, tabk = rope_tables(dm.tp)

    params, saved = [], []
    for i, (kind, idx) in enumerate(LAYER_ORDER):
        wl = dict(wf)
        wl.update(gathers[i].weights())
        comm = gathers[i + 1] if i + 1 < n_layers else NoComm()
        if kind == "ev":
            p = even_params(wl, idx)
            h, sv = even_fwd(h, p, dm, comm)
        elif kind == "od":
            p = odd_params(wl, idx)
            h, sv = odd_fwd(h, p, tabq, tabk, dm, comm)
        else:
            p = ffn_params(wl, idx)
            h, sv = ffn_fwd(h, p, dm, comm)
        params.append(p)
        saved.append(sv)

    dh, loss, dfinal = loss_head(h, tgt, _row(wf["final_norm"]), dm.tp, dm.t_real, dm.n, "loss_head")
    loss = lax.psum(loss[0, 0], ("x", "y", "c"))

    c_idx = lax.axis_index("c").astype(jnp.int32).reshape(1)
    layer_grads = {n: {} for n in WEIGHTS}
    pending, reduces = NoComm(), []
    for i in reversed(range(n_layers)):
        kind, idx = LAYER_ORDER[i]
        if kind == "ev":
            dh, g = even_bwd(dh, saved[i], params[i], dm, pending)
            g = even_grads(g)
        elif kind == "od":
            dh, g = odd_bwd(dh, saved[i], params[i], tabq, tabk, dm, pending)
            g = odd_grads(g)
        else:
            dh, g = ffn_bwd(dh, saved[i], params[i], dm, pending)
            g = ffn_grads(g)
        for n in g:
            if n not in SHARD_AXIS:
                layer_grads[n][idx] = g[n]
        if i > 0:
            pending = ReduceComm({n: g[n] for n in LAYER_SHARDED[kind]}, {n: SHARD_AXIS[n] - 1 for n in SHARD_AXIS},
                                 c_idx, "%s%d" % (kind, idx))
            reduces.append((pending, idx))
    dh3 = dh.reshape(nb, dm.tp, D_MODEL)
    grad_x = dh3[:, N_META:dm.t_real]

    repl = [n for n in WEIGHTS if n not in SHARD_AXIS]
    layer_grads["final_norm"] = {0: dfinal[0]}
    repl_full = {n: (layer_grads[n][0] if n == "final_norm" else
                     jnp.stack([layer_grads[n][j] for j in range(w[n].shape[0])], axis=0)) for n in repl}
    tail_buf, tail_offs = pack_rows([repl_full[n] for n in repl], 0, 64)
    first = {n: g[n] for n in LAYER_SHARDED["ev"]}
    first["meta_tokens"] = jnp.sum(dh3[:, :N_META], axis=0)
    axes = {n: SHARD_AXIS[n] - 1 for n in SHARD_AXIS}
    axes["meta_tokens"] = SHARD_AXIS["meta_tokens"]
    last = ReduceComm(first, axes, c_idx, "first_layer", tail=tail_buf)
    last.run_alone("grad_first_layer")
    reduces.append((last, 0))

    red = {}
    for comm, idx in reduces:
        got, tail_piece = comm.results()
        for n, v_ in got.items():
            if n == "meta_tokens":
                red[n] = v_
            else:
                layer_grads[n][idx] = v_
    tails = own_block(run_stage(stage_gather_chips([tail_piece]), "grad_gather_replicated")[0], tail_piece)
    tails = tails.reshape(tail_buf.shape[0], LANE)
    for n, off in zip(repl, tail_offs):
        red[n] = unpack_rows(tails, off, w[n].shape)
    for n in SHARD_AXIS:
        if n != "meta_tokens":
            red[n] = jnp.stack([layer_grads[n][j] for j in range(w[n].shape[0])], axis=0)

    outs = [adamw(red[n], w[n], given["m_" + n], given["v_" + n], "adamw_" + n)[0] for n in WEIGHTS]
    return (loss, grad_x, *[o[0] for o in outs], *[o[1] for o in outs], *[o[2] for o in outs], *[o[3] for o in outs])
```

```python
import math

import jax
import jax.numpy as jnp
from jax import lax
from jax.experimental import pallas as pl
from jax.experimental.pallas import tpu as pltpu

F32 = jnp.float32
MXU_DT = jnp.bfloat16
S = jax.ShapeDtypeStruct
MESH = pl.DeviceIdType.MESH

EPS = 1e-6
D_MODEL = 1024
N_META = 16
DEPTH = 4
CONV_W = 512
LRU_W = 512
LRU_HEADS = 8
LRU_C = 8.0
EVEN_IN = 2560
MLA_HEADS = 16
QK_NOPE = 64
QK_ROPE = 32
QK_HEAD = 96
V_HEAD = 64
Q_LORA = 384
KV_LORA = 256
ROPE_BASE = 10000.0
D_FF = 2816
ODD_PAD = 896
ODD_CKV_COL = 2
ODD_KR_COL = 6
HP = 128
ATT_BLK = 384
Q_PRESCALE = QK_HEAD ** -0.5 * math.log2(math.e)
FFN_CT = 256
STRIP_ROWS = 176
LANE = 128
SUBLANE = 8
VMEM_LIMIT_MB = 52

ADAM_LR = 0.001
ADAM_B1 = 0.9
ADAM_B2 = 0.999
ADAM_EPS = 1e-08
ADAM_WD = 0.01
ADAM_STEP = 10

NT_DIMS = (((1,), (1,)), ((), ()))
TN_DIMS = (((0,), (0,)), ((), ()))


def _cp(sem):
    return pltpu.CompilerParams(dimension_semantics=sem, vmem_limit_bytes=VMEM_LIMIT_MB << 20)


def _div_tile(n, cap, mult):
    if n <= cap:
        return n
    best = None
    for t in range(mult, cap + 1, mult):
        if n % t == 0:
            best = t
    assert best is not None, (n, cap, mult)
    return best


def _round_up(n, m):
    return -(-n // m) * m


def mat_cols(arr):
    return arr.shape[1] if arr.ndim == 2 else arr.shape[0] * arr.shape[2]


def mat_width(arr):
    return arr.shape[-1]


def mat_spec(arr, tm, tw, rc):
    if arr.ndim == 2:
        return pl.BlockSpec((tm, tw), lambda *g: rc(*g))
    per = arr.shape[2] // tw
    assert arr.shape[2] % tw == 0

    def imap(*g):
        r, c = rc(*g)
        return (c // per, r, c % per)

    return pl.BlockSpec((None, tm, tw), imap)


HBM_SPEC = pl.BlockSpec(memory_space=pltpu.HBM)


class Stage:
    def __init__(self, inputs, out_shapes, sems, start, finish):
        self.inputs, self.out_shapes, self.sems, self.start, self.finish = inputs, out_shapes, sems, start, finish


def run_stage(stage, name):
    n_in, n_out = len(stage.inputs), len(stage.out_shapes)

    def body(*refs):
        ins, outs, sems = refs[:n_in], refs[n_in:n_in + n_out], refs[n_in + n_out:]
        stage.start(ins, outs, sems)
        stage.finish(ins, outs, sems)

    return pl.pallas_call(body, out_shape=list(stage.out_shapes), in_specs=[HBM_SPEC] * n_in,
                          out_specs=[HBM_SPEC] * n_out, scratch_shapes=list(stage.sems), name=name)(*stage.inputs)


def _call(body, ops, carry, *, grid, in_specs, out_specs, out_shape, scratch_shapes, sem, name):
    if carry is None:
        outs = pl.pallas_call(body, grid=grid, in_specs=in_specs, out_specs=out_specs, out_shape=out_shape,
                              scratch_shapes=scratch_shapes, compiler_params=_cp(sem), name=name)(*ops)
        return outs, None
    multi = isinstance(out_shape, (list, tuple))
    shapes = list(out_shape) if multi else [out_shape]
    ospecs = list(out_specs) if multi else [out_specs]
    n_in, n_out, n_sc = len(ops), len(shapes), len(scratch_shapes)
    c_in, c_out = len(carry.inputs), len(carry.out_shapes)

    def wrapped(*refs):
        ins, cin = refs[:n_in], refs[n_in:n_in + c_in]
        o0 = n_in + c_in
        outs, cout = refs[o0:o0 + n_out], refs[o0 + n_out:o0 + n_out + c_out]
        s0 = o0 + n_out + c_out
        scs, csems = refs[s0:s0 + n_sc], refs[s0 + n_sc:]
        first = pl.program_id(0) == 0
        last = pl.program_id(0) == grid[0] - 1
        for d in range(1, len(grid)):
            first = first & (pl.program_id(d) == 0)
            last = last & (pl.program_id(d) == grid[d] - 1)

        @pl.when(first)
        def _():
            carry.start(cin, cout, csems)

        body(*ins, *outs, *scs)

        @pl.when(last)
        def _():
            carry.finish(cin, cout, csems)

    res = pl.pallas_call(
        wrapped, grid=grid, in_specs=list(in_specs) + [HBM_SPEC] * c_in, out_specs=ospecs + [HBM_SPEC] * c_out,
        out_shape=shapes + list(carry.out_shapes), scratch_shapes=list(scratch_shapes) + list(carry.sems),
        compiler_params=_cp(("arbitrary",) * len(grid)), name=name)(*ops, *carry.inputs)
    main = res[:n_out]
    return (list(main) if multi else main[0]), list(res[n_out:])


def norm_matmul(x, xcol, kdim, gain, w, tm, tn, out_dtype, name, epi=None, epi_ops=(), epi_specs=(), carry=None):
    rows, n = x.shape[0], mat_cols(w) if w.ndim == 3 else w.shape[1]
    n_epi = len(epi_ops)
    w_spec = (pl.BlockSpec((kdim, tn), lambda i, j: (0, j)) if w.ndim == 2 else
              pl.BlockSpec((None, kdim, tn), lambda i, j: (j // (w.shape[2] // tn), 0, j % (w.shape[2] // tn))))

    def body(x_ref, g_ref, w_ref, *rest):
        epi_refs = rest[:n_epi]
        out_ref, xn_ref, xn_sc = rest[n_epi:]

        @pl.when(pl.program_id(1) == 0)
        def _():
            xv = x_ref[...]
            y = xv * lax.rsqrt(jnp.mean(xv * xv, axis=-1, keepdims=True) + EPS)
            xn = (y * g_ref[...]).astype(MXU_DT)
            xn_sc[...] = xn
            xn_ref[...] = xn

        acc = jnp.dot(xn_sc[...], w_ref[...], preferred_element_type=F32)
        if epi is not None:
            acc = epi(acc, *[r[...] for r in epi_refs])
        out_ref[...] = acc.astype(out_dtype)

    return _call(
        body, (x, gain, w, *epi_ops), carry, grid=(rows // tm, n // tn),
        in_specs=[pl.BlockSpec((tm, kdim), lambda i, j: (i, xcol)), pl.BlockSpec((1, kdim), lambda i, j: (0, 0)),
                  w_spec, *epi_specs],
        out_specs=[pl.BlockSpec((tm, tn), lambda i, j: (i, j)), pl.BlockSpec((tm, kdim), lambda i, j: (i, 0))],
        out_shape=[S((rows, n), out_dtype), S((rows, kdim), MXU_DT)],
        scratch_shapes=[pltpu.VMEM((tm, kdim), MXU_DT)], sem=("parallel", "arbitrary"), name=name)


def matmul_res(a, w, res, tm, tn, name):
    grp, rows, k = a.shape
    n = w.shape[2]

    def body(a_ref, w_ref, r_ref, o_ref):
        acc = r_ref[...]
        for g in range(grp):
            acc = acc + jnp.dot(a_ref[g], w_ref[g], preferred_element_type=F32)
        o_ref[...] = acc

    return pl.pallas_call(
        body, grid=(rows // tm, n // tn),
        in_specs=[pl.BlockSpec((grp, tm, k), lambda i, j: (0, i, 0)), pl.BlockSpec((grp, k, tn), lambda i, j: (0, 0, j)),
                  pl.BlockSpec((tm, tn), lambda i, j: (i, j))],
        out_specs=pl.BlockSpec((tm, tn), lambda i, j: (i, j)),
        out_shape=S((rows, n), F32), compiler_params=_cp(("parallel", "parallel")), name=name)(a, w, res)


def matmul_nt(a, w, tm, tn, out_dtype, name):
    rows, k = a.shape
    n = w.shape[0]

    def body(a_ref, w_ref, o_ref):
        o_ref[...] = lax.dot_general(a_ref[...].astype(MXU_DT), w_ref[...], NT_DIMS,
                                     preferred_element_type=F32).astype(out_dtype)

    return pl.pallas_call(
        body, grid=(rows // tm, n // tn),
        in_specs=[pl.BlockSpec((tm, k), lambda i, j: (i, 0)), pl.BlockSpec((tn, k), lambda i, j: (j, 0))],
        out_specs=pl.BlockSpec((tm, tn), lambda i, j: (i, j)),
        out_shape=S((rows, n), out_dtype), compiler_params=_cp(("parallel", "parallel")), name=name)(a, w)


def matmul_nt_normbwd(du, w, x, xcol, gain, res, tm, tk, out_dtype, name):
    rows, kc = du.shape[-2], mat_cols(du)
    dn = w.shape[-2]
    nk = kc // tk
    has_res = res is not None
    w_spec = (pl.BlockSpec((dn, tk), lambda i, k: (0, k)) if w.ndim == 2 else
              pl.BlockSpec((None, dn, tk), lambda i, k: (k // (w.shape[2] // tk), 0, k % (w.shape[2] // tk))))

    def body(du_ref, w_ref, x_ref, g_ref, *rest):
        if has_res:
            res_ref, dx_ref, dg_ref, acc = rest
        else:
            dx_ref, dg_ref, acc = rest
        i, k = pl.program_id(0), pl.program_id(1)

        @pl.when(k == 0)
        def _():
            acc[...] = jnp.zeros_like(acc)

        @pl.when((i == 0) & (k == 0))
        def _():
            dg_ref[...] = jnp.zeros_like(dg_ref)

        acc[...] += lax.dot_general(du_ref[...], w_ref[...], NT_DIMS, preferred_element_type=F32)

        @pl.when(k == nk - 1)
        def _():
            dhn = acc[...]
            xv = x_ref[...]
            rstd = lax.rsqrt(jnp.mean(xv * xv, axis=-1, keepdims=True) + EPS)
            xhat = xv * rstd
            dg_ref[...] += jnp.sum(dhn * xhat, axis=0, keepdims=True)
            dxh = dhn * g_ref[...]
            dx = rstd * (dxh - xhat * jnp.mean(dxh * xhat, axis=-1, keepdims=True))
            if has_res:
                dx = dx + res_ref[...]
            dx_ref[...] = dx.astype(out_dtype)

    in_specs = [mat_spec(du, tm, tk, lambda i, k: (i, k)), w_spec,
                pl.BlockSpec((tm, dn), lambda i, k: (i, xcol)), pl.BlockSpec((1, dn), lambda i, k: (0, 0))]
    ops = [du, w, x, gain]
    if has_res:
        in_specs.append(pl.BlockSpec((tm, dn), lambda i, k: (i, 0)))
        ops.append(res)
    return pl.pallas_call(
        body, grid=(rows // tm, nk), in_specs=in_specs,
        out_specs=[pl.BlockSpec((tm, dn), lambda i, k: (i, 0)), pl.BlockSpec((1, dn), lambda i, k: (0, 0))],
        out_shape=[S((rows, dn), out_dtype), S((1, dn), F32)],
        scratch_shapes=[pltpu.VMEM((tm, dn), F32)],
        compiler_params=_cp(("arbitrary", "arbitrary")), name=name)(*ops)


def matmul_tn(a, b, tr, name, carry=None, col_shards=1):
    rows, ka, nb = a.shape[-2], mat_cols(a), mat_cols(b)
    ta = _div_tile(mat_width(a), 1536, LANE)
    tb = _div_tile(mat_width(b), 1536 if ta <= 1024 else 1024, LANE)
    nr = rows // tr
    if col_shards == 1:
        out_spec, out_shape = pl.BlockSpec((ta, tb), lambda i, j, r: (i, j)), S((ka, nb), F32)
    else:
        per = nb // col_shards // tb
        assert per * tb * col_shards == nb
        out_spec = pl.BlockSpec((None, ta, tb), lambda i, j, r: (j // per, i, j % per))
        out_shape = S((col_shards, ka, nb // col_shards), F32)

    def body(a_ref, b_ref, o_ref, acc):
        r = pl.program_id(2)

        @pl.when(r == 0)
        def _():
            acc[...] = jnp.zeros_like(acc)

        acc[...] += lax.dot_general(a_ref[...].astype(MXU_DT), b_ref[...].astype(MXU_DT), TN_DIMS,
                                    preferred_element_type=F32)

        @pl.when(r == nr - 1)
        def _():
            o_ref[...] = acc[...]

    return _call(
        body, (a, b), carry, grid=(ka // ta, nb // tb, nr),
        in_specs=[mat_spec(a, tr, ta, lambda i, j, r: (r, i)), mat_spec(b, tr, tb, lambda i, j, r: (r, j))],
        out_specs=out_spec, out_shape=out_shape, scratch_shapes=[pltpu.VMEM((ta, tb), F32)],
        sem=("parallel", "parallel", "arbitrary"), name=name)


def _sigmoid(x):
    return 1.0 / (1.0 + jnp.exp(-x))


def _log1p(e):
    return jnp.where(e < 1e-3, e * (1.0 - e * (0.5 - e * (1.0 / 3.0 - 0.25 * e))), jnp.log(1.0 + e))


def _softplus(x):
    return jnp.maximum(x, 0.0) + _log1p(jnp.exp(-jnp.abs(x)))


def _expm1(x):
    series = x * (1.0 + x * (0.5 + x * (1.0 / 6.0 + x * (1.0 / 24.0 + x * (1.0 / 120.0)))))
    return jnp.where(jnp.abs(x) < 0.1, series, jnp.exp(x) - 1.0)


_GELU_K = math.sqrt(2.0 / math.pi)
_GELU_C = 0.044715


def _gelu_and_grad(x):
    th = jnp.tanh(_GELU_K * (x + _GELU_C * x * x * x))
    g = 0.5 * x * (1.0 + th)
    dg = 0.5 * (1.0 + th) + 0.5 * x * (1.0 - th * th) * _GELU_K * (1.0 + 3.0 * _GELU_C * x * x)
    return g, dg


def _row_iota(shape):
    return lax.broadcasted_iota(jnp.int32, shape, 0)


def _scan_chunk_fwd(a_sc, u_sc, out_ref, hcar, n, width):
    rowi = _row_iota((SUBLANE, width))

    def step(c, hprev):
        r0 = pl.multiple_of(c * SUBLANE, SUBLANE)
        a = a_sc[pl.ds(r0, SUBLANE), :]
        u = u_sc[pl.ds(r0, SUBLANE), :]
        for d in (1, 2, 4):
            a_s = jnp.where(rowi >= d, pltpu.roll(a, d, axis=0), 1.0)
            u_s = jnp.where(rowi >= d, pltpu.roll(u, d, axis=0), 0.0)
            u = u + a * u_s
            a = a * a_s
        h = u + a * hprev
        out_ref[pl.ds(r0, SUBLANE), :] = h
        return jnp.broadcast_to(h[SUBLANE - 1:SUBLANE, :], (SUBLANE, width))

    hcar[...] = lax.fori_loop(0, n // SUBLANE, step, hcar[...], unroll=4)


def _scan_chunk_bwd(b_sc, d_sc, out_ref, gcar, n, width):
    rowi = _row_iota((SUBLANE, width))
    nc = n // SUBLANE

    def step(c, gnext):
        r0 = pl.multiple_of((nc - 1 - c) * SUBLANE, SUBLANE)
        b = b_sc[pl.ds(r0, SUBLANE), :]
        d = d_sc[pl.ds(r0, SUBLANE), :]
        for s in (1, 2, 4):
            keep = rowi < SUBLANE - s
            b_s = jnp.where(keep, pltpu.roll(b, SUBLANE - s, axis=0), 1.0)
            d_s = jnp.where(keep, pltpu.roll(d, SUBLANE - s, axis=0), 0.0)
            d = d + b * d_s
            b = b * b_s
        g = d + b * gnext
        out_ref[pl.ds(r0, SUBLANE), :] = g
        return jnp.broadcast_to(g[0:1, :], (SUBLANE, width))

    gcar[...] = lax.fori_loop(0, nc, step, gcar[...], unroll=4)


def even_mid_fwd(u, conv_a, conv_b, conv_b_bias, rw, rb, iw, ib, lam, nb, tp, n, name, carry=None):
    rows = u.shape[0]
    w = LANE
    nj = CONV_W // w
    nt = tp // n
    h8 = SUBLANE

    def body(gb_r, gc_r, xa_r, xb_r, gate_r, ca_w, cb_w, cb_b, rw_r, rb_r, iw_r, ib_r, lam_r,
             y_o, ca_o, xc_o, a_o, hs_o, pext, xext, hcar, a_sc, u_sc):
        t = pl.program_id(2)

        @pl.when(t == 0)
        def _():
            pext[0:h8, :] = jnp.zeros((h8, w), F32)
            xext[0:h8, :] = jnp.zeros((h8, w), F32)
            hcar[...] = jnp.zeros_like(hcar)

        p = gc_r[...] * xa_r[...]
        pext[h8:h8 + n, :] = p
        wa = ca_w[...]
        ca = wa[2:3, :] * p + wa[1:2, :] * pext[h8 - 1:h8 - 1 + n, :] + wa[0:1, :] * pext[h8 - 2:h8 - 2 + n, :]
        ca_o[...] = ca
        y_o[0] = (gb_r[...] * ca).astype(MXU_DT)
        pext[0:h8, :] = pext[n:n + h8, :]

        xb = xb_r[...]
        xext[h8:h8 + n, :] = xb
        wb = cb_w[...]
        xc = (wb[3:4, :] * xb + wb[2:3, :] * xext[h8 - 1:h8 - 1 + n, :] + wb[1:2, :] * xext[h8 - 2:h8 - 2 + n, :]
              + wb[0:1, :] * xext[h8 - 3:h8 - 3 + n, :]) + cb_b[...]
        xc_o[...] = xc
        xext[0:h8, :] = xext[n:n + h8, :]

        xcm = xc.astype(MXU_DT)
        r = _sigmoid(jnp.dot(xcm, rw_r[...], preferred_element_type=F32) + rb_r[...])
        ig = _sigmoid(jnp.dot(xcm, iw_r[...], preferred_element_type=F32) + ib_r[...])
        log_a = (-LRU_C) * r * _softplus(-lam_r[...])
        a = jnp.exp(log_a)
        mult = jnp.sqrt(-_expm1(2.0 * log_a))
        a_sc[...] = a
        a_o[...] = a
        u_sc[...] = mult * (ig * xc)
        _scan_chunk_fwd(a_sc, u_sc, hs_o, hcar, n, w)
        gel, _ = _gelu_and_grad(gate_r[...])
        y_o[1] = (gel * hs_o[...]).astype(MXU_DT)

    def ublk(off):
        return pl.BlockSpec((n, w), lambda j, b, t: (b * nt + t, off + j))

    def pblk(r_):
        return pl.BlockSpec((r_, w), lambda j, b, t: (0, j))

    act = pl.BlockSpec((n, w), lambda j, b, t: (b * nt + t, j))
    mat = pl.BlockSpec((w, w), lambda j, b, t: (j, j))
    return _call(
        body, (u, u, u, u, u, conv_a, conv_b, conv_b_bias, rw, rb, iw, ib, lam), carry, grid=(nj, nb, nt),
        in_specs=[ublk(0), ublk(nj), ublk(2 * nj), ublk(3 * nj), ublk(4 * nj), pblk(3), pblk(4), pblk(1),
                  mat, pblk(1), mat, pblk(1), pblk(1)],
        out_specs=[pl.BlockSpec((2, n, w), lambda j, b, t: (0, b * nt + t, j)), act, act, act, act],
        out_shape=[S((2, rows, CONV_W), MXU_DT), S((rows, CONV_W), F32), S((rows, LRU_W), F32), S((rows, LRU_W), F32),
                   S((rows, LRU_W), F32)],
        scratch_shapes=[pltpu.VMEM((n + h8, w), F32), pltpu.VMEM((n + h8, w), F32), pltpu.VMEM((h8, w), F32),
                        pltpu.VMEM((n, w), F32), pltpu.VMEM((n, w), F32)],
        sem=("parallel", "parallel", "arbitrary"), name=name)


def even_mid_bwd(u, dycat, ca, xc, a_sv, hs, conv_a, conv_b, rw, rb, iw, ib, lam, nb, tp, n, name, carry=None):
    rows = u.shape[0]
    w = LANE
    nj = CONV_W // w
    nt = tp // n
    h8 = SUBLANE

    def body(gb_r, gc_r, xa_r, xb_r, gate_r, dya_r, dyb_r, ca_r, xc_r, a_r, hs_r, hsp_r,
             ca_w, cb_w, rw_r, rb_r, iw_r, ib_r, lam_r,
             du_o, dca_w, dcb_w, dcb_b, drw, drb, diw, dib, dlam,
             aext, hext, dext, eext, gcar, b_sc, d_sc, g_sc):
        b, t = pl.program_id(1), pl.program_id(2)

        @pl.when((b == 0) & (t == 0))
        def _():
            for ref in (dca_w, dcb_w, dcb_b, drw, drb, diw, dib, dlam):
                ref[...] = jnp.zeros_like(ref)

        @pl.when(t == 0)
        def _():
            aext[n:n + h8, :] = jnp.zeros((h8, w), F32)
            dext[n:n + h8, :] = jnp.zeros((h8, w), F32)
            eext[n:n + h8, :] = jnp.zeros((h8, w), F32)
            gcar[...] = jnp.zeros_like(gcar)

        xc_v = xc_r[...]
        xcm = xc_v.astype(MXU_DT)
        r = _sigmoid(jnp.dot(xcm, rw_r[...], preferred_element_type=F32) + rb_r[...])
        ig = _sigmoid(jnp.dot(xcm, iw_r[...], preferred_element_type=F32) + ib_r[...])
        lam_v = lam_r[...]
        sp = _softplus(-lam_v)
        log_a = (-LRU_C) * r * sp
        a = a_r[...]
        mult = jnp.sqrt(-_expm1(2.0 * log_a))
        hs_v = hs_r[...]
        gel, dgel = _gelu_and_grad(gate_r[...])
        dyb = dyb_r[...]
        du_o[4] = (dyb * hs_v * dgel).astype(MXU_DT)

        aext[0:n, :] = a
        b_sc[...] = aext[1:1 + n, :]
        d_sc[...] = dyb * gel
        _scan_chunk_bwd(b_sc, d_sc, g_sc, gcar, n, w)
        aext[n:n + h8, :] = aext[0:h8, :]
        g = g_sc[...]

        hext[0:h8, :] = jnp.where(t == nt - 1, 0.0, hsp_r[...])
        hext[h8:h8 + n, :] = hs_v
        da = g * hext[h8 - 1:h8 - 1 + n, :]
        dmult = g * (ig * xc_v)
        di = g * mult * xc_v
        dxc = g * mult * ig
        dlog_a = da * a - dmult * (a * a) / mult
        dr = dlog_a * ((-LRU_C) * sp)
        dsp = jnp.sum(dlog_a * ((-LRU_C) * r), axis=0, keepdims=True)
        dlam[...] += dsp * (-_sigmoid(-lam_v))
        dzr = dr * r * (1.0 - r)
        dzi = di * ig * (1.0 - ig)
        dzr_m = dzr.astype(MXU_DT)
        dzi_m = dzi.astype(MXU_DT)
        dxc = (dxc + lax.dot_general(dzr_m, rw_r[...], NT_DIMS, preferred_element_type=F32)
               + lax.dot_general(dzi_m, iw_r[...], NT_DIMS, preferred_element_type=F32))
        drw[...] += lax.dot_general(xcm, dzr_m, TN_DIMS, preferred_element_type=F32)
        diw[...] += lax.dot_general(xcm, dzi_m, TN_DIMS, preferred_element_type=F32)
        drb[...] += jnp.sum(dzr, axis=0, keepdims=True)
        dib[...] += jnp.sum(dzi, axis=0, keepdims=True)
        dcb_b[...] += jnp.sum(dxc, axis=0, keepdims=True)

        xb = xb_r[...]
        dext[0:n, :] = dxc
        wb = cb_w[...]
        d1, d2, d3 = dext[1:1 + n, :], dext[2:2 + n, :], dext[3:3 + n, :]
        du_o[3] = (wb[3:4, :] * dxc + wb[2:3, :] * d1 + wb[1:2, :] * d2 + wb[0:1, :] * d3).astype(MXU_DT)
        dcb_w[3:4, :] += jnp.sum(xb * dxc, axis=0, keepdims=True)
        dcb_w[2:3, :] += jnp.sum(xb * d1, axis=0, keepdims=True)
        dcb_w[1:2, :] += jnp.sum(xb * d2, axis=0, keepdims=True)
        dcb_w[0:1, :] += jnp.sum(xb * d3, axis=0, keepdims=True)
        dext[n:n + h8, :] = dext[0:h8, :]

        gb, gc, xa = gb_r[...], gc_r[...], xa_r[...]
        dya = dya_r[...]
        du_o[0] = (dya * ca_r[...]).astype(MXU_DT)
        dca = dya * gb
        eext[0:n, :] = dca
        wa = ca_w[...]
        e1, e2 = eext[1:1 + n, :], eext[2:2 + n, :]
        dp = wa[2:3, :] * dca + wa[1:2, :] * e1 + wa[0:1, :] * e2
        p = gc * xa
        dca_w[2:3, :] += jnp.sum(p * dca, axis=0, keepdims=True)
        dca_w[1:2, :] += jnp.sum(p * e1, axis=0, keepdims=True)
        dca_w[0:1, :] += jnp.sum(p * e2, axis=0, keepdims=True)
        eext[n:n + h8, :] = eext[0:h8, :]
        du_o[1] = (dp * xa).astype(MXU_DT)
        du_o[2] = (dp * gc).astype(MXU_DT)

    def rt(b, t):
        return b * nt + (nt - 1 - t)

    def ublk(off):
        return pl.BlockSpec((n, w), lambda j, b, t: (rt(b, t), off + j))

    def pblk(r_):
        return pl.BlockSpec((r_, w), lambda j, b, t: (0, j))

    act = pl.BlockSpec((n, w), lambda j, b, t: (rt(b, t), j))
    n8 = n // h8
    hsp = pl.BlockSpec((h8, w), lambda j, b, t: (jnp.maximum(rt(b, t) * n8 - 1, 0), j))
    mat = pl.BlockSpec((w, w), lambda j, b, t: (j, j))
    return _call(
        body, (u, u, u, u, u, dycat, dycat, ca, xc, a_sv, hs, hs, conv_a, conv_b, rw, rb, iw, ib, lam), carry,
        grid=(nj, nb, nt),
        in_specs=[ublk(0), ublk(nj), ublk(2 * nj), ublk(3 * nj), ublk(4 * nj), ublk(0), ublk(nj), act, act, act, act,
                  hsp, pblk(3), pblk(4), mat, pblk(1), mat, pblk(1), pblk(1)],
        out_specs=[pl.BlockSpec((5, n, w), lambda j, b, t: (0, rt(b, t), j)), pblk(3), pblk(4), pblk(1),
                   mat, pblk(1), mat, pblk(1), pblk(1)],
        out_shape=[S((5, rows, CONV_W), MXU_DT), S((3, CONV_W), F32), S((4, LRU_W), F32), S((1, LRU_W), F32),
                   S((LRU_W, LRU_W), F32), S((1, LRU_W), F32), S((LRU_W, LRU_W), F32), S((1, LRU_W), F32),
                   S((1, LRU_W), F32)],
        scratch_shapes=[pltpu.VMEM((n + h8, w), F32)] * 4 + [pltpu.VMEM((h8, w), F32)] + [pltpu.VMEM((n, w), F32)] * 3,
        sem=("arbitrary", "arbitrary", "arbitrary"), name=name)


def ffn_mid_fwd(up, cw, cb, nb, tp, n, name):
    rows = up.shape[0]
    w = FFN_CT
    nj = D_FF // w
    nt = tp // n
    h8 = SUBLANE

    sr = STRIP_ROWS

    def body(xa_r, xg_r, w_r, b_r, u_o, y_o, halo):
        t = pl.program_id(2)

        @pl.when(t == 0)
        def _():
            halo[...] = jnp.zeros_like(halo)

        wv = (w_r[0], w_r[1])
        bv = (b_r[0], b_r[1])

        def strip(s, carry):
            r0 = pl.multiple_of(s * sr, sr)
            us, new = [], []
            for g, x_r in enumerate((xa_r, xg_r)):
                x = x_r[pl.ds(r0, sr), :].astype(F32)
                win = jnp.concatenate([carry[g], x], axis=0)
                x1 = pltpu.roll(win, 1, axis=0)[h8:, :]
                x2 = pltpu.roll(win, 2, axis=0)[h8:, :]
                u = (wv[g][2:3, :] * x + wv[g][1:2, :] * x1 + wv[g][0:1, :] * x2) + bv[g]
                u_o[g, pl.ds(r0, sr), :] = u.astype(MXU_DT)
                us.append(u)
                new.append(x[sr - h8:, :])
            y_o[pl.ds(r0, sr), :] = (us[0] * _sigmoid(us[0]) * us[1]).astype(MXU_DT)
            return tuple(new)

        ha, hg = lax.fori_loop(0, n // sr, strip, (halo[0], halo[1]))
        halo[0] = ha
        halo[1] = hg

    def ublk(off):
        return pl.BlockSpec((n, w), lambda j, b, t: (b * nt + t, off + j))

    return pl.pallas_call(
        body, grid=(nj, nb, nt),
        in_specs=[ublk(0), ublk(nj), pl.BlockSpec((2, 3, w), lambda j, b, t: (0, 0, j)),
                  pl.BlockSpec((2, 1, w), lambda j, b, t: (0, 0, j))],
        out_specs=[pl.BlockSpec((2, n, w), lambda j, b, t: (0, b * nt + t, j)), ublk(0)],
        out_shape=[S((2, rows, D_FF), MXU_DT), S((rows, D_FF), MXU_DT)],
        scratch_shapes=[pltpu.VMEM((2, h8, w), F32)],
        compiler_params=_cp(("parallel", "parallel", "arbitrary")), name=name,
    )(up, up, cw, cb)


def ffn_mid_bwd(dy, u, up, cw, nb, tp, n, name, carry=None):
    rows = up.shape[0]
    w = FFN_CT
    nj = D_FF // w
    nt = tp // n
    h8 = SUBLANE

    sr = STRIP_ROWS
    ns = n // sr

    def fold(v):
        acc = v[0:h8, :]
        for k in range(1, sr // h8):
            acc = acc + v[k * h8:(k + 1) * h8, :]
        return acc

    def body(dy_r, u_r, xa_r, xg_r, w_r, dx_o, dw, db, halo):
        b, t = pl.program_id(1), pl.program_id(2)

        @pl.when((b == 0) & (t == 0))
        def _():
            dw[...] = jnp.zeros_like(dw)
            db[...] = jnp.zeros_like(db)

        @pl.when(t == 0)
        def _():
            halo[...] = jnp.zeros_like(halo)

        wv = (w_r[0], w_r[1])

        def strip(s, carry):
            halos, sums = carry
            r0 = pl.multiple_of((ns - 1 - s) * sr, sr)
            dyv = dy_r[pl.ds(r0, sr), :].astype(F32)
            ua = u_r[0, pl.ds(r0, sr), :].astype(F32)
            ug = u_r[1, pl.ds(r0, sr), :].astype(F32)
            sg = _sigmoid(ua)
            dus = (dyv * ug * (sg * (1.0 + ua * (1.0 - sg))), dyv * (ua * sg))
            new_halos, new_sums = [], []
            for g, x_r in enumerate((xa_r, xg_r)):
                du = dus[g]
                win = jnp.concatenate([du, halos[g]], axis=0)
                d1 = pltpu.roll(win, sr + h8 - 1, axis=0)[0:sr, :]
                d2 = pltpu.roll(win, sr + h8 - 2, axis=0)[0:sr, :]
                dx_o[g, pl.ds(r0, sr), :] = (wv[g][2:3, :] * du + wv[g][1:2, :] * d1 + wv[g][0:1, :] * d2).astype(MXU_DT)
                x = x_r[pl.ds(r0, sr), :].astype(F32)
                s2, s1, s0, sb = sums[g]
                new_sums.append((s2 + fold(x * du), s1 + fold(x * d1), s0 + fold(x * d2), sb + fold(du)))
                new_halos.append(du[0:h8, :])
            return tuple(new_halos), tuple(new_sums)

        z = jnp.zeros((h8, w), F32)
        halos, sums = lax.fori_loop(0, ns, strip, ((halo[0], halo[1]), ((z, z, z, z), (z, z, z, z))))
        halo[0] = halos[0]
        halo[1] = halos[1]
        for g in range(2):
            s2, s1, s0, sb = sums[g]
            dw[g, 2:3, :] += jnp.sum(s2, axis=0, keepdims=True)
            dw[g, 1:2, :] += jnp.sum(s1, axis=0, keepdims=True)
            dw[g, 0:1, :] += jnp.sum(s0, axis=0, keepdims=True)
            db[g] += jnp.sum(sb, axis=0, keepdims=True)

    def rt(b, t):
        return b * nt + (nt - 1 - t)

    def ublk(off):
        return pl.BlockSpec((n, w), lambda j, b, t: (rt(b, t), off + j))

    pair = pl.BlockSpec((2, n, w), lambda j, b, t: (0, rt(b, t), j))
    return _call(
        body, (dy, u, up, up, cw), carry, grid=(nj, nb, nt),
        in_specs=[ublk(0), pair, ublk(0), ublk(nj), pl.BlockSpec((2, 3, w), lambda j, b, t: (0, 0, j))],
        out_specs=[pair, pl.BlockSpec((2, 3, w), lambda j, b, t: (0, 0, j)),
                   pl.BlockSpec((2, 1, w), lambda j, b, t: (0, 0, j))],
        out_shape=[S((2, rows, D_FF), MXU_DT), S((2, 3, D_FF), F32), S((2, 1, D_FF), F32)],
        scratch_shapes=[pltpu.VMEM((2, h8, w), F32)],
        sem=("arbitrary", "arbitrary", "arbitrary"), name=name)


def _lane_mod(shape):
    return lax.broadcasted_iota(jnp.int32, shape, 1) & (HP - 1)


def _q_rope_epi(acc, tab):
    reps = acc.shape[1] // HP
    a = acc * jnp.tile(tab, (1, reps))
    lane = _lane_mod(a.shape)
    shifted = pltpu.roll(a, a.shape[1] - QK_ROPE, axis=1)
    return jnp.where(lane < QK_NOPE, a, jnp.where(lane < QK_HEAD, a + shifted, 0.0)) * Q_PRESCALE


def _k_rope_block(krblk, tabk):
    a = krblk * tabk
    lane = _lane_mod(a.shape)
    b = a + pltpu.roll(a, HP - QK_ROPE, axis=1)
    return jnp.where((lane >= QK_NOPE) & (lane < QK_HEAD), b, 0.0)


def _k_rope_epi(acc, krblk, tabk):
    reps = acc.shape[1] // HP
    return acc + jnp.tile(_k_rope_block(krblk, tabk), (1, reps))


def attn_fwd(q, k, v, nb, tp, name, carry=None):
    rows = q.shape[0]
    blk = ATT_BLK
    nq = tp // blk
    npair = MLA_HEADS // 2

    def body(q_r, k_r, v_r, o_r, lse_r):
        qi = pl.program_id(2)
        lane = lax.broadcasted_iota(jnp.int32, (blk, LANE), 1)
        even = lane < V_HEAD
        sum_lane = (V_HEAD, 0)
        rowi = lax.broadcasted_iota(jnp.int32, (blk, blk), 0)
        coli = lax.broadcasted_iota(jnp.int32, (blk, blk), 1)
        qs = [q_r[:, h * HP:(h + 1) * HP] for h in range(2)]

        def kv_block(k0, width, carry, visible):
            ms, accs = carry
            vblk = v_r[pl.ds(k0, width), :]
            one = jnp.ones_like(vblk)
            zero = jnp.zeros_like(vblk)
            vlane = lax.broadcasted_iota(jnp.int32, (width, LANE), 1)
            ss = [lax.dot_general(qs[h], k_r[pl.ds(k0, width), h * HP:(h + 1) * HP], NT_DIMS,
                                  preferred_element_type=F32) for h in range(2)]
            new_ms, new_accs = [], []
            for h in range(2):
                s = ss[h]
                if visible is not None:
                    s = jnp.where(visible, s, -jnp.inf)
                m_new = jnp.maximum(ms[h], jnp.max(s, axis=1, keepdims=True))
                alpha = jnp.exp2(ms[h] - m_new)
                p = jnp.exp2(s - m_new).astype(MXU_DT)
                mine = (vlane < V_HEAD) if h == 0 else (vlane >= V_HEAD)
                vh = jnp.where(mine, vblk, jnp.where(vlane == sum_lane[h], one, zero))
                new_accs.append(alpha * accs[h] + jnp.dot(p, vh, preferred_element_type=F32))
                new_ms.append(m_new)
            return tuple(new_ms), tuple(new_accs)

        neg = jnp.full((blk, 1), -jnp.inf, F32)
        zacc = jnp.zeros((blk, LANE), F32)
        carry = lax.fori_loop(0, qi // 2, lambda i, c: kv_block(pl.multiple_of(i * 2 * blk, blk), 2 * blk, c, None),
                              ((neg, neg), (zacc, zacc)))
        rowi2 = lax.broadcasted_iota(jnp.int32, (blk, 2 * blk), 0)
        coli2 = lax.broadcasted_iota(jnp.int32, (blk, 2 * blk), 1)
        ms, accs = lax.cond(
            qi % 2 == 1,
            lambda c: kv_block(pl.multiple_of((qi - 1) * blk, blk), 2 * blk, c, coli2 - blk <= rowi2),
            lambda c: kv_block(pl.multiple_of(qi * blk, blk), blk, c, coli <= rowi), carry)
        ls = [accs[h][:, sum_lane[h]:sum_lane[h] + 1] for h in range(2)]
        o_r[...] = jnp.where(even, accs[0] / ls[0], accs[1] / ls[1]).astype(MXU_DT)
        lse_r[...] = jnp.where(even, ms[0] + jnp.log2(ls[0]), ms[1] + jnp.log2(ls[1]))

    return _call(
        body, (q, k, v), carry, grid=(nb, npair, nq),
        in_specs=[pl.BlockSpec((blk, 2 * HP), lambda b, p, i: (b * nq + i, p)),
                  pl.BlockSpec((tp, 2 * HP), lambda b, p, i: (b, p)),
                  pl.BlockSpec((tp, LANE), lambda b, p, i: (b, p))],
        out_specs=[pl.BlockSpec((blk, LANE), lambda b, p, i: (b * nq + i, p)),
                   pl.BlockSpec((None, blk, LANE), lambda b, p, i: (p, b * nq + i, 0))],
        out_shape=[S((rows, MLA_HEADS * V_HEAD), MXU_DT), S((npair, rows, LANE), F32)], scratch_shapes=[],
        sem=("parallel", "parallel", "arbitrary"), name=name)


def attn_bwd(q, k, v, o, do, lse, nb, tp, name, carry=None):
    rows = q.shape[0]
    blk = ATT_BLK
    nq = tp // blk
    npair = MLA_HEADS // 2
    scale = QK_HEAD ** -0.5

    def body(q_r, k_r, v_r, o_r, do_r, lse_r, dq_o, dk_o, dv_o, dq_acc, delta_sc):
        kb = pl.program_id(2)
        even = lax.broadcasted_iota(jnp.int32, (blk, LANE), 1) < V_HEAD
        rowi = lax.broadcasted_iota(jnp.int32, (blk, blk), 0)
        coli = lax.broadcasted_iota(jnp.int32, (blk, blk), 1)

        @pl.when(kb == 0)
        def _():
            dq_acc[...] = jnp.zeros_like(dq_acc)

            def dstep(i, c):
                r0 = pl.multiple_of(i * blk, blk)
                prod = do_r[pl.ds(r0, blk), :].astype(F32) * o_r[pl.ds(r0, blk), :].astype(F32)
                de = jnp.sum(jnp.where(even, prod, 0.0), axis=1, keepdims=True)
                dd = jnp.sum(jnp.where(even, 0.0, prod), axis=1, keepdims=True)
                delta_sc[pl.ds(r0, blk), :] = jnp.where(even, de, dd)
                return c

            lax.fori_loop(0, nq, dstep, 0)

        vblk = v_r[...]
        ks = [k_r[:, h * HP:(h + 1) * HP] for h in range(2)]

        def q_block(r0, height, carry, visible):
            dk0, dk1, dv = carry
            dob = do_r[pl.ds(r0, height), :]
            lse_b = lse_r[pl.ds(r0, height), :]
            dl_b = delta_sc[pl.ds(r0, height), :]
            qlane = lax.broadcasted_iota(jnp.int32, (height, LANE), 1)
            dks = [dk0, dk1]
            qhs = [q_r[pl.ds(r0, height), h * HP:(h + 1) * HP] for h in range(2)]
            dohs = [jnp.where((qlane < V_HEAD) if h == 0 else (qlane >= V_HEAD), dob, jnp.zeros_like(dob))
                    for h in range(2)]
            ss = [lax.dot_general(qhs[h], ks[h], NT_DIMS, preferred_element_type=F32) for h in range(2)]
            dps = [lax.dot_general(dohs[h], vblk, NT_DIMS, preferred_element_type=F32) for h in range(2)]
            for h in range(2):
                lo = 0 if h == 0 else V_HEAD
                p = jnp.exp2(ss[h] - lse_b[:, lo:lo + 1])
                if visible is not None:
                    p = jnp.where(visible, p, 0.0)
                ds = (p * (dps[h] - dl_b[:, lo:lo + 1])).astype(MXU_DT)
                dv = dv + lax.dot_general(p.astype(MXU_DT), dohs[h], TN_DIMS, preferred_element_type=F32)
                dks[h] = dks[h] + lax.dot_general(ds, qhs[h], TN_DIMS, preferred_element_type=F32)
                dq_acc[pl.ds(r0, height), h * HP:(h + 1) * HP] += jnp.dot(ds, ks[h], preferred_element_type=F32)
            return dks[0], dks[1], dv

        z = jnp.zeros((blk, HP), F32)
        below = nq - 1 - kb
        odd = below % 2
        rowi2 = lax.broadcasted_iota(jnp.int32, (2 * blk, blk), 0)
        coli2 = lax.broadcasted_iota(jnp.int32, (2 * blk, blk), 1)
        first = pl.multiple_of(kb * blk, blk)
        carry = lax.cond(odd == 1, lambda c: q_block(first, 2 * blk, c, coli2 <= rowi2),
                         lambda c: q_block(first, blk, c, coli <= rowi), (z, z, jnp.zeros((blk, LANE), F32)))
        dk0, dk1, dv = lax.fori_loop(
            0, below // 2, lambda i, c: q_block(pl.multiple_of((kb + 1 + odd + 2 * i) * blk, blk), 2 * blk, c, None),
            carry)
        dk_o[:, 0:HP] = (dk0 * (scale / Q_PRESCALE)).astype(MXU_DT)
        dk_o[:, HP:2 * HP] = (dk1 * (scale / Q_PRESCALE)).astype(MXU_DT)
        dv_o[...] = dv.astype(MXU_DT)

        @pl.when(kb == nq - 1)
        def _():
            dq_o[...] = (dq_acc[...] * scale).astype(MXU_DT)

    seq_pair = pl.BlockSpec((tp, LANE), lambda b, p, kk: (b, p))
    return _call(
        body, (q, k, v, o, do, lse), carry, grid=(nb, npair, nq),
        in_specs=[pl.BlockSpec((tp, 2 * HP), lambda b, p, kk: (b, p)),
                  pl.BlockSpec((blk, 2 * HP), lambda b, p, kk: (b * nq + kk, p)),
                  pl.BlockSpec((blk, LANE), lambda b, p, kk: (b * nq + kk, p)),
                  seq_pair, seq_pair, pl.BlockSpec((None, tp, LANE), lambda b, p, kk: (p, b, 0))],
        out_specs=[pl.BlockSpec((tp, 2 * HP), lambda b, p, kk: (b, p)),
                   pl.BlockSpec((blk, 2 * HP), lambda b, p, kk: (b * nq + kk, p)),
                   pl.BlockSpec((blk, LANE), lambda b, p, kk: (b * nq + kk, p))],
        out_shape=[S((rows, MLA_HEADS * HP), MXU_DT), S((rows, MLA_HEADS * HP), MXU_DT),
                   S((rows, MLA_HEADS * V_HEAD), MXU_DT)],
        scratch_shapes=[pltpu.VMEM((tp, 2 * HP), F32), pltpu.VMEM((tp, LANE), F32)],
        sem=("parallel", "parallel", "arbitrary"), name=name)


def rope_bwd(dq, dk, dv, tabq, tabk, tp, tm, name):
    rows = dq.shape[0]
    nt = tp // tm
    wq = MLA_HEADS * HP

    def body(dq_r, dk_r, dv_r, tq_r, tk_r, dqa_o, dkv_o, dkr_o):
        dqv = dq_r[...].astype(F32)
        lane = _lane_mod(dqv.shape)
        in_rope = (lane >= QK_NOPE) & (lane < QK_HEAD)
        rope = jnp.where(in_rope, dqv, 0.0)
        da = jnp.where(lane < QK_HEAD, dqv, 0.0) + pltpu.roll(rope, QK_ROPE, axis=1)
        dqa_o[...] = (da * jnp.tile(tq_r[...], (1, MLA_HEADS))).astype(MXU_DT)
        dkf = dk_r[...].astype(F32)
        dkv_o[:, 0:wq] = jnp.where(lane < QK_NOPE, dkf, 0.0).astype(MXU_DT)
        dkv_o[:, wq:] = dv_r[...]
        kr = jnp.where(in_rope, dkf, 0.0)
        tot = kr[:, 0:HP]
        for h in range(1, MLA_HEADS):
            tot = tot + kr[:, h * HP:(h + 1) * HP]
        dkr_o[...] = ((tot + pltpu.roll(tot, QK_ROPE, axis=1)) * tk_r[...]).astype(MXU_DT)

    def rowblk(wd):
        return pl.BlockSpec((tm, wd), lambda i: (i, 0))

    tab = pl.BlockSpec((tm, HP), lambda i: (i % nt, 0))
    return pl.pallas_call(
        body, grid=(rows // tm,), in_specs=[rowblk(wq), rowblk(wq), rowblk(MLA_HEADS * V_HEAD), tab, tab],
        out_specs=[rowblk(wq), rowblk(wq + MLA_HEADS * V_HEAD), rowblk(HP)],
        out_shape=[S((rows, wq), MXU_DT), S((rows, wq + MLA_HEADS * V_HEAD), MXU_DT), S((rows, HP), MXU_DT)],
        compiler_params=_cp(("parallel",)), name=name)(dq, dk, dv, tabq, tabk)


def loss_head(h, target, gain, tp, t_real, tm, name):
    rows = h.shape[0]
    nt = tp // tm

    def body(h_r, t_r, g_r, dh_o, loss_o, dg_o):
        i = pl.program_id(0)

        @pl.when(i == 0)
        def _():
            loss_o[...] = jnp.zeros_like(loss_o)
            dg_o[...] = jnp.zeros_like(dg_o)

        xv = h_r[...]
        rstd = lax.rsqrt(jnp.mean(xv * xv, axis=-1, keepdims=True) + EPS)
        xhat = xv * rstd
        g = g_r[...]
        pos = (i % nt) * tm + lax.broadcasted_iota(jnp.int32, (tm, 1), 0)
        valid = (pos >= N_META) & (pos < t_real)
        err = jnp.where(valid, xhat * g - t_r[...], 0.0)
        loss_o[...] += 0.5 * jnp.sum(jnp.mean(err * err, axis=-1, keepdims=True))
        dy = err * (1.0 / D_MODEL)
        dg_o[...] += jnp.sum(dy * xhat, axis=0, keepdims=True)
        dxh = dy * g
        dh_o[...] = rstd * (dxh - xhat * jnp.mean(dxh * xhat, axis=-1, keepdims=True))

    blk = pl.BlockSpec((tm, D_MODEL), lambda i: (i, 0))
    return pl.pallas_call(
        body, grid=(rows // tm,), in_specs=[blk, blk, pl.BlockSpec((1, D_MODEL), lambda i: (0, 0))],
        out_specs=[blk, pl.BlockSpec((1, LANE), lambda i: (0, 0)), pl.BlockSpec((1, D_MODEL), lambda i: (0, 0))],
        out_shape=[S((rows, D_MODEL), F32), S((1, LANE), F32), S((1, D_MODEL), F32)],
        compiler_params=_cp(("arbitrary",)), name=name)(h, target, gain)


ADAM_TILE_ELEMS = 128 * 1024


def adamw(g, w, m, v, name, carry=None):
    shape = w.shape
    cols = shape[-1]
    rws = max(1, math.prod(shape[:-1]))
    tr = rws if rws * cols <= ADAM_TILE_ELEMS else _div_tile(rws, max(SUBLANE, ADAM_TILE_ELEMS // cols), SUBLANE)
    bc1 = 1.0 - ADAM_B1 ** ADAM_STEP
    bc2 = 1.0 - ADAM_B2 ** ADAM_STEP

    def body(g_r, w_r, m_r, v_r, go, do, mo, vo):
        gv = g_r[...]
        mn = ADAM_B1 * m_r[...] + (1.0 - ADAM_B1) * gv
        vn = ADAM_B2 * v_r[...] + (1.0 - ADAM_B2) * (gv * gv)
        m_hat = mn / bc1
        v_hat = vn / bc2
        go[...] = gv
        do[...] = -ADAM_LR * (m_hat / (jnp.sqrt(v_hat) + ADAM_EPS) + ADAM_WD * w_r[...])
        mo[...] = mn
        vo[...] = vn

    blk = pl.BlockSpec((tr, cols), lambda i: (i, 0))
    outs, got = _call(
        body, [a.reshape(rws, cols) for a in (g, w, m, v)], carry, grid=(rws // tr,), in_specs=[blk] * 4,
        out_specs=[blk] * 4, out_shape=[S((rws, cols), F32)] * 4, scratch_shapes=[], sem=("parallel",), name=name)
    return tuple(o.reshape(shape) for o in outs), got


SUM_TILE_ELEMS = 128 * 1024


def _place():
    return lax.axis_index("x"), lax.axis_index("y"), lax.axis_index("c")


def _remote(src, dst, send_sems, recv_sems, k, to):
    return pltpu.make_async_remote_copy(src_ref=src, dst_ref=dst, send_sem=send_sems.at[k], recv_sem=recv_sems.at[k],
                                        device_id=to, device_id_type=MESH)


def chip_index():
    return 2 * lax.axis_index("x") + lax.axis_index("y")


def _sem_pair(n):
    return [pltpu.SemaphoreType.DMA((n,)), pltpu.SemaphoreType.DMA((n,))]


def stage_gather_chips(xs):
    def copies(ins, outs, sems):
        send_sems, recv_sems = sems
        mx, my, mc = _place()
        sibling = (mx, my, 1 - mc)
        chips = [(1 - mx, my), (mx, 1 - my), (1 - mx, 1 - my)]
        first, landed, passed, from_sibling = [], [], [], []
        for i, (x_ref, out_ref) in enumerate(zip(ins, outs)):
            def piece(cx, cy, h, out_ref=out_ref):
                return out_ref.at[2 * cx + cy, h]

            for j, (cx, cy) in enumerate(chips):
                k = 6 * i + j
                first.append(_remote(x_ref.at[mc], piece(mx, my, mc), send_sems, recv_sems, k, (cx, cy, mc)))
                landed.append(_remote(x_ref.at[mc], piece(cx, cy, mc), send_sems, recv_sems, k, (cx, cy, mc)))
                passed.append(_remote(piece(cx, cy, mc), piece(cx, cy, mc), send_sems, recv_sems, k + 3, sibling))
                from_sibling.append(_remote(x_ref.at[mc], piece(cx, cy, 1 - mc), send_sems, recv_sems, k + 3, sibling))
        return first, landed, passed, from_sibling

    def start(ins, outs, sems):
        for cp in copies(ins, outs, sems)[0]:
            cp.start()

    def finish(ins, outs, sems):
        first, landed, passed, from_sibling = copies(ins, outs, sems)
        for arrived, onward in zip(landed, passed):
            arrived.wait_recv()
            onward.start()
        for cp in from_sibling:
            cp.wait_recv()
        for cp in first + passed:
            cp.wait_send()

    return Stage(list(xs), [S((4,) + x.shape, x.dtype) for x in xs], _sem_pair(6 * len(xs)), start, finish)


def own_block(gathered, xs):
    return lax.dynamic_update_slice(gathered, xs[None], (chip_index(), 0, 0, 0))


def stage_pair_exchange(gs):
    def copies(ins, outs, sems):
        send_sems, recv_sems = sems
        mx, my, mc = _place()
        return [_remote(g_ref.at[s, 1 - mc], land_ref.at[s], send_sems, recv_sems, 4 * i + s, (mx, my, 1 - mc))
                for i, (g_ref, land_ref) in enumerate(zip(ins, outs)) for s in range(4)]

    def start(ins, outs, sems):
        for cp in copies(ins, outs, sems):
            cp.start()

    def finish(ins, outs, sems):
        cps = copies(ins, outs, sems)
        for cp in cps:
            cp.wait_recv()
        for cp in cps:
            cp.wait_send()

    return Stage(list(gs), [S((4,) + g.shape[2:], g.dtype) for g in gs], _sem_pair(4 * len(gs)), start, finish)


def _sum_rows(rws, width):
    return _div_tile(rws, max(SUBLANE, SUM_TILE_ELEMS // width), SUBLANE)


def pair_sum(g4, land, c_idx, name):
    _, _, rws, wd = g4.shape
    th = _sum_rows(rws, wd)

    def body(c_ref, a_ref, b_ref, o_ref):
        o_ref[...] = a_ref[...] + b_ref[...]

    return pl.pallas_call(
        body,
        grid_spec=pltpu.PrefetchScalarGridSpec(
            num_scalar_prefetch=1, grid=(4, rws // th),
            in_specs=[pl.BlockSpec((None, None, th, wd), lambda s, i, c: (s, c[0], i, 0)),
                      pl.BlockSpec((None, th, wd), lambda s, i, c: (s, i, 0))],
            out_specs=pl.BlockSpec((None, th, wd), lambda s, i, c: (s, i, 0))),
        out_shape=S((4, rws, wd), F32), compiler_params=_cp(("parallel", "parallel")), name=name)(c_idx, g4, land)


def stage_chip_scatter(ps):
    def copies(ins, outs, sems):
        send_sems, recv_sems = sems
        mx, my, mc = _place()
        me = 2 * mx + my
        chips = [(1 - mx, my), (mx, 1 - my), (1 - mx, 1 - my)]
        sent, landed = [], []
        for i, (p_ref, land_ref) in enumerate(zip(ins, outs)):
            for j, (cx, cy) in enumerate(chips):
                k = 3 * i + j
                sent.append(_remote(p_ref.at[2 * cx + cy], land_ref.at[me], send_sems, recv_sems, k, (cx, cy, mc)))
                landed.append(_remote(p_ref.at[me], land_ref.at[2 * cx + cy], send_sems, recv_sems, k, (cx, cy, mc)))
        return sent, landed

    def start(ins, outs, sems):
        for cp in copies(ins, outs, sems)[0]:
            cp.start()

    def finish(ins, outs, sems):
        sent, landed = copies(ins, outs, sems)
        for cp in landed:
            cp.wait_recv()
        for cp in sent:
            cp.wait_send()

    return Stage(list(ps), [S(p.shape, p.dtype) for p in ps], _sem_pair(3 * len(ps)), start, finish)


def chip_sum(l4, p4, me_idx, name):
    _, rws, wd = l4.shape
    th = _sum_rows(rws, wd)

    def body(me_ref, a, b, c, d, own, o_ref):
        me = me_ref[0]
        parts = [jnp.where(me == s, own[...], r[...]) for s, r in enumerate((a, b, c, d))]
        o_ref[...] = ((parts[0] + parts[1]) + parts[2]) + parts[3]

    def blk(s):
        return pl.BlockSpec((None, th, wd), lambda i, me: (jnp.where(me[0] == s, (s + 1) % 4, s), i, 0))

    return pl.pallas_call(
        body,
        grid_spec=pltpu.PrefetchScalarGridSpec(
            num_scalar_prefetch=1, grid=(rws // th,),
            in_specs=[blk(0), blk(1), blk(2), blk(3), pl.BlockSpec((None, th, wd), lambda i, me: (me[0], i, 0))],
            out_specs=pl.BlockSpec((th, wd), lambda i, me: (i, 0))),
        out_shape=S((rws, wd), F32), compiler_params=_cp(("parallel",)), name=name)(me_idx, l4, l4, l4, l4, p4)


def stage_pair_gather(rs):
    def copies(ins, outs, sems):
        send_sems, recv_sems = sems
        mx, my, mc = _place()
        return [_remote(r_ref, out_ref, send_sems, recv_sems, i, (mx, my, 1 - mc))
                for i, (r_ref, out_ref) in enumerate(zip(ins, outs))]

    def start(ins, outs, sems):
        for cp in copies(ins, outs, sems):
            cp.start()

    def finish(ins, outs, sems):
        for cp in copies(ins, outs, sems):
            cp.wait()

    return Stage(list(rs), [S(r.shape, r.dtype) for r in rs], _sem_pair(len(rs)), start, finish)


PACK_ELEMS = 16 * LANE


def pack_rows(arrays, lead, total_mult):
    parts, offs, r0 = [], [], 0
    for a in arrays:
        flat = a.reshape(a.shape[:lead] + (-1,))
        elems = _round_up(flat.shape[-1], PACK_ELEMS)
        flat = jnp.pad(flat, [(0, 0)] * lead + [(0, elems - flat.shape[-1])])
        parts.append(flat.reshape(flat.shape[:lead] + (elems // LANE, LANE)))
        offs.append((r0, elems // LANE))
        r0 += elems // LANE
    total = _round_up(r0, total_mult)
    if total > r0:
        parts.append(jnp.zeros(parts[0].shape[:lead] + (total - r0, LANE), parts[0].dtype))
    return jnp.concatenate(parts, axis=lead), offs


def unpack_rows(buf, off, shape):
    r0, nr = off
    lead = buf.shape[:-2]
    n = math.prod(shape)
    return buf[..., r0:r0 + nr, :].reshape(lead + (nr * LANE,))[..., :n].reshape(lead + tuple(shape))


def unshard(stacked, axis):
    x = jnp.moveaxis(stacked, 0, axis)
    return x.reshape(x.shape[:axis] + (4 * x.shape[axis + 1],) + x.shape[axis + 2:])


def to_shards(full, axis):
    n = full.shape[axis] // 4
    x = full.reshape(full.shape[:axis] + (4, n) + full.shape[axis + 1:])
    return jnp.moveaxis(x, axis, 0)


def _rot_cols(w):
    half = w.shape[-1] // 2
    return jnp.concatenate([-w[..., half:], w[..., :half]], axis=-1)


def _unrot_cols(dw):
    half = dw.shape[-1] // 2
    return jnp.concatenate([dw[..., half:], -dw[..., :half]], axis=-1)


def odd_w_in_padded(w_in):
    kr = w_in[:, Q_LORA + KV_LORA:]
    rows = w_in.shape[0]
    return jnp.concatenate([w_in[:, :Q_LORA], jnp.zeros((rows, 128), w_in.dtype), w_in[:, Q_LORA:Q_LORA + KV_LORA],
                            jnp.zeros((rows, 64), w_in.dtype), kr, _rot_cols(kr)], axis=1)


def odd_w_in_unpad(dwp):
    base = 512 + KV_LORA + 64
    dkr = dwp[:, base:base + QK_ROPE] + _unrot_cols(dwp[:, base + QK_ROPE:base + 2 * QK_ROPE])
    return jnp.concatenate([dwp[:, :Q_LORA], dwp[:, 512:512 + KV_LORA], dkr], axis=1)


def uq_padded(w_uq):
    w = w_uq.reshape(Q_LORA, MLA_HEADS, QK_HEAD)
    return jnp.concatenate([w, _rot_cols(w[:, :, QK_NOPE:])], axis=-1).reshape(Q_LORA, MLA_HEADS * HP)


def uq_unpad(dwp):
    d = dwp.reshape(Q_LORA, MLA_HEADS, HP)
    rope = d[:, :, QK_NOPE:QK_HEAD] + _unrot_cols(d[:, :, QK_HEAD:])
    return jnp.concatenate([d[:, :, :QK_NOPE], rope], axis=-1).reshape(Q_LORA, MLA_HEADS * QK_HEAD)


def ukv_padded(w_ukv):
    w = w_ukv.reshape(KV_LORA, MLA_HEADS, QK_NOPE + V_HEAD)
    wk = jnp.concatenate([w[:, :, :QK_NOPE], jnp.zeros((KV_LORA, MLA_HEADS, HP - QK_NOPE), w.dtype)], axis=-1)
    return jnp.concatenate([wk.reshape(KV_LORA, MLA_HEADS * HP), w[:, :, QK_NOPE:].reshape(KV_LORA, MLA_HEADS * V_HEAD)],
                           axis=1)


def ukv_unpad(dwp):
    dk = dwp[:, :MLA_HEADS * HP].reshape(KV_LORA, MLA_HEADS, HP)[:, :, :QK_NOPE]
    dv = dwp[:, MLA_HEADS * HP:].reshape(KV_LORA, MLA_HEADS, V_HEAD)
    return jnp.concatenate([dk, dv], axis=-1).reshape(KV_LORA, MLA_HEADS * (QK_NOPE + V_HEAD))


def block_diag(w):
    h, d, _ = w.shape
    eye = jnp.eye(h, dtype=w.dtype)
    return (eye[:, None, :, None] * w[:, :, None, :]).reshape(h * d, h * d)


def block_diag_part(dense, h):
    d = dense.shape[0] // h
    x = dense.reshape(h, d, h, d)
    return jnp.stack([x[i, :, i, :] for i in range(h)], axis=0)


def rope_tables(tp):
    pos = jnp.arange(tp, dtype=F32)
    inv_freq = ROPE_BASE ** (-jnp.arange(0, QK_ROPE, 2, dtype=F32) / QK_ROPE)
    ang = pos[:, None] * inv_freq[None, :]
    cos2 = jnp.tile(jnp.cos(ang), (1, 2))
    sin2 = jnp.tile(jnp.sin(ang), (1, 2))
    tabq = jnp.concatenate([jnp.ones((tp, QK_NOPE), F32), cos2, sin2], axis=1)
    tabk = jnp.concatenate([jnp.zeros((tp, QK_NOPE), F32), cos2, sin2], axis=1)
    return tabq, tabk


class Dims:
    def __init__(self, nb, seq):
        self.nb = nb
        self.t_real = seq + N_META
        self.tp = _round_up(self.t_real, ATT_BLK)
        self.n = self.tp // 4
        assert self.n % 16 == 0
        self.rows = nb * self.tp


class NoComm:
    def advance(self, carried):
        return None


def even_fwd(h, p, dm, comm):
    (u, hn), _ = norm_matmul(h, 0, D_MODEL, p["norm"], p["w_in"], dm.n, 512, F32, "ev_in")
    (y, ca, xc, a, hs), got = even_mid_fwd(u, p["conv_a"], p["conv_b"], p["conv_b_bias"], p["rw"], p["r_b"], p["iw"],
                                           p["i_b"], p["lam"], dm.nb, dm.tp, dm.n, "ev_mid", carry=comm.advance(None))
    comm.advance(got)
    out = matmul_res(y, p["w_out"].reshape(2, CONV_W, D_MODEL), h, dm.n, 512, "ev_out")
    return out, (h, u, hn, ca, xc, a, hs, y)


def even_bwd(dout, saved, p, dm, comm):
    h, u, hn, ca, xc, a, hs, y = saved
    g = {}
    dycat = matmul_nt(dout, p["w_out"], dm.n, 512, F32, "ev_dycat")
    g["w_out"], got = matmul_tn(y, dout, dm.n, "ev_dw_out", carry=comm.advance(None))
    outs, got = even_mid_bwd(u, dycat, ca, xc, a, hs, p["conv_a"], p["conv_b"], p["rw"], p["r_b"], p["iw"], p["i_b"],
                             p["lam"], dm.nb, dm.tp, dm.n, "ev_mid_bwd", carry=comm.advance(got))
    du, g["conv_a"], g["conv_b"], g["conv_b_bias"], drw, g["r_b"], diw, g["i_b"], g["lam"] = outs
    g["r_w"] = block_diag_part(drw, LRU_HEADS)
    g["i_w"] = block_diag_part(diw, LRU_HEADS)
    g["w_in"], got = matmul_tn(hn, du, dm.n, "ev_dw_in", carry=comm.advance(got))
    comm.advance(got)
    dx, g["norm"] = matmul_nt_normbwd(du, p["w_in"], h, 0, p["norm"], dout, dm.n, 512, F32, "ev_dx")
    return dx, g


def odd_fwd(h, p, tabq, tabk, dm, comm):
    nt = dm.tp // dm.n
    (u, hn), _ = norm_matmul(h, 0, D_MODEL, p["norm"], p["w_in_p"], dm.n, ODD_PAD, F32, "od_in")
    tab_spec = pl.BlockSpec((dm.n, HP), lambda i, j: (i % nt, 0))
    (q, cqn), _ = norm_matmul(u, 0, Q_LORA, p["q_norm"], p["w_uq_p"], dm.n, 512, MXU_DT, "od_q",
                              epi=_q_rope_epi, epi_ops=(tabq,), epi_specs=(tab_spec,))
    kr_spec = pl.BlockSpec((dm.n, HP), lambda i, j: (i, ODD_KR_COL))
    (k, ckvn), _ = norm_matmul(u, ODD_CKV_COL, KV_LORA, p["kv_norm"], p["w_uk_p"], dm.n, 512, MXU_DT, "od_k",
                               epi=_k_rope_epi, epi_ops=(u, tabk), epi_specs=(kr_spec, tab_spec))
    (v, _), _ = norm_matmul(u, ODD_CKV_COL, KV_LORA, p["kv_norm"], p["w_uv_p"], dm.n, 512, MXU_DT, "od_v")
    (o, lse), got = attn_fwd(q, k, v, dm.nb, dm.tp, "od_attn", carry=comm.advance(None))
    comm.advance(got)
    out = matmul_res(o[None], p["w_out"][None], h, dm.n, 512, "od_out")
    return out, (h, u, hn, cqn, ckvn, q, k, v, o, lse)


def odd_bwd(dout, saved, p, tabq, tabk, dm, comm):
    h, u, hn, cqn, ckvn, q, k, v, o, lse = saved
    g = {}
    do = matmul_nt(dout, p["w_out"], dm.n, 512, MXU_DT, "od_do")
    g["w_out"], got = matmul_tn(o, dout, dm.n, "od_dw_out", carry=comm.advance(None))
    (dq, dk, dv), got = attn_bwd(q, k, v, o, do, lse, dm.nb, dm.tp, "od_attn_bwd", carry=comm.advance(got))
    dqa, dkv, dkr = rope_bwd(dq, dk, dv, tabq, tabk, dm.tp, dm.n, "od_rope_bwd")
    g["w_uq_p"], got = matmul_tn(cqn, dqa, dm.n, "od_dw_uq", carry=comm.advance(got))
    comm.advance(got)
    g["w_ukv_p"], _ = matmul_tn(ckvn, dkv, dm.n, "od_dw_ukv")
    dcq, g["q_norm"] = matmul_nt_normbwd(dqa, p["w_uq_p"], u, 0, p["q_norm"], None, dm.n, 512, MXU_DT, "od_dcq")
    dckv, g["kv_norm"] = matmul_nt_normbwd(dkv, p["w_ukv_p"], u, ODD_CKV_COL, p["kv_norm"], None, dm.n, 512, MXU_DT,
                                           "od_dckv")
    du = jnp.concatenate([dcq, jnp.zeros((dm.rows, 128), MXU_DT), dckv, dkr], axis=1)
    g["w_in_p"], _ = matmul_tn(hn, du, dm.n, "od_dw_in")
    dx, g["norm"] = matmul_nt_normbwd(du, p["w_in_p"], h, 0, p["norm"], dout, dm.n, ODD_PAD, F32, "od_dx")
    return dx, g


def ffn_fwd(h, p, dm, comm):
    (up, hn), got = norm_matmul(h, 0, D_MODEL, p["norm"], p["w_up"], dm.n, D_FF // 2, MXU_DT, "ffn_up",
                                carry=comm.advance(None))
    comm.advance(got)
    u, y = ffn_mid_fwd(up, p["cw"], p["cb"], dm.nb, dm.tp, dm.n, "ffn_mid")
    out = matmul_res(y[None], p["w_down"][None], h, dm.n, 512, "ffn_down")
    return out, (h, up, hn, u, y)


def ffn_bwd(dout, saved, p, dm, comm):
    h, up, hn, u, y = saved
    g = {}
    dy = matmul_nt(dout, p["w_down"], dm.n, D_FF // 2, MXU_DT, "ffn_dy")
    g["w_down"], got = matmul_tn(y, dout, dm.n, "ffn_dw_down", carry=comm.advance(None))
    (dup, g["cw"], g["cb"]), got = ffn_mid_bwd(dy, u, up, p["cw"], dm.nb, dm.tp, dm.n, "ffn_mid_bwd",
                                               carry=comm.advance(got))
    g["w_up"], got = matmul_tn(hn, dup, dm.n, "ffn_dw_up", carry=comm.advance(got), col_shards=4)
    comm.advance(got)
    dx, g["norm"] = matmul_nt_normbwd(dup, p["w_up"], h, 0, p["norm"], dout, dm.n, D_FF // 2, F32, "ffn_dx")
    return dx, g


def _row(v):
    return v.reshape(1, -1)


def even_params(wf, j):
    return dict(norm=_row(wf["ev_norm"][j]), w_in=wf["ev_w_in"], conv_a=wf["ev_conv_a"][j], conv_b=wf["ev_conv_b"][j],
                conv_b_bias=_row(wf["ev_conv_b_bias"][j]), rw=block_diag(wf["ev_gate_r_w"][j]).astype(MXU_DT),
                r_b=_row(wf["ev_gate_r_b"][j]), iw=block_diag(wf["ev_gate_i_w"][j]).astype(MXU_DT),
                i_b=_row(wf["ev_gate_i_b"][j]), lam=_row(wf["ev_lru_lambda"][j]), w_out=wf["ev_w_out"])


def odd_params(wf, j):
    wkv = ukv_padded(wf["od_w_ukv"])
    return dict(norm=_row(wf["od_norm"][j]), w_in_p=odd_w_in_padded(wf["od_w_in"]), q_norm=_row(wf["od_q_norm"][j]),
                kv_norm=_row(wf["od_kv_norm"][j]), w_uq_p=uq_padded(wf["od_w_uq"]), w_ukv_p=wkv,
                w_uk_p=wkv[:, :MLA_HEADS * HP], w_uv_p=wkv[:, MLA_HEADS * HP:], w_out=wf["od_w_out"])


def ffn_params(wf, layer):
    return dict(norm=_row(wf["ffn_norm"][layer]), w_up=wf["ffn_w_up"],
                cw=jnp.moveaxis(wf["ffn_conv_w"][layer].reshape(3, 2, D_FF), 1, 0),
                cb=wf["ffn_conv_b"][layer].reshape(2, 1, D_FF), w_down=wf["ffn_w_down"])


def even_grads(g):
    out = {"ev_" + k_: g[k_] for k_ in ("w_in", "conv_a", "conv_b", "w_out")}
    out.update({"ev_norm": g["norm"][0], "ev_conv_b_bias": g["conv_b_bias"][0], "ev_gate_r_w": g["r_w"],
                "ev_gate_r_b": g["r_b"][0], "ev_gate_i_w": g["i_w"], "ev_gate_i_b": g["i_b"][0],
                "ev_lru_lambda": g["lam"][0]})
    return out


def odd_grads(g):
    return {"od_norm": g["norm"][0], "od_q_norm": g["q_norm"][0], "od_kv_norm": g["kv_norm"][0],
            "od_w_in": odd_w_in_unpad(g["w_in_p"]), "od_w_uq": uq_unpad(g["w_uq_p"]),
            "od_w_ukv": ukv_unpad(g["w_ukv_p"]), "od_w_out": g["w_out"]}


def ffn_grads(g):
    return {"ffn_norm": g["norm"][0], "ffn_w_up": g["w_up"], "ffn_conv_w": jnp.moveaxis(g["cw"], 0, 1).reshape(3, 2 * D_FF),
            "ffn_conv_b": g["cb"].reshape(2 * D_FF), "ffn_w_down": g["w_down"]}


WEIGHTS = ["meta_tokens", "ev_norm", "ev_w_in", "ev_conv_a", "ev_conv_b", "ev_conv_b_bias", "ev_gate_r_w", "ev_gate_r_b",
           "ev_gate_i_w", "ev_gate_i_b", "ev_lru_lambda", "ev_w_out", "od_norm", "od_w_in", "od_q_norm", "od_kv_norm",
           "od_w_uq", "od_w_ukv", "od_w_out", "ffn_norm", "ffn_w_up", "ffn_conv_w", "ffn_conv_b", "ffn_w_down",
           "final_norm"]
SHARD_AXIS = {"meta_tokens": 1, "ev_w_in": 2, "ev_conv_a": 2, "ev_conv_b": 2, "ev_w_out": 1, "od_norm": 1, "od_w_in": 1,
              "od_q_norm": 1, "od_kv_norm": 1, "od_w_uq": 2, "od_w_ukv": 2, "od_w_out": 1, "ffn_w_up": 2,
              "ffn_conv_w": 2, "ffn_w_down": 1}
MATMUL_WEIGHTS = ["ev_w_in", "ev_w_out", "od_w_in", "od_w_uq", "od_w_ukv", "od_w_out", "ffn_w_up", "ffn_w_down"]


LAYER_ORDER = [("ev", 0), ("ffn", 0), ("od", 0), ("ffn", 1), ("ev", 1), ("ffn", 2), ("od", 1), ("ffn", 3)]
LAYER_MATMUL = {"ev": ["ev_w_in", "ev_w_out"], "od": ["od_w_in", "od_w_uq", "od_w_ukv", "od_w_out"],
                "ffn": ["ffn_w_up", "ffn_w_down"]}
LAYER_SHARDED = {"ev": ["ev_w_in", "ev_conv_a", "ev_conv_b", "ev_w_out"],
                 "od": ["od_norm", "od_w_in", "od_q_norm", "od_kv_norm", "od_w_uq", "od_w_ukv", "od_w_out"],
                 "ffn": ["ffn_w_up", "ffn_conv_w", "ffn_w_down"]}
STACKED_SHARDS = "ffn_w_up"


def gather_all_layers(w, names, name):
    buf, offs = pack_rows([w[n] for n in names], 0, 32)
    halves = buf.reshape(2, buf.shape[0] // 2, LANE)
    got = own_block(run_stage(stage_gather_chips([halves]), name)[0], halves).reshape(4, buf.shape[0], LANE)
    return {n: unshard(unpack_rows(got, off, w[n].shape), SHARD_AXIS[n]) for n, off in zip(names, offs)}


def _halves(a):
    return a.reshape(2, a.shape[0] // 2, a.shape[1])


class GatherComm:
    def __init__(self, w, kind, idx):
        self.names = LAYER_MATMUL[kind]
        self.halves = [_halves(w[n][idx].astype(MXU_DT)) for n in self.names]
        self.stage = stage_gather_chips(self.halves)
        self.step, self.got = 0, None

    def advance(self, carried):
        self.step += 1
        if self.step == 1:
            return self.stage
        if self.step == 2:
            self.got = carried
        return None

    def run_alone(self, name):
        self.advance(run_stage(self.advance(None), name))

    def weights(self):
        out = {}
        for n, got, own in zip(self.names, self.got, self.halves):
            stacked = own_block(got, own).reshape(4, 2 * own.shape[1], own.shape[2])
            out[n] = stacked if n == STACKED_SHARDS else unshard(stacked, SHARD_AXIS[n] - 1)
        return out


class ReduceComm:
    def __init__(self, grads, axes, c_idx, tag, tail=None):
        shards = {n: grads[n] if n == STACKED_SHARDS else to_shards(grads[n], axes[n]) for n in grads}
        self.big = [n for n in grads if n in MATMUL_WEIGHTS]
        self.small = [n for n in grads if n not in MATMUL_WEIGHTS]
        self.shapes = {n: shards[n].shape[1:] for n in grads}
        arrays = [shards[n].reshape(4, 2, shards[n].shape[1] // 2, shards[n].shape[2]) for n in self.big]
        gs, self.offs = pack_rows([shards[n] for n in self.small], 1, 16)
        self.rs = gs.shape[1] // 2
        parts = [gs.reshape(4, 2, self.rs, LANE)]
        self.rr = 0
        if tail is not None:
            self.rr = tail.shape[0] // 8
            parts.append(tail.reshape(4, 2, self.rr, LANE))
        arrays.append(jnp.concatenate(parts, axis=2) if len(parts) > 1 else parts[0])
        self.arrays, self.c_idx, self.tag, self.step = arrays, c_idx, tag, 0
        self.part = self.mine = self.theirs = None

    def advance(self, carried):
        self.step += 1
        if self.step == 1:
            return stage_pair_exchange(self.arrays)
        if self.step == 2:
            self.part = [pair_sum(g, land, self.c_idx, "grad_pair_sum_%s_%d" % (self.tag, i))
                         for i, (g, land) in enumerate(zip(self.arrays, carried))]
            return stage_chip_scatter(self.part)
        if self.step == 3:
            me_idx = chip_index().astype(jnp.int32).reshape(1)
            self.mine = [chip_sum(land, part, me_idx, "grad_chip_sum_%s_%d" % (self.tag, i))
                         for i, (land, part) in enumerate(zip(carried, self.part))]
            return stage_pair_gather(self.mine)
        if self.step == 4:
            self.theirs = carried
        return None

    def run_alone(self, name):
        stage = self.advance(None)
        while stage is not None:
            stage = self.advance(run_stage(stage, name + "_%d" % self.step))

    def results(self):
        south = self.c_idx[0] == 0
        boths = [jnp.stack([jnp.where(south, m, t), jnp.where(south, t, m)], axis=0)
                 for m, t in zip(self.mine, self.theirs)]
        out = {n: b.reshape(self.shapes[n]) for n, b in zip(self.big, boths)}
        packed = boths[-1]
        flat = packed[:, :self.rs].reshape(2 * self.rs, LANE)
        out.update({n: unpack_rows(flat, off, self.shapes[n]) for n, off in zip(self.small, self.offs)})
        return out, packed[:, self.rs:self.rs + self.rr]


def kernel(x, meta_tokens, ev_norm, ev_w_in, ev_conv_a, ev_conv_b, ev_conv_b_bias, ev_gate_r_w, ev_gate_r_b, ev_gate_i_w, ev_gate_i_b, ev_lru_lambda, ev_w_out, od_norm, od_w_in, od_q_norm, od_kv_norm, od_w_uq, od_w_ukv, od_w_out, ffn_norm, ffn_w_up, ffn_conv_w, ffn_conv_b, ffn_w_down, final_norm, loss_target, m_meta_tokens, m_ev_norm, m_ev_w_in, m_ev_conv_a, m_ev_conv_b, m_ev_conv_b_bias, m_ev_gate_r_w, m_ev_gate_r_b, m_ev_gate_i_w, m_ev_gate_i_b, m_ev_lru_lambda, m_ev_w_out, m_od_norm, m_od_w_in, m_od_q_norm, m_od_kv_norm, m_od_w_uq, m_od_w_ukv, m_od_w_out, m_ffn_norm, m_ffn_w_up, m_ffn_conv_w, m_ffn_conv_b, m_ffn_w_down, m_final_norm, v_meta_tokens, v_ev_norm, v_ev_w_in, v_ev_conv_a, v_ev_conv_b, v_ev_conv_b_bias, v_ev_gate_r_w, v_ev_gate_r_b, v_ev_gate_i_w, v_ev_gate_i_b, v_ev_lru_lambda, v_ev_w_out, v_od_norm, v_od_w_in, v_od_q_norm, v_od_kv_norm, v_od_w_uq, v_od_w_ukv, v_od_w_out, v_ffn_norm, v_ffn_w_up, v_ffn_conv_w, v_ffn_conv_b, v_ffn_w_down, v_final_norm):
    given = dict(locals())
    w = {n: given[n] for n in WEIGHTS}
    nb, seq, _ = x.shape
    dm = Dims(nb, seq)
    n_layers = len(LAYER_ORDER)

    wf = {n: w[n] for n in WEIGHTS if n not in SHARD_AXIS}
    wf.update(gather_all_layers(w, [n for n in SHARD_AXIS if n not in MATMUL_WEIGHTS], "gather_small_weights"))
    gathers = [GatherComm(w, kind, idx) for kind, idx in LAYER_ORDER]
    gathers[0].run_alone("gather_first_layer")

    tail = dm.tp - dm.t_real
    meta = jnp.broadcast_to(wf["meta_tokens"][None], (nb, N_META, D_MODEL))
    h = jnp.concatenate([meta, x, jnp.zeros((nb, tail, D_MODEL), F32)], axis=1).reshape(dm.rows, D_MODEL)
    tgt = jnp.pad(loss_target, ((0, 0), (N_META, tail), (0, 0))).reshape(dm.rows, D_MODEL)
    tabq, tabk = rope_tables(dm.tp)

    params, saved = [], []
    for i, (kind, idx) in enumerate(LAYER_ORDER):
        wl = dict(wf)
        wl.update(gathers[i].weights())
        comm = gathers[i + 1] if i + 1 < n_layers else NoComm()
        if kind == "ev":
            p = even_params(wl, idx)
            h, sv = even_fwd(h, p, dm, comm)
        elif kind == "od":
            p = odd_params(wl, idx)
            h, sv = odd_fwd(h, p, tabq, tabk, dm, comm)
        else:
            p = ffn_params(wl, idx)
            h, sv = ffn_fwd(h, p, dm, comm)
        params.append(p)
        saved.append(sv)

    dh, loss, dfinal = loss_head(h, tgt, _row(wf["final_norm"]), dm.tp, dm.t_real, dm.n, "loss_head")
    loss = lax.psum(loss[0, 0], ("x", "y", "c"))

    c_idx = lax.axis_index("c").astype(jnp.int32).reshape(1)
    layer_grads = {n: {} for n in WEIGHTS}
    pending, reduces = NoComm(), []
    for i in reversed(range(n_layers)):
        kind, idx = LAYER_ORDER[i]
        if kind == "ev":
            dh, g = even_bwd(dh, saved[i], params[i], dm, pending)
            g = even_grads(g)
        elif kind == "od":
            dh, g = odd_bwd(dh, saved[i], params[i], tabq, tabk, dm, pending)
            g = odd_grads(g)
        else:
            dh, g = ffn_bwd(dh, saved[i], params[i], dm, pending)
            g = ffn_grads(g)
        for n in g:
            if n not in SHARD_AXIS:
                layer_grads[n][idx] = g[n]
        if i > 0:
            pending = ReduceComm({n: g[n] for n in LAYER_SHARDED[kind]}, {n: SHARD_AXIS[n] - 1 for n in SHARD_AXIS},
                                 c_idx, "%s%d" % (kind, idx))
            reduces.append((pending, idx))
    dh3 = dh.reshape(nb, dm.tp, D_MODEL)
    grad_x = dh3[:, N_META:dm.t_real]

    repl = [n for n in WEIGHTS if n not in SHARD_AXIS]
    layer_grads["final_norm"] = {0: dfinal[0]}
    repl_full = {n: (layer_grads[n][0] if n == "final_norm" else
                     jnp.stack([layer_grads[n][j] for j in range(w[n].shape[0])], axis=0)) for n in repl}
    tail_buf, tail_offs = pack_rows([repl_full[n] for n in repl], 0, 64)
    first = {n: g[n] for n in LAYER_SHARDED["ev"]}
    first["meta_tokens"] = jnp.sum(dh3[:, :N_META], axis=0)
    axes = {n: SHARD_AXIS[n] - 1 for n in SHARD_AXIS}
    axes["meta_tokens"] = SHARD_AXIS["meta_tokens"]
    last = ReduceComm(first, axes, c_idx, "first_layer", tail=tail_buf)
    last.run_alone("grad_first_layer")
    reduces.append((last, 0))

    red = {}
    for comm, idx in reduces:
        got, tail_piece = comm.results()
        for n, v_ in got.items():
            if n == "meta_tokens":
                red[n] = v_
            else:
                layer_grads[n][idx] = v_
    tails = own_block(run_stage(stage_gather_chips([tail_piece]), "grad_gather_replicated")[0], tail_piece)
    tails = tails.reshape(tail_buf.shape[0], LANE)
    for n, off in zip(repl, tail_offs):
        red[n] = unpack_rows(tails, off, w[n].shape)
    for n in SHARD_AXIS:
        if n != "meta_tokens":
            red[n] = jnp.stack([layer_grads[n][j] for j in range(w[n].shape[0])], axis=0)

    outs = [adamw(red[n], w[n], given["m_" + n], given["v_" + n], "adamw_" + n)[0] for n in WEIGHTS]
    return (loss, grad_x, *[o[0] for o in outs], *[o[1] for o in outs], *[o[2] for o in outs], *[o[3] for o in outs])
```

```python
import math

import jax
import jax.numpy as jnp
from jax import lax
from jax.experimental import pallas as pl
from jax.experimental.pallas import tpu as pltpu

F32 = jnp.float32
MXU_DT = jnp.bfloat16
S = jax.ShapeDtypeStruct
MESH = pl.DeviceIdType.MESH

EPS = 1e-6
D_MODEL = 1024
N_META = 16
DEPTH = 4
CONV_W = 512
LRU_W = 512
LRU_HEADS = 8
LRU_C = 8.0
EVEN_IN = 2560
MLA_HEADS = 16
QK_NOPE = 64
QK_ROPE = 32
QK_HEAD = 96
V_HEAD = 64
Q_LORA = 384
KV_LORA = 256
ROPE_BASE = 10000.0
D_FF = 2816
ODD_PAD = 896
ODD_CKV_COL = 2
ODD_KR_COL = 6
HP = 128
ATT_BLK = 384
Q_PRESCALE = QK_HEAD ** -0.5 * math.log2(math.e)
FFN_CT = 256
EV_CT = 256
STRIP_ROWS = 176
LANE = 128
SUBLANE = 8
VMEM_LIMIT_MB = 52

ADAM_LR = 0.001
ADAM_B1 = 0.9
ADAM_B2 = 0.999
ADAM_EPS = 1e-08
ADAM_WD = 0.01
ADAM_STEP = 10

NT_DIMS = (((1,), (1,)), ((), ()))
TN_DIMS = (((0,), (0,)), ((), ()))


def _cp(sem):
    return pltpu.CompilerParams(dimension_semantics=sem, vmem_limit_bytes=VMEM_LIMIT_MB << 20)


def _div_tile(n, cap, mult):
    if n <= cap:
        return n
    best = None
    for t in range(mult, cap + 1, mult):
        if n % t == 0:
            best = t
    assert best is not None, (n, cap, mult)
    return best


def _round_up(n, m):
    return -(-n // m) * m


def mat_cols(arr):
    return arr.shape[1] if arr.ndim == 2 else arr.shape[0] * arr.shape[2]


def mat_width(arr):
    return arr.shape[-1]


def mat_spec(arr, tm, tw, rc):
    if arr.ndim == 2:
        return pl.BlockSpec((tm, tw), lambda *g: rc(*g))
    per = arr.shape[2] // tw
    assert arr.shape[2] % tw == 0

    def imap(*g):
        r, c = rc(*g)
        return (c // per, r, c % per)

    return pl.BlockSpec((None, tm, tw), imap)


HBM_SPEC = pl.BlockSpec(memory_space=pltpu.HBM)


class Stage:
    def __init__(self, inputs, out_shapes, sems, start, finish):
        self.inputs, self.out_shapes, self.sems, self.start, self.finish = inputs, out_shapes, sems, start, finish


def run_stage(stage, name):
    n_in, n_out = len(stage.inputs), len(stage.out_shapes)

    def body(*refs):
        ins, outs, sems = refs[:n_in], refs[n_in:n_in + n_out], refs[n_in + n_out:]
        stage.start(ins, outs, sems)
        stage.finish(ins, outs, sems)

    return pl.pallas_call(body, out_shape=list(stage.out_shapes), in_specs=[HBM_SPEC] * n_in,
                          out_specs=[HBM_SPEC] * n_out, scratch_shapes=list(stage.sems), name=name)(*stage.inputs)


def _call(body, ops, carry, *, grid, in_specs, out_specs, out_shape, scratch_shapes, sem, name):
    if carry is None:
        outs = pl.pallas_call(body, grid=grid, in_specs=in_specs, out_specs=out_specs, out_shape=out_shape,
                              scratch_shapes=scratch_shapes, compiler_params=_cp(sem), name=name)(*ops)
        return outs, None
    multi = isinstance(out_shape, (list, tuple))
    shapes = list(out_shape) if multi else [out_shape]
    ospecs = list(out_specs) if multi else [out_specs]
    n_in, n_out, n_sc = len(ops), len(shapes), len(scratch_shapes)
    c_in, c_out = len(carry.inputs), len(carry.out_shapes)

    def wrapped(*refs):
        ins, cin = refs[:n_in], refs[n_in:n_in + c_in]
        o0 = n_in + c_in
        outs, cout = refs[o0:o0 + n_out], refs[o0 + n_out:o0 + n_out + c_out]
        s0 = o0 + n_out + c_out
        scs, csems = refs[s0:s0 + n_sc], refs[s0 + n_sc:]
        first = pl.program_id(0) == 0
        last = pl.program_id(0) == grid[0] - 1
        for d in range(1, len(grid)):
            first = first & (pl.program_id(d) == 0)
            last = last & (pl.program_id(d) == grid[d] - 1)

        @pl.when(first)
        def _():
            carry.start(cin, cout, csems)

        body(*ins, *outs, *scs)

        @pl.when(last)
        def _():
            carry.finish(cin, cout, csems)

    res = pl.pallas_call(
        wrapped, grid=grid, in_specs=list(in_specs) + [HBM_SPEC] * c_in, out_specs=ospecs + [HBM_SPEC] * c_out,
        out_shape=shapes + list(carry.out_shapes), scratch_shapes=list(scratch_shapes) + list(carry.sems),
        compiler_params=_cp(("arbitrary",) * len(grid)), name=name)(*ops, *carry.inputs)
    main = res[:n_out]
    return (list(main) if multi else main[0]), list(res[n_out:])


def norm_matmul(x, xcol, kdim, gain, w, tm, tn, out_dtype, name, epi=None, epi_ops=(), epi_specs=(), carry=None):
    rows, n = x.shape[0], mat_cols(w) if w.ndim == 3 else w.shape[1]
    n_epi = len(epi_ops)
    w_spec = (pl.BlockSpec((kdim, tn), lambda i, j: (0, j)) if w.ndim == 2 else
              pl.BlockSpec((None, kdim, tn), lambda i, j: (j // (w.shape[2] // tn), 0, j % (w.shape[2] // tn))))

    def body(x_ref, g_ref, w_ref, *rest):
        epi_refs = rest[:n_epi]
        out_ref, xn_ref, xn_sc = rest[n_epi:]

        @pl.when(pl.program_id(1) == 0)
        def _():
            xv = x_ref[...]
            y = xv * lax.rsqrt(jnp.mean(xv * xv, axis=-1, keepdims=True) + EPS)
            xn = (y * g_ref[...]).astype(MXU_DT)
            xn_sc[...] = xn
            xn_ref[...] = xn

        acc = jnp.dot(xn_sc[...], w_ref[...], preferred_element_type=F32)
        if epi is not None:
            acc = epi(acc, *[r[...] for r in epi_refs])
        out_ref[...] = acc.astype(out_dtype)

    return _call(
        body, (x, gain, w, *epi_ops), carry, grid=(rows // tm, n // tn),
        in_specs=[pl.BlockSpec((tm, kdim), lambda i, j: (i, xcol)), pl.BlockSpec((1, kdim), lambda i, j: (0, 0)),
                  w_spec, *epi_specs],
        out_specs=[pl.BlockSpec((tm, tn), lambda i, j: (i, j)), pl.BlockSpec((tm, kdim), lambda i, j: (i, 0))],
        out_shape=[S((rows, n), out_dtype), S((rows, kdim), MXU_DT)],
        scratch_shapes=[pltpu.VMEM((tm, kdim), MXU_DT)], sem=("parallel", "arbitrary"), name=name)


def matmul_res(a, w, res, tm, tn, name):
    grp, rows, k = a.shape
    n = w.shape[2]

    def body(a_ref, w_ref, r_ref, o_ref):
        acc = r_ref[...]
        for g in range(grp):
            acc = acc + jnp.dot(a_ref[g], w_ref[g], preferred_element_type=F32)
        o_ref[...] = acc

    return pl.pallas_call(
        body, grid=(rows // tm, n // tn),
        in_specs=[pl.BlockSpec((grp, tm, k), lambda i, j: (0, i, 0)), pl.BlockSpec((grp, k, tn), lambda i, j: (0, 0, j)),
                  pl.BlockSpec((tm, tn), lambda i, j: (i, j))],
        out_specs=pl.BlockSpec((tm, tn), lambda i, j: (i, j)),
        out_shape=S((rows, n), F32), compiler_params=_cp(("parallel", "parallel")), name=name)(a, w, res)


def matmul_nt(a, w, tm, tn, out_dtype, name):
    rows, k = a.shape
    n = w.shape[0]

    def body(a_ref, w_ref, o_ref):
        o_ref[...] = lax.dot_general(a_ref[...].astype(MXU_DT), w_ref[...], NT_DIMS,
                                     preferred_element_type=F32).astype(out_dtype)

    return pl.pallas_call(
        body, grid=(rows // tm, n // tn),
        in_specs=[pl.BlockSpec((tm, k), lambda i, j: (i, 0)), pl.BlockSpec((tn, k), lambda i, j: (j, 0))],
        out_specs=pl.BlockSpec((tm, tn), lambda i, j: (i, j)),
        out_shape=S((rows, n), out_dtype), compiler_params=_cp(("parallel", "parallel")), name=name)(a, w)


def matmul_nt_normbwd(du, w, x, xcol, gain, res, tm, tk, out_dtype, name):
    rows, kc = du.shape[-2], mat_cols(du)
    dn = w.shape[-2]
    nk = kc // tk
    has_res = res is not None
    w_spec = (pl.BlockSpec((dn, tk), lambda i, k: (0, k)) if w.ndim == 2 else
              pl.BlockSpec((None, dn, tk), lambda i, k: (k // (w.shape[2] // tk), 0, k % (w.shape[2] // tk))))

    def body(du_ref, w_ref, x_ref, g_ref, *rest):
        if has_res:
            res_ref, dx_ref, dg_ref, acc = rest
        else:
            dx_ref, dg_ref, acc = rest
        i, k = pl.program_id(0), pl.program_id(1)

        @pl.when(k == 0)
        def _():
            acc[...] = jnp.zeros_like(acc)

        @pl.when((i == 0) & (k == 0))
        def _():
            dg_ref[...] = jnp.zeros_like(dg_ref)

        acc[...] += lax.dot_general(du_ref[...], w_ref[...], NT_DIMS, preferred_element_type=F32)

        @pl.when(k == nk - 1)
        def _():
            dhn = acc[...]
            xv = x_ref[...]
            rstd = lax.rsqrt(jnp.mean(xv * xv, axis=-1, keepdims=True) + EPS)
            xhat = xv * rstd
            dg_ref[...] += jnp.sum(dhn * xhat, axis=0, keepdims=True)
            dxh = dhn * g_ref[...]
            dx = rstd * (dxh - xhat * jnp.mean(dxh * xhat, axis=-1, keepdims=True))
            if has_res:
                dx = dx + res_ref[...]
            dx_ref[...] = dx.astype(out_dtype)

    in_specs = [mat_spec(du, tm, tk, lambda i, k: (i, k)), w_spec,
                pl.BlockSpec((tm, dn), lambda i, k: (i, xcol)), pl.BlockSpec((1, dn), lambda i, k: (0, 0))]
    ops = [du, w, x, gain]
    if has_res:
        in_specs.append(pl.BlockSpec((tm, dn), lambda i, k: (i, 0)))
        ops.append(res)
    return pl.pallas_call(
        body, grid=(rows // tm, nk), in_specs=in_specs,
        out_specs=[pl.BlockSpec((tm, dn), lambda i, k: (i, 0)), pl.BlockSpec((1, dn), lambda i, k: (0, 0))],
        out_shape=[S((rows, dn), out_dtype), S((1, dn), F32)],
        scratch_shapes=[pltpu.VMEM((tm, dn), F32)],
        compiler_params=_cp(("arbitrary", "arbitrary")), name=name)(*ops)


def matmul_tn(a, b, tr, name, carry=None, col_shards=1):
    rows, ka, nb = a.shape[-2], mat_cols(a), mat_cols(b)
    ta = _div_tile(mat_width(a), 1536, LANE)
    tb = _div_tile(mat_width(b), 1536 if ta <= 1024 else 1024, LANE)
    nr = rows // tr
    if col_shards == 1:
        out_spec, out_shape = pl.BlockSpec((ta, tb), lambda i, j, r: (i, j)), S((ka, nb), F32)
    else:
        per = nb // col_shards // tb
        assert per * tb * col_shards == nb
        out_spec = pl.BlockSpec((None, ta, tb), lambda i, j, r: (j // per, i, j % per))
        out_shape = S((col_shards, ka, nb // col_shards), F32)

    def body(a_ref, b_ref, o_ref, acc):
        r = pl.program_id(2)

        @pl.when(r == 0)
        def _():
            acc[...] = jnp.zeros_like(acc)

        acc[...] += lax.dot_general(a_ref[...].astype(MXU_DT), b_ref[...].astype(MXU_DT), TN_DIMS,
                                    preferred_element_type=F32)

        @pl.when(r == nr - 1)
        def _():
            o_ref[...] = acc[...]

    return _call(
        body, (a, b), carry, grid=(ka // ta, nb // tb, nr),
        in_specs=[mat_spec(a, tr, ta, lambda i, j, r: (r, i)), mat_spec(b, tr, tb, lambda i, j, r: (r, j))],
        out_specs=out_spec, out_shape=out_shape, scratch_shapes=[pltpu.VMEM((ta, tb), F32)],
        sem=("parallel", "parallel", "arbitrary"), name=name)


def _sigmoid(x):
    return 1.0 / (1.0 + jnp.exp(-x))


def _log1p(e):
    return jnp.where(e < 1e-3, e * (1.0 - e * (0.5 - e * (1.0 / 3.0 - 0.25 * e))), jnp.log(1.0 + e))


def _softplus(x):
    return jnp.maximum(x, 0.0) + _log1p(jnp.exp(-jnp.abs(x)))


def _expm1(x):
    series = x * (1.0 + x * (0.5 + x * (1.0 / 6.0 + x * (1.0 / 24.0 + x * (1.0 / 120.0)))))
    return jnp.where(jnp.abs(x) < 0.1, series, jnp.exp(x) - 1.0)


_GELU_K = math.sqrt(2.0 / math.pi)
_GELU_C = 0.044715


def _gelu_and_grad(x):
    th = jnp.tanh(_GELU_K * (x + _GELU_C * x * x * x))
    g = 0.5 * x * (1.0 + th)
    dg = 0.5 * (1.0 + th) + 0.5 * x * (1.0 - th * th) * _GELU_K * (1.0 + 3.0 * _GELU_C * x * x)
    return g, dg


def _row_iota(shape):
    return lax.broadcasted_iota(jnp.int32, shape, 0)


def _scan_chunk_fwd(a_sc, u_sc, out_ref, hcar, n, width):
    rowi = _row_iota((SUBLANE, width))

    def step(c, hprev):
        r0 = pl.multiple_of(c * SUBLANE, SUBLANE)
        a = a_sc[pl.ds(r0, SUBLANE), :]
        u = u_sc[pl.ds(r0, SUBLANE), :]
        for d in (1, 2, 4):
            a_s = jnp.where(rowi >= d, pltpu.roll(a, d, axis=0), 1.0)
            u_s = jnp.where(rowi >= d, pltpu.roll(u, d, axis=0), 0.0)
            u = u + a * u_s
            a = a * a_s
        h = u + a * hprev
        out_ref[pl.ds(r0, SUBLANE), :] = h
        return jnp.broadcast_to(h[SUBLANE - 1:SUBLANE, :], (SUBLANE, width))

    hcar[...] = lax.fori_loop(0, n // SUBLANE, step, hcar[...], unroll=4)


def _scan_chunk_bwd(b_sc, d_sc, out_ref, gcar, n, width):
    rowi = _row_iota((SUBLANE, width))
    nc = n // SUBLANE

    def step(c, gnext):
        r0 = pl.multiple_of((nc - 1 - c) * SUBLANE, SUBLANE)
        b = b_sc[pl.ds(r0, SUBLANE), :]
        d = d_sc[pl.ds(r0, SUBLANE), :]
        for s in (1, 2, 4):
            keep = rowi < SUBLANE - s
            b_s = jnp.where(keep, pltpu.roll(b, SUBLANE - s, axis=0), 1.0)
            d_s = jnp.where(keep, pltpu.roll(d, SUBLANE - s, axis=0), 0.0)
            d = d + b * d_s
            b = b * b_s
        g = d + b * gnext
        out_ref[pl.ds(r0, SUBLANE), :] = g
        return jnp.broadcast_to(g[0:1, :], (SUBLANE, width))

    gcar[...] = lax.fori_loop(0, nc, step, gcar[...], unroll=4)


def even_mid_fwd(u, conv_a, conv_b, conv_b_bias, rw, rb, iw, ib, lam, nb, tp, n, name, carry=None):
    rows = u.shape[0]
    w = EV_CT
    nj = CONV_W // w
    nt = tp // n
    h8 = SUBLANE

    def body(gb_r, gc_r, xa_r, xb_r, gate_r, ca_w, cb_w, cb_b, rw_r, rb_r, iw_r, ib_r, lam_r,
             y_o, ca_o, xc_o, a_o, hs_o, pext, xext, hcar, a_sc, u_sc):
        t = pl.program_id(2)

        @pl.when(t == 0)
        def _():
            pext[0:h8, :] = jnp.zeros((h8, w), F32)
            xext[0:h8, :] = jnp.zeros((h8, w), F32)
            hcar[...] = jnp.zeros_like(hcar)

        p = gc_r[...] * xa_r[...]
        pext[h8:h8 + n, :] = p
        wa = ca_w[...]
        ca = wa[2:3, :] * p + wa[1:2, :] * pext[h8 - 1:h8 - 1 + n, :] + wa[0:1, :] * pext[h8 - 2:h8 - 2 + n, :]
        ca_o[...] = ca
        y_o[0] = (gb_r[...] * ca).astype(MXU_DT)
        pext[0:h8, :] = pext[n:n + h8, :]

        xb = xb_r[...]
        xext[h8:h8 + n, :] = xb
        wb = cb_w[...]
        xc = (wb[3:4, :] * xb + wb[2:3, :] * xext[h8 - 1:h8 - 1 + n, :] + wb[1:2, :] * xext[h8 - 2:h8 - 2 + n, :]
              + wb[0:1, :] * xext[h8 - 3:h8 - 3 + n, :]) + cb_b[...]
        xc_o[...] = xc
        xext[0:h8, :] = xext[n:n + h8, :]

        xcm = xc.astype(MXU_DT)
        r = _sigmoid(jnp.dot(xcm, rw_r[...], preferred_element_type=F32) + rb_r[...])
        ig = _sigmoid(jnp.dot(xcm, iw_r[...], preferred_element_type=F32) + ib_r[...])
        log_a = (-LRU_C) * r * _softplus(-lam_r[...])
        a = jnp.exp(log_a)
        mult = jnp.sqrt(-_expm1(2.0 * log_a))
        a_sc[...] = a
        a_o[...] = a
        u_sc[...] = mult * (ig * xc)
        _scan_chunk_fwd(a_sc, u_sc, hs_o, hcar, n, w)
        gel, _ = _gelu_and_grad(gate_r[...])
        y_o[1] = (gel * hs_o[...]).astype(MXU_DT)

    def ublk(off):
        return pl.BlockSpec((n, w), lambda j, b, t: (b * nt + t, off + j))

    def pblk(r_):
        return pl.BlockSpec((r_, w), lambda j, b, t: (0, j))

    act = pl.BlockSpec((n, w), lambda j, b, t: (b * nt + t, j))
    mat = pl.BlockSpec((w, w), lambda j, b, t: (j, j))
    return _call(
        body, (u, u, u, u, u, conv_a, conv_b, conv_b_bias, rw, rb, iw, ib, lam), carry, grid=(nj, nb, nt),
        in_specs=[ublk(0), ublk(nj), ublk(2 * nj), ublk(3 * nj), ublk(4 * nj), pblk(3), pblk(4), pblk(1),
                  mat, pblk(1), mat, pblk(1), pblk(1)],
        out_specs=[pl.BlockSpec((2, n, w), lambda j, b, t: (0, b * nt + t, j)), act, act, act, act],
        out_shape=[S((2, rows, CONV_W), MXU_DT), S((rows, CONV_W), F32), S((rows, LRU_W), F32), S((rows, LRU_W), F32),
                   S((rows, LRU_W), F32)],
        scratch_shapes=[pltpu.VMEM((n + h8, w), F32), pltpu.VMEM((n + h8, w), F32), pltpu.VMEM((h8, w), F32),
                        pltpu.VMEM((n, w), F32), pltpu.VMEM((n, w), F32)],
        sem=("parallel", "parallel", "arbitrary"), name=name)


def even_mid_bwd(u, dycat, ca, xc, a_sv, hs, conv_a, conv_b, rw, rb, iw, ib, lam, nb, tp, n, name, carry=None):
    rows = u.shape[0]
    w = EV_CT
    nj = CONV_W // w
    nt = tp // n
    h8 = SUBLANE

    def body(gb_r, gc_r, xa_r, xb_r, gate_r, dya_r, dyb_r, ca_r, xc_r, a_r, hs_r, hsp_r,
             ca_w, cb_w, rw_r, rb_r, iw_r, ib_r, lam_r,
             du_o, dca_w, dcb_w, dcb_b, drw, drb, diw, dib, dlam,
             aext, hext, dext, eext, gcar, b_sc, d_sc, g_sc):
        b, t = pl.program_id(1), pl.program_id(2)

        @pl.when((b == 0) & (t == 0))
        def _():
            for ref in (dca_w, dcb_w, dcb_b, drw, drb, diw, dib, dlam):
                ref[...] = jnp.zeros_like(ref)

        @pl.when(t == 0)
        def _():
            aext[n:n + h8, :] = jnp.zeros((h8, w), F32)
            dext[n:n + h8, :] = jnp.zeros((h8, w), F32)
            eext[n:n + h8, :] = jnp.zeros((h8, w), F32)
            gcar[...] = jnp.zeros_like(gcar)

        xc_v = xc_r[...]
        xcm = xc_v.astype(MXU_DT)
        r = _sigmoid(jnp.dot(xcm, rw_r[...], preferred_element_type=F32) + rb_r[...])
        ig = _sigmoid(jnp.dot(xcm, iw_r[...], preferred_element_type=F32) + ib_r[...])
        lam_v = lam_r[...]
        sp = _softplus(-lam_v)
        log_a = (-LRU_C) * r * sp
        a = a_r[...]
        mult = jnp.sqrt(-_expm1(2.0 * log_a))
        hs_v = hs_r[...]
        gel, dgel = _gelu_and_grad(gate_r[...])
        dyb = dyb_r[...]
        du_o[4] = (dyb * hs_v * dgel).astype(MXU_DT)

        aext[0:n, :] = a
        b_sc[...] = aext[1:1 + n, :]
        d_sc[...] = dyb * gel
        _scan_chunk_bwd(b_sc, d_sc, g_sc, gcar, n, w)
        aext[n:n + h8, :] = aext[0:h8, :]
        g = g_sc[...]

        hext[0:h8, :] = jnp.where(t == nt - 1, 0.0, hsp_r[...])
        hext[h8:h8 + n, :] = hs_v
        da = g * hext[h8 - 1:h8 - 1 + n, :]
        dmult = g * (ig * xc_v)
        di = g * mult * xc_v
        dxc = g * mult * ig
        dlog_a = da * a - dmult * (a * a) / mult
        dr = dlog_a * ((-LRU_C) * sp)
        dsp = jnp.sum(dlog_a * ((-LRU_C) * r), axis=0, keepdims=True)
        dlam[...] += dsp * (-_sigmoid(-lam_v))
        dzr = dr * r * (1.0 - r)
        dzi = di * ig * (1.0 - ig)
        dzr_m = dzr.astype(MXU_DT)
        dzi_m = dzi.astype(MXU_DT)
        dxc = (dxc + lax.dot_general(dzr_m, rw_r[...], NT_DIMS, preferred_element_type=F32)
               + lax.dot_general(dzi_m, iw_r[...], NT_DIMS, preferred_element_type=F32))
        drw[...] += lax.dot_general(xcm, dzr_m, TN_DIMS, preferred_element_type=F32)
        diw[...] += lax.dot_general(xcm, dzi_m, TN_DIMS, preferred_element_type=F32)
        drb[...] += jnp.sum(dzr, axis=0, keepdims=True)
        dib[...] += jnp.sum(dzi, axis=0, keepdims=True)
        dcb_b[...] += jnp.sum(dxc, axis=0, keepdims=True)

        xb = xb_r[...]
        dext[0:n, :] = dxc
        wb = cb_w[...]
        d1, d2, d3 = dext[1:1 + n, :], dext[2:2 + n, :], dext[3:3 + n, :]
        du_o[3] = (wb[3:4, :] * dxc + wb[2:3, :] * d1 + wb[1:2, :] * d2 + wb[0:1, :] * d3).astype(MXU_DT)
        dcb_w[3:4, :] += jnp.sum(xb * dxc, axis=0, keepdims=True)
        dcb_w[2:3, :] += jnp.sum(xb * d1, axis=0, keepdims=True)
        dcb_w[1:2, :] += jnp.sum(xb * d2, axis=0, keepdims=True)
        dcb_w[0:1, :] += jnp.sum(xb * d3, axis=0, keepdims=True)
        dext[n:n + h8, :] = dext[0:h8, :]

        gb, gc, xa = gb_r[...], gc_r[...], xa_r[...]
        dya = dya_r[...]
        du_o[0] = (dya * ca_r[...]).astype(MXU_DT)
        dca = dya * gb
        eext[0:n, :] = dca
        wa = ca_w[...]
        e1, e2 = eext[1:1 + n, :], eext[2:2 + n, :]
        dp = wa[2:3, :] * dca + wa[1:2, :] * e1 + wa[0:1, :] * e2
        p = gc * xa
        dca_w[2:3, :] += jnp.sum(p * dca, axis=0, keepdims=True)
        dca_w[1:2, :] += jnp.sum(p * e1, axis=0, keepdims=True)
        dca_w[0:1, :] += jnp.sum(p * e2, axis=0, keepdims=True)
        eext[n:n + h8, :] = eext[0:h8, :]
        du_o[1] = (dp * xa).astype(MXU_DT)
        du_o[2] = (dp * gc).astype(MXU_DT)

    def rt(b, t):
        return b * nt + (nt - 1 - t)

    def ublk(off):
        return pl.BlockSpec((n, w), lambda j, b, t: (rt(b, t), off + j))

    def pblk(r_):
        return pl.BlockSpec((r_, w), lambda j, b, t: (0, j))

    act = pl.BlockSpec((n, w), lambda j, b, t: (rt(b, t), j))
    n8 = n // h8
    hsp = pl.BlockSpec((h8, w), lambda j, b, t: (jnp.maximum(rt(b, t) * n8 - 1, 0), j))
    mat = pl.BlockSpec((w, w), lambda j, b, t: (j, j))
    return _call(
        body, (u, u, u, u, u, dycat, dycat, ca, xc, a_sv, hs, hs, conv_a, conv_b, rw, rb, iw, ib, lam), carry,
        grid=(nj, nb, nt),
        in_specs=[ublk(0), ublk(nj), ublk(2 * nj), ublk(3 * nj), ublk(4 * nj), ublk(0), ublk(nj), act, act, act, act,
                  hsp, pblk(3), pblk(4), mat, pblk(1), mat, pblk(1), pblk(1)],
        out_specs=[pl.BlockSpec((5, n, w), lambda j, b, t: (0, rt(b, t), j)), pblk(3), pblk(4), pblk(1),
                   mat, pblk(1), mat, pblk(1), pblk(1)],
        out_shape=[S((5, rows, CONV_W), MXU_DT), S((3, CONV_W), F32), S((4, LRU_W), F32), S((1, LRU_W), F32),
                   S((LRU_W, LRU_W), F32), S((1, LRU_W), F32), S((LRU_W, LRU_W), F32), S((1, LRU_W), F32),
                   S((1, LRU_W), F32)],
        scratch_shapes=[pltpu.VMEM((n + h8, w), F32)] * 4 + [pltpu.VMEM((h8, w), F32)] + [pltpu.VMEM((n, w), F32)] * 3,
        sem=("arbitrary", "arbitrary", "arbitrary"), name=name)


def ffn_mid_fwd(up, cw, cb, nb, tp, n, name):
    rows = up.shape[0]
    w = FFN_CT
    nj = D_FF // w
    nt = tp // n
    h8 = SUBLANE

    sr = STRIP_ROWS

    def body(xa_r, xg_r, w_r, b_r, u_o, y_o, halo):
        t = pl.program_id(2)

        @pl.when(t == 0)
        def _():
            halo[...] = jnp.zeros_like(halo)

        wv = (w_r[0], w_r[1])
        bv = (b_r[0], b_r[1])

        def strip(s, carry):
            r0 = pl.multiple_of(s * sr, sr)
            us, new = [], []
            for g, x_r in enumerate((xa_r, xg_r)):
                x = x_r[pl.ds(r0, sr), :].astype(F32)
                win = jnp.concatenate([carry[g], x], axis=0)
                x1 = pltpu.roll(win, 1, axis=0)[h8:, :]
                x2 = pltpu.roll(win, 2, axis=0)[h8:, :]
                u = (wv[g][2:3, :] * x + wv[g][1:2, :] * x1 + wv[g][0:1, :] * x2) + bv[g]
                u_o[g, pl.ds(r0, sr), :] = u.astype(MXU_DT)
                us.append(u)
                new.append(x[sr - h8:, :])
            y_o[pl.ds(r0, sr), :] = (us[0] * _sigmoid(us[0]) * us[1]).astype(MXU_DT)
            return tuple(new)

        ha, hg = lax.fori_loop(0, n // sr, strip, (halo[0], halo[1]))
        halo[0] = ha
        halo[1] = hg

    def ublk(off):
        return pl.BlockSpec((n, w), lambda j, b, t: (b * nt + t, off + j))

    return pl.pallas_call(
        body, grid=(nj, nb, nt),
        in_specs=[ublk(0), ublk(nj), pl.BlockSpec((2, 3, w), lambda j, b, t: (0, 0, j)),
                  pl.BlockSpec((2, 1, w), lambda j, b, t: (0, 0, j))],
        out_specs=[pl.BlockSpec((2, n, w), lambda j, b, t: (0, b * nt + t, j)), ublk(0)],
        out_shape=[S((2, rows, D_FF), MXU_DT), S((rows, D_FF), MXU_DT)],
        scratch_shapes=[pltpu.VMEM((2, h8, w), F32)],
        compiler_params=_cp(("parallel", "parallel", "arbitrary")), name=name,
    )(up, up, cw, cb)


def ffn_mid_bwd(dy, u, up, cw, nb, tp, n, name, carry=None):
    rows = up.shape[0]
    w = FFN_CT
    nj = D_FF // w
    nt = tp // n
    h8 = SUBLANE

    sr = STRIP_ROWS
    ns = n // sr

    def fold(v):
        acc = v[0:h8, :]
        for k in range(1, sr // h8):
            acc = acc + v[k * h8:(k + 1) * h8, :]
        return acc

    def body(dy_r, u_r, xa_r, xg_r, w_r, dx_o, dw, db, halo):
        b, t = pl.program_id(1), pl.program_id(2)

        @pl.when((b == 0) & (t == 0))
        def _():
            dw[...] = jnp.zeros_like(dw)
            db[...] = jnp.zeros_like(db)

        @pl.when(t == 0)
        def _():
            halo[...] = jnp.zeros_like(halo)

        wv = (w_r[0], w_r[1])

        def strip(s, carry):
            halos, sums = carry
            r0 = pl.multiple_of((ns - 1 - s) * sr, sr)
            dyv = dy_r[pl.ds(r0, sr), :].astype(F32)
            ua = u_r[0, pl.ds(r0, sr), :].astype(F32)
            ug = u_r[1, pl.ds(r0, sr), :].astype(F32)
            sg = _sigmoid(ua)
            dus = (dyv * ug * (sg * (1.0 + ua * (1.0 - sg))), dyv * (ua * sg))
            new_halos, new_sums = [], []
            for g, x_r in enumerate((xa_r, xg_r)):
                du = dus[g]
                win = jnp.concatenate([du, halos[g]], axis=0)
                d1 = pltpu.roll(win, sr + h8 - 1, axis=0)[0:sr, :]
                d2 = pltpu.roll(win, sr + h8 - 2, axis=0)[0:sr, :]
                dx_o[g, pl.ds(r0, sr), :] = (wv[g][2:3, :] * du + wv[g][1:2, :] * d1 + wv[g][0:1, :] * d2).astype(MXU_DT)
                x = x_r[pl.ds(r0, sr), :].astype(F32)
                s2, s1, s0, sb = sums[g]
                new_sums.append((s2 + fold(x * du), s1 + fold(x * d1), s0 + fold(x * d2), sb + fold(du)))
                new_halos.append(du[0:h8, :])
            return tuple(new_halos), tuple(new_sums)

        z = jnp.zeros((h8, w), F32)
        halos, sums = lax.fori_loop(0, ns, strip, ((halo[0], halo[1]), ((z, z, z, z), (z, z, z, z))))
        halo[0] = halos[0]
        halo[1] = halos[1]
        for g in range(2):
            s2, s1, s0, sb = sums[g]
            dw[g, 2:3, :] += jnp.sum(s2, axis=0, keepdims=True)
            dw[g, 1:2, :] += jnp.sum(s1, axis=0, keepdims=True)
            dw[g, 0:1, :] += jnp.sum(s0, axis=0, keepdims=True)
            db[g] += jnp.sum(sb, axis=0, keepdims=True)

    def rt(b, t):
        return b * nt + (nt - 1 - t)

    def ublk(off):
        return pl.BlockSpec((n, w), lambda j, b, t: (rt(b, t), off + j))

    pair = pl.BlockSpec((2, n, w), lambda j, b, t: (0, rt(b, t), j))
    return _call(
        body, (dy, u, up, up, cw), carry, grid=(nj, nb, nt),
        in_specs=[ublk(0), pair, ublk(0), ublk(nj), pl.BlockSpec((2, 3, w), lambda j, b, t: (0, 0, j))],
        out_specs=[pair, pl.BlockSpec((2, 3, w), lambda j, b, t: (0, 0, j)),
                   pl.BlockSpec((2, 1, w), lambda j, b, t: (0, 0, j))],
        out_shape=[S((2, rows, D_FF), MXU_DT), S((2, 3, D_FF), F32), S((2, 1, D_FF), F32)],
        scratch_shapes=[pltpu.VMEM((2, h8, w), F32)],
        sem=("arbitrary", "arbitrary", "arbitrary"), name=name)


def _lane_mod(shape):
    return lax.broadcasted_iota(jnp.int32, shape, 1) & (HP - 1)


def _q_rope_epi(acc, tab):
    reps = acc.shape[1] // HP
    a = acc * jnp.tile(tab, (1, reps))
    lane = _lane_mod(a.shape)
    shifted = pltpu.roll(a, a.shape[1] - QK_ROPE, axis=1)
    return jnp.where(lane < QK_NOPE, a, jnp.where(lane < QK_HEAD, a + shifted, 0.0)) * Q_PRESCALE


def _k_rope_block(krblk, tabk):
    a = krblk * tabk
    lane = _lane_mod(a.shape)
    b = a + pltpu.roll(a, HP - QK_ROPE, axis=1)
    return jnp.where((lane >= QK_NOPE) & (lane < QK_HEAD), b, 0.0)


def _k_rope_epi(acc, krblk, tabk):
    reps = acc.shape[1] // HP
    return acc + jnp.tile(_k_rope_block(krblk, tabk), (1, reps))


def attn_fwd(q, k, v, nb, tp, name, carry=None):
    rows = q.shape[0]
    blk = ATT_BLK
    nq = tp // blk
    npair = MLA_HEADS // 2

    def body(q_r, k_r, v_r, o_r, lse_r):
        qi = pl.program_id(2)
        lane = lax.broadcasted_iota(jnp.int32, (blk, LANE), 1)
        even = lane < V_HEAD
        sum_lane = (V_HEAD, 0)
        rowi = lax.broadcasted_iota(jnp.int32, (blk, blk), 0)
        coli = lax.broadcasted_iota(jnp.int32, (blk, blk), 1)
        qs = [q_r[:, h * HP:(h + 1) * HP] for h in range(2)]

        def kv_block(k0, width, carry, visible):
            ms, accs = carry
            vblk = v_r[pl.ds(k0, width), :]
            one = jnp.ones_like(vblk)
            zero = jnp.zeros_like(vblk)
            vlane = lax.broadcasted_iota(jnp.int32, (width, LANE), 1)
            ss = [lax.dot_general(qs[h], k_r[pl.ds(k0, width), h * HP:(h + 1) * HP], NT_DIMS,
                                  preferred_element_type=F32) for h in range(2)]
            new_ms, new_accs = [], []
            for h in range(2):
                s = ss[h]
                if visible is not None:
                    s = jnp.where(visible, s, -jnp.inf)
                m_new = jnp.maximum(ms[h], jnp.max(s, axis=1, keepdims=True))
                alpha = jnp.exp2(ms[h] - m_new)
                p = jnp.exp2(s - m_new).astype(MXU_DT)
                mine = (vlane < V_HEAD) if h == 0 else (vlane >= V_HEAD)
                vh = jnp.where(mine, vblk, jnp.where(vlane == sum_lane[h], one, zero))
                new_accs.append(alpha * accs[h] + jnp.dot(p, vh, preferred_element_type=F32))
                new_ms.append(m_new)
            return tuple(new_ms), tuple(new_accs)

        neg = jnp.full((blk, 1), -jnp.inf, F32)
        zacc = jnp.zeros((blk, LANE), F32)
        carry = lax.fori_loop(0, qi // 2, lambda i, c: kv_block(pl.multiple_of(i * 2 * blk, blk), 2 * blk, c, None),
                              ((neg, neg), (zacc, zacc)))
        rowi2 = lax.broadcasted_iota(jnp.int32, (blk, 2 * blk), 0)
        coli2 = lax.broadcasted_iota(jnp.int32, (blk, 2 * blk), 1)
        ms, accs = lax.cond(
            qi % 2 == 1,
            lambda c: kv_block(pl.multiple_of((qi - 1) * blk, blk), 2 * blk, c, coli2 - blk <= rowi2),
            lambda c: kv_block(pl.multiple_of(qi * blk, blk), blk, c, coli <= rowi), carry)
        ls = [accs[h][:, sum_lane[h]:sum_lane[h] + 1] for h in range(2)]
        o_r[...] = jnp.where(even, accs[0] / ls[0], accs[1] / ls[1]).astype(MXU_DT)
        lse_r[...] = jnp.where(even, ms[0] + jnp.log2(ls[0]), ms[1] + jnp.log2(ls[1]))

    return _call(
        body, (q, k, v), carry, grid=(nb, npair, nq),
        in_specs=[pl.BlockSpec((blk, 2 * HP), lambda b, p, i: (b * nq + i, p)),
                  pl.BlockSpec((tp, 2 * HP), lambda b, p, i: (b, p)),
                  pl.BlockSpec((tp, LANE), lambda b, p, i: (b, p))],
        out_specs=[pl.BlockSpec((blk, LANE), lambda b, p, i: (b * nq + i, p)),
                   pl.BlockSpec((None, blk, LANE), lambda b, p, i: (p, b * nq + i, 0))],
        out_shape=[S((rows, MLA_HEADS * V_HEAD), MXU_DT), S((npair, rows, LANE), F32)], scratch_shapes=[],
        sem=("parallel", "parallel", "arbitrary"), name=name)


def attn_bwd(q, k, v, o, do, lse, nb, tp, name, carry=None):
    rows = q.shape[0]
    blk = ATT_BLK
    nq = tp // blk
    npair = MLA_HEADS // 2
    scale = QK_HEAD ** -0.5

    def body(q_r, k_r, v_r, o_r, do_r, lse_r, dq_o, dk_o, dv_o, dq_acc, delta_sc):
        kb = pl.program_id(2)
        even = lax.broadcasted_iota(jnp.int32, (blk, LANE), 1) < V_HEAD
        rowi = lax.broadcasted_iota(jnp.int32, (blk, blk), 0)
        coli = lax.broadcasted_iota(jnp.int32, (blk, blk), 1)

        @pl.when(kb == 0)
        def _():
            dq_acc[...] = jnp.zeros_like(dq_acc)

            def dstep(i, c):
                r0 = pl.multiple_of(i * blk, blk)
                prod = do_r[pl.ds(r0, blk), :].astype(F32) * o_r[pl.ds(r0, blk), :].astype(F32)
                de = jnp.sum(jnp.where(even, prod, 0.0), axis=1, keepdims=True)
                dd = jnp.sum(jnp.where(even, 0.0, prod), axis=1, keepdims=True)
                delta_sc[pl.ds(r0, blk), :] = jnp.where(even, de, dd)
                return c

            lax.fori_loop(0, nq, dstep, 0)

        vblk = v_r[...]
        ks = [k_r[:, h * HP:(h + 1) * HP] for h in range(2)]

        def q_block(r0, height, carry, visible):
            dk0, dk1, dv = carry
            dob = do_r[pl.ds(r0, height), :]
            lse_b = lse_r[pl.ds(r0, height), :]
            dl_b = delta_sc[pl.ds(r0, height), :]
            qlane = lax.broadcasted_iota(jnp.int32, (height, LANE), 1)
            dks = [dk0, dk1]
            qhs = [q_r[pl.ds(r0, height), h * HP:(h + 1) * HP] for h in range(2)]
            dohs = [jnp.where((qlane < V_HEAD) if h == 0 else (qlane >= V_HEAD), dob, jnp.zeros_like(dob))
                    for h in range(2)]
            ss = [lax.dot_general(qhs[h], ks[h], NT_DIMS, preferred_element_type=F32) for h in range(2)]
            dps = [lax.dot_general(dohs[h], vblk, NT_DIMS, preferred_element_type=F32) for h in range(2)]
            for h in range(2):
                lo = 0 if h == 0 else V_HEAD
                p = jnp.exp2(ss[h] - lse_b[:, lo:lo + 1])
                if visible is not None:
                    p = jnp.where(visible, p, 0.0)
                ds = (p * (dps[h] - dl_b[:, lo:lo + 1])).astype(MXU_DT)
                dv = dv + lax.dot_general(p.astype(MXU_DT), dohs[h], TN_DIMS, preferred_element_type=F32)
                dks[h] = dks[h] + lax.dot_general(ds, qhs[h], TN_DIMS, preferred_element_type=F32)
                dq_acc[pl.ds(r0, height), h * HP:(h + 1) * HP] += jnp.dot(ds, ks[h], preferred_element_type=F32)
            return dks[0], dks[1], dv

        z = jnp.zeros((blk, HP), F32)
        below = nq - 1 - kb
        odd = below % 2
        rowi2 = lax.broadcasted_iota(jnp.int32, (2 * blk, blk), 0)
        coli2 = lax.broadcasted_iota(jnp.int32, (2 * blk, blk), 1)
        first = pl.multiple_of(kb * blk, blk)
        carry = lax.cond(odd == 1, lambda c: q_block(first, 2 * blk, c, coli2 <= rowi2),
                         lambda c: q_block(first, blk, c, coli <= rowi), (z, z, jnp.zeros((blk, LANE), F32)))
        dk0, dk1, dv = lax.fori_loop(
            0, below // 2, lambda i, c: q_block(pl.multiple_of((kb + 1 + odd + 2 * i) * blk, blk), 2 * blk, c, None),
            carry)
        dk_o[:, 0:HP] = (dk0 * (scale / Q_PRESCALE)).astype(MXU_DT)
        dk_o[:, HP:2 * HP] = (dk1 * (scale / Q_PRESCALE)).astype(MXU_DT)
        dv_o[...] = dv.astype(MXU_DT)

        @pl.when(kb == nq - 1)
        def _():
            dq_o[...] = (dq_acc[...] * scale).astype(MXU_DT)

    seq_pair = pl.BlockSpec((tp, LANE), lambda b, p, kk: (b, p))
    return _call(
        body, (q, k, v, o, do, lse), carry, grid=(nb, npair, nq),
        in_specs=[pl.BlockSpec((tp, 2 * HP), lambda b, p, kk: (b, p)),
                  pl.BlockSpec((blk, 2 * HP), lambda b, p, kk: (b * nq + kk, p)),
                  pl.BlockSpec((blk, LANE), lambda b, p, kk: (b * nq + kk, p)),
                  seq_pair, seq_pair, pl.BlockSpec((None, tp, LANE), lambda b, p, kk: (p, b, 0))],
        out_specs=[pl.BlockSpec((tp, 2 * HP), lambda b, p, kk: (b, p)),
                   pl.BlockSpec((blk, 2 * HP), lambda b, p, kk: (b * nq + kk, p)),
                   pl.BlockSpec((blk, LANE), lambda b, p, kk: (b * nq + kk, p))],
        out_shape=[S((rows, MLA_HEADS * HP), MXU_DT), S((rows, MLA_HEADS * HP), MXU_DT),
                   S((rows, MLA_HEADS * V_HEAD), MXU_DT)],
        scratch_shapes=[pltpu.VMEM((tp, 2 * HP), F32), pltpu.VMEM((tp, LANE), F32)],
        sem=("parallel", "parallel", "arbitrary"), name=name)


def rope_bwd(dq, dk, dv, tabq, tabk, tp, tm, name):
    rows = dq.shape[0]
    nt = tp // tm
    wq = MLA_HEADS * HP

    def body(dq_r, dk_r, dv_r, tq_r, tk_r, dqa_o, dkv_o, dkr_o):
        dqv = dq_r[...].astype(F32)
        lane = _lane_mod(dqv.shape)
        in_rope = (lane >= QK_NOPE) & (lane < QK_HEAD)
        rope = jnp.where(in_rope, dqv, 0.0)
        da = jnp.where(lane < QK_HEAD, dqv, 0.0) + pltpu.roll(rope, QK_ROPE, axis=1)
        dqa_o[...] = (da * jnp.tile(tq_r[...], (1, MLA_HEADS))).astype(MXU_DT)
        dkf = dk_r[...].astype(F32)
        dkv_o[:, 0:wq] = jnp.where(lane < QK_NOPE, dkf, 0.0).astype(MXU_DT)
        dkv_o[:, wq:] = dv_r[...]
        kr = jnp.where(in_rope, dkf, 0.0)
        tot = kr[:, 0:HP]
        for h in range(1, MLA_HEADS):
            tot = tot + kr[:, h * HP:(h + 1) * HP]
        dkr_o[...] = ((tot + pltpu.roll(tot, QK_ROPE, axis=1)) * tk_r[...]).astype(MXU_DT)

    def rowblk(wd):
        return pl.BlockSpec((tm, wd), lambda i: (i, 0))

    tab = pl.BlockSpec((tm, HP), lambda i: (i % nt, 0))
    return pl.pallas_call(
        body, grid=(rows // tm,), in_specs=[rowblk(wq), rowblk(wq), rowblk(MLA_HEADS * V_HEAD), tab, tab],
        out_specs=[rowblk(wq), rowblk(wq + MLA_HEADS * V_HEAD), rowblk(HP)],
        out_shape=[S((rows, wq), MXU_DT), S((rows, wq + MLA_HEADS * V_HEAD), MXU_DT), S((rows, HP), MXU_DT)],
        compiler_params=_cp(("parallel",)), name=name)(dq, dk, dv, tabq, tabk)


def loss_head(h, target, gain, tp, t_real, tm, name):
    rows = h.shape[0]
    nt = tp // tm

    def body(h_r, t_r, g_r, dh_o, loss_o, dg_o):
        i = pl.program_id(0)

        @pl.when(i == 0)
        def _():
            loss_o[...] = jnp.zeros_like(loss_o)
            dg_o[...] = jnp.zeros_like(dg_o)

        xv = h_r[...]
        rstd = lax.rsqrt(jnp.mean(xv * xv, axis=-1, keepdims=True) + EPS)
        xhat = xv * rstd
        g = g_r[...]
        pos = (i % nt) * tm + lax.broadcasted_iota(jnp.int32, (tm, 1), 0)
        valid = (pos >= N_META) & (pos < t_real)
        err = jnp.where(valid, xhat * g - t_r[...], 0.0)
        loss_o[...] += 0.5 * jnp.sum(jnp.mean(err * err, axis=-1, keepdims=True))
        dy = err * (1.0 / D_MODEL)
        dg_o[...] += jnp.sum(dy * xhat, axis=0, keepdims=True)
        dxh = dy * g
        dh_o[...] = rstd * (dxh - xhat * jnp.mean(dxh * xhat, axis=-1, keepdims=True))

    blk = pl.BlockSpec((tm, D_MODEL), lambda i: (i, 0))
    return pl.pallas_call(
        body, grid=(rows // tm,), in_specs=[blk, blk, pl.BlockSpec((1, D_MODEL), lambda i: (0, 0))],
        out_specs=[blk, pl.BlockSpec((1, LANE), lambda i: (0, 0)), pl.BlockSpec((1, D_MODEL), lambda i: (0, 0))],
        out_shape=[S((rows, D_MODEL), F32), S((1, LANE), F32), S((1, D_MODEL), F32)],
        compiler_params=_cp(("arbitrary",)), name=name)(h, target, gain)


ADAM_TILE_ELEMS = 128 * 1024


def adamw(g, w, m, v, name, carry=None):
    shape = w.shape
    cols = shape[-1]
    rws = max(1, math.prod(shape[:-1]))
    tr = rws if rws * cols <= ADAM_TILE_ELEMS else _div_tile(rws, max(SUBLANE, ADAM_TILE_ELEMS // cols), SUBLANE)
    bc1 = 1.0 - ADAM_B1 ** ADAM_STEP
    bc2 = 1.0 - ADAM_B2 ** ADAM_STEP

    def body(g_r, w_r, m_r, v_r, go, do, mo, vo):
        gv = g_r[...]
        mn = ADAM_B1 * m_r[...] + (1.0 - ADAM_B1) * gv
        vn = ADAM_B2 * v_r[...] + (1.0 - ADAM_B2) * (gv * gv)
        m_hat = mn / bc1
        v_hat = vn / bc2
        go[...] = gv
        do[...] = -ADAM_LR * (m_hat / (jnp.sqrt(v_hat) + ADAM_EPS) + ADAM_WD * w_r[...])
        mo[...] = mn
        vo[...] = vn

    blk = pl.BlockSpec((tr, cols), lambda i: (i, 0))
    outs, got = _call(
        body, [a.reshape(rws, cols) for a in (g, w, m, v)], carry, grid=(rws // tr,), in_specs=[blk] * 4,
        out_specs=[blk] * 4, out_shape=[S((rws, cols), F32)] * 4, scratch_shapes=[], sem=("parallel",), name=name)
    return tuple(o.reshape(shape) for o in outs), got


SUM_TILE_ELEMS = 128 * 1024


def _place():
    return lax.axis_index("x"), lax.axis_index("y"), lax.axis_index("c")


def _remote(src, dst, send_sems, recv_sems, k, to):
    return pltpu.make_async_remote_copy(src_ref=src, dst_ref=dst, send_sem=send_sems.at[k], recv_sem=recv_sems.at[k],
                                        device_id=to, device_id_type=MESH)


def chip_index():
    return 2 * lax.axis_index("x") + lax.axis_index("y")


def _sem_pair(n):
    return [pltpu.SemaphoreType.DMA((n,)), pltpu.SemaphoreType.DMA((n,))]


def stage_gather_chips(xs):
    def copies(ins, outs, sems):
        send_sems, recv_sems = sems
        mx, my, mc = _place()
        sibling = (mx, my, 1 - mc)
        chips = [(1 - mx, my), (mx, 1 - my), (1 - mx, 1 - my)]
        first, landed, passed, from_sibling = [], [], [], []
        for i, (x_ref, out_ref) in enumerate(zip(ins, outs)):
            def piece(cx, cy, h, out_ref=out_ref):
                return out_ref.at[2 * cx + cy, h]

            for j, (cx, cy) in enumerate(chips):
                k = 6 * i + j
                first.append(_remote(x_ref.at[mc], piece(mx, my, mc), send_sems, recv_sems, k, (cx, cy, mc)))
                landed.append(_remote(x_ref.at[mc], piece(cx, cy, mc), send_sems, recv_sems, k, (cx, cy, mc)))
                passed.append(_remote(piece(cx, cy, mc), piece(cx, cy, mc), send_sems, recv_sems, k + 3, sibling))
                from_sibling.append(_remote(x_ref.at[mc], piece(cx, cy, 1 - mc), send_sems, recv_sems, k + 3, sibling))
        return first, landed, passed, from_sibling

    def start(ins, outs, sems):
        for cp in copies(ins, outs, sems)[0]:
            cp.start()

    def finish(ins, outs, sems):
        first, landed, passed, from_sibling = copies(ins, outs, sems)
        for arrived, onward in zip(landed, passed):
            arrived.wait_recv()
            onward.start()
        for cp in from_sibling:
            cp.wait_recv()
        for cp in first + passed:
            cp.wait_send()

    return Stage(list(xs), [S((4,) + x.shape, x.dtype) for x in xs], _sem_pair(6 * len(xs)), start, finish)


def own_block(gathered, xs):
    return lax.dynamic_update_slice(gathered, xs[None], (chip_index(), 0, 0, 0))


def stage_pair_exchange(gs):
    def copies(ins, outs, sems):
        send_sems, recv_sems = sems
        mx, my, mc = _place()
        return [_remote(g_ref.at[s, 1 - mc], land_ref.at[s], send_sems, recv_sems, 4 * i + s, (mx, my, 1 - mc))
                for i, (g_ref, land_ref) in enumerate(zip(ins, outs)) for s in range(4)]

    def start(ins, outs, sems):
        for cp in copies(ins, outs, sems):
            cp.start()

    def finish(ins, outs, sems):
        cps = copies(ins, outs, sems)
        for cp in cps:
            cp.wait_recv()
        for cp in cps:
            cp.wait_send()

    return Stage(list(gs), [S((4,) + g.shape[2:], g.dtype) for g in gs], _sem_pair(4 * len(gs)), start, finish)


def _sum_rows(rws, width):
    return _div_tile(rws, max(SUBLANE, SUM_TILE_ELEMS // width), SUBLANE)


def pair_sum(g4, land, c_idx, name):
    _, _, rws, wd = g4.shape
    th = _sum_rows(rws, wd)

    def body(c_ref, a_ref, b_ref, o_ref):
        o_ref[...] = a_ref[...] + b_ref[...]

    return pl.pallas_call(
        body,
        grid_spec=pltpu.PrefetchScalarGridSpec(
            num_scalar_prefetch=1, grid=(4, rws // th),
            in_specs=[pl.BlockSpec((None, None, th, wd), lambda s, i, c: (s, c[0], i, 0)),
                      pl.BlockSpec((None, th, wd), lambda s, i, c: (s, i, 0))],
            out_specs=pl.BlockSpec((None, th, wd), lambda s, i, c: (s, i, 0))),
        out_shape=S((4, rws, wd), F32), compiler_params=_cp(("parallel", "parallel")), name=name)(c_idx, g4, land)


def stage_chip_scatter(ps):
    def copies(ins, outs, sems):
        send_sems, recv_sems = sems
        mx, my, mc = _place()
        me = 2 * mx + my
        chips = [(1 - mx, my), (mx, 1 - my), (1 - mx, 1 - my)]
        sent, landed = [], []
        for i, (p_ref, land_ref) in enumerate(zip(ins, outs)):
            for j, (cx, cy) in enumerate(chips):
                k = 3 * i + j
                sent.append(_remote(p_ref.at[2 * cx + cy], land_ref.at[me], send_sems, recv_sems, k, (cx, cy, mc)))
                landed.append(_remote(p_ref.at[me], land_ref.at[2 * cx + cy], send_sems, recv_sems, k, (cx, cy, mc)))
        return sent, landed

    def start(ins, outs, sems):
        for cp in copies(ins, outs, sems)[0]:
            cp.start()

    def finish(ins, outs, sems):
        sent, landed = copies(ins, outs, sems)
        for cp in landed:
            cp.wait_recv()
        for cp in sent:
            cp.wait_send()

    return Stage(list(ps), [S(p.shape, p.dtype) for p in ps], _sem_pair(3 * len(ps)), start, finish)


def chip_sum(l4, p4, me_idx, name):
    _, rws, wd = l4.shape
    th = _sum_rows(rws, wd)

    def body(me_ref, a, b, c, d, own, o_ref):
        me = me_ref[0]
        parts = [jnp.where(me == s, own[...], r[...]) for s, r in enumerate((a, b, c, d))]
        o_ref[...] = ((parts[0] + parts[1]) + parts[2]) + parts[3]

    def blk(s):
        return pl.BlockSpec((None, th, wd), lambda i, me: (jnp.where(me[0] == s, (s + 1) % 4, s), i, 0))

    return pl.pallas_call(
        body,
        grid_spec=pltpu.PrefetchScalarGridSpec(
            num_scalar_prefetch=1, grid=(rws // th,),
            in_specs=[blk(0), blk(1), blk(2), blk(3), pl.BlockSpec((None, th, wd), lambda i, me: (me[0], i, 0))],
            out_specs=pl.BlockSpec((th, wd), lambda i, me: (i, 0))),
        out_shape=S((rws, wd), F32), compiler_params=_cp(("parallel",)), name=name)(me_idx, l4, l4, l4, l4, p4)


def stage_pair_gather(rs):
    def copies(ins, outs, sems):
        send_sems, recv_sems = sems
        mx, my, mc = _place()
        return [_remote(r_ref, out_ref, send_sems, recv_sems, i, (mx, my, 1 - mc))
                for i, (r_ref, out_ref) in enumerate(zip(ins, outs))]

    def start(ins, outs, sems):
        for cp in copies(ins, outs, sems):
            cp.start()

    def finish(ins, outs, sems):
        for cp in copies(ins, outs, sems):
            cp.wait()

    return Stage(list(rs), [S(r.shape, r.dtype) for r in rs], _sem_pair(len(rs)), start, finish)


PACK_ELEMS = 16 * LANE


def pack_rows(arrays, lead, total_mult):
    parts, offs, r0 = [], [], 0
    for a in arrays:
        flat = a.reshape(a.shape[:lead] + (-1,))
        elems = _round_up(flat.shape[-1], PACK_ELEMS)
        flat = jnp.pad(flat, [(0, 0)] * lead + [(0, elems - flat.shape[-1])])
        parts.append(flat.reshape(flat.shape[:lead] + (elems // LANE, LANE)))
        offs.append((r0, elems // LANE))
        r0 += elems // LANE
    total = _round_up(r0, total_mult)
    if total > r0:
        parts.append(jnp.zeros(parts[0].shape[:lead] + (total - r0, LANE), parts[0].dtype))
    return jnp.concatenate(parts, axis=lead), offs


def unpack_rows(buf, off, shape):
    r0, nr = off
    lead = buf.shape[:-2]
    n = math.prod(shape)
    return buf[..., r0:r0 + nr, :].reshape(lead + (nr * LANE,))[..., :n].reshape(lead + tuple(shape))


def unshard(stacked, axis):
    x = jnp.moveaxis(stacked, 0, axis)
    return x.reshape(x.shape[:axis] + (4 * x.shape[axis + 1],) + x.shape[axis + 2:])


def to_shards(full, axis):
    n = full.shape[axis] // 4
    x = full.reshape(full.shape[:axis] + (4, n) + full.shape[axis + 1:])
    return jnp.moveaxis(x, axis, 0)


def _rot_cols(w):
    half = w.shape[-1] // 2
    return jnp.concatenate([-w[..., half:], w[..., :half]], axis=-1)


def _unrot_cols(dw):
    half = dw.shape[-1] // 2
    return jnp.concatenate([dw[..., half:], -dw[..., :half]], axis=-1)


def odd_w_in_padded(w_in):
    kr = w_in[:, Q_LORA + KV_LORA:]
    rows = w_in.shape[0]
    return jnp.concatenate([w_in[:, :Q_LORA], jnp.zeros((rows, 128), w_in.dtype), w_in[:, Q_LORA:Q_LORA + KV_LORA],
                            jnp.zeros((rows, 64), w_in.dtype), kr, _rot_cols(kr)], axis=1)


def odd_w_in_unpad(dwp):
    base = 512 + KV_LORA + 64
    dkr = dwp[:, base:base + QK_ROPE] + _unrot_cols(dwp[:, base + QK_ROPE:base + 2 * QK_ROPE])
    return jnp.concatenate([dwp[:, :Q_LORA], dwp[:, 512:512 + KV_LORA], dkr], axis=1)


def uq_padded(w_uq):
    w = w_uq.reshape(Q_LORA, MLA_HEADS, QK_HEAD)
    return jnp.concatenate([w, _rot_cols(w[:, :, QK_NOPE:])], axis=-1).reshape(Q_LORA, MLA_HEADS * HP)


def uq_unpad(dwp):
    d = dwp.reshape(Q_LORA, MLA_HEADS, HP)
    rope = d[:, :, QK_NOPE:QK_HEAD] + _unrot_cols(d[:, :, QK_HEAD:])
    return jnp.concatenate([d[:, :, :QK_NOPE], rope], axis=-1).reshape(Q_LORA, MLA_HEADS * QK_HEAD)


def ukv_padded(w_ukv):
    w = w_ukv.reshape(KV_LORA, MLA_HEADS, QK_NOPE + V_HEAD)
    wk = jnp.concatenate([w[:, :, :QK_NOPE], jnp.zeros((KV_LORA, MLA_HEADS, HP - QK_NOPE), w.dtype)], axis=-1)
    return jnp.concatenate([wk.reshape(KV_LORA, MLA_HEADS * HP), w[:, :, QK_NOPE:].reshape(KV_LORA, MLA_HEADS * V_HEAD)],
                           axis=1)


def ukv_unpad(dwp):
    dk = dwp[:, :MLA_HEADS * HP].reshape(KV_LORA, MLA_HEADS, HP)[:, :, :QK_NOPE]
    dv = dwp[:, MLA_HEADS * HP:].reshape(KV_LORA, MLA_HEADS, V_HEAD)
    return jnp.concatenate([dk, dv], axis=-1).reshape(KV_LORA, MLA_HEADS * (QK_NOPE + V_HEAD))


def block_diag(w):
    h, d, _ = w.shape
    eye = jnp.eye(h, dtype=w.dtype)
    return (eye[:, None, :, None] * w[:, :, None, :]).reshape(h * d, h * d)


def block_diag_part(dense, h):
    d = dense.shape[0] // h
    x = dense.reshape(h, d, h, d)
    return jnp.stack([x[i, :, i, :] for i in range(h)], axis=0)


def rope_tables(tp):
    pos = jnp.arange(tp, dtype=F32)
    inv_freq = ROPE_BASE ** (-jnp.arange(0, QK_ROPE, 2, dtype=F32) / QK_ROPE)
    ang = pos[:, None] * inv_freq[None, :]
    cos2 = jnp.tile(jnp.cos(ang), (1, 2))
    sin2 = jnp.tile(jnp.sin(ang), (1, 2))
    tabq = jnp.concatenate([jnp.ones((tp, QK_NOPE), F32), cos2, sin2], axis=1)
    tabk = jnp.concatenate([jnp.zeros((tp, QK_NOPE), F32), cos2, sin2], axis=1)
    return tabq, tabk


class Dims:
    def __init__(self, nb, seq):
        self.nb = nb
        self.t_real = seq + N_META
        self.tp = _round_up(self.t_real, ATT_BLK)
        self.n = self.tp // 4
        assert self.n % 16 == 0
        self.rows = nb * self.tp


class NoComm:
    def advance(self, carried):
        return None


def even_fwd(h, p, dm, comm):
    (u, hn), _ = norm_matmul(h, 0, D_MODEL, p["norm"], p["w_in"], dm.n, 512, F32, "ev_in")
    (y, ca, xc, a, hs), got = even_mid_fwd(u, p["conv_a"], p["conv_b"], p["conv_b_bias"], p["rw"], p["r_b"], p["iw"],
                                           p["i_b"], p["lam"], dm.nb, dm.tp, dm.n, "ev_mid", carry=comm.advance(None))
    comm.advance(got)
    out = matmul_res(y, p["w_out"].reshape(2, CONV_W, D_MODEL), h, dm.n, 512, "ev_out")
    return out, (h, u, hn, ca, xc, a, hs, y)


def even_bwd(dout, saved, p, dm, comm):
    h, u, hn, ca, xc, a, hs, y = saved
    g = {}
    dycat = matmul_nt(dout, p["w_out"], dm.n, 512, F32, "ev_dycat")
    g["w_out"], got = matmul_tn(y, dout, dm.n, "ev_dw_out", carry=comm.advance(None))
    outs, got = even_mid_bwd(u, dycat, ca, xc, a, hs, p["conv_a"], p["conv_b"], p["rw"], p["r_b"], p["iw"], p["i_b"],
                             p["lam"], dm.nb, dm.tp, dm.n, "ev_mid_bwd", carry=comm.advance(got))
    du, g["conv_a"], g["conv_b"], g["conv_b_bias"], drw, g["r_b"], diw, g["i_b"], g["lam"] = outs
    g["r_w"] = block_diag_part(drw, LRU_HEADS)
    g["i_w"] = block_diag_part(diw, LRU_HEADS)
    g["w_in"], got = matmul_tn(hn, du, dm.n, "ev_dw_in", carry=comm.advance(got))
    comm.advance(got)
    dx, g["norm"] = matmul_nt_normbwd(du, p["w_in"], h, 0, p["norm"], dout, dm.n, 512, F32, "ev_dx")
    return dx, g


def odd_fwd(h, p, tabq, tabk, dm, comm):
    nt = dm.tp // dm.n
    (u, hn), _ = norm_matmul(h, 0, D_MODEL, p["norm"], p["w_in_p"], dm.n, ODD_PAD, F32, "od_in")
    tab_spec = pl.BlockSpec((dm.n, HP), lambda i, j: (i % nt, 0))
    (q, cqn), _ = norm_matmul(u, 0, Q_LORA, p["q_norm"], p["w_uq_p"], dm.n, 512, MXU_DT, "od_q",
                              epi=_q_rope_epi, epi_ops=(tabq,), epi_specs=(tab_spec,))
    kr_spec = pl.BlockSpec((dm.n, HP), lambda i, j: (i, ODD_KR_COL))
    (k, ckvn), _ = norm_matmul(u, ODD_CKV_COL, KV_LORA, p["kv_norm"], p["w_uk_p"], dm.n, 512, MXU_DT, "od_k",
                               epi=_k_rope_epi, epi_ops=(u, tabk), epi_specs=(kr_spec, tab_spec))
    (v, _), _ = norm_matmul(u, ODD_CKV_COL, KV_LORA, p["kv_norm"], p["w_uv_p"], dm.n, 512, MXU_DT, "od_v")
    (o, lse), got = attn_fwd(q, k, v, dm.nb, dm.tp, "od_attn", carry=comm.advance(None))
    comm.advance(got)
    out = matmul_res(o[None], p["w_out"][None], h, dm.n, 512, "od_out")
    return out, (h, u, hn, cqn, ckvn, q, k, v, o, lse)


def odd_bwd(dout, saved, p, tabq, tabk, dm, comm):
    h, u, hn, cqn, ckvn, q, k, v, o, lse = saved
    g = {}
    do = matmul_nt(dout, p["w_out"], dm.n, 512, MXU_DT, "od_do")
    g["w_out"], got = matmul_tn(o, dout, dm.n, "od_dw_out", carry=comm.advance(None))
    (dq, dk, dv), got = attn_bwd(q, k, v, o, do, lse, dm.nb, dm.tp, "od_attn_bwd", carry=comm.advance(got))
    dqa, dkv, dkr = rope_bwd(dq, dk, dv, tabq, tabk, dm.tp, dm.n, "od_rope_bwd")
    g["w_uq_p"], got = matmul_tn(cqn, dqa, dm.n, "od_dw_uq", carry=comm.advance(got))
    comm.advance(got)
    g["w_ukv_p"], _ = matmul_tn(ckvn, dkv, dm.n, "od_dw_ukv")
    dcq, g["q_norm"] = matmul_nt_normbwd(dqa, p["w_uq_p"], u, 0, p["q_norm"], None, dm.n, 512, MXU_DT, "od_dcq")
    dckv, g["kv_norm"] = matmul_nt_normbwd(dkv, p["w_ukv_p"], u, ODD_CKV_COL, p["kv_norm"], None, dm.n, 512, MXU_DT,
                                           "od_dckv")
    du = jnp.concatenate([dcq, jnp.zeros((dm.rows, 128), MXU_DT), dckv, dkr], axis=1)
    g["w_in_p"], _ = matmul_tn(hn, du, dm.n, "od_dw_in")
    dx, g["norm"] = matmul_nt_normbwd(du, p["w_in_p"], h, 0, p["norm"], dout, dm.n, ODD_PAD, F32, "od_dx")
    return dx, g


def ffn_fwd(h, p, dm, comm):
    (up, hn), got = norm_matmul(h, 0, D_MODEL, p["norm"], p["w_up"], dm.n, D_FF // 2, MXU_DT, "ffn_up",
                                carry=comm.advance(None))
    comm.advance(got)
    u, y = ffn_mid_fwd(up, p["cw"], p["cb"], dm.nb, dm.tp, dm.n, "ffn_mid")
    out = matmul_res(y[None], p["w_down"][None], h, dm.n, 512, "ffn_down")
    return out, (h, up, hn, u, y)


def ffn_bwd(dout, saved, p, dm, comm):
    h, up, hn, u, y = saved
    g = {}
    dy = matmul_nt(dout, p["w_down"], dm.n, D_FF // 2, MXU_DT, "ffn_dy")
    g["w_down"], got = matmul_tn(y, dout, dm.n, "ffn_dw_down", carry=comm.advance(None))
    (dup, g["cw"], g["cb"]), got = ffn_mid_bwd(dy, u, up, p["cw"], dm.nb, dm.tp, dm.n, "ffn_mid_bwd",
                                               carry=comm.advance(got))
    g["w_up"], got = matmul_tn(hn, dup, dm.n, "ffn_dw_up", carry=comm.advance(got), col_shards=4)
    comm.advance(got)
    dx, g["norm"] = matmul_nt_normbwd(dup, p["w_up"], h, 0, p["norm"], dout, dm.n, D_FF // 2, F32, "ffn_dx")
    return dx, g


def _row(v):
    return v.reshape(1, -1)


def even_params(wf, j):
    return dict(norm=_row(wf["ev_norm"][j]), w_in=wf["ev_w_in"], conv_a=wf["ev_conv_a"][j], conv_b=wf["ev_conv_b"][j],
                conv_b_bias=_row(wf["ev_conv_b_bias"][j]), rw=block_diag(wf["ev_gate_r_w"][j]).astype(MXU_DT),
                r_b=_row(wf["ev_gate_r_b"][j]), iw=block_diag(wf["ev_gate_i_w"][j]).astype(MXU_DT),
                i_b=_row(wf["ev_gate_i_b"][j]), lam=_row(wf["ev_lru_lambda"][j]), w_out=wf["ev_w_out"])


def odd_params(wf, j):
    wkv = ukv_padded(wf["od_w_ukv"])
    return dict(norm=_row(wf["od_norm"][j]), w_in_p=odd_w_in_padded(wf["od_w_in"]), q_norm=_row(wf["od_q_norm"][j]),
                kv_norm=_row(wf["od_kv_norm"][j]), w_uq_p=uq_padded(wf["od_w_uq"]), w_ukv_p=wkv,
                w_uk_p=wkv[:, :MLA_HEADS * HP], w_uv_p=wkv[:, MLA_HEADS * HP:], w_out=wf["od_w_out"])


def ffn_params(wf, layer):
    return dict(norm=_row(wf["ffn_norm"][layer]), w_up=wf["ffn_w_up"],
                cw=jnp.moveaxis(wf["ffn_conv_w"][layer].reshape(3, 2, D_FF), 1, 0),
                cb=wf["ffn_conv_b"][layer].reshape(2, 1, D_FF), w_down=wf["ffn_w_down"])


def even_grads(g):
    out = {"ev_" + k_: g[k_] for k_ in ("w_in", "conv_a", "conv_b", "w_out")}
    out.update({"ev_norm": g["norm"][0], "ev_conv_b_bias": g["conv_b_bias"][0], "ev_gate_r_w": g["r_w"],
                "ev_gate_r_b": g["r_b"][0], "ev_gate_i_w": g["i_w"], "ev_gate_i_b": g["i_b"][0],
                "ev_lru_lambda": g["lam"][0]})
    return out


def odd_grads(g):
    return {"od_norm": g["norm"][0], "od_q_norm": g["q_norm"][0], "od_kv_norm": g["kv_norm"][0],
            "od_w_in": odd_w_in_unpad(g["w_in_p"]), "od_w_uq": uq_unpad(g["w_uq_p"]),
            "od_w_ukv": ukv_unpad(g["w_ukv_p"]), "od_w_out": g["w_out"]}


def ffn_grads(g):
    return {"ffn_norm": g["norm"][0], "ffn_w_up": g["w_up"], "ffn_conv_w": jnp.moveaxis(g["cw"], 0, 1).reshape(3, 2 * D_FF),
            "ffn_conv_b": g["cb"].reshape(2 * D_FF), "ffn_w_down": g["w_down"]}


WEIGHTS = ["meta_tokens", "ev_norm", "ev_w_in", "ev_conv_a", "ev_conv_b", "ev_conv_b_bias", "ev_gate_r_w", "ev_gate_r_b",
           "ev_gate_i_w", "ev_gate_i_b", "ev_lru_lambda", "ev_w_out", "od_norm", "od_w_in", "od_q_norm", "od_kv_norm",
           "od_w_uq", "od_w_ukv", "od_w_out", "ffn_norm", "ffn_w_up", "ffn_conv_w", "ffn_conv_b", "ffn_w_down",
           "final_norm"]
SHARD_AXIS = {"meta_tokens": 1, "ev_w_in": 2, "ev_conv_a": 2, "ev_conv_b": 2, "ev_w_out": 1, "od_norm": 1, "od_w_in": 1,
              "od_q_norm": 1, "od_kv_norm": 1, "od_w_uq": 2, "od_w_ukv": 2, "od_w_out": 1, "ffn_w_up": 2,
              "ffn_conv_w": 2, "ffn_w_down": 1}
MATMUL_WEIGHTS = ["ev_w_in", "ev_w_out", "od_w_in", "od_w_uq", "od_w_ukv", "od_w_out", "ffn_w_up", "ffn_w_down"]


LAYER_ORDER = [("ev", 0), ("ffn", 0), ("od", 0), ("ffn", 1), ("ev", 1), ("ffn", 2), ("od", 1), ("ffn", 3)]
LAYER_MATMUL = {"ev": ["ev_w_in", "ev_w_out"], "od": ["od_w_in", "od_w_uq", "od_w_ukv", "od_w_out"],
                "ffn": ["ffn_w_up", "ffn_w_down"]}
LAYER_SHARDED = {"ev": ["ev_w_in", "ev_conv_a", "ev_conv_b", "ev_w_out"],
                 "od": ["od_norm", "od_w_in", "od_q_norm", "od_kv_norm", "od_w_uq", "od_w_ukv", "od_w_out"],
                 "ffn": ["ffn_w_up", "ffn_conv_w", "ffn_w_down"]}
STACKED_SHARDS = "ffn_w_up"


def gather_at_entry(w, names, first, name):
    buf, offs = pack_rows([w[n] for n in names], 0, 32)
    halves = buf.reshape(2, buf.shape[0] // 2, LANE)
    outs = run_stage(stage_gather_chips([halves] + first.halves), name)
    first.step, first.got = 2, outs[1:]
    got = own_block(outs[0], halves).reshape(4, buf.shape[0], LANE)
    return {n: unshard(unpack_rows(got, off, w[n].shape), SHARD_AXIS[n]) for n, off in zip(names, offs)}


def _halves(a):
    return a.reshape(2, a.shape[0] // 2, a.shape[1])


class GatherComm:
    def __init__(self, w, kind, idx):
        self.names = LAYER_MATMUL[kind]
        self.halves = [_halves(w[n][idx].astype(MXU_DT)) for n in self.names]
        self.stage = stage_gather_chips(self.halves)
        self.step, self.got = 0, None

    def advance(self, carried):
        self.step += 1
        if self.step == 1:
            return self.stage
        if self.step == 2:
            self.got = carried
        return None

    def run_alone(self, name):
        self.advance(run_stage(self.advance(None), name))

    def weights(self):
        out = {}
        for n, got, own in zip(self.names, self.got, self.halves):
            stacked = own_block(got, own).reshape(4, 2 * own.shape[1], own.shape[2])
            out[n] = stacked if n == STACKED_SHARDS else unshard(stacked, SHARD_AXIS[n] - 1)
        return out


class ReduceComm:
    def __init__(self, grads, axes, c_idx, tag, tail=None):
        shards = {n: grads[n] if n == STACKED_SHARDS else to_shards(grads[n], axes[n]) for n in grads}
        self.big = [n for n in grads if n in MATMUL_WEIGHTS]
        self.small = [n for n in grads if n not in MATMUL_WEIGHTS]
        self.shapes = {n: shards[n].shape[1:] for n in grads}
        arrays = [shards[n].reshape(4, 2, shards[n].shape[1] // 2, shards[n].shape[2]) for n in self.big]
        gs, self.offs = pack_rows([shards[n] for n in self.small], 1, 16)
        self.rs = gs.shape[1] // 2
        parts = [gs.reshape(4, 2, self.rs, LANE)]
        self.rr = 0
        if tail is not None:
            self.rr = tail.shape[0] // 8
            parts.append(tail.reshape(4, 2, self.rr, LANE))
        arrays.append(jnp.concatenate(parts, axis=2) if len(parts) > 1 else parts[0])
        self.arrays, self.c_idx, self.tag, self.step = arrays, c_idx, tag, 0
        self.part = self.mine = self.theirs = None

    def advance(self, carried):
        self.step += 1
        if self.step == 1:
            return stage_pair_exchange(self.arrays)
        if self.step == 2:
            self.part = [pair_sum(g, land, self.c_idx, "grad_pair_sum_%s_%d" % (self.tag, i))
                         for i, (g, land) in enumerate(zip(self.arrays, carried))]
            return stage_chip_scatter(self.part)
        if self.step == 3:
            me_idx = chip_index().astype(jnp.int32).reshape(1)
            self.mine = [chip_sum(land, part, me_idx, "grad_chip_sum_%s_%d" % (self.tag, i))
                         for i, (land, part) in enumerate(zip(carried, self.part))]
            return stage_pair_gather(self.mine)
        if self.step == 4:
            self.theirs = carried
        return None

    def run_alone(self, name):
        stage = self.advance(None)
        while stage is not None:
            stage = self.advance(run_stage(stage, name + "_%d" % self.step))

    def results(self):
        south = self.c_idx[0] == 0
        boths = [jnp.stack([jnp.where(south, m, t), jnp.where(south, t, m)], axis=0)
                 for m, t in zip(self.mine, self.theirs)]
        out = {n: b.reshape(self.shapes[n]) for n, b in zip(self.big, boths)}
        packed = boths[-1]
        flat = packed[:, :self.rs].reshape(2 * self.rs, LANE)
        out.update({n: unpack_rows(flat, off, self.shapes[n]) for n, off in zip(self.small, self.offs)})
        return out, packed[:, self.rs:self.rs + self.rr]


def kernel(x, meta_tokens, ev_norm, ev_w_in, ev_conv_a, ev_conv_b, ev_conv_b_bias, ev_gate_r_w, ev_gate_r_b, ev_gate_i_w, ev_gate_i_b, ev_lru_lambda, ev_w_out, od_norm, od_w_in, od_q_norm, od_kv_norm, od_w_uq, od_w_ukv, od_w_out, ffn_norm, ffn_w_up, ffn_conv_w, ffn_conv_b, ffn_w_down, final_norm, loss_target, m_meta_tokens, m_ev_norm, m_ev_w_in, m_ev_conv_a, m_ev_conv_b, m_ev_conv_b_bias, m_ev_gate_r_w, m_ev_gate_r_b, m_ev_gate_i_w, m_ev_gate_i_b, m_ev_lru_lambda, m_ev_w_out, m_od_norm, m_od_w_in, m_od_q_norm, m_od_kv_norm, m_od_w_uq, m_od_w_ukv, m_od_w_out, m_ffn_norm, m_ffn_w_up, m_ffn_conv_w, m_ffn_conv_b, m_ffn_w_down, m_final_norm, v_meta_tokens, v_ev_norm, v_ev_w_in, v_ev_conv_a, v_ev_conv_b, v_ev_conv_b_bias, v_ev_gate_r_w, v_ev_gate_r_b, v_ev_gate_i_w, v_ev_gate_i_b, v_ev_lru_lambda, v_ev_w_out, v_od_norm, v_od_w_in, v_od_q_norm, v_od_kv_norm, v_od_w_uq, v_od_w_ukv, v_od_w_out, v_ffn_norm, v_ffn_w_up, v_ffn_conv_w, v_ffn_conv_b, v_ffn_w_down, v_final_norm):
    given = dict(locals())
    w = {n: given[n] for n in WEIGHTS}
    nb, seq, _ = x.shape
    dm = Dims(nb, seq)
    n_layers = len(LAYER_ORDER)

    wf = {n: w[n] for n in WEIGHTS if n not in SHARD_AXIS}
    gathers = [GatherComm(w, kind, idx) for kind, idx in LAYER_ORDER]
    wf.update(gather_at_entry(w, [n for n in SHARD_AXIS if n not in MATMUL_WEIGHTS], gathers[0], "gather_at_entry"))

    tail = dm.tp - dm.t_real
    meta = jnp.broadcast_to(wf["meta_tokens"][None], (nb, N_META, D_MODEL))
    h = jnp.concatenate([meta, x, jnp.zeros((nb, tail, D_MODEL), F32)], axis=1).reshape(dm.rows, D_MODEL)
    tgt = jnp.pad(loss_target, ((0, 0), (N_META, tail), (0, 0))).reshape(dm.rows, D_MODEL)
    tabq, tabk = rope_tables(dm.tp)

    params, saved = [], []
    for i, (kind, idx) in enumerate(LAYER_ORDER):
        wl = dict(wf)
        wl.update(gathers[i].weights())
        comm = gathers[i + 1] if i + 1 < n_layers else NoComm()
        if kind == "ev":
            p = even_params(wl, idx)
            h, sv = even_fwd(h, p, dm, comm)
        elif kind == "od":
            p = odd_params(wl, idx)
            h, sv = odd_fwd(h, p, tabq, tabk, dm, comm)
        else:
            p = ffn_params(wl, idx)
            h, sv = ffn_fwd(h, p, dm, comm)
        params.append(p)
        saved.append(sv)

    dh, loss, dfinal = loss_head(h, tgt, _row(wf["final_norm"]), dm.tp, dm.t_real, dm.n, "loss_head")
    loss = lax.psum(loss[0, 0], ("x", "y", "c"))

    c_idx = lax.axis_index("c").astype(jnp.int32).reshape(1)
    layer_grads = {n: {} for n in WEIGHTS}
    pending, reduces = NoComm(), []
    for i in reversed(range(n_layers)):
        kind, idx = LAYER_ORDER[i]
        if kind == "ev":
            dh, g = even_bwd(dh, saved[i], params[i], dm, pending)
            g = even_grads(g)
        elif kind == "od":
            dh, g = odd_bwd(dh, saved[i], params[i], tabq, tabk, dm, pending)
            g = odd_grads(g)
        else:
            dh, g = ffn_bwd(dh, saved[i], params[i], dm, pending)
            g = ffn_grads(g)
        for n in g:
            if n not in SHARD_AXIS:
                layer_grads[n][idx] = g[n]
        if i > 0:
            pending = ReduceComm({n: g[n] for n in LAYER_SHARDED[kind]}, {n: SHARD_AXIS[n] - 1 for n in SHARD_AXIS},
                                 c_idx, "%s%d" % (kind, idx))
            reduces.append((pending, idx))
    dh3 = dh.reshape(nb, dm.tp, D_MODEL)
    grad_x = dh3[:, N_META:dm.t_real]

    repl = [n for n in WEIGHTS if n not in SHARD_AXIS]
    layer_grads["final_norm"] = {0: dfinal[0]}
    repl_full = {n: (layer_grads[n][0] if n == "final_norm" else
                     jnp.stack([layer_grads[n][j] for j in range(w[n].shape[0])], axis=0)) for n in repl}
    tail_buf, tail_offs = pack_rows([repl_full[n] for n in repl], 0, 64)
    first = {n: g[n] for n in LAYER_SHARDED["ev"]}
    first["meta_tokens"] = jnp.sum(dh3[:, :N_META], axis=0)
    axes = {n: SHARD_AXIS[n] - 1 for n in SHARD_AXIS}
    axes["meta_tokens"] = SHARD_AXIS["meta_tokens"]
    last = ReduceComm(first, axes, c_idx, "first_layer", tail=tail_buf)
    last.run_alone("grad_first_layer")
    reduces.append((last, 0))

    red = {}
    for comm, idx in reduces:
        got, tail_piece = comm.results()
        for n, v_ in got.items():
            if n == "meta_tokens":
                red[n] = v_
            else:
                layer_grads[n][idx] = v_
    tails = own_block(run_stage(stage_gather_chips([tail_piece]), "grad_gather_replicated")[0], tail_piece)
    tails = tails.reshape(tail_buf.shape[0], LANE)
    for n, off in zip(repl, tail_offs):
        red[n] = unpack_rows(tails, off, w[n].shape)
    for n in SHARD_AXIS:
        if n != "meta_tokens":
            red[n] = jnp.stack([layer_grads[n][j] for j in range(w[n].shape[0])], axis=0)

    outs = [adamw(red[n], w[n], given["m_" + n], given["v_" + n], "adamw_" + n)[0] for n in WEIGHTS]
    return (loss, grad_x, *[o[0] for o in outs], *[o[1] for o in outs], *[o[2] for o in outs], *[o[3] for o in outs])
```

```python
import math

import jax
import jax.numpy as jnp
from jax import lax
from jax.experimental import pallas as pl
from jax.experimental.pallas import tpu as pltpu

F32 = jnp.float32
MXU_DT = jnp.bfloat16
S = jax.ShapeDtypeStruct
MESH = pl.DeviceIdType.MESH

EPS = 1e-6
D_MODEL = 1024
N_META = 16
DEPTH = 4
CONV_W = 512
LRU_W = 512
LRU_HEADS = 8
LRU_C = 8.0
EVEN_IN = 2560
MLA_HEADS = 16
QK_NOPE = 64
QK_ROPE = 32
QK_HEAD = 96
V_HEAD = 64
Q_LORA = 384
KV_LORA = 256
ROPE_BASE = 10000.0
D_FF = 2816
ODD_PAD = 896
ODD_CKV_COL = 2
ODD_KR_COL = 6
HP = 128
ATT_BLK = 384
Q_PRESCALE = QK_HEAD ** -0.5 * math.log2(math.e)
FFN_CT = 256
EV_CT = 256
STRIP_ROWS = 176
LANE = 128
SUBLANE = 8
VMEM_LIMIT_MB = 52

ADAM_LR = 0.001
ADAM_B1 = 0.9
ADAM_B2 = 0.999
ADAM_EPS = 1e-08
ADAM_WD = 0.01
ADAM_STEP = 10

NT_DIMS = (((1,), (1,)), ((), ()))
TN_DIMS = (((0,), (0,)), ((), ()))


def _cp(sem):
    return pltpu.CompilerParams(dimension_semantics=sem, vmem_limit_bytes=VMEM_LIMIT_MB << 20)


def _div_tile(n, cap, mult):
    if n <= cap:
        return n
    best = None
    for t in range(mult, cap + 1, mult):
        if n % t == 0:
            best = t
    assert best is not None, (n, cap, mult)
    return best


def _round_up(n, m):
    return -(-n // m) * m


def mat_cols(arr):
    return arr.shape[1] if arr.ndim == 2 else arr.shape[0] * arr.shape[2]


def mat_width(arr):
    return arr.shape[-1]


def mat_spec(arr, tm, tw, rc):
    if arr.ndim == 2:
        return pl.BlockSpec((tm, tw), lambda *g: rc(*g))
    per = arr.shape[2] // tw
    assert arr.shape[2] % tw == 0

    def imap(*g):
        r, c = rc(*g)
        return (c // per, r, c % per)

    return pl.BlockSpec((None, tm, tw), imap)


HBM_SPEC = pl.BlockSpec(memory_space=pltpu.HBM)


class Stage:
    def __init__(self, inputs, out_shapes, sems, start, finish):
        self.inputs, self.out_shapes, self.sems, self.start, self.finish = inputs, out_shapes, sems, start, finish


def run_stage(stage, name):
    n_in, n_out = len(stage.inputs), len(stage.out_shapes)

    def body(*refs):
        ins, outs, sems = refs[:n_in], refs[n_in:n_in + n_out], refs[n_in + n_out:]
        stage.start(ins, outs, sems)
        stage.finish(ins, outs, sems)

    return pl.pallas_call(body, out_shape=list(stage.out_shapes), in_specs=[HBM_SPEC] * n_in,
                          out_specs=[HBM_SPEC] * n_out, scratch_shapes=list(stage.sems), name=name)(*stage.inputs)


def _call(body, ops, carry, *, grid, in_specs, out_specs, out_shape, scratch_shapes, sem, name):
    if carry is None:
        outs = pl.pallas_call(body, grid=grid, in_specs=in_specs, out_specs=out_specs, out_shape=out_shape,
                              scratch_shapes=scratch_shapes, compiler_params=_cp(sem), name=name)(*ops)
        return outs, None
    multi = isinstance(out_shape, (list, tuple))
    shapes = list(out_shape) if multi else [out_shape]
    ospecs = list(out_specs) if multi else [out_specs]
    n_in, n_out, n_sc = len(ops), len(shapes), len(scratch_shapes)
    c_in, c_out = len(carry.inputs), len(carry.out_shapes)

    def wrapped(*refs):
        ins, cin = refs[:n_in], refs[n_in:n_in + c_in]
        o0 = n_in + c_in
        outs, cout = refs[o0:o0 + n_out], refs[o0 + n_out:o0 + n_out + c_out]
        s0 = o0 + n_out + c_out
        scs, csems = refs[s0:s0 + n_sc], refs[s0 + n_sc:]
        first = pl.program_id(0) == 0
        last = pl.program_id(0) == grid[0] - 1
        for d in range(1, len(grid)):
            first = first & (pl.program_id(d) == 0)
            last = last & (pl.program_id(d) == grid[d] - 1)

        @pl.when(first)
        def _():
            carry.start(cin, cout, csems)

        body(*ins, *outs, *scs)

        @pl.when(last)
        def _():
            carry.finish(cin, cout, csems)

    res = pl.pallas_call(
        wrapped, grid=grid, in_specs=list(in_specs) + [HBM_SPEC] * c_in, out_specs=ospecs + [HBM_SPEC] * c_out,
        out_shape=shapes + list(carry.out_shapes), scratch_shapes=list(scratch_shapes) + list(carry.sems),
        compiler_params=_cp(("arbitrary",) * len(grid)), name=name)(*ops, *carry.inputs)
    main = res[:n_out]
    return (list(main) if multi else main[0]), list(res[n_out:])


def norm_matmul(x, xcol, kdim, gain, w, tm, tn, out_dtype, name, epi=None, epi_ops=(), epi_specs=(), carry=None):
    rows, n = x.shape[0], mat_cols(w) if w.ndim == 3 else w.shape[1]
    n_epi = len(epi_ops)
    w_spec = (pl.BlockSpec((kdim, tn), lambda i, j: (0, j)) if w.ndim == 2 else
              pl.BlockSpec((None, kdim, tn), lambda i, j: (j // (w.shape[2] // tn), 0, j % (w.shape[2] // tn))))

    def body(x_ref, g_ref, w_ref, *rest):
        epi_refs = rest[:n_epi]
        out_ref, xn_ref, xn_sc = rest[n_epi:]

        @pl.when(pl.program_id(1) == 0)
        def _():
            xv = x_ref[...]
            y = xv * lax.rsqrt(jnp.mean(xv * xv, axis=-1, keepdims=True) + EPS)
            xn = (y * g_ref[...]).astype(MXU_DT)
            xn_sc[...] = xn
            xn_ref[...] = xn

        acc = jnp.dot(xn_sc[...], w_ref[...], preferred_element_type=F32)
        if epi is not None:
            acc = epi(acc, *[r[...] for r in epi_refs])
        out_ref[...] = acc.astype(out_dtype)

    return _call(
        body, (x, gain, w, *epi_ops), carry, grid=(rows // tm, n // tn),
        in_specs=[pl.BlockSpec((tm, kdim), lambda i, j: (i, xcol)), pl.BlockSpec((1, kdim), lambda i, j: (0, 0)),
                  w_spec, *epi_specs],
        out_specs=[pl.BlockSpec((tm, tn), lambda i, j: (i, j)), pl.BlockSpec((tm, kdim), lambda i, j: (i, 0))],
        out_shape=[S((rows, n), out_dtype), S((rows, kdim), MXU_DT)],
        scratch_shapes=[pltpu.VMEM((tm, kdim), MXU_DT)], sem=("parallel", "arbitrary"), name=name)


def matmul_res(a, w, res, tm, tn, name):
    grp, rows, k = a.shape
    n = w.shape[2]

    def body(a_ref, w_ref, r_ref, o_ref):
        acc = r_ref[...]
        for g in range(grp):
            acc = acc + jnp.dot(a_ref[g], w_ref[g], preferred_element_type=F32)
        o_ref[...] = acc

    return pl.pallas_call(
        body, grid=(rows // tm, n // tn),
        in_specs=[pl.BlockSpec((grp, tm, k), lambda i, j: (0, i, 0)), pl.BlockSpec((grp, k, tn), lambda i, j: (0, 0, j)),
                  pl.BlockSpec((tm, tn), lambda i, j: (i, j))],
        out_specs=pl.BlockSpec((tm, tn), lambda i, j: (i, j)),
        out_shape=S((rows, n), F32), compiler_params=_cp(("parallel", "parallel")), name=name)(a, w, res)


def matmul_nt(a, w, tm, tn, out_dtype, name):
    rows, k = a.shape
    n = w.shape[0]

    def body(a_ref, w_ref, o_ref):
        o_ref[...] = lax.dot_general(a_ref[...].astype(MXU_DT), w_ref[...], NT_DIMS,
                                     preferred_element_type=F32).astype(out_dtype)

    return pl.pallas_call(
        body, grid=(rows // tm, n // tn),
        in_specs=[pl.BlockSpec((tm, k), lambda i, j: (i, 0)), pl.BlockSpec((tn, k), lambda i, j: (j, 0))],
        out_specs=pl.BlockSpec((tm, tn), lambda i, j: (i, j)),
        out_shape=S((rows, n), out_dtype), compiler_params=_cp(("parallel", "parallel")), name=name)(a, w)


def matmul_nt_normbwd(du, w, x, xcol, gain, res, tm, tk, out_dtype, name):
    rows, kc = du.shape[-2], mat_cols(du)
    dn = w.shape[-2]
    nk = kc // tk
    has_res = res is not None
    w_spec = (pl.BlockSpec((dn, tk), lambda i, k: (0, k)) if w.ndim == 2 else
              pl.BlockSpec((None, dn, tk), lambda i, k: (k // (w.shape[2] // tk), 0, k % (w.shape[2] // tk))))

    def body(du_ref, w_ref, x_ref, g_ref, *rest):
        if has_res:
            res_ref, dx_ref, dg_ref, acc = rest
        else:
            dx_ref, dg_ref, acc = rest
        i, k = pl.program_id(0), pl.program_id(1)

        @pl.when(k == 0)
        def _():
            acc[...] = jnp.zeros_like(acc)

        @pl.when((i == 0) & (k == 0))
        def _():
            dg_ref[...] = jnp.zeros_like(dg_ref)

        acc[...] += lax.dot_general(du_ref[...], w_ref[...], NT_DIMS, preferred_element_type=F32)

        @pl.when(k == nk - 1)
        def _():
            dhn = acc[...]
            xv = x_ref[...]
            rstd = lax.rsqrt(jnp.mean(xv * xv, axis=-1, keepdims=True) + EPS)
            xhat = xv * rstd
            dg_ref[...] += jnp.sum(dhn * xhat, axis=0, keepdims=True)
            dxh = dhn * g_ref[...]
            dx = rstd * (dxh - xhat * jnp.mean(dxh * xhat, axis=-1, keepdims=True))
            if has_res:
                dx = dx + res_ref[...]
            dx_ref[...] = dx.astype(out_dtype)

    in_specs = [mat_spec(du, tm, tk, lambda i, k: (i, k)), w_spec,
                pl.BlockSpec((tm, dn), lambda i, k: (i, xcol)), pl.BlockSpec((1, dn), lambda i, k: (0, 0))]
    ops = [du, w, x, gain]
    if has_res:
        in_specs.append(pl.BlockSpec((tm, dn), lambda i, k: (i, 0)))
        ops.append(res)
    return pl.pallas_call(
        body, grid=(rows // tm, nk), in_specs=in_specs,
        out_specs=[pl.BlockSpec((tm, dn), lambda i, k: (i, 0)), pl.BlockSpec((1, dn), lambda i, k: (0, 0))],
        out_shape=[S((rows, dn), out_dtype), S((1, dn), F32)],
        scratch_shapes=[pltpu.VMEM((tm, dn), F32)],
        compiler_params=_cp(("arbitrary", "arbitrary")), name=name)(*ops)


def matmul_tn(a, b, tr, name, carry=None, col_shards=1):
    rows, ka, nb = a.shape[-2], mat_cols(a), mat_cols(b)
    ta = _div_tile(mat_width(a), 1536, LANE)
    tb = _div_tile(mat_width(b), 1536 if ta <= 1024 else 1024, LANE)
    nr = rows // tr
    if col_shards == 1:
        out_spec, out_shape = pl.BlockSpec((ta, tb), lambda i, j, r: (i, j)), S((ka, nb), F32)
    else:
        per = nb // col_shards // tb
        assert per * tb * col_shards == nb
        out_spec = pl.BlockSpec((None, ta, tb), lambda i, j, r: (j // per, i, j % per))
        out_shape = S((col_shards, ka, nb // col_shards), F32)

    def body(a_ref, b_ref, o_ref, acc):
        r = pl.program_id(2)

        @pl.when(r == 0)
        def _():
            acc[...] = jnp.zeros_like(acc)

        acc[...] += lax.dot_general(a_ref[...].astype(MXU_DT), b_ref[...].astype(MXU_DT), TN_DIMS,
                                    preferred_element_type=F32)

        @pl.when(r == nr - 1)
        def _():
            o_ref[...] = acc[...]

    return _call(
        body, (a, b), carry, grid=(ka // ta, nb // tb, nr),
        in_specs=[mat_spec(a, tr, ta, lambda i, j, r: (r, i)), mat_spec(b, tr, tb, lambda i, j, r: (r, j))],
        out_specs=out_spec, out_shape=out_shape, scratch_shapes=[pltpu.VMEM((ta, tb), F32)],
        sem=("parallel", "parallel", "arbitrary"), name=name)


def _sigmoid(x):
    return 1.0 / (1.0 + jnp.exp(-x))


def _log1p(e):
    return jnp.where(e < 1e-3, e * (1.0 - e * (0.5 - e * (1.0 / 3.0 - 0.25 * e))), jnp.log(1.0 + e))


def _softplus(x):
    return jnp.maximum(x, 0.0) + _log1p(jnp.exp(-jnp.abs(x)))


def _expm1(x):
    series = x * (1.0 + x * (0.5 + x * (1.0 / 6.0 + x * (1.0 / 24.0 + x * (1.0 / 120.0)))))
    return jnp.where(jnp.abs(x) < 0.1, series, jnp.exp(x) - 1.0)


_GELU_K = math.sqrt(2.0 / math.pi)
_GELU_C = 0.044715


def _gelu_and_grad(x):
    th = jnp.tanh(_GELU_K * (x + _GELU_C * x * x * x))
    g = 0.5 * x * (1.0 + th)
    dg = 0.5 * (1.0 + th) + 0.5 * x * (1.0 - th * th) * _GELU_K * (1.0 + 3.0 * _GELU_C * x * x)
    return g, dg


def _row_iota(shape):
    return lax.broadcasted_iota(jnp.int32, shape, 0)


def _scan_chunk_fwd(a_sc, u_sc, out_ref, hcar, n, width):
    rowi = _row_iota((SUBLANE, width))

    def step(c, hprev):
        r0 = pl.multiple_of(c * SUBLANE, SUBLANE)
        a = a_sc[pl.ds(r0, SUBLANE), :]
        u = u_sc[pl.ds(r0, SUBLANE), :]
        for d in (1, 2, 4):
            a_s = jnp.where(rowi >= d, pltpu.roll(a, d, axis=0), 1.0)
            u_s = jnp.where(rowi >= d, pltpu.roll(u, d, axis=0), 0.0)
            u = u + a * u_s
            a = a * a_s
        h = u + a * hprev
        out_ref[pl.ds(r0, SUBLANE), :] = h
        return jnp.broadcast_to(h[SUBLANE - 1:SUBLANE, :], (SUBLANE, width))

    hcar[...] = lax.fori_loop(0, n // SUBLANE, step, hcar[...], unroll=4)


def _scan_chunk_bwd(b_sc, d_sc, out_ref, gcar, n, width):
    rowi = _row_iota((SUBLANE, width))
    nc = n // SUBLANE

    def step(c, gnext):
        r0 = pl.multiple_of((nc - 1 - c) * SUBLANE, SUBLANE)
        b = b_sc[pl.ds(r0, SUBLANE), :]
        d = d_sc[pl.ds(r0, SUBLANE), :]
        for s in (1, 2, 4):
            keep = rowi < SUBLANE - s
            b_s = jnp.where(keep, pltpu.roll(b, SUBLANE - s, axis=0), 1.0)
            d_s = jnp.where(keep, pltpu.roll(d, SUBLANE - s, axis=0), 0.0)
            d = d + b * d_s
            b = b * b_s
        g = d + b * gnext
        out_ref[pl.ds(r0, SUBLANE), :] = g
        return jnp.broadcast_to(g[0:1, :], (SUBLANE, width))

    gcar[...] = lax.fori_loop(0, nc, step, gcar[...], unroll=4)


def even_mid_fwd(u, conv_a, conv_b, conv_b_bias, rw, rb, iw, ib, lam, nb, tp, n, name, carry=None):
    rows = u.shape[0]
    w = EV_CT
    nj = CONV_W // w
    nt = tp // n
    h8 = SUBLANE

    def body(gb_r, gc_r, xa_r, xb_r, gate_r, ca_w, cb_w, cb_b, rw_r, rb_r, iw_r, ib_r, lam_r,
             y_o, ca_o, xc_o, a_o, hs_o, pext, xext, hcar, a_sc, u_sc):
        t = pl.program_id(2)

        @pl.when(t == 0)
        def _():
            pext[0:h8, :] = jnp.zeros((h8, w), F32)
            xext[0:h8, :] = jnp.zeros((h8, w), F32)
            hcar[...] = jnp.zeros_like(hcar)

        p = gc_r[...] * xa_r[...]
        pext[h8:h8 + n, :] = p
        wa = ca_w[...]
        ca = wa[2:3, :] * p + wa[1:2, :] * pext[h8 - 1:h8 - 1 + n, :] + wa[0:1, :] * pext[h8 - 2:h8 - 2 + n, :]
        ca_o[...] = ca
        y_o[0] = (gb_r[...] * ca).astype(MXU_DT)
        pext[0:h8, :] = pext[n:n + h8, :]

        xb = xb_r[...]
        xext[h8:h8 + n, :] = xb
        wb = cb_w[...]
        xc = (wb[3:4, :] * xb + wb[2:3, :] * xext[h8 - 1:h8 - 1 + n, :] + wb[1:2, :] * xext[h8 - 2:h8 - 2 + n, :]
              + wb[0:1, :] * xext[h8 - 3:h8 - 3 + n, :]) + cb_b[...]
        xc_o[...] = xc
        xext[0:h8, :] = xext[n:n + h8, :]

        xcm = xc.astype(MXU_DT)
        r = _sigmoid(jnp.dot(xcm, rw_r[...], preferred_element_type=F32) + rb_r[...])
        ig = _sigmoid(jnp.dot(xcm, iw_r[...], preferred_element_type=F32) + ib_r[...])
        log_a = (-LRU_C) * r * _softplus(-lam_r[...])
        a = jnp.exp(log_a)
        mult = jnp.sqrt(-_expm1(2.0 * log_a))
        a_sc[...] = a
        a_o[...] = a
        u_sc[...] = mult * (ig * xc)
        _scan_chunk_fwd(a_sc, u_sc, hs_o, hcar, n, w)
        gel, _ = _gelu_and_grad(gate_r[...])
        y_o[1] = (gel * hs_o[...]).astype(MXU_DT)

    def ublk(off):
        return pl.BlockSpec((n, w), lambda j, b, t: (b * nt + t, off + j))

    def pblk(r_):
        return pl.BlockSpec((r_, w), lambda j, b, t: (0, j))

    act = pl.BlockSpec((n, w), lambda j, b, t: (b * nt + t, j))
    mat = pl.BlockSpec((w, w), lambda j, b, t: (j, j))
    return _call(
        body, (u, u, u, u, u, conv_a, conv_b, conv_b_bias, rw, rb, iw, ib, lam), carry, grid=(nj, nb, nt),
        in_specs=[ublk(0), ublk(nj), ublk(2 * nj), ublk(3 * nj), ublk(4 * nj), pblk(3), pblk(4), pblk(1),
                  mat, pblk(1), mat, pblk(1), pblk(1)],
        out_specs=[pl.BlockSpec((2, n, w), lambda j, b, t: (0, b * nt + t, j)), act, act, act, act],
        out_shape=[S((2, rows, CONV_W), MXU_DT), S((rows, CONV_W), F32), S((rows, LRU_W), F32), S((rows, LRU_W), F32),
                   S((rows, LRU_W), F32)],
        scratch_shapes=[pltpu.VMEM((n + h8, w), F32), pltpu.VMEM((n + h8, w), F32), pltpu.VMEM((h8, w), F32),
                        pltpu.VMEM((n, w), F32), pltpu.VMEM((n, w), F32)],
        sem=("parallel", "parallel", "arbitrary"), name=name)


def even_mid_bwd(u, dycat, ca, xc, a_sv, hs, conv_a, conv_b, rw, rb, iw, ib, lam, nb, tp, n, name, carry=None):
    rows = u.shape[0]
    w = EV_CT
    nj = CONV_W // w
    nt = tp // n
    h8 = SUBLANE

    def body(gb_r, gc_r, xa_r, xb_r, gate_r, dya_r, dyb_r, ca_r, xc_r, a_r, hs_r, hsp_r,
             ca_w, cb_w, rw_r, rb_r, iw_r, ib_r, lam_r,
             du_o, dca_w, dcb_w, dcb_b, drw, drb, diw, dib, dlam,
             aext, hext, dext, eext, gcar, b_sc, d_sc, g_sc):
        b, t = pl.program_id(1), pl.program_id(2)

        @pl.when((b == 0) & (t == 0))
        def _():
            for ref in (dca_w, dcb_w, dcb_b, drw, drb, diw, dib, dlam):
                ref[...] = jnp.zeros_like(ref)

        @pl.when(t == 0)
        def _():
            aext[n:n + h8, :] = jnp.zeros((h8, w), F32)
            dext[n:n + h8, :] = jnp.zeros((h8, w), F32)
            eext[n:n + h8, :] = jnp.zeros((h8, w), F32)
            gcar[...] = jnp.zeros_like(gcar)

        xc_v = xc_r[...]
        xcm = xc_v.astype(MXU_DT)
        r = _sigmoid(jnp.dot(xcm, rw_r[...], preferred_element_type=F32) + rb_r[...])
        ig = _sigmoid(jnp.dot(xcm, iw_r[...], preferred_element_type=F32) + ib_r[...])
        lam_v = lam_r[...]
        sp = _softplus(-lam_v)
        log_a = (-LRU_C) * r * sp
        a = a_r[...]
        mult = jnp.sqrt(-_expm1(2.0 * log_a))
        hs_v = hs_r[...]
        gel, dgel = _gelu_and_grad(gate_r[...])
        dyb = dyb_r[...]
        du_o[4] = (dyb * hs_v * dgel).astype(MXU_DT)

        aext[0:n, :] = a
        b_sc[...] = aext[1:1 + n, :]
        d_sc[...] = dyb * gel
        _scan_chunk_bwd(b_sc, d_sc, g_sc, gcar, n, w)
        aext[n:n + h8, :] = aext[0:h8, :]
        g = g_sc[...]

        hext[0:h8, :] = jnp.where(t == nt - 1, 0.0, hsp_r[...])
        hext[h8:h8 + n, :] = hs_v
        da = g * hext[h8 - 1:h8 - 1 + n, :]
        dmult = g * (ig * xc_v)
        di = g * mult * xc_v
        dxc = g * mult * ig
        dlog_a = da * a - dmult * (a * a) / mult
        dr = dlog_a * ((-LRU_C) * sp)
        dsp = jnp.sum(dlog_a * ((-LRU_C) * r), axis=0, keepdims=True)
        dlam[...] += dsp * (-_sigmoid(-lam_v))
        dzr = dr * r * (1.0 - r)
        dzi = di * ig * (1.0 - ig)
        dzr_m = dzr.astype(MXU_DT)
        dzi_m = dzi.astype(MXU_DT)
        dxc = (dxc + lax.dot_general(dzr_m, rw_r[...], NT_DIMS, preferred_element_type=F32)
               + lax.dot_general(dzi_m, iw_r[...], NT_DIMS, preferred_element_type=F32))
        drw[...] += lax.dot_general(xcm, dzr_m, TN_DIMS, preferred_element_type=F32)
        diw[...] += lax.dot_general(xcm, dzi_m, TN_DIMS, preferred_element_type=F32)
        drb[...] += jnp.sum(dzr, axis=0, keepdims=True)
        dib[...] += jnp.sum(dzi, axis=0, keepdims=True)
        dcb_b[...] += jnp.sum(dxc, axis=0, keepdims=True)

        xb = xb_r[...]
        dext[0:n, :] = dxc
        wb = cb_w[...]
        d1, d2, d3 = dext[1:1 + n, :], dext[2:2 + n, :], dext[3:3 + n, :]
        du_o[3] = (wb[3:4, :] * dxc + wb[2:3, :] * d1 + wb[1:2, :] * d2 + wb[0:1, :] * d3).astype(MXU_DT)
        dcb_w[3:4, :] += jnp.sum(xb * dxc, axis=0, keepdims=True)
        dcb_w[2:3, :] += jnp.sum(xb * d1, axis=0, keepdims=True)
        dcb_w[1:2, :] += jnp.sum(xb * d2, axis=0, keepdims=True)
        dcb_w[0:1, :] += jnp.sum(xb * d3, axis=0, keepdims=True)
        dext[n:n + h8, :] = dext[0:h8, :]

        gb, gc, xa = gb_r[...], gc_r[...], xa_r[...]
        dya = dya_r[...]
        du_o[0] = (dya * ca_r[...]).astype(MXU_DT)
        dca = dya * gb
        eext[0:n, :] = dca
        wa = ca_w[...]
        e1, e2 = eext[1:1 + n, :], eext[2:2 + n, :]
        dp = wa[2:3, :] * dca + wa[1:2, :] * e1 + wa[0:1, :] * e2
        p = gc * xa
        dca_w[2:3, :] += jnp.sum(p * dca, axis=0, keepdims=True)
        dca_w[1:2, :] += jnp.sum(p * e1, axis=0, keepdims=True)
        dca_w[0:1, :] += jnp.sum(p * e2, axis=0, keepdims=True)
        eext[n:n + h8, :] = eext[0:h8, :]
        du_o[1] = (dp * xa).astype(MXU_DT)
        du_o[2] = (dp * gc).astype(MXU_DT)

    def rt(b, t):
        return b * nt + (nt - 1 - t)

    def ublk(off):
        return pl.BlockSpec((n, w), lambda j, b, t: (rt(b, t), off + j))

    def pblk(r_):
        return pl.BlockSpec((r_, w), lambda j, b, t: (0, j))

    act = pl.BlockSpec((n, w), lambda j, b, t: (rt(b, t), j))
    n8 = n // h8
    hsp = pl.BlockSpec((h8, w), lambda j, b, t: (jnp.maximum(rt(b, t) * n8 - 1, 0), j))
    mat = pl.BlockSpec((w, w), lambda j, b, t: (j, j))
    return _call(
        body, (u, u, u, u, u, dycat, dycat, ca, xc, a_sv, hs, hs, conv_a, conv_b, rw, rb, iw, ib, lam), carry,
        grid=(nj, nb, nt),
        in_specs=[ublk(0), ublk(nj), ublk(2 * nj), ublk(3 * nj), ublk(4 * nj), ublk(0), ublk(nj), act, act, act, act,
                  hsp, pblk(3), pblk(4), mat, pblk(1), mat, pblk(1), pblk(1)],
        out_specs=[pl.BlockSpec((5, n, w), lambda j, b, t: (0, rt(b, t), j)), pblk(3), pblk(4), pblk(1),
                   mat, pblk(1), mat, pblk(1), pblk(1)],
        out_shape=[S((5, rows, CONV_W), MXU_DT), S((3, CONV_W), F32), S((4, LRU_W), F32), S((1, LRU_W), F32),
                   S((LRU_W, LRU_W), F32), S((1, LRU_W), F32), S((LRU_W, LRU_W), F32), S((1, LRU_W), F32),
                   S((1, LRU_W), F32)],
        scratch_shapes=[pltpu.VMEM((n + h8, w), F32)] * 4 + [pltpu.VMEM((h8, w), F32)] + [pltpu.VMEM((n, w), F32)] * 3,
        sem=("arbitrary", "arbitrary", "arbitrary"), name=name)


def ffn_mid_fwd(up, cw, cb, nb, tp, n, name):
    rows = up.shape[0]
    w = FFN_CT
    nj = D_FF // w
    nt = tp // n
    h8 = SUBLANE

    sr = STRIP_ROWS
    assert n % sr == 0, (n, sr)

    def body(xa_r, xg_r, w_r, b_r, u_o, y_o, halo):
        t = pl.program_id(2)

        @pl.when(t == 0)
        def _():
            halo[...] = jnp.zeros_like(halo)

        wv = (w_r[0], w_r[1])
        bv = (b_r[0], b_r[1])

        def strip(s, carry):
            r0 = pl.multiple_of(s * sr, sr)
            us, new = [], []
            for g, x_r in enumerate((xa_r, xg_r)):
                x = x_r[pl.ds(r0, sr), :].astype(F32)
                win = jnp.concatenate([carry[g], x], axis=0)
                x1 = pltpu.roll(win, 1, axis=0)[h8:, :]
                x2 = pltpu.roll(win, 2, axis=0)[h8:, :]
                u = (wv[g][2:3, :] * x + wv[g][1:2, :] * x1 + wv[g][0:1, :] * x2) + bv[g]
                u_o[g, pl.ds(r0, sr), :] = u.astype(MXU_DT)
                us.append(u)
                new.append(x[sr - h8:, :])
            y_o[pl.ds(r0, sr), :] = (us[0] * _sigmoid(us[0]) * us[1]).astype(MXU_DT)
            return tuple(new)

        ha, hg = lax.fori_loop(0, n // sr, strip, (halo[0], halo[1]))
        halo[0] = ha
        halo[1] = hg

    def ublk(off):
        return pl.BlockSpec((n, w), lambda j, b, t: (b * nt + t, off + j))

    return pl.pallas_call(
        body, grid=(nj, nb, nt),
        in_specs=[ublk(0), ublk(nj), pl.BlockSpec((2, 3, w), lambda j, b, t: (0, 0, j)),
                  pl.BlockSpec((2, 1, w), lambda j, b, t: (0, 0, j))],
        out_specs=[pl.BlockSpec((2, n, w), lambda j, b, t: (0, b * nt + t, j)), ublk(0)],
        out_shape=[S((2, rows, D_FF), MXU_DT), S((rows, D_FF), MXU_DT)],
        scratch_shapes=[pltpu.VMEM((2, h8, w), F32)],
        compiler_params=_cp(("parallel", "parallel", "arbitrary")), name=name,
    )(up, up, cw, cb)


def ffn_mid_bwd(dy, u, up, cw, nb, tp, n, name, carry=None):
    rows = up.shape[0]
    w = FFN_CT
    nj = D_FF // w
    nt = tp // n
    h8 = SUBLANE

    sr = STRIP_ROWS
    assert n % sr == 0, (n, sr)
    ns = n // sr

    def fold(v):
        acc = v[0:h8, :]
        for k in range(1, sr // h8):
            acc = acc + v[k * h8:(k + 1) * h8, :]
        return acc

    def body(dy_r, u_r, xa_r, xg_r, w_r, dx_o, dw, db, halo):
        b, t = pl.program_id(1), pl.program_id(2)

        @pl.when((b == 0) & (t == 0))
        def _():
            dw[...] = jnp.zeros_like(dw)
            db[...] = jnp.zeros_like(db)

        @pl.when(t == 0)
        def _():
            halo[...] = jnp.zeros_like(halo)

        wv = (w_r[0], w_r[1])

        def strip(s, carry):
            halos, sums = carry
            r0 = pl.multiple_of((ns - 1 - s) * sr, sr)
            dyv = dy_r[pl.ds(r0, sr), :].astype(F32)
            ua = u_r[0, pl.ds(r0, sr), :].astype(F32)
            ug = u_r[1, pl.ds(r0, sr), :].astype(F32)
            sg = _sigmoid(ua)
            dus = (dyv * ug * (sg * (1.0 + ua * (1.0 - sg))), dyv * (ua * sg))
            new_halos, new_sums = [], []
            for g, x_r in enumerate((xa_r, xg_r)):
                du = dus[g]
                win = jnp.concatenate([du, halos[g]], axis=0)
                d1 = pltpu.roll(win, sr + h8 - 1, axis=0)[0:sr, :]
                d2 = pltpu.roll(win, sr + h8 - 2, axis=0)[0:sr, :]
                dx_o[g, pl.ds(r0, sr), :] = (wv[g][2:3, :] * du + wv[g][1:2, :] * d1 + wv[g][0:1, :] * d2).astype(MXU_DT)
                x = x_r[pl.ds(r0, sr), :].astype(F32)
                s2, s1, s0, sb = sums[g]
                new_sums.append((s2 + fold(x * du), s1 + fold(x * d1), s0 + fold(x * d2), sb + fold(du)))
                new_halos.append(du[0:h8, :])
            return tuple(new_halos), tuple(new_sums)

        z = jnp.zeros((h8, w), F32)
        halos, sums = lax.fori_loop(0, ns, strip, ((halo[0], halo[1]), ((z, z, z, z), (z, z, z, z))))
        halo[0] = halos[0]
        halo[1] = halos[1]
        for g in range(2):
            s2, s1, s0, sb = sums[g]
            dw[g, 2:3, :] += jnp.sum(s2, axis=0, keepdims=True)
            dw[g, 1:2, :] += jnp.sum(s1, axis=0, keepdims=True)
            dw[g, 0:1, :] += jnp.sum(s0, axis=0, keepdims=True)
            db[g] += jnp.sum(sb, axis=0, keepdims=True)

    def rt(b, t):
        return b * nt + (nt - 1 - t)

    def ublk(off):
        return pl.BlockSpec((n, w), lambda j, b, t: (rt(b, t), off + j))

    pair = pl.BlockSpec((2, n, w), lambda j, b, t: (0, rt(b, t), j))
    return _call(
        body, (dy, u, up, up, cw), carry, grid=(nj, nb, nt),
        in_specs=[ublk(0), pair, ublk(0), ublk(nj), pl.BlockSpec((2, 3, w), lambda j, b, t: (0, 0, j))],
        out_specs=[pair, pl.BlockSpec((2, 3, w), lambda j, b, t: (0, 0, j)),
                   pl.BlockSpec((2, 1, w), lambda j, b, t: (0, 0, j))],
        out_shape=[S((2, rows, D_FF), MXU_DT), S((2, 3, D_FF), F32), S((2, 1, D_FF), F32)],
        scratch_shapes=[pltpu.VMEM((2, h8, w), F32)],
        sem=("arbitrary", "arbitrary", "arbitrary"), name=name)


def _lane_mod(shape):
    return lax.broadcasted_iota(jnp.int32, shape, 1) & (HP - 1)


def _q_rope_epi(acc, tab):
    reps = acc.shape[1] // HP
    a = acc * jnp.tile(tab, (1, reps))
    lane = _lane_mod(a.shape)
    shifted = pltpu.roll(a, a.shape[1] - QK_ROPE, axis=1)
    return jnp.where(lane < QK_NOPE, a, jnp.where(lane < QK_HEAD, a + shifted, 0.0)) * Q_PRESCALE


def _k_rope_block(krblk, tabk):
    a = krblk * tabk
    lane = _lane_mod(a.shape)
    b = a + pltpu.roll(a, HP - QK_ROPE, axis=1)
    return jnp.where((lane >= QK_NOPE) & (lane < QK_HEAD), b, 0.0)


def _k_rope_epi(acc, krblk, tabk):
    reps = acc.shape[1] // HP
    return acc + jnp.tile(_k_rope_block(krblk, tabk), (1, reps))


def attn_fwd(q, k, v, nb, tp, name, carry=None):
    rows = q.shape[0]
    blk = ATT_BLK
    nq = tp // blk
    npair = MLA_HEADS // 2

    def body(q_r, k_r, v_r, o_r, lse_r):
        qi = pl.program_id(2)
        lane = lax.broadcasted_iota(jnp.int32, (blk, LANE), 1)
        even = lane < V_HEAD
        sum_lane = (V_HEAD, 0)
        rowi = lax.broadcasted_iota(jnp.int32, (blk, blk), 0)
        coli = lax.broadcasted_iota(jnp.int32, (blk, blk), 1)
        qs = [q_r[:, h * HP:(h + 1) * HP] for h in range(2)]

        def kv_block(k0, width, carry, visible):
            ms, accs = carry
            vblk = v_r[pl.ds(k0, width), :]
            one = jnp.ones_like(vblk)
            zero = jnp.zeros_like(vblk)
            vlane = lax.broadcasted_iota(jnp.int32, (width, LANE), 1)
            ss = [lax.dot_general(qs[h], k_r[pl.ds(k0, width), h * HP:(h + 1) * HP], NT_DIMS,
                                  preferred_element_type=F32) for h in range(2)]
            new_ms, new_accs = [], []
            for h in range(2):
                s = ss[h]
                if visible is not None:
                    s = jnp.where(visible, s, -jnp.inf)
                m_new = jnp.maximum(ms[h], jnp.max(s, axis=1, keepdims=True))
                alpha = jnp.exp2(ms[h] - m_new)
                p = jnp.exp2(s - m_new).astype(MXU_DT)
                mine = (vlane < V_HEAD) if h == 0 else (vlane >= V_HEAD)
                vh = jnp.where(mine, vblk, jnp.where(vlane == sum_lane[h], one, zero))
                new_accs.append(alpha * accs[h] + jnp.dot(p, vh, preferred_element_type=F32))
                new_ms.append(m_new)
            return tuple(new_ms), tuple(new_accs)

        neg = jnp.full((blk, 1), -jnp.inf, F32)
        zacc = jnp.zeros((blk, LANE), F32)
        carry = lax.fori_loop(0, qi // 2, lambda i, c: kv_block(pl.multiple_of(i * 2 * blk, blk), 2 * blk, c, None),
                              ((neg, neg), (zacc, zacc)))
        rowi2 = lax.broadcasted_iota(jnp.int32, (blk, 2 * blk), 0)
        coli2 = lax.broadcasted_iota(jnp.int32, (blk, 2 * blk), 1)
        ms, accs = lax.cond(
            qi % 2 == 1,
            lambda c: kv_block(pl.multiple_of((qi - 1) * blk, blk), 2 * blk, c, coli2 - blk <= rowi2),
            lambda c: kv_block(pl.multiple_of(qi * blk, blk), blk, c, coli <= rowi), carry)
        ls = [accs[h][:, sum_lane[h]:sum_lane[h] + 1] for h in range(2)]
        o_r[...] = jnp.where(even, accs[0] / ls[0], accs[1] / ls[1]).astype(MXU_DT)
        lse_r[...] = jnp.where(even, ms[0] + jnp.log2(ls[0]), ms[1] + jnp.log2(ls[1]))

    return _call(
        body, (q, k, v), carry, grid=(nb, npair, nq),
        in_specs=[pl.BlockSpec((blk, 2 * HP), lambda b, p, i: (b * nq + i, p)),
                  pl.BlockSpec((tp, 2 * HP), lambda b, p, i: (b, p)),
                  pl.BlockSpec((tp, LANE), lambda b, p, i: (b, p))],
        out_specs=[pl.BlockSpec((blk, LANE), lambda b, p, i: (b * nq + i, p)),
                   pl.BlockSpec((None, blk, LANE), lambda b, p, i: (p, b * nq + i, 0))],
        out_shape=[S((rows, MLA_HEADS * V_HEAD), MXU_DT), S((npair, rows, LANE), F32)], scratch_shapes=[],
        sem=("parallel", "parallel", "arbitrary"), name=name)


def attn_bwd(q, k, v, o, do, lse, nb, tp, name, carry=None):
    rows = q.shape[0]
    blk = ATT_BLK
    nq = tp // blk
    npair = MLA_HEADS // 2
    scale = QK_HEAD ** -0.5

    def body(q_r, k_r, v_r, o_r, do_r, lse_r, dq_o, dk_o, dv_o, dq_acc, delta_sc):
        kb = pl.program_id(2)
        even = lax.broadcasted_iota(jnp.int32, (blk, LANE), 1) < V_HEAD
        rowi = lax.broadcasted_iota(jnp.int32, (blk, blk), 0)
        coli = lax.broadcasted_iota(jnp.int32, (blk, blk), 1)

        @pl.when(kb == 0)
        def _():
            dq_acc[...] = jnp.zeros_like(dq_acc)

            def dstep(i, c):
                r0 = pl.multiple_of(i * blk, blk)
                prod = do_r[pl.ds(r0, blk), :].astype(F32) * o_r[pl.ds(r0, blk), :].astype(F32)
                de = jnp.sum(jnp.where(even, prod, 0.0), axis=1, keepdims=True)
                dd = jnp.sum(jnp.where(even, 0.0, prod), axis=1, keepdims=True)
                delta_sc[pl.ds(r0, blk), :] = jnp.where(even, de, dd)
                return c

            lax.fori_loop(0, nq, dstep, 0)

        vblk = v_r[...]
        ks = [k_r[:, h * HP:(h + 1) * HP] for h in range(2)]

        def q_block(r0, height, carry, visible):
            dk0, dk1, dv = carry
            dob = do_r[pl.ds(r0, height), :]
            lse_b = lse_r[pl.ds(r0, height), :]
            dl_b = delta_sc[pl.ds(r0, height), :]
            qlane = lax.broadcasted_iota(jnp.int32, (height, LANE), 1)
            dks = [dk0, dk1]
            qhs = [q_r[pl.ds(r0, height), h * HP:(h + 1) * HP] for h in range(2)]
            dohs = [jnp.where((qlane < V_HEAD) if h == 0 else (qlane >= V_HEAD), dob, jnp.zeros_like(dob))
                    for h in range(2)]
            ss = [lax.dot_general(qhs[h], ks[h], NT_DIMS, preferred_element_type=F32) for h in range(2)]
            dps = [lax.dot_general(dohs[h], vblk, NT_DIMS, preferred_element_type=F32) for h in range(2)]
            for h in range(2):
                lo = 0 if h == 0 else V_HEAD
                p = jnp.exp2(ss[h] - lse_b[:, lo:lo + 1])
                if visible is not None:
                    p = jnp.where(visible, p, 0.0)
                ds = (p * (dps[h] - dl_b[:, lo:lo + 1])).astype(MXU_DT)
                dv = dv + lax.dot_general(p.astype(MXU_DT), dohs[h], TN_DIMS, preferred_element_type=F32)
                dks[h] = dks[h] + lax.dot_general(ds, qhs[h], TN_DIMS, preferred_element_type=F32)
                dq_acc[pl.ds(r0, height), h * HP:(h + 1) * HP] += jnp.dot(ds, ks[h], preferred_element_type=F32)
            return dks[0], dks[1], dv

        z = jnp.zeros((blk, HP), F32)
        below = nq - 1 - kb
        odd = below % 2
        rowi2 = lax.broadcasted_iota(jnp.int32, (2 * blk, blk), 0)
        coli2 = lax.broadcasted_iota(jnp.int32, (2 * blk, blk), 1)
        first = pl.multiple_of(kb * blk, blk)
        carry = lax.cond(odd == 1, lambda c: q_block(first, 2 * blk, c, coli2 <= rowi2),
                         lambda c: q_block(first, blk, c, coli <= rowi), (z, z, jnp.zeros((blk, LANE), F32)))
        dk0, dk1, dv = lax.fori_loop(
            0, below // 2, lambda i, c: q_block(pl.multiple_of((kb + 1 + odd + 2 * i) * blk, blk), 2 * blk, c, None),
            carry)
        dk_o[:, 0:HP] = (dk0 * (scale / Q_PRESCALE)).astype(MXU_DT)
        dk_o[:, HP:2 * HP] = (dk1 * (scale / Q_PRESCALE)).astype(MXU_DT)
        dv_o[...] = dv.astype(MXU_DT)

        @pl.when(kb == nq - 1)
        def _():
            dq_o[...] = (dq_acc[...] * scale).astype(MXU_DT)

    seq_pair = pl.BlockSpec((tp, LANE), lambda b, p, kk: (b, p))
    return _call(
        body, (q, k, v, o, do, lse), carry, grid=(nb, npair, nq),
        in_specs=[pl.BlockSpec((tp, 2 * HP), lambda b, p, kk: (b, p)),
                  pl.BlockSpec((blk, 2 * HP), lambda b, p, kk: (b * nq + kk, p)),
                  pl.BlockSpec((blk, LANE), lambda b, p, kk: (b * nq + kk, p)),
                  seq_pair, seq_pair, pl.BlockSpec((None, tp, LANE), lambda b, p, kk: (p, b, 0))],
        out_specs=[pl.BlockSpec((tp, 2 * HP), lambda b, p, kk: (b, p)),
                   pl.BlockSpec((blk, 2 * HP), lambda b, p, kk: (b * nq + kk, p)),
                   pl.BlockSpec((blk, LANE), lambda b, p, kk: (b * nq + kk, p))],
        out_shape=[S((rows, MLA_HEADS * HP), MXU_DT), S((rows, MLA_HEADS * HP), MXU_DT),
                   S((rows, MLA_HEADS * V_HEAD), MXU_DT)],
        scratch_shapes=[pltpu.VMEM((tp, 2 * HP), F32), pltpu.VMEM((tp, LANE), F32)],
        sem=("parallel", "parallel", "arbitrary"), name=name)


def rope_bwd(dq, dk, dv, tabq, tabk, tp, tm, name):
    rows = dq.shape[0]
    nt = tp // tm
    wq = MLA_HEADS * HP

    def body(dq_r, dk_r, dv_r, tq_r, tk_r, dqa_o, dkv_o, dkr_o):
        dqv = dq_r[...].astype(F32)
        lane = _lane_mod(dqv.shape)
        in_rope = (lane >= QK_NOPE) & (lane < QK_HEAD)
        rope = jnp.where(in_rope, dqv, 0.0)
        da = jnp.where(lane < QK_HEAD, dqv, 0.0) + pltpu.roll(rope, QK_ROPE, axis=1)
        dqa_o[...] = (da * jnp.tile(tq_r[...], (1, MLA_HEADS))).astype(MXU_DT)
        dkf = dk_r[...].astype(F32)
        dkv_o[:, 0:wq] = jnp.where(lane < QK_NOPE, dkf, 0.0).astype(MXU_DT)
        dkv_o[:, wq:] = dv_r[...]
        kr = jnp.where(in_rope, dkf, 0.0)
        tot = kr[:, 0:HP]
        for h in range(1, MLA_HEADS):
            tot = tot + kr[:, h * HP:(h + 1) * HP]
        dkr_o[...] = ((tot + pltpu.roll(tot, QK_ROPE, axis=1)) * tk_r[...]).astype(MXU_DT)

    def rowblk(wd):
        return pl.BlockSpec((tm, wd), lambda i: (i, 0))

    tab = pl.BlockSpec((tm, HP), lambda i: (i % nt, 0))
    return pl.pallas_call(
        body, grid=(rows // tm,), in_specs=[rowblk(wq), rowblk(wq), rowblk(MLA_HEADS * V_HEAD), tab, tab],
        out_specs=[rowblk(wq), rowblk(wq + MLA_HEADS * V_HEAD), rowblk(HP)],
        out_shape=[S((rows, wq), MXU_DT), S((rows, wq + MLA_HEADS * V_HEAD), MXU_DT), S((rows, HP), MXU_DT)],
        compiler_params=_cp(("parallel",)), name=name)(dq, dk, dv, tabq, tabk)


def loss_head(h, target, gain, tp, t_real, tm, name):
    rows = h.shape[0]
    nt = tp // tm

    def body(h_r, t_r, g_r, dh_o, loss_o, dg_o):
        i = pl.program_id(0)

        @pl.when(i == 0)
        def _():
            loss_o[...] = jnp.zeros_like(loss_o)
            dg_o[...] = jnp.zeros_like(dg_o)

        xv = h_r[...]
        rstd = lax.rsqrt(jnp.mean(xv * xv, axis=-1, keepdims=True) + EPS)
        xhat = xv * rstd
        g = g_r[...]
        pos = (i % nt) * tm + lax.broadcasted_iota(jnp.int32, (tm, 1), 0)
        valid = (pos >= N_META) & (pos < t_real)
        err = jnp.where(valid, xhat * g - t_r[...], 0.0)
        loss_o[...] += 0.5 * jnp.sum(jnp.mean(err * err, axis=-1, keepdims=True))
        dy = err * (1.0 / D_MODEL)
        dg_o[...] += jnp.sum(dy * xhat, axis=0, keepdims=True)
        dxh = dy * g
        dh_o[...] = rstd * (dxh - xhat * jnp.mean(dxh * xhat, axis=-1, keepdims=True))

    blk = pl.BlockSpec((tm, D_MODEL), lambda i: (i, 0))
    return pl.pallas_call(
        body, grid=(rows // tm,), in_specs=[blk, blk, pl.BlockSpec((1, D_MODEL), lambda i: (0, 0))],
        out_specs=[blk, pl.BlockSpec((1, LANE), lambda i: (0, 0)), pl.BlockSpec((1, D_MODEL), lambda i: (0, 0))],
        out_shape=[S((rows, D_MODEL), F32), S((1, LANE), F32), S((1, D_MODEL), F32)],
        compiler_params=_cp(("arbitrary",)), name=name)(h, target, gain)


ADAM_TILE_ELEMS = 128 * 1024


def adamw(g, w, m, v, name):
    shape = w.shape
    cols = shape[-1]
    rws = max(1, math.prod(shape[:-1]))
    tr = rws if rws * cols <= ADAM_TILE_ELEMS else _div_tile(rws, max(SUBLANE, ADAM_TILE_ELEMS // cols), SUBLANE)
    bc1 = 1.0 - ADAM_B1 ** ADAM_STEP
    bc2 = 1.0 - ADAM_B2 ** ADAM_STEP

    def body(g_r, w_r, m_r, v_r, do, mo, vo):
        gv = g_r[...]
        mn = ADAM_B1 * m_r[...] + (1.0 - ADAM_B1) * gv
        vn = ADAM_B2 * v_r[...] + (1.0 - ADAM_B2) * (gv * gv)
        m_hat = mn / bc1
        v_hat = vn / bc2
        do[...] = -ADAM_LR * (m_hat / (jnp.sqrt(v_hat) + ADAM_EPS) + ADAM_WD * w_r[...])
        mo[...] = mn
        vo[...] = vn

    blk = pl.BlockSpec((tr, cols), lambda i: (i, 0))
    outs = pl.pallas_call(
        body, grid=(rws // tr,), in_specs=[blk] * 4, out_specs=[blk] * 3, out_shape=[S((rws, cols), F32)] * 3,
        compiler_params=_cp(("parallel",)), name=name)(*[a.reshape(rws, cols) for a in (g, w, m, v)])
    return tuple(o.reshape(shape) for o in outs)


SUM_TILE_ELEMS = 128 * 1024


def _place():
    return lax.axis_index("x"), lax.axis_index("y"), lax.axis_index("c")


def _remote(src, dst, send_sems, recv_sems, k, to):
    return pltpu.make_async_remote_copy(src_ref=src, dst_ref=dst, send_sem=send_sems.at[k], recv_sem=recv_sems.at[k],
                                        device_id=to, device_id_type=MESH)


def chip_index():
    return 2 * lax.axis_index("x") + lax.axis_index("y")


def _sem_pair(n):
    return [pltpu.SemaphoreType.DMA((n,)), pltpu.SemaphoreType.DMA((n,))]


def stage_gather_chips(xs):
    def copies(ins, outs, sems):
        send_sems, recv_sems = sems
        mx, my, mc = _place()
        sibling = (mx, my, 1 - mc)
        chips = [(1 - mx, my), (mx, 1 - my), (1 - mx, 1 - my)]
        first, landed, passed, from_sibling = [], [], [], []
        for i, (x_ref, out_ref) in enumerate(zip(ins, outs)):
            def piece(cx, cy, h, out_ref=out_ref):
                return out_ref.at[2 * cx + cy, h]

            for j, (cx, cy) in enumerate(chips):
                k = 6 * i + j
                first.append(_remote(x_ref.at[mc], piece(mx, my, mc), send_sems, recv_sems, k, (cx, cy, mc)))
                landed.append(_remote(x_ref.at[mc], piece(cx, cy, mc), send_sems, recv_sems, k, (cx, cy, mc)))
                passed.append(_remote(piece(cx, cy, mc), piece(cx, cy, mc), send_sems, recv_sems, k + 3, sibling))
                from_sibling.append(_remote(x_ref.at[mc], piece(cx, cy, 1 - mc), send_sems, recv_sems, k + 3, sibling))
        return first, landed, passed, from_sibling

    def start(ins, outs, sems):
        for cp in copies(ins, outs, sems)[0]:
            cp.start()

    def finish(ins, outs, sems):
        first, landed, passed, from_sibling = copies(ins, outs, sems)
        for arrived, onward in zip(landed, passed):
            arrived.wait_recv()
            onward.start()
        for cp in from_sibling:
            cp.wait_recv()
        for cp in first + passed:
            cp.wait_send()

    return Stage(list(xs), [S((4,) + x.shape, x.dtype) for x in xs], _sem_pair(6 * len(xs)), start, finish)


def own_block(gathered, xs):
    return lax.dynamic_update_slice(gathered, xs[None], (chip_index(), 0, 0, 0))


def stage_pair_exchange(gs):
    def copies(ins, outs, sems):
        send_sems, recv_sems = sems
        mx, my, mc = _place()
        return [_remote(g_ref.at[s, 1 - mc], land_ref.at[s], send_sems, recv_sems, 4 * i + s, (mx, my, 1 - mc))
                for i, (g_ref, land_ref) in enumerate(zip(ins, outs)) for s in range(4)]

    def start(ins, outs, sems):
        for cp in copies(ins, outs, sems):
            cp.start()

    def finish(ins, outs, sems):
        cps = copies(ins, outs, sems)
        for cp in cps:
            cp.wait_recv()
        for cp in cps:
            cp.wait_send()

    return Stage(list(gs), [S((4,) + g.shape[2:], g.dtype) for g in gs], _sem_pair(4 * len(gs)), start, finish)


def _sum_rows(rws, width):
    return _div_tile(rws, max(SUBLANE, SUM_TILE_ELEMS // width), SUBLANE)


def pair_sum(g4, land, c_idx, name):
    _, _, rws, wd = g4.shape
    th = _sum_rows(rws, wd)

    def body(c_ref, a_ref, b_ref, o_ref):
        o_ref[...] = a_ref[...] + b_ref[...]

    return pl.pallas_call(
        body,
        grid_spec=pltpu.PrefetchScalarGridSpec(
            num_scalar_prefetch=1, grid=(4, rws // th),
            in_specs=[pl.BlockSpec((None, None, th, wd), lambda s, i, c: (s, c[0], i, 0)),
                      pl.BlockSpec((None, th, wd), lambda s, i, c: (s, i, 0))],
            out_specs=pl.BlockSpec((None, th, wd), lambda s, i, c: (s, i, 0))),
        out_shape=S((4, rws, wd), F32), compiler_params=_cp(("parallel", "parallel")), name=name)(c_idx, g4, land)


def stage_chip_scatter(ps):
    def copies(ins, outs, sems):
        send_sems, recv_sems = sems
        mx, my, mc = _place()
        me = 2 * mx + my
        chips = [(1 - mx, my), (mx, 1 - my), (1 - mx, 1 - my)]
        sent, landed = [], []
        for i, (p_ref, land_ref) in enumerate(zip(ins, outs)):
            for j, (cx, cy) in enumerate(chips):
                k = 3 * i + j
                sent.append(_remote(p_ref.at[2 * cx + cy], land_ref.at[me], send_sems, recv_sems, k, (cx, cy, mc)))
                landed.append(_remote(p_ref.at[me], land_ref.at[2 * cx + cy], send_sems, recv_sems, k, (cx, cy, mc)))
        return sent, landed

    def start(ins, outs, sems):
        for cp in copies(ins, outs, sems)[0]:
            cp.start()

    def finish(ins, outs, sems):
        sent, landed = copies(ins, outs, sems)
        for cp in landed:
            cp.wait_recv()
        for cp in sent:
            cp.wait_send()

    return Stage(list(ps), [S(p.shape, p.dtype) for p in ps], _sem_pair(3 * len(ps)), start, finish)


def chip_sum(l4, p4, me_idx, name):
    _, rws, wd = l4.shape
    th = _sum_rows(rws, wd)

    def body(me_ref, a, b, c, d, own, o_ref):
        me = me_ref[0]
        parts = [jnp.where(me == s, own[...], r[...]) for s, r in enumerate((a, b, c, d))]
        o_ref[...] = ((parts[0] + parts[1]) + parts[2]) + parts[3]

    def blk(s):
        return pl.BlockSpec((None, th, wd), lambda i, me: (jnp.where(me[0] == s, (s + 1) % 4, s), i, 0))

    return pl.pallas_call(
        body,
        grid_spec=pltpu.PrefetchScalarGridSpec(
            num_scalar_prefetch=1, grid=(rws // th,),
            in_specs=[blk(0), blk(1), blk(2), blk(3), pl.BlockSpec((None, th, wd), lambda i, me: (me[0], i, 0))],
            out_specs=pl.BlockSpec((th, wd), lambda i, me: (i, 0))),
        out_shape=S((rws, wd), F32), compiler_params=_cp(("parallel",)), name=name)(me_idx, l4, l4, l4, l4, p4)


def stage_pair_gather(rs):
    def copies(ins, outs, sems):
        send_sems, recv_sems = sems
        mx, my, mc = _place()
        return [_remote(r_ref, out_ref, send_sems, recv_sems, i, (mx, my, 1 - mc))
                for i, (r_ref, out_ref) in enumerate(zip(ins, outs))]

    def start(ins, outs, sems):
        for cp in copies(ins, outs, sems):
            cp.start()

    def finish(ins, outs, sems):
        for cp in copies(ins, outs, sems):
            cp.wait()

    return Stage(list(rs), [S(r.shape, r.dtype) for r in rs], _sem_pair(len(rs)), start, finish)


PACK_ELEMS = 16 * LANE


def pack_rows(arrays, lead, total_mult):
    parts, offs, r0 = [], [], 0
    for a in arrays:
        flat = a.reshape(a.shape[:lead] + (-1,))
        elems = _round_up(flat.shape[-1], PACK_ELEMS)
        flat = jnp.pad(flat, [(0, 0)] * lead + [(0, elems - flat.shape[-1])])
        parts.append(flat.reshape(flat.shape[:lead] + (elems // LANE, LANE)))
        offs.append((r0, elems // LANE))
        r0 += elems // LANE
    total = _round_up(r0, total_mult)
    if total > r0:
        parts.append(jnp.zeros(parts[0].shape[:lead] + (total - r0, LANE), parts[0].dtype))
    return jnp.concatenate(parts, axis=lead), offs


def unpack_rows(buf, off, shape):
    r0, nr = off
    lead = buf.shape[:-2]
    n = math.prod(shape)
    return buf[..., r0:r0 + nr, :].reshape(lead + (nr * LANE,))[..., :n].reshape(lead + tuple(shape))


def unshard(stacked, axis):
    x = jnp.moveaxis(stacked, 0, axis)
    return x.reshape(x.shape[:axis] + (4 * x.shape[axis + 1],) + x.shape[axis + 2:])


def to_shards(full, axis):
    n = full.shape[axis] // 4
    x = full.reshape(full.shape[:axis] + (4, n) + full.shape[axis + 1:])
    return jnp.moveaxis(x, axis, 0)


def _rot_cols(w):
    half = w.shape[-1] // 2
    return jnp.concatenate([-w[..., half:], w[..., :half]], axis=-1)


def _unrot_cols(dw):
    half = dw.shape[-1] // 2
    return jnp.concatenate([dw[..., half:], -dw[..., :half]], axis=-1)


def odd_w_in_padded(w_in):
    kr = w_in[:, Q_LORA + KV_LORA:]
    rows = w_in.shape[0]
    return jnp.concatenate([w_in[:, :Q_LORA], jnp.zeros((rows, 128), w_in.dtype), w_in[:, Q_LORA:Q_LORA + KV_LORA],
                            jnp.zeros((rows, 64), w_in.dtype), kr, _rot_cols(kr)], axis=1)


def odd_w_in_unpad(dwp):
    base = 512 + KV_LORA + 64
    dkr = dwp[:, base:base + QK_ROPE] + _unrot_cols(dwp[:, base + QK_ROPE:base + 2 * QK_ROPE])
    return jnp.concatenate([dwp[:, :Q_LORA], dwp[:, 512:512 + KV_LORA], dkr], axis=1)


def uq_padded(w_uq):
    w = w_uq.reshape(Q_LORA, MLA_HEADS, QK_HEAD)
    return jnp.concatenate([w, _rot_cols(w[:, :, QK_NOPE:])], axis=-1).reshape(Q_LORA, MLA_HEADS * HP)


def uq_unpad(dwp):
    d = dwp.reshape(Q_LORA, MLA_HEADS, HP)
    rope = d[:, :, QK_NOPE:QK_HEAD] + _unrot_cols(d[:, :, QK_HEAD:])
    return jnp.concatenate([d[:, :, :QK_NOPE], rope], axis=-1).reshape(Q_LORA, MLA_HEADS * QK_HEAD)


def ukv_padded(w_ukv):
    w = w_ukv.reshape(KV_LORA, MLA_HEADS, QK_NOPE + V_HEAD)
    wk = jnp.concatenate([w[:, :, :QK_NOPE], jnp.zeros((KV_LORA, MLA_HEADS, HP - QK_NOPE), w.dtype)], axis=-1)
    return jnp.concatenate([wk.reshape(KV_LORA, MLA_HEADS * HP), w[:, :, QK_NOPE:].reshape(KV_LORA, MLA_HEADS * V_HEAD)],
                           axis=1)


def ukv_unpad(dwp):
    dk = dwp[:, :MLA_HEADS * HP].reshape(KV_LORA, MLA_HEADS, HP)[:, :, :QK_NOPE]
    dv = dwp[:, MLA_HEADS * HP:].reshape(KV_LORA, MLA_HEADS, V_HEAD)
    return jnp.concatenate([dk, dv], axis=-1).reshape(KV_LORA, MLA_HEADS * (QK_NOPE + V_HEAD))


def block_diag(w):
    h, d, _ = w.shape
    eye = jnp.eye(h, dtype=w.dtype)
    return (eye[:, None, :, None] * w[:, :, None, :]).reshape(h * d, h * d)


def block_diag_part(dense, h):
    d = dense.shape[0] // h
    x = dense.reshape(h, d, h, d)
    return jnp.stack([x[i, :, i, :] for i in range(h)], axis=0)


def rope_tables(tp):
    pos = jnp.arange(tp, dtype=F32)
    inv_freq = ROPE_BASE ** (-jnp.arange(0, QK_ROPE, 2, dtype=F32) / QK_ROPE)
    ang = pos[:, None] * inv_freq[None, :]
    cos2 = jnp.tile(jnp.cos(ang), (1, 2))
    sin2 = jnp.tile(jnp.sin(ang), (1, 2))
    tabq = jnp.concatenate([jnp.ones((tp, QK_NOPE), F32), cos2, sin2], axis=1)
    tabk = jnp.concatenate([jnp.zeros((tp, QK_NOPE), F32), cos2, sin2], axis=1)
    return tabq, tabk


class Dims:
    def __init__(self, nb, seq):
        self.nb = nb
        self.t_real = seq + N_META
        self.tp = _round_up(self.t_real, ATT_BLK)
        self.n = self.tp // 4
        assert self.n % 16 == 0
        self.rows = nb * self.tp


class NoComm:
    def advance(self, carried):
        return None


def even_fwd(h, p, dm, comm):
    (u, hn), _ = norm_matmul(h, 0, D_MODEL, p["norm"], p["w_in"], dm.n, 512, F32, "ev_in")
    (y, ca, xc, a, hs), got = even_mid_fwd(u, p["conv_a"], p["conv_b"], p["conv_b_bias"], p["rw"], p["r_b"], p["iw"],
                                           p["i_b"], p["lam"], dm.nb, dm.tp, dm.n, "ev_mid", carry=comm.advance(None))
    comm.advance(got)
    out = matmul_res(y, p["w_out"].reshape(2, CONV_W, D_MODEL), h, dm.n, 512, "ev_out")
    return out, (h, u, hn, ca, xc, a, hs, y)


def even_bwd(dout, saved, p, dm, comm):
    h, u, hn, ca, xc, a, hs, y = saved
    g = {}
    dycat = matmul_nt(dout, p["w_out"], dm.n, 512, F32, "ev_dycat")
    g["w_out"], got = matmul_tn(y, dout, dm.n, "ev_dw_out", carry=comm.advance(None))
    outs, got = even_mid_bwd(u, dycat, ca, xc, a, hs, p["conv_a"], p["conv_b"], p["rw"], p["r_b"], p["iw"], p["i_b"],
                             p["lam"], dm.nb, dm.tp, dm.n, "ev_mid_bwd", carry=comm.advance(got))
    du, g["conv_a"], g["conv_b"], g["conv_b_bias"], drw, g["r_b"], diw, g["i_b"], g["lam"] = outs
    g["r_w"] = block_diag_part(drw, LRU_HEADS)
    g["i_w"] = block_diag_part(diw, LRU_HEADS)
    g["w_in"], got = matmul_tn(hn, du, dm.n, "ev_dw_in", carry=comm.advance(got))
    comm.advance(got)
    dx, g["norm"] = matmul_nt_normbwd(du, p["w_in"], h, 0, p["norm"], dout, dm.n, 512, F32, "ev_dx")
    return dx, g


def odd_fwd(h, p, tabq, tabk, dm, comm):
    nt = dm.tp // dm.n
    (u, hn), _ = norm_matmul(h, 0, D_MODEL, p["norm"], p["w_in_p"], dm.n, ODD_PAD, F32, "od_in")
    tab_spec = pl.BlockSpec((dm.n, HP), lambda i, j: (i % nt, 0))
    (q, cqn), _ = norm_matmul(u, 0, Q_LORA, p["q_norm"], p["w_uq_p"], dm.n, 512, MXU_DT, "od_q",
                              epi=_q_rope_epi, epi_ops=(tabq,), epi_specs=(tab_spec,))
    kr_spec = pl.BlockSpec((dm.n, HP), lambda i, j: (i, ODD_KR_COL))
    (k, ckvn), _ = norm_matmul(u, ODD_CKV_COL, KV_LORA, p["kv_norm"], p["w_uk_p"], dm.n, 512, MXU_DT, "od_k",
                               epi=_k_rope_epi, epi_ops=(u, tabk), epi_specs=(kr_spec, tab_spec))
    (v, _), _ = norm_matmul(u, ODD_CKV_COL, KV_LORA, p["kv_norm"], p["w_uv_p"], dm.n, 512, MXU_DT, "od_v")
    (o, lse), got = attn_fwd(q, k, v, dm.nb, dm.tp, "od_attn", carry=comm.advance(None))
    comm.advance(got)
    out = matmul_res(o[None], p["w_out"][None], h, dm.n, 512, "od_out")
    return out, (h, u, hn, cqn, ckvn, q, k, v, o, lse)


def odd_bwd(dout, saved, p, tabq, tabk, dm, comm):
    h, u, hn, cqn, ckvn, q, k, v, o, lse = saved
    g = {}
    do = matmul_nt(dout, p["w_out"], dm.n, 512, MXU_DT, "od_do")
    g["w_out"], got = matmul_tn(o, dout, dm.n, "od_dw_out", carry=comm.advance(None))
    (dq, dk, dv), got = attn_bwd(q, k, v, o, do, lse, dm.nb, dm.tp, "od_attn_bwd", carry=comm.advance(got))
    dqa, dkv, dkr = rope_bwd(dq, dk, dv, tabq, tabk, dm.tp, dm.n, "od_rope_bwd")
    g["w_uq_p"], got = matmul_tn(cqn, dqa, dm.n, "od_dw_uq", carry=comm.advance(got))
    comm.advance(got)
    g["w_ukv_p"], _ = matmul_tn(ckvn, dkv, dm.n, "od_dw_ukv")
    dcq, g["q_norm"] = matmul_nt_normbwd(dqa, p["w_uq_p"], u, 0, p["q_norm"], None, dm.n, 512, MXU_DT, "od_dcq")
    dckv, g["kv_norm"] = matmul_nt_normbwd(dkv, p["w_ukv_p"], u, ODD_CKV_COL, p["kv_norm"], None, dm.n, 512, MXU_DT,
                                           "od_dckv")
    du = jnp.concatenate([dcq, jnp.zeros((dm.rows, 128), MXU_DT), dckv, dkr], axis=1)
    g["w_in_p"], _ = matmul_tn(hn, du, dm.n, "od_dw_in")
    dx, g["norm"] = matmul_nt_normbwd(du, p["w_in_p"], h, 0, p["norm"], dout, dm.n, ODD_PAD, F32, "od_dx")
    return dx, g


def ffn_fwd(h, p, dm, comm):
    (up, hn), got = norm_matmul(h, 0, D_MODEL, p["norm"], p["w_up"], dm.n, D_FF // 2, MXU_DT, "ffn_up",
                                carry=comm.advance(None))
    comm.advance(got)
    u, y = ffn_mid_fwd(up, p["cw"], p["cb"], dm.nb, dm.tp, dm.n, "ffn_mid")
    out = matmul_res(y[None], p["w_down"][None], h, dm.n, 512, "ffn_down")
    return out, (h, up, hn, u, y)


def ffn_bwd(dout, saved, p, dm, comm):
    h, up, hn, u, y = saved
    g = {}
    dy = matmul_nt(dout, p["w_down"], dm.n, D_FF // 2, MXU_DT, "ffn_dy")
    g["w_down"], got = matmul_tn(y, dout, dm.n, "ffn_dw_down", carry=comm.advance(None))
    (dup, g["cw"], g["cb"]), got = ffn_mid_bwd(dy, u, up, p["cw"], dm.nb, dm.tp, dm.n, "ffn_mid_bwd",
                                               carry=comm.advance(got))
    g["w_up"], got = matmul_tn(hn, dup, dm.n, "ffn_dw_up", carry=comm.advance(got), col_shards=4)
    comm.advance(got)
    dx, g["norm"] = matmul_nt_normbwd(dup, p["w_up"], h, 0, p["norm"], dout, dm.n, D_FF // 2, F32, "ffn_dx")
    return dx, g


def _row(v):
    return v.reshape(1, -1)


def even_params(wf, j):
    return dict(norm=_row(wf["ev_norm"][j]), w_in=wf["ev_w_in"], conv_a=wf["ev_conv_a"][j], conv_b=wf["ev_conv_b"][j],
                conv_b_bias=_row(wf["ev_conv_b_bias"][j]), rw=block_diag(wf["ev_gate_r_w"][j]).astype(MXU_DT),
                r_b=_row(wf["ev_gate_r_b"][j]), iw=block_diag(wf["ev_gate_i_w"][j]).astype(MXU_DT),
                i_b=_row(wf["ev_gate_i_b"][j]), lam=_row(wf["ev_lru_lambda"][j]), w_out=wf["ev_w_out"])


def odd_params(wf, j):
    wkv = ukv_padded(wf["od_w_ukv"])
    return dict(norm=_row(wf["od_norm"][j]), w_in_p=odd_w_in_padded(wf["od_w_in"]), q_norm=_row(wf["od_q_norm"][j]),
                kv_norm=_row(wf["od_kv_norm"][j]), w_uq_p=uq_padded(wf["od_w_uq"]), w_ukv_p=wkv,
                w_uk_p=wkv[:, :MLA_HEADS * HP], w_uv_p=wkv[:, MLA_HEADS * HP:], w_out=wf["od_w_out"])


def ffn_params(wf, layer):
    return dict(norm=_row(wf["ffn_norm"][layer]), w_up=wf["ffn_w_up"],
                cw=jnp.moveaxis(wf["ffn_conv_w"][layer].reshape(3, 2, D_FF), 1, 0),
                cb=wf["ffn_conv_b"][layer].reshape(2, 1, D_FF), w_down=wf["ffn_w_down"])


def even_grads(g):
    out = {"ev_" + k_: g[k_] for k_ in ("w_in", "conv_a", "conv_b", "w_out")}
    out.update({"ev_norm": g["norm"][0], "ev_conv_b_bias": g["conv_b_bias"][0], "ev_gate_r_w": g["r_w"],
                "ev_gate_r_b": g["r_b"][0], "ev_gate_i_w": g["i_w"], "ev_gate_i_b": g["i_b"][0],
                "ev_lru_lambda": g["lam"][0]})
    return out


def odd_grads(g):
    return {"od_norm": g["norm"][0], "od_q_norm": g["q_norm"][0], "od_kv_norm": g["kv_norm"][0],
            "od_w_in": odd_w_in_unpad(g["w_in_p"]), "od_w_uq": uq_unpad(g["w_uq_p"]),
            "od_w_ukv": ukv_unpad(g["w_ukv_p"]), "od_w_out": g["w_out"]}


def ffn_grads(g):
    return {"ffn_norm": g["norm"][0], "ffn_w_up": g["w_up"], "ffn_conv_w": jnp.moveaxis(g["cw"], 0, 1).reshape(3, 2 * D_FF),
            "ffn_conv_b": g["cb"].reshape(2 * D_FF), "ffn_w_down": g["w_down"]}


WEIGHTS = ["meta_tokens", "ev_norm", "ev_w_in", "ev_conv_a", "ev_conv_b", "ev_conv_b_bias", "ev_gate_r_w", "ev_gate_r_b",
           "ev_gate_i_w", "ev_gate_i_b", "ev_lru_lambda", "ev_w_out", "od_norm", "od_w_in", "od_q_norm", "od_kv_norm",
           "od_w_uq", "od_w_ukv", "od_w_out", "ffn_norm", "ffn_w_up", "ffn_conv_w", "ffn_conv_b", "ffn_w_down",
           "final_norm"]
SHARD_AXIS = {"meta_tokens": 1, "ev_w_in": 2, "ev_conv_a": 2, "ev_conv_b": 2, "ev_w_out": 1, "od_norm": 1, "od_w_in": 1,
              "od_q_norm": 1, "od_kv_norm": 1, "od_w_uq": 2, "od_w_ukv": 2, "od_w_out": 1, "ffn_w_up": 2,
              "ffn_conv_w": 2, "ffn_w_down": 1}
MATMUL_WEIGHTS = ["ev_w_in", "ev_w_out", "od_w_in", "od_w_uq", "od_w_ukv", "od_w_out", "ffn_w_up", "ffn_w_down"]


LAYER_ORDER = [("ev", 0), ("ffn", 0), ("od", 0), ("ffn", 1), ("ev", 1), ("ffn", 2), ("od", 1), ("ffn", 3)]
LAYER_MATMUL = {"ev": ["ev_w_in", "ev_w_out"], "od": ["od_w_in", "od_w_uq", "od_w_ukv", "od_w_out"],
                "ffn": ["ffn_w_up", "ffn_w_down"]}
LAYER_SHARDED = {"ev": ["ev_w_in", "ev_conv_a", "ev_conv_b", "ev_w_out"],
                 "od": ["od_norm", "od_w_in", "od_q_norm", "od_kv_norm", "od_w_uq", "od_w_ukv", "od_w_out"],
                 "ffn": ["ffn_w_up", "ffn_conv_w", "ffn_w_down"]}
STACKED_SHARDS = "ffn_w_up"


def gather_at_entry(w, names, first, name):
    buf, offs = pack_rows([w[n] for n in names], 0, 32)
    halves = buf.reshape(2, buf.shape[0] // 2, LANE)
    outs = run_stage(stage_gather_chips([halves] + first.halves), name)
    first.step, first.got = 2, outs[1:]
    got = own_block(outs[0], halves).reshape(4, buf.shape[0], LANE)
    return {n: unshard(unpack_rows(got, off, w[n].shape), SHARD_AXIS[n]) for n, off in zip(names, offs)}


def _halves(a):
    return a.reshape(2, a.shape[0] // 2, a.shape[1])


class GatherComm:
    def __init__(self, w, kind, idx):
        self.names = LAYER_MATMUL[kind]
        self.halves = [_halves(w[n][idx].astype(MXU_DT)) for n in self.names]
        self.stage = stage_gather_chips(self.halves)
        self.step, self.got = 0, None

    def advance(self, carried):
        self.step += 1
        if self.step == 1:
            return self.stage
        if self.step == 2:
            self.got = carried
        return None

    def weights(self):
        out = {}
        for n, got, own in zip(self.names, self.got, self.halves):
            stacked = own_block(got, own).reshape(4, 2 * own.shape[1], own.shape[2])
            out[n] = stacked if n == STACKED_SHARDS else unshard(stacked, SHARD_AXIS[n] - 1)
        return out


class ReduceComm:
    def __init__(self, grads, axes, c_idx, tag, tail=None):
        shards = {n: grads[n] if n == STACKED_SHARDS else to_shards(grads[n], axes[n]) for n in grads}
        self.big = [n for n in grads if n in MATMUL_WEIGHTS]
        self.small = [n for n in grads if n not in MATMUL_WEIGHTS]
        self.shapes = {n: shards[n].shape[1:] for n in grads}
        arrays = [shards[n].reshape(4, 2, shards[n].shape[1] // 2, shards[n].shape[2]) for n in self.big]
        gs, self.offs = pack_rows([shards[n] for n in self.small], 1, 16)
        self.rs = gs.shape[1] // 2
        parts = [gs.reshape(4, 2, self.rs, LANE)]
        self.rr = 0
        if tail is not None:
            self.rr = tail.shape[0] // 8
            parts.append(tail.reshape(4, 2, self.rr, LANE))
        arrays.append(jnp.concatenate(parts, axis=2) if len(parts) > 1 else parts[0])
        self.arrays, self.c_idx, self.tag, self.step = arrays, c_idx, tag, 0
        self.part = self.mine = self.theirs = None

    def advance(self, carried):
        self.step += 1
        if self.step == 1:
            return stage_pair_exchange(self.arrays)
        if self.step == 2:
            self.part = [pair_sum(g, land, self.c_idx, "grad_pair_sum_%s_%d" % (self.tag, i))
                         for i, (g, land) in enumerate(zip(self.arrays, carried))]
            return stage_chip_scatter(self.part)
        if self.step == 3:
            me_idx = chip_index().astype(jnp.int32).reshape(1)
            self.mine = [chip_sum(land, part, me_idx, "grad_chip_sum_%s_%d" % (self.tag, i))
                         for i, (land, part) in enumerate(zip(carried, self.part))]
            return stage_pair_gather(self.mine)
        if self.step == 4:
            self.theirs = carried
        return None

    def run_alone(self, name):
        stage = self.advance(None)
        while stage is not None:
            stage = self.advance(run_stage(stage, name + "_%d" % self.step))

    def results(self):
        south = self.c_idx[0] == 0
        boths = [jnp.stack([jnp.where(south, m, t), jnp.where(south, t, m)], axis=0)
                 for m, t in zip(self.mine, self.theirs)]
        out = {n: b.reshape(self.shapes[n]) for n, b in zip(self.big, boths)}
        packed = boths[-1]
        flat = packed[:, :self.rs].reshape(2 * self.rs, LANE)
        out.update({n: unpack_rows(flat, off, self.shapes[n]) for n, off in zip(self.small, self.offs)})
        return out, packed[:, self.rs:self.rs + self.rr]


def kernel(x, meta_tokens, ev_norm, ev_w_in, ev_conv_a, ev_conv_b, ev_conv_b_bias, ev_gate_r_w, ev_gate_r_b, ev_gate_i_w, ev_gate_i_b, ev_lru_lambda, ev_w_out, od_norm, od_w_in, od_q_norm, od_kv_norm, od_w_uq, od_w_ukv, od_w_out, ffn_norm, ffn_w_up, ffn_conv_w, ffn_conv_b, ffn_w_down, final_norm, loss_target, m_meta_tokens, m_ev_norm, m_ev_w_in, m_ev_conv_a, m_ev_conv_b, m_ev_conv_b_bias, m_ev_gate_r_w, m_ev_gate_r_b, m_ev_gate_i_w, m_ev_gate_i_b, m_ev_lru_lambda, m_ev_w_out, m_od_norm, m_od_w_in, m_od_q_norm, m_od_kv_norm, m_od_w_uq, m_od_w_ukv, m_od_w_out, m_ffn_norm, m_ffn_w_up, m_ffn_conv_w, m_ffn_conv_b, m_ffn_w_down, m_final_norm, v_meta_tokens, v_ev_norm, v_ev_w_in, v_ev_conv_a, v_ev_conv_b, v_ev_conv_b_bias, v_ev_gate_r_w, v_ev_gate_r_b, v_ev_gate_i_w, v_ev_gate_i_b, v_ev_lru_lambda, v_ev_w_out, v_od_norm, v_od_w_in, v_od_q_norm, v_od_kv_norm, v_od_w_uq, v_od_w_ukv, v_od_w_out, v_ffn_norm, v_ffn_w_up, v_ffn_conv_w, v_ffn_conv_b, v_ffn_w_down, v_final_norm):
    given = dict(locals())
    w = {n: given[n] for n in WEIGHTS}
    nb, seq, _ = x.shape
    dm = Dims(nb, seq)
    n_layers = len(LAYER_ORDER)

    wf = {n: w[n] for n in WEIGHTS if n not in SHARD_AXIS}
    gathers = [GatherComm(w, kind, idx) for kind, idx in LAYER_ORDER]
    wf.update(gather_at_entry(w, [n for n in SHARD_AXIS if n not in MATMUL_WEIGHTS], gathers[0], "gather_at_entry"))

    tail = dm.tp - dm.t_real
    meta = jnp.broadcast_to(wf["meta_tokens"][None], (nb, N_META, D_MODEL))
    h = jnp.concatenate([meta, x, jnp.zeros((nb, tail, D_MODEL), F32)], axis=1).reshape(dm.rows, D_MODEL)
    tgt = jnp.pad(loss_target, ((0, 0), (N_META, tail), (0, 0))).reshape(dm.rows, D_MODEL)
    tabq, tabk = rope_tables(dm.tp)

    params, saved = [], []
    for i, (kind, idx) in enumerate(LAYER_ORDER):
        wl = dict(wf)
        wl.update(gathers[i].weights())
        comm = gathers[i + 1] if i + 1 < n_layers else NoComm()
        if kind == "ev":
            p = even_params(wl, idx)
            h, sv = even_fwd(h, p, dm, comm)
        elif kind == "od":
            p = odd_params(wl, idx)
            h, sv = odd_fwd(h, p, tabq, tabk, dm, comm)
        else:
            p = ffn_params(wl, idx)
            h, sv = ffn_fwd(h, p, dm, comm)
        params.append(p)
        saved.append(sv)

    dh, loss, dfinal = loss_head(h, tgt, _row(wf["final_norm"]), dm.tp, dm.t_real, dm.n, "loss_head")
    loss = lax.psum(loss[0, 0], ("x", "y", "c"))

    c_idx = lax.axis_index("c").astype(jnp.int32).reshape(1)
    layer_grads = {n: {} for n in WEIGHTS}
    pending, reduces = NoComm(), []
    for i in reversed(range(n_layers)):
        kind, idx = LAYER_ORDER[i]
        if kind == "ev":
            dh, g = even_bwd(dh, saved[i], params[i], dm, pending)
            g = even_grads(g)
        elif kind == "od":
            dh, g = odd_bwd(dh, saved[i], params[i], tabq, tabk, dm, pending)
            g = odd_grads(g)
        else:
            dh, g = ffn_bwd(dh, saved[i], params[i], dm, pending)
            g = ffn_grads(g)
        for n in g:
            if n not in SHARD_AXIS:
                layer_grads[n][idx] = g[n]
        if i > 0:
            pending = ReduceComm({n: g[n] for n in LAYER_SHARDED[kind]}, {n: SHARD_AXIS[n] - 1 for n in SHARD_AXIS},
                                 c_idx, "%s%d" % (kind, idx))
            reduces.append((pending, idx))
    dh3 = dh.reshape(nb, dm.tp, D_MODEL)
    grad_x = dh3[:, N_META:dm.t_real]

    repl = [n for n in WEIGHTS if n not in SHARD_AXIS]
    layer_grads["final_norm"] = {0: dfinal[0]}
    repl_full = {n: (layer_grads[n][0] if n == "final_norm" else
                     jnp.stack([layer_grads[n][j] for j in range(w[n].shape[0])], axis=0)) for n in repl}
    tail_buf, tail_offs = pack_rows([repl_full[n] for n in repl], 0, 64)
    first = {n: g[n] for n in LAYER_SHARDED["ev"]}
    first["meta_tokens"] = jnp.sum(dh3[:, :N_META], axis=0)
    axes = {n: SHARD_AXIS[n] - 1 for n in SHARD_AXIS}
    axes["meta_tokens"] = SHARD_AXIS["meta_tokens"]
    last = ReduceComm(first, axes, c_idx, "first_layer", tail=tail_buf)
    last.run_alone("grad_first_layer")
    reduces.append((last, 0))

    red = {}
    for comm, idx in reduces:
        got, tail_piece = comm.results()
        for n, v_ in got.items():
            if n == "meta_tokens":
                red[n] = v_
            else:
                layer_grads[n][idx] = v_
    tails = own_block(run_stage(stage_gather_chips([tail_piece]), "grad_gather_replicated")[0], tail_piece)
    tails = tails.reshape(tail_buf.shape[0], LANE)
    for n, off in zip(repl, tail_offs):
        red[n] = unpack_rows(tails, off, w[n].shape)
    for n in SHARD_AXIS:
        if n != "meta_tokens":
            red[n] = jnp.stack([layer_grads[n][j] for j in range(w[n].shape[0])], axis=0)

    outs = [adamw(red[n], w[n], given["m_" + n], given["v_" + n], "adamw_" + n) for n in WEIGHTS]
    return (loss, grad_x, *[red[n] for n in WEIGHTS], *[o[0] for o in outs], *[o[1] for o in outs],
            *[o[2] for o in outs])
```

```python
import math

import jax
import jax.numpy as jnp
from jax import lax
from jax.experimental import pallas as pl
from jax.experimental.pallas import tpu as pltpu

F32 = jnp.float32
MXU_DT = jnp.bfloat16
S = jax.ShapeDtypeStruct
MESH = pl.DeviceIdType.MESH

EPS = 1e-6
D_MODEL = 1024
N_META = 16
DEPTH = 4
CONV_W = 512
LRU_W = 512
LRU_HEADS = 8
LRU_C = 8.0
EVEN_IN = 2560
MLA_HEADS = 16
QK_NOPE = 64
QK_ROPE = 32
QK_HEAD = 96
V_HEAD = 64
Q_LORA = 384
KV_LORA = 256
ROPE_BASE = 10000.0
D_FF = 2816
ODD_PAD = 896
ODD_CKV_COL = 2
ODD_KR_COL = 6
HP = 128
ATT_BLK = 384
Q_PRESCALE = QK_HEAD ** -0.5 * math.log2(math.e)
FFN_CT = 256
EV_CT = 256
STRIP_ROWS = 352
LANE = 128
SUBLANE = 8
VMEM_LIMIT_MB = 52

ADAM_LR = 0.001
ADAM_B1 = 0.9
ADAM_B2 = 0.999
ADAM_EPS = 1e-08
ADAM_WD = 0.01
ADAM_STEP = 10

NT_DIMS = (((1,), (1,)), ((), ()))
TN_DIMS = (((0,), (0,)), ((), ()))


def _cp(sem):
    return pltpu.CompilerParams(dimension_semantics=sem, vmem_limit_bytes=VMEM_LIMIT_MB << 20)


def _div_tile(n, cap, mult):
    if n <= cap:
        return n
    best = None
    for t in range(mult, cap + 1, mult):
        if n % t == 0:
            best = t
    assert best is not None, (n, cap, mult)
    return best


def _round_up(n, m):
    return -(-n // m) * m


def mat_cols(arr):
    return arr.shape[1] if arr.ndim == 2 else arr.shape[0] * arr.shape[2]


def mat_width(arr):
    return arr.shape[-1]


def mat_spec(arr, tm, tw, rc):
    if arr.ndim == 2:
        return pl.BlockSpec((tm, tw), lambda *g: rc(*g))
    per = arr.shape[2] // tw
    assert arr.shape[2] % tw == 0

    def imap(*g):
        r, c = rc(*g)
        return (c // per, r, c % per)

    return pl.BlockSpec((None, tm, tw), imap)


HBM_SPEC = pl.BlockSpec(memory_space=pltpu.HBM)


class Stage:
    def __init__(self, inputs, out_shapes, sems, start, finish):
        self.inputs, self.out_shapes, self.sems, self.start, self.finish = inputs, out_shapes, sems, start, finish


def run_stage(stage, name):
    n_in, n_out = len(stage.inputs), len(stage.out_shapes)

    def body(*refs):
        ins, outs, sems = refs[:n_in], refs[n_in:n_in + n_out], refs[n_in + n_out:]
        stage.start(ins, outs, sems)
        stage.finish(ins, outs, sems)

    return pl.pallas_call(body, out_shape=list(stage.out_shapes), in_specs=[HBM_SPEC] * n_in,
                          out_specs=[HBM_SPEC] * n_out, scratch_shapes=list(stage.sems), name=name)(*stage.inputs)


def _call(body, ops, carry, *, grid, in_specs, out_specs, out_shape, scratch_shapes, sem, name):
    if carry is None:
        outs = pl.pallas_call(body, grid=grid, in_specs=in_specs, out_specs=out_specs, out_shape=out_shape,
                              scratch_shapes=scratch_shapes, compiler_params=_cp(sem), name=name)(*ops)
        return outs, None
    multi = isinstance(out_shape, (list, tuple))
    shapes = list(out_shape) if multi else [out_shape]
    ospecs = list(out_specs) if multi else [out_specs]
    n_in, n_out, n_sc = len(ops), len(shapes), len(scratch_shapes)
    c_in, c_out = len(carry.inputs), len(carry.out_shapes)

    def wrapped(*refs):
        ins, cin = refs[:n_in], refs[n_in:n_in + c_in]
        o0 = n_in + c_in
        outs, cout = refs[o0:o0 + n_out], refs[o0 + n_out:o0 + n_out + c_out]
        s0 = o0 + n_out + c_out
        scs, csems = refs[s0:s0 + n_sc], refs[s0 + n_sc:]
        first = pl.program_id(0) == 0
        last = pl.program_id(0) == grid[0] - 1
        for d in range(1, len(grid)):
            first = first & (pl.program_id(d) == 0)
            last = last & (pl.program_id(d) == grid[d] - 1)

        @pl.when(first)
        def _():
            carry.start(cin, cout, csems)

        body(*ins, *outs, *scs)

        @pl.when(last)
        def _():
            carry.finish(cin, cout, csems)

    res = pl.pallas_call(
        wrapped, grid=grid, in_specs=list(in_specs) + [HBM_SPEC] * c_in, out_specs=ospecs + [HBM_SPEC] * c_out,
        out_shape=shapes + list(carry.out_shapes), scratch_shapes=list(scratch_shapes) + list(carry.sems),
        compiler_params=_cp(("arbitrary",) * len(grid)), name=name)(*ops, *carry.inputs)
    main = res[:n_out]
    return (list(main) if multi else main[0]), list(res[n_out:])


def norm_matmul(x, xcol, kdim, gain, w, tm, tn, out_dtype, name, epi=None, epi_ops=(), epi_specs=(), carry=None):
    rows, n = x.shape[0], mat_cols(w) if w.ndim == 3 else w.shape[1]
    n_epi = len(epi_ops)
    w_spec = (pl.BlockSpec((kdim, tn), lambda i, j: (0, j)) if w.ndim == 2 else
              pl.BlockSpec((None, kdim, tn), lambda i, j: (j // (w.shape[2] // tn), 0, j % (w.shape[2] // tn))))

    def body(x_ref, g_ref, w_ref, *rest):
        epi_refs = rest[:n_epi]
        out_ref, xn_ref, xn_sc = rest[n_epi:]

        @pl.when(pl.program_id(1) == 0)
        def _():
            xv = x_ref[...]
            y = xv * lax.rsqrt(jnp.mean(xv * xv, axis=-1, keepdims=True) + EPS)
            xn = (y * g_ref[...]).astype(MXU_DT)
            xn_sc[...] = xn
            xn_ref[...] = xn

        acc = jnp.dot(xn_sc[...], w_ref[...], preferred_element_type=F32)
        if epi is not None:
            acc = epi(acc, *[r[...] for r in epi_refs])
        out_ref[...] = acc.astype(out_dtype)

    return _call(
        body, (x, gain, w, *epi_ops), carry, grid=(rows // tm, n // tn),
        in_specs=[pl.BlockSpec((tm, kdim), lambda i, j: (i, xcol)), pl.BlockSpec((1, kdim), lambda i, j: (0, 0)),
                  w_spec, *epi_specs],
        out_specs=[pl.BlockSpec((tm, tn), lambda i, j: (i, j)), pl.BlockSpec((tm, kdim), lambda i, j: (i, 0))],
        out_shape=[S((rows, n), out_dtype), S((rows, kdim), MXU_DT)],
        scratch_shapes=[pltpu.VMEM((tm, kdim), MXU_DT)], sem=("parallel", "arbitrary"), name=name)


def matmul_res(a, w, res, tm, tn, name):
    grp, rows, k = a.shape
    n = w.shape[2]

    def body(a_ref, w_ref, r_ref, o_ref):
        acc = r_ref[...]
        for g in range(grp):
            acc = acc + jnp.dot(a_ref[g], w_ref[g], preferred_element_type=F32)
        o_ref[...] = acc

    return pl.pallas_call(
        body, grid=(rows // tm, n // tn),
        in_specs=[pl.BlockSpec((grp, tm, k), lambda i, j: (0, i, 0)), pl.BlockSpec((grp, k, tn), lambda i, j: (0, 0, j)),
                  pl.BlockSpec((tm, tn), lambda i, j: (i, j))],
        out_specs=pl.BlockSpec((tm, tn), lambda i, j: (i, j)),
        out_shape=S((rows, n), F32), compiler_params=_cp(("parallel", "parallel")), name=name)(a, w, res)


def matmul_nt(a, w, tm, tn, out_dtype, name):
    rows, k = a.shape
    n = w.shape[0]

    def body(a_ref, w_ref, o_ref):
        o_ref[...] = lax.dot_general(a_ref[...].astype(MXU_DT), w_ref[...], NT_DIMS,
                                     preferred_element_type=F32).astype(out_dtype)

    return pl.pallas_call(
        body, grid=(rows // tm, n // tn),
        in_specs=[pl.BlockSpec((tm, k), lambda i, j: (i, 0)), pl.BlockSpec((tn, k), lambda i, j: (j, 0))],
        out_specs=pl.BlockSpec((tm, tn), lambda i, j: (i, j)),
        out_shape=S((rows, n), out_dtype), compiler_params=_cp(("parallel", "parallel")), name=name)(a, w)


def matmul_nt_normbwd(du, w, x, xcol, gain, res, tm, tk, out_dtype, name):
    rows, kc = du.shape[-2], mat_cols(du)
    dn = w.shape[-2]
    nk = kc // tk
    has_res = res is not None
    w_spec = (pl.BlockSpec((dn, tk), lambda i, k: (0, k)) if w.ndim == 2 else
              pl.BlockSpec((None, dn, tk), lambda i, k: (k // (w.shape[2] // tk), 0, k % (w.shape[2] // tk))))

    def body(du_ref, w_ref, x_ref, g_ref, *rest):
        if has_res:
            res_ref, dx_ref, dg_ref, acc = rest
        else:
            dx_ref, dg_ref, acc = rest
        i, k = pl.program_id(0), pl.program_id(1)

        @pl.when(k == 0)
        def _():
            acc[...] = jnp.zeros_like(acc)

        @pl.when((i == 0) & (k == 0))
        def _():
            dg_ref[...] = jnp.zeros_like(dg_ref)

        acc[...] += lax.dot_general(du_ref[...], w_ref[...], NT_DIMS, preferred_element_type=F32)

        @pl.when(k == nk - 1)
        def _():
            dhn = acc[...]
            xv = x_ref[...]
            rstd = lax.rsqrt(jnp.mean(xv * xv, axis=-1, keepdims=True) + EPS)
            xhat = xv * rstd
            dg_ref[...] += jnp.sum(dhn * xhat, axis=0, keepdims=True)
            dxh = dhn * g_ref[...]
            dx = rstd * (dxh - xhat * jnp.mean(dxh * xhat, axis=-1, keepdims=True))
            if has_res:
                dx = dx + res_ref[...]
            dx_ref[...] = dx.astype(out_dtype)

    in_specs = [mat_spec(du, tm, tk, lambda i, k: (i, k)), w_spec,
                pl.BlockSpec((tm, dn), lambda i, k: (i, xcol)), pl.BlockSpec((1, dn), lambda i, k: (0, 0))]
    ops = [du, w, x, gain]
    if has_res:
        in_specs.append(pl.BlockSpec((tm, dn), lambda i, k: (i, 0)))
        ops.append(res)
    return pl.pallas_call(
        body, grid=(rows // tm, nk), in_specs=in_specs,
        out_specs=[pl.BlockSpec((tm, dn), lambda i, k: (i, 0)), pl.BlockSpec((1, dn), lambda i, k: (0, 0))],
        out_shape=[S((rows, dn), out_dtype), S((1, dn), F32)],
        scratch_shapes=[pltpu.VMEM((tm, dn), F32)],
        compiler_params=_cp(("arbitrary", "arbitrary")), name=name)(*ops)


def matmul_tn(a, b, tr, name, carry=None, col_shards=1):
    rows, ka, nb = a.shape[-2], mat_cols(a), mat_cols(b)
    ta = _div_tile(mat_width(a), 1536, LANE)
    tb = _div_tile(mat_width(b), 1536 if ta <= 1024 else 1024, LANE)
    nr = rows // tr
    if col_shards == 1:
        out_spec, out_shape = pl.BlockSpec((ta, tb), lambda i, j, r: (i, j)), S((ka, nb), F32)
    else:
        per = nb // col_shards // tb
        assert per * tb * col_shards == nb
        out_spec = pl.BlockSpec((None, ta, tb), lambda i, j, r: (j // per, i, j % per))
        out_shape = S((col_shards, ka, nb // col_shards), F32)

    def body(a_ref, b_ref, o_ref, acc):
        r = pl.program_id(2)

        @pl.when(r == 0)
        def _():
            acc[...] = jnp.zeros_like(acc)

        acc[...] += lax.dot_general(a_ref[...].astype(MXU_DT), b_ref[...].astype(MXU_DT), TN_DIMS,
                                    preferred_element_type=F32)

        @pl.when(r == nr - 1)
        def _():
            o_ref[...] = acc[...]

    return _call(
        body, (a, b), carry, grid=(ka // ta, nb // tb, nr),
        in_specs=[mat_spec(a, tr, ta, lambda i, j, r: (r, i)), mat_spec(b, tr, tb, lambda i, j, r: (r, j))],
        out_specs=out_spec, out_shape=out_shape, scratch_shapes=[pltpu.VMEM((ta, tb), F32)],
        sem=("parallel", "parallel", "arbitrary"), name=name)


def _sigmoid(x):
    return 1.0 / (1.0 + jnp.exp(-x))


def _sigmoid_by_tanh(x):
    return 0.5 * jnp.tanh(0.5 * x) + 0.5


def _log1p(e):
    return jnp.where(e < 1e-3, e * (1.0 - e * (0.5 - e * (1.0 / 3.0 - 0.25 * e))), jnp.log(1.0 + e))


def _softplus(x):
    return jnp.maximum(x, 0.0) + _log1p(jnp.exp(-jnp.abs(x)))


def _expm1(x):
    series = x * (1.0 + x * (0.5 + x * (1.0 / 6.0 + x * (1.0 / 24.0 + x * (1.0 / 120.0)))))
    return jnp.where(jnp.abs(x) < 0.1, series, jnp.exp(x) - 1.0)


_GELU_K = math.sqrt(2.0 / math.pi)
_GELU_C = 0.044715


def _gelu_and_grad(x):
    th = jnp.tanh(_GELU_K * (x + _GELU_C * x * x * x))
    g = 0.5 * x * (1.0 + th)
    dg = 0.5 * (1.0 + th) + 0.5 * x * (1.0 - th * th) * _GELU_K * (1.0 + 3.0 * _GELU_C * x * x)
    return g, dg


def _row_iota(shape):
    return lax.broadcasted_iota(jnp.int32, shape, 0)


def _scan_chunk_fwd(a_sc, u_sc, out_ref, hcar, n, width):
    rowi = _row_iota((SUBLANE, width))

    def step(c, hprev):
        r0 = pl.multiple_of(c * SUBLANE, SUBLANE)
        a = a_sc[pl.ds(r0, SUBLANE), :]
        u = u_sc[pl.ds(r0, SUBLANE), :]
        for d in (1, 2, 4):
            a_s = jnp.where(rowi >= d, pltpu.roll(a, d, axis=0), 1.0)
            u_s = jnp.where(rowi >= d, pltpu.roll(u, d, axis=0), 0.0)
            u = u + a * u_s
            a = a * a_s
        h = u + a * hprev
        out_ref[pl.ds(r0, SUBLANE), :] = h
        return jnp.broadcast_to(h[SUBLANE - 1:SUBLANE, :], (SUBLANE, width))

    hcar[...] = lax.fori_loop(0, n // SUBLANE, step, hcar[...], unroll=4)


def _scan_chunk_bwd(b_sc, d_sc, out_ref, gcar, n, width):
    rowi = _row_iota((SUBLANE, width))
    nc = n // SUBLANE

    def step(c, gnext):
        r0 = pl.multiple_of((nc - 1 - c) * SUBLANE, SUBLANE)
        b = b_sc[pl.ds(r0, SUBLANE), :]
        d = d_sc[pl.ds(r0, SUBLANE), :]
        for s in (1, 2, 4):
            keep = rowi < SUBLANE - s
            b_s = jnp.where(keep, pltpu.roll(b, SUBLANE - s, axis=0), 1.0)
            d_s = jnp.where(keep, pltpu.roll(d, SUBLANE - s, axis=0), 0.0)
            d = d + b * d_s
            b = b * b_s
        g = d + b * gnext
        out_ref[pl.ds(r0, SUBLANE), :] = g
        return jnp.broadcast_to(g[0:1, :], (SUBLANE, width))

    gcar[...] = lax.fori_loop(0, nc, step, gcar[...], unroll=4)


def even_mid_fwd(u, conv_a, conv_b, conv_b_bias, rw, rb, iw, ib, lam, nb, tp, n, name, carry=None):
    rows = u.shape[0]
    w = EV_CT
    nj = CONV_W // w
    nt = tp // n
    h8 = SUBLANE

    def body(gb_r, gc_r, xa_r, xb_r, gate_r, ca_w, cb_w, cb_b, rw_r, rb_r, iw_r, ib_r, lam_r,
             y_o, ca_o, xc_o, a_o, hs_o, pext, xext, hcar, a_sc, u_sc):
        t = pl.program_id(2)

        @pl.when(t == 0)
        def _():
            pext[0:h8, :] = jnp.zeros((h8, w), F32)
            xext[0:h8, :] = jnp.zeros((h8, w), F32)
            hcar[...] = jnp.zeros_like(hcar)

        p = gc_r[...] * xa_r[...]
        pext[h8:h8 + n, :] = p
        wa = ca_w[...]
        ca = wa[2:3, :] * p + wa[1:2, :] * pext[h8 - 1:h8 - 1 + n, :] + wa[0:1, :] * pext[h8 - 2:h8 - 2 + n, :]
        ca_o[...] = ca
        y_o[0] = (gb_r[...] * ca).astype(MXU_DT)
        pext[0:h8, :] = pext[n:n + h8, :]

        xb = xb_r[...]
        xext[h8:h8 + n, :] = xb
        wb = cb_w[...]
        xc = (wb[3:4, :] * xb + wb[2:3, :] * xext[h8 - 1:h8 - 1 + n, :] + wb[1:2, :] * xext[h8 - 2:h8 - 2 + n, :]
              + wb[0:1, :] * xext[h8 - 3:h8 - 3 + n, :]) + cb_b[...]
        xc_o[...] = xc
        xext[0:h8, :] = xext[n:n + h8, :]

        xcm = xc.astype(MXU_DT)
        r = _sigmoid(jnp.dot(xcm, rw_r[...], preferred_element_type=F32) + rb_r[...])
        ig = _sigmoid(jnp.dot(xcm, iw_r[...], preferred_element_type=F32) + ib_r[...])
        log_a = (-LRU_C) * r * _softplus(-lam_r[...])
        a = jnp.exp(log_a)
        mult = jnp.sqrt(-_expm1(2.0 * log_a))
        a_sc[...] = a
        a_o[...] = a
        u_sc[...] = mult * (ig * xc)
        _scan_chunk_fwd(a_sc, u_sc, hs_o, hcar, n, w)
        gel, _ = _gelu_and_grad(gate_r[...])
        y_o[1] = (gel * hs_o[...]).astype(MXU_DT)

    def ublk(off):
        return pl.BlockSpec((n, w), lambda j, b, t: (b * nt + t, off + j))

    def pblk(r_):
        return pl.BlockSpec((r_, w), lambda j, b, t: (0, j))

    act = pl.BlockSpec((n, w), lambda j, b, t: (b * nt + t, j))
    mat = pl.BlockSpec((w, w), lambda j, b, t: (j, j))
    return _call(
        body, (u, u, u, u, u, conv_a, conv_b, conv_b_bias, rw, rb, iw, ib, lam), carry, grid=(nj, nb, nt),
        in_specs=[ublk(0), ublk(nj), ublk(2 * nj), ublk(3 * nj), ublk(4 * nj), pblk(3), pblk(4), pblk(1),
                  mat, pblk(1), mat, pblk(1), pblk(1)],
        out_specs=[pl.BlockSpec((2, n, w), lambda j, b, t: (0, b * nt + t, j)), act, act, act, act],
        out_shape=[S((2, rows, CONV_W), MXU_DT), S((rows, CONV_W), F32), S((rows, LRU_W), F32), S((rows, LRU_W), F32),
                   S((rows, LRU_W), F32)],
        scratch_shapes=[pltpu.VMEM((n + h8, w), F32), pltpu.VMEM((n + h8, w), F32), pltpu.VMEM((h8, w), F32),
                        pltpu.VMEM((n, w), F32), pltpu.VMEM((n, w), F32)],
        sem=("parallel", "parallel", "arbitrary"), name=name)


def even_mid_bwd(u, dycat, ca, xc, a_sv, hs, conv_a, conv_b, rw, rb, iw, ib, lam, nb, tp, n, name, carry=None):
    rows = u.shape[0]
    w = EV_CT
    nj = CONV_W // w
    nt = tp // n
    h8 = SUBLANE

    def body(gb_r, gc_r, xa_r, xb_r, gate_r, dya_r, dyb_r, ca_r, xc_r, a_r, hs_r, hsp_r,
             ca_w, cb_w, rw_r, rb_r, iw_r, ib_r, lam_r,
             du_o, dca_w, dcb_w, dcb_b, drw, drb, diw, dib, dlam,
             aext, hext, dext, eext, gcar, b_sc, d_sc, g_sc):
        b, t = pl.program_id(1), pl.program_id(2)

        @pl.when((b == 0) & (t == 0))
        def _():
            for ref in (dca_w, dcb_w, dcb_b, drw, drb, diw, dib, dlam):
                ref[...] = jnp.zeros_like(ref)

        @pl.when(t == 0)
        def _():
            aext[n:n + h8, :] = jnp.zeros((h8, w), F32)
            dext[n:n + h8, :] = jnp.zeros((h8, w), F32)
            eext[n:n + h8, :] = jnp.zeros((h8, w), F32)
            gcar[...] = jnp.zeros_like(gcar)

        xc_v = xc_r[...]
        xcm = xc_v.astype(MXU_DT)
        r = _sigmoid(jnp.dot(xcm, rw_r[...], preferred_element_type=F32) + rb_r[...])
        ig = _sigmoid(jnp.dot(xcm, iw_r[...], preferred_element_type=F32) + ib_r[...])
        lam_v = lam_r[...]
        sp = _softplus(-lam_v)
        log_a = (-LRU_C) * r * sp
        a = a_r[...]
        mult = jnp.sqrt(-_expm1(2.0 * log_a))
        hs_v = hs_r[...]
        gel, dgel = _gelu_and_grad(gate_r[...])
        dyb = dyb_r[...]
        du_o[4] = (dyb * hs_v * dgel).astype(MXU_DT)

        aext[0:n, :] = a
        b_sc[...] = aext[1:1 + n, :]
        d_sc[...] = dyb * gel
        _scan_chunk_bwd(b_sc, d_sc, g_sc, gcar, n, w)
        aext[n:n + h8, :] = aext[0:h8, :]
        g = g_sc[...]

        hext[0:h8, :] = jnp.where(t == nt - 1, 0.0, hsp_r[...])
        hext[h8:h8 + n, :] = hs_v
        da = g * hext[h8 - 1:h8 - 1 + n, :]
        dmult = g * (ig * xc_v)
        di = g * mult * xc_v
        dxc = g * mult * ig
        dlog_a = da * a - dmult * (a * a) / mult
        dr = dlog_a * ((-LRU_C) * sp)
        dsp = jnp.sum(dlog_a * ((-LRU_C) * r), axis=0, keepdims=True)
        dlam[...] += dsp * (-_sigmoid(-lam_v))
        dzr = dr * r * (1.0 - r)
        dzi = di * ig * (1.0 - ig)
        dzr_m = dzr.astype(MXU_DT)
        dzi_m = dzi.astype(MXU_DT)
        dxc = (dxc + lax.dot_general(dzr_m, rw_r[...], NT_DIMS, preferred_element_type=F32)
               + lax.dot_general(dzi_m, iw_r[...], NT_DIMS, preferred_element_type=F32))
        drw[...] += lax.dot_general(xcm, dzr_m, TN_DIMS, preferred_element_type=F32)
        diw[...] += lax.dot_general(xcm, dzi_m, TN_DIMS, preferred_element_type=F32)
        drb[...] += jnp.sum(dzr, axis=0, keepdims=True)
        dib[...] += jnp.sum(dzi, axis=0, keepdims=True)
        dcb_b[...] += jnp.sum(dxc, axis=0, keepdims=True)

        xb = xb_r[...]
        dext[0:n, :] = dxc
        wb = cb_w[...]
        d1, d2, d3 = dext[1:1 + n, :], dext[2:2 + n, :], dext[3:3 + n, :]
        du_o[3] = (wb[3:4, :] * dxc + wb[2:3, :] * d1 + wb[1:2, :] * d2 + wb[0:1, :] * d3).astype(MXU_DT)
        dcb_w[3:4, :] += jnp.sum(xb * dxc, axis=0, keepdims=True)
        dcb_w[2:3, :] += jnp.sum(xb * d1, axis=0, keepdims=True)
        dcb_w[1:2, :] += jnp.sum(xb * d2, axis=0, keepdims=True)
        dcb_w[0:1, :] += jnp.sum(xb * d3, axis=0, keepdims=True)
        dext[n:n + h8, :] = dext[0:h8, :]

        gb, gc, xa = gb_r[...], gc_r[...], xa_r[...]
        dya = dya_r[...]
        du_o[0] = (dya * ca_r[...]).astype(MXU_DT)
        dca = dya * gb
        eext[0:n, :] = dca
        wa = ca_w[...]
        e1, e2 = eext[1:1 + n, :], eext[2:2 + n, :]
        dp = wa[2:3, :] * dca + wa[1:2, :] * e1 + wa[0:1, :] * e2
        p = gc * xa
        dca_w[2:3, :] += jnp.sum(p * dca, axis=0, keepdims=True)
        dca_w[1:2, :] += jnp.sum(p * e1, axis=0, keepdims=True)
        dca_w[0:1, :] += jnp.sum(p * e2, axis=0, keepdims=True)
        eext[n:n + h8, :] = eext[0:h8, :]
        du_o[1] = (dp * xa).astype(MXU_DT)
        du_o[2] = (dp * gc).astype(MXU_DT)

    def rt(b, t):
        return b * nt + (nt - 1 - t)

    def ublk(off):
        return pl.BlockSpec((n, w), lambda j, b, t: (rt(b, t), off + j))

    def pblk(r_):
        return pl.BlockSpec((r_, w), lambda j, b, t: (0, j))

    act = pl.BlockSpec((n, w), lambda j, b, t: (rt(b, t), j))
    n8 = n // h8
    hsp = pl.BlockSpec((h8, w), lambda j, b, t: (jnp.maximum(rt(b, t) * n8 - 1, 0), j))
    mat = pl.BlockSpec((w, w), lambda j, b, t: (j, j))
    return _call(
        body, (u, u, u, u, u, dycat, dycat, ca, xc, a_sv, hs, hs, conv_a, conv_b, rw, rb, iw, ib, lam), carry,
        grid=(nj, nb, nt),
        in_specs=[ublk(0), ublk(nj), ublk(2 * nj), ublk(3 * nj), ublk(4 * nj), ublk(0), ublk(nj), act, act, act, act,
                  hsp, pblk(3), pblk(4), mat, pblk(1), mat, pblk(1), pblk(1)],
        out_specs=[pl.BlockSpec((5, n, w), lambda j, b, t: (0, rt(b, t), j)), pblk(3), pblk(4), pblk(1),
                   mat, pblk(1), mat, pblk(1), pblk(1)],
        out_shape=[S((5, rows, CONV_W), MXU_DT), S((3, CONV_W), F32), S((4, LRU_W), F32), S((1, LRU_W), F32),
                   S((LRU_W, LRU_W), F32), S((1, LRU_W), F32), S((LRU_W, LRU_W), F32), S((1, LRU_W), F32),
                   S((1, LRU_W), F32)],
        scratch_shapes=[pltpu.VMEM((n + h8, w), F32)] * 4 + [pltpu.VMEM((h8, w), F32)] + [pltpu.VMEM((n, w), F32)] * 3,
        sem=("arbitrary", "arbitrary", "arbitrary"), name=name)


def ffn_mid_fwd(up, cw, cb, nb, tp, n, name):
    rows = up.shape[0]
    w = FFN_CT
    nj = D_FF // w
    nt = tp // n
    h8 = SUBLANE

    sr = STRIP_ROWS
    assert n % sr == 0, (n, sr)

    def body(xa_r, xg_r, w_r, b_r, u_o, y_o, halo):
        t = pl.program_id(2)

        @pl.when(t == 0)
        def _():
            halo[...] = jnp.zeros_like(halo)

        wv = (w_r[0], w_r[1])
        bv = (b_r[0], b_r[1])

        def strip(s, carry):
            r0 = pl.multiple_of(s * sr, sr)
            us, new = [], []
            for g, x_r in enumerate((xa_r, xg_r)):
                x = x_r[pl.ds(r0, sr), :].astype(F32)
                win = jnp.concatenate([carry[g], x], axis=0)
                x1 = pltpu.roll(win, 1, axis=0)[h8:, :]
                x2 = pltpu.roll(win, 2, axis=0)[h8:, :]
                u = (wv[g][2:3, :] * x + wv[g][1:2, :] * x1 + wv[g][0:1, :] * x2) + bv[g]
                u_o[g, pl.ds(r0, sr), :] = u.astype(MXU_DT)
                us.append(u)
                new.append(x[sr - h8:, :])
            y_o[pl.ds(r0, sr), :] = (us[0] * _sigmoid_by_tanh(us[0]) * us[1]).astype(MXU_DT)
            return tuple(new)

        ha, hg = lax.fori_loop(0, n // sr, strip, (halo[0], halo[1]))
        halo[0] = ha
        halo[1] = hg

    def ublk(off):
        return pl.BlockSpec((n, w), lambda j, b, t: (b * nt + t, off + j))

    return pl.pallas_call(
        body, grid=(nj, nb, nt),
        in_specs=[ublk(0), ublk(nj), pl.BlockSpec((2, 3, w), lambda j, b, t: (0, 0, j)),
                  pl.BlockSpec((2, 1, w), lambda j, b, t: (0, 0, j))],
        out_specs=[pl.BlockSpec((2, n, w), lambda j, b, t: (0, b * nt + t, j)), ublk(0)],
        out_shape=[S((2, rows, D_FF), MXU_DT), S((rows, D_FF), MXU_DT)],
        scratch_shapes=[pltpu.VMEM((2, h8, w), F32)],
        compiler_params=_cp(("parallel", "parallel", "arbitrary")), name=name,
    )(up, up, cw, cb)


def ffn_mid_bwd(dy, u, up, cw, nb, tp, n, name, carry=None):
    rows = up.shape[0]
    w = FFN_CT
    nj = D_FF // w
    nt = tp // n
    h8 = SUBLANE

    sr = STRIP_ROWS
    assert n % sr == 0, (n, sr)
    ns = n // sr

    def fold(v):
        acc = v[0:h8, :]
        for k in range(1, sr // h8):
            acc = acc + v[k * h8:(k + 1) * h8, :]
        return acc

    def body(dy_r, u_r, xa_r, xg_r, w_r, dx_o, dw, db, halo):
        b, t = pl.program_id(1), pl.program_id(2)

        @pl.when((b == 0) & (t == 0))
        def _():
            dw[...] = jnp.zeros_like(dw)
            db[...] = jnp.zeros_like(db)

        @pl.when(t == 0)
        def _():
            halo[...] = jnp.zeros_like(halo)

        wv = (w_r[0], w_r[1])

        def strip(s, carry):
            halos, sums = carry
            r0 = pl.multiple_of((ns - 1 - s) * sr, sr)
            dyv = dy_r[pl.ds(r0, sr), :].astype(F32)
            ua = u_r[0, pl.ds(r0, sr), :].astype(F32)
            ug = u_r[1, pl.ds(r0, sr), :].astype(F32)
            sg = _sigmoid_by_tanh(ua)
            dus = (dyv * ug * (sg * (1.0 + ua * (1.0 - sg))), dyv * (ua * sg))
            new_halos, new_sums = [], []
            for g, x_r in enumerate((xa_r, xg_r)):
                du = dus[g]
                win = jnp.concatenate([du, halos[g]], axis=0)
                d1 = pltpu.roll(win, sr + h8 - 1, axis=0)[0:sr, :]
                d2 = pltpu.roll(win, sr + h8 - 2, axis=0)[0:sr, :]
                dx_o[g, pl.ds(r0, sr), :] = (wv[g][2:3, :] * du + wv[g][1:2, :] * d1 + wv[g][0:1, :] * d2).astype(MXU_DT)
                x = x_r[pl.ds(r0, sr), :].astype(F32)
                s2, s1, s0, sb = sums[g]
                new_sums.append((s2 + fold(x * du), s1 + fold(x * d1), s0 + fold(x * d2), sb + fold(du)))
                new_halos.append(du[0:h8, :])
            return tuple(new_halos), tuple(new_sums)

        z = jnp.zeros((h8, w), F32)
        halos, sums = lax.fori_loop(0, ns, strip, ((halo[0], halo[1]), ((z, z, z, z), (z, z, z, z))))
        halo[0] = halos[0]
        halo[1] = halos[1]
        for g in range(2):
            s2, s1, s0, sb = sums[g]
            dw[g, 2:3, :] += jnp.sum(s2, axis=0, keepdims=True)
            dw[g, 1:2, :] += jnp.sum(s1, axis=0, keepdims=True)
            dw[g, 0:1, :] += jnp.sum(s0, axis=0, keepdims=True)
            db[g] += jnp.sum(sb, axis=0, keepdims=True)

    def rt(b, t):
        return b * nt + (nt - 1 - t)

    def ublk(off):
        return pl.BlockSpec((n, w), lambda j, b, t: (rt(b, t), off + j))

    pair = pl.BlockSpec((2, n, w), lambda j, b, t: (0, rt(b, t), j))
    return _call(
        body, (dy, u, up, up, cw), carry, grid=(nj, nb, nt),
        in_specs=[ublk(0), pair, ublk(0), ublk(nj), pl.BlockSpec((2, 3, w), lambda j, b, t: (0, 0, j))],
        out_specs=[pair, pl.BlockSpec((2, 3, w), lambda j, b, t: (0, 0, j)),
                   pl.BlockSpec((2, 1, w), lambda j, b, t: (0, 0, j))],
        out_shape=[S((2, rows, D_FF), MXU_DT), S((2, 3, D_FF), F32), S((2, 1, D_FF), F32)],
        scratch_shapes=[pltpu.VMEM((2, h8, w), F32)],
        sem=("arbitrary", "arbitrary", "arbitrary"), name=name)


def _lane_mod(shape):
    return lax.broadcasted_iota(jnp.int32, shape, 1) & (HP - 1)


def _q_rope_epi(acc, tab):
    reps = acc.shape[1] // HP
    a = acc * jnp.tile(tab, (1, reps))
    lane = _lane_mod(a.shape)
    shifted = pltpu.roll(a, a.shape[1] - QK_ROPE, axis=1)
    return jnp.where(lane < QK_NOPE, a, jnp.where(lane < QK_HEAD, a + shifted, 0.0)) * Q_PRESCALE


def _k_rope_block(krblk, tabk):
    a = krblk * tabk
    lane = _lane_mod(a.shape)
    b = a + pltpu.roll(a, HP - QK_ROPE, axis=1)
    return jnp.where((lane >= QK_NOPE) & (lane < QK_HEAD), b, 0.0)


def _k_rope_epi(acc, krblk, tabk):
    reps = acc.shape[1] // HP
    return acc + jnp.tile(_k_rope_block(krblk, tabk), (1, reps))


def attn_fwd(q, k, v, nb, tp, name, carry=None):
    rows = q.shape[0]
    blk = ATT_BLK
    nq = tp // blk
    npair = MLA_HEADS // 2

    def body(q_r, k_r, v_r, o_r, lse_r):
        qi = pl.program_id(2)
        lane = lax.broadcasted_iota(jnp.int32, (blk, LANE), 1)
        even = lane < V_HEAD
        sum_lane = (V_HEAD, 0)
        rowi = lax.broadcasted_iota(jnp.int32, (blk, blk), 0)
        coli = lax.broadcasted_iota(jnp.int32, (blk, blk), 1)
        qs = [q_r[:, h * HP:(h + 1) * HP] for h in range(2)]

        def kv_block(k0, width, carry, visible):
            ms, accs = carry
            vblk = v_r[pl.ds(k0, width), :]
            one = jnp.ones_like(vblk)
            zero = jnp.zeros_like(vblk)
            vlane = lax.broadcasted_iota(jnp.int32, (width, LANE), 1)
            ss = [lax.dot_general(qs[h], k_r[pl.ds(k0, width), h * HP:(h + 1) * HP], NT_DIMS,
                                  preferred_element_type=F32) for h in range(2)]
            new_ms, new_accs = [], []
            for h in range(2):
                s = ss[h]
                if visible is not None:
                    s = jnp.where(visible, s, -jnp.inf)
                m_new = jnp.maximum(ms[h], jnp.max(s, axis=1, keepdims=True))
                alpha = jnp.exp2(ms[h] - m_new)
                p = jnp.exp2(s - m_new).astype(MXU_DT)
                mine = (vlane < V_HEAD) if h == 0 else (vlane >= V_HEAD)
                vh = jnp.where(mine, vblk, jnp.where(vlane == sum_lane[h], one, zero))
                new_accs.append(alpha * accs[h] + jnp.dot(p, vh, preferred_element_type=F32))
                new_ms.append(m_new)
            return tuple(new_ms), tuple(new_accs)

        neg = jnp.full((blk, 1), -jnp.inf, F32)
        zacc = jnp.zeros((blk, LANE), F32)
        carry = lax.fori_loop(0, qi // 2, lambda i, c: kv_block(pl.multiple_of(i * 2 * blk, blk), 2 * blk, c, None),
                              ((neg, neg), (zacc, zacc)))
        rowi2 = lax.broadcasted_iota(jnp.int32, (blk, 2 * blk), 0)
        coli2 = lax.broadcasted_iota(jnp.int32, (blk, 2 * blk), 1)
        ms, accs = lax.cond(
            qi % 2 == 1,
            lambda c: kv_block(pl.multiple_of((qi - 1) * blk, blk), 2 * blk, c, coli2 - blk <= rowi2),
            lambda c: kv_block(pl.multiple_of(qi * blk, blk), blk, c, coli <= rowi), carry)
        ls = [accs[h][:, sum_lane[h]:sum_lane[h] + 1] for h in range(2)]
        o_r[...] = jnp.where(even, accs[0] / ls[0], accs[1] / ls[1]).astype(MXU_DT)
        lse_r[...] = jnp.where(even, ms[0] + jnp.log2(ls[0]), ms[1] + jnp.log2(ls[1]))

    return _call(
        body, (q, k, v), carry, grid=(nb, npair, nq),
        in_specs=[pl.BlockSpec((blk, 2 * HP), lambda b, p, i: (b * nq + i, p)),
                  pl.BlockSpec((tp, 2 * HP), lambda b, p, i: (b, p)),
                  pl.BlockSpec((tp, LANE), lambda b, p, i: (b, p))],
        out_specs=[pl.BlockSpec((blk, LANE), lambda b, p, i: (b * nq + i, p)),
                   pl.BlockSpec((None, blk, LANE), lambda b, p, i: (p, b * nq + i, 0))],
        out_shape=[S((rows, MLA_HEADS * V_HEAD), MXU_DT), S((npair, rows, LANE), F32)], scratch_shapes=[],
        sem=("parallel", "parallel", "arbitrary"), name=name)


def attn_bwd(q, k, v, o, do, lse, nb, tp, name, carry=None):
    rows = q.shape[0]
    blk = ATT_BLK
    nq = tp // blk
    npair = MLA_HEADS // 2
    scale = QK_HEAD ** -0.5

    def body(q_r, k_r, v_r, o_r, do_r, lse_r, dq_o, dk_o, dv_o, dq_acc, delta_sc):
        kb = pl.program_id(2)
        even = lax.broadcasted_iota(jnp.int32, (blk, LANE), 1) < V_HEAD
        rowi = lax.broadcasted_iota(jnp.int32, (blk, blk), 0)
        coli = lax.broadcasted_iota(jnp.int32, (blk, blk), 1)

        @pl.when(kb == 0)
        def _():
            dq_acc[...] = jnp.zeros_like(dq_acc)

            def dstep(i, c):
                r0 = pl.multiple_of(i * blk, blk)
                prod = do_r[pl.ds(r0, blk), :].astype(F32) * o_r[pl.ds(r0, blk), :].astype(F32)
                de = jnp.sum(jnp.where(even, prod, 0.0), axis=1, keepdims=True)
                dd = jnp.sum(jnp.where(even, 0.0, prod), axis=1, keepdims=True)
                delta_sc[pl.ds(r0, blk), :] = jnp.where(even, de, dd)
                return c

            lax.fori_loop(0, nq, dstep, 0)

        vblk = v_r[...]
        ks = [k_r[:, h * HP:(h + 1) * HP] for h in range(2)]

        def q_block(r0, height, carry, visible):
            dk0, dk1, dv = carry
            dob = do_r[pl.ds(r0, height), :]
            lse_b = lse_r[pl.ds(r0, height), :]
            dl_b = delta_sc[pl.ds(r0, height), :]
            qlane = lax.broadcasted_iota(jnp.int32, (height, LANE), 1)
            dks = [dk0, dk1]
            qhs = [q_r[pl.ds(r0, height), h * HP:(h + 1) * HP] for h in range(2)]
            dohs = [jnp.where((qlane < V_HEAD) if h == 0 else (qlane >= V_HEAD), dob, jnp.zeros_like(dob))
                    for h in range(2)]
            ss = [lax.dot_general(qhs[h], ks[h], NT_DIMS, preferred_element_type=F32) for h in range(2)]
            dps = [lax.dot_general(dohs[h], vblk, NT_DIMS, preferred_element_type=F32) for h in range(2)]
            for h in range(2):
                lo = 0 if h == 0 else V_HEAD
                p = jnp.exp2(ss[h] - lse_b[:, lo:lo + 1])
                if visible is not None:
                    p = jnp.where(visible, p, 0.0)
                ds = (p * (dps[h] - dl_b[:, lo:lo + 1])).astype(MXU_DT)
                dv = dv + lax.dot_general(p.astype(MXU_DT), dohs[h], TN_DIMS, preferred_element_type=F32)
                dks[h] = dks[h] + lax.dot_general(ds, qhs[h], TN_DIMS, preferred_element_type=F32)
                dq_acc[pl.ds(r0, height), h * HP:(h + 1) * HP] += jnp.dot(ds, ks[h], preferred_element_type=F32)
            return dks[0], dks[1], dv

        z = jnp.zeros((blk, HP), F32)
        below = nq - 1 - kb
        odd = below % 2
        rowi2 = lax.broadcasted_iota(jnp.int32, (2 * blk, blk), 0)
        coli2 = lax.broadcasted_iota(jnp.int32, (2 * blk, blk), 1)
        first = pl.multiple_of(kb * blk, blk)
        carry = lax.cond(odd == 1, lambda c: q_block(first, 2 * blk, c, coli2 <= rowi2),
                         lambda c: q_block(first, blk, c, coli <= rowi), (z, z, jnp.zeros((blk, LANE), F32)))
        dk0, dk1, dv = lax.fori_loop(
            0, below // 2, lambda i, c: q_block(pl.multiple_of((kb + 1 + odd + 2 * i) * blk, blk), 2 * blk, c, None),
            carry)
        dk_o[:, 0:HP] = (dk0 * (scale / Q_PRESCALE)).astype(MXU_DT)
        dk_o[:, HP:2 * HP] = (dk1 * (scale / Q_PRESCALE)).astype(MXU_DT)
        dv_o[...] = dv.astype(MXU_DT)

        @pl.when(kb == nq - 1)
        def _():
            dq_o[...] = (dq_acc[...] * scale).astype(MXU_DT)

    seq_pair = pl.BlockSpec((tp, LANE), lambda b, p, kk: (b, p))
    return _call(
        body, (q, k, v, o, do, lse), carry, grid=(nb, npair, nq),
        in_specs=[pl.BlockSpec((tp, 2 * HP), lambda b, p, kk: (b, p)),
                  pl.BlockSpec((blk, 2 * HP), lambda b, p, kk: (b * nq + kk, p)),
                  pl.BlockSpec((blk, LANE), lambda b, p, kk: (b * nq + kk, p)),
                  seq_pair, seq_pair, pl.BlockSpec((None, tp, LANE), lambda b, p, kk: (p, b, 0))],
        out_specs=[pl.BlockSpec((tp, 2 * HP), lambda b, p, kk: (b, p)),
                   pl.BlockSpec((blk, 2 * HP), lambda b, p, kk: (b * nq + kk, p)),
                   pl.BlockSpec((blk, LANE), lambda b, p, kk: (b * nq + kk, p))],
        out_shape=[S((rows, MLA_HEADS * HP), MXU_DT), S((rows, MLA_HEADS * HP), MXU_DT),
                   S((rows, MLA_HEADS * V_HEAD), MXU_DT)],
        scratch_shapes=[pltpu.VMEM((tp, 2 * HP), F32), pltpu.VMEM((tp, LANE), F32)],
        sem=("parallel", "parallel", "arbitrary"), name=name)


def rope_bwd(dq, dk, dv, tabq, tabk, tp, tm, name):
    rows = dq.shape[0]
    nt = tp // tm
    wq = MLA_HEADS * HP

    def body(dq_r, dk_r, dv_r, tq_r, tk_r, dqa_o, dkv_o, dkr_o):
        dqv = dq_r[...].astype(F32)
        lane = _lane_mod(dqv.shape)
        in_rope = (lane >= QK_NOPE) & (lane < QK_HEAD)
        rope = jnp.where(in_rope, dqv, 0.0)
        da = jnp.where(lane < QK_HEAD, dqv, 0.0) + pltpu.roll(rope, QK_ROPE, axis=1)
        dqa_o[...] = (da * jnp.tile(tq_r[...], (1, MLA_HEADS))).astype(MXU_DT)
        dkf = dk_r[...].astype(F32)
        dkv_o[:, 0:wq] = jnp.where(lane < QK_NOPE, dkf, 0.0).astype(MXU_DT)
        dkv_o[:, wq:] = dv_r[...]
        kr = jnp.where(in_rope, dkf, 0.0)
        tot = kr[:, 0:HP]
        for h in range(1, MLA_HEADS):
            tot = tot + kr[:, h * HP:(h + 1) * HP]
        dkr_o[...] = ((tot + pltpu.roll(tot, QK_ROPE, axis=1)) * tk_r[...]).astype(MXU_DT)

    def rowblk(wd):
        return pl.BlockSpec((tm, wd), lambda i: (i, 0))

    tab = pl.BlockSpec((tm, HP), lambda i: (i % nt, 0))
    return pl.pallas_call(
        body, grid=(rows // tm,), in_specs=[rowblk(wq), rowblk(wq), rowblk(MLA_HEADS * V_HEAD), tab, tab],
        out_specs=[rowblk(wq), rowblk(wq + MLA_HEADS * V_HEAD), rowblk(HP)],
        out_shape=[S((rows, wq), MXU_DT), S((rows, wq + MLA_HEADS * V_HEAD), MXU_DT), S((rows, HP), MXU_DT)],
        compiler_params=_cp(("parallel",)), name=name)(dq, dk, dv, tabq, tabk)


def loss_head(h, target, gain, tp, t_real, tm, name):
    rows = h.shape[0]
    nt = tp // tm

    def body(h_r, t_r, g_r, dh_o, loss_o, dg_o):
        i = pl.program_id(0)

        @pl.when(i == 0)
        def _():
            loss_o[...] = jnp.zeros_like(loss_o)
            dg_o[...] = jnp.zeros_like(dg_o)

        xv = h_r[...]
        rstd = lax.rsqrt(jnp.mean(xv * xv, axis=-1, keepdims=True) + EPS)
        xhat = xv * rstd
        g = g_r[...]
        pos = (i % nt) * tm + lax.broadcasted_iota(jnp.int32, (tm, 1), 0)
        valid = (pos >= N_META) & (pos < t_real)
        err = jnp.where(valid, xhat * g - t_r[...], 0.0)
        loss_o[...] += 0.5 * jnp.sum(jnp.mean(err * err, axis=-1, keepdims=True))
        dy = err * (1.0 / D_MODEL)
        dg_o[...] += jnp.sum(dy * xhat, axis=0, keepdims=True)
        dxh = dy * g
        dh_o[...] = rstd * (dxh - xhat * jnp.mean(dxh * xhat, axis=-1, keepdims=True))

    blk = pl.BlockSpec((tm, D_MODEL), lambda i: (i, 0))
    return pl.pallas_call(
        body, grid=(rows // tm,), in_specs=[blk, blk, pl.BlockSpec((1, D_MODEL), lambda i: (0, 0))],
        out_specs=[blk, pl.BlockSpec((1, LANE), lambda i: (0, 0)), pl.BlockSpec((1, D_MODEL), lambda i: (0, 0))],
        out_shape=[S((rows, D_MODEL), F32), S((1, LANE), F32), S((1, D_MODEL), F32)],
        compiler_params=_cp(("arbitrary",)), name=name)(h, target, gain)


ADAM_TILE_ELEMS = 128 * 1024


def adamw(g, w, m, v, name):
    shape = w.shape
    cols = shape[-1]
    rws = max(1, math.prod(shape[:-1]))
    tr = rws if rws * cols <= ADAM_TILE_ELEMS else _div_tile(rws, max(SUBLANE, ADAM_TILE_ELEMS // cols), SUBLANE)
    bc1 = 1.0 - ADAM_B1 ** ADAM_STEP
    bc2 = 1.0 - ADAM_B2 ** ADAM_STEP

    def body(g_r, w_r, m_r, v_r, do, mo, vo):
        gv = g_r[...]
        mn = ADAM_B1 * m_r[...] + (1.0 - ADAM_B1) * gv
        vn = ADAM_B2 * v_r[...] + (1.0 - ADAM_B2) * (gv * gv)
        m_hat = mn / bc1
        v_hat = vn / bc2
        do[...] = -ADAM_LR * (m_hat / (jnp.sqrt(v_hat) + ADAM_EPS) + ADAM_WD * w_r[...])
        mo[...] = mn
        vo[...] = vn

    blk = pl.BlockSpec((tr, cols), lambda i: (i, 0))
    outs = pl.pallas_call(
        body, grid=(rws // tr,), in_specs=[blk] * 4, out_specs=[blk] * 3, out_shape=[S((rws, cols), F32)] * 3,
        compiler_params=_cp(("parallel",)), name=name)(*[a.reshape(rws, cols) for a in (g, w, m, v)])
    return tuple(o.reshape(shape) for o in outs)


SUM_TILE_ELEMS = 128 * 1024


def _place():
    return lax.axis_index("x"), lax.axis_index("y"), lax.axis_index("c")


def _remote(src, dst, send_sems, recv_sems, k, to):
    return pltpu.make_async_remote_copy(src_ref=src, dst_ref=dst, send_sem=send_sems.at[k], recv_sem=recv_sems.at[k],
                                        device_id=to, device_id_type=MESH)


def chip_index():
    return 2 * lax.axis_index("x") + lax.axis_index("y")


def _sem_pair(n):
    return [pltpu.SemaphoreType.DMA((n,)), pltpu.SemaphoreType.DMA((n,))]


def stage_gather_chips(xs):
    def copies(ins, outs, sems):
        send_sems, recv_sems = sems
        mx, my, mc = _place()
        sibling = (mx, my, 1 - mc)
        chips = [(1 - mx, my), (mx, 1 - my), (1 - mx, 1 - my)]
        first, landed, passed, from_sibling = [], [], [], []
        for i, (x_ref, out_ref) in enumerate(zip(ins, outs)):
            def piece(cx, cy, h, out_ref=out_ref):
                return out_ref.at[2 * cx + cy, h]

            for j, (cx, cy) in enumerate(chips):
                k = 6 * i + j
                first.append(_remote(x_ref.at[mc], piece(mx, my, mc), send_sems, recv_sems, k, (cx, cy, mc)))
                landed.append(_remote(x_ref.at[mc], piece(cx, cy, mc), send_sems, recv_sems, k, (cx, cy, mc)))
                passed.append(_remote(piece(cx, cy, mc), piece(cx, cy, mc), send_sems, recv_sems, k + 3, sibling))
                from_sibling.append(_remote(x_ref.at[mc], piece(cx, cy, 1 - mc), send_sems, recv_sems, k + 3, sibling))
        return first, landed, passed, from_sibling

    def start(ins, outs, sems):
        for cp in copies(ins, outs, sems)[0]:
            cp.start()

    def finish(ins, outs, sems):
        first, landed, passed, from_sibling = copies(ins, outs, sems)
        for arrived, onward in zip(landed, passed):
            arrived.wait_recv()
            onward.start()
        for cp in from_sibling:
            cp.wait_recv()
        for cp in first + passed:
            cp.wait_send()

    return Stage(list(xs), [S((4,) + x.shape, x.dtype) for x in xs], _sem_pair(6 * len(xs)), start, finish)


def own_block(gathered, xs):
    return lax.dynamic_update_slice(gathered, xs[None], (chip_index(), 0, 0, 0))


def stage_pair_exchange(gs):
    def copies(ins, outs, sems):
        send_sems, recv_sems = sems
        mx, my, mc = _place()
        return [_remote(g_ref.at[s, 1 - mc], land_ref.at[s], send_sems, recv_sems, 4 * i + s, (mx, my, 1 - mc))
                for i, (g_ref, land_ref) in enumerate(zip(ins, outs)) for s in range(4)]

    def start(ins, outs, sems):
        for cp in copies(ins, outs, sems):
            cp.start()

    def finish(ins, outs, sems):
        cps = copies(ins, outs, sems)
        for cp in cps:
            cp.wait_recv()
        for cp in cps:
            cp.wait_send()

    return Stage(list(gs), [S((4,) + g.shape[2:], g.dtype) for g in gs], _sem_pair(4 * len(gs)), start, finish)


def _sum_rows(rws, width):
    return _div_tile(rws, max(SUBLANE, SUM_TILE_ELEMS // width), SUBLANE)


def pair_sum(g4, land, c_idx, name):
    _, _, rws, wd = g4.shape
    th = _sum_rows(rws, wd)

    def body(c_ref, a_ref, b_ref, o_ref):
        o_ref[...] = a_ref[...] + b_ref[...]

    return pl.pallas_call(
        body,
        grid_spec=pltpu.PrefetchScalarGridSpec(
            num_scalar_prefetch=1, grid=(4, rws // th),
            in_specs=[pl.BlockSpec((None, None, th, wd), lambda s, i, c: (s, c[0], i, 0)),
                      pl.BlockSpec((None, th, wd), lambda s, i, c: (s, i, 0))],
            out_specs=pl.BlockSpec((None, th, wd), lambda s, i, c: (s, i, 0))),
        out_shape=S((4, rws, wd), F32), compiler_params=_cp(("parallel", "parallel")), name=name)(c_idx, g4, land)


def stage_chip_scatter(ps):
    def copies(ins, outs, sems):
        send_sems, recv_sems = sems
        mx, my, mc = _place()
        me = 2 * mx + my
        chips = [(1 - mx, my), (mx, 1 - my), (1 - mx, 1 - my)]
        sent, landed = [], []
        for i, (p_ref, land_ref) in enumerate(zip(ins, outs)):
            for j, (cx, cy) in enumerate(chips):
                k = 3 * i + j
                sent.append(_remote(p_ref.at[2 * cx + cy], land_ref.at[me], send_sems, recv_sems, k, (cx, cy, mc)))
                landed.append(_remote(p_ref.at[me], land_ref.at[2 * cx + cy], send_sems, recv_sems, k, (cx, cy, mc)))
        return sent, landed

    def start(ins, outs, sems):
        for cp in copies(ins, outs, sems)[0]:
            cp.start()

    def finish(ins, outs, sems):
        sent, landed = copies(ins, outs, sems)
        for cp in landed:
            cp.wait_recv()
        for cp in sent:
            cp.wait_send()

    return Stage(list(ps), [S(p.shape, p.dtype) for p in ps], _sem_pair(3 * len(ps)), start, finish)


def chip_sum(l4, p4, me_idx, name):
    _, rws, wd = l4.shape
    th = _sum_rows(rws, wd)

    def body(me_ref, a, b, c, d, own, o_ref):
        me = me_ref[0]
        parts = [jnp.where(me == s, own[...], r[...]) for s, r in enumerate((a, b, c, d))]
        o_ref[...] = ((parts[0] + parts[1]) + parts[2]) + parts[3]

    def blk(s):
        return pl.BlockSpec((None, th, wd), lambda i, me: (jnp.where(me[0] == s, (s + 1) % 4, s), i, 0))

    return pl.pallas_call(
        body,
        grid_spec=pltpu.PrefetchScalarGridSpec(
            num_scalar_prefetch=1, grid=(rws // th,),
            in_specs=[blk(0), blk(1), blk(2), blk(3), pl.BlockSpec((None, th, wd), lambda i, me: (me[0], i, 0))],
            out_specs=pl.BlockSpec((th, wd), lambda i, me: (i, 0))),
        out_shape=S((rws, wd), F32), compiler_params=_cp(("parallel",)), name=name)(me_idx, l4, l4, l4, l4, p4)


def stage_pair_gather(rs):
    def copies(ins, outs, sems):
        send_sems, recv_sems = sems
        mx, my, mc = _place()
        return [_remote(r_ref, out_ref, send_sems, recv_sems, i, (mx, my, 1 - mc))
                for i, (r_ref, out_ref) in enumerate(zip(ins, outs))]

    def start(ins, outs, sems):
        for cp in copies(ins, outs, sems):
            cp.start()

    def finish(ins, outs, sems):
        for cp in copies(ins, outs, sems):
            cp.wait()

    return Stage(list(rs), [S(r.shape, r.dtype) for r in rs], _sem_pair(len(rs)), start, finish)


PACK_ELEMS = 16 * LANE


def pack_rows(arrays, lead, total_mult):
    parts, offs, r0 = [], [], 0
    for a in arrays:
        flat = a.reshape(a.shape[:lead] + (-1,))
        elems = _round_up(flat.shape[-1], PACK_ELEMS)
        flat = jnp.pad(flat, [(0, 0)] * lead + [(0, elems - flat.shape[-1])])
        parts.append(flat.reshape(flat.shape[:lead] + (elems // LANE, LANE)))
        offs.append((r0, elems // LANE))
        r0 += elems // LANE
    total = _round_up(r0, total_mult)
    if total > r0:
        parts.append(jnp.zeros(parts[0].shape[:lead] + (total - r0, LANE), parts[0].dtype))
    return jnp.concatenate(parts, axis=lead), offs


def unpack_rows(buf, off, shape):
    r0, nr = off
    lead = buf.shape[:-2]
    n = math.prod(shape)
    return buf[..., r0:r0 + nr, :].reshape(lead + (nr * LANE,))[..., :n].reshape(lead + tuple(shape))


def unshard(stacked, axis):
    x = jnp.moveaxis(stacked, 0, axis)
    return x.reshape(x.shape[:axis] + (4 * x.shape[axis + 1],) + x.shape[axis + 2:])


def to_shards(full, axis):
    n = full.shape[axis] // 4
    x = full.reshape(full.shape[:axis] + (4, n) + full.shape[axis + 1:])
    return jnp.moveaxis(x, axis, 0)


def _rot_cols(w):
    half = w.shape[-1] // 2
    return jnp.concatenate([-w[..., half:], w[..., :half]], axis=-1)


def _unrot_cols(dw):
    half = dw.shape[-1] // 2
    return jnp.concatenate([dw[..., half:], -dw[..., :half]], axis=-1)


def odd_w_in_padded(w_in):
    kr = w_in[:, Q_LORA + KV_LORA:]
    rows = w_in.shape[0]
    return jnp.concatenate([w_in[:, :Q_LORA], jnp.zeros((rows, 128), w_in.dtype), w_in[:, Q_LORA:Q_LORA + KV_LORA],
                            jnp.zeros((rows, 64), w_in.dtype), kr, _rot_cols(kr)], axis=1)


def odd_w_in_unpad(dwp):
    base = 512 + KV_LORA + 64
    dkr = dwp[:, base:base + QK_ROPE] + _unrot_cols(dwp[:, base + QK_ROPE:base + 2 * QK_ROPE])
    return jnp.concatenate([dwp[:, :Q_LORA], dwp[:, 512:512 + KV_LORA], dkr], axis=1)


def uq_padded(w_uq):
    w = w_uq.reshape(Q_LORA, MLA_HEADS, QK_HEAD)
    return jnp.concatenate([w, _rot_cols(w[:, :, QK_NOPE:])], axis=-1).reshape(Q_LORA, MLA_HEADS * HP)


def uq_unpad(dwp):
    d = dwp.reshape(Q_LORA, MLA_HEADS, HP)
    rope = d[:, :, QK_NOPE:QK_HEAD] + _unrot_cols(d[:, :, QK_HEAD:])
    return jnp.concatenate([d[:, :, :QK_NOPE], rope], axis=-1).reshape(Q_LORA, MLA_HEADS * QK_HEAD)


def ukv_padded(w_ukv):
    w = w_ukv.reshape(KV_LORA, MLA_HEADS, QK_NOPE + V_HEAD)
    wk = jnp.concatenate([w[:, :, :QK_NOPE], jnp.zeros((KV_LORA, MLA_HEADS, HP - QK_NOPE), w.dtype)], axis=-1)
    return jnp.concatenate([wk.reshape(KV_LORA, MLA_HEADS * HP), w[:, :, QK_NOPE:].reshape(KV_LORA, MLA_HEADS * V_HEAD)],
                           axis=1)


def ukv_unpad(dwp):
    dk = dwp[:, :MLA_HEADS * HP].reshape(KV_LORA, MLA_HEADS, HP)[:, :, :QK_NOPE]
    dv = dwp[:, MLA_HEADS * HP:].reshape(KV_LORA, MLA_HEADS, V_HEAD)
    return jnp.concatenate([dk, dv], axis=-1).reshape(KV_LORA, MLA_HEADS * (QK_NOPE + V_HEAD))


def block_diag(w):
    h, d, _ = w.shape
    eye = jnp.eye(h, dtype=w.dtype)
    return (eye[:, None, :, None] * w[:, :, None, :]).reshape(h * d, h * d)


def block_diag_part(dense, h):
    d = dense.shape[0] // h
    x = dense.reshape(h, d, h, d)
    return jnp.stack([x[i, :, i, :] for i in range(h)], axis=0)


def rope_tables(tp):
    pos = jnp.arange(tp, dtype=F32)
    inv_freq = ROPE_BASE ** (-jnp.arange(0, QK_ROPE, 2, dtype=F32) / QK_ROPE)
    ang = pos[:, None] * inv_freq[None, :]
    cos2 = jnp.tile(jnp.cos(ang), (1, 2))
    sin2 = jnp.tile(jnp.sin(ang), (1, 2))
    tabq = jnp.concatenate([jnp.ones((tp, QK_NOPE), F32), cos2, sin2], axis=1)
    tabk = jnp.concatenate([jnp.zeros((tp, QK_NOPE), F32), cos2, sin2], axis=1)
    return tabq, tabk


class Dims:
    def __init__(self, nb, seq):
        self.nb = nb
        self.t_real = seq + N_META
        self.tp = _round_up(self.t_real, ATT_BLK)
        self.n = self.tp // 4
        assert self.n % 16 == 0
        self.rows = nb * self.tp


class NoComm:
    def advance(self, carried):
        return None


def even_fwd(h, p, dm, comm):
    (u, hn), _ = norm_matmul(h, 0, D_MODEL, p["norm"], p["w_in"], dm.n, 512, F32, "ev_in")
    (y, ca, xc, a, hs), got = even_mid_fwd(u, p["conv_a"], p["conv_b"], p["conv_b_bias"], p["rw"], p["r_b"], p["iw"],
                                           p["i_b"], p["lam"], dm.nb, dm.tp, dm.n, "ev_mid", carry=comm.advance(None))
    comm.advance(got)
    out = matmul_res(y, p["w_out"].reshape(2, CONV_W, D_MODEL), h, dm.n, 512, "ev_out")
    return out, (h, u, hn, ca, xc, a, hs, y)


def even_bwd(dout, saved, p, dm, comm):
    h, u, hn, ca, xc, a, hs, y = saved
    g = {}
    dycat = matmul_nt(dout, p["w_out"], dm.n, 512, F32, "ev_dycat")
    g["w_out"], got = matmul_tn(y, dout, dm.n, "ev_dw_out", carry=comm.advance(None))
    outs, got = even_mid_bwd(u, dycat, ca, xc, a, hs, p["conv_a"], p["conv_b"], p["rw"], p["r_b"], p["iw"], p["i_b"],
                             p["lam"], dm.nb, dm.tp, dm.n, "ev_mid_bwd", carry=comm.advance(got))
    du, g["conv_a"], g["conv_b"], g["conv_b_bias"], drw, g["r_b"], diw, g["i_b"], g["lam"] = outs
    g["r_w"] = block_diag_part(drw, LRU_HEADS)
    g["i_w"] = block_diag_part(diw, LRU_HEADS)
    g["w_in"], got = matmul_tn(hn, du, dm.n, "ev_dw_in", carry=comm.advance(got))
    comm.advance(got)
    dx, g["norm"] = matmul_nt_normbwd(du, p["w_in"], h, 0, p["norm"], dout, dm.n, 512, F32, "ev_dx")
    return dx, g


def odd_fwd(h, p, tabq, tabk, dm, comm):
    nt = dm.tp // dm.n
    (u, hn), _ = norm_matmul(h, 0, D_MODEL, p["norm"], p["w_in_p"], dm.n, ODD_PAD, F32, "od_in")
    tab_spec = pl.BlockSpec((dm.n, HP), lambda i, j: (i % nt, 0))
    (q, cqn), _ = norm_matmul(u, 0, Q_LORA, p["q_norm"], p["w_uq_p"], dm.n, 512, MXU_DT, "od_q",
                              epi=_q_rope_epi, epi_ops=(tabq,), epi_specs=(tab_spec,))
    kr_spec = pl.BlockSpec((dm.n, HP), lambda i, j: (i, ODD_KR_COL))
    (k, ckvn), _ = norm_matmul(u, ODD_CKV_COL, KV_LORA, p["kv_norm"], p["w_uk_p"], dm.n, 512, MXU_DT, "od_k",
                               epi=_k_rope_epi, epi_ops=(u, tabk), epi_specs=(kr_spec, tab_spec))
    (v, _), _ = norm_matmul(u, ODD_CKV_COL, KV_LORA, p["kv_norm"], p["w_uv_p"], dm.n, 512, MXU_DT, "od_v")
    (o, lse), got = attn_fwd(q, k, v, dm.nb, dm.tp, "od_attn", carry=comm.advance(None))
    comm.advance(got)
    out = matmul_res(o[None], p["w_out"][None], h, dm.n, 512, "od_out")
    return out, (h, u, hn, cqn, ckvn, q, k, v, o, lse)


def odd_bwd(dout, saved, p, tabq, tabk, dm, comm):
    h, u, hn, cqn, ckvn, q, k, v, o, lse = saved
    g = {}
    do = matmul_nt(dout, p["w_out"], dm.n, 512, MXU_DT, "od_do")
    g["w_out"], got = matmul_tn(o, dout, dm.n, "od_dw_out", carry=comm.advance(None))
    (dq, dk, dv), got = attn_bwd(q, k, v, o, do, lse, dm.nb, dm.tp, "od_attn_bwd", carry=comm.advance(got))
    dqa, dkv, dkr = rope_bwd(dq, dk, dv, tabq, tabk, dm.tp, dm.n, "od_rope_bwd")
    g["w_uq_p"], got = matmul_tn(cqn, dqa, dm.n, "od_dw_uq", carry=comm.advance(got))
    comm.advance(got)
    g["w_ukv_p"], _ = matmul_tn(ckvn, dkv, dm.n, "od_dw_ukv")
    dcq, g["q_norm"] = matmul_nt_normbwd(dqa, p["w_uq_p"], u, 0, p["q_norm"], None, dm.n, 512, MXU_DT, "od_dcq")
    dckv, g["kv_norm"] = matmul_nt_normbwd(dkv, p["w_ukv_p"], u, ODD_CKV_COL, p["kv_norm"], None, dm.n, 512, MXU_DT,
                                           "od_dckv")
    du = jnp.concatenate([dcq, jnp.zeros((dm.rows, 128), MXU_DT), dckv, dkr], axis=1)
    g["w_in_p"], _ = matmul_tn(hn, du, dm.n, "od_dw_in")
    dx, g["norm"] = matmul_nt_normbwd(du, p["w_in_p"], h, 0, p["norm"], dout, dm.n, ODD_PAD, F32, "od_dx")
    return dx, g


def ffn_fwd(h, p, dm, comm):
    (up, hn), got = norm_matmul(h, 0, D_MODEL, p["norm"], p["w_up"], dm.n, D_FF // 2, MXU_DT, "ffn_up",
                                carry=comm.advance(None))
    comm.advance(got)
    u, y = ffn_mid_fwd(up, p["cw"], p["cb"], dm.nb, dm.tp, dm.n, "ffn_mid")
    out = matmul_res(y[None], p["w_down"][None], h, dm.n, 512, "ffn_down")
    return out, (h, up, hn, u, y)


def ffn_bwd(dout, saved, p, dm, comm):
    h, up, hn, u, y = saved
    g = {}
    dy = matmul_nt(dout, p["w_down"], dm.n, D_FF // 2, MXU_DT, "ffn_dy")
    g["w_down"], got = matmul_tn(y, dout, dm.n, "ffn_dw_down", carry=comm.advance(None))
    (dup, g["cw"], g["cb"]), got = ffn_mid_bwd(dy, u, up, p["cw"], dm.nb, dm.tp, dm.n, "ffn_mid_bwd",
                                               carry=comm.advance(got))
    g["w_up"], got = matmul_tn(hn, dup, dm.n, "ffn_dw_up", carry=comm.advance(got), col_shards=4)
    comm.advance(got)
    dx, g["norm"] = matmul_nt_normbwd(dup, p["w_up"], h, 0, p["norm"], dout, dm.n, D_FF // 2, F32, "ffn_dx")
    return dx, g


def _row(v):
    return v.reshape(1, -1)


def even_params(wf, j):
    return dict(norm=_row(wf["ev_norm"][j]), w_in=wf["ev_w_in"], conv_a=wf["ev_conv_a"][j], conv_b=wf["ev_conv_b"][j],
                conv_b_bias=_row(wf["ev_conv_b_bias"][j]), rw=block_diag(wf["ev_gate_r_w"][j]).astype(MXU_DT),
                r_b=_row(wf["ev_gate_r_b"][j]), iw=block_diag(wf["ev_gate_i_w"][j]).astype(MXU_DT),
                i_b=_row(wf["ev_gate_i_b"][j]), lam=_row(wf["ev_lru_lambda"][j]), w_out=wf["ev_w_out"])


def odd_params(wf, j):
    wkv = ukv_padded(wf["od_w_ukv"])
    return dict(norm=_row(wf["od_norm"][j]), w_in_p=odd_w_in_padded(wf["od_w_in"]), q_norm=_row(wf["od_q_norm"][j]),
                kv_norm=_row(wf["od_kv_norm"][j]), w_uq_p=uq_padded(wf["od_w_uq"]), w_ukv_p=wkv,
                w_uk_p=wkv[:, :MLA_HEADS * HP], w_uv_p=wkv[:, MLA_HEADS * HP:], w_out=wf["od_w_out"])


def ffn_params(wf, layer):
    return dict(norm=_row(wf["ffn_norm"][layer]), w_up=wf["ffn_w_up"],
                cw=jnp.moveaxis(wf["ffn_conv_w"][layer].reshape(3, 2, D_FF), 1, 0),
                cb=wf["ffn_conv_b"][layer].reshape(2, 1, D_FF), w_down=wf["ffn_w_down"])


def even_grads(g):
    out = {"ev_" + k_: g[k_] for k_ in ("w_in", "conv_a", "conv_b", "w_out")}
    out.update({"ev_norm": g["norm"][0], "ev_conv_b_bias": g["conv_b_bias"][0], "ev_gate_r_w": g["r_w"],
                "ev_gate_r_b": g["r_b"][0], "ev_gate_i_w": g["i_w"], "ev_gate_i_b": g["i_b"][0],
                "ev_lru_lambda": g["lam"][0]})
    return out


def odd_grads(g):
    return {"od_norm": g["norm"][0], "od_q_norm": g["q_norm"][0], "od_kv_norm": g["kv_norm"][0],
            "od_w_in": odd_w_in_unpad(g["w_in_p"]), "od_w_uq": uq_unpad(g["w_uq_p"]),
            "od_w_ukv": ukv_unpad(g["w_ukv_p"]), "od_w_out": g["w_out"]}


def ffn_grads(g):
    return {"ffn_norm": g["norm"][0], "ffn_w_up": g["w_up"], "ffn_conv_w": jnp.moveaxis(g["cw"], 0, 1).reshape(3, 2 * D_FF),
            "ffn_conv_b": g["cb"].reshape(2 * D_FF), "ffn_w_down": g["w_down"]}


WEIGHTS = ["meta_tokens", "ev_norm", "ev_w_in", "ev_conv_a", "ev_conv_b", "ev_conv_b_bias", "ev_gate_r_w", "ev_gate_r_b",
           "ev_gate_i_w", "ev_gate_i_b", "ev_lru_lambda", "ev_w_out", "od_norm", "od_w_in", "od_q_norm", "od_kv_norm",
           "od_w_uq", "od_w_ukv", "od_w_out", "ffn_norm", "ffn_w_up", "ffn_conv_w", "ffn_conv_b", "ffn_w_down",
           "final_norm"]
SHARD_AXIS = {"meta_tokens": 1, "ev_w_in": 2, "ev_conv_a": 2, "ev_conv_b": 2, "ev_w_out": 1, "od_norm": 1, "od_w_in": 1,
              "od_q_norm": 1, "od_kv_norm": 1, "od_w_uq": 2, "od_w_ukv": 2, "od_w_out": 1, "ffn_w_up": 2,
              "ffn_conv_w": 2, "ffn_w_down": 1}
MATMUL_WEIGHTS = ["ev_w_in", "ev_w_out", "od_w_in", "od_w_uq", "od_w_ukv", "od_w_out", "ffn_w_up", "ffn_w_down"]


LAYER_ORDER = [("ev", 0), ("ffn", 0), ("od", 0), ("ffn", 1), ("ev", 1), ("ffn", 2), ("od", 1), ("ffn", 3)]
LAYER_MATMUL = {"ev": ["ev_w_in", "ev_w_out"], "od": ["od_w_in", "od_w_uq", "od_w_ukv", "od_w_out"],
                "ffn": ["ffn_w_up", "ffn_w_down"]}
LAYER_SHARDED = {"ev": ["ev_w_in", "ev_conv_a", "ev_conv_b", "ev_w_out"],
                 "od": ["od_norm", "od_w_in", "od_q_norm", "od_kv_norm", "od_w_uq", "od_w_ukv", "od_w_out"],
                 "ffn": ["ffn_w_up", "ffn_conv_w", "ffn_w_down"]}
STACKED_SHARDS = "ffn_w_up"


def gather_at_entry(w, names, first, name):
    buf, offs = pack_rows([w[n] for n in names], 0, 32)
    halves = buf.reshape(2, buf.shape[0] // 2, LANE)
    outs = run_stage(stage_gather_chips([halves] + first.halves), name)
    first.step, first.got = 2, outs[1:]
    got = own_block(outs[0], halves).reshape(4, buf.shape[0], LANE)
    return {n: unshard(unpack_rows(got, off, w[n].shape), SHARD_AXIS[n]) for n, off in zip(names, offs)}


def _halves(a):
    return a.reshape(2, a.shape[0] // 2, a.shape[1])


class GatherComm:
    def __init__(self, w, kind, idx):
        self.names = LAYER_MATMUL[kind]
        self.halves = [_halves(w[n][idx].astype(MXU_DT)) for n in self.names]
        self.stage = stage_gather_chips(self.halves)
        self.step, self.got = 0, None

    def advance(self, carried):
        self.step += 1
        if self.step == 1:
            return self.stage
        if self.step == 2:
            self.got = carried
        return None

    def weights(self):
        out = {}
        for n, got, own in zip(self.names, self.got, self.halves):
            stacked = own_block(got, own).reshape(4, 2 * own.shape[1], own.shape[2])
            out[n] = stacked if n == STACKED_SHARDS else unshard(stacked, SHARD_AXIS[n] - 1)
        return out


class ReduceComm:
    def __init__(self, grads, axes, c_idx, tag, tail=None):
        shards = {n: grads[n] if n == STACKED_SHARDS else to_shards(grads[n], axes[n]) for n in grads}
        self.big = [n for n in grads if n in MATMUL_WEIGHTS]
        self.small = [n for n in grads if n not in MATMUL_WEIGHTS]
        self.shapes = {n: shards[n].shape[1:] for n in grads}
        arrays = [shards[n].reshape(4, 2, shards[n].shape[1] // 2, shards[n].shape[2]) for n in self.big]
        gs, self.offs = pack_rows([shards[n] for n in self.small], 1, 16)
        self.rs = gs.shape[1] // 2
        parts = [gs.reshape(4, 2, self.rs, LANE)]
        self.rr = 0
        if tail is not None:
            self.rr = tail.shape[0] // 8
            parts.append(tail.reshape(4, 2, self.rr, LANE))
        arrays.append(jnp.concatenate(parts, axis=2) if len(parts) > 1 else parts[0])
        self.arrays, self.c_idx, self.tag, self.step = arrays, c_idx, tag, 0
        self.part = self.mine = self.theirs = None

    def advance(self, carried):
        self.step += 1
        if self.step == 1:
            return stage_pair_exchange(self.arrays)
        if self.step == 2:
            self.part = [pair_sum(g, land, self.c_idx, "grad_pair_sum_%s_%d" % (self.tag, i))
                         for i, (g, land) in enumerate(zip(self.arrays, carried))]
            return stage_chip_scatter(self.part)
        if self.step == 3:
            me_idx = chip_index().astype(jnp.int32).reshape(1)
            self.mine = [chip_sum(land, part, me_idx, "grad_chip_sum_%s_%d" % (self.tag, i))
                         for i, (land, part) in enumerate(zip(carried, self.part))]
            return stage_pair_gather(self.mine)
        if self.step == 4:
            self.theirs = carried
        return None

    def run_alone(self, name):
        stage = self.advance(None)
        while stage is not None:
            stage = self.advance(run_stage(stage, name + "_%d" % self.step))

    def results(self):
        south = self.c_idx[0] == 0
        boths = [jnp.stack([jnp.where(south, m, t), jnp.where(south, t, m)], axis=0)
                 for m, t in zip(self.mine, self.theirs)]
        out = {n: b.reshape(self.shapes[n]) for n, b in zip(self.big, boths)}
        packed = boths[-1]
        flat = packed[:, :self.rs].reshape(2 * self.rs, LANE)
        out.update({n: unpack_rows(flat, off, self.shapes[n]) for n, off in zip(self.small, self.offs)})
        return out, packed[:, self.rs:self.rs + self.rr]


def kernel(x, meta_tokens, ev_norm, ev_w_in, ev_conv_a, ev_conv_b, ev_conv_b_bias, ev_gate_r_w, ev_gate_r_b, ev_gate_i_w, ev_gate_i_b, ev_lru_lambda, ev_w_out, od_norm, od_w_in, od_q_norm, od_kv_norm, od_w_uq, od_w_ukv, od_w_out, ffn_norm, ffn_w_up, ffn_conv_w, ffn_conv_b, ffn_w_down, final_norm, loss_target, m_meta_tokens, m_ev_norm, m_ev_w_in, m_ev_conv_a, m_ev_conv_b, m_ev_conv_b_bias, m_ev_gate_r_w, m_ev_gate_r_b, m_ev_gate_i_w, m_ev_gate_i_b, m_ev_lru_lambda, m_ev_w_out, m_od_norm, m_od_w_in, m_od_q_norm, m_od_kv_norm, m_od_w_uq, m_od_w_ukv, m_od_w_out, m_ffn_norm, m_ffn_w_up, m_ffn_conv_w, m_ffn_conv_b, m_ffn_w_down, m_final_norm, v_meta_tokens, v_ev_norm, v_ev_w_in, v_ev_conv_a, v_ev_conv_b, v_ev_conv_b_bias, v_ev_gate_r_w, v_ev_gate_r_b, v_ev_gate_i_w, v_ev_gate_i_b, v_ev_lru_lambda, v_ev_w_out, v_od_norm, v_od_w_in, v_od_q_norm, v_od_kv_norm, v_od_w_uq, v_od_w_ukv, v_od_w_out, v_ffn_norm, v_ffn_w_up, v_ffn_conv_w, v_ffn_conv_b, v_ffn_w_down, v_final_norm):
    given = dict(locals())
    w = {n: given[n] for n in WEIGHTS}
    nb, seq, _ = x.shape
    dm = Dims(nb, seq)
    n_layers = len(LAYER_ORDER)

    wf = {n: w[n] for n in WEIGHTS if n not in SHARD_AXIS}
    gathers = [GatherComm(w, kind, idx) for kind, idx in LAYER_ORDER]
    wf.update(gather_at_entry(w, [n for n in SHARD_AXIS if n not in MATMUL_WEIGHTS], gathers[0], "gather_at_entry"))

    tail = dm.tp - dm.t_real
    meta = jnp.broadcast_to(wf["meta_tokens"][None], (nb, N_META, D_MODEL))
    h = jnp.concatenate([meta, x, jnp.zeros((nb, tail, D_MODEL), F32)], axis=1).reshape(dm.rows, D_MODEL)
    tgt = jnp.pad(loss_target, ((0, 0), (N_META, tail), (0, 0))).reshape(dm.rows, D_MODEL)
    tabq, tabk = rope_tables(dm.tp)

    params, saved = [], []
    for i, (kind, idx) in enumerate(LAYER_ORDER):
        wl = dict(wf)
        wl.update(gathers[i].weights())
        comm = gathers[i + 1] if i + 1 < n_layers else NoComm()
        if kind == "ev":
            p = even_params(wl, idx)
            h, sv = even_fwd(h, p, dm, comm)
        elif kind == "od":
            p = odd_params(wl, idx)
            h, sv = odd_fwd(h, p, tabq, tabk, dm, comm)
        else:
            p = ffn_params(wl, idx)
            h, sv = ffn_fwd(h, p, dm, comm)
        params.append(p)
        saved.append(sv)

    dh, loss, dfinal = loss_head(h, tgt, _row(wf["final_norm"]), dm.tp, dm.t_real, dm.n, "loss_head")
    loss = lax.psum(loss[0, 0], ("x", "y", "c"))

    c_idx = lax.axis_index("c").astype(jnp.int32).reshape(1)
    layer_grads = {n: {} for n in WEIGHTS}
    pending, reduces = NoComm(), []
    for i in reversed(range(n_layers)):
        kind, idx = LAYER_ORDER[i]
        if kind == "ev":
            dh, g = even_bwd(dh, saved[i], params[i], dm, pending)
            g = even_grads(g)
        elif kind == "od":
            dh, g = odd_bwd(dh, saved[i], params[i], tabq, tabk, dm, pending)
            g = odd_grads(g)
        else:
            dh, g = ffn_bwd(dh, saved[i], params[i], dm, pending)
            g = ffn_grads(g)
        for n in g:
            if n not in SHARD_AXIS:
                layer_grads[n][idx] = g[n]
        if i > 0:
            pending = ReduceComm({n: g[n] for n in LAYER_SHARDED[kind]}, {n: SHARD_AXIS[n] - 1 for n in SHARD_AXIS},
                                 c_idx, "%s%d" % (kind, idx))
            reduces.append((pending, idx))
    dh3 = dh.reshape(nb, dm.tp, D_MODEL)
    grad_x = dh3[:, N_META:dm.t_real]

    repl = [n for n in WEIGHTS if n not in SHARD_AXIS]
    layer_grads["final_norm"] = {0: dfinal[0]}
    repl_full = {n: (layer_grads[n][0] if n == "final_norm" else
                     jnp.stack([layer_grads[n][j] for j in range(w[n].shape[0])], axis=0)) for n in repl}
    tail_buf, tail_offs = pack_rows([repl_full[n] for n in repl], 0, 64)
    first = {n: g[n] for n in LAYER_SHARDED["ev"]}
    first["meta_tokens"] = jnp.sum(dh3[:, :N_META], axis=0)
    axes = {n: SHARD_AXIS[n] - 1 for n in SHARD_AXIS}
    axes["meta_tokens"] = SHARD_AXIS["meta_tokens"]
    last = ReduceComm(first, axes, c_idx, "first_layer", tail=tail_buf)
    last.run_alone("grad_first_layer")
    reduces.append((last, 0))

    red = {}
    for comm, idx in reduces:
        got, tail_piece = comm.results()
        for n, v_ in got.items():
            if n == "meta_tokens":
                red[n] = v_
            else:
                layer_grads[n][idx] = v_
    tails = own_block(run_stage(stage_gather_chips([tail_piece]), "grad_gather_replicated")[0], tail_piece)
    tails = tails.reshape(tail_buf.shape[0], LANE)
    for n, off in zip(repl, tail_offs):
        red[n] = unpack_rows(tails, off, w[n].shape)
    for n in SHARD_AXIS:
        if n != "meta_tokens":
            red[n] = jnp.stack([layer_grads[n][j] for j in range(w[n].shape[0])], axis=0)

    outs = [adamw(red[n], w[n], given["m_" + n], given["v_" + n], "adamw_" + n) for n in WEIGHTS]
    return (loss, grad_x, *[red[n] for n in WEIGHTS], *[o[0] for o in outs], *[o[1] for o in outs],
            *[o[2] for o in outs])
```

```python
import math

import jax
import jax.numpy as jnp
from jax import lax
from jax.experimental import pallas as pl
from jax.experimental.pallas import tpu as pltpu

F32 = jnp.float32
MXU_DT = jnp.bfloat16
S = jax.ShapeDtypeStruct
MESH = pl.DeviceIdType.MESH

EPS = 1e-6
D_MODEL = 1024
N_META = 16
DEPTH = 4
CONV_W = 512
LRU_W = 512
LRU_HEADS = 8
LRU_C = 8.0
EVEN_IN = 2560
MLA_HEADS = 16
QK_NOPE = 64
QK_ROPE = 32
QK_HEAD = 96
V_HEAD = 64
Q_LORA = 384
KV_LORA = 256
ROPE_BASE = 10000.0
D_FF = 2816
ODD_PAD = 896
ODD_CKV_COL = 2
ODD_KR_COL = 6
HP = 128
ATT_BLK = 384
Q_PRESCALE = QK_HEAD ** -0.5 * math.log2(math.e)
FFN_CT = 256
EV_CT = 256
STRIP_ROWS = 352
LANE = 128
SUBLANE = 8
VMEM_LIMIT_MB = 52

ADAM_LR = 0.001
ADAM_B1 = 0.9
ADAM_B2 = 0.999
ADAM_EPS = 1e-08
ADAM_WD = 0.01
ADAM_STEP = 10

NT_DIMS = (((1,), (1,)), ((), ()))
TN_DIMS = (((0,), (0,)), ((), ()))


def _cp(sem):
    return pltpu.CompilerParams(dimension_semantics=sem, vmem_limit_bytes=VMEM_LIMIT_MB << 20)


def _div_tile(n, cap, mult):
    if n <= cap:
        return n
    best = None
    for t in range(mult, cap + 1, mult):
        if n % t == 0:
            best = t
    assert best is not None, (n, cap, mult)
    return best


def _round_up(n, m):
    return -(-n // m) * m


def mat_cols(arr):
    return arr.shape[1] if arr.ndim == 2 else arr.shape[0] * arr.shape[2]


def mat_width(arr):
    return arr.shape[-1]


def mat_spec(arr, tm, tw, rc):
    if arr.ndim == 2:
        return pl.BlockSpec((tm, tw), lambda *g: rc(*g))
    per = arr.shape[2] // tw
    assert arr.shape[2] % tw == 0

    def imap(*g):
        r, c = rc(*g)
        return (c // per, r, c % per)

    return pl.BlockSpec((None, tm, tw), imap)


HBM_SPEC = pl.BlockSpec(memory_space=pltpu.HBM)


class Stage:
    def __init__(self, inputs, out_shapes, sems, start, finish):
        self.inputs, self.out_shapes, self.sems, self.start, self.finish = inputs, out_shapes, sems, start, finish


def run_stage(stage, name):
    n_in, n_out = len(stage.inputs), len(stage.out_shapes)

    def body(*refs):
        ins, outs, sems = refs[:n_in], refs[n_in:n_in + n_out], refs[n_in + n_out:]
        stage.start(ins, outs, sems)
        stage.finish(ins, outs, sems)

    return pl.pallas_call(body, out_shape=list(stage.out_shapes), in_specs=[HBM_SPEC] * n_in,
                          out_specs=[HBM_SPEC] * n_out, scratch_shapes=list(stage.sems), name=name)(*stage.inputs)


def _call(body, ops, carry, *, grid, in_specs, out_specs, out_shape, scratch_shapes, sem, name):
    if carry is None:
        outs = pl.pallas_call(body, grid=grid, in_specs=in_specs, out_specs=out_specs, out_shape=out_shape,
                              scratch_shapes=scratch_shapes, compiler_params=_cp(sem), name=name)(*ops)
        return outs, None
    multi = isinstance(out_shape, (list, tuple))
    shapes = list(out_shape) if multi else [out_shape]
    ospecs = list(out_specs) if multi else [out_specs]
    n_in, n_out, n_sc = len(ops), len(shapes), len(scratch_shapes)
    c_in, c_out = len(carry.inputs), len(carry.out_shapes)

    def wrapped(*refs):
        ins, cin = refs[:n_in], refs[n_in:n_in + c_in]
        o0 = n_in + c_in
        outs, cout = refs[o0:o0 + n_out], refs[o0 + n_out:o0 + n_out + c_out]
        s0 = o0 + n_out + c_out
        scs, csems = refs[s0:s0 + n_sc], refs[s0 + n_sc:]
        first = pl.program_id(0) == 0
        last = pl.program_id(0) == grid[0] - 1
        for d in range(1, len(grid)):
            first = first & (pl.program_id(d) == 0)
            last = last & (pl.program_id(d) == grid[d] - 1)

        @pl.when(first)
        def _():
            carry.start(cin, cout, csems)

        body(*ins, *outs, *scs)

        @pl.when(last)
        def _():
            carry.finish(cin, cout, csems)

    res = pl.pallas_call(
        wrapped, grid=grid, in_specs=list(in_specs) + [HBM_SPEC] * c_in, out_specs=ospecs + [HBM_SPEC] * c_out,
        out_shape=shapes + list(carry.out_shapes), scratch_shapes=list(scratch_shapes) + list(carry.sems),
        compiler_params=_cp(("arbitrary",) * len(grid)), name=name)(*ops, *carry.inputs)
    main = res[:n_out]
    return (list(main) if multi else main[0]), list(res[n_out:])


def norm_matmul(x, xcol, kdim, gain, w, tm, tn, out_dtype, name, epi=None, epi_ops=(), epi_specs=(), carry=None):
    rows, n = x.shape[0], mat_cols(w) if w.ndim == 3 else w.shape[1]
    n_epi = len(epi_ops)
    w_spec = (pl.BlockSpec((kdim, tn), lambda i, j: (0, j)) if w.ndim == 2 else
              pl.BlockSpec((None, kdim, tn), lambda i, j: (j // (w.shape[2] // tn), 0, j % (w.shape[2] // tn))))

    def body(x_ref, g_ref, w_ref, *rest):
        epi_refs = rest[:n_epi]
        out_ref, xn_ref, xn_sc = rest[n_epi:]

        @pl.when(pl.program_id(1) == 0)
        def _():
            xv = x_ref[...]
            y = xv * lax.rsqrt(jnp.mean(xv * xv, axis=-1, keepdims=True) + EPS)
            xn = (y * g_ref[...]).astype(MXU_DT)
            xn_sc[...] = xn
            xn_ref[...] = xn

        acc = jnp.dot(xn_sc[...], w_ref[...], preferred_element_type=F32)
        if epi is not None:
            acc = epi(acc, *[r[...] for r in epi_refs])
        out_ref[...] = acc.astype(out_dtype)

    return _call(
        body, (x, gain, w, *epi_ops), carry, grid=(rows // tm, n // tn),
        in_specs=[pl.BlockSpec((tm, kdim), lambda i, j: (i, xcol)), pl.BlockSpec((1, kdim), lambda i, j: (0, 0)),
                  w_spec, *epi_specs],
        out_specs=[pl.BlockSpec((tm, tn), lambda i, j: (i, j)), pl.BlockSpec((tm, kdim), lambda i, j: (i, 0))],
        out_shape=[S((rows, n), out_dtype), S((rows, kdim), MXU_DT)],
        scratch_shapes=[pltpu.VMEM((tm, kdim), MXU_DT)], sem=("parallel", "arbitrary"), name=name)


def matmul_res(a, w, res, tm, tn, name):
    grp, rows, k = a.shape
    n = w.shape[2]

    def body(a_ref, w_ref, r_ref, o_ref):
        acc = r_ref[...]
        for g in range(grp):
            acc = acc + jnp.dot(a_ref[g], w_ref[g], preferred_element_type=F32)
        o_ref[...] = acc

    return pl.pallas_call(
        body, grid=(rows // tm, n // tn),
        in_specs=[pl.BlockSpec((grp, tm, k), lambda i, j: (0, i, 0)), pl.BlockSpec((grp, k, tn), lambda i, j: (0, 0, j)),
                  pl.BlockSpec((tm, tn), lambda i, j: (i, j))],
        out_specs=pl.BlockSpec((tm, tn), lambda i, j: (i, j)),
        out_shape=S((rows, n), F32), compiler_params=_cp(("parallel", "parallel")), name=name)(a, w, res)


def matmul_nt(a, w, tm, tn, out_dtype, name):
    rows, k = a.shape
    n = w.shape[0]

    def body(a_ref, w_ref, o_ref):
        o_ref[...] = lax.dot_general(a_ref[...].astype(MXU_DT), w_ref[...], NT_DIMS,
                                     preferred_element_type=F32).astype(out_dtype)

    return pl.pallas_call(
        body, grid=(rows // tm, n // tn),
        in_specs=[pl.BlockSpec((tm, k), lambda i, j: (i, 0)), pl.BlockSpec((tn, k), lambda i, j: (j, 0))],
        out_specs=pl.BlockSpec((tm, tn), lambda i, j: (i, j)),
        out_shape=S((rows, n), out_dtype), compiler_params=_cp(("parallel", "parallel")), name=name)(a, w)


def matmul_nt_normbwd(du, w, x, xcol, gain, res, tm, tk, out_dtype, name):
    rows, kc = du.shape[-2], mat_cols(du)
    dn = w.shape[-2]
    nk = kc // tk
    has_res = res is not None
    w_spec = (pl.BlockSpec((dn, tk), lambda i, k: (0, k)) if w.ndim == 2 else
              pl.BlockSpec((None, dn, tk), lambda i, k: (k // (w.shape[2] // tk), 0, k % (w.shape[2] // tk))))

    def body(du_ref, w_ref, x_ref, g_ref, *rest):
        if has_res:
            res_ref, dx_ref, dg_ref, acc = rest
        else:
            dx_ref, dg_ref, acc = rest
        i, k = pl.program_id(0), pl.program_id(1)

        @pl.when(k == 0)
        def _():
            acc[...] = jnp.zeros_like(acc)

        @pl.when((i == 0) & (k == 0))
        def _():
            dg_ref[...] = jnp.zeros_like(dg_ref)

        acc[...] += lax.dot_general(du_ref[...], w_ref[...], NT_DIMS, preferred_element_type=F32)

        @pl.when(k == nk - 1)
        def _():
            dhn = acc[...]
            xv = x_ref[...]
            rstd = lax.rsqrt(jnp.mean(xv * xv, axis=-1, keepdims=True) + EPS)
            xhat = xv * rstd
            dg_ref[...] += jnp.sum(dhn * xhat, axis=0, keepdims=True)
            dxh = dhn * g_ref[...]
            dx = rstd * (dxh - xhat * jnp.mean(dxh * xhat, axis=-1, keepdims=True))
            if has_res:
                dx = dx + res_ref[...]
            dx_ref[...] = dx.astype(out_dtype)

    in_specs = [mat_spec(du, tm, tk, lambda i, k: (i, k)), w_spec,
                pl.BlockSpec((tm, dn), lambda i, k: (i, xcol)), pl.BlockSpec((1, dn), lambda i, k: (0, 0))]
    ops = [du, w, x, gain]
    if has_res:
        in_specs.append(pl.BlockSpec((tm, dn), lambda i, k: (i, 0)))
        ops.append(res)
    return pl.pallas_call(
        body, grid=(rows // tm, nk), in_specs=in_specs,
        out_specs=[pl.BlockSpec((tm, dn), lambda i, k: (i, 0)), pl.BlockSpec((1, dn), lambda i, k: (0, 0))],
        out_shape=[S((rows, dn), out_dtype), S((1, dn), F32)],
        scratch_shapes=[pltpu.VMEM((tm, dn), F32)],
        compiler_params=_cp(("arbitrary", "arbitrary")), name=name)(*ops)


def matmul_tn(a, b, tr, name, carry=None, col_shards=1):
    rows, ka, nb = a.shape[-2], mat_cols(a), mat_cols(b)
    ta = _div_tile(mat_width(a), 1536, LANE)
    tb = _div_tile(mat_width(b), 1536 if ta <= 1024 else 1024, LANE)
    nr = rows // tr
    if col_shards == 1:
        out_spec, out_shape = pl.BlockSpec((ta, tb), lambda i, j, r: (i, j)), S((ka, nb), F32)
    else:
        per = nb // col_shards // tb
        assert per * tb * col_shards == nb
        out_spec = pl.BlockSpec((None, ta, tb), lambda i, j, r: (j // per, i, j % per))
        out_shape = S((col_shards, ka, nb // col_shards), F32)

    def body(a_ref, b_ref, o_ref, acc):
        r = pl.program_id(2)

        @pl.when(r == 0)
        def _():
            acc[...] = jnp.zeros_like(acc)

        acc[...] += lax.dot_general(a_ref[...].astype(MXU_DT), b_ref[...].astype(MXU_DT), TN_DIMS,
                                    preferred_element_type=F32)

        @pl.when(r == nr - 1)
        def _():
            o_ref[...] = acc[...]

    return _call(
        body, (a, b), carry, grid=(ka // ta, nb // tb, nr),
        in_specs=[mat_spec(a, tr, ta, lambda i, j, r: (r, i)), mat_spec(b, tr, tb, lambda i, j, r: (r, j))],
        out_specs=out_spec, out_shape=out_shape, scratch_shapes=[pltpu.VMEM((ta, tb), F32)],
        sem=("parallel", "parallel", "arbitrary"), name=name)


def _sigmoid(x):
    return 1.0 / (1.0 + jnp.exp(-x))


def _sigmoid_by_tanh(x):
    return 0.5 * jnp.tanh(0.5 * x) + 0.5


def _log1p(e):
    return jnp.where(e < 1e-3, e * (1.0 - e * (0.5 - e * (1.0 / 3.0 - 0.25 * e))), jnp.log(1.0 + e))


def _softplus(x):
    return jnp.maximum(x, 0.0) + _log1p(jnp.exp(-jnp.abs(x)))


def _expm1(x):
    series = x * (1.0 + x * (0.5 + x * (1.0 / 6.0 + x * (1.0 / 24.0 + x * (1.0 / 120.0)))))
    return jnp.where(jnp.abs(x) < 0.1, series, jnp.exp(x) - 1.0)


_GELU_K = math.sqrt(2.0 / math.pi)
_GELU_C = 0.044715


def _gelu_and_grad(x):
    th = jnp.tanh(_GELU_K * (x + _GELU_C * x * x * x))
    g = 0.5 * x * (1.0 + th)
    dg = 0.5 * (1.0 + th) + 0.5 * x * (1.0 - th * th) * _GELU_K * (1.0 + 3.0 * _GELU_C * x * x)
    return g, dg


def _row_iota(shape):
    return lax.broadcasted_iota(jnp.int32, shape, 0)


def _scan_chunk_fwd(a_sc, u_sc, out_ref, hcar, n, width):
    rowi = _row_iota((SUBLANE, width))

    def step(c, hprev):
        r0 = pl.multiple_of(c * SUBLANE, SUBLANE)
        a = a_sc[pl.ds(r0, SUBLANE), :]
        u = u_sc[pl.ds(r0, SUBLANE), :]
        for d in (1, 2, 4):
            a_s = jnp.where(rowi >= d, pltpu.roll(a, d, axis=0), 1.0)
            u_s = jnp.where(rowi >= d, pltpu.roll(u, d, axis=0), 0.0)
            u = u + a * u_s
            a = a * a_s
        h = u + a * hprev
        out_ref[pl.ds(r0, SUBLANE), :] = h
        return jnp.broadcast_to(h[SUBLANE - 1:SUBLANE, :], (SUBLANE, width))

    hcar[...] = lax.fori_loop(0, n // SUBLANE, step, hcar[...], unroll=4)


def _scan_chunk_bwd(b_sc, d_sc, out_ref, gcar, n, width):
    rowi = _row_iota((SUBLANE, width))
    nc = n // SUBLANE

    def step(c, gnext):
        r0 = pl.multiple_of((nc - 1 - c) * SUBLANE, SUBLANE)
        b = b_sc[pl.ds(r0, SUBLANE), :]
        d = d_sc[pl.ds(r0, SUBLANE), :]
        for s in (1, 2, 4):
            keep = rowi < SUBLANE - s
            b_s = jnp.where(keep, pltpu.roll(b, SUBLANE - s, axis=0), 1.0)
            d_s = jnp.where(keep, pltpu.roll(d, SUBLANE - s, axis=0), 0.0)
            d = d + b * d_s
            b = b * b_s
        g = d + b * gnext
        out_ref[pl.ds(r0, SUBLANE), :] = g
        return jnp.broadcast_to(g[0:1, :], (SUBLANE, width))

    gcar[...] = lax.fori_loop(0, nc, step, gcar[...], unroll=4)


def even_mid_fwd(u, conv_a, conv_b, conv_b_bias, rw, rb, iw, ib, lam, nb, tp, n, name, carry=None):
    rows = u.shape[0]
    w = EV_CT
    nj = CONV_W // w
    nt = tp // n
    h8 = SUBLANE

    def body(gb_r, gc_r, xa_r, xb_r, gate_r, ca_w, cb_w, cb_b, rw_r, rb_r, iw_r, ib_r, lam_r,
             y_o, ca_o, xc_o, a_o, hs_o, pext, xext, hcar, a_sc, u_sc):
        t = pl.program_id(2)

        @pl.when(t == 0)
        def _():
            pext[0:h8, :] = jnp.zeros((h8, w), F32)
            xext[0:h8, :] = jnp.zeros((h8, w), F32)
            hcar[...] = jnp.zeros_like(hcar)

        p = gc_r[...] * xa_r[...]
        pext[h8:h8 + n, :] = p
        wa = ca_w[...]
        ca = wa[2:3, :] * p + wa[1:2, :] * pext[h8 - 1:h8 - 1 + n, :] + wa[0:1, :] * pext[h8 - 2:h8 - 2 + n, :]
        ca_o[...] = ca
        y_o[0] = (gb_r[...] * ca).astype(MXU_DT)
        pext[0:h8, :] = pext[n:n + h8, :]

        xb = xb_r[...]
        xext[h8:h8 + n, :] = xb
        wb = cb_w[...]
        xc = (wb[3:4, :] * xb + wb[2:3, :] * xext[h8 - 1:h8 - 1 + n, :] + wb[1:2, :] * xext[h8 - 2:h8 - 2 + n, :]
              + wb[0:1, :] * xext[h8 - 3:h8 - 3 + n, :]) + cb_b[...]
        xc_o[...] = xc
        xext[0:h8, :] = xext[n:n + h8, :]

        xcm = xc.astype(MXU_DT)
        r = _sigmoid(jnp.dot(xcm, rw_r[...], preferred_element_type=F32) + rb_r[...])
        ig = _sigmoid(jnp.dot(xcm, iw_r[...], preferred_element_type=F32) + ib_r[...])
        log_a = (-LRU_C) * r * _softplus(-lam_r[...])
        a = jnp.exp(log_a)
        mult = jnp.sqrt(-_expm1(2.0 * log_a))
        a_sc[...] = a
        a_o[...] = a
        u_sc[...] = mult * (ig * xc)
        _scan_chunk_fwd(a_sc, u_sc, hs_o, hcar, n, w)
        gel, _ = _gelu_and_grad(gate_r[...])
        y_o[1] = (gel * hs_o[...]).astype(MXU_DT)

    def ublk(off):
        return pl.BlockSpec((n, w), lambda j, b, t: (b * nt + t, off + j))

    def pblk(r_):
        return pl.BlockSpec((r_, w), lambda j, b, t: (0, j))

    act = pl.BlockSpec((n, w), lambda j, b, t: (b * nt + t, j))
    mat = pl.BlockSpec((w, w), lambda j, b, t: (j, j))
    return _call(
        body, (u, u, u, u, u, conv_a, conv_b, conv_b_bias, rw, rb, iw, ib, lam), carry, grid=(nj, nb, nt),
        in_specs=[ublk(0), ublk(nj), ublk(2 * nj), ublk(3 * nj), ublk(4 * nj), pblk(3), pblk(4), pblk(1),
                  mat, pblk(1), mat, pblk(1), pblk(1)],
        out_specs=[pl.BlockSpec((2, n, w), lambda j, b, t: (0, b * nt + t, j)), act, act, act, act],
        out_shape=[S((2, rows, CONV_W), MXU_DT), S((rows, CONV_W), F32), S((rows, LRU_W), F32), S((rows, LRU_W), F32),
                   S((rows, LRU_W), F32)],
        scratch_shapes=[pltpu.VMEM((n + h8, w), F32), pltpu.VMEM((n + h8, w), F32), pltpu.VMEM((h8, w), F32),
                        pltpu.VMEM((n, w), F32), pltpu.VMEM((n, w), F32)],
        sem=("parallel", "parallel", "arbitrary"), name=name)


def even_mid_bwd(u, dycat, ca, xc, a_sv, hs, conv_a, conv_b, rw, rb, iw, ib, lam, nb, tp, n, name, carry=None):
    rows = u.shape[0]
    w = EV_CT
    nj = CONV_W // w
    nt = tp // n
    h8 = SUBLANE

    def body(gb_r, gc_r, xa_r, xb_r, gate_r, dya_r, dyb_r, ca_r, xc_r, a_r, hs_r, hsp_r,
             ca_w, cb_w, rw_r, rb_r, iw_r, ib_r, lam_r,
             du_o, dca_w, dcb_w, dcb_b, drw, drb, diw, dib, dlam,
             aext, hext, dext, eext, gcar, b_sc, d_sc, g_sc):
        b, t = pl.program_id(1), pl.program_id(2)

        @pl.when((b == 0) & (t == 0))
        def _():
            for ref in (dca_w, dcb_w, dcb_b, drw, drb, diw, dib, dlam):
                ref[...] = jnp.zeros_like(ref)

        @pl.when(t == 0)
        def _():
            aext[n:n + h8, :] = jnp.zeros((h8, w), F32)
            dext[n:n + h8, :] = jnp.zeros((h8, w), F32)
            eext[n:n + h8, :] = jnp.zeros((h8, w), F32)
            gcar[...] = jnp.zeros_like(gcar)

        xc_v = xc_r[...]
        xcm = xc_v.astype(MXU_DT)
        r = _sigmoid(jnp.dot(xcm, rw_r[...], preferred_element_type=F32) + rb_r[...])
        ig = _sigmoid(jnp.dot(xcm, iw_r[...], preferred_element_type=F32) + ib_r[...])
        lam_v = lam_r[...]
        sp = _softplus(-lam_v)
        log_a = (-LRU_C) * r * sp
        a = a_r[...]
        mult = jnp.sqrt(-_expm1(2.0 * log_a))
        hs_v = hs_r[...]
        gel, dgel = _gelu_and_grad(gate_r[...])
        dyb = dyb_r[...]
        du_o[4] = (dyb * hs_v * dgel).astype(MXU_DT)

        aext[0:n, :] = a
        b_sc[...] = aext[1:1 + n, :]
        d_sc[...] = dyb * gel
        _scan_chunk_bwd(b_sc, d_sc, g_sc, gcar, n, w)
        aext[n:n + h8, :] = aext[0:h8, :]
        g = g_sc[...]

        hext[0:h8, :] = jnp.where(t == nt - 1, 0.0, hsp_r[...])
        hext[h8:h8 + n, :] = hs_v
        da = g * hext[h8 - 1:h8 - 1 + n, :]
        dmult = g * (ig * xc_v)
        di = g * mult * xc_v
        dxc = g * mult * ig
        dlog_a = da * a - dmult * (a * a) / mult
        dr = dlog_a * ((-LRU_C) * sp)
        dsp = jnp.sum(dlog_a * ((-LRU_C) * r), axis=0, keepdims=True)
        dlam[...] += dsp * (-_sigmoid(-lam_v))
        dzr = dr * r * (1.0 - r)
        dzi = di * ig * (1.0 - ig)
        dzr_m = dzr.astype(MXU_DT)
        dzi_m = dzi.astype(MXU_DT)
        dxc = (dxc + lax.dot_general(dzr_m, rw_r[...], NT_DIMS, preferred_element_type=F32)
               + lax.dot_general(dzi_m, iw_r[...], NT_DIMS, preferred_element_type=F32))
        drw[...] += lax.dot_general(xcm, dzr_m, TN_DIMS, preferred_element_type=F32)
        diw[...] += lax.dot_general(xcm, dzi_m, TN_DIMS, preferred_element_type=F32)
        drb[...] += jnp.sum(dzr, axis=0, keepdims=True)
        dib[...] += jnp.sum(dzi, axis=0, keepdims=True)
        dcb_b[...] += jnp.sum(dxc, axis=0, keepdims=True)

        xb = xb_r[...]
        dext[0:n, :] = dxc
        wb = cb_w[...]
        d1, d2, d3 = dext[1:1 + n, :], dext[2:2 + n, :], dext[3:3 + n, :]
        du_o[3] = (wb[3:4, :] * dxc + wb[2:3, :] * d1 + wb[1:2, :] * d2 + wb[0:1, :] * d3).astype(MXU_DT)
        dcb_w[3:4, :] += jnp.sum(xb * dxc, axis=0, keepdims=True)
        dcb_w[2:3, :] += jnp.sum(xb * d1, axis=0, keepdims=True)
        dcb_w[1:2, :] += jnp.sum(xb * d2, axis=0, keepdims=True)
        dcb_w[0:1, :] += jnp.sum(xb * d3, axis=0, keepdims=True)
        dext[n:n + h8, :] = dext[0:h8, :]

        gb, gc, xa = gb_r[...], gc_r[...], xa_r[...]
        dya = dya_r[...]
        du_o[0] = (dya * ca_r[...]).astype(MXU_DT)
        dca = dya * gb
        eext[0:n, :] = dca
        wa = ca_w[...]
        e1, e2 = eext[1:1 + n, :], eext[2:2 + n, :]
        dp = wa[2:3, :] * dca + wa[1:2, :] * e1 + wa[0:1, :] * e2
        p = gc * xa
        dca_w[2:3, :] += jnp.sum(p * dca, axis=0, keepdims=True)
        dca_w[1:2, :] += jnp.sum(p * e1, axis=0, keepdims=True)
        dca_w[0:1, :] += jnp.sum(p * e2, axis=0, keepdims=True)
        eext[n:n + h8, :] = eext[0:h8, :]
        du_o[1] = (dp * xa).astype(MXU_DT)
        du_o[2] = (dp * gc).astype(MXU_DT)

    def rt(b, t):
        return b * nt + (nt - 1 - t)

    def ublk(off):
        return pl.BlockSpec((n, w), lambda j, b, t: (rt(b, t), off + j))

    def pblk(r_):
        return pl.BlockSpec((r_, w), lambda j, b, t: (0, j))

    act = pl.BlockSpec((n, w), lambda j, b, t: (rt(b, t), j))
    n8 = n // h8
    hsp = pl.BlockSpec((h8, w), lambda j, b, t: (jnp.maximum(rt(b, t) * n8 - 1, 0), j))
    mat = pl.BlockSpec((w, w), lambda j, b, t: (j, j))
    return _call(
        body, (u, u, u, u, u, dycat, dycat, ca, xc, a_sv, hs, hs, conv_a, conv_b, rw, rb, iw, ib, lam), carry,
        grid=(nj, nb, nt),
        in_specs=[ublk(0), ublk(nj), ublk(2 * nj), ublk(3 * nj), ublk(4 * nj), ublk(0), ublk(nj), act, act, act, act,
                  hsp, pblk(3), pblk(4), mat, pblk(1), mat, pblk(1), pblk(1)],
        out_specs=[pl.BlockSpec((5, n, w), lambda j, b, t: (0, rt(b, t), j)), pblk(3), pblk(4), pblk(1),
                   mat, pblk(1), mat, pblk(1), pblk(1)],
        out_shape=[S((5, rows, CONV_W), MXU_DT), S((3, CONV_W), F32), S((4, LRU_W), F32), S((1, LRU_W), F32),
                   S((LRU_W, LRU_W), F32), S((1, LRU_W), F32), S((LRU_W, LRU_W), F32), S((1, LRU_W), F32),
                   S((1, LRU_W), F32)],
        scratch_shapes=[pltpu.VMEM((n + h8, w), F32)] * 4 + [pltpu.VMEM((h8, w), F32)] + [pltpu.VMEM((n, w), F32)] * 3,
        sem=("arbitrary", "arbitrary", "arbitrary"), name=name)


def ffn_mid_fwd(up, cw, cb, nb, tp, n, name):
    rows = up.shape[0]
    w = FFN_CT
    nj = D_FF // w
    nt = tp // n
    h8 = SUBLANE

    sr = STRIP_ROWS
    assert n % sr == 0, (n, sr)

    def body(xa_r, xg_r, w_r, b_r, u_o, y_o, halo):
        t = pl.program_id(2)

        @pl.when(t == 0)
        def _():
            halo[...] = jnp.zeros_like(halo)

        wv = (w_r[0], w_r[1])
        bv = (b_r[0], b_r[1])

        def strip(s, carry):
            r0 = pl.multiple_of(s * sr, sr)
            us, new = [], []
            for g, x_r in enumerate((xa_r, xg_r)):
                x = x_r[pl.ds(r0, sr), :].astype(F32)
                win = jnp.concatenate([carry[g], x], axis=0)
                x1 = pltpu.roll(win, 1, axis=0)[h8:, :]
                x2 = pltpu.roll(win, 2, axis=0)[h8:, :]
                u = (wv[g][2:3, :] * x + wv[g][1:2, :] * x1 + wv[g][0:1, :] * x2) + bv[g]
                u_o[g, pl.ds(r0, sr), :] = u.astype(MXU_DT)
                us.append(u)
                new.append(x[sr - h8:, :])
            y_o[pl.ds(r0, sr), :] = (us[0] * _sigmoid_by_tanh(us[0]) * us[1]).astype(MXU_DT)
            return tuple(new)

        ha, hg = lax.fori_loop(0, n // sr, strip, (halo[0], halo[1]))
        halo[0] = ha
        halo[1] = hg

    def ublk(off):
        return pl.BlockSpec((n, w), lambda j, b, t: (b * nt + t, off + j))

    return pl.pallas_call(
        body, grid=(nj, nb, nt),
        in_specs=[ublk(0), ublk(nj), pl.BlockSpec((2, 3, w), lambda j, b, t: (0, 0, j)),
                  pl.BlockSpec((2, 1, w), lambda j, b, t: (0, 0, j))],
        out_specs=[pl.BlockSpec((2, n, w), lambda j, b, t: (0, b * nt + t, j)), ublk(0)],
        out_shape=[S((2, rows, D_FF), MXU_DT), S((rows, D_FF), MXU_DT)],
        scratch_shapes=[pltpu.VMEM((2, h8, w), F32)],
        compiler_params=_cp(("parallel", "parallel", "arbitrary")), name=name,
    )(up, up, cw, cb)


def ffn_mid_bwd(dy, u, up, cw, nb, tp, n, name, carry=None):
    rows = up.shape[0]
    w = FFN_CT
    nj = D_FF // w
    nt = tp // n
    h8 = SUBLANE

    sr = STRIP_ROWS
    assert n % sr == 0, (n, sr)
    ns = n // sr

    def fold(v):
        acc = v[0:h8, :]
        for k in range(1, sr // h8):
            acc = acc + v[k * h8:(k + 1) * h8, :]
        return acc

    def body(dy_r, u_r, xa_r, xg_r, w_r, dx_o, dw, db, halo):
        b, t = pl.program_id(1), pl.program_id(2)

        @pl.when((b == 0) & (t == 0))
        def _():
            dw[...] = jnp.zeros_like(dw)
            db[...] = jnp.zeros_like(db)

        @pl.when(t == 0)
        def _():
            halo[...] = jnp.zeros_like(halo)

        wv = (w_r[0], w_r[1])

        def strip(s, carry):
            halos, sums = carry
            r0 = pl.multiple_of((ns - 1 - s) * sr, sr)
            dyv = dy_r[pl.ds(r0, sr), :].astype(F32)
            ua = u_r[0, pl.ds(r0, sr), :].astype(F32)
            ug = u_r[1, pl.ds(r0, sr), :].astype(F32)
            sg = _sigmoid_by_tanh(ua)
            dus = (dyv * ug * (sg * (1.0 + ua * (1.0 - sg))), dyv * (ua * sg))
            new_halos, new_sums = [], []
            for g, x_r in enumerate((xa_r, xg_r)):
                du = dus[g]
                win = jnp.concatenate([du, halos[g]], axis=0)
                d1 = pltpu.roll(win, sr + h8 - 1, axis=0)[0:sr, :]
                d2 = pltpu.roll(win, sr + h8 - 2, axis=0)[0:sr, :]
                dx_o[g, pl.ds(r0, sr), :] = (wv[g][2:3, :] * du + wv[g][1:2, :] * d1 + wv[g][0:1, :] * d2).astype(MXU_DT)
                x = x_r[pl.ds(r0, sr), :].astype(F32)
                s2, s1, s0, sb = sums[g]
                new_sums.append((s2 + fold(x * du), s1 + fold(x * d1), s0 + fold(x * d2), sb + fold(du)))
                new_halos.append(du[0:h8, :])
            return tuple(new_halos), tuple(new_sums)

        z = jnp.zeros((h8, w), F32)
        halos, sums = lax.fori_loop(0, ns, strip, ((halo[0], halo[1]), ((z, z, z, z), (z, z, z, z))))
        halo[0] = halos[0]
        halo[1] = halos[1]
        for g in range(2):
            s2, s1, s0, sb = sums[g]
            dw[g, 2:3, :] += jnp.sum(s2, axis=0, keepdims=True)
            dw[g, 1:2, :] += jnp.sum(s1, axis=0, keepdims=True)
            dw[g, 0:1, :] += jnp.sum(s0, axis=0, keepdims=True)
            db[g] += jnp.sum(sb, axis=0, keepdims=True)

    def rt(b, t):
        return b * nt + (nt - 1 - t)

    def ublk(off):
        return pl.BlockSpec((n, w), lambda j, b, t: (rt(b, t), off + j))

    pair = pl.BlockSpec((2, n, w), lambda j, b, t: (0, rt(b, t), j))
    return _call(
        body, (dy, u, up, up, cw), carry, grid=(nj, nb, nt),
        in_specs=[ublk(0), pair, ublk(0), ublk(nj), pl.BlockSpec((2, 3, w), lambda j, b, t: (0, 0, j))],
        out_specs=[pair, pl.BlockSpec((2, 3, w), lambda j, b, t: (0, 0, j)),
                   pl.BlockSpec((2, 1, w), lambda j, b, t: (0, 0, j))],
        out_shape=[S((2, rows, D_FF), MXU_DT), S((2, 3, D_FF), F32), S((2, 1, D_FF), F32)],
        scratch_shapes=[pltpu.VMEM((2, h8, w), F32)],
        sem=("arbitrary", "arbitrary", "arbitrary"), name=name)


def _lane_mod(shape):
    return lax.broadcasted_iota(jnp.int32, shape, 1) & (HP - 1)


def _q_rope_epi(acc, tab):
    reps = acc.shape[1] // HP
    a = acc * jnp.tile(tab, (1, reps))
    lane = _lane_mod(a.shape)
    shifted = pltpu.roll(a, a.shape[1] - QK_ROPE, axis=1)
    return jnp.where(lane < QK_NOPE, a, jnp.where(lane < QK_HEAD, a + shifted, 0.0)) * Q_PRESCALE


def _k_rope_block(krblk, tabk):
    a = krblk * tabk
    lane = _lane_mod(a.shape)
    b = a + pltpu.roll(a, HP - QK_ROPE, axis=1)
    return jnp.where((lane >= QK_NOPE) & (lane < QK_HEAD), b, 0.0)


def _k_rope_epi(acc, krblk, tabk):
    reps = acc.shape[1] // HP
    return acc + jnp.tile(_k_rope_block(krblk, tabk), (1, reps))


def attn_fwd(q, k, v, nb, tp, name, carry=None):
    rows = q.shape[0]
    blk = ATT_BLK
    nq = tp // blk
    npair = MLA_HEADS // 2

    def body(q_r, k_r, v_r, o_r, lse_r):
        qi = pl.program_id(2)
        lane = lax.broadcasted_iota(jnp.int32, (blk, LANE), 1)
        even = lane < V_HEAD
        sum_lane = (V_HEAD, 0)
        rowi = lax.broadcasted_iota(jnp.int32, (blk, blk), 0)
        coli = lax.broadcasted_iota(jnp.int32, (blk, blk), 1)
        qs = [q_r[:, h * HP:(h + 1) * HP] for h in range(2)]

        def kv_block(k0, width, carry, visible):
            ms, accs = carry
            vblk = v_r[pl.ds(k0, width), :]
            one = jnp.ones_like(vblk)
            zero = jnp.zeros_like(vblk)
            vlane = lax.broadcasted_iota(jnp.int32, (width, LANE), 1)
            ss = [lax.dot_general(qs[h], k_r[pl.ds(k0, width), h * HP:(h + 1) * HP], NT_DIMS,
                                  preferred_element_type=F32) for h in range(2)]
            new_ms, new_accs = [], []
            for h in range(2):
                s = ss[h]
                if visible is not None:
                    s = jnp.where(visible, s, -jnp.inf)
                m_new = jnp.maximum(ms[h], jnp.max(s, axis=1, keepdims=True))
                alpha = jnp.exp2(ms[h] - m_new)
                p = jnp.exp2(s - m_new).astype(MXU_DT)
                mine = (vlane < V_HEAD) if h == 0 else (vlane >= V_HEAD)
                vh = jnp.where(mine, vblk, jnp.where(vlane == sum_lane[h], one, zero))
                new_accs.append(alpha * accs[h] + jnp.dot(p, vh, preferred_element_type=F32))
                new_ms.append(m_new)
            return tuple(new_ms), tuple(new_accs)

        neg = jnp.full((blk, 1), -jnp.inf, F32)
        zacc = jnp.zeros((blk, LANE), F32)
        carry = lax.fori_loop(0, qi // 2, lambda i, c: kv_block(pl.multiple_of(i * 2 * blk, blk), 2 * blk, c, None),
                              ((neg, neg), (zacc, zacc)))
        rowi2 = lax.broadcasted_iota(jnp.int32, (blk, 2 * blk), 0)
        coli2 = lax.broadcasted_iota(jnp.int32, (blk, 2 * blk), 1)
        ms, accs = lax.cond(
            qi % 2 == 1,
            lambda c: kv_block(pl.multiple_of((qi - 1) * blk, blk), 2 * blk, c, coli2 - blk <= rowi2),
            lambda c: kv_block(pl.multiple_of(qi * blk, blk), blk, c, coli <= rowi), carry)
        ls = [accs[h][:, sum_lane[h]:sum_lane[h] + 1] for h in range(2)]
        o_r[...] = jnp.where(even, accs[0] / ls[0], accs[1] / ls[1]).astype(MXU_DT)
        lse_r[...] = jnp.where(even, ms[0] + jnp.log2(ls[0]), ms[1] + jnp.log2(ls[1]))

    return _call(
        body, (q, k, v), carry, grid=(nb, npair, nq),
        in_specs=[pl.BlockSpec((blk, 2 * HP), lambda b, p, i: (b * nq + i, p)),
                  pl.BlockSpec((tp, 2 * HP), lambda b, p, i: (b, p)),
                  pl.BlockSpec((tp, LANE), lambda b, p, i: (b, p))],
        out_specs=[pl.BlockSpec((blk, LANE), lambda b, p, i: (b * nq + i, p)),
                   pl.BlockSpec((None, blk, LANE), lambda b, p, i: (p, b * nq + i, 0))],
        out_shape=[S((rows, MLA_HEADS * V_HEAD), MXU_DT), S((npair, rows, LANE), F32)], scratch_shapes=[],
        sem=("parallel", "parallel", "arbitrary"), name=name)


def attn_bwd(q, k, v, o, do, lse, nb, tp, name, carry=None):
    rows = q.shape[0]
    blk = ATT_BLK
    nq = tp // blk
    npair = MLA_HEADS // 2
    scale = QK_HEAD ** -0.5

    def body(q_r, k_r, v_r, o_r, do_r, lse_r, dq_o, dk_o, dv_o, dq_acc, delta_sc):
        kb = pl.program_id(2)
        even = lax.broadcasted_iota(jnp.int32, (blk, LANE), 1) < V_HEAD
        rowi = lax.broadcasted_iota(jnp.int32, (blk, blk), 0)
        coli = lax.broadcasted_iota(jnp.int32, (blk, blk), 1)

        @pl.when(kb == 0)
        def _():
            dq_acc[...] = jnp.zeros_like(dq_acc)

            def dstep(i, c):
                r0 = pl.multiple_of(i * blk, blk)
                prod = do_r[pl.ds(r0, blk), :].astype(F32) * o_r[pl.ds(r0, blk), :].astype(F32)
                de = jnp.sum(jnp.where(even, prod, 0.0), axis=1, keepdims=True)
                dd = jnp.sum(jnp.where(even, 0.0, prod), axis=1, keepdims=True)
                delta_sc[pl.ds(r0, blk), :] = jnp.where(even, de, dd)
                return c

            lax.fori_loop(0, nq, dstep, 0)

        vblk = v_r[...]
        ks = [k_r[:, h * HP:(h + 1) * HP] for h in range(2)]

        def q_block(r0, height, carry, visible):
            dk0, dk1, dv = carry
            dob = do_r[pl.ds(r0, height), :]
            lse_b = lse_r[pl.ds(r0, height), :]
            dl_b = delta_sc[pl.ds(r0, height), :]
            qlane = lax.broadcasted_iota(jnp.int32, (height, LANE), 1)
            dks = [dk0, dk1]
            qhs = [q_r[pl.ds(r0, height), h * HP:(h + 1) * HP] for h in range(2)]
            dohs = [jnp.where((qlane < V_HEAD) if h == 0 else (qlane >= V_HEAD), dob, jnp.zeros_like(dob))
                    for h in range(2)]
            ss = [lax.dot_general(qhs[h], ks[h], NT_DIMS, preferred_element_type=F32) for h in range(2)]
            dps = [lax.dot_general(dohs[h], vblk, NT_DIMS, preferred_element_type=F32) for h in range(2)]
            for h in range(2):
                lo = 0 if h == 0 else V_HEAD
                p = jnp.exp2(ss[h] - lse_b[:, lo:lo + 1])
                if visible is not None:
                    p = jnp.where(visible, p, 0.0)
                ds = (p * (dps[h] - dl_b[:, lo:lo + 1])).astype(MXU_DT)
                dv = dv + lax.dot_general(p.astype(MXU_DT), dohs[h], TN_DIMS, preferred_element_type=F32)
                dks[h] = dks[h] + lax.dot_general(ds, qhs[h], TN_DIMS, preferred_element_type=F32)
                dq_acc[pl.ds(r0, height), h * HP:(h + 1) * HP] += jnp.dot(ds, ks[h], preferred_element_type=F32)
            return dks[0], dks[1], dv

        z = jnp.zeros((blk, HP), F32)
        below = nq - 1 - kb
        odd = below % 2
        rowi2 = lax.broadcasted_iota(jnp.int32, (2 * blk, blk), 0)
        coli2 = lax.broadcasted_iota(jnp.int32, (2 * blk, blk), 1)
        first = pl.multiple_of(kb * blk, blk)
        carry = lax.cond(odd == 1, lambda c: q_block(first, 2 * blk, c, coli2 <= rowi2),
                         lambda c: q_block(first, blk, c, coli <= rowi), (z, z, jnp.zeros((blk, LANE), F32)))
        dk0, dk1, dv = lax.fori_loop(
            0, below // 2, lambda i, c: q_block(pl.multiple_of((kb + 1 + odd + 2 * i) * blk, blk), 2 * blk, c, None),
            carry)
        dk_o[:, 0:HP] = (dk0 * (scale / Q_PRESCALE)).astype(MXU_DT)
        dk_o[:, HP:2 * HP] = (dk1 * (scale / Q_PRESCALE)).astype(MXU_DT)
        dv_o[...] = dv.astype(MXU_DT)

        @pl.when(kb == nq - 1)
        def _():
            dq_o[...] = (dq_acc[...] * scale).astype(MXU_DT)

    seq_pair = pl.BlockSpec((tp, LANE), lambda b, p, kk: (b, p))
    return _call(
        body, (q, k, v, o, do, lse), carry, grid=(nb, npair, nq),
        in_specs=[pl.BlockSpec((tp, 2 * HP), lambda b, p, kk: (b, p)),
                  pl.BlockSpec((blk, 2 * HP), lambda b, p, kk: (b * nq + kk, p)),
                  pl.BlockSpec((blk, LANE), lambda b, p, kk: (b * nq + kk, p)),
                  seq_pair, seq_pair, pl.BlockSpec((None, tp, LANE), lambda b, p, kk: (p, b, 0))],
        out_specs=[pl.BlockSpec((tp, 2 * HP), lambda b, p, kk: (b, p)),
                   pl.BlockSpec((blk, 2 * HP), lambda b, p, kk: (b * nq + kk, p)),
                   pl.BlockSpec((blk, LANE), lambda b, p, kk: (b * nq + kk, p))],
        out_shape=[S((rows, MLA_HEADS * HP), MXU_DT), S((rows, MLA_HEADS * HP), MXU_DT),
                   S((rows, MLA_HEADS * V_HEAD), MXU_DT)],
        scratch_shapes=[pltpu.VMEM((tp, 2 * HP), F32), pltpu.VMEM((tp, LANE), F32)],
        sem=("parallel", "parallel", "arbitrary"), name=name)


def rope_bwd(dq, dk, dv, tabq, tabk, tp, tm, name):
    rows = dq.shape[0]
    nt = tp // tm
    wq = MLA_HEADS * HP

    def body(dq_r, dk_r, dv_r, tq_r, tk_r, dqa_o, dkv_o, dkr_o):
        dqv = dq_r[...].astype(F32)
        lane = _lane_mod(dqv.shape)
        in_rope = (lane >= QK_NOPE) & (lane < QK_HEAD)
        rope = jnp.where(in_rope, dqv, 0.0)
        da = jnp.where(lane < QK_HEAD, dqv, 0.0) + pltpu.roll(rope, QK_ROPE, axis=1)
        dqa_o[...] = (da * jnp.tile(tq_r[...], (1, MLA_HEADS))).astype(MXU_DT)
        dkf = dk_r[...].astype(F32)
        dkv_o[:, 0:wq] = jnp.where(lane < QK_NOPE, dkf, 0.0).astype(MXU_DT)
        dkv_o[:, wq:] = dv_r[...]
        kr = jnp.where(in_rope, dkf, 0.0)
        tot = kr[:, 0:HP]
        for h in range(1, MLA_HEADS):
            tot = tot + kr[:, h * HP:(h + 1) * HP]
        dkr_o[...] = ((tot + pltpu.roll(tot, QK_ROPE, axis=1)) * tk_r[...]).astype(MXU_DT)

    def rowblk(wd):
        return pl.BlockSpec((tm, wd), lambda i: (i, 0))

    tab = pl.BlockSpec((tm, HP), lambda i: (i % nt, 0))
    return pl.pallas_call(
        body, grid=(rows // tm,), in_specs=[rowblk(wq), rowblk(wq), rowblk(MLA_HEADS * V_HEAD), tab, tab],
        out_specs=[rowblk(wq), rowblk(wq + MLA_HEADS * V_HEAD), rowblk(HP)],
        out_shape=[S((rows, wq), MXU_DT), S((rows, wq + MLA_HEADS * V_HEAD), MXU_DT), S((rows, HP), MXU_DT)],
        compiler_params=_cp(("parallel",)), name=name)(dq, dk, dv, tabq, tabk)


def loss_head(h, target, gain, tp, t_real, tm, name):
    rows = h.shape[0]
    nt = tp // tm

    def body(h_r, t_r, g_r, dh_o, loss_o, dg_o):
        i = pl.program_id(0)

        @pl.when(i == 0)
        def _():
            loss_o[...] = jnp.zeros_like(loss_o)
            dg_o[...] = jnp.zeros_like(dg_o)

        xv = h_r[...]
        rstd = lax.rsqrt(jnp.mean(xv * xv, axis=-1, keepdims=True) + EPS)
        xhat = xv * rstd
        g = g_r[...]
        pos = (i % nt) * tm + lax.broadcasted_iota(jnp.int32, (tm, 1), 0)
        valid = (pos >= N_META) & (pos < t_real)
        err = jnp.where(valid, xhat * g - t_r[...], 0.0)
        loss_o[...] += 0.5 * jnp.sum(jnp.mean(err * err, axis=-1, keepdims=True))
        dy = err * (1.0 / D_MODEL)
        dg_o[...] += jnp.sum(dy * xhat, axis=0, keepdims=True)
        dxh = dy * g
        dh_o[...] = rstd * (dxh - xhat * jnp.mean(dxh * xhat, axis=-1, keepdims=True))

    blk = pl.BlockSpec((tm, D_MODEL), lambda i: (i, 0))
    return pl.pallas_call(
        body, grid=(rows // tm,), in_specs=[blk, blk, pl.BlockSpec((1, D_MODEL), lambda i: (0, 0))],
        out_specs=[blk, pl.BlockSpec((1, LANE), lambda i: (0, 0)), pl.BlockSpec((1, D_MODEL), lambda i: (0, 0))],
        out_shape=[S((rows, D_MODEL), F32), S((1, LANE), F32), S((1, D_MODEL), F32)],
        compiler_params=_cp(("arbitrary",)), name=name)(h, target, gain)


ADAM_TILE_ELEMS = 128 * 1024


def adamw(g, w, m, v, name):
    shape = w.shape
    cols = shape[-1]
    rws = max(1, math.prod(shape[:-1]))
    tr = rws if rws * cols <= ADAM_TILE_ELEMS else _div_tile(rws, max(SUBLANE, ADAM_TILE_ELEMS // cols), SUBLANE)
    bc1 = 1.0 - ADAM_B1 ** ADAM_STEP
    bc2 = 1.0 - ADAM_B2 ** ADAM_STEP

    def body(g_r, w_r, m_r, v_r, do, mo, vo):
        gv = g_r[...]
        mn = ADAM_B1 * m_r[...] + (1.0 - ADAM_B1) * gv
        vn = ADAM_B2 * v_r[...] + (1.0 - ADAM_B2) * (gv * gv)
        m_hat = mn / bc1
        v_hat = vn / bc2
        do[...] = -ADAM_LR * (m_hat / (jnp.sqrt(v_hat) + ADAM_EPS) + ADAM_WD * w_r[...])
        mo[...] = mn
        vo[...] = vn

    blk = pl.BlockSpec((tr, cols), lambda i: (i, 0))
    outs = pl.pallas_call(
        body, grid=(rws // tr,), in_specs=[blk] * 4, out_specs=[blk] * 3, out_shape=[S((rws, cols), F32)] * 3,
        compiler_params=_cp(("parallel",)), name=name)(*[a.reshape(rws, cols) for a in (g, w, m, v)])
    return tuple(o.reshape(shape) for o in outs)


SUM_TILE_ELEMS = 128 * 1024


def _place():
    return lax.axis_index("x"), lax.axis_index("y"), lax.axis_index("c")


def _remote(src, dst, send_sems, recv_sems, k, to):
    return pltpu.make_async_remote_copy(src_ref=src, dst_ref=dst, send_sem=send_sems.at[k], recv_sem=recv_sems.at[k],
                                        device_id=to, device_id_type=MESH)


def chip_index():
    return 2 * lax.axis_index("x") + lax.axis_index("y")


def _sem_pair(n):
    return [pltpu.SemaphoreType.DMA((n,)), pltpu.SemaphoreType.DMA((n,))]


def stage_gather_chips(xs):
    def copies(ins, outs, sems):
        send_sems, recv_sems = sems
        mx, my, mc = _place()
        sibling = (mx, my, 1 - mc)
        chips = [(1 - mx, my), (mx, 1 - my), (1 - mx, 1 - my)]
        first, landed, passed, from_sibling = [], [], [], []
        for i, (x_ref, out_ref) in enumerate(zip(ins, outs)):
            def piece(cx, cy, h, out_ref=out_ref):
                return out_ref.at[2 * cx + cy, h]

            for j, (cx, cy) in enumerate(chips):
                k = 6 * i + j
                first.append(_remote(x_ref.at[mc], piece(mx, my, mc), send_sems, recv_sems, k, (cx, cy, mc)))
                landed.append(_remote(x_ref.at[mc], piece(cx, cy, mc), send_sems, recv_sems, k, (cx, cy, mc)))
                passed.append(_remote(piece(cx, cy, mc), piece(cx, cy, mc), send_sems, recv_sems, k + 3, sibling))
                from_sibling.append(_remote(x_ref.at[mc], piece(cx, cy, 1 - mc), send_sems, recv_sems, k + 3, sibling))
        return first, landed, passed, from_sibling

    def start(ins, outs, sems):
        for cp in copies(ins, outs, sems)[0]:
            cp.start()

    def finish(ins, outs, sems):
        first, landed, passed, from_sibling = copies(ins, outs, sems)
        for arrived, onward in zip(landed, passed):
            arrived.wait_recv()
            onward.start()
        for cp in from_sibling:
            cp.wait_recv()
        for cp in first + passed:
            cp.wait_send()

    return Stage(list(xs), [S((4,) + x.shape, x.dtype) for x in xs], _sem_pair(6 * len(xs)), start, finish)


def own_block(gathered, xs):
    return lax.dynamic_update_slice(gathered, xs[None], (chip_index(), 0, 0, 0))


def stage_pair_exchange(gs):
    def copies(ins, outs, sems):
        send_sems, recv_sems = sems
        mx, my, mc = _place()
        return [_remote(g_ref.at[s, 1 - mc], land_ref.at[s], send_sems, recv_sems, 4 * i + s, (mx, my, 1 - mc))
                for i, (g_ref, land_ref) in enumerate(zip(ins, outs)) for s in range(4)]

    def start(ins, outs, sems):
        for cp in copies(ins, outs, sems):
            cp.start()

    def finish(ins, outs, sems):
        cps = copies(ins, outs, sems)
        for cp in cps:
            cp.wait_recv()
        for cp in cps:
            cp.wait_send()

    return Stage(list(gs), [S((4,) + g.shape[2:], g.dtype) for g in gs], _sem_pair(4 * len(gs)), start, finish)


def _sum_rows(rws, width):
    return _div_tile(rws, max(SUBLANE, SUM_TILE_ELEMS // width), SUBLANE)


def pair_sum(g4, land, c_idx, name):
    _, _, rws, wd = g4.shape
    th = _sum_rows(rws, wd)

    def body(c_ref, a_ref, b_ref, o_ref):
        o_ref[...] = a_ref[...] + b_ref[...]

    return pl.pallas_call(
        body,
        grid_spec=pltpu.PrefetchScalarGridSpec(
            num_scalar_prefetch=1, grid=(4, rws // th),
            in_specs=[pl.BlockSpec((None, None, th, wd), lambda s, i, c: (s, c[0], i, 0)),
                      pl.BlockSpec((None, th, wd), lambda s, i, c: (s, i, 0))],
            out_specs=pl.BlockSpec((None, th, wd), lambda s, i, c: (s, i, 0))),
        out_shape=S((4, rws, wd), F32), compiler_params=_cp(("parallel", "parallel")), name=name)(c_idx, g4, land)


def stage_chip_scatter(ps):
    def copies(ins, outs, sems):
        send_sems, recv_sems = sems
        mx, my, mc = _place()
        me = 2 * mx + my
        chips = [(1 - mx, my), (mx, 1 - my), (1 - mx, 1 - my)]
        sent, landed = [], []
        for i, (p_ref, land_ref) in enumerate(zip(ins, outs)):
            for j, (cx, cy) in enumerate(chips):
                k = 3 * i + j
                sent.append(_remote(p_ref.at[2 * cx + cy], land_ref.at[me], send_sems, recv_sems, k, (cx, cy, mc)))
                landed.append(_remote(p_ref.at[me], land_ref.at[2 * cx + cy], send_sems, recv_sems, k, (cx, cy, mc)))
        return sent, landed

    def start(ins, outs, sems):
        for cp in copies(ins, outs, sems)[0]:
            cp.start()

    def finish(ins, outs, sems):
        sent, landed = copies(ins, outs, sems)
        for cp in landed:
            cp.wait_recv()
        for cp in sent:
            cp.wait_send()

    return Stage(list(ps), [S(p.shape, p.dtype) for p in ps], _sem_pair(3 * len(ps)), start, finish)


def chip_sum(l4, p4, me_idx, name):
    _, rws, wd = l4.shape
    th = _sum_rows(rws, wd)

    def body(me_ref, a, b, c, d, own, o_ref):
        me = me_ref[0]
        parts = [jnp.where(me == s, own[...], r[...]) for s, r in enumerate((a, b, c, d))]
        o_ref[...] = ((parts[0] + parts[1]) + parts[2]) + parts[3]

    def blk(s):
        return pl.BlockSpec((None, th, wd), lambda i, me: (jnp.where(me[0] == s, (s + 1) % 4, s), i, 0))

    return pl.pallas_call(
        body,
        grid_spec=pltpu.PrefetchScalarGridSpec(
            num_scalar_prefetch=1, grid=(rws // th,),
            in_specs=[blk(0), blk(1), blk(2), blk(3), pl.BlockSpec((None, th, wd), lambda i, me: (me[0], i, 0))],
            out_specs=pl.BlockSpec((th, wd), lambda i, me: (i, 0))),
        out_shape=S((rws, wd), F32), compiler_params=_cp(("parallel",)), name=name)(me_idx, l4, l4, l4, l4, p4)


def stage_pair_gather(rs):
    def copies(ins, outs, sems):
        send_sems, recv_sems = sems
        mx, my, mc = _place()
        return [_remote(r_ref, out_ref, send_sems, recv_sems, i, (mx, my, 1 - mc))
                for i, (r_ref, out_ref) in enumerate(zip(ins, outs))]

    def start(ins, outs, sems):
        for cp in copies(ins, outs, sems):
            cp.start()

    def finish(ins, outs, sems):
        for cp in copies(ins, outs, sems):
            cp.wait()

    return Stage(list(rs), [S(r.shape, r.dtype) for r in rs], _sem_pair(len(rs)), start, finish)


PACK_ELEMS = 16 * LANE


def pack_rows(arrays, lead, total_mult):
    parts, offs, r0 = [], [], 0
    for a in arrays:
        flat = a.reshape(a.shape[:lead] + (-1,))
        elems = _round_up(flat.shape[-1], PACK_ELEMS)
        flat = jnp.pad(flat, [(0, 0)] * lead + [(0, elems - flat.shape[-1])])
        parts.append(flat.reshape(flat.shape[:lead] + (elems // LANE, LANE)))
        offs.append((r0, elems // LANE))
        r0 += elems // LANE
    total = _round_up(r0, total_mult)
    if total > r0:
        parts.append(jnp.zeros(parts[0].shape[:lead] + (total - r0, LANE), parts[0].dtype))
    return jnp.concatenate(parts, axis=lead), offs


def unpack_rows(buf, off, shape):
    r0, nr = off
    lead = buf.shape[:-2]
    n = math.prod(shape)
    return buf[..., r0:r0 + nr, :].reshape(lead + (nr * LANE,))[..., :n].reshape(lead + tuple(shape))


def unshard(stacked, axis):
    x = jnp.moveaxis(stacked, 0, axis)
    return x.reshape(x.shape[:axis] + (4 * x.shape[axis + 1],) + x.shape[axis + 2:])


def to_shards(full, axis):
    n = full.shape[axis] // 4
    x = full.reshape(full.shape[:axis] + (4, n) + full.shape[axis + 1:])
    return jnp.moveaxis(x, axis, 0)


def _rot_cols(w):
    half = w.shape[-1] // 2
    return jnp.concatenate([-w[..., half:], w[..., :half]], axis=-1)


def _unrot_cols(dw):
    half = dw.shape[-1] // 2
    return jnp.concatenate([dw[..., half:], -dw[..., :half]], axis=-1)


def odd_w_in_padded(w_in):
    kr = w_in[:, Q_LORA + KV_LORA:]
    rows = w_in.shape[0]
    return jnp.concatenate([w_in[:, :Q_LORA], jnp.zeros((rows, 128), w_in.dtype), w_in[:, Q_LORA:Q_LORA + KV_LORA],
                            jnp.zeros((rows, 64), w_in.dtype), kr, _rot_cols(kr)], axis=1)


def odd_w_in_unpad(dwp):
    base = 512 + KV_LORA + 64
    dkr = dwp[:, base:base + QK_ROPE] + _unrot_cols(dwp[:, base + QK_ROPE:base + 2 * QK_ROPE])
    return jnp.concatenate([dwp[:, :Q_LORA], dwp[:, 512:512 + KV_LORA], dkr], axis=1)


def uq_padded(w_uq):
    w = w_uq.reshape(Q_LORA, MLA_HEADS, QK_HEAD)
    return jnp.concatenate([w, _rot_cols(w[:, :, QK_NOPE:])], axis=-1).reshape(Q_LORA, MLA_HEADS * HP)


def uq_unpad(dwp):
    d = dwp.reshape(Q_LORA, MLA_HEADS, HP)
    rope = d[:, :, QK_NOPE:QK_HEAD] + _unrot_cols(d[:, :, QK_HEAD:])
    return jnp.concatenate([d[:, :, :QK_NOPE], rope], axis=-1).reshape(Q_LORA, MLA_HEADS * QK_HEAD)


def ukv_padded(w_ukv):
    w = w_ukv.reshape(KV_LORA, MLA_HEADS, QK_NOPE + V_HEAD)
    wk = jnp.concatenate([w[:, :, :QK_NOPE], jnp.zeros((KV_LORA, MLA_HEADS, HP - QK_NOPE), w.dtype)], axis=-1)
    return jnp.concatenate([wk.reshape(KV_LORA, MLA_HEADS * HP), w[:, :, QK_NOPE:].reshape(KV_LORA, MLA_HEADS * V_HEAD)],
                           axis=1)


def ukv_unpad(dwp):
    dk = dwp[:, :MLA_HEADS * HP].reshape(KV_LORA, MLA_HEADS, HP)[:, :, :QK_NOPE]
    dv = dwp[:, MLA_HEADS * HP:].reshape(KV_LORA, MLA_HEADS, V_HEAD)
    return jnp.concatenate([dk, dv], axis=-1).reshape(KV_LORA, MLA_HEADS * (QK_NOPE + V_HEAD))


def block_diag(w):
    h, d, _ = w.shape
    eye = jnp.eye(h, dtype=w.dtype)
    return (eye[:, None, :, None] * w[:, :, None, :]).reshape(h * d, h * d)


def block_diag_part(dense, h):
    d = dense.shape[0] // h
    x = dense.reshape(h, d, h, d)
    return jnp.stack([x[i, :, i, :] for i in range(h)], axis=0)


def rope_tables(tp):
    pos = jnp.arange(tp, dtype=F32)
    inv_freq = ROPE_BASE ** (-jnp.arange(0, QK_ROPE, 2, dtype=F32) / QK_ROPE)
    ang = pos[:, None] * inv_freq[None, :]
    cos2 = jnp.tile(jnp.cos(ang), (1, 2))
    sin2 = jnp.tile(jnp.sin(ang), (1, 2))
    tabq = jnp.concatenate([jnp.ones((tp, QK_NOPE), F32), cos2, sin2], axis=1)
    tabk = jnp.concatenate([jnp.zeros((tp, QK_NOPE), F32), cos2, sin2], axis=1)
    return tabq, tabk


class Dims:
    def __init__(self, nb, seq):
        self.nb = nb
        self.t_real = seq + N_META
        self.tp = _round_up(self.t_real, ATT_BLK)
        self.n = self.tp // 4
        assert self.n % 16 == 0
        self.rows = nb * self.tp


class NoComm:
    def advance(self, carried):
        return None


def even_fwd(h, p, dm, comm):
    (u, hn), _ = norm_matmul(h, 0, D_MODEL, p["norm"], p["w_in"], dm.n, EVEN_IN // 2, F32, "ev_in")
    (y, ca, xc, a, hs), got = even_mid_fwd(u, p["conv_a"], p["conv_b"], p["conv_b_bias"], p["rw"], p["r_b"], p["iw"],
                                           p["i_b"], p["lam"], dm.nb, dm.tp, dm.n, "ev_mid", carry=comm.advance(None))
    comm.advance(got)
    out = matmul_res(y, p["w_out"].reshape(2, CONV_W, D_MODEL), h, dm.n, D_MODEL, "ev_out")
    return out, (h, u, hn, ca, xc, a, hs, y)


def even_bwd(dout, saved, p, dm, comm):
    h, u, hn, ca, xc, a, hs, y = saved
    g = {}
    dycat = matmul_nt(dout, p["w_out"], dm.n, D_MODEL, F32, "ev_dycat")
    g["w_out"], got = matmul_tn(y, dout, dm.n, "ev_dw_out", carry=comm.advance(None))
    outs, got = even_mid_bwd(u, dycat, ca, xc, a, hs, p["conv_a"], p["conv_b"], p["rw"], p["r_b"], p["iw"], p["i_b"],
                             p["lam"], dm.nb, dm.tp, dm.n, "ev_mid_bwd", carry=comm.advance(got))
    du, g["conv_a"], g["conv_b"], g["conv_b_bias"], drw, g["r_b"], diw, g["i_b"], g["lam"] = outs
    g["r_w"] = block_diag_part(drw, LRU_HEADS)
    g["i_w"] = block_diag_part(diw, LRU_HEADS)
    g["w_in"], got = matmul_tn(hn, du, dm.n, "ev_dw_in", carry=comm.advance(got))
    comm.advance(got)
    dx, g["norm"] = matmul_nt_normbwd(du, p["w_in"], h, 0, p["norm"], dout, dm.n, 512, F32, "ev_dx")
    return dx, g


def odd_fwd(h, p, tabq, tabk, dm, comm):
    nt = dm.tp // dm.n
    (u, hn), _ = norm_matmul(h, 0, D_MODEL, p["norm"], p["w_in_p"], dm.n, ODD_PAD, F32, "od_in")
    tab_spec = pl.BlockSpec((dm.n, HP), lambda i, j: (i % nt, 0))
    (q, cqn), _ = norm_matmul(u, 0, Q_LORA, p["q_norm"], p["w_uq_p"], dm.n, 512, MXU_DT, "od_q",
                              epi=_q_rope_epi, epi_ops=(tabq,), epi_specs=(tab_spec,))
    kr_spec = pl.BlockSpec((dm.n, HP), lambda i, j: (i, ODD_KR_COL))
    (k, ckvn), _ = norm_matmul(u, ODD_CKV_COL, KV_LORA, p["kv_norm"], p["w_uk_p"], dm.n, 512, MXU_DT, "od_k",
                               epi=_k_rope_epi, epi_ops=(u, tabk), epi_specs=(kr_spec, tab_spec))
    (v, _), _ = norm_matmul(u, ODD_CKV_COL, KV_LORA, p["kv_norm"], p["w_uv_p"], dm.n, MLA_HEADS * V_HEAD, MXU_DT,
                            "od_v")
    (o, lse), got = attn_fwd(q, k, v, dm.nb, dm.tp, "od_attn", carry=comm.advance(None))
    comm.advance(got)
    out = matmul_res(o[None], p["w_out"][None], h, dm.n, D_MODEL, "od_out")
    return out, (h, u, hn, cqn, ckvn, q, k, v, o, lse)


def odd_bwd(dout, saved, p, tabq, tabk, dm, comm):
    h, u, hn, cqn, ckvn, q, k, v, o, lse = saved
    g = {}
    do = matmul_nt(dout, p["w_out"], dm.n, D_MODEL, MXU_DT, "od_do")
    g["w_out"], got = matmul_tn(o, dout, dm.n, "od_dw_out", carry=comm.advance(None))
    (dq, dk, dv), got = attn_bwd(q, k, v, o, do, lse, dm.nb, dm.tp, "od_attn_bwd", carry=comm.advance(got))
    dqa, dkv, dkr = rope_bwd(dq, dk, dv, tabq, tabk, dm.tp, dm.n, "od_rope_bwd")
    g["w_uq_p"], got = matmul_tn(cqn, dqa, dm.n, "od_dw_uq", carry=comm.advance(got))
    comm.advance(got)
    g["w_ukv_p"], _ = matmul_tn(ckvn, dkv, dm.n, "od_dw_ukv")
    dcq, g["q_norm"] = matmul_nt_normbwd(dqa, p["w_uq_p"], u, 0, p["q_norm"], None, dm.n, 512, MXU_DT, "od_dcq")
    dckv, g["kv_norm"] = matmul_nt_normbwd(dkv, p["w_ukv_p"], u, ODD_CKV_COL, p["kv_norm"], None, dm.n, 512, MXU_DT,
                                           "od_dckv")
    du = jnp.concatenate([dcq, jnp.zeros((dm.rows, 128), MXU_DT), dckv, dkr], axis=1)
    g["w_in_p"], _ = matmul_tn(hn, du, dm.n, "od_dw_in")
    dx, g["norm"] = matmul_nt_normbwd(du, p["w_in_p"], h, 0, p["norm"], dout, dm.n, ODD_PAD, F32, "od_dx")
    return dx, g


def ffn_fwd(h, p, dm, comm):
    (up, hn), got = norm_matmul(h, 0, D_MODEL, p["norm"], p["w_up"], dm.n, D_FF // 2, MXU_DT, "ffn_up",
                                carry=comm.advance(None))
    comm.advance(got)
    u, y = ffn_mid_fwd(up, p["cw"], p["cb"], dm.nb, dm.tp, dm.n, "ffn_mid")
    out = matmul_res(y[None], p["w_down"][None], h, dm.n, D_MODEL, "ffn_down")
    return out, (h, up, hn, u, y)


def ffn_bwd(dout, saved, p, dm, comm):
    h, up, hn, u, y = saved
    g = {}
    dy = matmul_nt(dout, p["w_down"], dm.n, D_FF // 2, MXU_DT, "ffn_dy")
    g["w_down"], got = matmul_tn(y, dout, dm.n, "ffn_dw_down", carry=comm.advance(None))
    (dup, g["cw"], g["cb"]), got = ffn_mid_bwd(dy, u, up, p["cw"], dm.nb, dm.tp, dm.n, "ffn_mid_bwd",
                                               carry=comm.advance(got))
    g["w_up"], got = matmul_tn(hn, dup, dm.n, "ffn_dw_up", carry=comm.advance(got), col_shards=4)
    comm.advance(got)
    dx, g["norm"] = matmul_nt_normbwd(dup, p["w_up"], h, 0, p["norm"], dout, dm.n, D_FF // 2, F32, "ffn_dx")
    return dx, g


def _row(v):
    return v.reshape(1, -1)


def even_params(wf, j):
    return dict(norm=_row(wf["ev_norm"][j]), w_in=wf["ev_w_in"], conv_a=wf["ev_conv_a"][j], conv_b=wf["ev_conv_b"][j],
                conv_b_bias=_row(wf["ev_conv_b_bias"][j]), rw=block_diag(wf["ev_gate_r_w"][j]).astype(MXU_DT),
                r_b=_row(wf["ev_gate_r_b"][j]), iw=block_diag(wf["ev_gate_i_w"][j]).astype(MXU_DT),
                i_b=_row(wf["ev_gate_i_b"][j]), lam=_row(wf["ev_lru_lambda"][j]), w_out=wf["ev_w_out"])


def odd_params(wf, j):
    wkv = ukv_padded(wf["od_w_ukv"])
    return dict(norm=_row(wf["od_norm"][j]), w_in_p=odd_w_in_padded(wf["od_w_in"]), q_norm=_row(wf["od_q_norm"][j]),
                kv_norm=_row(wf["od_kv_norm"][j]), w_uq_p=uq_padded(wf["od_w_uq"]), w_ukv_p=wkv,
                w_uk_p=wkv[:, :MLA_HEADS * HP], w_uv_p=wkv[:, MLA_HEADS * HP:], w_out=wf["od_w_out"])


def ffn_params(wf, layer):
    return dict(norm=_row(wf["ffn_norm"][layer]), w_up=wf["ffn_w_up"],
                cw=jnp.moveaxis(wf["ffn_conv_w"][layer].reshape(3, 2, D_FF), 1, 0),
                cb=wf["ffn_conv_b"][layer].reshape(2, 1, D_FF), w_down=wf["ffn_w_down"])


def even_grads(g):
    out = {"ev_" + k_: g[k_] for k_ in ("w_in", "conv_a", "conv_b", "w_out")}
    out.update({"ev_norm": g["norm"][0], "ev_conv_b_bias": g["conv_b_bias"][0], "ev_gate_r_w": g["r_w"],
                "ev_gate_r_b": g["r_b"][0], "ev_gate_i_w": g["i_w"], "ev_gate_i_b": g["i_b"][0],
                "ev_lru_lambda": g["lam"][0]})
    return out


def odd_grads(g):
    return {"od_norm": g["norm"][0], "od_q_norm": g["q_norm"][0], "od_kv_norm": g["kv_norm"][0],
            "od_w_in": odd_w_in_unpad(g["w_in_p"]), "od_w_uq": uq_unpad(g["w_uq_p"]),
            "od_w_ukv": ukv_unpad(g["w_ukv_p"]), "od_w_out": g["w_out"]}


def ffn_grads(g):
    return {"ffn_norm": g["norm"][0], "ffn_w_up": g["w_up"], "ffn_conv_w": jnp.moveaxis(g["cw"], 0, 1).reshape(3, 2 * D_FF),
            "ffn_conv_b": g["cb"].reshape(2 * D_FF), "ffn_w_down": g["w_down"]}


WEIGHTS = ["meta_tokens", "ev_norm", "ev_w_in", "ev_conv_a", "ev_conv_b", "ev_conv_b_bias", "ev_gate_r_w", "ev_gate_r_b",
           "ev_gate_i_w", "ev_gate_i_b", "ev_lru_lambda", "ev_w_out", "od_norm", "od_w_in", "od_q_norm", "od_kv_norm",
           "od_w_uq", "od_w_ukv", "od_w_out", "ffn_norm", "ffn_w_up", "ffn_conv_w", "ffn_conv_b", "ffn_w_down",
           "final_norm"]
SHARD_AXIS = {"meta_tokens": 1, "ev_w_in": 2, "ev_conv_a": 2, "ev_conv_b": 2, "ev_w_out": 1, "od_norm": 1, "od_w_in": 1,
              "od_q_norm": 1, "od_kv_norm": 1, "od_w_uq": 2, "od_w_ukv": 2, "od_w_out": 1, "ffn_w_up": 2,
              "ffn_conv_w": 2, "ffn_w_down": 1}
MATMUL_WEIGHTS = ["ev_w_in", "ev_w_out", "od_w_in", "od_w_uq", "od_w_ukv", "od_w_out", "ffn_w_up", "ffn_w_down"]


LAYER_ORDER = [("ev", 0), ("ffn", 0), ("od", 0), ("ffn", 1), ("ev", 1), ("ffn", 2), ("od", 1), ("ffn", 3)]
LAYER_MATMUL = {"ev": ["ev_w_in", "ev_w_out"], "od": ["od_w_in", "od_w_uq", "od_w_ukv", "od_w_out"],
                "ffn": ["ffn_w_up", "ffn_w_down"]}
LAYER_SHARDED = {"ev": ["ev_w_in", "ev_conv_a", "ev_conv_b", "ev_w_out"],
                 "od": ["od_norm", "od_w_in", "od_q_norm", "od_kv_norm", "od_w_uq", "od_w_ukv", "od_w_out"],
                 "ffn": ["ffn_w_up", "ffn_conv_w", "ffn_w_down"]}
STACKED_SHARDS = "ffn_w_up"


def gather_at_entry(w, names, first, name):
    buf, offs = pack_rows([w[n] for n in names], 0, 32)
    halves = buf.reshape(2, buf.shape[0] // 2, LANE)
    outs = run_stage(stage_gather_chips([halves] + first.halves), name)
    first.step, first.got = 2, outs[1:]
    got = own_block(outs[0], halves).reshape(4, buf.shape[0], LANE)
    return {n: unshard(unpack_rows(got, off, w[n].shape), SHARD_AXIS[n]) for n, off in zip(names, offs)}


def _halves(a):
    return a.reshape(2, a.shape[0] // 2, a.shape[1])


class GatherComm:
    def __init__(self, w, kind, idx):
        self.names = LAYER_MATMUL[kind]
        self.halves = [_halves(w[n][idx].astype(MXU_DT)) for n in self.names]
        self.stage = stage_gather_chips(self.halves)
        self.step, self.got = 0, None

    def advance(self, carried):
        self.step += 1
        if self.step == 1:
            return self.stage
        if self.step == 2:
            self.got = carried
        return None

    def weights(self):
        out = {}
        for n, got, own in zip(self.names, self.got, self.halves):
            stacked = own_block(got, own).reshape(4, 2 * own.shape[1], own.shape[2])
            out[n] = stacked if n == STACKED_SHARDS else unshard(stacked, SHARD_AXIS[n] - 1)
        return out


class ReduceComm:
    def __init__(self, grads, axes, c_idx, tag, tail=None):
        shards = {n: grads[n] if n == STACKED_SHARDS else to_shards(grads[n], axes[n]) for n in grads}
        self.big = [n for n in grads if n in MATMUL_WEIGHTS]
        self.small = [n for n in grads if n not in MATMUL_WEIGHTS]
        self.shapes = {n: shards[n].shape[1:] for n in grads}
        arrays = [shards[n].reshape(4, 2, shards[n].shape[1] // 2, shards[n].shape[2]) for n in self.big]
        gs, self.offs = pack_rows([shards[n] for n in self.small], 1, 16)
        self.rs = gs.shape[1] // 2
        parts = [gs.reshape(4, 2, self.rs, LANE)]
        self.rr = 0
        if tail is not None:
            self.rr = tail.shape[0] // 8
            parts.append(tail.reshape(4, 2, self.rr, LANE))
        arrays.append(jnp.concatenate(parts, axis=2) if len(parts) > 1 else parts[0])
        self.arrays, self.c_idx, self.tag, self.step = arrays, c_idx, tag, 0
        self.part = self.mine = self.theirs = None

    def advance(self, carried):
        self.step += 1
        if self.step == 1:
            return stage_pair_exchange(self.arrays)
        if self.step == 2:
            self.part = [pair_sum(g, land, self.c_idx, "grad_pair_sum_%s_%d" % (self.tag, i))
                         for i, (g, land) in enumerate(zip(self.arrays, carried))]
            return stage_chip_scatter(self.part)
        if self.step == 3:
            me_idx = chip_index().astype(jnp.int32).reshape(1)
            self.mine = [chip_sum(land, part, me_idx, "grad_chip_sum_%s_%d" % (self.tag, i))
                         for i, (land, part) in enumerate(zip(carried, self.part))]
            return stage_pair_gather(self.mine)
        if self.step == 4:
            self.theirs = carried
        return None

    def run_alone(self, name):
        stage = self.advance(None)
        while stage is not None:
            stage = self.advance(run_stage(stage, name + "_%d" % self.step))

    def results(self):
        south = self.c_idx[0] == 0
        boths = [jnp.stack([jnp.where(south, m, t), jnp.where(south, t, m)], axis=0)
                 for m, t in zip(self.mine, self.theirs)]
        out = {n: b.reshape(self.shapes[n]) for n, b in zip(self.big, boths)}
        packed = boths[-1]
        flat = packed[:, :self.rs].reshape(2 * self.rs, LANE)
        out.update({n: unpack_rows(flat, off, self.shapes[n]) for n, off in zip(self.small, self.offs)})
        return out, packed[:, self.rs:self.rs + self.rr]


def kernel(x, meta_tokens, ev_norm, ev_w_in, ev_conv_a, ev_conv_b, ev_conv_b_bias, ev_gate_r_w, ev_gate_r_b, ev_gate_i_w, ev_gate_i_b, ev_lru_lambda, ev_w_out, od_norm, od_w_in, od_q_norm, od_kv_norm, od_w_uq, od_w_ukv, od_w_out, ffn_norm, ffn_w_up, ffn_conv_w, ffn_conv_b, ffn_w_down, final_norm, loss_target, m_meta_tokens, m_ev_norm, m_ev_w_in, m_ev_conv_a, m_ev_conv_b, m_ev_conv_b_bias, m_ev_gate_r_w, m_ev_gate_r_b, m_ev_gate_i_w, m_ev_gate_i_b, m_ev_lru_lambda, m_ev_w_out, m_od_norm, m_od_w_in, m_od_q_norm, m_od_kv_norm, m_od_w_uq, m_od_w_ukv, m_od_w_out, m_ffn_norm, m_ffn_w_up, m_ffn_conv_w, m_ffn_conv_b, m_ffn_w_down, m_final_norm, v_meta_tokens, v_ev_norm, v_ev_w_in, v_ev_conv_a, v_ev_conv_b, v_ev_conv_b_bias, v_ev_gate_r_w, v_ev_gate_r_b, v_ev_gate_i_w, v_ev_gate_i_b, v_ev_lru_lambda, v_ev_w_out, v_od_norm, v_od_w_in, v_od_q_norm, v_od_kv_norm, v_od_w_uq, v_od_w_ukv, v_od_w_out, v_ffn_norm, v_ffn_w_up, v_ffn_conv_w, v_ffn_conv_b, v_ffn_w_down, v_final_norm):
    given = dict(locals())
    w = {n: given[n] for n in WEIGHTS}
    nb, seq, _ = x.shape
    dm = Dims(nb, seq)
    n_layers = len(LAYER_ORDER)

    wf = {n: w[n] for n in WEIGHTS if n not in SHARD_AXIS}
    gathers = [GatherComm(w, kind, idx) for kind, idx in LAYER_ORDER]
    wf.update(gather_at_entry(w, [n for n in SHARD_AXIS if n not in MATMUL_WEIGHTS], gathers[0], "gather_at_entry"))

    tail = dm.tp - dm.t_real
    meta = jnp.broadcast_to(wf["meta_tokens"][None], (nb, N_META, D_MODEL))
    h = jnp.concatenate([meta, x, jnp.zeros((nb, tail, D_MODEL), F32)], axis=1).reshape(dm.rows, D_MODEL)
    tgt = jnp.pad(loss_target, ((0, 0), (N_META, tail), (0, 0))).reshape(dm.rows, D_MODEL)
    tabq, tabk = rope_tables(dm.tp)

    params, saved = [], []
    for i, (kind, idx) in enumerate(LAYER_ORDER):
        wl = dict(wf)
        wl.update(gathers[i].weights())
        comm = gathers[i + 1] if i + 1 < n_layers else NoComm()
        if kind == "ev":
            p = even_params(wl, idx)
            h, sv = even_fwd(h, p, dm, comm)
        elif kind == "od":
            p = odd_params(wl, idx)
            h, sv = odd_fwd(h, p, tabq, tabk, dm, comm)
        else:
            p = ffn_params(wl, idx)
            h, sv = ffn_fwd(h, p, dm, comm)
        params.append(p)
        saved.append(sv)

    dh, loss, dfinal = loss_head(h, tgt, _row(wf["final_norm"]), dm.tp, dm.t_real, dm.n, "loss_head")
    loss = lax.psum(loss[0, 0], ("x", "y", "c"))

    c_idx = lax.axis_index("c").astype(jnp.int32).reshape(1)
    layer_grads = {n: {} for n in WEIGHTS}
    pending, reduces = NoComm(), []
    for i in reversed(range(n_layers)):
        kind, idx = LAYER_ORDER[i]
        if kind == "ev":
            dh, g = even_bwd(dh, saved[i], params[i], dm, pending)
            g = even_grads(g)
        elif kind == "od":
            dh, g = odd_bwd(dh, saved[i], params[i], tabq, tabk, dm, pending)
            g = odd_grads(g)
        else:
            dh, g = ffn_bwd(dh, saved[i], params[i], dm, pending)
            g = ffn_grads(g)
        for n in g:
            if n not in SHARD_AXIS:
                layer_grads[n][idx] = g[n]
        if i > 0:
            pending = ReduceComm({n: g[n] for n in LAYER_SHARDED[kind]}, {n: SHARD_AXIS[n] - 1 for n in SHARD_AXIS},
                                 c_idx, "%s%d" % (kind, idx))
            reduces.append((pending, idx))
    dh3 = dh.reshape(nb, dm.tp, D_MODEL)
    grad_x = dh3[:, N_META:dm.t_real]

    repl = [n for n in WEIGHTS if n not in SHARD_AXIS]
    layer_grads["final_norm"] = {0: dfinal[0]}
    repl_full = {n: (layer_grads[n][0] if n == "final_norm" else
                     jnp.stack([layer_grads[n][j] for j in range(w[n].shape[0])], axis=0)) for n in repl}
    tail_buf, tail_offs = pack_rows([repl_full[n] for n in repl], 0, 64)
    first = {n: g[n] for n in LAYER_SHARDED["ev"]}
    first["meta_tokens"] = jnp.sum(dh3[:, :N_META], axis=0)
    axes = {n: SHARD_AXIS[n] - 1 for n in SHARD_AXIS}
    axes["meta_tokens"] = SHARD_AXIS["meta_tokens"]
    last = ReduceComm(first, axes, c_idx, "first_layer", tail=tail_buf)
    last.run_alone("grad_first_layer")
    reduces.append((last, 0))

    red = {}
    for comm, idx in reduces:
        got, tail_piece = comm.results()
        for n, v_ in got.items():
            if n == "meta_tokens":
                red[n] = v_
            else:
                layer_grads[n][idx] = v_
    tails = own_block(run_stage(stage_gather_chips([tail_piece]), "grad_gather_replicated")[0], tail_piece)
    tails = tails.reshape(tail_buf.shape[0], LANE)
    for n, off in zip(repl, tail_offs):
        red[n] = unpack_rows(tails, off, w[n].shape)
    for n in SHARD_AXIS:
        if n != "meta_tokens":
            red[n] = jnp.stack([layer_grads[n][j] for j in range(w[n].shape[0])], axis=0)

    outs = [adamw(red[n], w[n], given["m_" + n], given["v_" + n], "adamw_" + n) for n in WEIGHTS]
    return (loss, grad_x, *[red[n] for n in WEIGHTS], *[o[0] for o in outs], *[o[1] for o in outs],
            *[o[2] for o in outs])
```

```python
import math

import jax
import jax.numpy as jnp
from jax import lax
from jax.experimental import pallas as pl
from jax.experimental.pallas import tpu as pltpu

F32 = jnp.float32
MXU_DT = jnp.bfloat16
S = jax.ShapeDtypeStruct
MESH = pl.DeviceIdType.MESH

EPS = 1e-6
D_MODEL = 1024
N_META = 16
DEPTH = 4
CONV_W = 512
LRU_W = 512
LRU_HEADS = 8
LRU_C = 8.0
EVEN_IN = 2560
MLA_HEADS = 16
QK_NOPE = 64
QK_ROPE = 32
QK_HEAD = 96
V_HEAD = 64
Q_LORA = 384
KV_LORA = 256
ROPE_BASE = 10000.0
D_FF = 2816
ODD_PAD = 896
ODD_CKV_COL = 2
ODD_KR_COL = 6
HP = 128
ATT_BLK = 384
Q_PRESCALE = QK_HEAD ** -0.5 * math.log2(math.e)
FFN_CT = 256
EV_CT = 256
STRIP_ROWS = 352
LANE = 128
SUBLANE = 8
VMEM_LIMIT_MB = 52

ADAM_LR = 0.001
ADAM_B1 = 0.9
ADAM_B2 = 0.999
ADAM_EPS = 1e-08
ADAM_WD = 0.01
ADAM_STEP = 10

TN_ACC_ELEMS = 1536 * 1024
NT_DIMS = (((1,), (1,)), ((), ()))
TN_DIMS = (((0,), (0,)), ((), ()))


def _cp(sem):
    return pltpu.CompilerParams(dimension_semantics=sem, vmem_limit_bytes=VMEM_LIMIT_MB << 20)


def _div_tile(n, cap, mult):
    if n <= cap:
        return n
    best = None
    for t in range(mult, cap + 1, mult):
        if n % t == 0:
            best = t
    assert best is not None, (n, cap, mult)
    return best


def _round_up(n, m):
    return -(-n // m) * m


def mat_cols(arr):
    return arr.shape[1] if arr.ndim == 2 else arr.shape[0] * arr.shape[2]


def mat_width(arr):
    return arr.shape[-1]


def mat_spec(arr, tm, tw, rc):
    if arr.ndim == 2:
        return pl.BlockSpec((tm, tw), lambda *g: rc(*g))
    per = arr.shape[2] // tw
    assert arr.shape[2] % tw == 0

    def imap(*g):
        r, c = rc(*g)
        return (c // per, r, c % per)

    return pl.BlockSpec((None, tm, tw), imap)


HBM_SPEC = pl.BlockSpec(memory_space=pltpu.HBM)


class Stage:
    def __init__(self, inputs, out_shapes, sems, start, finish):
        self.inputs, self.out_shapes, self.sems, self.start, self.finish = inputs, out_shapes, sems, start, finish


def run_stage(stage, name):
    n_in, n_out = len(stage.inputs), len(stage.out_shapes)

    def body(*refs):
        ins, outs, sems = refs[:n_in], refs[n_in:n_in + n_out], refs[n_in + n_out:]
        stage.start(ins, outs, sems)
        stage.finish(ins, outs, sems)

    return pl.pallas_call(body, out_shape=list(stage.out_shapes), in_specs=[HBM_SPEC] * n_in,
                          out_specs=[HBM_SPEC] * n_out, scratch_shapes=list(stage.sems), name=name)(*stage.inputs)


def _call(body, ops, carry, *, grid, in_specs, out_specs, out_shape, scratch_shapes, sem, name):
    if carry is None:
        outs = pl.pallas_call(body, grid=grid, in_specs=in_specs, out_specs=out_specs, out_shape=out_shape,
                              scratch_shapes=scratch_shapes, compiler_params=_cp(sem), name=name)(*ops)
        return outs, None
    multi = isinstance(out_shape, (list, tuple))
    shapes = list(out_shape) if multi else [out_shape]
    ospecs = list(out_specs) if multi else [out_specs]
    n_in, n_out, n_sc = len(ops), len(shapes), len(scratch_shapes)
    c_in, c_out = len(carry.inputs), len(carry.out_shapes)

    def wrapped(*refs):
        ins, cin = refs[:n_in], refs[n_in:n_in + c_in]
        o0 = n_in + c_in
        outs, cout = refs[o0:o0 + n_out], refs[o0 + n_out:o0 + n_out + c_out]
        s0 = o0 + n_out + c_out
        scs, csems = refs[s0:s0 + n_sc], refs[s0 + n_sc:]
        first = pl.program_id(0) == 0
        last = pl.program_id(0) == grid[0] - 1
        for d in range(1, len(grid)):
            first = first & (pl.program_id(d) == 0)
            last = last & (pl.program_id(d) == grid[d] - 1)

        @pl.when(first)
        def _():
            carry.start(cin, cout, csems)

        body(*ins, *outs, *scs)

        @pl.when(last)
        def _():
            carry.finish(cin, cout, csems)

    res = pl.pallas_call(
        wrapped, grid=grid, in_specs=list(in_specs) + [HBM_SPEC] * c_in, out_specs=ospecs + [HBM_SPEC] * c_out,
        out_shape=shapes + list(carry.out_shapes), scratch_shapes=list(scratch_shapes) + list(carry.sems),
        compiler_params=_cp(("arbitrary",) * len(grid)), name=name)(*ops, *carry.inputs)
    main = res[:n_out]
    return (list(main) if multi else main[0]), list(res[n_out:])


def norm_matmul(x, xcol, kdim, gain, w, tm, tn, out_dtype, name, epi=None, epi_ops=(), epi_specs=(), carry=None):
    rows, n = x.shape[0], mat_cols(w) if w.ndim == 3 else w.shape[1]
    n_epi = len(epi_ops)
    w_spec = (pl.BlockSpec((kdim, tn), lambda i, j: (0, j)) if w.ndim == 2 else
              pl.BlockSpec((None, kdim, tn), lambda i, j: (j // (w.shape[2] // tn), 0, j % (w.shape[2] // tn))))

    def body(x_ref, g_ref, w_ref, *rest):
        epi_refs = rest[:n_epi]
        out_ref, xn_ref, xn_sc = rest[n_epi:]

        @pl.when(pl.program_id(1) == 0)
        def _():
            xv = x_ref[...]
            y = xv * lax.rsqrt(jnp.mean(xv * xv, axis=-1, keepdims=True) + EPS)
            xn = (y * g_ref[...]).astype(MXU_DT)
            xn_sc[...] = xn
            xn_ref[...] = xn

        acc = jnp.dot(xn_sc[...], w_ref[...], preferred_element_type=F32)
        if epi is not None:
            acc = epi(acc, *[r[...] for r in epi_refs])
        out_ref[...] = acc.astype(out_dtype)

    return _call(
        body, (x, gain, w, *epi_ops), carry, grid=(rows // tm, n // tn),
        in_specs=[pl.BlockSpec((tm, kdim), lambda i, j: (i, xcol)), pl.BlockSpec((1, kdim), lambda i, j: (0, 0)),
                  w_spec, *epi_specs],
        out_specs=[pl.BlockSpec((tm, tn), lambda i, j: (i, j)), pl.BlockSpec((tm, kdim), lambda i, j: (i, 0))],
        out_shape=[S((rows, n), out_dtype), S((rows, kdim), MXU_DT)],
        scratch_shapes=[pltpu.VMEM((tm, kdim), MXU_DT)], sem=("parallel", "arbitrary"), name=name)


def matmul_res(a, w, res, tm, tn, name):
    grp, rows, k = a.shape
    n = w.shape[2]

    def body(a_ref, w_ref, r_ref, o_ref):
        acc = r_ref[...]
        for g in range(grp):
            acc = acc + jnp.dot(a_ref[g], w_ref[g], preferred_element_type=F32)
        o_ref[...] = acc

    return pl.pallas_call(
        body, grid=(rows // tm, n // tn),
        in_specs=[pl.BlockSpec((grp, tm, k), lambda i, j: (0, i, 0)), pl.BlockSpec((grp, k, tn), lambda i, j: (0, 0, j)),
                  pl.BlockSpec((tm, tn), lambda i, j: (i, j))],
        out_specs=pl.BlockSpec((tm, tn), lambda i, j: (i, j)),
        out_shape=S((rows, n), F32), compiler_params=_cp(("parallel", "parallel")), name=name)(a, w, res)


def matmul_nt(a, w, tm, tn, out_dtype, name):
    rows, k = a.shape
    n = w.shape[0]

    def body(a_ref, w_ref, o_ref):
        o_ref[...] = lax.dot_general(a_ref[...].astype(MXU_DT), w_ref[...], NT_DIMS,
                                     preferred_element_type=F32).astype(out_dtype)

    return pl.pallas_call(
        body, grid=(rows // tm, n // tn),
        in_specs=[pl.BlockSpec((tm, k), lambda i, j: (i, 0)), pl.BlockSpec((tn, k), lambda i, j: (j, 0))],
        out_specs=pl.BlockSpec((tm, tn), lambda i, j: (i, j)),
        out_shape=S((rows, n), out_dtype), compiler_params=_cp(("parallel", "parallel")), name=name)(a, w)


def matmul_nt_normbwd(du, w, x, xcol, gain, res, tm, tk, out_dtype, name):
    rows, kc = du.shape[-2], mat_cols(du)
    dn = w.shape[-2]
    nk = kc // tk
    has_res = res is not None
    w_spec = (pl.BlockSpec((dn, tk), lambda i, k: (0, k)) if w.ndim == 2 else
              pl.BlockSpec((None, dn, tk), lambda i, k: (k // (w.shape[2] // tk), 0, k % (w.shape[2] // tk))))

    def body(du_ref, w_ref, x_ref, g_ref, *rest):
        if has_res:
            res_ref, dx_ref, dg_ref, acc = rest
        else:
            dx_ref, dg_ref, acc = rest
        i, k = pl.program_id(0), pl.program_id(1)

        @pl.when(k == 0)
        def _():
            acc[...] = jnp.zeros_like(acc)

        @pl.when((i == 0) & (k == 0))
        def _():
            dg_ref[...] = jnp.zeros_like(dg_ref)

        acc[...] += lax.dot_general(du_ref[...], w_ref[...], NT_DIMS, preferred_element_type=F32)

        @pl.when(k == nk - 1)
        def _():
            dhn = acc[...]
            xv = x_ref[...]
            rstd = lax.rsqrt(jnp.mean(xv * xv, axis=-1, keepdims=True) + EPS)
            xhat = xv * rstd
            dg_ref[...] += jnp.sum(dhn * xhat, axis=0, keepdims=True)
            dxh = dhn * g_ref[...]
            dx = rstd * (dxh - xhat * jnp.mean(dxh * xhat, axis=-1, keepdims=True))
            if has_res:
                dx = dx + res_ref[...]
            dx_ref[...] = dx.astype(out_dtype)

    in_specs = [mat_spec(du, tm, tk, lambda i, k: (i, k)), w_spec,
                pl.BlockSpec((tm, dn), lambda i, k: (i, xcol)), pl.BlockSpec((1, dn), lambda i, k: (0, 0))]
    ops = [du, w, x, gain]
    if has_res:
        in_specs.append(pl.BlockSpec((tm, dn), lambda i, k: (i, 0)))
        ops.append(res)
    return pl.pallas_call(
        body, grid=(rows // tm, nk), in_specs=in_specs,
        out_specs=[pl.BlockSpec((tm, dn), lambda i, k: (i, 0)), pl.BlockSpec((1, dn), lambda i, k: (0, 0))],
        out_shape=[S((rows, dn), out_dtype), S((1, dn), F32)],
        scratch_shapes=[pltpu.VMEM((tm, dn), F32)],
        compiler_params=_cp(("arbitrary", "arbitrary")), name=name)(*ops)


def matmul_tn(a, b, tr, name, carry=None, col_shards=1):
    rows, ka, nb = a.shape[-2], mat_cols(a), mat_cols(b)
    ta = _div_tile(mat_width(a), 1536, LANE)
    tb = _div_tile(mat_width(b), max(LANE, TN_ACC_ELEMS // ta // LANE * LANE), LANE)
    nr = rows // tr
    if col_shards == 1:
        out_spec, out_shape = pl.BlockSpec((ta, tb), lambda i, j, r: (i, j)), S((ka, nb), F32)
    else:
        per = nb // col_shards // tb
        assert per * tb * col_shards == nb
        out_spec = pl.BlockSpec((None, ta, tb), lambda i, j, r: (j // per, i, j % per))
        out_shape = S((col_shards, ka, nb // col_shards), F32)

    def body(a_ref, b_ref, o_ref, acc):
        r = pl.program_id(2)

        @pl.when(r == 0)
        def _():
            acc[...] = jnp.zeros_like(acc)

        acc[...] += lax.dot_general(a_ref[...].astype(MXU_DT), b_ref[...].astype(MXU_DT), TN_DIMS,
                                    preferred_element_type=F32)

        @pl.when(r == nr - 1)
        def _():
            o_ref[...] = acc[...]

    return _call(
        body, (a, b), carry, grid=(ka // ta, nb // tb, nr),
        in_specs=[mat_spec(a, tr, ta, lambda i, j, r: (r, i)), mat_spec(b, tr, tb, lambda i, j, r: (r, j))],
        out_specs=out_spec, out_shape=out_shape, scratch_shapes=[pltpu.VMEM((ta, tb), F32)],
        sem=("parallel", "parallel", "arbitrary"), name=name)


def _sigmoid(x):
    return 1.0 / (1.0 + jnp.exp(-x))


def _sigmoid_by_tanh(x):
    return 0.5 * jnp.tanh(0.5 * x) + 0.5


def _log1p(e):
    return jnp.where(e < 1e-3, e * (1.0 - e * (0.5 - e * (1.0 / 3.0 - 0.25 * e))), jnp.log(1.0 + e))


def _softplus(x):
    return jnp.maximum(x, 0.0) + _log1p(jnp.exp(-jnp.abs(x)))


def _expm1(x):
    series = x * (1.0 + x * (0.5 + x * (1.0 / 6.0 + x * (1.0 / 24.0 + x * (1.0 / 120.0)))))
    return jnp.where(jnp.abs(x) < 0.1, series, jnp.exp(x) - 1.0)


_GELU_K = math.sqrt(2.0 / math.pi)
_GELU_C = 0.044715


def _gelu_and_grad(x):
    th = jnp.tanh(_GELU_K * (x + _GELU_C * x * x * x))
    g = 0.5 * x * (1.0 + th)
    dg = 0.5 * (1.0 + th) + 0.5 * x * (1.0 - th * th) * _GELU_K * (1.0 + 3.0 * _GELU_C * x * x)
    return g, dg


def _row_iota(shape):
    return lax.broadcasted_iota(jnp.int32, shape, 0)


def _scan_chunk_fwd(a_sc, u_sc, out_ref, hcar, n, width):
    rowi = _row_iota((SUBLANE, width))

    def step(c, hprev):
        r0 = pl.multiple_of(c * SUBLANE, SUBLANE)
        a = a_sc[pl.ds(r0, SUBLANE), :]
        u = u_sc[pl.ds(r0, SUBLANE), :]
        for d in (1, 2, 4):
            a_s = jnp.where(rowi >= d, pltpu.roll(a, d, axis=0), 1.0)
            u_s = jnp.where(rowi >= d, pltpu.roll(u, d, axis=0), 0.0)
            u = u + a * u_s
            a = a * a_s
        h = u + a * hprev
        out_ref[pl.ds(r0, SUBLANE), :] = h
        return jnp.broadcast_to(h[SUBLANE - 1:SUBLANE, :], (SUBLANE, width))

    hcar[...] = lax.fori_loop(0, n // SUBLANE, step, hcar[...], unroll=4)


def _scan_chunk_bwd(b_sc, d_sc, out_ref, gcar, n, width):
    rowi = _row_iota((SUBLANE, width))
    nc = n // SUBLANE

    def step(c, gnext):
        r0 = pl.multiple_of((nc - 1 - c) * SUBLANE, SUBLANE)
        b = b_sc[pl.ds(r0, SUBLANE), :]
        d = d_sc[pl.ds(r0, SUBLANE), :]
        for s in (1, 2, 4):
            keep = rowi < SUBLANE - s
            b_s = jnp.where(keep, pltpu.roll(b, SUBLANE - s, axis=0), 1.0)
            d_s = jnp.where(keep, pltpu.roll(d, SUBLANE - s, axis=0), 0.0)
            d = d + b * d_s
            b = b * b_s
        g = d + b * gnext
        out_ref[pl.ds(r0, SUBLANE), :] = g
        return jnp.broadcast_to(g[0:1, :], (SUBLANE, width))

    gcar[...] = lax.fori_loop(0, nc, step, gcar[...], unroll=4)


def even_mid_fwd(u, conv_a, conv_b, conv_b_bias, rw, rb, iw, ib, lam, nb, tp, n, name, carry=None):
    rows = u.shape[0]
    w = EV_CT
    nj = CONV_W // w
    nt = tp // n
    h8 = SUBLANE

    def body(gb_r, gc_r, xa_r, xb_r, gate_r, ca_w, cb_w, cb_b, rw_r, rb_r, iw_r, ib_r, lam_r,
             y_o, ca_o, xc_o, a_o, hs_o, pext, xext, hcar, a_sc, u_sc):
        t = pl.program_id(2)

        @pl.when(t == 0)
        def _():
            pext[0:h8, :] = jnp.zeros((h8, w), F32)
            xext[0:h8, :] = jnp.zeros((h8, w), F32)
            hcar[...] = jnp.zeros_like(hcar)

        p = gc_r[...] * xa_r[...]
        pext[h8:h8 + n, :] = p
        wa = ca_w[...]
        ca = wa[2:3, :] * p + wa[1:2, :] * pext[h8 - 1:h8 - 1 + n, :] + wa[0:1, :] * pext[h8 - 2:h8 - 2 + n, :]
        ca_o[...] = ca
        y_o[0] = (gb_r[...] * ca).astype(MXU_DT)
        pext[0:h8, :] = pext[n:n + h8, :]

        xb = xb_r[...]
        xext[h8:h8 + n, :] = xb
        wb = cb_w[...]
        xc = (wb[3:4, :] * xb + wb[2:3, :] * xext[h8 - 1:h8 - 1 + n, :] + wb[1:2, :] * xext[h8 - 2:h8 - 2 + n, :]
              + wb[0:1, :] * xext[h8 - 3:h8 - 3 + n, :]) + cb_b[...]
        xc_o[...] = xc
        xext[0:h8, :] = xext[n:n + h8, :]

        xcm = xc.astype(MXU_DT)
        r = _sigmoid(jnp.dot(xcm, rw_r[...], preferred_element_type=F32) + rb_r[...])
        ig = _sigmoid(jnp.dot(xcm, iw_r[...], preferred_element_type=F32) + ib_r[...])
        log_a = (-LRU_C) * r * _softplus(-lam_r[...])
        a = jnp.exp(log_a)
        mult = jnp.sqrt(-_expm1(2.0 * log_a))
        a_sc[...] = a
        a_o[...] = a
        u_sc[...] = mult * (ig * xc)
        _scan_chunk_fwd(a_sc, u_sc, hs_o, hcar, n, w)
        gel, _ = _gelu_and_grad(gate_r[...])
        y_o[1] = (gel * hs_o[...]).astype(MXU_DT)

    def ublk(off):
        return pl.BlockSpec((n, w), lambda j, b, t: (b * nt + t, off + j))

    def pblk(r_):
        return pl.BlockSpec((r_, w), lambda j, b, t: (0, j))

    act = pl.BlockSpec((n, w), lambda j, b, t: (b * nt + t, j))
    mat = pl.BlockSpec((w, w), lambda j, b, t: (j, j))
    return _call(
        body, (u, u, u, u, u, conv_a, conv_b, conv_b_bias, rw, rb, iw, ib, lam), carry, grid=(nj, nb, nt),
        in_specs=[ublk(0), ublk(nj), ublk(2 * nj), ublk(3 * nj), ublk(4 * nj), pblk(3), pblk(4), pblk(1),
                  mat, pblk(1), mat, pblk(1), pblk(1)],
        out_specs=[pl.BlockSpec((2, n, w), lambda j, b, t: (0, b * nt + t, j)), act, act, act, act],
        out_shape=[S((2, rows, CONV_W), MXU_DT), S((rows, CONV_W), F32), S((rows, LRU_W), F32), S((rows, LRU_W), F32),
                   S((rows, LRU_W), F32)],
        scratch_shapes=[pltpu.VMEM((n + h8, w), F32), pltpu.VMEM((n + h8, w), F32), pltpu.VMEM((h8, w), F32),
                        pltpu.VMEM((n, w), F32), pltpu.VMEM((n, w), F32)],
        sem=("parallel", "parallel", "arbitrary"), name=name)


def even_mid_bwd(u, dycat, ca, xc, a_sv, hs, conv_a, conv_b, rw, rb, iw, ib, lam, nb, tp, n, name, carry=None):
    rows = u.shape[0]
    w = EV_CT
    nj = CONV_W // w
    nt = tp // n
    h8 = SUBLANE

    def body(gb_r, gc_r, xa_r, xb_r, gate_r, dya_r, dyb_r, ca_r, xc_r, a_r, hs_r, hsp_r,
             ca_w, cb_w, rw_r, rb_r, iw_r, ib_r, lam_r,
             du_o, dca_w, dcb_w, dcb_b, drw, drb, diw, dib, dlam,
             aext, hext, dext, eext, gcar, b_sc, d_sc, g_sc):
        b, t = pl.program_id(1), pl.program_id(2)

        @pl.when((b == 0) & (t == 0))
        def _():
            for ref in (dca_w, dcb_w, dcb_b, drw, drb, diw, dib, dlam):
                ref[...] = jnp.zeros_like(ref)

        @pl.when(t == 0)
        def _():
            aext[n:n + h8, :] = jnp.zeros((h8, w), F32)
            dext[n:n + h8, :] = jnp.zeros((h8, w), F32)
            eext[n:n + h8, :] = jnp.zeros((h8, w), F32)
            gcar[...] = jnp.zeros_like(gcar)

        xc_v = xc_r[...]
        xcm = xc_v.astype(MXU_DT)
        r = _sigmoid(jnp.dot(xcm, rw_r[...], preferred_element_type=F32) + rb_r[...])
        ig = _sigmoid(jnp.dot(xcm, iw_r[...], preferred_element_type=F32) + ib_r[...])
        lam_v = lam_r[...]
        sp = _softplus(-lam_v)
        log_a = (-LRU_C) * r * sp
        a = a_r[...]
        mult = jnp.sqrt(-_expm1(2.0 * log_a))
        hs_v = hs_r[...]
        gel, dgel = _gelu_and_grad(gate_r[...])
        dyb = dyb_r[...]
        du_o[4] = (dyb * hs_v * dgel).astype(MXU_DT)

        aext[0:n, :] = a
        b_sc[...] = aext[1:1 + n, :]
        d_sc[...] = dyb * gel
        _scan_chunk_bwd(b_sc, d_sc, g_sc, gcar, n, w)
        aext[n:n + h8, :] = aext[0:h8, :]
        g = g_sc[...]

        hext[0:h8, :] = jnp.where(t == nt - 1, 0.0, hsp_r[...])
        hext[h8:h8 + n, :] = hs_v
        da = g * hext[h8 - 1:h8 - 1 + n, :]
        dmult = g * (ig * xc_v)
        di = g * mult * xc_v
        dxc = g * mult * ig
        dlog_a = da * a - dmult * (a * a) / mult
        dr = dlog_a * ((-LRU_C) * sp)
        dsp = jnp.sum(dlog_a * ((-LRU_C) * r), axis=0, keepdims=True)
        dlam[...] += dsp * (-_sigmoid(-lam_v))
        dzr = dr * r * (1.0 - r)
        dzi = di * ig * (1.0 - ig)
        dzr_m = dzr.astype(MXU_DT)
        dzi_m = dzi.astype(MXU_DT)
        dxc = (dxc + lax.dot_general(dzr_m, rw_r[...], NT_DIMS, preferred_element_type=F32)
               + lax.dot_general(dzi_m, iw_r[...], NT_DIMS, preferred_element_type=F32))
        drw[...] += lax.dot_general(xcm, dzr_m, TN_DIMS, preferred_element_type=F32)
        diw[...] += lax.dot_general(xcm, dzi_m, TN_DIMS, preferred_element_type=F32)
        drb[...] += jnp.sum(dzr, axis=0, keepdims=True)
        dib[...] += jnp.sum(dzi, axis=0, keepdims=True)
        dcb_b[...] += jnp.sum(dxc, axis=0, keepdims=True)

        xb = xb_r[...]
        dext[0:n, :] = dxc
        wb = cb_w[...]
        d1, d2, d3 = dext[1:1 + n, :], dext[2:2 + n, :], dext[3:3 + n, :]
        du_o[3] = (wb[3:4, :] * dxc + wb[2:3, :] * d1 + wb[1:2, :] * d2 + wb[0:1, :] * d3).astype(MXU_DT)
        dcb_w[3:4, :] += jnp.sum(xb * dxc, axis=0, keepdims=True)
        dcb_w[2:3, :] += jnp.sum(xb * d1, axis=0, keepdims=True)
        dcb_w[1:2, :] += jnp.sum(xb * d2, axis=0, keepdims=True)
        dcb_w[0:1, :] += jnp.sum(xb * d3, axis=0, keepdims=True)
        dext[n:n + h8, :] = dext[0:h8, :]

        gb, gc, xa = gb_r[...], gc_r[...], xa_r[...]
        dya = dya_r[...]
        du_o[0] = (dya * ca_r[...]).astype(MXU_DT)
        dca = dya * gb
        eext[0:n, :] = dca
        wa = ca_w[...]
        e1, e2 = eext[1:1 + n, :], eext[2:2 + n, :]
        dp = wa[2:3, :] * dca + wa[1:2, :] * e1 + wa[0:1, :] * e2
        p = gc * xa
        dca_w[2:3, :] += jnp.sum(p * dca, axis=0, keepdims=True)
        dca_w[1:2, :] += jnp.sum(p * e1, axis=0, keepdims=True)
        dca_w[0:1, :] += jnp.sum(p * e2, axis=0, keepdims=True)
        eext[n:n + h8, :] = eext[0:h8, :]
        du_o[1] = (dp * xa).astype(MXU_DT)
        du_o[2] = (dp * gc).astype(MXU_DT)

    def rt(b, t):
        return b * nt + (nt - 1 - t)

    def ublk(off):
        return pl.BlockSpec((n, w), lambda j, b, t: (rt(b, t), off + j))

    def pblk(r_):
        return pl.BlockSpec((r_, w), lambda j, b, t: (0, j))

    act = pl.BlockSpec((n, w), lambda j, b, t: (rt(b, t), j))
    n8 = n // h8
    hsp = pl.BlockSpec((h8, w), lambda j, b, t: (jnp.maximum(rt(b, t) * n8 - 1, 0), j))
    mat = pl.BlockSpec((w, w), lambda j, b, t: (j, j))
    return _call(
        body, (u, u, u, u, u, dycat, dycat, ca, xc, a_sv, hs, hs, conv_a, conv_b, rw, rb, iw, ib, lam), carry,
        grid=(nj, nb, nt),
        in_specs=[ublk(0), ublk(nj), ublk(2 * nj), ublk(3 * nj), ublk(4 * nj), ublk(0), ublk(nj), act, act, act, act,
                  hsp, pblk(3), pblk(4), mat, pblk(1), mat, pblk(1), pblk(1)],
        out_specs=[pl.BlockSpec((5, n, w), lambda j, b, t: (0, rt(b, t), j)), pblk(3), pblk(4), pblk(1),
                   mat, pblk(1), mat, pblk(1), pblk(1)],
        out_shape=[S((5, rows, CONV_W), MXU_DT), S((3, CONV_W), F32), S((4, LRU_W), F32), S((1, LRU_W), F32),
                   S((LRU_W, LRU_W), F32), S((1, LRU_W), F32), S((LRU_W, LRU_W), F32), S((1, LRU_W), F32),
                   S((1, LRU_W), F32)],
        scratch_shapes=[pltpu.VMEM((n + h8, w), F32)] * 4 + [pltpu.VMEM((h8, w), F32)] + [pltpu.VMEM((n, w), F32)] * 3,
        sem=("arbitrary", "arbitrary", "arbitrary"), name=name)


def ffn_mid_fwd(up, cw, cb, nb, tp, n, name):
    rows = up.shape[0]
    w = FFN_CT
    nj = D_FF // w
    nt = tp // n
    h8 = SUBLANE

    sr = STRIP_ROWS
    assert n % sr == 0, (n, sr)

    def body(xa_r, xg_r, w_r, b_r, u_o, y_o, halo):
        t = pl.program_id(2)

        @pl.when(t == 0)
        def _():
            halo[...] = jnp.zeros_like(halo)

        wv = (w_r[0], w_r[1])
        bv = (b_r[0], b_r[1])

        def strip(s, carry):
            r0 = pl.multiple_of(s * sr, sr)
            us, new = [], []
            for g, x_r in enumerate((xa_r, xg_r)):
                x = x_r[pl.ds(r0, sr), :].astype(F32)
                win = jnp.concatenate([carry[g], x], axis=0)
                x1 = pltpu.roll(win, 1, axis=0)[h8:, :]
                x2 = pltpu.roll(win, 2, axis=0)[h8:, :]
                u = (wv[g][2:3, :] * x + wv[g][1:2, :] * x1 + wv[g][0:1, :] * x2) + bv[g]
                u_o[g, pl.ds(r0, sr), :] = u.astype(MXU_DT)
                us.append(u)
                new.append(x[sr - h8:, :])
            y_o[pl.ds(r0, sr), :] = (us[0] * _sigmoid_by_tanh(us[0]) * us[1]).astype(MXU_DT)
            return tuple(new)

        ha, hg = lax.fori_loop(0, n // sr, strip, (halo[0], halo[1]))
        halo[0] = ha
        halo[1] = hg

    def ublk(off):
        return pl.BlockSpec((n, w), lambda j, b, t: (b * nt + t, off + j))

    return pl.pallas_call(
        body, grid=(nj, nb, nt),
        in_specs=[ublk(0), ublk(nj), pl.BlockSpec((2, 3, w), lambda j, b, t: (0, 0, j)),
                  pl.BlockSpec((2, 1, w), lambda j, b, t: (0, 0, j))],
        out_specs=[pl.BlockSpec((2, n, w), lambda j, b, t: (0, b * nt + t, j)), ublk(0)],
        out_shape=[S((2, rows, D_FF), MXU_DT), S((rows, D_FF), MXU_DT)],
        scratch_shapes=[pltpu.VMEM((2, h8, w), F32)],
        compiler_params=_cp(("parallel", "parallel", "arbitrary")), name=name,
    )(up, up, cw, cb)


def ffn_mid_bwd(dy, u, up, cw, nb, tp, n, name, carry=None):
    rows = up.shape[0]
    w = FFN_CT
    nj = D_FF // w
    nt = tp // n
    h8 = SUBLANE

    sr = STRIP_ROWS
    assert n % sr == 0, (n, sr)
    ns = n // sr

    def fold(v):
        acc = v[0:h8, :]
        for k in range(1, sr // h8):
            acc = acc + v[k * h8:(k + 1) * h8, :]
        return acc

    def body(dy_r, u_r, xa_r, xg_r, w_r, dx_o, dw, db, halo):
        b, t = pl.program_id(1), pl.program_id(2)

        @pl.when((b == 0) & (t == 0))
        def _():
            dw[...] = jnp.zeros_like(dw)
            db[...] = jnp.zeros_like(db)

        @pl.when(t == 0)
        def _():
            halo[...] = jnp.zeros_like(halo)

        wv = (w_r[0], w_r[1])

        def strip(s, carry):
            halos, sums = carry
            r0 = pl.multiple_of((ns - 1 - s) * sr, sr)
            dyv = dy_r[pl.ds(r0, sr), :].astype(F32)
            ua = u_r[0, pl.ds(r0, sr), :].astype(F32)
            ug = u_r[1, pl.ds(r0, sr), :].astype(F32)
            sg = _sigmoid_by_tanh(ua)
            dus = (dyv * ug * (sg * (1.0 + ua * (1.0 - sg))), dyv * (ua * sg))
            new_halos, new_sums = [], []
            for g, x_r in enumerate((xa_r, xg_r)):
                du = dus[g]
                win = jnp.concatenate([du, halos[g]], axis=0)
                d1 = pltpu.roll(win, sr + h8 - 1, axis=0)[0:sr, :]
                d2 = pltpu.roll(win, sr + h8 - 2, axis=0)[0:sr, :]
                dx_o[g, pl.ds(r0, sr), :] = (wv[g][2:3, :] * du + wv[g][1:2, :] * d1 + wv[g][0:1, :] * d2).astype(MXU_DT)
                x = x_r[pl.ds(r0, sr), :].astype(F32)
                s2, s1, s0, sb = sums[g]
                new_sums.append((s2 + fold(x * du), s1 + fold(x * d1), s0 + fold(x * d2), sb + fold(du)))
                new_halos.append(du[0:h8, :])
            return tuple(new_halos), tuple(new_sums)

        z = jnp.zeros((h8, w), F32)
        halos, sums = lax.fori_loop(0, ns, strip, ((halo[0], halo[1]), ((z, z, z, z), (z, z, z, z))))
        halo[0] = halos[0]
        halo[1] = halos[1]
        for g in range(2):
            s2, s1, s0, sb = sums[g]
            dw[g, 2:3, :] += jnp.sum(s2, axis=0, keepdims=True)
            dw[g, 1:2, :] += jnp.sum(s1, axis=0, keepdims=True)
            dw[g, 0:1, :] += jnp.sum(s0, axis=0, keepdims=True)
            db[g] += jnp.sum(sb, axis=0, keepdims=True)

    def rt(b, t):
        return b * nt + (nt - 1 - t)

    def ublk(off):
        return pl.BlockSpec((n, w), lambda j, b, t: (rt(b, t), off + j))

    pair = pl.BlockSpec((2, n, w), lambda j, b, t: (0, rt(b, t), j))
    return _call(
        body, (dy, u, up, up, cw), carry, grid=(nj, nb, nt),
        in_specs=[ublk(0), pair, ublk(0), ublk(nj), pl.BlockSpec((2, 3, w), lambda j, b, t: (0, 0, j))],
        out_specs=[pair, pl.BlockSpec((2, 3, w), lambda j, b, t: (0, 0, j)),
                   pl.BlockSpec((2, 1, w), lambda j, b, t: (0, 0, j))],
        out_shape=[S((2, rows, D_FF), MXU_DT), S((2, 3, D_FF), F32), S((2, 1, D_FF), F32)],
        scratch_shapes=[pltpu.VMEM((2, h8, w), F32)],
        sem=("arbitrary", "arbitrary", "arbitrary"), name=name)


def _lane_mod(shape):
    return lax.broadcasted_iota(jnp.int32, shape, 1) & (HP - 1)


def _q_rope_epi(acc, tab):
    reps = acc.shape[1] // HP
    a = acc * jnp.tile(tab, (1, reps))
    lane = _lane_mod(a.shape)
    shifted = pltpu.roll(a, a.shape[1] - QK_ROPE, axis=1)
    return jnp.where(lane < QK_NOPE, a, jnp.where(lane < QK_HEAD, a + shifted, 0.0)) * Q_PRESCALE


def _k_rope_block(krblk, tabk):
    a = krblk * tabk
    lane = _lane_mod(a.shape)
    b = a + pltpu.roll(a, HP - QK_ROPE, axis=1)
    return jnp.where((lane >= QK_NOPE) & (lane < QK_HEAD), b, 0.0)


def _k_rope_epi(acc, krblk, tabk):
    reps = acc.shape[1] // HP
    return acc + jnp.tile(_k_rope_block(krblk, tabk), (1, reps))


def attn_fwd(q, k, v, nb, tp, name, carry=None):
    rows = q.shape[0]
    blk = ATT_BLK
    nq = tp // blk
    npair = MLA_HEADS // 2

    def body(q_r, k_r, v_r, o_r, lse_r):
        qi = pl.program_id(2)
        lane = lax.broadcasted_iota(jnp.int32, (blk, LANE), 1)
        even = lane < V_HEAD
        sum_lane = (V_HEAD, 0)
        rowi = lax.broadcasted_iota(jnp.int32, (blk, blk), 0)
        coli = lax.broadcasted_iota(jnp.int32, (blk, blk), 1)
        qs = [q_r[:, h * HP:(h + 1) * HP] for h in range(2)]

        def kv_block(k0, width, carry, visible):
            ms, accs = carry
            vblk = v_r[pl.ds(k0, width), :]
            one = jnp.ones_like(vblk)
            zero = jnp.zeros_like(vblk)
            vlane = lax.broadcasted_iota(jnp.int32, (width, LANE), 1)
            ss = [lax.dot_general(qs[h], k_r[pl.ds(k0, width), h * HP:(h + 1) * HP], NT_DIMS,
                                  preferred_element_type=F32) for h in range(2)]
            new_ms, new_accs = [], []
            for h in range(2):
                s = ss[h]
                if visible is not None:
                    s = jnp.where(visible, s, -jnp.inf)
                m_new = jnp.maximum(ms[h], jnp.max(s, axis=1, keepdims=True))
                alpha = jnp.exp2(ms[h] - m_new)
                p = jnp.exp2(s - m_new).astype(MXU_DT)
                mine = (vlane < V_HEAD) if h == 0 else (vlane >= V_HEAD)
                vh = jnp.where(mine, vblk, jnp.where(vlane == sum_lane[h], one, zero))
                new_accs.append(alpha * accs[h] + jnp.dot(p, vh, preferred_element_type=F32))
                new_ms.append(m_new)
            return tuple(new_ms), tuple(new_accs)

        neg = jnp.full((blk, 1), -jnp.inf, F32)
        zacc = jnp.zeros((blk, LANE), F32)
        carry = lax.fori_loop(0, qi // 2, lambda i, c: kv_block(pl.multiple_of(i * 2 * blk, blk), 2 * blk, c, None),
                              ((neg, neg), (zacc, zacc)))
        rowi2 = lax.broadcasted_iota(jnp.int32, (blk, 2 * blk), 0)
        coli2 = lax.broadcasted_iota(jnp.int32, (blk, 2 * blk), 1)
        ms, accs = lax.cond(
            qi % 2 == 1,
            lambda c: kv_block(pl.multiple_of((qi - 1) * blk, blk), 2 * blk, c, coli2 - blk <= rowi2),
            lambda c: kv_block(pl.multiple_of(qi * blk, blk), blk, c, coli <= rowi), carry)
        ls = [accs[h][:, sum_lane[h]:sum_lane[h] + 1] for h in range(2)]
        o_r[...] = jnp.where(even, accs[0] / ls[0], accs[1] / ls[1]).astype(MXU_DT)
        lse_r[...] = jnp.where(even, ms[0] + jnp.log2(ls[0]), ms[1] + jnp.log2(ls[1]))

    return _call(
        body, (q, k, v), carry, grid=(nb, npair, nq),
        in_specs=[pl.BlockSpec((blk, 2 * HP), lambda b, p, i: (b * nq + i, p)),
                  pl.BlockSpec((tp, 2 * HP), lambda b, p, i: (b, p)),
                  pl.BlockSpec((tp, LANE), lambda b, p, i: (b, p))],
        out_specs=[pl.BlockSpec((blk, LANE), lambda b, p, i: (b * nq + i, p)),
                   pl.BlockSpec((None, blk, LANE), lambda b, p, i: (p, b * nq + i, 0))],
        out_shape=[S((rows, MLA_HEADS * V_HEAD), MXU_DT), S((npair, rows, LANE), F32)], scratch_shapes=[],
        sem=("parallel", "parallel", "arbitrary"), name=name)


def attn_bwd(q, k, v, o, do, lse, nb, tp, name, carry=None):
    rows = q.shape[0]
    blk = ATT_BLK
    nq = tp // blk
    npair = MLA_HEADS // 2
    scale = QK_HEAD ** -0.5

    def body(q_r, k_r, v_r, o_r, do_r, lse_r, dq_o, dk_o, dv_o, dq_acc, delta_sc):
        kb = pl.program_id(2)
        even = lax.broadcasted_iota(jnp.int32, (blk, LANE), 1) < V_HEAD
        rowi = lax.broadcasted_iota(jnp.int32, (blk, blk), 0)
        coli = lax.broadcasted_iota(jnp.int32, (blk, blk), 1)

        @pl.when(kb == 0)
        def _():
            dq_acc[...] = jnp.zeros_like(dq_acc)

            def dstep(i, c):
                r0 = pl.multiple_of(i * blk, blk)
                prod = do_r[pl.ds(r0, blk), :].astype(F32) * o_r[pl.ds(r0, blk), :].astype(F32)
                de = jnp.sum(jnp.where(even, prod, 0.0), axis=1, keepdims=True)
                dd = jnp.sum(jnp.where(even, 0.0, prod), axis=1, keepdims=True)
                delta_sc[pl.ds(r0, blk), :] = jnp.where(even, de, dd)
                return c

            lax.fori_loop(0, nq, dstep, 0)

        vblk = v_r[...]
        ks = [k_r[:, h * HP:(h + 1) * HP] for h in range(2)]

        def q_block(r0, height, carry, visible):
            dk0, dk1, dv = carry
            dob = do_r[pl.ds(r0, height), :]
            lse_b = lse_r[pl.ds(r0, height), :]
            dl_b = delta_sc[pl.ds(r0, height), :]
            qlane = lax.broadcasted_iota(jnp.int32, (height, LANE), 1)
            dks = [dk0, dk1]
            qhs = [q_r[pl.ds(r0, height), h * HP:(h + 1) * HP] for h in range(2)]
            dohs = [jnp.where((qlane < V_HEAD) if h == 0 else (qlane >= V_HEAD), dob, jnp.zeros_like(dob))
                    for h in range(2)]
            ss = [lax.dot_general(qhs[h], ks[h], NT_DIMS, preferred_element_type=F32) for h in range(2)]
            dps = [lax.dot_general(dohs[h], vblk, NT_DIMS, preferred_element_type=F32) for h in range(2)]
            for h in range(2):
                lo = 0 if h == 0 else V_HEAD
                p = jnp.exp2(ss[h] - lse_b[:, lo:lo + 1])
                if visible is not None:
                    p = jnp.where(visible, p, 0.0)
                ds = (p * (dps[h] - dl_b[:, lo:lo + 1])).astype(MXU_DT)
                dv = dv + lax.dot_general(p.astype(MXU_DT), dohs[h], TN_DIMS, preferred_element_type=F32)
                dks[h] = dks[h] + lax.dot_general(ds, qhs[h], TN_DIMS, preferred_element_type=F32)
                dq_acc[pl.ds(r0, height), h * HP:(h + 1) * HP] += jnp.dot(ds, ks[h], preferred_element_type=F32)
            return dks[0], dks[1], dv

        z = jnp.zeros((blk, HP), F32)
        below = nq - 1 - kb
        odd = below % 2
        rowi2 = lax.broadcasted_iota(jnp.int32, (2 * blk, blk), 0)
        coli2 = lax.broadcasted_iota(jnp.int32, (2 * blk, blk), 1)
        first = pl.multiple_of(kb * blk, blk)
        carry = lax.cond(odd == 1, lambda c: q_block(first, 2 * blk, c, coli2 <= rowi2),
                         lambda c: q_block(first, blk, c, coli <= rowi), (z, z, jnp.zeros((blk, LANE), F32)))
        dk0, dk1, dv = lax.fori_loop(
            0, below // 2, lambda i, c: q_block(pl.multiple_of((kb + 1 + odd + 2 * i) * blk, blk), 2 * blk, c, None),
            carry)
        dk_o[:, 0:HP] = (dk0 * (scale / Q_PRESCALE)).astype(MXU_DT)
        dk_o[:, HP:2 * HP] = (dk1 * (scale / Q_PRESCALE)).astype(MXU_DT)
        dv_o[...] = dv.astype(MXU_DT)

        @pl.when(kb == nq - 1)
        def _():
            dq_o[...] = (dq_acc[...] * scale).astype(MXU_DT)

    seq_pair = pl.BlockSpec((tp, LANE), lambda b, p, kk: (b, p))
    return _call(
        body, (q, k, v, o, do, lse), carry, grid=(nb, npair, nq),
        in_specs=[pl.BlockSpec((tp, 2 * HP), lambda b, p, kk: (b, p)),
                  pl.BlockSpec((blk, 2 * HP), lambda b, p, kk: (b * nq + kk, p)),
                  pl.BlockSpec((blk, LANE), lambda b, p, kk: (b * nq + kk, p)),
                  seq_pair, seq_pair, pl.BlockSpec((None, tp, LANE), lambda b, p, kk: (p, b, 0))],
        out_specs=[pl.BlockSpec((tp, 2 * HP), lambda b, p, kk: (b, p)),
                   pl.BlockSpec((blk, 2 * HP), lambda b, p, kk: (b * nq + kk, p)),
                   pl.BlockSpec((blk, LANE), lambda b, p, kk: (b * nq + kk, p))],
        out_shape=[S((rows, MLA_HEADS * HP), MXU_DT), S((rows, MLA_HEADS * HP), MXU_DT),
                   S((rows, MLA_HEADS * V_HEAD), MXU_DT)],
        scratch_shapes=[pltpu.VMEM((tp, 2 * HP), F32), pltpu.VMEM((tp, LANE), F32)],
        sem=("parallel", "parallel", "arbitrary"), name=name)


def rope_bwd(dq, dk, dv, tabq, tabk, tp, tm, name):
    rows = dq.shape[0]
    nt = tp // tm
    wq = MLA_HEADS * HP

    def body(dq_r, dk_r, dv_r, tq_r, tk_r, dqa_o, dkv_o, dkr_o):
        dqv = dq_r[...].astype(F32)
        lane = _lane_mod(dqv.shape)
        in_rope = (lane >= QK_NOPE) & (lane < QK_HEAD)
        rope = jnp.where(in_rope, dqv, 0.0)
        da = jnp.where(lane < QK_HEAD, dqv, 0.0) + pltpu.roll(rope, QK_ROPE, axis=1)
        dqa_o[...] = (da * jnp.tile(tq_r[...], (1, MLA_HEADS))).astype(MXU_DT)
        dkf = dk_r[...].astype(F32)
        dkv_o[:, 0:wq] = jnp.where(lane < QK_NOPE, dkf, 0.0).astype(MXU_DT)
        dkv_o[:, wq:] = dv_r[...]
        kr = jnp.where(in_rope, dkf, 0.0)
        tot = kr[:, 0:HP]
        for h in range(1, MLA_HEADS):
            tot = tot + kr[:, h * HP:(h + 1) * HP]
        dkr_o[...] = ((tot + pltpu.roll(tot, QK_ROPE, axis=1)) * tk_r[...]).astype(MXU_DT)

    def rowblk(wd):
        return pl.BlockSpec((tm, wd), lambda i: (i, 0))

    tab = pl.BlockSpec((tm, HP), lambda i: (i % nt, 0))
    return pl.pallas_call(
        body, grid=(rows // tm,), in_specs=[rowblk(wq), rowblk(wq), rowblk(MLA_HEADS * V_HEAD), tab, tab],
        out_specs=[rowblk(wq), rowblk(wq + MLA_HEADS * V_HEAD), rowblk(HP)],
        out_shape=[S((rows, wq), MXU_DT), S((rows, wq + MLA_HEADS * V_HEAD), MXU_DT), S((rows, HP), MXU_DT)],
        compiler_params=_cp(("parallel",)), name=name)(dq, dk, dv, tabq, tabk)


def loss_head(h, target, gain, tp, t_real, tm, name):
    rows = h.shape[0]
    nt = tp // tm

    def body(h_r, t_r, g_r, dh_o, loss_o, dg_o):
        i = pl.program_id(0)

        @pl.when(i == 0)
        def _():
            loss_o[...] = jnp.zeros_like(loss_o)
            dg_o[...] = jnp.zeros_like(dg_o)

        xv = h_r[...]
        rstd = lax.rsqrt(jnp.mean(xv * xv, axis=-1, keepdims=True) + EPS)
        xhat = xv * rstd
        g = g_r[...]
        pos = (i % nt) * tm + lax.broadcasted_iota(jnp.int32, (tm, 1), 0)
        valid = (pos >= N_META) & (pos < t_real)
        err = jnp.where(valid, xhat * g - t_r[...], 0.0)
        loss_o[...] += 0.5 * jnp.sum(jnp.mean(err * err, axis=-1, keepdims=True))
        dy = err * (1.0 / D_MODEL)
        dg_o[...] += jnp.sum(dy * xhat, axis=0, keepdims=True)
        dxh = dy * g
        dh_o[...] = rstd * (dxh - xhat * jnp.mean(dxh * xhat, axis=-1, keepdims=True))

    blk = pl.BlockSpec((tm, D_MODEL), lambda i: (i, 0))
    return pl.pallas_call(
        body, grid=(rows // tm,), in_specs=[blk, blk, pl.BlockSpec((1, D_MODEL), lambda i: (0, 0))],
        out_specs=[blk, pl.BlockSpec((1, LANE), lambda i: (0, 0)), pl.BlockSpec((1, D_MODEL), lambda i: (0, 0))],
        out_shape=[S((rows, D_MODEL), F32), S((1, LANE), F32), S((1, D_MODEL), F32)],
        compiler_params=_cp(("arbitrary",)), name=name)(h, target, gain)


ADAM_TILE_ELEMS = 128 * 1024


def adamw(g, w, m, v, name):
    shape = w.shape
    cols = shape[-1]
    rws = max(1, math.prod(shape[:-1]))
    tr = rws if rws * cols <= ADAM_TILE_ELEMS else _div_tile(rws, max(SUBLANE, ADAM_TILE_ELEMS // cols), SUBLANE)
    bc1 = 1.0 - ADAM_B1 ** ADAM_STEP
    bc2 = 1.0 - ADAM_B2 ** ADAM_STEP

    def body(g_r, w_r, m_r, v_r, do, mo, vo):
        gv = g_r[...]
        mn = ADAM_B1 * m_r[...] + (1.0 - ADAM_B1) * gv
        vn = ADAM_B2 * v_r[...] + (1.0 - ADAM_B2) * (gv * gv)
        m_hat = mn / bc1
        v_hat = vn / bc2
        do[...] = -ADAM_LR * (m_hat / (jnp.sqrt(v_hat) + ADAM_EPS) + ADAM_WD * w_r[...])
        mo[...] = mn
        vo[...] = vn

    blk = pl.BlockSpec((tr, cols), lambda i: (i, 0))
    outs = pl.pallas_call(
        body, grid=(rws // tr,), in_specs=[blk] * 4, out_specs=[blk] * 3, out_shape=[S((rws, cols), F32)] * 3,
        compiler_params=_cp(("parallel",)), name=name)(*[a.reshape(rws, cols) for a in (g, w, m, v)])
    return tuple(o.reshape(shape) for o in outs)


SUM_TILE_ELEMS = 128 * 1024


def _place():
    return lax.axis_index("x"), lax.axis_index("y"), lax.axis_index("c")


def _remote(src, dst, send_sems, recv_sems, k, to):
    return pltpu.make_async_remote_copy(src_ref=src, dst_ref=dst, send_sem=send_sems.at[k], recv_sem=recv_sems.at[k],
                                        device_id=to, device_id_type=MESH)


def chip_index():
    return 2 * lax.axis_index("x") + lax.axis_index("y")


def _sem_pair(n):
    return [pltpu.SemaphoreType.DMA((n,)), pltpu.SemaphoreType.DMA((n,))]


def stage_gather_chips(xs):
    def copies(ins, outs, sems):
        send_sems, recv_sems = sems
        mx, my, mc = _place()
        sibling = (mx, my, 1 - mc)
        chips = [(1 - mx, my), (mx, 1 - my), (1 - mx, 1 - my)]
        first, landed, passed, from_sibling = [], [], [], []
        for i, (x_ref, out_ref) in enumerate(zip(ins, outs)):
            def piece(cx, cy, h, out_ref=out_ref):
                return out_ref.at[2 * cx + cy, h]

            for j, (cx, cy) in enumerate(chips):
                k = 6 * i + j
                first.append(_remote(x_ref.at[mc], piece(mx, my, mc), send_sems, recv_sems, k, (cx, cy, mc)))
                landed.append(_remote(x_ref.at[mc], piece(cx, cy, mc), send_sems, recv_sems, k, (cx, cy, mc)))
                passed.append(_remote(piece(cx, cy, mc), piece(cx, cy, mc), send_sems, recv_sems, k + 3, sibling))
                from_sibling.append(_remote(x_ref.at[mc], piece(cx, cy, 1 - mc), send_sems, recv_sems, k + 3, sibling))
        return first, landed, passed, from_sibling

    def start(ins, outs, sems):
        for cp in copies(ins, outs, sems)[0]:
            cp.start()

    def finish(ins, outs, sems):
        first, landed, passed, from_sibling = copies(ins, outs, sems)
        for arrived, onward in zip(landed, passed):
            arrived.wait_recv()
            onward.start()
        for cp in from_sibling:
            cp.wait_recv()
        for cp in first + passed:
            cp.wait_send()

    return Stage(list(xs), [S((4,) + x.shape, x.dtype) for x in xs], _sem_pair(6 * len(xs)), start, finish)


def own_block(gathered, xs):
    return lax.dynamic_update_slice(gathered, xs[None], (chip_index(), 0, 0, 0))


def stage_pair_exchange(gs):
    def copies(ins, outs, sems):
        send_sems, recv_sems = sems
        mx, my, mc = _place()
        return [_remote(g_ref.at[s, 1 - mc], land_ref.at[s], send_sems, recv_sems, 4 * i + s, (mx, my, 1 - mc))
                for i, (g_ref, land_ref) in enumerate(zip(ins, outs)) for s in range(4)]

    def start(ins, outs, sems):
        for cp in copies(ins, outs, sems):
            cp.start()

    def finish(ins, outs, sems):
        cps = copies(ins, outs, sems)
        for cp in cps:
            cp.wait_recv()
        for cp in cps:
            cp.wait_send()

    return Stage(list(gs), [S((4,) + g.shape[2:], g.dtype) for g in gs], _sem_pair(4 * len(gs)), start, finish)


def _sum_rows(rws, width):
    return _div_tile(rws, max(SUBLANE, SUM_TILE_ELEMS // width), SUBLANE)


def pair_sum(g4, land, c_idx, name):
    _, _, rws, wd = g4.shape
    th = _sum_rows(rws, wd)

    def body(c_ref, a_ref, b_ref, o_ref):
        o_ref[...] = a_ref[...] + b_ref[...]

    return pl.pallas_call(
        body,
        grid_spec=pltpu.PrefetchScalarGridSpec(
            num_scalar_prefetch=1, grid=(4, rws // th),
            in_specs=[pl.BlockSpec((None, None, th, wd), lambda s, i, c: (s, c[0], i, 0)),
                      pl.BlockSpec((None, th, wd), lambda s, i, c: (s, i, 0))],
            out_specs=pl.BlockSpec((None, th, wd), lambda s, i, c: (s, i, 0))),
        out_shape=S((4, rws, wd), F32), compiler_params=_cp(("parallel", "parallel")), name=name)(c_idx, g4, land)


def stage_chip_scatter(ps):
    def copies(ins, outs, sems):
        send_sems, recv_sems = sems
        mx, my, mc = _place()
        me = 2 * mx + my
        chips = [(1 - mx, my), (mx, 1 - my), (1 - mx, 1 - my)]
        sent, landed = [], []
        for i, (p_ref, land_ref) in enumerate(zip(ins, outs)):
            for j, (cx, cy) in enumerate(chips):
                k = 3 * i + j
                sent.append(_remote(p_ref.at[2 * cx + cy], land_ref.at[me], send_sems, recv_sems, k, (cx, cy, mc)))
                landed.append(_remote(p_ref.at[me], land_ref.at[2 * cx + cy], send_sems, recv_sems, k, (cx, cy, mc)))
        return sent, landed

    def start(ins, outs, sems):
        for cp in copies(ins, outs, sems)[0]:
            cp.start()

    def finish(ins, outs, sems):
        sent, landed = copies(ins, outs, sems)
        for cp in landed:
            cp.wait_recv()
        for cp in sent:
            cp.wait_send()

    return Stage(list(ps), [S(p.shape, p.dtype) for p in ps], _sem_pair(3 * len(ps)), start, finish)


def chip_sum(l4, p4, me_idx, name):
    _, rws, wd = l4.shape
    th = _sum_rows(rws, wd)

    def body(me_ref, a, b, c, d, own, o_ref):
        me = me_ref[0]
        parts = [jnp.where(me == s, own[...], r[...]) for s, r in enumerate((a, b, c, d))]
        o_ref[...] = ((parts[0] + parts[1]) + parts[2]) + parts[3]

    def blk(s):
        return pl.BlockSpec((None, th, wd), lambda i, me: (jnp.where(me[0] == s, (s + 1) % 4, s), i, 0))

    return pl.pallas_call(
        body,
        grid_spec=pltpu.PrefetchScalarGridSpec(
            num_scalar_prefetch=1, grid=(rws // th,),
            in_specs=[blk(0), blk(1), blk(2), blk(3), pl.BlockSpec((None, th, wd), lambda i, me: (me[0], i, 0))],
            out_specs=pl.BlockSpec((th, wd), lambda i, me: (i, 0))),
        out_shape=S((rws, wd), F32), compiler_params=_cp(("parallel",)), name=name)(me_idx, l4, l4, l4, l4, p4)


def stage_pair_gather(rs):
    def copies(ins, outs, sems):
        send_sems, recv_sems = sems
        mx, my, mc = _place()
        return [_remote(r_ref, out_ref, send_sems, recv_sems, i, (mx, my, 1 - mc))
                for i, (r_ref, out_ref) in enumerate(zip(ins, outs))]

    def start(ins, outs, sems):
        for cp in copies(ins, outs, sems):
            cp.start()

    def finish(ins, outs, sems):
        for cp in copies(ins, outs, sems):
            cp.wait()

    return Stage(list(rs), [S(r.shape, r.dtype) for r in rs], _sem_pair(len(rs)), start, finish)


PACK_ELEMS = 16 * LANE


def pack_rows(arrays, lead, total_mult):
    parts, offs, r0 = [], [], 0
    for a in arrays:
        flat = a.reshape(a.shape[:lead] + (-1,))
        elems = _round_up(flat.shape[-1], PACK_ELEMS)
        flat = jnp.pad(flat, [(0, 0)] * lead + [(0, elems - flat.shape[-1])])
        parts.append(flat.reshape(flat.shape[:lead] + (elems // LANE, LANE)))
        offs.append((r0, elems // LANE))
        r0 += elems // LANE
    total = _round_up(r0, total_mult)
    if total > r0:
        parts.append(jnp.zeros(parts[0].shape[:lead] + (total - r0, LANE), parts[0].dtype))
    return jnp.concatenate(parts, axis=lead), offs


def unpack_rows(buf, off, shape):
    r0, nr = off
    lead = buf.shape[:-2]
    n = math.prod(shape)
    return buf[..., r0:r0 + nr, :].reshape(lead + (nr * LANE,))[..., :n].reshape(lead + tuple(shape))


def unshard(stacked, axis):
    x = jnp.moveaxis(stacked, 0, axis)
    return x.reshape(x.shape[:axis] + (4 * x.shape[axis + 1],) + x.shape[axis + 2:])


def to_shards(full, axis):
    n = full.shape[axis] // 4
    x = full.reshape(full.shape[:axis] + (4, n) + full.shape[axis + 1:])
    return jnp.moveaxis(x, axis, 0)


def _rot_cols(w):
    half = w.shape[-1] // 2
    return jnp.concatenate([-w[..., half:], w[..., :half]], axis=-1)


def _unrot_cols(dw):
    half = dw.shape[-1] // 2
    return jnp.concatenate([dw[..., half:], -dw[..., :half]], axis=-1)


def odd_w_in_padded(w_in):
    kr = w_in[:, Q_LORA + KV_LORA:]
    rows = w_in.shape[0]
    return jnp.concatenate([w_in[:, :Q_LORA], jnp.zeros((rows, 128), w_in.dtype), w_in[:, Q_LORA:Q_LORA + KV_LORA],
                            jnp.zeros((rows, 64), w_in.dtype), kr, _rot_cols(kr)], axis=1)


def odd_w_in_unpad(dwp):
    base = 512 + KV_LORA + 64
    dkr = dwp[:, base:base + QK_ROPE] + _unrot_cols(dwp[:, base + QK_ROPE:base + 2 * QK_ROPE])
    return jnp.concatenate([dwp[:, :Q_LORA], dwp[:, 512:512 + KV_LORA], dkr], axis=1)


def uq_padded(w_uq):
    w = w_uq.reshape(Q_LORA, MLA_HEADS, QK_HEAD)
    return jnp.concatenate([w, _rot_cols(w[:, :, QK_NOPE:])], axis=-1).reshape(Q_LORA, MLA_HEADS * HP)


def uq_unpad(dwp):
    d = dwp.reshape(Q_LORA, MLA_HEADS, HP)
    rope = d[:, :, QK_NOPE:QK_HEAD] + _unrot_cols(d[:, :, QK_HEAD:])
    return jnp.concatenate([d[:, :, :QK_NOPE], rope], axis=-1).reshape(Q_LORA, MLA_HEADS * QK_HEAD)


def ukv_padded(w_ukv):
    w = w_ukv.reshape(KV_LORA, MLA_HEADS, QK_NOPE + V_HEAD)
    wk = jnp.concatenate([w[:, :, :QK_NOPE], jnp.zeros((KV_LORA, MLA_HEADS, HP - QK_NOPE), w.dtype)], axis=-1)
    return jnp.concatenate([wk.reshape(KV_LORA, MLA_HEADS * HP), w[:, :, QK_NOPE:].reshape(KV_LORA, MLA_HEADS * V_HEAD)],
                           axis=1)


def ukv_unpad(dwp):
    dk = dwp[:, :MLA_HEADS * HP].reshape(KV_LORA, MLA_HEADS, HP)[:, :, :QK_NOPE]
    dv = dwp[:, MLA_HEADS * HP:].reshape(KV_LORA, MLA_HEADS, V_HEAD)
    return jnp.concatenate([dk, dv], axis=-1).reshape(KV_LORA, MLA_HEADS * (QK_NOPE + V_HEAD))


def block_diag(w):
    h, d, _ = w.shape
    eye = jnp.eye(h, dtype=w.dtype)
    return (eye[:, None, :, None] * w[:, :, None, :]).reshape(h * d, h * d)


def block_diag_part(dense, h):
    d = dense.shape[0] // h
    x = dense.reshape(h, d, h, d)
    return jnp.stack([x[i, :, i, :] for i in range(h)], axis=0)


def rope_tables(tp):
    pos = jnp.arange(tp, dtype=F32)
    inv_freq = ROPE_BASE ** (-jnp.arange(0, QK_ROPE, 2, dtype=F32) / QK_ROPE)
    ang = pos[:, None] * inv_freq[None, :]
    cos2 = jnp.tile(jnp.cos(ang), (1, 2))
    sin2 = jnp.tile(jnp.sin(ang), (1, 2))
    tabq = jnp.concatenate([jnp.ones((tp, QK_NOPE), F32), cos2, sin2], axis=1)
    tabk = jnp.concatenate([jnp.zeros((tp, QK_NOPE), F32), cos2, sin2], axis=1)
    return tabq, tabk


class Dims:
    def __init__(self, nb, seq):
        self.nb = nb
        self.t_real = seq + N_META
        self.tp = _round_up(self.t_real, ATT_BLK)
        self.n = self.tp // 4
        assert self.n % 16 == 0
        self.rows = nb * self.tp


class NoComm:
    def advance(self, carried):
        return None


def even_fwd(h, p, dm, comm):
    (u, hn), _ = norm_matmul(h, 0, D_MODEL, p["norm"], p["w_in"], dm.n, EVEN_IN // 2, F32, "ev_in")
    (y, ca, xc, a, hs), got = even_mid_fwd(u, p["conv_a"], p["conv_b"], p["conv_b_bias"], p["rw"], p["r_b"], p["iw"],
                                           p["i_b"], p["lam"], dm.nb, dm.tp, dm.n, "ev_mid", carry=comm.advance(None))
    comm.advance(got)
    out = matmul_res(y, p["w_out"].reshape(2, CONV_W, D_MODEL), h, dm.n, D_MODEL, "ev_out")
    return out, (h, u, hn, ca, xc, a, hs, y)


def even_bwd(dout, saved, p, dm, comm):
    h, u, hn, ca, xc, a, hs, y = saved
    g = {}
    dycat = matmul_nt(dout, p["w_out"], dm.n, D_MODEL, F32, "ev_dycat")
    g["w_out"], got = matmul_tn(y, dout, dm.n, "ev_dw_out", carry=comm.advance(None))
    outs, got = even_mid_bwd(u, dycat, ca, xc, a, hs, p["conv_a"], p["conv_b"], p["rw"], p["r_b"], p["iw"], p["i_b"],
                             p["lam"], dm.nb, dm.tp, dm.n, "ev_mid_bwd", carry=comm.advance(got))
    du, g["conv_a"], g["conv_b"], g["conv_b_bias"], drw, g["r_b"], diw, g["i_b"], g["lam"] = outs
    g["r_w"] = block_diag_part(drw, LRU_HEADS)
    g["i_w"] = block_diag_part(diw, LRU_HEADS)
    g["w_in"], got = matmul_tn(hn, du, dm.n, "ev_dw_in", carry=comm.advance(got))
    comm.advance(got)
    dx, g["norm"] = matmul_nt_normbwd(du, p["w_in"], h, 0, p["norm"], dout, dm.n, 512, F32, "ev_dx")
    return dx, g


def odd_fwd(h, p, tabq, tabk, dm, comm):
    nt = dm.tp // dm.n
    (u, hn), _ = norm_matmul(h, 0, D_MODEL, p["norm"], p["w_in_p"], dm.n, ODD_PAD, F32, "od_in")
    tab_spec = pl.BlockSpec((dm.n, HP), lambda i, j: (i % nt, 0))
    (q, cqn), _ = norm_matmul(u, 0, Q_LORA, p["q_norm"], p["w_uq_p"], dm.n, 1024, MXU_DT, "od_q",
                              epi=_q_rope_epi, epi_ops=(tabq,), epi_specs=(tab_spec,))
    kr_spec = pl.BlockSpec((dm.n, HP), lambda i, j: (i, ODD_KR_COL))
    (k, ckvn), _ = norm_matmul(u, ODD_CKV_COL, KV_LORA, p["kv_norm"], p["w_uk_p"], dm.n, 1024, MXU_DT, "od_k",
                               epi=_k_rope_epi, epi_ops=(u, tabk), epi_specs=(kr_spec, tab_spec))
    (v, _), _ = norm_matmul(u, ODD_CKV_COL, KV_LORA, p["kv_norm"], p["w_uv_p"], dm.n, MLA_HEADS * V_HEAD, MXU_DT,
                            "od_v")
    (o, lse), got = attn_fwd(q, k, v, dm.nb, dm.tp, "od_attn", carry=comm.advance(None))
    comm.advance(got)
    out = matmul_res(o[None], p["w_out"][None], h, dm.n, D_MODEL, "od_out")
    return out, (h, u, hn, cqn, ckvn, q, k, v, o, lse)


def odd_bwd(dout, saved, p, tabq, tabk, dm, comm):
    h, u, hn, cqn, ckvn, q, k, v, o, lse = saved
    g = {}
    do = matmul_nt(dout, p["w_out"], dm.n, D_MODEL, MXU_DT, "od_do")
    g["w_out"], got = matmul_tn(o, dout, dm.n, "od_dw_out", carry=comm.advance(None))
    (dq, dk, dv), got = attn_bwd(q, k, v, o, do, lse, dm.nb, dm.tp, "od_attn_bwd", carry=comm.advance(got))
    dqa, dkv, dkr = rope_bwd(dq, dk, dv, tabq, tabk, dm.tp, dm.n, "od_rope_bwd")
    g["w_uq_p"], got = matmul_tn(cqn, dqa, dm.n, "od_dw_uq", carry=comm.advance(got))
    comm.advance(got)
    g["w_ukv_p"], _ = matmul_tn(ckvn, dkv, dm.n, "od_dw_ukv")
    dcq, g["q_norm"] = matmul_nt_normbwd(dqa, p["w_uq_p"], u, 0, p["q_norm"], None, dm.n, mat_cols(dqa), MXU_DT,
                                         "od_dcq")
    dckv, g["kv_norm"] = matmul_nt_normbwd(dkv, p["w_ukv_p"], u, ODD_CKV_COL, p["kv_norm"], None, dm.n, mat_cols(dkv),
                                           MXU_DT, "od_dckv")
    du = jnp.concatenate([dcq, jnp.zeros((dm.rows, 128), MXU_DT), dckv, dkr], axis=1)
    g["w_in_p"], _ = matmul_tn(hn, du, dm.n, "od_dw_in")
    dx, g["norm"] = matmul_nt_normbwd(du, p["w_in_p"], h, 0, p["norm"], dout, dm.n, ODD_PAD, F32, "od_dx")
    return dx, g


def ffn_fwd(h, p, dm, comm):
    (up, hn), got = norm_matmul(h, 0, D_MODEL, p["norm"], p["w_up"], dm.n, D_FF // 2, MXU_DT, "ffn_up",
                                carry=comm.advance(None))
    comm.advance(got)
    u, y = ffn_mid_fwd(up, p["cw"], p["cb"], dm.nb, dm.tp, dm.n, "ffn_mid")
    out = matmul_res(y[None], p["w_down"][None], h, dm.n, D_MODEL, "ffn_down")
    return out, (h, up, hn, u, y)


def ffn_bwd(dout, saved, p, dm, comm):
    h, up, hn, u, y = saved
    g = {}
    dy = matmul_nt(dout, p["w_down"], dm.n, D_FF, MXU_DT, "ffn_dy")
    g["w_down"], got = matmul_tn(y, dout, dm.n, "ffn_dw_down", carry=comm.advance(None))
    (dup, g["cw"], g["cb"]), got = ffn_mid_bwd(dy, u, up, p["cw"], dm.nb, dm.tp, dm.n, "ffn_mid_bwd",
                                               carry=comm.advance(got))
    g["w_up"], got = matmul_tn(hn, dup, dm.n, "ffn_dw_up", carry=comm.advance(got), col_shards=4)
    comm.advance(got)
    dx, g["norm"] = matmul_nt_normbwd(dup, p["w_up"], h, 0, p["norm"], dout, dm.n, D_FF // 2, F32, "ffn_dx")
    return dx, g


def _row(v):
    return v.reshape(1, -1)


def even_params(wf, j):
    return dict(norm=_row(wf["ev_norm"][j]), w_in=wf["ev_w_in"], conv_a=wf["ev_conv_a"][j], conv_b=wf["ev_conv_b"][j],
                conv_b_bias=_row(wf["ev_conv_b_bias"][j]), rw=block_diag(wf["ev_gate_r_w"][j]).astype(MXU_DT),
                r_b=_row(wf["ev_gate_r_b"][j]), iw=block_diag(wf["ev_gate_i_w"][j]).astype(MXU_DT),
                i_b=_row(wf["ev_gate_i_b"][j]), lam=_row(wf["ev_lru_lambda"][j]), w_out=wf["ev_w_out"])


def odd_params(wf, j):
    wkv = ukv_padded(wf["od_w_ukv"])
    return dict(norm=_row(wf["od_norm"][j]), w_in_p=odd_w_in_padded(wf["od_w_in"]), q_norm=_row(wf["od_q_norm"][j]),
                kv_norm=_row(wf["od_kv_norm"][j]), w_uq_p=uq_padded(wf["od_w_uq"]), w_ukv_p=wkv,
                w_uk_p=wkv[:, :MLA_HEADS * HP], w_uv_p=wkv[:, MLA_HEADS * HP:], w_out=wf["od_w_out"])


def ffn_params(wf, layer):
    return dict(norm=_row(wf["ffn_norm"][layer]), w_up=wf["ffn_w_up"],
                cw=jnp.moveaxis(wf["ffn_conv_w"][layer].reshape(3, 2, D_FF), 1, 0),
                cb=wf["ffn_conv_b"][layer].reshape(2, 1, D_FF), w_down=wf["ffn_w_down"])


def even_grads(g):
    out = {"ev_" + k_: g[k_] for k_ in ("w_in", "conv_a", "conv_b", "w_out")}
    out.update({"ev_norm": g["norm"][0], "ev_conv_b_bias": g["conv_b_bias"][0], "ev_gate_r_w": g["r_w"],
                "ev_gate_r_b": g["r_b"][0], "ev_gate_i_w": g["i_w"], "ev_gate_i_b": g["i_b"][0],
                "ev_lru_lambda": g["lam"][0]})
    return out


def odd_grads(g):
    return {"od_norm": g["norm"][0], "od_q_norm": g["q_norm"][0], "od_kv_norm": g["kv_norm"][0],
            "od_w_in": odd_w_in_unpad(g["w_in_p"]), "od_w_uq": uq_unpad(g["w_uq_p"]),
            "od_w_ukv": ukv_unpad(g["w_ukv_p"]), "od_w_out": g["w_out"]}


def ffn_grads(g):
    return {"ffn_norm": g["norm"][0], "ffn_w_up": g["w_up"], "ffn_conv_w": jnp.moveaxis(g["cw"], 0, 1).reshape(3, 2 * D_FF),
            "ffn_conv_b": g["cb"].reshape(2 * D_FF), "ffn_w_down": g["w_down"]}


WEIGHTS = ["meta_tokens", "ev_norm", "ev_w_in", "ev_conv_a", "ev_conv_b", "ev_conv_b_bias", "ev_gate_r_w", "ev_gate_r_b",
           "ev_gate_i_w", "ev_gate_i_b", "ev_lru_lambda", "ev_w_out", "od_norm", "od_w_in", "od_q_norm", "od_kv_norm",
           "od_w_uq", "od_w_ukv", "od_w_out", "ffn_norm", "ffn_w_up", "ffn_conv_w", "ffn_conv_b", "ffn_w_down",
           "final_norm"]
SHARD_AXIS = {"meta_tokens": 1, "ev_w_in": 2, "ev_conv_a": 2, "ev_conv_b": 2, "ev_w_out": 1, "od_norm": 1, "od_w_in": 1,
              "od_q_norm": 1, "od_kv_norm": 1, "od_w_uq": 2, "od_w_ukv": 2, "od_w_out": 1, "ffn_w_up": 2,
              "ffn_conv_w": 2, "ffn_w_down": 1}
MATMUL_WEIGHTS = ["ev_w_in", "ev_w_out", "od_w_in", "od_w_uq", "od_w_ukv", "od_w_out", "ffn_w_up", "ffn_w_down"]


LAYER_ORDER = [("ev", 0), ("ffn", 0), ("od", 0), ("ffn", 1), ("ev", 1), ("ffn", 2), ("od", 1), ("ffn", 3)]
LAYER_MATMUL = {"ev": ["ev_w_in", "ev_w_out"], "od": ["od_w_in", "od_w_uq", "od_w_ukv", "od_w_out"],
                "ffn": ["ffn_w_up", "ffn_w_down"]}
LAYER_SHARDED = {"ev": ["ev_w_in", "ev_conv_a", "ev_conv_b", "ev_w_out"],
                 "od": ["od_norm", "od_w_in", "od_q_norm", "od_kv_norm", "od_w_uq", "od_w_ukv", "od_w_out"],
                 "ffn": ["ffn_w_up", "ffn_conv_w", "ffn_w_down"]}
STACKED_SHARDS = "ffn_w_up"


def gather_at_entry(w, names, first, name):
    buf, offs = pack_rows([w[n] for n in names], 0, 32)
    halves = buf.reshape(2, buf.shape[0] // 2, LANE)
    outs = run_stage(stage_gather_chips([halves] + first.halves), name)
    first.step, first.got = 2, outs[1:]
    got = own_block(outs[0], halves).reshape(4, buf.shape[0], LANE)
    return {n: unshard(unpack_rows(got, off, w[n].shape), SHARD_AXIS[n]) for n, off in zip(names, offs)}


def _halves(a):
    return a.reshape(2, a.shape[0] // 2, a.shape[1])


class GatherComm:
    def __init__(self, w, kind, idx):
        self.names = LAYER_MATMUL[kind]
        self.halves = [_halves(w[n][idx].astype(MXU_DT)) for n in self.names]
        self.stage = stage_gather_chips(self.halves)
        self.step, self.got = 0, None

    def advance(self, carried):
        self.step += 1
        if self.step == 1:
            return self.stage
        if self.step == 2:
            self.got = carried
        return None

    def weights(self):
        out = {}
        for n, got, own in zip(self.names, self.got, self.halves):
            stacked = own_block(got, own).reshape(4, 2 * own.shape[1], own.shape[2])
            out[n] = stacked if n == STACKED_SHARDS else unshard(stacked, SHARD_AXIS[n] - 1)
        return out


class ReduceComm:
    def __init__(self, grads, axes, c_idx, tag, tail=None):
        shards = {n: grads[n] if n == STACKED_SHARDS else to_shards(grads[n], axes[n]) for n in grads}
        self.big = [n for n in grads if n in MATMUL_WEIGHTS]
        self.small = [n for n in grads if n not in MATMUL_WEIGHTS]
        self.shapes = {n: shards[n].shape[1:] for n in grads}
        arrays = [shards[n].reshape(4, 2, shards[n].shape[1] // 2, shards[n].shape[2]) for n in self.big]
        gs, self.offs = pack_rows([shards[n] for n in self.small], 1, 16)
        self.rs = gs.shape[1] // 2
        parts = [gs.reshape(4, 2, self.rs, LANE)]
        self.rr = 0
        if tail is not None:
            self.rr = tail.shape[0] // 8
            parts.append(tail.reshape(4, 2, self.rr, LANE))
        arrays.append(jnp.concatenate(parts, axis=2) if len(parts) > 1 else parts[0])
        self.arrays, self.c_idx, self.tag, self.step = arrays, c_idx, tag, 0
        self.part = self.mine = self.theirs = None

    def advance(self, carried):
        self.step += 1
        if self.step == 1:
            return stage_pair_exchange(self.arrays)
        if self.step == 2:
            self.part = [pair_sum(g, land, self.c_idx, "grad_pair_sum_%s_%d" % (self.tag, i))
                         for i, (g, land) in enumerate(zip(self.arrays, carried))]
            return stage_chip_scatter(self.part)
        if self.step == 3:
            me_idx = chip_index().astype(jnp.int32).reshape(1)
            self.mine = [chip_sum(land, part, me_idx, "grad_chip_sum_%s_%d" % (self.tag, i))
                         for i, (land, part) in enumerate(zip(carried, self.part))]
            return stage_pair_gather(self.mine)
        if self.step == 4:
            self.theirs = carried
        return None

    def run_alone(self, name):
        stage = self.advance(None)
        while stage is not None:
            stage = self.advance(run_stage(stage, name + "_%d" % self.step))

    def results(self):
        south = self.c_idx[0] == 0
        boths = [jnp.stack([jnp.where(south, m, t), jnp.where(south, t, m)], axis=0)
                 for m, t in zip(self.mine, self.theirs)]
        out = {n: b.reshape(self.shapes[n]) for n, b in zip(self.big, boths)}
        packed = boths[-1]
        flat = packed[:, :self.rs].reshape(2 * self.rs, LANE)
        out.update({n: unpack_rows(flat, off, self.shapes[n]) for n, off in zip(self.small, self.offs)})
        return out, packed[:, self.rs:self.rs + self.rr]


def kernel(x, meta_tokens, ev_norm, ev_w_in, ev_conv_a, ev_conv_b, ev_conv_b_bias, ev_gate_r_w, ev_gate_r_b, ev_gate_i_w, ev_gate_i_b, ev_lru_lambda, ev_w_out, od_norm, od_w_in, od_q_norm, od_kv_norm, od_w_uq, od_w_ukv, od_w_out, ffn_norm, ffn_w_up, ffn_conv_w, ffn_conv_b, ffn_w_down, final_norm, loss_target, m_meta_tokens, m_ev_norm, m_ev_w_in, m_ev_conv_a, m_ev_conv_b, m_ev_conv_b_bias, m_ev_gate_r_w, m_ev_gate_r_b, m_ev_gate_i_w, m_ev_gate_i_b, m_ev_lru_lambda, m_ev_w_out, m_od_norm, m_od_w_in, m_od_q_norm, m_od_kv_norm, m_od_w_uq, m_od_w_ukv, m_od_w_out, m_ffn_norm, m_ffn_w_up, m_ffn_conv_w, m_ffn_conv_b, m_ffn_w_down, m_final_norm, v_meta_tokens, v_ev_norm, v_ev_w_in, v_ev_conv_a, v_ev_conv_b, v_ev_conv_b_bias, v_ev_gate_r_w, v_ev_gate_r_b, v_ev_gate_i_w, v_ev_gate_i_b, v_ev_lru_lambda, v_ev_w_out, v_od_norm, v_od_w_in, v_od_q_norm, v_od_kv_norm, v_od_w_uq, v_od_w_ukv, v_od_w_out, v_ffn_norm, v_ffn_w_up, v_ffn_conv_w, v_ffn_conv_b, v_ffn_w_down, v_final_norm):
    given = dict(locals())
    w = {n: given[n] for n in WEIGHTS}
    nb, seq, _ = x.shape
    dm = Dims(nb, seq)
    n_layers = len(LAYER_ORDER)

    wf = {n: w[n] for n in WEIGHTS if n not in SHARD_AXIS}
    gathers = [GatherComm(w, kind, idx) for kind, idx in LAYER_ORDER]
    wf.update(gather_at_entry(w, [n for n in SHARD_AXIS if n not in MATMUL_WEIGHTS], gathers[0], "gather_at_entry"))

    tail = dm.tp - dm.t_real
    meta = jnp.broadcast_to(wf["meta_tokens"][None], (nb, N_META, D_MODEL))
    h = jnp.concatenate([meta, x, jnp.zeros((nb, tail, D_MODEL), F32)], axis=1).reshape(dm.rows, D_MODEL)
    tgt = jnp.pad(loss_target, ((0, 0), (N_META, tail), (0, 0))).reshape(dm.rows, D_MODEL)
    tabq, tabk = rope_tables(dm.tp)

    params, saved = [], []
    for i, (kind, idx) in enumerate(LAYER_ORDER):
        wl = dict(wf)
        wl.update(gathers[i].weights())
        comm = gathers[i + 1] if i + 1 < n_layers else NoComm()
        if kind == "ev":
            p = even_params(wl, idx)
            h, sv = even_fwd(h, p, dm, comm)
        elif kind == "od":
            p = odd_params(wl, idx)
            h, sv = odd_fwd(h, p, tabq, tabk, dm, comm)
        else:
            p = ffn_params(wl, idx)
            h, sv = ffn_fwd(h, p, dm, comm)
        params.append(p)
        saved.append(sv)

    dh, loss, dfinal = loss_head(h, tgt, _row(wf["final_norm"]), dm.tp, dm.t_real, dm.n, "loss_head")
    loss = lax.psum(loss[0, 0], ("x", "y", "c"))

    c_idx = lax.axis_index("c").astype(jnp.int32).reshape(1)
    layer_grads = {n: {} for n in WEIGHTS}
    pending, reduces = NoComm(), []
    for i in reversed(range(n_layers)):
        kind, idx = LAYER_ORDER[i]
        if kind == "ev":
            dh, g = even_bwd(dh, saved[i], params[i], dm, pending)
            g = even_grads(g)
        elif kind == "od":
            dh, g = odd_bwd(dh, saved[i], params[i], tabq, tabk, dm, pending)
            g = odd_grads(g)
        else:
            dh, g = ffn_bwd(dh, saved[i], params[i], dm, pending)
            g = ffn_grads(g)
        for n in g:
            if n not in SHARD_AXIS:
                layer_grads[n][idx] = g[n]
        if i > 0:
            pending = ReduceComm({n: g[n] for n in LAYER_SHARDED[kind]}, {n: SHARD_AXIS[n] - 1 for n in SHARD_AXIS},
                                 c_idx, "%s%d" % (kind, idx))
            reduces.append((pending, idx))
    dh3 = dh.reshape(nb, dm.tp, D_MODEL)
    grad_x = dh3[:, N_META:dm.t_real]

    repl = [n for n in WEIGHTS if n not in SHARD_AXIS]
    layer_grads["final_norm"] = {0: dfinal[0]}
    repl_full = {n: (layer_grads[n][0] if n == "final_norm" else
                     jnp.stack([layer_grads[n][j] for j in range(w[n].shape[0])], axis=0)) for n in repl}
    tail_buf, tail_offs = pack_rows([repl_full[n] for n in repl], 0, 64)
    first = {n: g[n] for n in LAYER_SHARDED["ev"]}
    first["meta_tokens"] = jnp.sum(dh3[:, :N_META], axis=0)
    axes = {n: SHARD_AXIS[n] - 1 for n in SHARD_AXIS}
    axes["meta_tokens"] = SHARD_AXIS["meta_tokens"]
    last = ReduceComm(first, axes, c_idx, "first_layer", tail=tail_buf)
    last.run_alone("grad_first_layer")
    reduces.append((last, 0))

    red = {}
    for comm, idx in reduces:
        got, tail_piece = comm.results()
        for n, v_ in got.items():
            if n == "meta_tokens":
                red[n] = v_
            else:
                layer_grads[n][idx] = v_
    tails = own_block(run_stage(stage_gather_chips([tail_piece]), "grad_gather_replicated")[0], tail_piece)
    tails = tails.reshape(tail_buf.shape[0], LANE)
    for n, off in zip(repl, tail_offs):
        red[n] = unpack_rows(tails, off, w[n].shape)
    for n in SHARD_AXIS:
        if n != "meta_tokens":
            red[n] = jnp.stack([layer_grads[n][j] for j in range(w[n].shape[0])], axis=0)

    outs = [adamw(red[n], w[n], given["m_" + n], given["v_" + n], "adamw_" + n) for n in WEIGHTS]
    return (loss, grad_x, *[red[n] for n in WEIGHTS], *[o[0] for o in outs], *[o[1] for o in outs],
            *[o[2] for o in outs])
```

```python
import math

import jax
import jax.numpy as jnp
from jax import lax
from jax.experimental import pallas as pl
from jax.experimental.pallas import tpu as pltpu

F32 = jnp.float32
MXU_DT = jnp.bfloat16
S = jax.ShapeDtypeStruct
MESH = pl.DeviceIdType.MESH

EPS = 1e-6
D_MODEL = 1024
N_META = 16
DEPTH = 4
CONV_W = 512
LRU_W = 512
LRU_HEADS = 8
LRU_C = 8.0
EVEN_IN = 2560
MLA_HEADS = 16
QK_NOPE = 64
QK_ROPE = 32
QK_HEAD = 96
V_HEAD = 64
Q_LORA = 384
KV_LORA = 256
ROPE_BASE = 10000.0
D_FF = 2816
ODD_PAD = 896
ODD_CKV_COL = 2
ODD_KR_COL = 6
HP = 128
ATT_BLK = 384
Q_PRESCALE = QK_HEAD ** -0.5 * math.log2(math.e)
FFN_CT = 256
EV_CT = 256
STRIP_ROWS = 352
LANE = 128
SUBLANE = 8
VMEM_LIMIT_MB = 52

ADAM_LR = 0.001
ADAM_B1 = 0.9
ADAM_B2 = 0.999
ADAM_EPS = 1e-08
ADAM_WD = 0.01
ADAM_STEP = 10

TN_ACC_ELEMS = 1536 * 1024
NT_DIMS = (((1,), (1,)), ((), ()))
TN_DIMS = (((0,), (0,)), ((), ()))


def _cp(sem):
    return pltpu.CompilerParams(dimension_semantics=sem, vmem_limit_bytes=VMEM_LIMIT_MB << 20)


def _div_tile(n, cap, mult):
    if n <= cap:
        return n
    best = None
    for t in range(mult, cap + 1, mult):
        if n % t == 0:
            best = t
    assert best is not None, (n, cap, mult)
    return best


def _round_up(n, m):
    return -(-n // m) * m


def mat_cols(arr):
    return arr.shape[1] if arr.ndim == 2 else arr.shape[0] * arr.shape[2]


def mat_width(arr):
    return arr.shape[-1]


def mat_spec(arr, tm, tw, rc):
    if arr.ndim == 2:
        return pl.BlockSpec((tm, tw), lambda *g: rc(*g))
    per = arr.shape[2] // tw
    assert arr.shape[2] % tw == 0

    def imap(*g):
        r, c = rc(*g)
        return (c // per, r, c % per)

    return pl.BlockSpec((None, tm, tw), imap)


HBM_SPEC = pl.BlockSpec(memory_space=pltpu.HBM)


class Stage:
    def __init__(self, inputs, out_shapes, sems, start, finish):
        self.inputs, self.out_shapes, self.sems, self.start, self.finish = inputs, out_shapes, sems, start, finish


def run_stage(stage, name):
    n_in, n_out = len(stage.inputs), len(stage.out_shapes)

    def body(*refs):
        ins, outs, sems = refs[:n_in], refs[n_in:n_in + n_out], refs[n_in + n_out:]
        stage.start(ins, outs, sems)
        stage.finish(ins, outs, sems)

    return pl.pallas_call(body, out_shape=list(stage.out_shapes), in_specs=[HBM_SPEC] * n_in,
                          out_specs=[HBM_SPEC] * n_out, scratch_shapes=list(stage.sems), name=name)(*stage.inputs)


def _call(body, ops, carry, *, grid, in_specs, out_specs, out_shape, scratch_shapes, sem, name):
    if carry is None:
        outs = pl.pallas_call(body, grid=grid, in_specs=in_specs, out_specs=out_specs, out_shape=out_shape,
                              scratch_shapes=scratch_shapes, compiler_params=_cp(sem), name=name)(*ops)
        return outs, None
    multi = isinstance(out_shape, (list, tuple))
    shapes = list(out_shape) if multi else [out_shape]
    ospecs = list(out_specs) if multi else [out_specs]
    n_in, n_out, n_sc = len(ops), len(shapes), len(scratch_shapes)
    c_in, c_out = len(carry.inputs), len(carry.out_shapes)

    def wrapped(*refs):
        ins, cin = refs[:n_in], refs[n_in:n_in + c_in]
        o0 = n_in + c_in
        outs, cout = refs[o0:o0 + n_out], refs[o0 + n_out:o0 + n_out + c_out]
        s0 = o0 + n_out + c_out
        scs, csems = refs[s0:s0 + n_sc], refs[s0 + n_sc:]
        first = pl.program_id(0) == 0
        last = pl.program_id(0) == grid[0] - 1
        for d in range(1, len(grid)):
            first = first & (pl.program_id(d) == 0)
            last = last & (pl.program_id(d) == grid[d] - 1)

        @pl.when(first)
        def _():
            carry.start(cin, cout, csems)

        body(*ins, *outs, *scs)

        @pl.when(last)
        def _():
            carry.finish(cin, cout, csems)

    res = pl.pallas_call(
        wrapped, grid=grid, in_specs=list(in_specs) + [HBM_SPEC] * c_in, out_specs=ospecs + [HBM_SPEC] * c_out,
        out_shape=shapes + list(carry.out_shapes), scratch_shapes=list(scratch_shapes) + list(carry.sems),
        compiler_params=_cp(("arbitrary",) * len(grid)), name=name)(*ops, *carry.inputs)
    main = res[:n_out]
    return (list(main) if multi else main[0]), list(res[n_out:])


def norm_matmul(x, xcol, kdim, gain, w, tm, tn, out_dtype, name, epi=None, epi_ops=(), epi_specs=(), carry=None):
    rows, n = x.shape[0], mat_cols(w) if w.ndim == 3 else w.shape[1]
    n_epi = len(epi_ops)
    w_spec = (pl.BlockSpec((kdim, tn), lambda i, j: (0, j)) if w.ndim == 2 else
              pl.BlockSpec((None, kdim, tn), lambda i, j: (j // (w.shape[2] // tn), 0, j % (w.shape[2] // tn))))

    def body(x_ref, g_ref, w_ref, *rest):
        epi_refs = rest[:n_epi]
        out_ref, xn_ref, xn_sc = rest[n_epi:]

        @pl.when(pl.program_id(1) == 0)
        def _():
            xv = x_ref[...]
            y = xv * lax.rsqrt(jnp.mean(xv * xv, axis=-1, keepdims=True) + EPS)
            xn = (y * g_ref[...]).astype(MXU_DT)
            xn_sc[...] = xn
            xn_ref[...] = xn

        acc = jnp.dot(xn_sc[...], w_ref[...], preferred_element_type=F32)
        if epi is not None:
            acc = epi(acc, *[r[...] for r in epi_refs])
        out_ref[...] = acc.astype(out_dtype)

    return _call(
        body, (x, gain, w, *epi_ops), carry, grid=(rows // tm, n // tn),
        in_specs=[pl.BlockSpec((tm, kdim), lambda i, j: (i, xcol)), pl.BlockSpec((1, kdim), lambda i, j: (0, 0)),
                  w_spec, *epi_specs],
        out_specs=[pl.BlockSpec((tm, tn), lambda i, j: (i, j)), pl.BlockSpec((tm, kdim), lambda i, j: (i, 0))],
        out_shape=[S((rows, n), out_dtype), S((rows, kdim), MXU_DT)],
        scratch_shapes=[pltpu.VMEM((tm, kdim), MXU_DT)], sem=("parallel", "arbitrary"), name=name)


def matmul_res(a, w, res, tm, tn, name):
    grp, rows, k = a.shape
    n = w.shape[2]

    def body(a_ref, w_ref, r_ref, o_ref):
        acc = r_ref[...]
        for g in range(grp):
            acc = acc + jnp.dot(a_ref[g], w_ref[g], preferred_element_type=F32)
        o_ref[...] = acc

    return pl.pallas_call(
        body, grid=(rows // tm, n // tn),
        in_specs=[pl.BlockSpec((grp, tm, k), lambda i, j: (0, i, 0)), pl.BlockSpec((grp, k, tn), lambda i, j: (0, 0, j)),
                  pl.BlockSpec((tm, tn), lambda i, j: (i, j))],
        out_specs=pl.BlockSpec((tm, tn), lambda i, j: (i, j)),
        out_shape=S((rows, n), F32), compiler_params=_cp(("parallel", "parallel")), name=name)(a, w, res)


def matmul_nt(a, w, tm, tn, out_dtype, name):
    rows, k = a.shape
    n = w.shape[0]

    def body(a_ref, w_ref, o_ref):
        o_ref[...] = lax.dot_general(a_ref[...].astype(MXU_DT), w_ref[...], NT_DIMS,
                                     preferred_element_type=F32).astype(out_dtype)

    return pl.pallas_call(
        body, grid=(rows // tm, n // tn),
        in_specs=[pl.BlockSpec((tm, k), lambda i, j: (i, 0)), pl.BlockSpec((tn, k), lambda i, j: (j, 0))],
        out_specs=pl.BlockSpec((tm, tn), lambda i, j: (i, j)),
        out_shape=S((rows, n), out_dtype), compiler_params=_cp(("parallel", "parallel")), name=name)(a, w)


def matmul_nt_normbwd(du, w, x, xcol, gain, res, tm, tk, out_dtype, name):
    rows, kc = du.shape[-2], mat_cols(du)
    dn = w.shape[-2]
    nk = kc // tk
    has_res = res is not None
    w_spec = (pl.BlockSpec((dn, tk), lambda i, k: (0, k)) if w.ndim == 2 else
              pl.BlockSpec((None, dn, tk), lambda i, k: (k // (w.shape[2] // tk), 0, k % (w.shape[2] // tk))))

    def body(du_ref, w_ref, x_ref, g_ref, *rest):
        if has_res:
            res_ref, dx_ref, dg_ref, acc = rest
        else:
            dx_ref, dg_ref, acc = rest
        i, k = pl.program_id(0), pl.program_id(1)

        @pl.when(k == 0)
        def _():
            acc[...] = jnp.zeros_like(acc)

        @pl.when((i == 0) & (k == 0))
        def _():
            dg_ref[...] = jnp.zeros_like(dg_ref)

        acc[...] += lax.dot_general(du_ref[...], w_ref[...], NT_DIMS, preferred_element_type=F32)

        @pl.when(k == nk - 1)
        def _():
            dhn = acc[...]
            xv = x_ref[...]
            rstd = lax.rsqrt(jnp.mean(xv * xv, axis=-1, keepdims=True) + EPS)
            xhat = xv * rstd
            dg_ref[...] += jnp.sum(dhn * xhat, axis=0, keepdims=True)
            dxh = dhn * g_ref[...]
            dx = rstd * (dxh - xhat * jnp.mean(dxh * xhat, axis=-1, keepdims=True))
            if has_res:
                dx = dx + res_ref[...]
            dx_ref[...] = dx.astype(out_dtype)

    in_specs = [mat_spec(du, tm, tk, lambda i, k: (i, k)), w_spec,
                pl.BlockSpec((tm, dn), lambda i, k: (i, xcol)), pl.BlockSpec((1, dn), lambda i, k: (0, 0))]
    ops = [du, w, x, gain]
    if has_res:
        in_specs.append(pl.BlockSpec((tm, dn), lambda i, k: (i, 0)))
        ops.append(res)
    return pl.pallas_call(
        body, grid=(rows // tm, nk), in_specs=in_specs,
        out_specs=[pl.BlockSpec((tm, dn), lambda i, k: (i, 0)), pl.BlockSpec((1, dn), lambda i, k: (0, 0))],
        out_shape=[S((rows, dn), out_dtype), S((1, dn), F32)],
        scratch_shapes=[pltpu.VMEM((tm, dn), F32)],
        compiler_params=_cp(("arbitrary", "arbitrary")), name=name)(*ops)


def matmul_tn(a, b, tr, name, carry=None, col_shards=1):
    rows, ka, nb = a.shape[-2], mat_cols(a), mat_cols(b)
    ta = _div_tile(mat_width(a), 1536, LANE)
    tb = _div_tile(mat_width(b), max(LANE, TN_ACC_ELEMS // ta // LANE * LANE), LANE)
    nr = rows // tr
    if col_shards == 1:
        out_spec, out_shape = pl.BlockSpec((ta, tb), lambda i, j, r: (i, j)), S((ka, nb), F32)
    else:
        per = nb // col_shards // tb
        assert per * tb * col_shards == nb
        out_spec = pl.BlockSpec((None, ta, tb), lambda i, j, r: (j // per, i, j % per))
        out_shape = S((col_shards, ka, nb // col_shards), F32)

    def body(a_ref, b_ref, o_ref, acc):
        r = pl.program_id(2)

        @pl.when(r == 0)
        def _():
            acc[...] = jnp.zeros_like(acc)

        acc[...] += lax.dot_general(a_ref[...].astype(MXU_DT), b_ref[...].astype(MXU_DT), TN_DIMS,
                                    preferred_element_type=F32)

        @pl.when(r == nr - 1)
        def _():
            o_ref[...] = acc[...]

    return _call(
        body, (a, b), carry, grid=(ka // ta, nb // tb, nr),
        in_specs=[mat_spec(a, tr, ta, lambda i, j, r: (r, i)), mat_spec(b, tr, tb, lambda i, j, r: (r, j))],
        out_specs=out_spec, out_shape=out_shape, scratch_shapes=[pltpu.VMEM((ta, tb), F32)],
        sem=("parallel", "parallel", "arbitrary"), name=name)


def _sigmoid(x):
    return 1.0 / (1.0 + jnp.exp(-x))


def _sigmoid_by_tanh(x):
    return 0.5 * jnp.tanh(0.5 * x) + 0.5


def _log1p(e):
    return jnp.where(e < 1e-3, e * (1.0 - e * (0.5 - e * (1.0 / 3.0 - 0.25 * e))), jnp.log(1.0 + e))


def _softplus(x):
    return jnp.maximum(x, 0.0) + _log1p(jnp.exp(-jnp.abs(x)))


def _expm1(x):
    series = x * (1.0 + x * (0.5 + x * (1.0 / 6.0 + x * (1.0 / 24.0 + x * (1.0 / 120.0)))))
    return jnp.where(jnp.abs(x) < 0.1, series, jnp.exp(x) - 1.0)


_GELU_K = math.sqrt(2.0 / math.pi)
_GELU_C = 0.044715


def _gelu_and_grad(x):
    th = jnp.tanh(_GELU_K * (x + _GELU_C * x * x * x))
    g = 0.5 * x * (1.0 + th)
    dg = 0.5 * (1.0 + th) + 0.5 * x * (1.0 - th * th) * _GELU_K * (1.0 + 3.0 * _GELU_C * x * x)
    return g, dg


def _row_iota(shape):
    return lax.broadcasted_iota(jnp.int32, shape, 0)


def _scan_chunk_fwd(a_sc, u_sc, out_ref, hcar, n, width):
    rowi = _row_iota((SUBLANE, width))

    def step(c, hprev):
        r0 = pl.multiple_of(c * SUBLANE, SUBLANE)
        a = a_sc[pl.ds(r0, SUBLANE), :]
        u = u_sc[pl.ds(r0, SUBLANE), :]
        for d in (1, 2, 4):
            a_s = jnp.where(rowi >= d, pltpu.roll(a, d, axis=0), 1.0)
            u_s = jnp.where(rowi >= d, pltpu.roll(u, d, axis=0), 0.0)
            u = u + a * u_s
            a = a * a_s
        h = u + a * hprev
        out_ref[pl.ds(r0, SUBLANE), :] = h
        return jnp.broadcast_to(h[SUBLANE - 1:SUBLANE, :], (SUBLANE, width))

    hcar[...] = lax.fori_loop(0, n // SUBLANE, step, hcar[...], unroll=4)


def _scan_chunk_bwd(b_sc, d_sc, out_ref, gcar, n, width):
    rowi = _row_iota((SUBLANE, width))
    nc = n // SUBLANE

    def step(c, gnext):
        r0 = pl.multiple_of((nc - 1 - c) * SUBLANE, SUBLANE)
        b = b_sc[pl.ds(r0, SUBLANE), :]
        d = d_sc[pl.ds(r0, SUBLANE), :]
        for s in (1, 2, 4):
            keep = rowi < SUBLANE - s
            b_s = jnp.where(keep, pltpu.roll(b, SUBLANE - s, axis=0), 1.0)
            d_s = jnp.where(keep, pltpu.roll(d, SUBLANE - s, axis=0), 0.0)
            d = d + b * d_s
            b = b * b_s
        g = d + b * gnext
        out_ref[pl.ds(r0, SUBLANE), :] = g
        return jnp.broadcast_to(g[0:1, :], (SUBLANE, width))

    gcar[...] = lax.fori_loop(0, nc, step, gcar[...], unroll=4)


def even_mid_fwd(u, conv_a, conv_b, conv_b_bias, rw, rb, iw, ib, lam, nb, tp, n, name, carry=None):
    rows = u.shape[0]
    w = EV_CT
    nj = CONV_W // w
    nt = tp // n
    h8 = SUBLANE

    def body(gb_r, gc_r, xa_r, xb_r, gate_r, ca_w, cb_w, cb_b, rw_r, rb_r, iw_r, ib_r, lam_r,
             y_o, ca_o, xc_o, a_o, hs_o, pext, xext, hcar, a_sc, u_sc):
        t = pl.program_id(2)

        @pl.when(t == 0)
        def _():
            pext[0:h8, :] = jnp.zeros((h8, w), F32)
            xext[0:h8, :] = jnp.zeros((h8, w), F32)
            hcar[...] = jnp.zeros_like(hcar)

        p = gc_r[...] * xa_r[...]
        pext[h8:h8 + n, :] = p
        wa = ca_w[...]
        ca = wa[2:3, :] * p + wa[1:2, :] * pext[h8 - 1:h8 - 1 + n, :] + wa[0:1, :] * pext[h8 - 2:h8 - 2 + n, :]
        ca_o[...] = ca
        y_o[0] = (gb_r[...] * ca).astype(MXU_DT)
        pext[0:h8, :] = pext[n:n + h8, :]

        xb = xb_r[...]
        xext[h8:h8 + n, :] = xb
        wb = cb_w[...]
        xc = (wb[3:4, :] * xb + wb[2:3, :] * xext[h8 - 1:h8 - 1 + n, :] + wb[1:2, :] * xext[h8 - 2:h8 - 2 + n, :]
              + wb[0:1, :] * xext[h8 - 3:h8 - 3 + n, :]) + cb_b[...]
        xc_o[...] = xc
        xext[0:h8, :] = xext[n:n + h8, :]

        xcm = xc.astype(MXU_DT)
        r = _sigmoid(jnp.dot(xcm, rw_r[...], preferred_element_type=F32) + rb_r[...])
        ig = _sigmoid(jnp.dot(xcm, iw_r[...], preferred_element_type=F32) + ib_r[...])
        log_a = (-LRU_C) * r * _softplus(-lam_r[...])
        a = jnp.exp(log_a)
        mult = jnp.sqrt(-_expm1(2.0 * log_a))
        a_sc[...] = a
        a_o[...] = a
        u_sc[...] = mult * (ig * xc)
        _scan_chunk_fwd(a_sc, u_sc, hs_o, hcar, n, w)
        gel, _ = _gelu_and_grad(gate_r[...])
        y_o[1] = (gel * hs_o[...]).astype(MXU_DT)

    def ublk(off):
        return pl.BlockSpec((n, w), lambda j, b, t: (b * nt + t, off + j))

    def pblk(r_):
        return pl.BlockSpec((r_, w), lambda j, b, t: (0, j))

    act = pl.BlockSpec((n, w), lambda j, b, t: (b * nt + t, j))
    mat = pl.BlockSpec((w, w), lambda j, b, t: (j, j))
    return _call(
        body, (u, u, u, u, u, conv_a, conv_b, conv_b_bias, rw, rb, iw, ib, lam), carry, grid=(nj, nb, nt),
        in_specs=[ublk(0), ublk(nj), ublk(2 * nj), ublk(3 * nj), ublk(4 * nj), pblk(3), pblk(4), pblk(1),
                  mat, pblk(1), mat, pblk(1), pblk(1)],
        out_specs=[pl.BlockSpec((2, n, w), lambda j, b, t: (0, b * nt + t, j)), act, act, act, act],
        out_shape=[S((2, rows, CONV_W), MXU_DT), S((rows, CONV_W), F32), S((rows, LRU_W), F32), S((rows, LRU_W), F32),
                   S((rows, LRU_W), F32)],
        scratch_shapes=[pltpu.VMEM((n + h8, w), F32), pltpu.VMEM((n + h8, w), F32), pltpu.VMEM((h8, w), F32),
                        pltpu.VMEM((n, w), F32), pltpu.VMEM((n, w), F32)],
        sem=("parallel", "parallel", "arbitrary"), name=name)


def even_mid_bwd(u, dycat, ca, xc, a_sv, hs, conv_a, conv_b, rw, rb, iw, ib, lam, nb, tp, n, name, carry=None):
    rows = u.shape[0]
    w = EV_CT
    nj = CONV_W // w
    nt = tp // n
    h8 = SUBLANE

    def body(gb_r, gc_r, xa_r, xb_r, gate_r, dya_r, dyb_r, ca_r, xc_r, a_r, hs_r, hsp_r,
             ca_w, cb_w, rw_r, rb_r, iw_r, ib_r, lam_r,
             du_o, dca_w, dcb_w, dcb_b, drw, drb, diw, dib, dlam,
             aext, hext, dext, eext, gcar, b_sc, d_sc, g_sc):
        b, t = pl.program_id(1), pl.program_id(2)

        @pl.when((b == 0) & (t == 0))
        def _():
            for ref in (dca_w, dcb_w, dcb_b, drw, drb, diw, dib, dlam):
                ref[...] = jnp.zeros_like(ref)

        @pl.when(t == 0)
        def _():
            aext[n:n + h8, :] = jnp.zeros((h8, w), F32)
            dext[n:n + h8, :] = jnp.zeros((h8, w), F32)
            eext[n:n + h8, :] = jnp.zeros((h8, w), F32)
            gcar[...] = jnp.zeros_like(gcar)

        xc_v = xc_r[...]
        xcm = xc_v.astype(MXU_DT)
        r = _sigmoid(jnp.dot(xcm, rw_r[...], preferred_element_type=F32) + rb_r[...])
        ig = _sigmoid(jnp.dot(xcm, iw_r[...], preferred_element_type=F32) + ib_r[...])
        lam_v = lam_r[...]
        sp = _softplus(-lam_v)
        log_a = (-LRU_C) * r * sp
        a = a_r[...]
        mult = jnp.sqrt(-_expm1(2.0 * log_a))
        hs_v = hs_r[...]
        gel, dgel = _gelu_and_grad(gate_r[...])
        dyb = dyb_r[...]
        du_o[4] = (dyb * hs_v * dgel).astype(MXU_DT)

        aext[0:n, :] = a
        b_sc[...] = aext[1:1 + n, :]
        d_sc[...] = dyb * gel
        _scan_chunk_bwd(b_sc, d_sc, g_sc, gcar, n, w)
        aext[n:n + h8, :] = aext[0:h8, :]
        g = g_sc[...]

        hext[0:h8, :] = jnp.where(t == nt - 1, 0.0, hsp_r[...])
        hext[h8:h8 + n, :] = hs_v
        da = g * hext[h8 - 1:h8 - 1 + n, :]
        dmult = g * (ig * xc_v)
        di = g * mult * xc_v
        dxc = g * mult * ig
        dlog_a = da * a - dmult * (a * a) / mult
        dr = dlog_a * ((-LRU_C) * sp)
        dsp = jnp.sum(dlog_a * ((-LRU_C) * r), axis=0, keepdims=True)
        dlam[...] += dsp * (-_sigmoid(-lam_v))
        dzr = dr * r * (1.0 - r)
        dzi = di * ig * (1.0 - ig)
        dzr_m = dzr.astype(MXU_DT)
        dzi_m = dzi.astype(MXU_DT)
        dxc = (dxc + lax.dot_general(dzr_m, rw_r[...], NT_DIMS, preferred_element_type=F32)
               + lax.dot_general(dzi_m, iw_r[...], NT_DIMS, preferred_element_type=F32))
        drw[...] += lax.dot_general(xcm, dzr_m, TN_DIMS, preferred_element_type=F32)
        diw[...] += lax.dot_general(xcm, dzi_m, TN_DIMS, preferred_element_type=F32)
        drb[...] += jnp.sum(dzr, axis=0, keepdims=True)
        dib[...] += jnp.sum(dzi, axis=0, keepdims=True)
        dcb_b[...] += jnp.sum(dxc, axis=0, keepdims=True)

        xb = xb_r[...]
        dext[0:n, :] = dxc
        wb = cb_w[...]
        d1, d2, d3 = dext[1:1 + n, :], dext[2:2 + n, :], dext[3:3 + n, :]
        du_o[3] = (wb[3:4, :] * dxc + wb[2:3, :] * d1 + wb[1:2, :] * d2 + wb[0:1, :] * d3).astype(MXU_DT)
        dcb_w[3:4, :] += jnp.sum(xb * dxc, axis=0, keepdims=True)
        dcb_w[2:3, :] += jnp.sum(xb * d1, axis=0, keepdims=True)
        dcb_w[1:2, :] += jnp.sum(xb * d2, axis=0, keepdims=True)
        dcb_w[0:1, :] += jnp.sum(xb * d3, axis=0, keepdims=True)
        dext[n:n + h8, :] = dext[0:h8, :]

        gb, gc, xa = gb_r[...], gc_r[...], xa_r[...]
        dya = dya_r[...]
        du_o[0] = (dya * ca_r[...]).astype(MXU_DT)
        dca = dya * gb
        eext[0:n, :] = dca
        wa = ca_w[...]
        e1, e2 = eext[1:1 + n, :], eext[2:2 + n, :]
        dp = wa[2:3, :] * dca + wa[1:2, :] * e1 + wa[0:1, :] * e2
        p = gc * xa
        dca_w[2:3, :] += jnp.sum(p * dca, axis=0, keepdims=True)
        dca_w[1:2, :] += jnp.sum(p * e1, axis=0, keepdims=True)
        dca_w[0:1, :] += jnp.sum(p * e2, axis=0, keepdims=True)
        eext[n:n + h8, :] = eext[0:h8, :]
        du_o[1] = (dp * xa).astype(MXU_DT)
        du_o[2] = (dp * gc).astype(MXU_DT)

    def rt(b, t):
        return b * nt + (nt - 1 - t)

    def ublk(off):
        return pl.BlockSpec((n, w), lambda j, b, t: (rt(b, t), off + j))

    def pblk(r_):
        return pl.BlockSpec((r_, w), lambda j, b, t: (0, j))

    act = pl.BlockSpec((n, w), lambda j, b, t: (rt(b, t), j))
    n8 = n // h8
    hsp = pl.BlockSpec((h8, w), lambda j, b, t: (jnp.maximum(rt(b, t) * n8 - 1, 0), j))
    mat = pl.BlockSpec((w, w), lambda j, b, t: (j, j))
    return _call(
        body, (u, u, u, u, u, dycat, dycat, ca, xc, a_sv, hs, hs, conv_a, conv_b, rw, rb, iw, ib, lam), carry,
        grid=(nj, nb, nt),
        in_specs=[ublk(0), ublk(nj), ublk(2 * nj), ublk(3 * nj), ublk(4 * nj), ublk(0), ublk(nj), act, act, act, act,
                  hsp, pblk(3), pblk(4), mat, pblk(1), mat, pblk(1), pblk(1)],
        out_specs=[pl.BlockSpec((5, n, w), lambda j, b, t: (0, rt(b, t), j)), pblk(3), pblk(4), pblk(1),
                   mat, pblk(1), mat, pblk(1), pblk(1)],
        out_shape=[S((5, rows, CONV_W), MXU_DT), S((3, CONV_W), F32), S((4, LRU_W), F32), S((1, LRU_W), F32),
                   S((LRU_W, LRU_W), F32), S((1, LRU_W), F32), S((LRU_W, LRU_W), F32), S((1, LRU_W), F32),
                   S((1, LRU_W), F32)],
        scratch_shapes=[pltpu.VMEM((n + h8, w), F32)] * 4 + [pltpu.VMEM((h8, w), F32)] + [pltpu.VMEM((n, w), F32)] * 3,
        sem=("arbitrary", "arbitrary", "arbitrary"), name=name)


def ffn_mid_fwd(up, cw, cb, nb, tp, n, name):
    rows = up.shape[0]
    w = FFN_CT
    nj = D_FF // w
    nt = tp // n
    h8 = SUBLANE

    sr = STRIP_ROWS
    assert n % sr == 0, (n, sr)

    def body(xa_r, xg_r, w_r, b_r, u_o, y_o, halo):
        t = pl.program_id(2)

        @pl.when(t == 0)
        def _():
            halo[...] = jnp.zeros_like(halo)

        wv = (w_r[0], w_r[1])
        bv = (b_r[0], b_r[1])

        def strip(s, carry):
            r0 = pl.multiple_of(s * sr, sr)
            us, new = [], []
            for g, x_r in enumerate((xa_r, xg_r)):
                x = x_r[pl.ds(r0, sr), :].astype(F32)
                win = jnp.concatenate([carry[g], x], axis=0)
                x1 = pltpu.roll(win, 1, axis=0)[h8:, :]
                x2 = pltpu.roll(win, 2, axis=0)[h8:, :]
                u = (wv[g][2:3, :] * x + wv[g][1:2, :] * x1 + wv[g][0:1, :] * x2) + bv[g]
                u_o[g, pl.ds(r0, sr), :] = u.astype(MXU_DT)
                us.append(u)
                new.append(x[sr - h8:, :])
            y_o[pl.ds(r0, sr), :] = (us[0] * _sigmoid_by_tanh(us[0]) * us[1]).astype(MXU_DT)
            return tuple(new)

        ha, hg = lax.fori_loop(0, n // sr, strip, (halo[0], halo[1]))
        halo[0] = ha
        halo[1] = hg

    def ublk(off):
        return pl.BlockSpec((n, w), lambda j, b, t: (b * nt + t, off + j))

    return pl.pallas_call(
        body, grid=(nj, nb, nt),
        in_specs=[ublk(0), ublk(nj), pl.BlockSpec((2, 3, w), lambda j, b, t: (0, 0, j)),
                  pl.BlockSpec((2, 1, w), lambda j, b, t: (0, 0, j))],
        out_specs=[pl.BlockSpec((2, n, w), lambda j, b, t: (0, b * nt + t, j)), ublk(0)],
        out_shape=[S((2, rows, D_FF), MXU_DT), S((rows, D_FF), MXU_DT)],
        scratch_shapes=[pltpu.VMEM((2, h8, w), F32)],
        compiler_params=_cp(("parallel", "parallel", "arbitrary")), name=name,
    )(up, up, cw, cb)


def ffn_mid_bwd(dy, u, up, cw, nb, tp, n, name, carry=None):
    rows = up.shape[0]
    w = FFN_CT
    nj = D_FF // w
    nt = tp // n
    h8 = SUBLANE

    sr = STRIP_ROWS
    assert n % sr == 0, (n, sr)
    ns = n // sr

    def fold(v):
        acc = v[0:h8, :]
        for k in range(1, sr // h8):
            acc = acc + v[k * h8:(k + 1) * h8, :]
        return acc

    def body(dy_r, u_r, xa_r, xg_r, w_r, dx_o, dw, db, halo):
        b, t = pl.program_id(1), pl.program_id(2)

        @pl.when((b == 0) & (t == 0))
        def _():
            dw[...] = jnp.zeros_like(dw)
            db[...] = jnp.zeros_like(db)

        @pl.when(t == 0)
        def _():
            halo[...] = jnp.zeros_like(halo)

        wv = (w_r[0], w_r[1])

        def strip(s, carry):
            halos, sums = carry
            r0 = pl.multiple_of((ns - 1 - s) * sr, sr)
            dyv = dy_r[pl.ds(r0, sr), :].astype(F32)
            ua = u_r[0, pl.ds(r0, sr), :].astype(F32)
            ug = u_r[1, pl.ds(r0, sr), :].astype(F32)
            sg = _sigmoid_by_tanh(ua)
            dus = (dyv * ug * (sg * (1.0 + ua * (1.0 - sg))), dyv * (ua * sg))
            new_halos, new_sums = [], []
            for g, x_r in enumerate((xa_r, xg_r)):
                du = dus[g]
                win = jnp.concatenate([du, halos[g]], axis=0)
                d1 = pltpu.roll(win, sr + h8 - 1, axis=0)[0:sr, :]
                d2 = pltpu.roll(win, sr + h8 - 2, axis=0)[0:sr, :]
                dx_o[g, pl.ds(r0, sr), :] = (wv[g][2:3, :] * du + wv[g][1:2, :] * d1 + wv[g][0:1, :] * d2).astype(MXU_DT)
                x = x_r[pl.ds(r0, sr), :].astype(F32)
                s2, s1, s0, sb = sums[g]
                new_sums.append((s2 + fold(x * du), s1 + fold(x * d1), s0 + fold(x * d2), sb + fold(du)))
                new_halos.append(du[0:h8, :])
            return tuple(new_halos), tuple(new_sums)

        z = jnp.zeros((h8, w), F32)
        halos, sums = lax.fori_loop(0, ns, strip, ((halo[0], halo[1]), ((z, z, z, z), (z, z, z, z))))
        halo[0] = halos[0]
        halo[1] = halos[1]
        for g in range(2):
            s2, s1, s0, sb = sums[g]
            dw[g, 2:3, :] += jnp.sum(s2, axis=0, keepdims=True)
            dw[g, 1:2, :] += jnp.sum(s1, axis=0, keepdims=True)
            dw[g, 0:1, :] += jnp.sum(s0, axis=0, keepdims=True)
            db[g] += jnp.sum(sb, axis=0, keepdims=True)

    def rt(b, t):
        return b * nt + (nt - 1 - t)

    def ublk(off):
        return pl.BlockSpec((n, w), lambda j, b, t: (rt(b, t), off + j))

    pair = pl.BlockSpec((2, n, w), lambda j, b, t: (0, rt(b, t), j))
    return _call(
        body, (dy, u, up, up, cw), carry, grid=(nj, nb, nt),
        in_specs=[ublk(0), pair, ublk(0), ublk(nj), pl.BlockSpec((2, 3, w), lambda j, b, t: (0, 0, j))],
        out_specs=[pair, pl.BlockSpec((2, 3, w), lambda j, b, t: (0, 0, j)),
                   pl.BlockSpec((2, 1, w), lambda j, b, t: (0, 0, j))],
        out_shape=[S((2, rows, D_FF), MXU_DT), S((2, 3, D_FF), F32), S((2, 1, D_FF), F32)],
        scratch_shapes=[pltpu.VMEM((2, h8, w), F32)],
        sem=("arbitrary", "arbitrary", "arbitrary"), name=name)


def _lane_mod(shape):
    return lax.broadcasted_iota(jnp.int32, shape, 1) & (HP - 1)


def _q_rope_epi(acc, tab):
    reps = acc.shape[1] // HP
    a = acc * jnp.tile(tab, (1, reps))
    lane = _lane_mod(a.shape)
    shifted = pltpu.roll(a, a.shape[1] - QK_ROPE, axis=1)
    return jnp.where(lane < QK_NOPE, a, jnp.where(lane < QK_HEAD, a + shifted, 0.0)) * Q_PRESCALE


def _k_rope_block(krblk, tabk):
    a = krblk * tabk
    lane = _lane_mod(a.shape)
    b = a + pltpu.roll(a, HP - QK_ROPE, axis=1)
    return jnp.where((lane >= QK_NOPE) & (lane < QK_HEAD), b, 0.0)


def _k_rope_epi(acc, krblk, tabk):
    reps = acc.shape[1] // HP
    return acc + jnp.tile(_k_rope_block(krblk, tabk), (1, reps))


def attn_fwd(q, k, v, nb, tp, name, carry=None):
    rows = q.shape[0]
    blk = ATT_BLK
    nq = tp // blk
    npair = MLA_HEADS // 2

    def body(q_r, k_r, v_r, o_r, lse_r):
        qi = pl.program_id(2)
        lane = lax.broadcasted_iota(jnp.int32, (blk, LANE), 1)
        even = lane < V_HEAD
        sum_lane = (V_HEAD, 0)
        rowi = lax.broadcasted_iota(jnp.int32, (blk, blk), 0)
        coli = lax.broadcasted_iota(jnp.int32, (blk, blk), 1)
        qs = [q_r[:, h * HP:(h + 1) * HP] for h in range(2)]

        def kv_block(k0, width, carry, visible):
            ms, accs = carry
            vblk = v_r[pl.ds(k0, width), :]
            one = jnp.ones_like(vblk)
            zero = jnp.zeros_like(vblk)
            vlane = lax.broadcasted_iota(jnp.int32, (width, LANE), 1)
            ss = [lax.dot_general(qs[h], k_r[pl.ds(k0, width), h * HP:(h + 1) * HP], NT_DIMS,
                                  preferred_element_type=F32) for h in range(2)]
            new_ms, new_accs = [], []
            for h in range(2):
                s = ss[h]
                if visible is not None:
                    s = jnp.where(visible, s, -jnp.inf)
                m_new = jnp.maximum(ms[h], jnp.max(s, axis=1, keepdims=True))
                alpha = jnp.exp2(ms[h] - m_new)
                p = jnp.exp2(s - m_new).astype(MXU_DT)
                mine = (vlane < V_HEAD) if h == 0 else (vlane >= V_HEAD)
                vh = jnp.where(mine, vblk, jnp.where(vlane == sum_lane[h], one, zero))
                new_accs.append(alpha * accs[h] + jnp.dot(p, vh, preferred_element_type=F32))
                new_ms.append(m_new)
            return tuple(new_ms), tuple(new_accs)

        neg = jnp.full((blk, 1), -jnp.inf, F32)
        zacc = jnp.zeros((blk, LANE), F32)
        carry = lax.fori_loop(0, qi // 2, lambda i, c: kv_block(pl.multiple_of(i * 2 * blk, blk), 2 * blk, c, None),
                              ((neg, neg), (zacc, zacc)))
        rowi2 = lax.broadcasted_iota(jnp.int32, (blk, 2 * blk), 0)
        coli2 = lax.broadcasted_iota(jnp.int32, (blk, 2 * blk), 1)
        ms, accs = lax.cond(
            qi % 2 == 1,
            lambda c: kv_block(pl.multiple_of((qi - 1) * blk, blk), 2 * blk, c, coli2 - blk <= rowi2),
            lambda c: kv_block(pl.multiple_of(qi * blk, blk), blk, c, coli <= rowi), carry)
        ls = [accs[h][:, sum_lane[h]:sum_lane[h] + 1] for h in range(2)]
        o_r[...] = jnp.where(even, accs[0] / ls[0], accs[1] / ls[1]).astype(MXU_DT)
        lse_r[...] = jnp.where(even, ms[0] + jnp.log2(ls[0]), ms[1] + jnp.log2(ls[1]))

    return _call(
        body, (q, k, v), carry, grid=(nb, npair, nq),
        in_specs=[pl.BlockSpec((blk, 2 * HP), lambda b, p, i: (b * nq + i, p)),
                  pl.BlockSpec((tp, 2 * HP), lambda b, p, i: (b, p)),
                  pl.BlockSpec((tp, LANE), lambda b, p, i: (b, p))],
        out_specs=[pl.BlockSpec((blk, LANE), lambda b, p, i: (b * nq + i, p)),
                   pl.BlockSpec((None, blk, LANE), lambda b, p, i: (p, b * nq + i, 0))],
        out_shape=[S((rows, MLA_HEADS * V_HEAD), MXU_DT), S((npair, rows, LANE), F32)], scratch_shapes=[],
        sem=("parallel", "parallel", "arbitrary"), name=name)


def attn_bwd(q, k, v, o, do, lse, nb, tp, name, carry=None):
    rows = q.shape[0]
    blk = ATT_BLK
    nq = tp // blk
    npair = MLA_HEADS // 2
    scale = QK_HEAD ** -0.5

    def body(q_r, k_r, v_r, o_r, do_r, lse_r, dq_o, dk_o, dv_o, dq_acc, delta_sc):
        kb = pl.program_id(2)
        even = lax.broadcasted_iota(jnp.int32, (blk, LANE), 1) < V_HEAD
        rowi = lax.broadcasted_iota(jnp.int32, (blk, blk), 0)
        coli = lax.broadcasted_iota(jnp.int32, (blk, blk), 1)

        @pl.when(kb == 0)
        def _():
            dq_acc[...] = jnp.zeros_like(dq_acc)

            def dstep(i, c):
                r0 = pl.multiple_of(i * blk, blk)
                prod = do_r[pl.ds(r0, blk), :].astype(F32) * o_r[pl.ds(r0, blk), :].astype(F32)
                de = jnp.sum(jnp.where(even, prod, 0.0), axis=1, keepdims=True)
                dd = jnp.sum(jnp.where(even, 0.0, prod), axis=1, keepdims=True)
                delta_sc[pl.ds(r0, blk), :] = jnp.where(even, de, dd)
                return c

            lax.fori_loop(0, nq, dstep, 0)

        vblk = v_r[...]
        ks = [k_r[:, h * HP:(h + 1) * HP] for h in range(2)]

        def q_block(r0, height, carry, visible):
            dk0, dk1, dv = carry
            dob = do_r[pl.ds(r0, height), :]
            lse_b = lse_r[pl.ds(r0, height), :]
            dl_b = delta_sc[pl.ds(r0, height), :]
            qlane = lax.broadcasted_iota(jnp.int32, (height, LANE), 1)
            dks = [dk0, dk1]
            qhs = [q_r[pl.ds(r0, height), h * HP:(h + 1) * HP] for h in range(2)]
            dohs = [jnp.where((qlane < V_HEAD) if h == 0 else (qlane >= V_HEAD), dob, jnp.zeros_like(dob))
                    for h in range(2)]
            ss = [lax.dot_general(qhs[h], ks[h], NT_DIMS, preferred_element_type=F32) for h in range(2)]
            dps = [lax.dot_general(dohs[h], vblk, NT_DIMS, preferred_element_type=F32) for h in range(2)]
            for h in range(2):
                lo = 0 if h == 0 else V_HEAD
                p = jnp.exp2(ss[h] - lse_b[:, lo:lo + 1])
                if visible is not None:
                    p = jnp.where(visible, p, 0.0)
                ds = (p * (dps[h] - dl_b[:, lo:lo + 1])).astype(MXU_DT)
                dv = dv + lax.dot_general(p.astype(MXU_DT), dohs[h], TN_DIMS, preferred_element_type=F32)
                dks[h] = dks[h] + lax.dot_general(ds, qhs[h], TN_DIMS, preferred_element_type=F32)
                dq_acc[pl.ds(r0, height), h * HP:(h + 1) * HP] += jnp.dot(ds, ks[h], preferred_element_type=F32)
            return dks[0], dks[1], dv

        z = jnp.zeros((blk, HP), F32)
        below = nq - 1 - kb
        odd = below % 2
        rowi2 = lax.broadcasted_iota(jnp.int32, (2 * blk, blk), 0)
        coli2 = lax.broadcasted_iota(jnp.int32, (2 * blk, blk), 1)
        first = pl.multiple_of(kb * blk, blk)
        carry = lax.cond(odd == 1, lambda c: q_block(first, 2 * blk, c, coli2 <= rowi2),
                         lambda c: q_block(first, blk, c, coli <= rowi), (z, z, jnp.zeros((blk, LANE), F32)))
        dk0, dk1, dv = lax.fori_loop(
            0, below // 2, lambda i, c: q_block(pl.multiple_of((kb + 1 + odd + 2 * i) * blk, blk), 2 * blk, c, None),
            carry)
        dk_o[:, 0:HP] = (dk0 * (scale / Q_PRESCALE)).astype(MXU_DT)
        dk_o[:, HP:2 * HP] = (dk1 * (scale / Q_PRESCALE)).astype(MXU_DT)
        dv_o[...] = dv.astype(MXU_DT)

        @pl.when(kb == nq - 1)
        def _():
            dq_o[...] = (dq_acc[...] * scale).astype(MXU_DT)

    seq_pair = pl.BlockSpec((tp, LANE), lambda b, p, kk: (b, p))
    return _call(
        body, (q, k, v, o, do, lse), carry, grid=(nb, npair, nq),
        in_specs=[pl.BlockSpec((tp, 2 * HP), lambda b, p, kk: (b, p)),
                  pl.BlockSpec((blk, 2 * HP), lambda b, p, kk: (b * nq + kk, p)),
                  pl.BlockSpec((blk, LANE), lambda b, p, kk: (b * nq + kk, p)),
                  seq_pair, seq_pair, pl.BlockSpec((None, tp, LANE), lambda b, p, kk: (p, b, 0))],
        out_specs=[pl.BlockSpec((tp, 2 * HP), lambda b, p, kk: (b, p)),
                   pl.BlockSpec((blk, 2 * HP), lambda b, p, kk: (b * nq + kk, p)),
                   pl.BlockSpec((blk, LANE), lambda b, p, kk: (b * nq + kk, p))],
        out_shape=[S((rows, MLA_HEADS * HP), MXU_DT), S((rows, MLA_HEADS * HP), MXU_DT),
                   S((rows, MLA_HEADS * V_HEAD), MXU_DT)],
        scratch_shapes=[pltpu.VMEM((tp, 2 * HP), F32), pltpu.VMEM((tp, LANE), F32)],
        sem=("parallel", "parallel", "arbitrary"), name=name)


def rope_bwd(dq, dk, dv, tabq, tabk, tp, tm, name):
    rows = dq.shape[0]
    nt = tp // tm
    wq = MLA_HEADS * HP

    def body(dq_r, dk_r, dv_r, tq_r, tk_r, dqa_o, dkv_o, dkr_o):
        dqv = dq_r[...].astype(F32)
        lane = _lane_mod(dqv.shape)
        in_rope = (lane >= QK_NOPE) & (lane < QK_HEAD)
        rope = jnp.where(in_rope, dqv, 0.0)
        da = jnp.where(lane < QK_HEAD, dqv, 0.0) + pltpu.roll(rope, QK_ROPE, axis=1)
        dqa_o[...] = (da * jnp.tile(tq_r[...], (1, MLA_HEADS))).astype(MXU_DT)
        dkf = dk_r[...].astype(F32)
        dkv_o[:, 0:wq] = jnp.where(lane < QK_NOPE, dkf, 0.0).astype(MXU_DT)
        dkv_o[:, wq:] = dv_r[...]
        kr = jnp.where(in_rope, dkf, 0.0)
        tot = kr[:, 0:HP]
        for h in range(1, MLA_HEADS):
            tot = tot + kr[:, h * HP:(h + 1) * HP]
        dkr_o[...] = ((tot + pltpu.roll(tot, QK_ROPE, axis=1)) * tk_r[...]).astype(MXU_DT)

    def rowblk(wd):
        return pl.BlockSpec((tm, wd), lambda i: (i, 0))

    tab = pl.BlockSpec((tm, HP), lambda i: (i % nt, 0))
    return pl.pallas_call(
        body, grid=(rows // tm,), in_specs=[rowblk(wq), rowblk(wq), rowblk(MLA_HEADS * V_HEAD), tab, tab],
        out_specs=[rowblk(wq), rowblk(wq + MLA_HEADS * V_HEAD), rowblk(HP)],
        out_shape=[S((rows, wq), MXU_DT), S((rows, wq + MLA_HEADS * V_HEAD), MXU_DT), S((rows, HP), MXU_DT)],
        compiler_params=_cp(("parallel",)), name=name)(dq, dk, dv, tabq, tabk)


def loss_head(h, target, gain, tp, t_real, tm, name):
    rows = h.shape[0]
    nt = tp // tm

    def body(h_r, t_r, g_r, dh_o, loss_o, dg_o):
        i = pl.program_id(0)

        @pl.when(i == 0)
        def _():
            loss_o[...] = jnp.zeros_like(loss_o)
            dg_o[...] = jnp.zeros_like(dg_o)

        xv = h_r[...]
        rstd = lax.rsqrt(jnp.mean(xv * xv, axis=-1, keepdims=True) + EPS)
        xhat = xv * rstd
        g = g_r[...]
        pos = (i % nt) * tm + lax.broadcasted_iota(jnp.int32, (tm, 1), 0)
        valid = (pos >= N_META) & (pos < t_real)
        err = jnp.where(valid, xhat * g - t_r[...], 0.0)
        loss_o[...] += 0.5 * jnp.sum(jnp.mean(err * err, axis=-1, keepdims=True))
        dy = err * (1.0 / D_MODEL)
        dg_o[...] += jnp.sum(dy * xhat, axis=0, keepdims=True)
        dxh = dy * g
        dh_o[...] = rstd * (dxh - xhat * jnp.mean(dxh * xhat, axis=-1, keepdims=True))

    blk = pl.BlockSpec((tm, D_MODEL), lambda i: (i, 0))
    return pl.pallas_call(
        body, grid=(rows // tm,), in_specs=[blk, blk, pl.BlockSpec((1, D_MODEL), lambda i: (0, 0))],
        out_specs=[blk, pl.BlockSpec((1, LANE), lambda i: (0, 0)), pl.BlockSpec((1, D_MODEL), lambda i: (0, 0))],
        out_shape=[S((rows, D_MODEL), F32), S((1, LANE), F32), S((1, D_MODEL), F32)],
        compiler_params=_cp(("arbitrary",)), name=name)(h, target, gain)


ADAM_TILE_ELEMS = 512 * 1024


def adamw(g, w, m, v, name):
    shape = w.shape
    cols = shape[-1]
    rws = max(1, math.prod(shape[:-1]))
    tr = rws if rws * cols <= ADAM_TILE_ELEMS else _div_tile(rws, max(SUBLANE, ADAM_TILE_ELEMS // cols), SUBLANE)
    bc1 = 1.0 - ADAM_B1 ** ADAM_STEP
    bc2 = 1.0 - ADAM_B2 ** ADAM_STEP

    def body(g_r, w_r, m_r, v_r, do, mo, vo):
        gv = g_r[...]
        mn = ADAM_B1 * m_r[...] + (1.0 - ADAM_B1) * gv
        vn = ADAM_B2 * v_r[...] + (1.0 - ADAM_B2) * (gv * gv)
        m_hat = mn / bc1
        v_hat = vn / bc2
        do[...] = -ADAM_LR * (m_hat / (jnp.sqrt(v_hat) + ADAM_EPS) + ADAM_WD * w_r[...])
        mo[...] = mn
        vo[...] = vn

    blk = pl.BlockSpec((tr, cols), lambda i: (i, 0))
    outs = pl.pallas_call(
        body, grid=(rws // tr,), in_specs=[blk] * 4, out_specs=[blk] * 3, out_shape=[S((rws, cols), F32)] * 3,
        compiler_params=_cp(("parallel",)), name=name)(*[a.reshape(rws, cols) for a in (g, w, m, v)])
    return tuple(o.reshape(shape) for o in outs)


SUM_TILE_ELEMS = 512 * 1024


def _place():
    return lax.axis_index("x"), lax.axis_index("y"), lax.axis_index("c")


def _remote(src, dst, send_sems, recv_sems, k, to):
    return pltpu.make_async_remote_copy(src_ref=src, dst_ref=dst, send_sem=send_sems.at[k], recv_sem=recv_sems.at[k],
                                        device_id=to, device_id_type=MESH)


def chip_index():
    return 2 * lax.axis_index("x") + lax.axis_index("y")


def _sem_pair(n):
    return [pltpu.SemaphoreType.DMA((n,)), pltpu.SemaphoreType.DMA((n,))]


def stage_gather_chips(xs):
    def copies(ins, outs, sems):
        send_sems, recv_sems = sems
        mx, my, mc = _place()
        sibling = (mx, my, 1 - mc)
        chips = [(1 - mx, my), (mx, 1 - my), (1 - mx, 1 - my)]
        first, landed, passed, from_sibling = [], [], [], []
        for i, (x_ref, out_ref) in enumerate(zip(ins, outs)):
            def piece(cx, cy, h, out_ref=out_ref):
                return out_ref.at[2 * cx + cy, h]

            for j, (cx, cy) in enumerate(chips):
                k = 6 * i + j
                first.append(_remote(x_ref.at[mc], piece(mx, my, mc), send_sems, recv_sems, k, (cx, cy, mc)))
                landed.append(_remote(x_ref.at[mc], piece(cx, cy, mc), send_sems, recv_sems, k, (cx, cy, mc)))
                passed.append(_remote(piece(cx, cy, mc), piece(cx, cy, mc), send_sems, recv_sems, k + 3, sibling))
                from_sibling.append(_remote(x_ref.at[mc], piece(cx, cy, 1 - mc), send_sems, recv_sems, k + 3, sibling))
        return first, landed, passed, from_sibling

    def start(ins, outs, sems):
        for cp in copies(ins, outs, sems)[0]:
            cp.start()

    def finish(ins, outs, sems):
        first, landed, passed, from_sibling = copies(ins, outs, sems)
        for arrived, onward in zip(landed, passed):
            arrived.wait_recv()
            onward.start()
        for cp in from_sibling:
            cp.wait_recv()
        for cp in first + passed:
            cp.wait_send()

    return Stage(list(xs), [S((4,) + x.shape, x.dtype) for x in xs], _sem_pair(6 * len(xs)), start, finish)


def own_block(gathered, xs):
    return lax.dynamic_update_slice(gathered, xs[None], (chip_index(), 0, 0, 0))


def stage_pair_exchange(gs):
    def copies(ins, outs, sems):
        send_sems, recv_sems = sems
        mx, my, mc = _place()
        return [_remote(g_ref.at[s, 1 - mc], land_ref.at[s], send_sems, recv_sems, 4 * i + s, (mx, my, 1 - mc))
                for i, (g_ref, land_ref) in enumerate(zip(ins, outs)) for s in range(4)]

    def start(ins, outs, sems):
        for cp in copies(ins, outs, sems):
            cp.start()

    def finish(ins, outs, sems):
        cps = copies(ins, outs, sems)
        for cp in cps:
            cp.wait_recv()
        for cp in cps:
            cp.wait_send()

    return Stage(list(gs), [S((4,) + g.shape[2:], g.dtype) for g in gs], _sem_pair(4 * len(gs)), start, finish)


def _sum_rows(rws, width):
    return _div_tile(rws, max(SUBLANE, SUM_TILE_ELEMS // width), SUBLANE)


def pair_sum(g4, land, c_idx, name):
    _, _, rws, wd = g4.shape
    th = _sum_rows(rws, wd)

    def body(c_ref, a_ref, b_ref, o_ref):
        o_ref[...] = a_ref[...] + b_ref[...]

    return pl.pallas_call(
        body,
        grid_spec=pltpu.PrefetchScalarGridSpec(
            num_scalar_prefetch=1, grid=(4, rws // th),
            in_specs=[pl.BlockSpec((None, None, th, wd), lambda s, i, c: (s, c[0], i, 0)),
                      pl.BlockSpec((None, th, wd), lambda s, i, c: (s, i, 0))],
            out_specs=pl.BlockSpec((None, th, wd), lambda s, i, c: (s, i, 0))),
        out_shape=S((4, rws, wd), F32), compiler_params=_cp(("parallel", "parallel")), name=name)(c_idx, g4, land)


def stage_chip_scatter(ps):
    def copies(ins, outs, sems):
        send_sems, recv_sems = sems
        mx, my, mc = _place()
        me = 2 * mx + my
        chips = [(1 - mx, my), (mx, 1 - my), (1 - mx, 1 - my)]
        sent, landed = [], []
        for i, (p_ref, land_ref) in enumerate(zip(ins, outs)):
            for j, (cx, cy) in enumerate(chips):
                k = 3 * i + j
                sent.append(_remote(p_ref.at[2 * cx + cy], land_ref.at[me], send_sems, recv_sems, k, (cx, cy, mc)))
                landed.append(_remote(p_ref.at[me], land_ref.at[2 * cx + cy], send_sems, recv_sems, k, (cx, cy, mc)))
        return sent, landed

    def start(ins, outs, sems):
        for cp in copies(ins, outs, sems)[0]:
            cp.start()

    def finish(ins, outs, sems):
        sent, landed = copies(ins, outs, sems)
        for cp in landed:
            cp.wait_recv()
        for cp in sent:
            cp.wait_send()

    return Stage(list(ps), [S(p.shape, p.dtype) for p in ps], _sem_pair(3 * len(ps)), start, finish)


def chip_sum(l4, p4, me_idx, name):
    _, rws, wd = l4.shape
    th = _sum_rows(rws, wd)

    def body(me_ref, a, b, c, d, own, o_ref):
        me = me_ref[0]
        parts = [jnp.where(me == s, own[...], r[...]) for s, r in enumerate((a, b, c, d))]
        o_ref[...] = ((parts[0] + parts[1]) + parts[2]) + parts[3]

    def blk(s):
        return pl.BlockSpec((None, th, wd), lambda i, me: (jnp.where(me[0] == s, (s + 1) % 4, s), i, 0))

    return pl.pallas_call(
        body,
        grid_spec=pltpu.PrefetchScalarGridSpec(
            num_scalar_prefetch=1, grid=(rws // th,),
            in_specs=[blk(0), blk(1), blk(2), blk(3), pl.BlockSpec((None, th, wd), lambda i, me: (me[0], i, 0))],
            out_specs=pl.BlockSpec((th, wd), lambda i, me: (i, 0))),
        out_shape=S((rws, wd), F32), compiler_params=_cp(("parallel",)), name=name)(me_idx, l4, l4, l4, l4, p4)


def stage_pair_gather(rs):
    def copies(ins, outs, sems):
        send_sems, recv_sems = sems
        mx, my, mc = _place()
        return [_remote(r_ref, out_ref, send_sems, recv_sems, i, (mx, my, 1 - mc))
                for i, (r_ref, out_ref) in enumerate(zip(ins, outs))]

    def start(ins, outs, sems):
        for cp in copies(ins, outs, sems):
            cp.start()

    def finish(ins, outs, sems):
        for cp in copies(ins, outs, sems):
            cp.wait()

    return Stage(list(rs), [S(r.shape, r.dtype) for r in rs], _sem_pair(len(rs)), start, finish)


PACK_ELEMS = 16 * LANE


def pack_rows(arrays, lead, total_mult):
    parts, offs, r0 = [], [], 0
    for a in arrays:
        flat = a.reshape(a.shape[:lead] + (-1,))
        elems = _round_up(flat.shape[-1], PACK_ELEMS)
        flat = jnp.pad(flat, [(0, 0)] * lead + [(0, elems - flat.shape[-1])])
        parts.append(flat.reshape(flat.shape[:lead] + (elems // LANE, LANE)))
        offs.append((r0, elems // LANE))
        r0 += elems // LANE
    total = _round_up(r0, total_mult)
    if total > r0:
        parts.append(jnp.zeros(parts[0].shape[:lead] + (total - r0, LANE), parts[0].dtype))
    return jnp.concatenate(parts, axis=lead), offs


def unpack_rows(buf, off, shape):
    r0, nr = off
    lead = buf.shape[:-2]
    n = math.prod(shape)
    return buf[..., r0:r0 + nr, :].reshape(lead + (nr * LANE,))[..., :n].reshape(lead + tuple(shape))


def unshard(stacked, axis):
    x = jnp.moveaxis(stacked, 0, axis)
    return x.reshape(x.shape[:axis] + (4 * x.shape[axis + 1],) + x.shape[axis + 2:])


def to_shards(full, axis):
    n = full.shape[axis] // 4
    x = full.reshape(full.shape[:axis] + (4, n) + full.shape[axis + 1:])
    return jnp.moveaxis(x, axis, 0)


def _rot_cols(w):
    half = w.shape[-1] // 2
    return jnp.concatenate([-w[..., half:], w[..., :half]], axis=-1)


def _unrot_cols(dw):
    half = dw.shape[-1] // 2
    return jnp.concatenate([dw[..., half:], -dw[..., :half]], axis=-1)


def odd_w_in_padded(w_in):
    kr = w_in[:, Q_LORA + KV_LORA:]
    rows = w_in.shape[0]
    return jnp.concatenate([w_in[:, :Q_LORA], jnp.zeros((rows, 128), w_in.dtype), w_in[:, Q_LORA:Q_LORA + KV_LORA],
                            jnp.zeros((rows, 64), w_in.dtype), kr, _rot_cols(kr)], axis=1)


def odd_w_in_unpad(dwp):
    base = 512 + KV_LORA + 64
    dkr = dwp[:, base:base + QK_ROPE] + _unrot_cols(dwp[:, base + QK_ROPE:base + 2 * QK_ROPE])
    return jnp.concatenate([dwp[:, :Q_LORA], dwp[:, 512:512 + KV_LORA], dkr], axis=1)


def uq_padded(w_uq):
    w = w_uq.reshape(Q_LORA, MLA_HEADS, QK_HEAD)
    return jnp.concatenate([w, _rot_cols(w[:, :, QK_NOPE:])], axis=-1).reshape(Q_LORA, MLA_HEADS * HP)


def uq_unpad(dwp):
    d = dwp.reshape(Q_LORA, MLA_HEADS, HP)
    rope = d[:, :, QK_NOPE:QK_HEAD] + _unrot_cols(d[:, :, QK_HEAD:])
    return jnp.concatenate([d[:, :, :QK_NOPE], rope], axis=-1).reshape(Q_LORA, MLA_HEADS * QK_HEAD)


def ukv_padded(w_ukv):
    w = w_ukv.reshape(KV_LORA, MLA_HEADS, QK_NOPE + V_HEAD)
    wk = jnp.concatenate([w[:, :, :QK_NOPE], jnp.zeros((KV_LORA, MLA_HEADS, HP - QK_NOPE), w.dtype)], axis=-1)
    return jnp.concatenate([wk.reshape(KV_LORA, MLA_HEADS * HP), w[:, :, QK_NOPE:].reshape(KV_LORA, MLA_HEADS * V_HEAD)],
                           axis=1)


def ukv_unpad(dwp):
    dk = dwp[:, :MLA_HEADS * HP].reshape(KV_LORA, MLA_HEADS, HP)[:, :, :QK_NOPE]
    dv = dwp[:, MLA_HEADS * HP:].reshape(KV_LORA, MLA_HEADS, V_HEAD)
    return jnp.concatenate([dk, dv], axis=-1).reshape(KV_LORA, MLA_HEADS * (QK_NOPE + V_HEAD))


def block_diag(w):
    h, d, _ = w.shape
    eye = jnp.eye(h, dtype=w.dtype)
    return (eye[:, None, :, None] * w[:, :, None, :]).reshape(h * d, h * d)


def block_diag_part(dense, h):
    d = dense.shape[0] // h
    x = dense.reshape(h, d, h, d)
    return jnp.stack([x[i, :, i, :] for i in range(h)], axis=0)


def rope_tables(tp):
    pos = jnp.arange(tp, dtype=F32)
    inv_freq = ROPE_BASE ** (-jnp.arange(0, QK_ROPE, 2, dtype=F32) / QK_ROPE)
    ang = pos[:, None] * inv_freq[None, :]
    cos2 = jnp.tile(jnp.cos(ang), (1, 2))
    sin2 = jnp.tile(jnp.sin(ang), (1, 2))
    tabq = jnp.concatenate([jnp.ones((tp, QK_NOPE), F32), cos2, sin2], axis=1)
    tabk = jnp.concatenate([jnp.zeros((tp, QK_NOPE), F32), cos2, sin2], axis=1)
    return tabq, tabk


class Dims:
    def __init__(self, nb, seq):
        self.nb = nb
        self.t_real = seq + N_META
        self.tp = _round_up(self.t_real, ATT_BLK)
        self.n = self.tp // 4
        assert self.n % 16 == 0
        self.rows = nb * self.tp


class NoComm:
    def advance(self, carried):
        return None


def even_fwd(h, p, dm, comm):
    (u, hn), _ = norm_matmul(h, 0, D_MODEL, p["norm"], p["w_in"], dm.n, EVEN_IN // 2, F32, "ev_in")
    (y, ca, xc, a, hs), got = even_mid_fwd(u, p["conv_a"], p["conv_b"], p["conv_b_bias"], p["rw"], p["r_b"], p["iw"],
                                           p["i_b"], p["lam"], dm.nb, dm.tp, dm.n, "ev_mid", carry=comm.advance(None))
    comm.advance(got)
    out = matmul_res(y, p["w_out"].reshape(2, CONV_W, D_MODEL), h, dm.n, D_MODEL, "ev_out")
    return out, (h, u, hn, ca, xc, a, hs, y)


def even_bwd(dout, saved, p, dm, comm):
    h, u, hn, ca, xc, a, hs, y = saved
    g = {}
    dycat = matmul_nt(dout, p["w_out"], dm.n, D_MODEL, F32, "ev_dycat")
    g["w_out"], got = matmul_tn(y, dout, dm.n, "ev_dw_out", carry=comm.advance(None))
    outs, got = even_mid_bwd(u, dycat, ca, xc, a, hs, p["conv_a"], p["conv_b"], p["rw"], p["r_b"], p["iw"], p["i_b"],
                             p["lam"], dm.nb, dm.tp, dm.n, "ev_mid_bwd", carry=comm.advance(got))
    du, g["conv_a"], g["conv_b"], g["conv_b_bias"], drw, g["r_b"], diw, g["i_b"], g["lam"] = outs
    g["r_w"] = block_diag_part(drw, LRU_HEADS)
    g["i_w"] = block_diag_part(diw, LRU_HEADS)
    g["w_in"], got = matmul_tn(hn, du, dm.n, "ev_dw_in", carry=comm.advance(got))
    comm.advance(got)
    dx, g["norm"] = matmul_nt_normbwd(du, p["w_in"], h, 0, p["norm"], dout, dm.n, 512, F32, "ev_dx")
    return dx, g


def odd_fwd(h, p, tabq, tabk, dm, comm):
    nt = dm.tp // dm.n
    (u, hn), _ = norm_matmul(h, 0, D_MODEL, p["norm"], p["w_in_p"], dm.n, ODD_PAD, F32, "od_in")
    tab_spec = pl.BlockSpec((dm.n, HP), lambda i, j: (i % nt, 0))
    (q, cqn), _ = norm_matmul(u, 0, Q_LORA, p["q_norm"], p["w_uq_p"], dm.n, 1024, MXU_DT, "od_q",
                              epi=_q_rope_epi, epi_ops=(tabq,), epi_specs=(tab_spec,))
    kr_spec = pl.BlockSpec((dm.n, HP), lambda i, j: (i, ODD_KR_COL))
    (k, ckvn), _ = norm_matmul(u, ODD_CKV_COL, KV_LORA, p["kv_norm"], p["w_uk_p"], dm.n, 1024, MXU_DT, "od_k",
                               epi=_k_rope_epi, epi_ops=(u, tabk), epi_specs=(kr_spec, tab_spec))
    (v, _), _ = norm_matmul(u, ODD_CKV_COL, KV_LORA, p["kv_norm"], p["w_uv_p"], dm.n, MLA_HEADS * V_HEAD, MXU_DT,
                            "od_v")
    (o, lse), got = attn_fwd(q, k, v, dm.nb, dm.tp, "od_attn", carry=comm.advance(None))
    comm.advance(got)
    out = matmul_res(o[None], p["w_out"][None], h, dm.n, D_MODEL, "od_out")
    return out, (h, u, hn, cqn, ckvn, q, k, v, o, lse)


def odd_bwd(dout, saved, p, tabq, tabk, dm, comm):
    h, u, hn, cqn, ckvn, q, k, v, o, lse = saved
    g = {}
    do = matmul_nt(dout, p["w_out"], dm.n, D_MODEL, MXU_DT, "od_do")
    g["w_out"], got = matmul_tn(o, dout, dm.n, "od_dw_out", carry=comm.advance(None))
    (dq, dk, dv), got = attn_bwd(q, k, v, o, do, lse, dm.nb, dm.tp, "od_attn_bwd", carry=comm.advance(got))
    dqa, dkv, dkr = rope_bwd(dq, dk, dv, tabq, tabk, dm.tp, dm.n, "od_rope_bwd")
    g["w_uq_p"], got = matmul_tn(cqn, dqa, dm.n, "od_dw_uq", carry=comm.advance(got))
    comm.advance(got)
    g["w_ukv_p"], _ = matmul_tn(ckvn, dkv, dm.n, "od_dw_ukv")
    dcq, g["q_norm"] = matmul_nt_normbwd(dqa, p["w_uq_p"], u, 0, p["q_norm"], None, dm.n, mat_cols(dqa), MXU_DT,
                                         "od_dcq")
    dckv, g["kv_norm"] = matmul_nt_normbwd(dkv, p["w_ukv_p"], u, ODD_CKV_COL, p["kv_norm"], None, dm.n, mat_cols(dkv),
                                           MXU_DT, "od_dckv")
    du = jnp.concatenate([dcq, jnp.zeros((dm.rows, 128), MXU_DT), dckv, dkr], axis=1)
    g["w_in_p"], _ = matmul_tn(hn, du, dm.n, "od_dw_in")
    dx, g["norm"] = matmul_nt_normbwd(du, p["w_in_p"], h, 0, p["norm"], dout, dm.n, ODD_PAD, F32, "od_dx")
    return dx, g


def ffn_fwd(h, p, dm, comm):
    (up, hn), got = norm_matmul(h, 0, D_MODEL, p["norm"], p["w_up"], dm.n, D_FF // 2, MXU_DT, "ffn_up",
                                carry=comm.advance(None))
    comm.advance(got)
    u, y = ffn_mid_fwd(up, p["cw"], p["cb"], dm.nb, dm.tp, dm.n, "ffn_mid")
    out = matmul_res(y[None], p["w_down"][None], h, dm.n, D_MODEL, "ffn_down")
    return out, (h, up, hn, u, y)


def ffn_bwd(dout, saved, p, dm, comm):
    h, up, hn, u, y = saved
    g = {}
    dy = matmul_nt(dout, p["w_down"], dm.n, D_FF, MXU_DT, "ffn_dy")
    g["w_down"], got = matmul_tn(y, dout, dm.n, "ffn_dw_down", carry=comm.advance(None))
    (dup, g["cw"], g["cb"]), got = ffn_mid_bwd(dy, u, up, p["cw"], dm.nb, dm.tp, dm.n, "ffn_mid_bwd",
                                               carry=comm.advance(got))
    g["w_up"], got = matmul_tn(hn, dup, dm.n, "ffn_dw_up", carry=comm.advance(got), col_shards=4)
    comm.advance(got)
    dx, g["norm"] = matmul_nt_normbwd(dup, p["w_up"], h, 0, p["norm"], dout, dm.n, D_FF // 2, F32, "ffn_dx")
    return dx, g


def _row(v):
    return v.reshape(1, -1)


def even_params(wf, j):
    return dict(norm=_row(wf["ev_norm"][j]), w_in=wf["ev_w_in"], conv_a=wf["ev_conv_a"][j], conv_b=wf["ev_conv_b"][j],
                conv_b_bias=_row(wf["ev_conv_b_bias"][j]), rw=block_diag(wf["ev_gate_r_w"][j]).astype(MXU_DT),
                r_b=_row(wf["ev_gate_r_b"][j]), iw=block_diag(wf["ev_gate_i_w"][j]).astype(MXU_DT),
                i_b=_row(wf["ev_gate_i_b"][j]), lam=_row(wf["ev_lru_lambda"][j]), w_out=wf["ev_w_out"])


def odd_params(wf, j):
    wkv = ukv_padded(wf["od_w_ukv"])
    return dict(norm=_row(wf["od_norm"][j]), w_in_p=odd_w_in_padded(wf["od_w_in"]), q_norm=_row(wf["od_q_norm"][j]),
                kv_norm=_row(wf["od_kv_norm"][j]), w_uq_p=uq_padded(wf["od_w_uq"]), w_ukv_p=wkv,
                w_uk_p=wkv[:, :MLA_HEADS * HP], w_uv_p=wkv[:, MLA_HEADS * HP:], w_out=wf["od_w_out"])


def ffn_params(wf, layer):
    return dict(norm=_row(wf["ffn_norm"][layer]), w_up=wf["ffn_w_up"],
                cw=jnp.moveaxis(wf["ffn_conv_w"][layer].reshape(3, 2, D_FF), 1, 0),
                cb=wf["ffn_conv_b"][layer].reshape(2, 1, D_FF), w_down=wf["ffn_w_down"])


def even_grads(g):
    out = {"ev_" + k_: g[k_] for k_ in ("w_in", "conv_a", "conv_b", "w_out")}
    out.update({"ev_norm": g["norm"][0], "ev_conv_b_bias": g["conv_b_bias"][0], "ev_gate_r_w": g["r_w"],
                "ev_gate_r_b": g["r_b"][0], "ev_gate_i_w": g["i_w"], "ev_gate_i_b": g["i_b"][0],
                "ev_lru_lambda": g["lam"][0]})
    return out


def odd_grads(g):
    return {"od_norm": g["norm"][0], "od_q_norm": g["q_norm"][0], "od_kv_norm": g["kv_norm"][0],
            "od_w_in": odd_w_in_unpad(g["w_in_p"]), "od_w_uq": uq_unpad(g["w_uq_p"]),
            "od_w_ukv": ukv_unpad(g["w_ukv_p"]), "od_w_out": g["w_out"]}


def ffn_grads(g):
    return {"ffn_norm": g["norm"][0], "ffn_w_up": g["w_up"], "ffn_conv_w": jnp.moveaxis(g["cw"], 0, 1).reshape(3, 2 * D_FF),
            "ffn_conv_b": g["cb"].reshape(2 * D_FF), "ffn_w_down": g["w_down"]}


WEIGHTS = ["meta_tokens", "ev_norm", "ev_w_in", "ev_conv_a", "ev_conv_b", "ev_conv_b_bias", "ev_gate_r_w", "ev_gate_r_b",
           "ev_gate_i_w", "ev_gate_i_b", "ev_lru_lambda", "ev_w_out", "od_norm", "od_w_in", "od_q_norm", "od_kv_norm",
           "od_w_uq", "od_w_ukv", "od_w_out", "ffn_norm", "ffn_w_up", "ffn_conv_w", "ffn_conv_b", "ffn_w_down",
           "final_norm"]
SHARD_AXIS = {"meta_tokens": 1, "ev_w_in": 2, "ev_conv_a": 2, "ev_conv_b": 2, "ev_w_out": 1, "od_norm": 1, "od_w_in": 1,
              "od_q_norm": 1, "od_kv_norm": 1, "od_w_uq": 2, "od_w_ukv": 2, "od_w_out": 1, "ffn_w_up": 2,
              "ffn_conv_w": 2, "ffn_w_down": 1}
MATMUL_WEIGHTS = ["ev_w_in", "ev_w_out", "od_w_in", "od_w_uq", "od_w_ukv", "od_w_out", "ffn_w_up", "ffn_w_down"]


LAYER_ORDER = [("ev", 0), ("ffn", 0), ("od", 0), ("ffn", 1), ("ev", 1), ("ffn", 2), ("od", 1), ("ffn", 3)]
LAYER_MATMUL = {"ev": ["ev_w_in", "ev_w_out"], "od": ["od_w_in", "od_w_uq", "od_w_ukv", "od_w_out"],
                "ffn": ["ffn_w_up", "ffn_w_down"]}
LAYER_SHARDED = {"ev": ["ev_w_in", "ev_conv_a", "ev_conv_b", "ev_w_out"],
                 "od": ["od_norm", "od_w_in", "od_q_norm", "od_kv_norm", "od_w_uq", "od_w_ukv", "od_w_out"],
                 "ffn": ["ffn_w_up", "ffn_conv_w", "ffn_w_down"]}
STACKED_SHARDS = "ffn_w_up"


def gather_at_entry(w, names, first, name):
    buf, offs = pack_rows([w[n] for n in names], 0, 32)
    halves = buf.reshape(2, buf.shape[0] // 2, LANE)
    outs = run_stage(stage_gather_chips([halves] + first.halves), name)
    first.step, first.got = 2, outs[1:]
    got = own_block(outs[0], halves).reshape(4, buf.shape[0], LANE)
    return {n: unshard(unpack_rows(got, off, w[n].shape), SHARD_AXIS[n]) for n, off in zip(names, offs)}


def _halves(a):
    return a.reshape(2, a.shape[0] // 2, a.shape[1])


class GatherComm:
    def __init__(self, w, kind, idx):
        self.names = LAYER_MATMUL[kind]
        self.halves = [_halves(w[n][idx].astype(MXU_DT)) for n in self.names]
        self.stage = stage_gather_chips(self.halves)
        self.step, self.got = 0, None

    def advance(self, carried):
        self.step += 1
        if self.step == 1:
            return self.stage
        if self.step == 2:
            self.got = carried
        return None

    def weights(self):
        out = {}
        for n, got, own in zip(self.names, self.got, self.halves):
            stacked = own_block(got, own).reshape(4, 2 * own.shape[1], own.shape[2])
            out[n] = stacked if n == STACKED_SHARDS else unshard(stacked, SHARD_AXIS[n] - 1)
        return out


class ReduceComm:
    def __init__(self, grads, axes, c_idx, tag, tail=None):
        shards = {n: grads[n] if n == STACKED_SHARDS else to_shards(grads[n], axes[n]) for n in grads}
        self.big = [n for n in grads if n in MATMUL_WEIGHTS]
        self.small = [n for n in grads if n not in MATMUL_WEIGHTS]
        self.shapes = {n: shards[n].shape[1:] for n in grads}
        arrays = [shards[n].reshape(4, 2, shards[n].shape[1] // 2, shards[n].shape[2]) for n in self.big]
        gs, self.offs = pack_rows([shards[n] for n in self.small], 1, 16)
        self.rs = gs.shape[1] // 2
        parts = [gs.reshape(4, 2, self.rs, LANE)]
        self.rr = 0
        if tail is not None:
            self.rr = tail.shape[0] // 8
            parts.append(tail.reshape(4, 2, self.rr, LANE))
        arrays.append(jnp.concatenate(parts, axis=2) if len(parts) > 1 else parts[0])
        self.arrays, self.c_idx, self.tag, self.step = arrays, c_idx, tag, 0
        self.part = self.mine = self.theirs = None

    def advance(self, carried):
        self.step += 1
        if self.step == 1:
            return stage_pair_exchange(self.arrays)
        if self.step == 2:
            self.part = [pair_sum(g, land, self.c_idx, "grad_pair_sum_%s_%d" % (self.tag, i))
                         for i, (g, land) in enumerate(zip(self.arrays, carried))]
            return stage_chip_scatter(self.part)
        if self.step == 3:
            me_idx = chip_index().astype(jnp.int32).reshape(1)
            self.mine = [chip_sum(land, part, me_idx, "grad_chip_sum_%s_%d" % (self.tag, i))
                         for i, (land, part) in enumerate(zip(carried, self.part))]
            return stage_pair_gather(self.mine)
        if self.step == 4:
            self.theirs = carried
        return None

    def run_alone(self, name):
        stage = self.advance(None)
        while stage is not None:
            stage = self.advance(run_stage(stage, name + "_%d" % self.step))

    def results(self):
        south = self.c_idx[0] == 0
        boths = [jnp.stack([jnp.where(south, m, t), jnp.where(south, t, m)], axis=0)
                 for m, t in zip(self.mine, self.theirs)]
        out = {n: b.reshape(self.shapes[n]) for n, b in zip(self.big, boths)}
        packed = boths[-1]
        flat = packed[:, :self.rs].reshape(2 * self.rs, LANE)
        out.update({n: unpack_rows(flat, off, self.shapes[n]) for n, off in zip(self.small, self.offs)})
        return out, packed[:, self.rs:self.rs + self.rr]


def kernel(x, meta_tokens, ev_norm, ev_w_in, ev_conv_a, ev_conv_b, ev_conv_b_bias, ev_gate_r_w, ev_gate_r_b, ev_gate_i_w, ev_gate_i_b, ev_lru_lambda, ev_w_out, od_norm, od_w_in, od_q_norm, od_kv_norm, od_w_uq, od_w_ukv, od_w_out, ffn_norm, ffn_w_up, ffn_conv_w, ffn_conv_b, ffn_w_down, final_norm, loss_target, m_meta_tokens, m_ev_norm, m_ev_w_in, m_ev_conv_a, m_ev_conv_b, m_ev_conv_b_bias, m_ev_gate_r_w, m_ev_gate_r_b, m_ev_gate_i_w, m_ev_gate_i_b, m_ev_lru_lambda, m_ev_w_out, m_od_norm, m_od_w_in, m_od_q_norm, m_od_kv_norm, m_od_w_uq, m_od_w_ukv, m_od_w_out, m_ffn_norm, m_ffn_w_up, m_ffn_conv_w, m_ffn_conv_b, m_ffn_w_down, m_final_norm, v_meta_tokens, v_ev_norm, v_ev_w_in, v_ev_conv_a, v_ev_conv_b, v_ev_conv_b_bias, v_ev_gate_r_w, v_ev_gate_r_b, v_ev_gate_i_w, v_ev_gate_i_b, v_ev_lru_lambda, v_ev_w_out, v_od_norm, v_od_w_in, v_od_q_norm, v_od_kv_norm, v_od_w_uq, v_od_w_ukv, v_od_w_out, v_ffn_norm, v_ffn_w_up, v_ffn_conv_w, v_ffn_conv_b, v_ffn_w_down, v_final_norm):
    given = dict(locals())
    w = {n: given[n] for n in WEIGHTS}
    nb, seq, _ = x.shape
    dm = Dims(nb, seq)
    n_layers = len(LAYER_ORDER)

    wf = {n: w[n] for n in WEIGHTS if n not in SHARD_AXIS}
    gathers = [GatherComm(w, kind, idx) for kind, idx in LAYER_ORDER]
    wf.update(gather_at_entry(w, [n for n in SHARD_AXIS if n not in MATMUL_WEIGHTS], gathers[0], "gather_at_entry"))

    tail = dm.tp - dm.t_real
    meta = jnp.broadcast_to(wf["meta_tokens"][None], (nb, N_META, D_MODEL))
    h = jnp.concatenate([meta, x, jnp.zeros((nb, tail, D_MODEL), F32)], axis=1).reshape(dm.rows, D_MODEL)
    tgt = jnp.pad(loss_target, ((0, 0), (N_META, tail), (0, 0))).reshape(dm.rows, D_MODEL)
    tabq, tabk = rope_tables(dm.tp)

    params, saved = [], []
    for i, (kind, idx) in enumerate(LAYER_ORDER):
        wl = dict(wf)
        wl.update(gathers[i].weights())
        comm = gathers[i + 1] if i + 1 < n_layers else NoComm()
        if kind == "ev":
            p = even_params(wl, idx)
            h, sv = even_fwd(h, p, dm, comm)
        elif kind == "od":
            p = odd_params(wl, idx)
            h, sv = odd_fwd(h, p, tabq, tabk, dm, comm)
        else:
            p = ffn_params(wl, idx)
            h, sv = ffn_fwd(h, p, dm, comm)
        params.append(p)
        saved.append(sv)

    dh, loss, dfinal = loss_head(h, tgt, _row(wf["final_norm"]), dm.tp, dm.t_real, dm.n, "loss_head")
    loss = lax.psum(loss[0, 0], ("x", "y", "c"))

    c_idx = lax.axis_index("c").astype(jnp.int32).reshape(1)
    layer_grads = {n: {} for n in WEIGHTS}
    pending, reduces = NoComm(), []
    for i in reversed(range(n_layers)):
        kind, idx = LAYER_ORDER[i]
        if kind == "ev":
            dh, g = even_bwd(dh, saved[i], params[i], dm, pending)
            g = even_grads(g)
        elif kind == "od":
            dh, g = odd_bwd(dh, saved[i], params[i], tabq, tabk, dm, pending)
            g = odd_grads(g)
        else:
            dh, g = ffn_bwd(dh, saved[i], params[i], dm, pending)
            g = ffn_grads(g)
        for n in g:
            if n not in SHARD_AXIS:
                layer_grads[n][idx] = g[n]
        if i > 0:
            pending = ReduceComm({n: g[n] for n in LAYER_SHARDED[kind]}, {n: SHARD_AXIS[n] - 1 for n in SHARD_AXIS},
                                 c_idx, "%s%d" % (kind, idx))
            reduces.append((pending, idx))
    dh3 = dh.reshape(nb, dm.tp, D_MODEL)
    grad_x = dh3[:, N_META:dm.t_real]

    repl = [n for n in WEIGHTS if n not in SHARD_AXIS]
    layer_grads["final_norm"] = {0: dfinal[0]}
    repl_full = {n: (layer_grads[n][0] if n == "final_norm" else
                     jnp.stack([layer_grads[n][j] for j in range(w[n].shape[0])], axis=0)) for n in repl}
    tail_buf, tail_offs = pack_rows([repl_full[n] for n in repl], 0, 64)
    first = {n: g[n] for n in LAYER_SHARDED["ev"]}
    first["meta_tokens"] = jnp.sum(dh3[:, :N_META], axis=0)
    axes = {n: SHARD_AXIS[n] - 1 for n in SHARD_AXIS}
    axes["meta_tokens"] = SHARD_AXIS["meta_tokens"]
    last = ReduceComm(first, axes, c_idx, "first_layer", tail=tail_buf)
    last.run_alone("grad_first_layer")
    reduces.append((last, 0))

    red = {}
    for comm, idx in reduces:
        got, tail_piece = comm.results()
        for n, v_ in got.items():
            if n == "meta_tokens":
                red[n] = v_
            else:
                layer_grads[n][idx] = v_
    tails = own_block(run_stage(stage_gather_chips([tail_piece]), "grad_gather_replicated")[0], tail_piece)
    tails = tails.reshape(tail_buf.shape[0], LANE)
    for n, off in zip(repl, tail_offs):
        red[n] = unpack_rows(tails, off, w[n].shape)
    for n in SHARD_AXIS:
        if n != "meta_tokens":
            red[n] = jnp.stack([layer_grads[n][j] for j in range(w[n].shape[0])], axis=0)

    outs = [adamw(red[n], w[n], given["m_" + n], given["v_" + n], "adamw_" + n) for n in WEIGHTS]
    return (loss, grad_x, *[red[n] for n in WEIGHTS], *[o[0] for o in outs], *[o[1] for o in outs],
            *[o[2] for o in outs])
```

```python
import math

import jax
import jax.numpy as jnp
from jax import lax
from jax.experimental import pallas as pl
from jax.experimental.pallas import tpu as pltpu

F32 = jnp.float32
MXU_DT = jnp.bfloat16
S = jax.ShapeDtypeStruct
MESH = pl.DeviceIdType.MESH

EPS = 1e-6
D_MODEL = 1024
N_META = 16
DEPTH = 4
CONV_W = 512
LRU_W = 512
LRU_HEADS = 8
LRU_C = 8.0
EVEN_IN = 2560
MLA_HEADS = 16
QK_NOPE = 64
QK_ROPE = 32
QK_HEAD = 96
V_HEAD = 64
Q_LORA = 384
KV_LORA = 256
ROPE_BASE = 10000.0
D_FF = 2816
ODD_PAD = 896
ODD_CKV_COL = 2
ODD_KR_COL = 6
HP = 128
ATT_BLK = 384
Q_PRESCALE = QK_HEAD ** -0.5 * math.log2(math.e)
FFN_CT = 256
EV_CT = 256
STRIP_ROWS = 352
LANE = 128
SUBLANE = 8
VMEM_LIMIT_MB = 52

ADAM_LR = 0.001
ADAM_B1 = 0.9
ADAM_B2 = 0.999
ADAM_EPS = 1e-08
ADAM_WD = 0.01
ADAM_STEP = 10

TN_ACC_ELEMS = 1536 * 1024
NT_DIMS = (((1,), (1,)), ((), ()))
TN_DIMS = (((0,), (0,)), ((), ()))


def _cp(sem):
    return pltpu.CompilerParams(dimension_semantics=sem, vmem_limit_bytes=VMEM_LIMIT_MB << 20)


def _div_tile(n, cap, mult):
    if n <= cap:
        return n
    best = None
    for t in range(mult, cap + 1, mult):
        if n % t == 0:
            best = t
    assert best is not None, (n, cap, mult)
    return best


def _round_up(n, m):
    return -(-n // m) * m


def mat_cols(arr):
    return arr.shape[1] if arr.ndim == 2 else arr.shape[0] * arr.shape[2]


def mat_width(arr):
    return arr.shape[-1]


def mat_spec(arr, tm, tw, rc):
    if arr.ndim == 2:
        return pl.BlockSpec((tm, tw), lambda *g: rc(*g))
    per = arr.shape[2] // tw
    assert arr.shape[2] % tw == 0

    def imap(*g):
        r, c = rc(*g)
        return (c // per, r, c % per)

    return pl.BlockSpec((None, tm, tw), imap)


HBM_SPEC = pl.BlockSpec(memory_space=pltpu.HBM)


class Stage:
    def __init__(self, inputs, out_shapes, sems, start, finish):
        self.inputs, self.out_shapes, self.sems, self.start, self.finish = inputs, out_shapes, sems, start, finish


def run_stage(stage, name):
    n_in, n_out = len(stage.inputs), len(stage.out_shapes)

    def body(*refs):
        ins, outs, sems = refs[:n_in], refs[n_in:n_in + n_out], refs[n_in + n_out:]
        stage.start(ins, outs, sems)
        stage.finish(ins, outs, sems)

    return pl.pallas_call(body, out_shape=list(stage.out_shapes), in_specs=[HBM_SPEC] * n_in,
                          out_specs=[HBM_SPEC] * n_out, scratch_shapes=list(stage.sems), name=name)(*stage.inputs)


def _call(body, ops, carry, *, grid, in_specs, out_specs, out_shape, scratch_shapes, sem, name):
    if carry is None:
        outs = pl.pallas_call(body, grid=grid, in_specs=in_specs, out_specs=out_specs, out_shape=out_shape,
                              scratch_shapes=scratch_shapes, compiler_params=_cp(sem), name=name)(*ops)
        return outs, None
    multi = isinstance(out_shape, (list, tuple))
    shapes = list(out_shape) if multi else [out_shape]
    ospecs = list(out_specs) if multi else [out_specs]
    n_in, n_out, n_sc = len(ops), len(shapes), len(scratch_shapes)
    c_in, c_out = len(carry.inputs), len(carry.out_shapes)

    def wrapped(*refs):
        ins, cin = refs[:n_in], refs[n_in:n_in + c_in]
        o0 = n_in + c_in
        outs, cout = refs[o0:o0 + n_out], refs[o0 + n_out:o0 + n_out + c_out]
        s0 = o0 + n_out + c_out
        scs, csems = refs[s0:s0 + n_sc], refs[s0 + n_sc:]
        first = pl.program_id(0) == 0
        last = pl.program_id(0) == grid[0] - 1
        for d in range(1, len(grid)):
            first = first & (pl.program_id(d) == 0)
            last = last & (pl.program_id(d) == grid[d] - 1)

        @pl.when(first)
        def _():
            carry.start(cin, cout, csems)

        body(*ins, *outs, *scs)

        @pl.when(last)
        def _():
            carry.finish(cin, cout, csems)

    res = pl.pallas_call(
        wrapped, grid=grid, in_specs=list(in_specs) + [HBM_SPEC] * c_in, out_specs=ospecs + [HBM_SPEC] * c_out,
        out_shape=shapes + list(carry.out_shapes), scratch_shapes=list(scratch_shapes) + list(carry.sems),
        compiler_params=_cp(("arbitrary",) * len(grid)), name=name)(*ops, *carry.inputs)
    main = res[:n_out]
    return (list(main) if multi else main[0]), list(res[n_out:])


def norm_matmul(x, xcol, kdim, gain, w, tm, tn, out_dtype, name, epi=None, epi_ops=(), epi_specs=(), carry=None):
    rows, n = x.shape[0], mat_cols(w) if w.ndim == 3 else w.shape[1]
    n_epi = len(epi_ops)
    w_spec = (pl.BlockSpec((kdim, tn), lambda i, j: (0, j)) if w.ndim == 2 else
              pl.BlockSpec((None, kdim, tn), lambda i, j: (j // (w.shape[2] // tn), 0, j % (w.shape[2] // tn))))

    def body(x_ref, g_ref, w_ref, *rest):
        epi_refs = rest[:n_epi]
        out_ref, xn_ref, xn_sc = rest[n_epi:]

        @pl.when(pl.program_id(1) == 0)
        def _():
            xv = x_ref[...]
            y = xv * lax.rsqrt(jnp.mean(xv * xv, axis=-1, keepdims=True) + EPS)
            xn = (y * g_ref[...]).astype(MXU_DT)
            xn_sc[...] = xn
            xn_ref[...] = xn

        acc = jnp.dot(xn_sc[...], w_ref[...], preferred_element_type=F32)
        if epi is not None:
            acc = epi(acc, *[r[...] for r in epi_refs])
        out_ref[...] = acc.astype(out_dtype)

    return _call(
        body, (x, gain, w, *epi_ops), carry, grid=(rows // tm, n // tn),
        in_specs=[pl.BlockSpec((tm, kdim), lambda i, j: (i, xcol)), pl.BlockSpec((1, kdim), lambda i, j: (0, 0)),
                  w_spec, *epi_specs],
        out_specs=[pl.BlockSpec((tm, tn), lambda i, j: (i, j)), pl.BlockSpec((tm, kdim), lambda i, j: (i, 0))],
        out_shape=[S((rows, n), out_dtype), S((rows, kdim), MXU_DT)],
        scratch_shapes=[pltpu.VMEM((tm, kdim), MXU_DT)], sem=("parallel", "arbitrary"), name=name)


def matmul_res(a, w, res, tm, tn, name):
    grp, rows, k = a.shape
    n = w.shape[2]

    def body(a_ref, w_ref, r_ref, o_ref):
        acc = r_ref[...]
        for g in range(grp):
            acc = acc + jnp.dot(a_ref[g], w_ref[g], preferred_element_type=F32)
        o_ref[...] = acc

    return pl.pallas_call(
        body, grid=(rows // tm, n // tn),
        in_specs=[pl.BlockSpec((grp, tm, k), lambda i, j: (0, i, 0)), pl.BlockSpec((grp, k, tn), lambda i, j: (0, 0, j)),
                  pl.BlockSpec((tm, tn), lambda i, j: (i, j))],
        out_specs=pl.BlockSpec((tm, tn), lambda i, j: (i, j)),
        out_shape=S((rows, n), F32), compiler_params=_cp(("parallel", "parallel")), name=name)(a, w, res)


def matmul_nt(a, w, tm, tn, out_dtype, name):
    rows, k = a.shape
    n = w.shape[0]

    def body(a_ref, w_ref, o_ref):
        o_ref[...] = lax.dot_general(a_ref[...].astype(MXU_DT), w_ref[...], NT_DIMS,
                                     preferred_element_type=F32).astype(out_dtype)

    return pl.pallas_call(
        body, grid=(rows // tm, n // tn),
        in_specs=[pl.BlockSpec((tm, k), lambda i, j: (i, 0)), pl.BlockSpec((tn, k), lambda i, j: (j, 0))],
        out_specs=pl.BlockSpec((tm, tn), lambda i, j: (i, j)),
        out_shape=S((rows, n), out_dtype), compiler_params=_cp(("parallel", "parallel")), name=name)(a, w)


def matmul_nt_normbwd(du, w, x, xcol, gain, res, tm, tk, out_dtype, name):
    rows, kc = du.shape[-2], mat_cols(du)
    dn = w.shape[-2]
    nk = kc // tk
    has_res = res is not None
    w_spec = (pl.BlockSpec((dn, tk), lambda i, k: (0, k)) if w.ndim == 2 else
              pl.BlockSpec((None, dn, tk), lambda i, k: (k // (w.shape[2] // tk), 0, k % (w.shape[2] // tk))))

    def body(du_ref, w_ref, x_ref, g_ref, *rest):
        if has_res:
            res_ref, dx_ref, dg_ref, acc = rest
        else:
            dx_ref, dg_ref, acc = rest
        i, k = pl.program_id(0), pl.program_id(1)

        @pl.when(k == 0)
        def _():
            acc[...] = jnp.zeros_like(acc)

        @pl.when((i == 0) & (k == 0))
        def _():
            dg_ref[...] = jnp.zeros_like(dg_ref)

        acc[...] += lax.dot_general(du_ref[...], w_ref[...], NT_DIMS, preferred_element_type=F32)

        @pl.when(k == nk - 1)
        def _():
            dhn = acc[...]
            xv = x_ref[...]
            rstd = lax.rsqrt(jnp.mean(xv * xv, axis=-1, keepdims=True) + EPS)
            xhat = xv * rstd
            dg_ref[...] += jnp.sum(dhn * xhat, axis=0, keepdims=True)
            dxh = dhn * g_ref[...]
            dx = rstd * (dxh - xhat * jnp.mean(dxh * xhat, axis=-1, keepdims=True))
            if has_res:
                dx = dx + res_ref[...]
            dx_ref[...] = dx.astype(out_dtype)

    in_specs = [mat_spec(du, tm, tk, lambda i, k: (i, k)), w_spec,
                pl.BlockSpec((tm, dn), lambda i, k: (i, xcol)), pl.BlockSpec((1, dn), lambda i, k: (0, 0))]
    ops = [du, w, x, gain]
    if has_res:
        in_specs.append(pl.BlockSpec((tm, dn), lambda i, k: (i, 0)))
        ops.append(res)
    return pl.pallas_call(
        body, grid=(rows // tm, nk), in_specs=in_specs,
        out_specs=[pl.BlockSpec((tm, dn), lambda i, k: (i, 0)), pl.BlockSpec((1, dn), lambda i, k: (0, 0))],
        out_shape=[S((rows, dn), out_dtype), S((1, dn), F32)],
        scratch_shapes=[pltpu.VMEM((tm, dn), F32)],
        compiler_params=_cp(("arbitrary", "arbitrary")), name=name)(*ops)


def matmul_tn(a, b, tr, name, carry=None, col_shards=1):
    rows, ka, nb = a.shape[-2], mat_cols(a), mat_cols(b)
    ta = _div_tile(mat_width(a), 1536, LANE)
    tb = _div_tile(mat_width(b), max(LANE, TN_ACC_ELEMS // ta // LANE * LANE), LANE)
    nr = rows // tr
    if col_shards == 1:
        out_spec, out_shape = pl.BlockSpec((ta, tb), lambda i, j, r: (i, j)), S((ka, nb), F32)
    else:
        per = nb // col_shards // tb
        assert per * tb * col_shards == nb
        out_spec = pl.BlockSpec((None, ta, tb), lambda i, j, r: (j // per, i, j % per))
        out_shape = S((col_shards, ka, nb // col_shards), F32)

    def body(a_ref, b_ref, o_ref, acc):
        r = pl.program_id(2)

        @pl.when(r == 0)
        def _():
            acc[...] = jnp.zeros_like(acc)

        acc[...] += lax.dot_general(a_ref[...].astype(MXU_DT), b_ref[...].astype(MXU_DT), TN_DIMS,
                                    preferred_element_type=F32)

        @pl.when(r == nr - 1)
        def _():
            o_ref[...] = acc[...]

    return _call(
        body, (a, b), carry, grid=(ka // ta, nb // tb, nr),
        in_specs=[mat_spec(a, tr, ta, lambda i, j, r: (r, i)), mat_spec(b, tr, tb, lambda i, j, r: (r, j))],
        out_specs=out_spec, out_shape=out_shape, scratch_shapes=[pltpu.VMEM((ta, tb), F32)],
        sem=("parallel", "parallel", "arbitrary"), name=name)


def _sigmoid(x):
    return 1.0 / (1.0 + jnp.exp(-x))


def _sigmoid_by_tanh(x):
    return 0.5 * jnp.tanh(0.5 * x) + 0.5


def _log1p(e):
    return jnp.where(e < 1e-3, e * (1.0 - e * (0.5 - e * (1.0 / 3.0 - 0.25 * e))), jnp.log(1.0 + e))


def _softplus(x):
    return jnp.maximum(x, 0.0) + _log1p(jnp.exp(-jnp.abs(x)))


def _expm1(x):
    series = x * (1.0 + x * (0.5 + x * (1.0 / 6.0 + x * (1.0 / 24.0 + x * (1.0 / 120.0)))))
    return jnp.where(jnp.abs(x) < 0.1, series, jnp.exp(x) - 1.0)


_GELU_K = math.sqrt(2.0 / math.pi)
_GELU_C = 0.044715


def _gelu_and_grad(x):
    th = jnp.tanh(_GELU_K * (x + _GELU_C * x * x * x))
    g = 0.5 * x * (1.0 + th)
    dg = 0.5 * (1.0 + th) + 0.5 * x * (1.0 - th * th) * _GELU_K * (1.0 + 3.0 * _GELU_C * x * x)
    return g, dg


def _row_iota(shape):
    return lax.broadcasted_iota(jnp.int32, shape, 0)


def _scan_chunk_fwd(a_sc, u_sc, out_ref, hcar, n, width):
    rowi = _row_iota((SUBLANE, width))

    def step(c, hprev):
        r0 = pl.multiple_of(c * SUBLANE, SUBLANE)
        a = a_sc[pl.ds(r0, SUBLANE), :]
        u = u_sc[pl.ds(r0, SUBLANE), :]
        for d in (1, 2, 4):
            a_s = jnp.where(rowi >= d, pltpu.roll(a, d, axis=0), 1.0)
            u_s = jnp.where(rowi >= d, pltpu.roll(u, d, axis=0), 0.0)
            u = u + a * u_s
            a = a * a_s
        h = u + a * hprev
        out_ref[pl.ds(r0, SUBLANE), :] = h
        return jnp.broadcast_to(h[SUBLANE - 1:SUBLANE, :], (SUBLANE, width))

    hcar[...] = lax.fori_loop(0, n // SUBLANE, step, hcar[...], unroll=4)


def _scan_chunk_bwd(b_sc, d_sc, out_ref, gcar, n, width):
    rowi = _row_iota((SUBLANE, width))
    nc = n // SUBLANE

    def step(c, gnext):
        r0 = pl.multiple_of((nc - 1 - c) * SUBLANE, SUBLANE)
        b = b_sc[pl.ds(r0, SUBLANE), :]
        d = d_sc[pl.ds(r0, SUBLANE), :]
        for s in (1, 2, 4):
            keep = rowi < SUBLANE - s
            b_s = jnp.where(keep, pltpu.roll(b, SUBLANE - s, axis=0), 1.0)
            d_s = jnp.where(keep, pltpu.roll(d, SUBLANE - s, axis=0), 0.0)
            d = d + b * d_s
            b = b * b_s
        g = d + b * gnext
        out_ref[pl.ds(r0, SUBLANE), :] = g
        return jnp.broadcast_to(g[0:1, :], (SUBLANE, width))

    gcar[...] = lax.fori_loop(0, nc, step, gcar[...], unroll=4)


def even_mid_fwd(u, conv_a, conv_b, conv_b_bias, rw, rb, iw, ib, lam, nb, tp, n, name, carry=None):
    rows = u.shape[0]
    w = EV_CT
    nj = CONV_W // w
    nt = tp // n
    h8 = SUBLANE

    def body(gb_r, gc_r, xa_r, xb_r, gate_r, ca_w, cb_w, cb_b, rw_r, rb_r, iw_r, ib_r, lam_r,
             y_o, ca_o, xc_o, a_o, hs_o, pext, xext, hcar, a_sc, u_sc):
        t = pl.program_id(2)

        @pl.when(t == 0)
        def _():
            pext[0:h8, :] = jnp.zeros((h8, w), F32)
            xext[0:h8, :] = jnp.zeros((h8, w), F32)
            hcar[...] = jnp.zeros_like(hcar)

        p = gc_r[...] * xa_r[...]
        pext[h8:h8 + n, :] = p
        wa = ca_w[...]
        ca = wa[2:3, :] * p + wa[1:2, :] * pext[h8 - 1:h8 - 1 + n, :] + wa[0:1, :] * pext[h8 - 2:h8 - 2 + n, :]
        ca_o[...] = ca
        y_o[0] = (gb_r[...] * ca).astype(MXU_DT)
        pext[0:h8, :] = pext[n:n + h8, :]

        xb = xb_r[...]
        xext[h8:h8 + n, :] = xb
        wb = cb_w[...]
        xc = (wb[3:4, :] * xb + wb[2:3, :] * xext[h8 - 1:h8 - 1 + n, :] + wb[1:2, :] * xext[h8 - 2:h8 - 2 + n, :]
              + wb[0:1, :] * xext[h8 - 3:h8 - 3 + n, :]) + cb_b[...]
        xc_o[...] = xc
        xext[0:h8, :] = xext[n:n + h8, :]

        xcm = xc.astype(MXU_DT)
        r = _sigmoid(jnp.dot(xcm, rw_r[...], preferred_element_type=F32) + rb_r[...])
        ig = _sigmoid(jnp.dot(xcm, iw_r[...], preferred_element_type=F32) + ib_r[...])
        log_a = (-LRU_C) * r * _softplus(-lam_r[...])
        a = jnp.exp(log_a)
        mult = jnp.sqrt(-_expm1(2.0 * log_a))
        a_sc[...] = a
        a_o[...] = a
        u_sc[...] = mult * (ig * xc)
        _scan_chunk_fwd(a_sc, u_sc, hs_o, hcar, n, w)
        gel, _ = _gelu_and_grad(gate_r[...])
        y_o[1] = (gel * hs_o[...]).astype(MXU_DT)

    def ublk(off):
        return pl.BlockSpec((n, w), lambda j, b, t: (b * nt + t, off + j))

    def pblk(r_):
        return pl.BlockSpec((r_, w), lambda j, b, t: (0, j))

    act = pl.BlockSpec((n, w), lambda j, b, t: (b * nt + t, j))
    mat = pl.BlockSpec((w, w), lambda j, b, t: (j, j))
    return _call(
        body, (u, u, u, u, u, conv_a, conv_b, conv_b_bias, rw, rb, iw, ib, lam), carry, grid=(nj, nb, nt),
        in_specs=[ublk(0), ublk(nj), ublk(2 * nj), ublk(3 * nj), ublk(4 * nj), pblk(3), pblk(4), pblk(1),
                  mat, pblk(1), mat, pblk(1), pblk(1)],
        out_specs=[pl.BlockSpec((2, n, w), lambda j, b, t: (0, b * nt + t, j)), act, act, act, act],
        out_shape=[S((2, rows, CONV_W), MXU_DT), S((rows, CONV_W), F32), S((rows, LRU_W), F32), S((rows, LRU_W), F32),
                   S((rows, LRU_W), F32)],
        scratch_shapes=[pltpu.VMEM((n + h8, w), F32), pltpu.VMEM((n + h8, w), F32), pltpu.VMEM((h8, w), F32),
                        pltpu.VMEM((n, w), F32), pltpu.VMEM((n, w), F32)],
        sem=("parallel", "parallel", "arbitrary"), name=name)


def even_mid_bwd(u, dycat, ca, xc, a_sv, hs, conv_a, conv_b, rw, rb, iw, ib, lam, nb, tp, n, name, carry=None):
    rows = u.shape[0]
    w = EV_CT
    nj = CONV_W // w
    nt = tp // n
    h8 = SUBLANE

    def body(gb_r, gc_r, xa_r, xb_r, gate_r, dya_r, dyb_r, ca_r, xc_r, a_r, hs_r, hsp_r,
             ca_w, cb_w, rw_r, rb_r, iw_r, ib_r, lam_r,
             du_o, dca_w, dcb_w, dcb_b, drw, drb, diw, dib, dlam,
             aext, hext, dext, eext, gcar, b_sc, d_sc, g_sc):
        b, t = pl.program_id(1), pl.program_id(2)

        @pl.when((b == 0) & (t == 0))
        def _():
            for ref in (dca_w, dcb_w, dcb_b, drw, drb, diw, dib, dlam):
                ref[...] = jnp.zeros_like(ref)

        @pl.when(t == 0)
        def _():
            aext[n:n + h8, :] = jnp.zeros((h8, w), F32)
            dext[n:n + h8, :] = jnp.zeros((h8, w), F32)
            eext[n:n + h8, :] = jnp.zeros((h8, w), F32)
            gcar[...] = jnp.zeros_like(gcar)

        xc_v = xc_r[...]
        xcm = xc_v.astype(MXU_DT)
        r = _sigmoid(jnp.dot(xcm, rw_r[...], preferred_element_type=F32) + rb_r[...])
        ig = _sigmoid(jnp.dot(xcm, iw_r[...], preferred_element_type=F32) + ib_r[...])
        lam_v = lam_r[...]
        sp = _softplus(-lam_v)
        log_a = (-LRU_C) * r * sp
        a = a_r[...]
        mult = jnp.sqrt(-_expm1(2.0 * log_a))
        hs_v = hs_r[...]
        gel, dgel = _gelu_and_grad(gate_r[...])
        dyb = dyb_r[...]
        du_o[4] = (dyb * hs_v * dgel).astype(MXU_DT)

        aext[0:n, :] = a
        b_sc[...] = aext[1:1 + n, :]
        d_sc[...] = dyb * gel
        _scan_chunk_bwd(b_sc, d_sc, g_sc, gcar, n, w)
        aext[n:n + h8, :] = aext[0:h8, :]
        g = g_sc[...]

        hext[0:h8, :] = jnp.where(t == nt - 1, 0.0, hsp_r[...])
        hext[h8:h8 + n, :] = hs_v
        da = g * hext[h8 - 1:h8 - 1 + n, :]
        dmult = g * (ig * xc_v)
        di = g * mult * xc_v
        dxc = g * mult * ig
        dlog_a = da * a - dmult * (a * a) / mult
        dr = dlog_a * ((-LRU_C) * sp)
        dsp = jnp.sum(dlog_a * ((-LRU_C) * r), axis=0, keepdims=True)
        dlam[...] += dsp * (-_sigmoid(-lam_v))
        dzr = dr * r * (1.0 - r)
        dzi = di * ig * (1.0 - ig)
        dzr_m = dzr.astype(MXU_DT)
        dzi_m = dzi.astype(MXU_DT)
        dxc = (dxc + lax.dot_general(dzr_m, rw_r[...], NT_DIMS, preferred_element_type=F32)
               + lax.dot_general(dzi_m, iw_r[...], NT_DIMS, preferred_element_type=F32))
        drw[...] += lax.dot_general(xcm, dzr_m, TN_DIMS, preferred_element_type=F32)
        diw[...] += lax.dot_general(xcm, dzi_m, TN_DIMS, preferred_element_type=F32)
        drb[...] += jnp.sum(dzr, axis=0, keepdims=True)
        dib[...] += jnp.sum(dzi, axis=0, keepdims=True)
        dcb_b[...] += jnp.sum(dxc, axis=0, keepdims=True)

        xb = xb_r[...]
        dext[0:n, :] = dxc
        wb = cb_w[...]
        d1, d2, d3 = dext[1:1 + n, :], dext[2:2 + n, :], dext[3:3 + n, :]
        du_o[3] = (wb[3:4, :] * dxc + wb[2:3, :] * d1 + wb[1:2, :] * d2 + wb[0:1, :] * d3).astype(MXU_DT)
        dcb_w[3:4, :] += jnp.sum(xb * dxc, axis=0, keepdims=True)
        dcb_w[2:3, :] += jnp.sum(xb * d1, axis=0, keepdims=True)
        dcb_w[1:2, :] += jnp.sum(xb * d2, axis=0, keepdims=True)
        dcb_w[0:1, :] += jnp.sum(xb * d3, axis=0, keepdims=True)
        dext[n:n + h8, :] = dext[0:h8, :]

        gb, gc, xa = gb_r[...], gc_r[...], xa_r[...]
        dya = dya_r[...]
        du_o[0] = (dya * ca_r[...]).astype(MXU_DT)
        dca = dya * gb
        eext[0:n, :] = dca
        wa = ca_w[...]
        e1, e2 = eext[1:1 + n, :], eext[2:2 + n, :]
        dp = wa[2:3, :] * dca + wa[1:2, :] * e1 + wa[0:1, :] * e2
        p = gc * xa
        dca_w[2:3, :] += jnp.sum(p * dca, axis=0, keepdims=True)
        dca_w[1:2, :] += jnp.sum(p * e1, axis=0, keepdims=True)
        dca_w[0:1, :] += jnp.sum(p * e2, axis=0, keepdims=True)
        eext[n:n + h8, :] = eext[0:h8, :]
        du_o[1] = (dp * xa).astype(MXU_DT)
        du_o[2] = (dp * gc).astype(MXU_DT)

    def rt(b, t):
        return b * nt + (nt - 1 - t)

    def ublk(off):
        return pl.BlockSpec((n, w), lambda j, b, t: (rt(b, t), off + j))

    def pblk(r_):
        return pl.BlockSpec((r_, w), lambda j, b, t: (0, j))

    act = pl.BlockSpec((n, w), lambda j, b, t: (rt(b, t), j))
    n8 = n // h8
    hsp = pl.BlockSpec((h8, w), lambda j, b, t: (jnp.maximum(rt(b, t) * n8 - 1, 0), j))
    mat = pl.BlockSpec((w, w), lambda j, b, t: (j, j))
    return _call(
        body, (u, u, u, u, u, dycat, dycat, ca, xc, a_sv, hs, hs, conv_a, conv_b, rw, rb, iw, ib, lam), carry,
        grid=(nj, nb, nt),
        in_specs=[ublk(0), ublk(nj), ublk(2 * nj), ublk(3 * nj), ublk(4 * nj), ublk(0), ublk(nj), act, act, act, act,
                  hsp, pblk(3), pblk(4), mat, pblk(1), mat, pblk(1), pblk(1)],
        out_specs=[pl.BlockSpec((5, n, w), lambda j, b, t: (0, rt(b, t), j)), pblk(3), pblk(4), pblk(1),
                   mat, pblk(1), mat, pblk(1), pblk(1)],
        out_shape=[S((5, rows, CONV_W), MXU_DT), S((3, CONV_W), F32), S((4, LRU_W), F32), S((1, LRU_W), F32),
                   S((LRU_W, LRU_W), F32), S((1, LRU_W), F32), S((LRU_W, LRU_W), F32), S((1, LRU_W), F32),
                   S((1, LRU_W), F32)],
        scratch_shapes=[pltpu.VMEM((n + h8, w), F32)] * 4 + [pltpu.VMEM((h8, w), F32)] + [pltpu.VMEM((n, w), F32)] * 3,
        sem=("arbitrary", "arbitrary", "arbitrary"), name=name)


def ffn_mid_fwd(up, cw, cb, nb, tp, n, name):
    rows = up.shape[0]
    w = FFN_CT
    nj = D_FF // w
    nt = tp // n
    h8 = SUBLANE

    sr = STRIP_ROWS
    assert n % sr == 0, (n, sr)

    def body(xa_r, xg_r, w_r, b_r, u_o, y_o, halo):
        t = pl.program_id(2)

        @pl.when(t == 0)
        def _():
            halo[...] = jnp.zeros_like(halo)

        wv = (w_r[0], w_r[1])
        bv = (b_r[0], b_r[1])

        def strip(s, carry):
            r0 = pl.multiple_of(s * sr, sr)
            us, new = [], []
            for g, x_r in enumerate((xa_r, xg_r)):
                x = x_r[pl.ds(r0, sr), :].astype(F32)
                win = jnp.concatenate([carry[g], x], axis=0)
                x1 = pltpu.roll(win, 1, axis=0)[h8:, :]
                x2 = pltpu.roll(win, 2, axis=0)[h8:, :]
                u = (wv[g][2:3, :] * x + wv[g][1:2, :] * x1 + wv[g][0:1, :] * x2) + bv[g]
                u_o[g, pl.ds(r0, sr), :] = u.astype(MXU_DT)
                us.append(u)
                new.append(x[sr - h8:, :])
            y_o[pl.ds(r0, sr), :] = (us[0] * _sigmoid_by_tanh(us[0]) * us[1]).astype(MXU_DT)
            return tuple(new)

        ha, hg = lax.fori_loop(0, n // sr, strip, (halo[0], halo[1]))
        halo[0] = ha
        halo[1] = hg

    def ublk(off):
        return pl.BlockSpec((n, w), lambda j, b, t: (b * nt + t, off + j))

    return pl.pallas_call(
        body, grid=(nj, nb, nt),
        in_specs=[ublk(0), ublk(nj), pl.BlockSpec((2, 3, w), lambda j, b, t: (0, 0, j)),
                  pl.BlockSpec((2, 1, w), lambda j, b, t: (0, 0, j))],
        out_specs=[pl.BlockSpec((2, n, w), lambda j, b, t: (0, b * nt + t, j)), ublk(0)],
        out_shape=[S((2, rows, D_FF), MXU_DT), S((rows, D_FF), MXU_DT)],
        scratch_shapes=[pltpu.VMEM((2, h8, w), F32)],
        compiler_params=_cp(("parallel", "parallel", "arbitrary")), name=name,
    )(up, up, cw, cb)


def ffn_mid_bwd(dy, u, up, cw, nb, tp, n, name, carry=None):
    rows = up.shape[0]
    w = FFN_CT
    nj = D_FF // w
    nt = tp // n
    h8 = SUBLANE

    sr = STRIP_ROWS
    assert n % sr == 0, (n, sr)
    ns = n // sr

    def fold(v):
        acc = v[0:h8, :]
        for k in range(1, sr // h8):
            acc = acc + v[k * h8:(k + 1) * h8, :]
        return acc

    def body(dy_r, u_r, xa_r, xg_r, w_r, dx_o, dw, db, halo):
        b, t = pl.program_id(1), pl.program_id(2)

        @pl.when((b == 0) & (t == 0))
        def _():
            dw[...] = jnp.zeros_like(dw)
            db[...] = jnp.zeros_like(db)

        @pl.when(t == 0)
        def _():
            halo[...] = jnp.zeros_like(halo)

        wv = (w_r[0], w_r[1])

        def strip(s, carry):
            halos, sums = carry
            r0 = pl.multiple_of((ns - 1 - s) * sr, sr)
            dyv = dy_r[pl.ds(r0, sr), :].astype(F32)
            ua = u_r[0, pl.ds(r0, sr), :].astype(F32)
            ug = u_r[1, pl.ds(r0, sr), :].astype(F32)
            sg = _sigmoid_by_tanh(ua)
            dus = (dyv * ug * (sg * (1.0 + ua * (1.0 - sg))), dyv * (ua * sg))
            new_halos, new_sums = [], []
            for g, x_r in enumerate((xa_r, xg_r)):
                du = dus[g]
                win = jnp.concatenate([du, halos[g]], axis=0)
                d1 = pltpu.roll(win, sr + h8 - 1, axis=0)[0:sr, :]
                d2 = pltpu.roll(win, sr + h8 - 2, axis=0)[0:sr, :]
                dx_o[g, pl.ds(r0, sr), :] = (wv[g][2:3, :] * du + wv[g][1:2, :] * d1 + wv[g][0:1, :] * d2).astype(MXU_DT)
                x = x_r[pl.ds(r0, sr), :].astype(F32)
                s2, s1, s0, sb = sums[g]
                new_sums.append((s2 + fold(x * du), s1 + fold(x * d1), s0 + fold(x * d2), sb + fold(du)))
                new_halos.append(du[0:h8, :])
            return tuple(new_halos), tuple(new_sums)

        z = jnp.zeros((h8, w), F32)
        halos, sums = lax.fori_loop(0, ns, strip, ((halo[0], halo[1]), ((z, z, z, z), (z, z, z, z))))
        halo[0] = halos[0]
        halo[1] = halos[1]
        for g in range(2):
            s2, s1, s0, sb = sums[g]
            dw[g, 2:3, :] += jnp.sum(s2, axis=0, keepdims=True)
            dw[g, 1:2, :] += jnp.sum(s1, axis=0, keepdims=True)
            dw[g, 0:1, :] += jnp.sum(s0, axis=0, keepdims=True)
            db[g] += jnp.sum(sb, axis=0, keepdims=True)

    def rt(b, t):
        return b * nt + (nt - 1 - t)

    def ublk(off):
        return pl.BlockSpec((n, w), lambda j, b, t: (rt(b, t), off + j))

    pair = pl.BlockSpec((2, n, w), lambda j, b, t: (0, rt(b, t), j))
    return _call(
        body, (dy, u, up, up, cw), carry, grid=(nj, nb, nt),
        in_specs=[ublk(0), pair, ublk(0), ublk(nj), pl.BlockSpec((2, 3, w), lambda j, b, t: (0, 0, j))],
        out_specs=[pair, pl.BlockSpec((2, 3, w), lambda j, b, t: (0, 0, j)),
                   pl.BlockSpec((2, 1, w), lambda j, b, t: (0, 0, j))],
        out_shape=[S((2, rows, D_FF), MXU_DT), S((2, 3, D_FF), F32), S((2, 1, D_FF), F32)],
        scratch_shapes=[pltpu.VMEM((2, h8, w), F32)],
        sem=("arbitrary", "arbitrary", "arbitrary"), name=name)


def _lane_mod(shape):
    return lax.broadcasted_iota(jnp.int32, shape, 1) & (HP - 1)


def _q_rope_epi(acc, tab):
    reps = acc.shape[1] // HP
    a = acc * jnp.tile(tab, (1, reps))
    lane = _lane_mod(a.shape)
    shifted = pltpu.roll(a, a.shape[1] - QK_ROPE, axis=1)
    return jnp.where(lane < QK_NOPE, a, jnp.where(lane < QK_HEAD, a + shifted, 0.0)) * Q_PRESCALE


def _k_rope_block(krblk, tabk):
    a = krblk * tabk
    lane = _lane_mod(a.shape)
    b = a + pltpu.roll(a, HP - QK_ROPE, axis=1)
    return jnp.where((lane >= QK_NOPE) & (lane < QK_HEAD), b, 0.0)


def _k_rope_epi(acc, krblk, tabk):
    reps = acc.shape[1] // HP
    return acc + jnp.tile(_k_rope_block(krblk, tabk), (1, reps))


def attn_fwd(q, k, v, nb, tp, name, carry=None):
    rows = q.shape[0]
    blk = ATT_BLK
    nq = tp // blk
    npair = MLA_HEADS // 2

    def body(q_r, k_r, v_r, o_r, lse_r):
        qi = pl.program_id(2)
        lane = lax.broadcasted_iota(jnp.int32, (blk, LANE), 1)
        even = lane < V_HEAD
        sum_lane = (V_HEAD, 0)
        rowi = lax.broadcasted_iota(jnp.int32, (blk, blk), 0)
        coli = lax.broadcasted_iota(jnp.int32, (blk, blk), 1)
        qs = [q_r[:, h * HP:(h + 1) * HP] for h in range(2)]

        def kv_block(k0, width, carry, visible):
            ms, accs = carry
            vblk = v_r[pl.ds(k0, width), :]
            one = jnp.ones_like(vblk)
            zero = jnp.zeros_like(vblk)
            vlane = lax.broadcasted_iota(jnp.int32, (width, LANE), 1)
            ss = [lax.dot_general(qs[h], k_r[pl.ds(k0, width), h * HP:(h + 1) * HP], NT_DIMS,
                                  preferred_element_type=F32) for h in range(2)]
            new_ms, new_accs = [], []
            for h in range(2):
                s = ss[h]
                if visible is not None:
                    s = jnp.where(visible, s, -jnp.inf)
                m_new = jnp.maximum(ms[h], jnp.max(s, axis=1, keepdims=True))
                alpha = jnp.exp2(ms[h] - m_new)
                p = jnp.exp2(s - m_new).astype(MXU_DT)
                mine = (vlane < V_HEAD) if h == 0 else (vlane >= V_HEAD)
                vh = jnp.where(mine, vblk, jnp.where(vlane == sum_lane[h], one, zero))
                new_accs.append(alpha * accs[h] + jnp.dot(p, vh, preferred_element_type=F32))
                new_ms.append(m_new)
            return tuple(new_ms), tuple(new_accs)

        neg = jnp.full((blk, 1), -jnp.inf, F32)
        zacc = jnp.zeros((blk, LANE), F32)
        carry = lax.fori_loop(0, qi // 2, lambda i, c: kv_block(pl.multiple_of(i * 2 * blk, blk), 2 * blk, c, None),
                              ((neg, neg), (zacc, zacc)))
        rowi2 = lax.broadcasted_iota(jnp.int32, (blk, 2 * blk), 0)
        coli2 = lax.broadcasted_iota(jnp.int32, (blk, 2 * blk), 1)
        ms, accs = lax.cond(
            qi % 2 == 1,
            lambda c: kv_block(pl.multiple_of((qi - 1) * blk, blk), 2 * blk, c, coli2 - blk <= rowi2),
            lambda c: kv_block(pl.multiple_of(qi * blk, blk), blk, c, coli <= rowi), carry)
        ls = [accs[h][:, sum_lane[h]:sum_lane[h] + 1] for h in range(2)]
        o_r[...] = jnp.where(even, accs[0] / ls[0], accs[1] / ls[1]).astype(MXU_DT)
        lse_r[...] = jnp.where(even, ms[0] + jnp.log2(ls[0]), ms[1] + jnp.log2(ls[1]))

    return _call(
        body, (q, k, v), carry, grid=(nb, npair, nq),
        in_specs=[pl.BlockSpec((blk, 2 * HP), lambda b, p, i: (b * nq + i, p)),
                  pl.BlockSpec((tp, 2 * HP), lambda b, p, i: (b, p)),
                  pl.BlockSpec((tp, LANE), lambda b, p, i: (b, p))],
        out_specs=[pl.BlockSpec((blk, LANE), lambda b, p, i: (b * nq + i, p)),
                   pl.BlockSpec((None, blk, LANE), lambda b, p, i: (p, b * nq + i, 0))],
        out_shape=[S((rows, MLA_HEADS * V_HEAD), MXU_DT), S((npair, rows, LANE), F32)], scratch_shapes=[],
        sem=("parallel", "parallel", "arbitrary"), name=name)


def attn_bwd(q, k, v, o, do, lse, nb, tp, name, carry=None):
    rows = q.shape[0]
    blk = ATT_BLK
    nq = tp // blk
    npair = MLA_HEADS // 2
    scale = QK_HEAD ** -0.5

    def body(q_r, k_r, v_r, o_r, do_r, lse_r, dq_o, dk_o, dv_o, dq_acc, delta_sc):
        kb = pl.program_id(2)
        even = lax.broadcasted_iota(jnp.int32, (blk, LANE), 1) < V_HEAD
        rowi = lax.broadcasted_iota(jnp.int32, (blk, blk), 0)
        coli = lax.broadcasted_iota(jnp.int32, (blk, blk), 1)

        @pl.when(kb == 0)
        def _():
            dq_acc[...] = jnp.zeros_like(dq_acc)

            def dstep(i, c):
                r0 = pl.multiple_of(i * blk, blk)
                prod = do_r[pl.ds(r0, blk), :].astype(F32) * o_r[pl.ds(r0, blk), :].astype(F32)
                de = jnp.sum(jnp.where(even, prod, 0.0), axis=1, keepdims=True)
                dd = jnp.sum(jnp.where(even, 0.0, prod), axis=1, keepdims=True)
                delta_sc[pl.ds(r0, blk), :] = jnp.where(even, de, dd)
                return c

            lax.fori_loop(0, nq, dstep, 0)

        vblk = v_r[...]
        ks = [k_r[:, h * HP:(h + 1) * HP] for h in range(2)]

        def q_block(r0, height, carry, visible):
            dk0, dk1, dv = carry
            dob = do_r[pl.ds(r0, height), :]
            lse_b = lse_r[pl.ds(r0, height), :]
            dl_b = delta_sc[pl.ds(r0, height), :]
            qlane = lax.broadcasted_iota(jnp.int32, (height, LANE), 1)
            dks = [dk0, dk1]
            qhs = [q_r[pl.ds(r0, height), h * HP:(h + 1) * HP] for h in range(2)]
            dohs = [jnp.where((qlane < V_HEAD) if h == 0 else (qlane >= V_HEAD), dob, jnp.zeros_like(dob))
                    for h in range(2)]
            ss = [lax.dot_general(qhs[h], ks[h], NT_DIMS, preferred_element_type=F32) for h in range(2)]
            dps = [lax.dot_general(dohs[h], vblk, NT_DIMS, preferred_element_type=F32) for h in range(2)]
            for h in range(2):
                lo = 0 if h == 0 else V_HEAD
                p = jnp.exp2(ss[h] - lse_b[:, lo:lo + 1])
                if visible is not None:
                    p = jnp.where(visible, p, 0.0)
                ds = (p * (dps[h] - dl_b[:, lo:lo + 1])).astype(MXU_DT)
                dv = dv + lax.dot_general(p.astype(MXU_DT), dohs[h], TN_DIMS, preferred_element_type=F32)
                dks[h] = dks[h] + lax.dot_general(ds, qhs[h], TN_DIMS, preferred_element_type=F32)
                dq_acc[pl.ds(r0, height), h * HP:(h + 1) * HP] += jnp.dot(ds, ks[h], preferred_element_type=F32)
            return dks[0], dks[1], dv

        z = jnp.zeros((blk, HP), F32)
        below = nq - 1 - kb
        odd = below % 2
        rowi2 = lax.broadcasted_iota(jnp.int32, (2 * blk, blk), 0)
        coli2 = lax.broadcasted_iota(jnp.int32, (2 * blk, blk), 1)
        first = pl.multiple_of(kb * blk, blk)
        carry = lax.cond(odd == 1, lambda c: q_block(first, 2 * blk, c, coli2 <= rowi2),
                         lambda c: q_block(first, blk, c, coli <= rowi), (z, z, jnp.zeros((blk, LANE), F32)))
        dk0, dk1, dv = lax.fori_loop(
            0, below // 2, lambda i, c: q_block(pl.multiple_of((kb + 1 + odd + 2 * i) * blk, blk), 2 * blk, c, None),
            carry)
        dk_o[:, 0:HP] = (dk0 * (scale / Q_PRESCALE)).astype(MXU_DT)
        dk_o[:, HP:2 * HP] = (dk1 * (scale / Q_PRESCALE)).astype(MXU_DT)
        dv_o[...] = dv.astype(MXU_DT)

        @pl.when(kb == nq - 1)
        def _():
            dq_o[...] = (dq_acc[...] * scale).astype(MXU_DT)

    seq_pair = pl.BlockSpec((tp, LANE), lambda b, p, kk: (b, p))
    return _call(
        body, (q, k, v, o, do, lse), carry, grid=(nb, npair, nq),
        in_specs=[pl.BlockSpec((tp, 2 * HP), lambda b, p, kk: (b, p)),
                  pl.BlockSpec((blk, 2 * HP), lambda b, p, kk: (b * nq + kk, p)),
                  pl.BlockSpec((blk, LANE), lambda b, p, kk: (b * nq + kk, p)),
                  seq_pair, seq_pair, pl.BlockSpec((None, tp, LANE), lambda b, p, kk: (p, b, 0))],
        out_specs=[pl.BlockSpec((tp, 2 * HP), lambda b, p, kk: (b, p)),
                   pl.BlockSpec((blk, 2 * HP), lambda b, p, kk: (b * nq + kk, p)),
                   pl.BlockSpec((blk, LANE), lambda b, p, kk: (b * nq + kk, p))],
        out_shape=[S((rows, MLA_HEADS * HP), MXU_DT), S((rows, MLA_HEADS * HP), MXU_DT),
                   S((rows, MLA_HEADS * V_HEAD), MXU_DT)],
        scratch_shapes=[pltpu.VMEM((tp, 2 * HP), F32), pltpu.VMEM((tp, LANE), F32)],
        sem=("parallel", "parallel", "arbitrary"), name=name)


def rope_bwd(dq, dk, dv, tabq, tabk, tp, tm, name):
    rows = dq.shape[0]
    nt = tp // tm
    wq = MLA_HEADS * HP

    def body(dq_r, dk_r, dv_r, tq_r, tk_r, dqa_o, dkv_o, dkr_o):
        dqv = dq_r[...].astype(F32)
        lane = _lane_mod(dqv.shape)
        in_rope = (lane >= QK_NOPE) & (lane < QK_HEAD)
        rope = jnp.where(in_rope, dqv, 0.0)
        da = jnp.where(lane < QK_HEAD, dqv, 0.0) + pltpu.roll(rope, QK_ROPE, axis=1)
        dqa_o[...] = (da * jnp.tile(tq_r[...], (1, MLA_HEADS))).astype(MXU_DT)
        dkf = dk_r[...].astype(F32)
        dkv_o[:, 0:wq] = jnp.where(lane < QK_NOPE, dkf, 0.0).astype(MXU_DT)
        dkv_o[:, wq:] = dv_r[...]
        kr = jnp.where(in_rope, dkf, 0.0)
        tot = kr[:, 0:HP]
        for h in range(1, MLA_HEADS):
            tot = tot + kr[:, h * HP:(h + 1) * HP]
        dkr_o[...] = ((tot + pltpu.roll(tot, QK_ROPE, axis=1)) * tk_r[...]).astype(MXU_DT)

    def rowblk(wd):
        return pl.BlockSpec((tm, wd), lambda i: (i, 0))

    tab = pl.BlockSpec((tm, HP), lambda i: (i % nt, 0))
    return pl.pallas_call(
        body, grid=(rows // tm,), in_specs=[rowblk(wq), rowblk(wq), rowblk(MLA_HEADS * V_HEAD), tab, tab],
        out_specs=[rowblk(wq), rowblk(wq + MLA_HEADS * V_HEAD), rowblk(HP)],
        out_shape=[S((rows, wq), MXU_DT), S((rows, wq + MLA_HEADS * V_HEAD), MXU_DT), S((rows, HP), MXU_DT)],
        compiler_params=_cp(("parallel",)), name=name)(dq, dk, dv, tabq, tabk)


def loss_head(h, target, gain, tp, t_real, tm, name):
    rows = h.shape[0]
    nt = tp // tm

    def body(h_r, t_r, g_r, dh_o, loss_o, dg_o):
        i = pl.program_id(0)

        @pl.when(i == 0)
        def _():
            loss_o[...] = jnp.zeros_like(loss_o)
            dg_o[...] = jnp.zeros_like(dg_o)

        xv = h_r[...]
        rstd = lax.rsqrt(jnp.mean(xv * xv, axis=-1, keepdims=True) + EPS)
        xhat = xv * rstd
        g = g_r[...]
        pos = (i % nt) * tm + lax.broadcasted_iota(jnp.int32, (tm, 1), 0)
        valid = (pos >= N_META) & (pos < t_real)
        err = jnp.where(valid, xhat * g - t_r[...], 0.0)
        loss_o[...] += 0.5 * jnp.sum(jnp.mean(err * err, axis=-1, keepdims=True))
        dy = err * (1.0 / D_MODEL)
        dg_o[...] += jnp.sum(dy * xhat, axis=0, keepdims=True)
        dxh = dy * g
        dh_o[...] = rstd * (dxh - xhat * jnp.mean(dxh * xhat, axis=-1, keepdims=True))

    blk = pl.BlockSpec((tm, D_MODEL), lambda i: (i, 0))
    return pl.pallas_call(
        body, grid=(rows // tm,), in_specs=[blk, blk, pl.BlockSpec((1, D_MODEL), lambda i: (0, 0))],
        out_specs=[blk, pl.BlockSpec((1, LANE), lambda i: (0, 0)), pl.BlockSpec((1, D_MODEL), lambda i: (0, 0))],
        out_shape=[S((rows, D_MODEL), F32), S((1, LANE), F32), S((1, D_MODEL), F32)],
        compiler_params=_cp(("arbitrary",)), name=name)(h, target, gain)


ADAM_TILE_ELEMS = 512 * 1024


def adamw(g, w, m, v, name):
    shape = w.shape
    cols = shape[-1]
    rws = max(1, math.prod(shape[:-1]))
    tr = rws if rws * cols <= ADAM_TILE_ELEMS else _div_tile(rws, max(SUBLANE, ADAM_TILE_ELEMS // cols), SUBLANE)
    bc1 = 1.0 - ADAM_B1 ** ADAM_STEP
    bc2 = 1.0 - ADAM_B2 ** ADAM_STEP

    def body(g_r, w_r, m_r, v_r, do, mo, vo):
        gv = g_r[...]
        mn = ADAM_B1 * m_r[...] + (1.0 - ADAM_B1) * gv
        vn = ADAM_B2 * v_r[...] + (1.0 - ADAM_B2) * (gv * gv)
        m_hat = mn / bc1
        v_hat = vn / bc2
        do[...] = -ADAM_LR * (m_hat / (jnp.sqrt(v_hat) + ADAM_EPS) + ADAM_WD * w_r[...])
        mo[...] = mn
        vo[...] = vn

    blk = pl.BlockSpec((tr, cols), lambda i: (i, 0))
    outs = pl.pallas_call(
        body, grid=(rws // tr,), in_specs=[blk] * 4, out_specs=[blk] * 3, out_shape=[S((rws, cols), F32)] * 3,
        compiler_params=_cp(("parallel",)), name=name)(*[a.reshape(rws, cols) for a in (g, w, m, v)])
    return tuple(o.reshape(shape) for o in outs)


SUM_TILE_ELEMS = 512 * 1024


def _place():
    return lax.axis_index("x"), lax.axis_index("y"), lax.axis_index("c")


def _remote(src, dst, send_sems, recv_sems, k, to):
    return pltpu.make_async_remote_copy(src_ref=src, dst_ref=dst, send_sem=send_sems.at[k], recv_sem=recv_sems.at[k],
                                        device_id=to, device_id_type=MESH)


def chip_index():
    return 2 * lax.axis_index("x") + lax.axis_index("y")


def _sem_pair(n):
    return [pltpu.SemaphoreType.DMA((n,)), pltpu.SemaphoreType.DMA((n,))]


def stage_gather_chips(xs):
    def copies(ins, outs, sems):
        send_sems, recv_sems = sems
        mx, my, mc = _place()
        sibling = (mx, my, 1 - mc)
        chips = [(1 - mx, my), (mx, 1 - my), (1 - mx, 1 - my)]
        first, landed, passed, from_sibling = [], [], [], []
        for i, (x_ref, out_ref) in enumerate(zip(ins, outs)):
            def piece(cx, cy, h, out_ref=out_ref):
                return out_ref.at[2 * cx + cy, h]

            for j, (cx, cy) in enumerate(chips):
                k = 6 * i + j
                first.append(_remote(x_ref.at[mc], piece(mx, my, mc), send_sems, recv_sems, k, (cx, cy, mc)))
                landed.append(_remote(x_ref.at[mc], piece(cx, cy, mc), send_sems, recv_sems, k, (cx, cy, mc)))
                passed.append(_remote(piece(cx, cy, mc), piece(cx, cy, mc), send_sems, recv_sems, k + 3, sibling))
                from_sibling.append(_remote(x_ref.at[mc], piece(cx, cy, 1 - mc), send_sems, recv_sems, k + 3, sibling))
        return first, landed, passed, from_sibling

    def start(ins, outs, sems):
        for cp in copies(ins, outs, sems)[0]:
            cp.start()

    def finish(ins, outs, sems):
        first, landed, passed, from_sibling = copies(ins, outs, sems)
        for arrived, onward in zip(landed, passed):
            arrived.wait_recv()
            onward.start()
        for cp in from_sibling:
            cp.wait_recv()
        for cp in first + passed:
            cp.wait_send()

    return Stage(list(xs), [S((4,) + x.shape, x.dtype) for x in xs], _sem_pair(6 * len(xs)), start, finish)


def own_block(gathered, xs):
    return lax.dynamic_update_slice(gathered, xs[None], (chip_index(), 0, 0, 0))


def stage_pair_exchange(gs):
    def copies(ins, outs, sems):
        send_sems, recv_sems = sems
        mx, my, mc = _place()
        return [_remote(g_ref.at[s, 1 - mc], land_ref.at[s], send_sems, recv_sems, 4 * i + s, (mx, my, 1 - mc))
                for i, (g_ref, land_ref) in enumerate(zip(ins, outs)) for s in range(4)]

    def start(ins, outs, sems):
        for cp in copies(ins, outs, sems):
            cp.start()

    def finish(ins, outs, sems):
        cps = copies(ins, outs, sems)
        for cp in cps:
            cp.wait_recv()
        for cp in cps:
            cp.wait_send()

    return Stage(list(gs), [S((4,) + g.shape[2:], g.dtype) for g in gs], _sem_pair(4 * len(gs)), start, finish)


def _sum_rows(rws, width):
    return _div_tile(rws, max(SUBLANE, SUM_TILE_ELEMS // width), SUBLANE)


def pair_sum(g4, land, c_idx, name):
    _, _, rws, wd = g4.shape
    th = _sum_rows(rws, wd)

    def body(c_ref, a_ref, b_ref, o_ref):
        o_ref[...] = a_ref[...] + b_ref[...]

    return pl.pallas_call(
        body,
        grid_spec=pltpu.PrefetchScalarGridSpec(
            num_scalar_prefetch=1, grid=(4, rws // th),
            in_specs=[pl.BlockSpec((None, None, th, wd), lambda s, i, c: (s, c[0], i, 0)),
                      pl.BlockSpec((None, th, wd), lambda s, i, c: (s, i, 0))],
            out_specs=pl.BlockSpec((None, th, wd), lambda s, i, c: (s, i, 0))),
        out_shape=S((4, rws, wd), F32), compiler_params=_cp(("parallel", "parallel")), name=name)(c_idx, g4, land)


def stage_chip_scatter(ps):
    def copies(ins, outs, sems):
        send_sems, recv_sems = sems
        mx, my, mc = _place()
        me = 2 * mx + my
        chips = [(1 - mx, my), (mx, 1 - my), (1 - mx, 1 - my)]
        sent, landed = [], []
        for i, (p_ref, land_ref) in enumerate(zip(ins, outs)):
            for j, (cx, cy) in enumerate(chips):
                k = 3 * i + j
                sent.append(_remote(p_ref.at[2 * cx + cy], land_ref.at[me], send_sems, recv_sems, k, (cx, cy, mc)))
                landed.append(_remote(p_ref.at[me], land_ref.at[2 * cx + cy], send_sems, recv_sems, k, (cx, cy, mc)))
        return sent, landed

    def start(ins, outs, sems):
        for cp in copies(ins, outs, sems)[0]:
            cp.start()

    def finish(ins, outs, sems):
        sent, landed = copies(ins, outs, sems)
        for cp in landed:
            cp.wait_recv()
        for cp in sent:
            cp.wait_send()

    return Stage(list(ps), [S(p.shape, p.dtype) for p in ps], _sem_pair(3 * len(ps)), start, finish)


def chip_sum(l4, p4, me_idx, name):
    _, rws, wd = l4.shape
    th = _sum_rows(rws, wd)

    def body(me_ref, a, b, c, d, own, o_ref):
        me = me_ref[0]
        parts = [jnp.where(me == s, own[...], r[...]) for s, r in enumerate((a, b, c, d))]
        o_ref[...] = ((parts[0] + parts[1]) + parts[2]) + parts[3]

    def blk(s):
        return pl.BlockSpec((None, th, wd), lambda i, me: (jnp.where(me[0] == s, (s + 1) % 4, s), i, 0))

    return pl.pallas_call(
        body,
        grid_spec=pltpu.PrefetchScalarGridSpec(
            num_scalar_prefetch=1, grid=(rws // th,),
            in_specs=[blk(0), blk(1), blk(2), blk(3), pl.BlockSpec((None, th, wd), lambda i, me: (me[0], i, 0))],
            out_specs=pl.BlockSpec((th, wd), lambda i, me: (i, 0))),
        out_shape=S((rws, wd), F32), compiler_params=_cp(("parallel",)), name=name)(me_idx, l4, l4, l4, l4, p4)


def stage_pair_gather(rs):
    def copies(ins, outs, sems):
        send_sems, recv_sems = sems
        mx, my, mc = _place()
        return [_remote(r_ref, out_ref, send_sems, recv_sems, i, (mx, my, 1 - mc))
                for i, (r_ref, out_ref) in enumerate(zip(ins, outs))]

    def start(ins, outs, sems):
        for cp in copies(ins, outs, sems):
            cp.start()

    def finish(ins, outs, sems):
        for cp in copies(ins, outs, sems):
            cp.wait()

    return Stage(list(rs), [S(r.shape, r.dtype) for r in rs], _sem_pair(len(rs)), start, finish)


PACK_ELEMS = 16 * LANE


def pack_rows(arrays, lead, total_mult):
    parts, offs, r0 = [], [], 0
    for a in arrays:
        flat = a.reshape(a.shape[:lead] + (-1,))
        elems = _round_up(flat.shape[-1], PACK_ELEMS)
        flat = jnp.pad(flat, [(0, 0)] * lead + [(0, elems - flat.shape[-1])])
        parts.append(flat.reshape(flat.shape[:lead] + (elems // LANE, LANE)))
        offs.append((r0, elems // LANE))
        r0 += elems // LANE
    total = _round_up(r0, total_mult)
    if total > r0:
        parts.append(jnp.zeros(parts[0].shape[:lead] + (total - r0, LANE), parts[0].dtype))
    return jnp.concatenate(parts, axis=lead), offs


def unpack_rows(buf, off, shape):
    r0, nr = off
    lead = buf.shape[:-2]
    n = math.prod(shape)
    return buf[..., r0:r0 + nr, :].reshape(lead + (nr * LANE,))[..., :n].reshape(lead + tuple(shape))


def unshard(stacked, axis):
    x = jnp.moveaxis(stacked, 0, axis)
    return x.reshape(x.shape[:axis] + (4 * x.shape[axis + 1],) + x.shape[axis + 2:])


def to_shards(full, axis):
    n = full.shape[axis] // 4
    x = full.reshape(full.shape[:axis] + (4, n) + full.shape[axis + 1:])
    return jnp.moveaxis(x, axis, 0)


def _rot_cols(w):
    half = w.shape[-1] // 2
    return jnp.concatenate([-w[..., half:], w[..., :half]], axis=-1)


def _unrot_cols(dw):
    half = dw.shape[-1] // 2
    return jnp.concatenate([dw[..., half:], -dw[..., :half]], axis=-1)


def odd_w_in_padded(w_in):
    kr = w_in[:, Q_LORA + KV_LORA:]
    rows = w_in.shape[0]
    return jnp.concatenate([w_in[:, :Q_LORA], jnp.zeros((rows, 128), w_in.dtype), w_in[:, Q_LORA:Q_LORA + KV_LORA],
                            jnp.zeros((rows, 64), w_in.dtype), kr, _rot_cols(kr)], axis=1)


def odd_w_in_unpad(dwp):
    base = 512 + KV_LORA + 64
    dkr = dwp[:, base:base + QK_ROPE] + _unrot_cols(dwp[:, base + QK_ROPE:base + 2 * QK_ROPE])
    return jnp.concatenate([dwp[:, :Q_LORA], dwp[:, 512:512 + KV_LORA], dkr], axis=1)


def uq_padded(w_uq):
    w = w_uq.reshape(Q_LORA, MLA_HEADS, QK_HEAD)
    return jnp.concatenate([w, _rot_cols(w[:, :, QK_NOPE:])], axis=-1).reshape(Q_LORA, MLA_HEADS * HP)


def uq_unpad(dwp):
    d = dwp.reshape(Q_LORA, MLA_HEADS, HP)
    rope = d[:, :, QK_NOPE:QK_HEAD] + _unrot_cols(d[:, :, QK_HEAD:])
    return jnp.concatenate([d[:, :, :QK_NOPE], rope], axis=-1).reshape(Q_LORA, MLA_HEADS * QK_HEAD)


def ukv_padded(w_ukv):
    w = w_ukv.reshape(KV_LORA, MLA_HEADS, QK_NOPE + V_HEAD)
    wk = jnp.concatenate([w[:, :, :QK_NOPE], jnp.zeros((KV_LORA, MLA_HEADS, HP - QK_NOPE), w.dtype)], axis=-1)
    return jnp.concatenate([wk.reshape(KV_LORA, MLA_HEADS * HP), w[:, :, QK_NOPE:].reshape(KV_LORA, MLA_HEADS * V_HEAD)],
                           axis=1)


def ukv_unpad(dwp):
    dk = dwp[:, :MLA_HEADS * HP].reshape(KV_LORA, MLA_HEADS, HP)[:, :, :QK_NOPE]
    dv = dwp[:, MLA_HEADS * HP:].reshape(KV_LORA, MLA_HEADS, V_HEAD)
    return jnp.concatenate([dk, dv], axis=-1).reshape(KV_LORA, MLA_HEADS * (QK_NOPE + V_HEAD))


def block_diag(w):
    h, d, _ = w.shape
    eye = jnp.eye(h, dtype=w.dtype)
    return (eye[:, None, :, None] * w[:, :, None, :]).reshape(h * d, h * d)


def block_diag_part(dense, h):
    d = dense.shape[0] // h
    x = dense.reshape(h, d, h, d)
    return jnp.stack([x[i, :, i, :] for i in range(h)], axis=0)


def rope_tables(tp):
    pos = jnp.arange(tp, dtype=F32)
    inv_freq = ROPE_BASE ** (-jnp.arange(0, QK_ROPE, 2, dtype=F32) / QK_ROPE)
    ang = pos[:, None] * inv_freq[None, :]
    cos2 = jnp.tile(jnp.cos(ang), (1, 2))
    sin2 = jnp.tile(jnp.sin(ang), (1, 2))
    tabq = jnp.concatenate([jnp.ones((tp, QK_NOPE), F32), cos2, sin2], axis=1)
    tabk = jnp.concatenate([jnp.zeros((tp, QK_NOPE), F32), cos2, sin2], axis=1)
    return tabq, tabk


class Dims:
    def __init__(self, nb, seq):
        self.nb = nb
        self.t_real = seq + N_META
        self.tp = _round_up(self.t_real, ATT_BLK)
        self.n = self.tp // 4
        assert self.n % 16 == 0
        self.rows = nb * self.tp


class NoComm:
    def advance(self, carried):
        return None


def even_fwd(h, p, dm, comm):
    (u, hn), _ = norm_matmul(h, 0, D_MODEL, p["norm"], p["w_in"], dm.n, EVEN_IN // 2, F32, "ev_in")
    (y, ca, xc, a, hs), got = even_mid_fwd(u, p["conv_a"], p["conv_b"], p["conv_b_bias"], p["rw"], p["r_b"], p["iw"],
                                           p["i_b"], p["lam"], dm.nb, dm.tp, dm.n, "ev_mid", carry=comm.advance(None))
    comm.advance(got)
    out = matmul_res(y, p["w_out"].reshape(2, CONV_W, D_MODEL), h, dm.n, D_MODEL, "ev_out")
    return out, (h, u, hn, ca, xc, a, hs, y)


def even_bwd(dout, saved, p, dm, comm):
    h, u, hn, ca, xc, a, hs, y = saved
    g = {}
    dycat = matmul_nt(dout, p["w_out"], dm.n, D_MODEL, F32, "ev_dycat")
    g["w_out"], got = matmul_tn(y, dout, dm.n, "ev_dw_out", carry=comm.advance(None))
    outs, got = even_mid_bwd(u, dycat, ca, xc, a, hs, p["conv_a"], p["conv_b"], p["rw"], p["r_b"], p["iw"], p["i_b"],
                             p["lam"], dm.nb, dm.tp, dm.n, "ev_mid_bwd", carry=comm.advance(got))
    du, g["conv_a"], g["conv_b"], g["conv_b_bias"], drw, g["r_b"], diw, g["i_b"], g["lam"] = outs
    g["r_w"] = block_diag_part(drw, LRU_HEADS)
    g["i_w"] = block_diag_part(diw, LRU_HEADS)
    g["w_in"], got = matmul_tn(hn, du, dm.n, "ev_dw_in", carry=comm.advance(got))
    comm.advance(got)
    dx, g["norm"] = matmul_nt_normbwd(du, p["w_in"], h, 0, p["norm"], dout, dm.n, 512, F32, "ev_dx")
    return dx, g


def odd_fwd(h, p, tabq, tabk, dm, comm):
    nt = dm.tp // dm.n
    (u, hn), _ = norm_matmul(h, 0, D_MODEL, p["norm"], p["w_in_p"], dm.n, ODD_PAD, F32, "od_in")
    tab_spec = pl.BlockSpec((dm.n, HP), lambda i, j: (i % nt, 0))
    (q, cqn), _ = norm_matmul(u, 0, Q_LORA, p["q_norm"], p["w_uq_p"], dm.n, MLA_HEADS * HP, MXU_DT, "od_q",
                              epi=_q_rope_epi, epi_ops=(tabq,), epi_specs=(tab_spec,))
    kr_spec = pl.BlockSpec((dm.n, HP), lambda i, j: (i, ODD_KR_COL))
    (k, ckvn), _ = norm_matmul(u, ODD_CKV_COL, KV_LORA, p["kv_norm"], p["w_uk_p"], dm.n, MLA_HEADS * HP, MXU_DT, "od_k",
                               epi=_k_rope_epi, epi_ops=(u, tabk), epi_specs=(kr_spec, tab_spec))
    (v, _), _ = norm_matmul(u, ODD_CKV_COL, KV_LORA, p["kv_norm"], p["w_uv_p"], dm.n, MLA_HEADS * V_HEAD, MXU_DT,
                            "od_v")
    (o, lse), got = attn_fwd(q, k, v, dm.nb, dm.tp, "od_attn", carry=comm.advance(None))
    comm.advance(got)
    out = matmul_res(o[None], p["w_out"][None], h, dm.n, D_MODEL, "od_out")
    return out, (h, u, hn, cqn, ckvn, q, k, v, o, lse)


def odd_bwd(dout, saved, p, tabq, tabk, dm, comm):
    h, u, hn, cqn, ckvn, q, k, v, o, lse = saved
    g = {}
    do = matmul_nt(dout, p["w_out"], dm.n, D_MODEL, MXU_DT, "od_do")
    g["w_out"], got = matmul_tn(o, dout, dm.n, "od_dw_out", carry=comm.advance(None))
    (dq, dk, dv), got = attn_bwd(q, k, v, o, do, lse, dm.nb, dm.tp, "od_attn_bwd", carry=comm.advance(got))
    dqa, dkv, dkr = rope_bwd(dq, dk, dv, tabq, tabk, dm.tp, dm.n, "od_rope_bwd")
    g["w_uq_p"], got = matmul_tn(cqn, dqa, dm.n, "od_dw_uq", carry=comm.advance(got))
    comm.advance(got)
    g["w_ukv_p"], _ = matmul_tn(ckvn, dkv, dm.n, "od_dw_ukv")
    dcq, g["q_norm"] = matmul_nt_normbwd(dqa, p["w_uq_p"], u, 0, p["q_norm"], None, dm.n, mat_cols(dqa), MXU_DT,
                                         "od_dcq")
    dckv, g["kv_norm"] = matmul_nt_normbwd(dkv, p["w_ukv_p"], u, ODD_CKV_COL, p["kv_norm"], None, dm.n, mat_cols(dkv),
                                           MXU_DT, "od_dckv")
    du = jnp.concatenate([dcq, jnp.zeros((dm.rows, 128), MXU_DT), dckv, dkr], axis=1)
    g["w_in_p"], _ = matmul_tn(hn, du, dm.n, "od_dw_in")
    dx, g["norm"] = matmul_nt_normbwd(du, p["w_in_p"], h, 0, p["norm"], dout, dm.n, ODD_PAD, F32, "od_dx")
    return dx, g


def ffn_fwd(h, p, dm, comm):
    (up, hn), got = norm_matmul(h, 0, D_MODEL, p["norm"], p["w_up"], dm.n, D_FF // 2, MXU_DT, "ffn_up",
                                carry=comm.advance(None))
    comm.advance(got)
    u, y = ffn_mid_fwd(up, p["cw"], p["cb"], dm.nb, dm.tp, dm.n, "ffn_mid")
    out = matmul_res(y[None], p["w_down"][None], h, dm.n, D_MODEL, "ffn_down")
    return out, (h, up, hn, u, y)


def ffn_bwd(dout, saved, p, dm, comm):
    h, up, hn, u, y = saved
    g = {}
    dy = matmul_nt(dout, p["w_down"], dm.n, D_FF, MXU_DT, "ffn_dy")
    g["w_down"], got = matmul_tn(y, dout, dm.n, "ffn_dw_down", carry=comm.advance(None))
    (dup, g["cw"], g["cb"]), got = ffn_mid_bwd(dy, u, up, p["cw"], dm.nb, dm.tp, dm.n, "ffn_mid_bwd",
                                               carry=comm.advance(got))
    g["w_up"], got = matmul_tn(hn, dup, dm.n, "ffn_dw_up", carry=comm.advance(got), col_shards=4)
    comm.advance(got)
    dx, g["norm"] = matmul_nt_normbwd(dup, p["w_up"], h, 0, p["norm"], dout, dm.n, D_FF // 2, F32, "ffn_dx")
    return dx, g


def _row(v):
    return v.reshape(1, -1)


def even_params(wf, j):
    return dict(norm=_row(wf["ev_norm"][j]), w_in=wf["ev_w_in"], conv_a=wf["ev_conv_a"][j], conv_b=wf["ev_conv_b"][j],
                conv_b_bias=_row(wf["ev_conv_b_bias"][j]), rw=block_diag(wf["ev_gate_r_w"][j]).astype(MXU_DT),
                r_b=_row(wf["ev_gate_r_b"][j]), iw=block_diag(wf["ev_gate_i_w"][j]).astype(MXU_DT),
                i_b=_row(wf["ev_gate_i_b"][j]), lam=_row(wf["ev_lru_lambda"][j]), w_out=wf["ev_w_out"])


def odd_params(wf, j):
    wkv = ukv_padded(wf["od_w_ukv"])
    return dict(norm=_row(wf["od_norm"][j]), w_in_p=odd_w_in_padded(wf["od_w_in"]), q_norm=_row(wf["od_q_norm"][j]),
                kv_norm=_row(wf["od_kv_norm"][j]), w_uq_p=uq_padded(wf["od_w_uq"]), w_ukv_p=wkv,
                w_uk_p=wkv[:, :MLA_HEADS * HP], w_uv_p=wkv[:, MLA_HEADS * HP:], w_out=wf["od_w_out"])


def ffn_params(wf, layer):
    return dict(norm=_row(wf["ffn_norm"][layer]), w_up=wf["ffn_w_up"],
                cw=jnp.moveaxis(wf["ffn_conv_w"][layer].reshape(3, 2, D_FF), 1, 0),
                cb=wf["ffn_conv_b"][layer].reshape(2, 1, D_FF), w_down=wf["ffn_w_down"])


def even_grads(g):
    out = {"ev_" + k_: g[k_] for k_ in ("w_in", "conv_a", "conv_b", "w_out")}
    out.update({"ev_norm": g["norm"][0], "ev_conv_b_bias": g["conv_b_bias"][0], "ev_gate_r_w": g["r_w"],
                "ev_gate_r_b": g["r_b"][0], "ev_gate_i_w": g["i_w"], "ev_gate_i_b": g["i_b"][0],
                "ev_lru_lambda": g["lam"][0]})
    return out


def odd_grads(g):
    return {"od_norm": g["norm"][0], "od_q_norm": g["q_norm"][0], "od_kv_norm": g["kv_norm"][0],
            "od_w_in": odd_w_in_unpad(g["w_in_p"]), "od_w_uq": uq_unpad(g["w_uq_p"]),
            "od_w_ukv": ukv_unpad(g["w_ukv_p"]), "od_w_out": g["w_out"]}


def ffn_grads(g):
    return {"ffn_norm": g["norm"][0], "ffn_w_up": g["w_up"], "ffn_conv_w": jnp.moveaxis(g["cw"], 0, 1).reshape(3, 2 * D_FF),
            "ffn_conv_b": g["cb"].reshape(2 * D_FF), "ffn_w_down": g["w_down"]}


WEIGHTS = ["meta_tokens", "ev_norm", "ev_w_in", "ev_conv_a", "ev_conv_b", "ev_conv_b_bias", "ev_gate_r_w", "ev_gate_r_b",
           "ev_gate_i_w", "ev_gate_i_b", "ev_lru_lambda", "ev_w_out", "od_norm", "od_w_in", "od_q_norm", "od_kv_norm",
           "od_w_uq", "od_w_ukv", "od_w_out", "ffn_norm", "ffn_w_up", "ffn_conv_w", "ffn_conv_b", "ffn_w_down",
           "final_norm"]
SHARD_AXIS = {"meta_tokens": 1, "ev_w_in": 2, "ev_conv_a": 2, "ev_conv_b": 2, "ev_w_out": 1, "od_norm": 1, "od_w_in": 1,
              "od_q_norm": 1, "od_kv_norm": 1, "od_w_uq": 2, "od_w_ukv": 2, "od_w_out": 1, "ffn_w_up": 2,
              "ffn_conv_w": 2, "ffn_w_down": 1}
MATMUL_WEIGHTS = ["ev_w_in", "ev_w_out", "od_w_in", "od_w_uq", "od_w_ukv", "od_w_out", "ffn_w_up", "ffn_w_down"]


LAYER_ORDER = [("ev", 0), ("ffn", 0), ("od", 0), ("ffn", 1), ("ev", 1), ("ffn", 2), ("od", 1), ("ffn", 3)]
LAYER_MATMUL = {"ev": ["ev_w_in", "ev_w_out"], "od": ["od_w_in", "od_w_uq", "od_w_ukv", "od_w_out"],
                "ffn": ["ffn_w_up", "ffn_w_down"]}
LAYER_SHARDED = {"ev": ["ev_w_in", "ev_conv_a", "ev_conv_b", "ev_w_out"],
                 "od": ["od_norm", "od_w_in", "od_q_norm", "od_kv_norm", "od_w_uq", "od_w_ukv", "od_w_out"],
                 "ffn": ["ffn_w_up", "ffn_conv_w", "ffn_w_down"]}
STACKED_SHARDS = "ffn_w_up"


def gather_at_entry(w, names, first, name):
    buf, offs = pack_rows([w[n] for n in names], 0, 32)
    halves = buf.reshape(2, buf.shape[0] // 2, LANE)
    outs = run_stage(stage_gather_chips([halves] + first.halves), name)
    first.step, first.got = 2, outs[1:]
    got = own_block(outs[0], halves).reshape(4, buf.shape[0], LANE)
    return {n: unshard(unpack_rows(got, off, w[n].shape), SHARD_AXIS[n]) for n, off in zip(names, offs)}


def _halves(a):
    return a.reshape(2, a.shape[0] // 2, a.shape[1])


class GatherComm:
    def __init__(self, w, kind, idx):
        self.names = LAYER_MATMUL[kind]
        self.halves = [_halves(w[n][idx].astype(MXU_DT)) for n in self.names]
        self.stage = stage_gather_chips(self.halves)
        self.step, self.got = 0, None

    def advance(self, carried):
        self.step += 1
        if self.step == 1:
            return self.stage
        if self.step == 2:
            self.got = carried
        return None

    def weights(self):
        out = {}
        for n, got, own in zip(self.names, self.got, self.halves):
            stacked = own_block(got, own).reshape(4, 2 * own.shape[1], own.shape[2])
            out[n] = stacked if n == STACKED_SHARDS else unshard(stacked, SHARD_AXIS[n] - 1)
        return out


class ReduceComm:
    def __init__(self, grads, axes, c_idx, tag, tail=None):
        shards = {n: grads[n] if n == STACKED_SHARDS else to_shards(grads[n], axes[n]) for n in grads}
        self.big = [n for n in grads if n in MATMUL_WEIGHTS]
        self.small = [n for n in grads if n not in MATMUL_WEIGHTS]
        self.shapes = {n: shards[n].shape[1:] for n in grads}
        arrays = [shards[n].reshape(4, 2, shards[n].shape[1] // 2, shards[n].shape[2]) for n in self.big]
        gs, self.offs = pack_rows([shards[n] for n in self.small], 1, 16)
        self.rs = gs.shape[1] // 2
        parts = [gs.reshape(4, 2, self.rs, LANE)]
        self.rr = 0
        if tail is not None:
            self.rr = tail.shape[0] // 8
            parts.append(tail.reshape(4, 2, self.rr, LANE))
        arrays.append(jnp.concatenate(parts, axis=2) if len(parts) > 1 else parts[0])
        self.arrays, self.c_idx, self.tag, self.step = arrays, c_idx, tag, 0
        self.part = self.mine = self.theirs = None

    def advance(self, carried):
        self.step += 1
        if self.step == 1:
            return stage_pair_exchange(self.arrays)
        if self.step == 2:
            self.part = [pair_sum(g, land, self.c_idx, "grad_pair_sum_%s_%d" % (self.tag, i))
                         for i, (g, land) in enumerate(zip(self.arrays, carried))]
            return stage_chip_scatter(self.part)
        if self.step == 3:
            me_idx = chip_index().astype(jnp.int32).reshape(1)
            self.mine = [chip_sum(land, part, me_idx, "grad_chip_sum_%s_%d" % (self.tag, i))
                         for i, (land, part) in enumerate(zip(carried, self.part))]
            return stage_pair_gather(self.mine)
        if self.step == 4:
            self.theirs = carried
        return None

    def run_alone(self, name):
        stage = self.advance(None)
        while stage is not None:
            stage = self.advance(run_stage(stage, name + "_%d" % self.step))

    def results(self):
        south = self.c_idx[0] == 0
        boths = [jnp.stack([jnp.where(south, m, t), jnp.where(south, t, m)], axis=0)
                 for m, t in zip(self.mine, self.theirs)]
        out = {n: b.reshape(self.shapes[n]) for n, b in zip(self.big, boths)}
        packed = boths[-1]
        flat = packed[:, :self.rs].reshape(2 * self.rs, LANE)
        out.update({n: unpack_rows(flat, off, self.shapes[n]) for n, off in zip(self.small, self.offs)})
        return out, packed[:, self.rs:self.rs + self.rr]


def kernel(x, meta_tokens, ev_norm, ev_w_in, ev_conv_a, ev_conv_b, ev_conv_b_bias, ev_gate_r_w, ev_gate_r_b, ev_gate_i_w, ev_gate_i_b, ev_lru_lambda, ev_w_out, od_norm, od_w_in, od_q_norm, od_kv_norm, od_w_uq, od_w_ukv, od_w_out, ffn_norm, ffn_w_up, ffn_conv_w, ffn_conv_b, ffn_w_down, final_norm, loss_target, m_meta_tokens, m_ev_norm, m_ev_w_in, m_ev_conv_a, m_ev_conv_b, m_ev_conv_b_bias, m_ev_gate_r_w, m_ev_gate_r_b, m_ev_gate_i_w, m_ev_gate_i_b, m_ev_lru_lambda, m_ev_w_out, m_od_norm, m_od_w_in, m_od_q_norm, m_od_kv_norm, m_od_w_uq, m_od_w_ukv, m_od_w_out, m_ffn_norm, m_ffn_w_up, m_ffn_conv_w, m_ffn_conv_b, m_ffn_w_down, m_final_norm, v_meta_tokens, v_ev_norm, v_ev_w_in, v_ev_conv_a, v_ev_conv_b, v_ev_conv_b_bias, v_ev_gate_r_w, v_ev_gate_r_b, v_ev_gate_i_w, v_ev_gate_i_b, v_ev_lru_lambda, v_ev_w_out, v_od_norm, v_od_w_in, v_od_q_norm, v_od_kv_norm, v_od_w_uq, v_od_w_ukv, v_od_w_out, v_ffn_norm, v_ffn_w_up, v_ffn_conv_w, v_ffn_conv_b, v_ffn_w_down, v_final_norm):
    given = dict(locals())
    w = {n: given[n] for n in WEIGHTS}
    nb, seq, _ = x.shape
    dm = Dims(nb, seq)
    n_layers = len(LAYER_ORDER)

    wf = {n: w[n] for n in WEIGHTS if n not in SHARD_AXIS}
    gathers = [GatherComm(w, kind, idx) for kind, idx in LAYER_ORDER]
    wf.update(gather_at_entry(w, [n for n in SHARD_AXIS if n not in MATMUL_WEIGHTS], gathers[0], "gather_at_entry"))

    tail = dm.tp - dm.t_real
    meta = jnp.broadcast_to(wf["meta_tokens"][None], (nb, N_META, D_MODEL))
    h = jnp.concatenate([meta, x, jnp.zeros((nb, tail, D_MODEL), F32)], axis=1).reshape(dm.rows, D_MODEL)
    tgt = jnp.pad(loss_target, ((0, 0), (N_META, tail), (0, 0))).reshape(dm.rows, D_MODEL)
    tabq, tabk = rope_tables(dm.tp)

    params, saved = [], []
    for i, (kind, idx) in enumerate(LAYER_ORDER):
        wl = dict(wf)
        wl.update(gathers[i].weights())
        comm = gathers[i + 1] if i + 1 < n_layers else NoComm()
        if kind == "ev":
            p = even_params(wl, idx)
            h, sv = even_fwd(h, p, dm, comm)
        elif kind == "od":
            p = odd_params(wl, idx)
            h, sv = odd_fwd(h, p, tabq, tabk, dm, comm)
        else:
            p = ffn_params(wl, idx)
            h, sv = ffn_fwd(h, p, dm, comm)
        params.append(p)
        saved.append(sv)

    dh, loss, dfinal = loss_head(h, tgt, _row(wf["final_norm"]), dm.tp, dm.t_real, dm.n, "loss_head")
    loss = lax.psum(loss[0, 0], ("x", "y", "c"))

    c_idx = lax.axis_index("c").astype(jnp.int32).reshape(1)
    layer_grads = {n: {} for n in WEIGHTS}
    pending, reduces = NoComm(), []
    for i in reversed(range(n_layers)):
        kind, idx = LAYER_ORDER[i]
        if kind == "ev":
            dh, g = even_bwd(dh, saved[i], params[i], dm, pending)
            g = even_grads(g)
        elif kind == "od":
            dh, g = odd_bwd(dh, saved[i], params[i], tabq, tabk, dm, pending)
            g = odd_grads(g)
        else:
            dh, g = ffn_bwd(dh, saved[i], params[i], dm, pending)
            g = ffn_grads(g)
        for n in g:
            if n not in SHARD_AXIS:
                layer_grads[n][idx] = g[n]
        if i > 0:
            pending = ReduceComm({n: g[n] for n in LAYER_SHARDED[kind]}, {n: SHARD_AXIS[n] - 1 for n in SHARD_AXIS},
                                 c_idx, "%s%d" % (kind, idx))
            reduces.append((pending, idx))
    dh3 = dh.reshape(nb, dm.tp, D_MODEL)
    grad_x = dh3[:, N_META:dm.t_real]

    repl = [n for n in WEIGHTS if n not in SHARD_AXIS]
    layer_grads["final_norm"] = {0: dfinal[0]}
    repl_full = {n: (layer_grads[n][0] if n == "final_norm" else
                     jnp.stack([layer_grads[n][j] for j in range(w[n].shape[0])], axis=0)) for n in repl}
    tail_buf, tail_offs = pack_rows([repl_full[n] for n in repl], 0, 64)
    first = {n: g[n] for n in LAYER_SHARDED["ev"]}
    first["meta_tokens"] = jnp.sum(dh3[:, :N_META], axis=0)
    axes = {n: SHARD_AXIS[n] - 1 for n in SHARD_AXIS}
    axes["meta_tokens"] = SHARD_AXIS["meta_tokens"]
    last = ReduceComm(first, axes, c_idx, "first_layer", tail=tail_buf)
    last.run_alone("grad_first_layer")
    reduces.append((last, 0))

    red = {}
    for comm, idx in reduces:
        got, tail_piece = comm.results()
        for n, v_ in got.items():
            if n == "meta_tokens":
                red[n] = v_
            else:
                layer_grads[n][idx] = v_
    tails = own_block(run_stage(stage_gather_chips([tail_piece]), "grad_gather_replicated")[0], tail_piece)
    tails = tails.reshape(tail_buf.shape[0], LANE)
    for n, off in zip(repl, tail_offs):
        red[n] = unpack_rows(tails, off, w[n].shape)
    for n in SHARD_AXIS:
        if n != "meta_tokens":
            red[n] = jnp.stack([layer_grads[n][j] for j in range(w[n].shape[0])], axis=0)

    outs = [adamw(red[n], w[n], given["m_" + n], given["v_" + n], "adamw_" + n) for n in WEIGHTS]
    return (loss, grad_x, *[red[n] for n in WEIGHTS], *[o[0] for o in outs], *[o[1] for o in outs],
            *[o[2] for o in outs])
```

```python
import math

import jax
import jax.numpy as jnp
from jax import lax
from jax.experimental import pallas as pl
from jax.experimental.pallas import tpu as pltpu

F32 = jnp.float32
MXU_DT = jnp.bfloat16
S = jax.ShapeDtypeStruct
MESH = pl.DeviceIdType.MESH

EPS = 1e-6
D_MODEL = 1024
N_META = 16
DEPTH = 4
CONV_W = 512
LRU_W = 512
LRU_HEADS = 8
LRU_C = 8.0
EVEN_IN = 2560
MLA_HEADS = 16
QK_NOPE = 64
QK_ROPE = 32
QK_HEAD = 96
V_HEAD = 64
Q_LORA = 384
KV_LORA = 256
ROPE_BASE = 10000.0
D_FF = 2816
ODD_PAD = 896
ODD_CKV_COL = 2
ODD_KR_COL = 6
HP = 128
ATT_BLK = 384
Q_PRESCALE = QK_HEAD ** -0.5 * math.log2(math.e)
FFN_CT = 256
EV_CT = 256
STRIP_ROWS = 352
LANE = 128
SUBLANE = 8
VMEM_LIMIT_MB = 52

ADAM_LR = 0.001
ADAM_B1 = 0.9
ADAM_B2 = 0.999
ADAM_EPS = 1e-08
ADAM_WD = 0.01
ADAM_STEP = 10

TN_ACC_ELEMS = 1536 * 1024
NT_DIMS = (((1,), (1,)), ((), ()))
TN_DIMS = (((0,), (0,)), ((), ()))


def _cp(sem):
    return pltpu.CompilerParams(dimension_semantics=sem, vmem_limit_bytes=VMEM_LIMIT_MB << 20)


def _div_tile(n, cap, mult):
    if n <= cap:
        return n
    best = None
    for t in range(mult, cap + 1, mult):
        if n % t == 0:
            best = t
    assert best is not None, (n, cap, mult)
    return best


def _round_up(n, m):
    return -(-n // m) * m


def mat_cols(arr):
    return arr.shape[1] if arr.ndim == 2 else arr.shape[0] * arr.shape[2]


def mat_width(arr):
    return arr.shape[-1]


def mat_spec(arr, tm, tw, rc):
    if arr.ndim == 2:
        return pl.BlockSpec((tm, tw), lambda *g: rc(*g))
    per = arr.shape[2] // tw
    assert arr.shape[2] % tw == 0

    def imap(*g):
        r, c = rc(*g)
        return (c // per, r, c % per)

    return pl.BlockSpec((None, tm, tw), imap)


HBM_SPEC = pl.BlockSpec(memory_space=pltpu.HBM)


class Stage:
    def __init__(self, inputs, out_shapes, sems, start, finish):
        self.inputs, self.out_shapes, self.sems, self.start, self.finish = inputs, out_shapes, sems, start, finish


def run_stage(stage, name):
    n_in, n_out = len(stage.inputs), len(stage.out_shapes)

    def body(*refs):
        ins, outs, sems = refs[:n_in], refs[n_in:n_in + n_out], refs[n_in + n_out:]
        stage.start(ins, outs, sems)
        stage.finish(ins, outs, sems)

    return pl.pallas_call(body, out_shape=list(stage.out_shapes), in_specs=[HBM_SPEC] * n_in,
                          out_specs=[HBM_SPEC] * n_out, scratch_shapes=list(stage.sems), name=name)(*stage.inputs)


def _call(body, ops, carry, *, grid, in_specs, out_specs, out_shape, scratch_shapes, sem, name):
    if carry is None:
        outs = pl.pallas_call(body, grid=grid, in_specs=in_specs, out_specs=out_specs, out_shape=out_shape,
                              scratch_shapes=scratch_shapes, compiler_params=_cp(sem), name=name)(*ops)
        return outs, None
    multi = isinstance(out_shape, (list, tuple))
    shapes = list(out_shape) if multi else [out_shape]
    ospecs = list(out_specs) if multi else [out_specs]
    n_in, n_out, n_sc = len(ops), len(shapes), len(scratch_shapes)
    c_in, c_out = len(carry.inputs), len(carry.out_shapes)

    def wrapped(*refs):
        ins, cin = refs[:n_in], refs[n_in:n_in + c_in]
        o0 = n_in + c_in
        outs, cout = refs[o0:o0 + n_out], refs[o0 + n_out:o0 + n_out + c_out]
        s0 = o0 + n_out + c_out
        scs, csems = refs[s0:s0 + n_sc], refs[s0 + n_sc:]
        first = pl.program_id(0) == 0
        last = pl.program_id(0) == grid[0] - 1
        for d in range(1, len(grid)):
            first = first & (pl.program_id(d) == 0)
            last = last & (pl.program_id(d) == grid[d] - 1)

        @pl.when(first)
        def _():
            carry.start(cin, cout, csems)

        body(*ins, *outs, *scs)

        @pl.when(last)
        def _():
            carry.finish(cin, cout, csems)

    res = pl.pallas_call(
        wrapped, grid=grid, in_specs=list(in_specs) + [HBM_SPEC] * c_in, out_specs=ospecs + [HBM_SPEC] * c_out,
        out_shape=shapes + list(carry.out_shapes), scratch_shapes=list(scratch_shapes) + list(carry.sems),
        compiler_params=_cp(("arbitrary",) * len(grid)), name=name)(*ops, *carry.inputs)
    main = res[:n_out]
    return (list(main) if multi else main[0]), list(res[n_out:])


def norm_matmul(x, xcol, kdim, gain, w, tm, tn, out_dtype, name, epi=None, epi_ops=(), epi_specs=(), carry=None):
    rows, n = x.shape[0], mat_cols(w) if w.ndim == 3 else w.shape[1]
    n_epi = len(epi_ops)
    w_spec = (pl.BlockSpec((kdim, tn), lambda i, j: (0, j)) if w.ndim == 2 else
              pl.BlockSpec((None, kdim, tn), lambda i, j: (j // (w.shape[2] // tn), 0, j % (w.shape[2] // tn))))

    def body(x_ref, g_ref, w_ref, *rest):
        epi_refs = rest[:n_epi]
        out_ref, xn_ref, xn_sc = rest[n_epi:]

        @pl.when(pl.program_id(1) == 0)
        def _():
            xv = x_ref[...]
            y = xv * lax.rsqrt(jnp.mean(xv * xv, axis=-1, keepdims=True) + EPS)
            xn = (y * g_ref[...]).astype(MXU_DT)
            xn_sc[...] = xn
            xn_ref[...] = xn

        acc = jnp.dot(xn_sc[...], w_ref[...], preferred_element_type=F32)
        if epi is not None:
            acc = epi(acc, *[r[...] for r in epi_refs])
        out_ref[...] = acc.astype(out_dtype)

    return _call(
        body, (x, gain, w, *epi_ops), carry, grid=(rows // tm, n // tn),
        in_specs=[pl.BlockSpec((tm, kdim), lambda i, j: (i, xcol)), pl.BlockSpec((1, kdim), lambda i, j: (0, 0)),
                  w_spec, *epi_specs],
        out_specs=[pl.BlockSpec((tm, tn), lambda i, j: (i, j)), pl.BlockSpec((tm, kdim), lambda i, j: (i, 0))],
        out_shape=[S((rows, n), out_dtype), S((rows, kdim), MXU_DT)],
        scratch_shapes=[pltpu.VMEM((tm, kdim), MXU_DT)], sem=("parallel", "arbitrary"), name=name)


def matmul_res(a, w, res, tm, tn, name):
    grp, rows, k = a.shape
    n = w.shape[2]

    def body(a_ref, w_ref, r_ref, o_ref):
        acc = r_ref[...]
        for g in range(grp):
            acc = acc + jnp.dot(a_ref[g], w_ref[g], preferred_element_type=F32)
        o_ref[...] = acc

    return pl.pallas_call(
        body, grid=(rows // tm, n // tn),
        in_specs=[pl.BlockSpec((grp, tm, k), lambda i, j: (0, i, 0)), pl.BlockSpec((grp, k, tn), lambda i, j: (0, 0, j)),
                  pl.BlockSpec((tm, tn), lambda i, j: (i, j))],
        out_specs=pl.BlockSpec((tm, tn), lambda i, j: (i, j)),
        out_shape=S((rows, n), F32), compiler_params=_cp(("parallel", "parallel")), name=name)(a, w, res)


def matmul_nt(a, w, tm, tn, out_dtype, name):
    rows, k = a.shape
    n = w.shape[0]

    def body(a_ref, w_ref, o_ref):
        o_ref[...] = lax.dot_general(a_ref[...].astype(MXU_DT), w_ref[...], NT_DIMS,
                                     preferred_element_type=F32).astype(out_dtype)

    return pl.pallas_call(
        body, grid=(rows // tm, n // tn),
        in_specs=[pl.BlockSpec((tm, k), lambda i, j: (i, 0)), pl.BlockSpec((tn, k), lambda i, j: (j, 0))],
        out_specs=pl.BlockSpec((tm, tn), lambda i, j: (i, j)),
        out_shape=S((rows, n), out_dtype), compiler_params=_cp(("parallel", "parallel")), name=name)(a, w)


def matmul_nt_normbwd(du, w, x, xcol, gain, res, tm, tk, out_dtype, name):
    rows, kc = du.shape[-2], mat_cols(du)
    dn = w.shape[-2]
    nk = kc // tk
    has_res = res is not None
    w_spec = (pl.BlockSpec((dn, tk), lambda i, k: (0, k)) if w.ndim == 2 else
              pl.BlockSpec((None, dn, tk), lambda i, k: (k // (w.shape[2] // tk), 0, k % (w.shape[2] // tk))))

    def body(du_ref, w_ref, x_ref, g_ref, *rest):
        if has_res:
            res_ref, dx_ref, dg_ref, acc = rest
        else:
            dx_ref, dg_ref, acc = rest
        i, k = pl.program_id(0), pl.program_id(1)

        @pl.when(k == 0)
        def _():
            acc[...] = jnp.zeros_like(acc)

        @pl.when((i == 0) & (k == 0))
        def _():
            dg_ref[...] = jnp.zeros_like(dg_ref)

        acc[...] += lax.dot_general(du_ref[...], w_ref[...], NT_DIMS, preferred_element_type=F32)

        @pl.when(k == nk - 1)
        def _():
            dhn = acc[...]
            xv = x_ref[...]
            rstd = lax.rsqrt(jnp.mean(xv * xv, axis=-1, keepdims=True) + EPS)
            xhat = xv * rstd
            dg_ref[...] += jnp.sum(dhn * xhat, axis=0, keepdims=True)
            dxh = dhn * g_ref[...]
            dx = rstd * (dxh - xhat * jnp.mean(dxh * xhat, axis=-1, keepdims=True))
            if has_res:
                dx = dx + res_ref[...]
            dx_ref[...] = dx.astype(out_dtype)

    in_specs = [mat_spec(du, tm, tk, lambda i, k: (i, k)), w_spec,
                pl.BlockSpec((tm, dn), lambda i, k: (i, xcol)), pl.BlockSpec((1, dn), lambda i, k: (0, 0))]
    ops = [du, w, x, gain]
    if has_res:
        in_specs.append(pl.BlockSpec((tm, dn), lambda i, k: (i, 0)))
        ops.append(res)
    return pl.pallas_call(
        body, grid=(rows // tm, nk), in_specs=in_specs,
        out_specs=[pl.BlockSpec((tm, dn), lambda i, k: (i, 0)), pl.BlockSpec((1, dn), lambda i, k: (0, 0))],
        out_shape=[S((rows, dn), out_dtype), S((1, dn), F32)],
        scratch_shapes=[pltpu.VMEM((tm, dn), F32)],
        compiler_params=_cp(("arbitrary", "arbitrary")), name=name)(*ops)


def matmul_tn(a, b, tr, name, carry=None, col_shards=1):
    rows, ka, nb = a.shape[-2], mat_cols(a), mat_cols(b)
    ta = _div_tile(mat_width(a), 1536, LANE)
    tb = _div_tile(mat_width(b), max(LANE, TN_ACC_ELEMS // ta // LANE * LANE), LANE)
    nr = rows // tr
    if col_shards == 1:
        out_spec, out_shape = pl.BlockSpec((ta, tb), lambda i, j, r: (i, j)), S((ka, nb), F32)
    else:
        per = nb // col_shards // tb
        assert per * tb * col_shards == nb
        out_spec = pl.BlockSpec((None, ta, tb), lambda i, j, r: (j // per, i, j % per))
        out_shape = S((col_shards, ka, nb // col_shards), F32)

    def body(a_ref, b_ref, o_ref, acc):
        r = pl.program_id(2)

        @pl.when(r == 0)
        def _():
            acc[...] = jnp.zeros_like(acc)

        acc[...] += lax.dot_general(a_ref[...].astype(MXU_DT), b_ref[...].astype(MXU_DT), TN_DIMS,
                                    preferred_element_type=F32)

        @pl.when(r == nr - 1)
        def _():
            o_ref[...] = acc[...]

    return _call(
        body, (a, b), carry, grid=(ka // ta, nb // tb, nr),
        in_specs=[mat_spec(a, tr, ta, lambda i, j, r: (r, i)), mat_spec(b, tr, tb, lambda i, j, r: (r, j))],
        out_specs=out_spec, out_shape=out_shape, scratch_shapes=[pltpu.VMEM((ta, tb), F32)],
        sem=("parallel", "parallel", "arbitrary"), name=name)


def _sigmoid(x):
    return 1.0 / (1.0 + jnp.exp(-x))


def _sigmoid_by_tanh(x):
    return 0.5 * jnp.tanh(0.5 * x) + 0.5


def _log1p(e):
    return jnp.where(e < 1e-3, e * (1.0 - e * (0.5 - e * (1.0 / 3.0 - 0.25 * e))), jnp.log(1.0 + e))


def _softplus(x):
    return jnp.maximum(x, 0.0) + _log1p(jnp.exp(-jnp.abs(x)))


def _expm1(x):
    series = x * (1.0 + x * (0.5 + x * (1.0 / 6.0 + x * (1.0 / 24.0 + x * (1.0 / 120.0)))))
    return jnp.where(jnp.abs(x) < 0.1, series, jnp.exp(x) - 1.0)


_GELU_K = math.sqrt(2.0 / math.pi)
_GELU_C = 0.044715


def _gelu_and_grad(x):
    th = jnp.tanh(_GELU_K * (x + _GELU_C * x * x * x))
    g = 0.5 * x * (1.0 + th)
    dg = 0.5 * (1.0 + th) + 0.5 * x * (1.0 - th * th) * _GELU_K * (1.0 + 3.0 * _GELU_C * x * x)
    return g, dg


def _row_iota(shape):
    return lax.broadcasted_iota(jnp.int32, shape, 0)


def _scan_chunk_fwd(a_sc, u_sc, out_ref, hcar, n, width):
    rowi = _row_iota((SUBLANE, width))

    def step(c, hprev):
        r0 = pl.multiple_of(c * SUBLANE, SUBLANE)
        a = a_sc[pl.ds(r0, SUBLANE), :]
        u = u_sc[pl.ds(r0, SUBLANE), :]
        for d in (1, 2, 4):
            a_s = jnp.where(rowi >= d, pltpu.roll(a, d, axis=0), 1.0)
            u_s = jnp.where(rowi >= d, pltpu.roll(u, d, axis=0), 0.0)
            u = u + a * u_s
            a = a * a_s
        h = u + a * hprev
        out_ref[pl.ds(r0, SUBLANE), :] = h
        return jnp.broadcast_to(h[SUBLANE - 1:SUBLANE, :], (SUBLANE, width))

    hcar[...] = lax.fori_loop(0, n // SUBLANE, step, hcar[...], unroll=4)


def _scan_chunk_bwd(b_sc, d_sc, out_ref, gcar, n, width):
    rowi = _row_iota((SUBLANE, width))
    nc = n // SUBLANE

    def step(c, gnext):
        r0 = pl.multiple_of((nc - 1 - c) * SUBLANE, SUBLANE)
        b = b_sc[pl.ds(r0, SUBLANE), :]
        d = d_sc[pl.ds(r0, SUBLANE), :]
        for s in (1, 2, 4):
            keep = rowi < SUBLANE - s
            b_s = jnp.where(keep, pltpu.roll(b, SUBLANE - s, axis=0), 1.0)
            d_s = jnp.where(keep, pltpu.roll(d, SUBLANE - s, axis=0), 0.0)
            d = d + b * d_s
            b = b * b_s
        g = d + b * gnext
        out_ref[pl.ds(r0, SUBLANE), :] = g
        return jnp.broadcast_to(g[0:1, :], (SUBLANE, width))

    gcar[...] = lax.fori_loop(0, nc, step, gcar[...], unroll=4)


def even_mid_fwd(u, conv_a, conv_b, conv_b_bias, rw, rb, iw, ib, lam, nb, tp, n, name, carry=None):
    rows = u.shape[0]
    w = EV_CT
    nj = CONV_W // w
    nt = tp // n
    h8 = SUBLANE

    def body(gb_r, gc_r, xa_r, xb_r, gate_r, ca_w, cb_w, cb_b, rw_r, rb_r, iw_r, ib_r, lam_r,
             y_o, ca_o, xc_o, a_o, hs_o, pext, xext, hcar, a_sc, u_sc):
        t = pl.program_id(2)

        @pl.when(t == 0)
        def _():
            pext[0:h8, :] = jnp.zeros((h8, w), F32)
            xext[0:h8, :] = jnp.zeros((h8, w), F32)
            hcar[...] = jnp.zeros_like(hcar)

        p = gc_r[...] * xa_r[...]
        pext[h8:h8 + n, :] = p
        wa = ca_w[...]
        ca = wa[2:3, :] * p + wa[1:2, :] * pext[h8 - 1:h8 - 1 + n, :] + wa[0:1, :] * pext[h8 - 2:h8 - 2 + n, :]
        ca_o[...] = ca
        y_o[0] = (gb_r[...] * ca).astype(MXU_DT)
        pext[0:h8, :] = pext[n:n + h8, :]

        xb = xb_r[...]
        xext[h8:h8 + n, :] = xb
        wb = cb_w[...]
        xc = (wb[3:4, :] * xb + wb[2:3, :] * xext[h8 - 1:h8 - 1 + n, :] + wb[1:2, :] * xext[h8 - 2:h8 - 2 + n, :]
              + wb[0:1, :] * xext[h8 - 3:h8 - 3 + n, :]) + cb_b[...]
        xc_o[...] = xc
        xext[0:h8, :] = xext[n:n + h8, :]

        xcm = xc.astype(MXU_DT)
        r = _sigmoid(jnp.dot(xcm, rw_r[...], preferred_element_type=F32) + rb_r[...])
        ig = _sigmoid(jnp.dot(xcm, iw_r[...], preferred_element_type=F32) + ib_r[...])
        log_a = (-LRU_C) * r * _softplus(-lam_r[...])
        a = jnp.exp(log_a)
        mult = jnp.sqrt(-_expm1(2.0 * log_a))
        a_sc[...] = a
        a_o[...] = a
        u_sc[...] = mult * (ig * xc)
        _scan_chunk_fwd(a_sc, u_sc, hs_o, hcar, n, w)
        gel, _ = _gelu_and_grad(gate_r[...])
        y_o[1] = (gel * hs_o[...]).astype(MXU_DT)

    def ublk(off):
        return pl.BlockSpec((n, w), lambda j, b, t: (b * nt + t, off + j))

    def pblk(r_):
        return pl.BlockSpec((r_, w), lambda j, b, t: (0, j))

    act = pl.BlockSpec((n, w), lambda j, b, t: (b * nt + t, j))
    mat = pl.BlockSpec((w, w), lambda j, b, t: (j, j))
    return _call(
        body, (u, u, u, u, u, conv_a, conv_b, conv_b_bias, rw, rb, iw, ib, lam), carry, grid=(nj, nb, nt),
        in_specs=[ublk(0), ublk(nj), ublk(2 * nj), ublk(3 * nj), ublk(4 * nj), pblk(3), pblk(4), pblk(1),
                  mat, pblk(1), mat, pblk(1), pblk(1)],
        out_specs=[pl.BlockSpec((2, n, w), lambda j, b, t: (0, b * nt + t, j)), act, act, act, act],
        out_shape=[S((2, rows, CONV_W), MXU_DT), S((rows, CONV_W), F32), S((rows, LRU_W), F32), S((rows, LRU_W), F32),
                   S((rows, LRU_W), F32)],
        scratch_shapes=[pltpu.VMEM((n + h8, w), F32), pltpu.VMEM((n + h8, w), F32), pltpu.VMEM((h8, w), F32),
                        pltpu.VMEM((n, w), F32), pltpu.VMEM((n, w), F32)],
        sem=("parallel", "parallel", "arbitrary"), name=name)


def even_mid_bwd(u, dycat, ca, xc, a_sv, hs, conv_a, conv_b, rw, rb, iw, ib, lam, nb, tp, n, name, carry=None):
    rows = u.shape[0]
    w = EV_CT
    nj = CONV_W // w
    nt = tp // n
    h8 = SUBLANE

    def body(gb_r, gc_r, xa_r, xb_r, gate_r, dya_r, dyb_r, ca_r, xc_r, a_r, hs_r, hsp_r,
             ca_w, cb_w, rw_r, rb_r, iw_r, ib_r, lam_r,
             du_o, dca_w, dcb_w, dcb_b, drw, drb, diw, dib, dlam,
             aext, hext, dext, eext, gcar, b_sc, d_sc, g_sc):
        b, t = pl.program_id(1), pl.program_id(2)

        @pl.when((b == 0) & (t == 0))
        def _():
            for ref in (dca_w, dcb_w, dcb_b, drw, drb, diw, dib, dlam):
                ref[...] = jnp.zeros_like(ref)

        @pl.when(t == 0)
        def _():
            aext[n:n + h8, :] = jnp.zeros((h8, w), F32)
            dext[n:n + h8, :] = jnp.zeros((h8, w), F32)
            eext[n:n + h8, :] = jnp.zeros((h8, w), F32)
            gcar[...] = jnp.zeros_like(gcar)

        xc_v = xc_r[...]
        xcm = xc_v.astype(MXU_DT)
        r = _sigmoid(jnp.dot(xcm, rw_r[...], preferred_element_type=F32) + rb_r[...])
        ig = _sigmoid(jnp.dot(xcm, iw_r[...], preferred_element_type=F32) + ib_r[...])
        lam_v = lam_r[...]
        sp = _softplus(-lam_v)
        log_a = (-LRU_C) * r * sp
        a = a_r[...]
        mult = jnp.sqrt(-_expm1(2.0 * log_a))
        hs_v = hs_r[...]
        gel, dgel = _gelu_and_grad(gate_r[...])
        dyb = dyb_r[...]
        du_o[4] = (dyb * hs_v * dgel).astype(MXU_DT)

        aext[0:n, :] = a
        b_sc[...] = aext[1:1 + n, :]
        d_sc[...] = dyb * gel
        _scan_chunk_bwd(b_sc, d_sc, g_sc, gcar, n, w)
        aext[n:n + h8, :] = aext[0:h8, :]
        g = g_sc[...]

        hext[0:h8, :] = jnp.where(t == nt - 1, 0.0, hsp_r[...])
        hext[h8:h8 + n, :] = hs_v
        da = g * hext[h8 - 1:h8 - 1 + n, :]
        dmult = g * (ig * xc_v)
        di = g * mult * xc_v
        dxc = g * mult * ig
        dlog_a = da * a - dmult * (a * a) / mult
        dr = dlog_a * ((-LRU_C) * sp)
        dsp = jnp.sum(dlog_a * ((-LRU_C) * r), axis=0, keepdims=True)
        dlam[...] += dsp * (-_sigmoid(-lam_v))
        dzr = dr * r * (1.0 - r)
        dzi = di * ig * (1.0 - ig)
        dzr_m = dzr.astype(MXU_DT)
        dzi_m = dzi.astype(MXU_DT)
        dxc = (dxc + lax.dot_general(dzr_m, rw_r[...], NT_DIMS, preferred_element_type=F32)
               + lax.dot_general(dzi_m, iw_r[...], NT_DIMS, preferred_element_type=F32))
        drw[...] += lax.dot_general(xcm, dzr_m, TN_DIMS, preferred_element_type=F32)
        diw[...] += lax.dot_general(xcm, dzi_m, TN_DIMS, preferred_element_type=F32)
        drb[...] += jnp.sum(dzr, axis=0, keepdims=True)
        dib[...] += jnp.sum(dzi, axis=0, keepdims=True)
        dcb_b[...] += jnp.sum(dxc, axis=0, keepdims=True)

        xb = xb_r[...]
        dext[0:n, :] = dxc
        wb = cb_w[...]
        d1, d2, d3 = dext[1:1 + n, :], dext[2:2 + n, :], dext[3:3 + n, :]
        du_o[3] = (wb[3:4, :] * dxc + wb[2:3, :] * d1 + wb[1:2, :] * d2 + wb[0:1, :] * d3).astype(MXU_DT)
        dcb_w[3:4, :] += jnp.sum(xb * dxc, axis=0, keepdims=True)
        dcb_w[2:3, :] += jnp.sum(xb * d1, axis=0, keepdims=True)
        dcb_w[1:2, :] += jnp.sum(xb * d2, axis=0, keepdims=True)
        dcb_w[0:1, :] += jnp.sum(xb * d3, axis=0, keepdims=True)
        dext[n:n + h8, :] = dext[0:h8, :]

        gb, gc, xa = gb_r[...], gc_r[...], xa_r[...]
        dya = dya_r[...]
        du_o[0] = (dya * ca_r[...]).astype(MXU_DT)
        dca = dya * gb
        eext[0:n, :] = dca
        wa = ca_w[...]
        e1, e2 = eext[1:1 + n, :], eext[2:2 + n, :]
        dp = wa[2:3, :] * dca + wa[1:2, :] * e1 + wa[0:1, :] * e2
        p = gc * xa
        dca_w[2:3, :] += jnp.sum(p * dca, axis=0, keepdims=True)
        dca_w[1:2, :] += jnp.sum(p * e1, axis=0, keepdims=True)
        dca_w[0:1, :] += jnp.sum(p * e2, axis=0, keepdims=True)
        eext[n:n + h8, :] = eext[0:h8, :]
        du_o[1] = (dp * xa).astype(MXU_DT)
        du_o[2] = (dp * gc).astype(MXU_DT)

    def rt(b, t):
        return b * nt + (nt - 1 - t)

    def ublk(off):
        return pl.BlockSpec((n, w), lambda j, b, t: (rt(b, t), off + j))

    def pblk(r_):
        return pl.BlockSpec((r_, w), lambda j, b, t: (0, j))

    act = pl.BlockSpec((n, w), lambda j, b, t: (rt(b, t), j))
    n8 = n // h8
    hsp = pl.BlockSpec((h8, w), lambda j, b, t: (jnp.maximum(rt(b, t) * n8 - 1, 0), j))
    mat = pl.BlockSpec((w, w), lambda j, b, t: (j, j))
    return _call(
        body, (u, u, u, u, u, dycat, dycat, ca, xc, a_sv, hs, hs, conv_a, conv_b, rw, rb, iw, ib, lam), carry,
        grid=(nj, nb, nt),
        in_specs=[ublk(0), ublk(nj), ublk(2 * nj), ublk(3 * nj), ublk(4 * nj), ublk(0), ublk(nj), act, act, act, act,
                  hsp, pblk(3), pblk(4), mat, pblk(1), mat, pblk(1), pblk(1)],
        out_specs=[pl.BlockSpec((5, n, w), lambda j, b, t: (0, rt(b, t), j)), pblk(3), pblk(4), pblk(1),
                   mat, pblk(1), mat, pblk(1), pblk(1)],
        out_shape=[S((5, rows, CONV_W), MXU_DT), S((3, CONV_W), F32), S((4, LRU_W), F32), S((1, LRU_W), F32),
                   S((LRU_W, LRU_W), F32), S((1, LRU_W), F32), S((LRU_W, LRU_W), F32), S((1, LRU_W), F32),
                   S((1, LRU_W), F32)],
        scratch_shapes=[pltpu.VMEM((n + h8, w), F32)] * 4 + [pltpu.VMEM((h8, w), F32)] + [pltpu.VMEM((n, w), F32)] * 3,
        sem=("arbitrary", "arbitrary", "arbitrary"), name=name)


def ffn_mid_fwd(up, cw, cb, nb, tp, n, name):
    rows = up.shape[0]
    w = FFN_CT
    nj = D_FF // w
    nt = tp // n
    h8 = SUBLANE

    sr = STRIP_ROWS
    assert n % sr == 0, (n, sr)

    def body(xa_r, xg_r, w_r, b_r, u_o, y_o, halo):
        t = pl.program_id(2)

        @pl.when(t == 0)
        def _():
            halo[...] = jnp.zeros_like(halo)

        wv = (w_r[0], w_r[1])
        bv = (b_r[0], b_r[1])

        def strip(s, carry):
            r0 = pl.multiple_of(s * sr, sr)
            us, new = [], []
            for g, x_r in enumerate((xa_r, xg_r)):
                x = x_r[pl.ds(r0, sr), :].astype(F32)
                win = jnp.concatenate([carry[g], x], axis=0)
                x1 = pltpu.roll(win, 1, axis=0)[h8:, :]
                x2 = pltpu.roll(win, 2, axis=0)[h8:, :]
                u = (wv[g][2:3, :] * x + wv[g][1:2, :] * x1 + wv[g][0:1, :] * x2) + bv[g]
                u_o[g, pl.ds(r0, sr), :] = u.astype(MXU_DT)
                us.append(u)
                new.append(x[sr - h8:, :])
            y_o[pl.ds(r0, sr), :] = (us[0] * _sigmoid_by_tanh(us[0]) * us[1]).astype(MXU_DT)
            return tuple(new)

        ha, hg = lax.fori_loop(0, n // sr, strip, (halo[0], halo[1]))
        halo[0] = ha
        halo[1] = hg

    def ublk(off):
        return pl.BlockSpec((n, w), lambda j, b, t: (b * nt + t, off + j))

    return pl.pallas_call(
        body, grid=(nj, nb, nt),
        in_specs=[ublk(0), ublk(nj), pl.BlockSpec((2, 3, w), lambda j, b, t: (0, 0, j)),
                  pl.BlockSpec((2, 1, w), lambda j, b, t: (0, 0, j))],
        out_specs=[pl.BlockSpec((2, n, w), lambda j, b, t: (0, b * nt + t, j)), ublk(0)],
        out_shape=[S((2, rows, D_FF), MXU_DT), S((rows, D_FF), MXU_DT)],
        scratch_shapes=[pltpu.VMEM((2, h8, w), F32)],
        compiler_params=_cp(("parallel", "parallel", "arbitrary")), name=name,
    )(up, up, cw, cb)


def ffn_mid_bwd(dy, u, up, cw, nb, tp, n, name, carry=None):
    rows = up.shape[0]
    w = FFN_CT
    nj = D_FF // w
    nt = tp // n
    h8 = SUBLANE

    sr = STRIP_ROWS
    assert n % sr == 0, (n, sr)
    ns = n // sr

    def fold(v):
        acc = v[0:h8, :]
        for k in range(1, sr // h8):
            acc = acc + v[k * h8:(k + 1) * h8, :]
        return acc

    def body(dy_r, u_r, xa_r, xg_r, w_r, dx_o, dw, db, halo):
        b, t = pl.program_id(1), pl.program_id(2)

        @pl.when((b == 0) & (t == 0))
        def _():
            dw[...] = jnp.zeros_like(dw)
            db[...] = jnp.zeros_like(db)

        @pl.when(t == 0)
        def _():
            halo[...] = jnp.zeros_like(halo)

        wv = (w_r[0], w_r[1])

        def strip(s, carry):
            halos, sums = carry
            r0 = pl.multiple_of((ns - 1 - s) * sr, sr)
            dyv = dy_r[pl.ds(r0, sr), :].astype(F32)
            ua = u_r[0, pl.ds(r0, sr), :].astype(F32)
            ug = u_r[1, pl.ds(r0, sr), :].astype(F32)
            sg = _sigmoid_by_tanh(ua)
            dus = (dyv * ug * (sg * (1.0 + ua * (1.0 - sg))), dyv * (ua * sg))
            new_halos, new_sums = [], []
            for g, x_r in enumerate((xa_r, xg_r)):
                du = dus[g]
                win = jnp.concatenate([du, halos[g]], axis=0)
                d1 = pltpu.roll(win, sr + h8 - 1, axis=0)[0:sr, :]
                d2 = pltpu.roll(win, sr + h8 - 2, axis=0)[0:sr, :]
                dx_o[g, pl.ds(r0, sr), :] = (wv[g][2:3, :] * du + wv[g][1:2, :] * d1 + wv[g][0:1, :] * d2).astype(MXU_DT)
                x = x_r[pl.ds(r0, sr), :].astype(F32)
                s2, s1, s0, sb = sums[g]
                new_sums.append((s2 + fold(x * du), s1 + fold(x * d1), s0 + fold(x * d2), sb + fold(du)))
                new_halos.append(du[0:h8, :])
            return tuple(new_halos), tuple(new_sums)

        z = jnp.zeros((h8, w), F32)
        halos, sums = lax.fori_loop(0, ns, strip, ((halo[0], halo[1]), ((z, z, z, z), (z, z, z, z))))
        halo[0] = halos[0]
        halo[1] = halos[1]
        for g in range(2):
            s2, s1, s0, sb = sums[g]
            dw[g, 2:3, :] += jnp.sum(s2, axis=0, keepdims=True)
            dw[g, 1:2, :] += jnp.sum(s1, axis=0, keepdims=True)
            dw[g, 0:1, :] += jnp.sum(s0, axis=0, keepdims=True)
            db[g] += jnp.sum(sb, axis=0, keepdims=True)

    def rt(b, t):
        return b * nt + (nt - 1 - t)

    def ublk(off):
        return pl.BlockSpec((n, w), lambda j, b, t: (rt(b, t), off + j))

    pair = pl.BlockSpec((2, n, w), lambda j, b, t: (0, rt(b, t), j))
    return _call(
        body, (dy, u, up, up, cw), carry, grid=(nj, nb, nt),
        in_specs=[ublk(0), pair, ublk(0), ublk(nj), pl.BlockSpec((2, 3, w), lambda j, b, t: (0, 0, j))],
        out_specs=[pair, pl.BlockSpec((2, 3, w), lambda j, b, t: (0, 0, j)),
                   pl.BlockSpec((2, 1, w), lambda j, b, t: (0, 0, j))],
        out_shape=[S((2, rows, D_FF), MXU_DT), S((2, 3, D_FF), F32), S((2, 1, D_FF), F32)],
        scratch_shapes=[pltpu.VMEM((2, h8, w), F32)],
        sem=("arbitrary", "arbitrary", "arbitrary"), name=name)


def _lane_mod(shape):
    return lax.broadcasted_iota(jnp.int32, shape, 1) & (HP - 1)


def _q_rope_epi(acc, tab):
    reps = acc.shape[1] // HP
    a = acc * jnp.tile(tab, (1, reps))
    lane = _lane_mod(a.shape)
    shifted = pltpu.roll(a, a.shape[1] - QK_ROPE, axis=1)
    return jnp.where(lane < QK_NOPE, a, jnp.where(lane < QK_HEAD, a + shifted, 0.0)) * Q_PRESCALE


def _k_rope_block(krblk, tabk):
    a = krblk * tabk
    lane = _lane_mod(a.shape)
    b = a + pltpu.roll(a, HP - QK_ROPE, axis=1)
    return jnp.where((lane >= QK_NOPE) & (lane < QK_HEAD), b, 0.0)


def _k_rope_epi(acc, krblk, tabk):
    reps = acc.shape[1] // HP
    return acc + jnp.tile(_k_rope_block(krblk, tabk), (1, reps))


def attn_fwd(q, k, v, nb, tp, name, carry=None):
    rows = q.shape[0]
    blk = ATT_BLK
    nq = tp // blk
    npair = MLA_HEADS // 2

    def body(q_r, k_r, v_r, o_r, lse_r):
        qi = pl.program_id(2)
        lane = lax.broadcasted_iota(jnp.int32, (blk, LANE), 1)
        even = lane < V_HEAD
        sum_lane = (V_HEAD, 0)
        rowi = lax.broadcasted_iota(jnp.int32, (blk, blk), 0)
        coli = lax.broadcasted_iota(jnp.int32, (blk, blk), 1)
        qs = [q_r[:, h * HP:(h + 1) * HP] for h in range(2)]

        def kv_block(k0, width, carry, visible):
            ms, accs = carry
            vblk = v_r[pl.ds(k0, width), :]
            one = jnp.ones_like(vblk)
            zero = jnp.zeros_like(vblk)
            vlane = lax.broadcasted_iota(jnp.int32, (width, LANE), 1)
            ss = [lax.dot_general(qs[h], k_r[pl.ds(k0, width), h * HP:(h + 1) * HP], NT_DIMS,
                                  preferred_element_type=F32) for h in range(2)]
            new_ms, new_accs = [], []
            for h in range(2):
                s = ss[h]
                if visible is not None:
                    s = jnp.where(visible, s, -jnp.inf)
                m_new = jnp.maximum(ms[h], jnp.max(s, axis=1, keepdims=True))
                alpha = jnp.exp2(ms[h] - m_new)
                p = jnp.exp2(s - m_new).astype(MXU_DT)
                mine = (vlane < V_HEAD) if h == 0 else (vlane >= V_HEAD)
                vh = jnp.where(mine, vblk, jnp.where(vlane == sum_lane[h], one, zero))
                new_accs.append(alpha * accs[h] + jnp.dot(p, vh, preferred_element_type=F32))
                new_ms.append(m_new)
            return tuple(new_ms), tuple(new_accs)

        neg = jnp.full((blk, 1), -jnp.inf, F32)
        zacc = jnp.zeros((blk, LANE), F32)
        carry = lax.fori_loop(0, qi // 2, lambda i, c: kv_block(pl.multiple_of(i * 2 * blk, blk), 2 * blk, c, None),
                              ((neg, neg), (zacc, zacc)))
        rowi2 = lax.broadcasted_iota(jnp.int32, (blk, 2 * blk), 0)
        coli2 = lax.broadcasted_iota(jnp.int32, (blk, 2 * blk), 1)
        ms, accs = lax.cond(
            qi % 2 == 1,
            lambda c: kv_block(pl.multiple_of((qi - 1) * blk, blk), 2 * blk, c, coli2 - blk <= rowi2),
            lambda c: kv_block(pl.multiple_of(qi * blk, blk), blk, c, coli <= rowi), carry)
        ls = [accs[h][:, sum_lane[h]:sum_lane[h] + 1] for h in range(2)]
        o_r[...] = jnp.where(even, accs[0] / ls[0], accs[1] / ls[1]).astype(MXU_DT)
        lse_r[...] = jnp.where(even, ms[0] + jnp.log2(ls[0]), ms[1] + jnp.log2(ls[1]))

    return _call(
        body, (q, k, v), carry, grid=(nb, npair, nq),
        in_specs=[pl.BlockSpec((blk, 2 * HP), lambda b, p, i: (b * nq + i, p)),
                  pl.BlockSpec((tp, 2 * HP), lambda b, p, i: (b, p)),
                  pl.BlockSpec((tp, LANE), lambda b, p, i: (b, p))],
        out_specs=[pl.BlockSpec((blk, LANE), lambda b, p, i: (b * nq + i, p)),
                   pl.BlockSpec((None, blk, LANE), lambda b, p, i: (p, b * nq + i, 0))],
        out_shape=[S((rows, MLA_HEADS * V_HEAD), MXU_DT), S((npair, rows, LANE), F32)], scratch_shapes=[],
        sem=("parallel", "parallel", "arbitrary"), name=name)


def attn_bwd(q, k, v, o, do, lse, nb, tp, name, carry=None):
    rows = q.shape[0]
    blk = ATT_BLK
    nq = tp // blk
    npair = MLA_HEADS // 2
    scale = QK_HEAD ** -0.5

    def body(q_r, k_r, v_r, o_r, do_r, lse_r, dq_o, dk_o, dv_o, dq_acc, delta_sc):
        kb = pl.program_id(2)
        even = lax.broadcasted_iota(jnp.int32, (blk, LANE), 1) < V_HEAD
        rowi = lax.broadcasted_iota(jnp.int32, (blk, blk), 0)
        coli = lax.broadcasted_iota(jnp.int32, (blk, blk), 1)

        @pl.when(kb == 0)
        def _():
            dq_acc[...] = jnp.zeros_like(dq_acc)

            def dstep(i, c):
                r0 = pl.multiple_of(i * blk, blk)
                prod = do_r[pl.ds(r0, blk), :].astype(F32) * o_r[pl.ds(r0, blk), :].astype(F32)
                de = jnp.sum(jnp.where(even, prod, 0.0), axis=1, keepdims=True)
                dd = jnp.sum(jnp.where(even, 0.0, prod), axis=1, keepdims=True)
                delta_sc[pl.ds(r0, blk), :] = jnp.where(even, de, dd)
                return c

            lax.fori_loop(0, nq, dstep, 0)

        vblk = v_r[...]
        ks = [k_r[:, h * HP:(h + 1) * HP] for h in range(2)]

        def q_block(r0, height, carry, visible):
            dk0, dk1, dv = carry
            dob = do_r[pl.ds(r0, height), :]
            lse_b = lse_r[pl.ds(r0, height), :]
            dl_b = delta_sc[pl.ds(r0, height), :]
            qlane = lax.broadcasted_iota(jnp.int32, (height, LANE), 1)
            dks = [dk0, dk1]
            qhs = [q_r[pl.ds(r0, height), h * HP:(h + 1) * HP] for h in range(2)]
            dohs = [jnp.where((qlane < V_HEAD) if h == 0 else (qlane >= V_HEAD), dob, jnp.zeros_like(dob))
                    for h in range(2)]
            ss = [lax.dot_general(qhs[h], ks[h], NT_DIMS, preferred_element_type=F32) for h in range(2)]
            dps = [lax.dot_general(dohs[h], vblk, NT_DIMS, preferred_element_type=F32) for h in range(2)]
            for h in range(2):
                lo = 0 if h == 0 else V_HEAD
                p = jnp.exp2(ss[h] - lse_b[:, lo:lo + 1])
                if visible is not None:
                    p = jnp.where(visible, p, 0.0)
                ds = (p * (dps[h] - dl_b[:, lo:lo + 1])).astype(MXU_DT)
                dv = dv + lax.dot_general(p.astype(MXU_DT), dohs[h], TN_DIMS, preferred_element_type=F32)
                dks[h] = dks[h] + lax.dot_general(ds, qhs[h], TN_DIMS, preferred_element_type=F32)
                dq_acc[pl.ds(r0, height), h * HP:(h + 1) * HP] += jnp.dot(ds, ks[h], preferred_element_type=F32)
            return dks[0], dks[1], dv

        z = jnp.zeros((blk, HP), F32)
        below = nq - 1 - kb
        odd = below % 2
        rowi2 = lax.broadcasted_iota(jnp.int32, (2 * blk, blk), 0)
        coli2 = lax.broadcasted_iota(jnp.int32, (2 * blk, blk), 1)
        first = pl.multiple_of(kb * blk, blk)
        carry = lax.cond(odd == 1, lambda c: q_block(first, 2 * blk, c, coli2 <= rowi2),
                         lambda c: q_block(first, blk, c, coli <= rowi), (z, z, jnp.zeros((blk, LANE), F32)))
        dk0, dk1, dv = lax.fori_loop(
            0, below // 2, lambda i, c: q_block(pl.multiple_of((kb + 1 + odd + 2 * i) * blk, blk), 2 * blk, c, None),
            carry)
        dk_o[:, 0:HP] = (dk0 * (scale / Q_PRESCALE)).astype(MXU_DT)
        dk_o[:, HP:2 * HP] = (dk1 * (scale / Q_PRESCALE)).astype(MXU_DT)
        dv_o[...] = dv.astype(MXU_DT)

        @pl.when(kb == nq - 1)
        def _():
            dq_o[...] = (dq_acc[...] * scale).astype(MXU_DT)

    seq_pair = pl.BlockSpec((tp, LANE), lambda b, p, kk: (b, p))
    return _call(
        body, (q, k, v, o, do, lse), carry, grid=(nb, npair, nq),
        in_specs=[pl.BlockSpec((tp, 2 * HP), lambda b, p, kk: (b, p)),
                  pl.BlockSpec((blk, 2 * HP), lambda b, p, kk: (b * nq + kk, p)),
                  pl.BlockSpec((blk, LANE), lambda b, p, kk: (b * nq + kk, p)),
                  seq_pair, seq_pair, pl.BlockSpec((None, tp, LANE), lambda b, p, kk: (p, b, 0))],
        out_specs=[pl.BlockSpec((tp, 2 * HP), lambda b, p, kk: (b, p)),
                   pl.BlockSpec((blk, 2 * HP), lambda b, p, kk: (b * nq + kk, p)),
                   pl.BlockSpec((blk, LANE), lambda b, p, kk: (b * nq + kk, p))],
        out_shape=[S((rows, MLA_HEADS * HP), MXU_DT), S((rows, MLA_HEADS * HP), MXU_DT),
                   S((rows, MLA_HEADS * V_HEAD), MXU_DT)],
        scratch_shapes=[pltpu.VMEM((tp, 2 * HP), F32), pltpu.VMEM((tp, LANE), F32)],
        sem=("parallel", "parallel", "arbitrary"), name=name)


def rope_bwd(dq, dk, dv, tabq, tabk, tp, tm, name):
    rows = dq.shape[0]
    nt = tp // tm
    wq = MLA_HEADS * HP

    def body(dq_r, dk_r, dv_r, tq_r, tk_r, dqa_o, dkv_o, dkr_o):
        dqv = dq_r[...].astype(F32)
        lane = _lane_mod(dqv.shape)
        in_rope = (lane >= QK_NOPE) & (lane < QK_HEAD)
        rope = jnp.where(in_rope, dqv, 0.0)
        da = jnp.where(lane < QK_HEAD, dqv, 0.0) + pltpu.roll(rope, QK_ROPE, axis=1)
        dqa_o[...] = (da * jnp.tile(tq_r[...], (1, MLA_HEADS))).astype(MXU_DT)
        dkf = dk_r[...].astype(F32)
        dkv_o[:, 0:wq] = jnp.where(lane < QK_NOPE, dkf, 0.0).astype(MXU_DT)
        dkv_o[:, wq:] = dv_r[...]
        kr = jnp.where(in_rope, dkf, 0.0)
        tot = kr[:, 0:HP]
        for h in range(1, MLA_HEADS):
            tot = tot + kr[:, h * HP:(h + 1) * HP]
        dkr_o[...] = ((tot + pltpu.roll(tot, QK_ROPE, axis=1)) * tk_r[...]).astype(MXU_DT)

    def rowblk(wd):
        return pl.BlockSpec((tm, wd), lambda i: (i, 0))

    tab = pl.BlockSpec((tm, HP), lambda i: (i % nt, 0))
    return pl.pallas_call(
        body, grid=(rows // tm,), in_specs=[rowblk(wq), rowblk(wq), rowblk(MLA_HEADS * V_HEAD), tab, tab],
        out_specs=[rowblk(wq), rowblk(wq + MLA_HEADS * V_HEAD), rowblk(HP)],
        out_shape=[S((rows, wq), MXU_DT), S((rows, wq + MLA_HEADS * V_HEAD), MXU_DT), S((rows, HP), MXU_DT)],
        compiler_params=_cp(("parallel",)), name=name)(dq, dk, dv, tabq, tabk)


def loss_head(h, target, gain, tp, t_real, tm, name):
    rows = h.shape[0]
    nt = tp // tm

    def body(h_r, t_r, g_r, dh_o, loss_o, dg_o):
        i = pl.program_id(0)

        @pl.when(i == 0)
        def _():
            loss_o[...] = jnp.zeros_like(loss_o)
            dg_o[...] = jnp.zeros_like(dg_o)

        xv = h_r[...]
        rstd = lax.rsqrt(jnp.mean(xv * xv, axis=-1, keepdims=True) + EPS)
        xhat = xv * rstd
        g = g_r[...]
        pos = (i % nt) * tm + lax.broadcasted_iota(jnp.int32, (tm, 1), 0)
        valid = (pos >= N_META) & (pos < t_real)
        err = jnp.where(valid, xhat * g - t_r[...], 0.0)
        loss_o[...] += 0.5 * jnp.sum(jnp.mean(err * err, axis=-1, keepdims=True))
        dy = err * (1.0 / D_MODEL)
        dg_o[...] += jnp.sum(dy * xhat, axis=0, keepdims=True)
        dxh = dy * g
        dh_o[...] = rstd * (dxh - xhat * jnp.mean(dxh * xhat, axis=-1, keepdims=True))

    blk = pl.BlockSpec((tm, D_MODEL), lambda i: (i, 0))
    return pl.pallas_call(
        body, grid=(rows // tm,), in_specs=[blk, blk, pl.BlockSpec((1, D_MODEL), lambda i: (0, 0))],
        out_specs=[blk, pl.BlockSpec((1, LANE), lambda i: (0, 0)), pl.BlockSpec((1, D_MODEL), lambda i: (0, 0))],
        out_shape=[S((rows, D_MODEL), F32), S((1, LANE), F32), S((1, D_MODEL), F32)],
        compiler_params=_cp(("arbitrary",)), name=name)(h, target, gain)


ADAM_TILE_ELEMS = 512 * 1024


def adamw(g, w, m, v, name):
    shape = w.shape
    cols = shape[-1]
    rws = max(1, math.prod(shape[:-1]))
    tr = rws if rws * cols <= ADAM_TILE_ELEMS else _div_tile(rws, max(SUBLANE, ADAM_TILE_ELEMS // cols), SUBLANE)
    bc1 = 1.0 - ADAM_B1 ** ADAM_STEP
    bc2 = 1.0 - ADAM_B2 ** ADAM_STEP

    def body(g_r, w_r, m_r, v_r, do, mo, vo):
        gv = g_r[...]
        mn = ADAM_B1 * m_r[...] + (1.0 - ADAM_B1) * gv
        vn = ADAM_B2 * v_r[...] + (1.0 - ADAM_B2) * (gv * gv)
        m_hat = mn / bc1
        v_hat = vn / bc2
        do[...] = -ADAM_LR * (m_hat / (jnp.sqrt(v_hat) + ADAM_EPS) + ADAM_WD * w_r[...])
        mo[...] = mn
        vo[...] = vn

    blk = pl.BlockSpec((tr, cols), lambda i: (i, 0))
    outs = pl.pallas_call(
        body, grid=(rws // tr,), in_specs=[blk] * 4, out_specs=[blk] * 3, out_shape=[S((rws, cols), F32)] * 3,
        compiler_params=_cp(("parallel",)), name=name)(*[a.reshape(rws, cols) for a in (g, w, m, v)])
    return tuple(o.reshape(shape) for o in outs)


SUM_TILE_ELEMS = 512 * 1024


def _place():
    return lax.axis_index("x"), lax.axis_index("y"), lax.axis_index("c")


def _remote(src, dst, send_sems, recv_sems, k, to):
    return pltpu.make_async_remote_copy(src_ref=src, dst_ref=dst, send_sem=send_sems.at[k], recv_sem=recv_sems.at[k],
                                        device_id=to, device_id_type=MESH)


def chip_index():
    return 2 * lax.axis_index("x") + lax.axis_index("y")


def _sem_pair(n):
    return [pltpu.SemaphoreType.DMA((n,)), pltpu.SemaphoreType.DMA((n,))]


def stage_gather_chips(xs):
    def copies(ins, outs, sems):
        send_sems, recv_sems = sems
        mx, my, mc = _place()
        sibling = (mx, my, 1 - mc)
        chips = [(1 - mx, my), (mx, 1 - my), (1 - mx, 1 - my)]
        first, landed, passed, from_sibling = [], [], [], []
        for i, (x_ref, out_ref) in enumerate(zip(ins, outs)):
            def piece(cx, cy, h, out_ref=out_ref):
                return out_ref.at[2 * cx + cy, h]

            for j, (cx, cy) in enumerate(chips):
                k = 6 * i + j
                first.append(_remote(x_ref.at[mc], piece(mx, my, mc), send_sems, recv_sems, k, (cx, cy, mc)))
                landed.append(_remote(x_ref.at[mc], piece(cx, cy, mc), send_sems, recv_sems, k, (cx, cy, mc)))
                passed.append(_remote(piece(cx, cy, mc), piece(cx, cy, mc), send_sems, recv_sems, k + 3, sibling))
                from_sibling.append(_remote(x_ref.at[mc], piece(cx, cy, 1 - mc), send_sems, recv_sems, k + 3, sibling))
        return first, landed, passed, from_sibling

    def start(ins, outs, sems):
        for cp in copies(ins, outs, sems)[0]:
            cp.start()

    def finish(ins, outs, sems):
        first, landed, passed, from_sibling = copies(ins, outs, sems)
        for arrived, onward in zip(landed, passed):
            arrived.wait_recv()
            onward.start()
        for cp in from_sibling:
            cp.wait_recv()
        for cp in first + passed:
            cp.wait_send()

    return Stage(list(xs), [S((4,) + x.shape, x.dtype) for x in xs], _sem_pair(6 * len(xs)), start, finish)


def own_block(gathered, xs):
    return lax.dynamic_update_slice(gathered, xs[None], (chip_index(), 0, 0, 0))


def stage_pair_exchange(gs):
    def copies(ins, outs, sems):
        send_sems, recv_sems = sems
        mx, my, mc = _place()
        return [_remote(g_ref.at[s, 1 - mc], land_ref.at[s], send_sems, recv_sems, 4 * i + s, (mx, my, 1 - mc))
                for i, (g_ref, land_ref) in enumerate(zip(ins, outs)) for s in range(4)]

    def start(ins, outs, sems):
        for cp in copies(ins, outs, sems):
            cp.start()

    def finish(ins, outs, sems):
        cps = copies(ins, outs, sems)
        for cp in cps:
            cp.wait_recv()
        for cp in cps:
            cp.wait_send()

    return Stage(list(gs), [S((4,) + g.shape[2:], g.dtype) for g in gs], _sem_pair(4 * len(gs)), start, finish)


def _sum_rows(rws, width):
    return _div_tile(rws, max(SUBLANE, SUM_TILE_ELEMS // width), SUBLANE)


def pair_sum(g4, land, c_idx, name):
    _, _, rws, wd = g4.shape
    th = _sum_rows(rws, wd)

    def body(c_ref, a_ref, b_ref, o_ref):
        o_ref[...] = a_ref[...] + b_ref[...]

    return pl.pallas_call(
        body,
        grid_spec=pltpu.PrefetchScalarGridSpec(
            num_scalar_prefetch=1, grid=(4, rws // th),
            in_specs=[pl.BlockSpec((None, None, th, wd), lambda s, i, c: (s, c[0], i, 0)),
                      pl.BlockSpec((None, th, wd), lambda s, i, c: (s, i, 0))],
            out_specs=pl.BlockSpec((None, th, wd), lambda s, i, c: (s, i, 0))),
        out_shape=S((4, rws, wd), F32), compiler_params=_cp(("parallel", "parallel")), name=name)(c_idx, g4, land)


def stage_chip_scatter(ps):
    def copies(ins, outs, sems):
        send_sems, recv_sems = sems
        mx, my, mc = _place()
        me = 2 * mx + my
        chips = [(1 - mx, my), (mx, 1 - my), (1 - mx, 1 - my)]
        sent, landed = [], []
        for i, (p_ref, land_ref) in enumerate(zip(ins, outs)):
            for j, (cx, cy) in enumerate(chips):
                k = 3 * i + j
                sent.append(_remote(p_ref.at[2 * cx + cy], land_ref.at[me], send_sems, recv_sems, k, (cx, cy, mc)))
                landed.append(_remote(p_ref.at[me], land_ref.at[2 * cx + cy], send_sems, recv_sems, k, (cx, cy, mc)))
        return sent, landed

    def start(ins, outs, sems):
        for cp in copies(ins, outs, sems)[0]:
            cp.start()

    def finish(ins, outs, sems):
        sent, landed = copies(ins, outs, sems)
        for cp in landed:
            cp.wait_recv()
        for cp in sent:
            cp.wait_send()

    return Stage(list(ps), [S(p.shape, p.dtype) for p in ps], _sem_pair(3 * len(ps)), start, finish)


def chip_sum(l4, p4, me_idx, name):
    _, rws, wd = l4.shape
    th = _sum_rows(rws, wd)

    def body(me_ref, a, b, c, d, own, o_ref):
        me = me_ref[0]
        parts = [jnp.where(me == s, own[...], r[...]) for s, r in enumerate((a, b, c, d))]
        o_ref[...] = ((parts[0] + parts[1]) + parts[2]) + parts[3]

    def blk(s):
        return pl.BlockSpec((None, th, wd), lambda i, me: (jnp.where(me[0] == s, (s + 1) % 4, s), i, 0))

    return pl.pallas_call(
        body,
        grid_spec=pltpu.PrefetchScalarGridSpec(
            num_scalar_prefetch=1, grid=(rws // th,),
            in_specs=[blk(0), blk(1), blk(2), blk(3), pl.BlockSpec((None, th, wd), lambda i, me: (me[0], i, 0))],
            out_specs=pl.BlockSpec((th, wd), lambda i, me: (i, 0))),
        out_shape=S((rws, wd), F32), compiler_params=_cp(("parallel",)), name=name)(me_idx, l4, l4, l4, l4, p4)


def stage_pair_gather(rs):
    def copies(ins, outs, sems):
        send_sems, recv_sems = sems
        mx, my, mc = _place()
        return [_remote(r_ref, out_ref, send_sems, recv_sems, i, (mx, my, 1 - mc))
                for i, (r_ref, out_ref) in enumerate(zip(ins, outs))]

    def start(ins, outs, sems):
        for cp in copies(ins, outs, sems):
            cp.start()

    def finish(ins, outs, sems):
        for cp in copies(ins, outs, sems):
            cp.wait()

    return Stage(list(rs), [S(r.shape, r.dtype) for r in rs], _sem_pair(len(rs)), start, finish)


PACK_ELEMS = 16 * LANE


def pack_rows(arrays, lead, total_mult):
    parts, offs, r0 = [], [], 0
    for a in arrays:
        flat = a.reshape(a.shape[:lead] + (-1,))
        elems = _round_up(flat.shape[-1], PACK_ELEMS)
        flat = jnp.pad(flat, [(0, 0)] * lead + [(0, elems - flat.shape[-1])])
        parts.append(flat.reshape(flat.shape[:lead] + (elems // LANE, LANE)))
        offs.append((r0, elems // LANE))
        r0 += elems // LANE
    total = _round_up(r0, total_mult)
    if total > r0:
        parts.append(jnp.zeros(parts[0].shape[:lead] + (total - r0, LANE), parts[0].dtype))
    return jnp.concatenate(parts, axis=lead), offs


def unpack_rows(buf, off, shape):
    r0, nr = off
    lead = buf.shape[:-2]
    n = math.prod(shape)
    return buf[..., r0:r0 + nr, :].reshape(lead + (nr * LANE,))[..., :n].reshape(lead + tuple(shape))


def unshard(stacked, axis):
    x = jnp.moveaxis(stacked, 0, axis)
    return x.reshape(x.shape[:axis] + (4 * x.shape[axis + 1],) + x.shape[axis + 2:])


def to_shards(full, axis):
    n = full.shape[axis] // 4
    x = full.reshape(full.shape[:axis] + (4, n) + full.shape[axis + 1:])
    return jnp.moveaxis(x, axis, 0)


def _rot_cols(w):
    half = w.shape[-1] // 2
    return jnp.concatenate([-w[..., half:], w[..., :half]], axis=-1)


def _unrot_cols(dw):
    half = dw.shape[-1] // 2
    return jnp.concatenate([dw[..., half:], -dw[..., :half]], axis=-1)


def odd_w_in_padded(w_in):
    kr = w_in[:, Q_LORA + KV_LORA:]
    rows = w_in.shape[0]
    return jnp.concatenate([w_in[:, :Q_LORA], jnp.zeros((rows, 128), w_in.dtype), w_in[:, Q_LORA:Q_LORA + KV_LORA],
                            jnp.zeros((rows, 64), w_in.dtype), kr, _rot_cols(kr)], axis=1)


def odd_w_in_unpad(dwp):
    base = 512 + KV_LORA + 64
    dkr = dwp[:, base:base + QK_ROPE] + _unrot_cols(dwp[:, base + QK_ROPE:base + 2 * QK_ROPE])
    return jnp.concatenate([dwp[:, :Q_LORA], dwp[:, 512:512 + KV_LORA], dkr], axis=1)


def uq_padded(w_uq):
    w = w_uq.reshape(Q_LORA, MLA_HEADS, QK_HEAD)
    return jnp.concatenate([w, _rot_cols(w[:, :, QK_NOPE:])], axis=-1).reshape(Q_LORA, MLA_HEADS * HP)


def uq_unpad(dwp):
    d = dwp.reshape(Q_LORA, MLA_HEADS, HP)
    rope = d[:, :, QK_NOPE:QK_HEAD] + _unrot_cols(d[:, :, QK_HEAD:])
    return jnp.concatenate([d[:, :, :QK_NOPE], rope], axis=-1).reshape(Q_LORA, MLA_HEADS * QK_HEAD)


def ukv_padded(w_ukv):
    w = w_ukv.reshape(KV_LORA, MLA_HEADS, QK_NOPE + V_HEAD)
    wk = jnp.concatenate([w[:, :, :QK_NOPE], jnp.zeros((KV_LORA, MLA_HEADS, HP - QK_NOPE), w.dtype)], axis=-1)
    return jnp.concatenate([wk.reshape(KV_LORA, MLA_HEADS * HP), w[:, :, QK_NOPE:].reshape(KV_LORA, MLA_HEADS * V_HEAD)],
                           axis=1)


def ukv_unpad(dwp):
    dk = dwp[:, :MLA_HEADS * HP].reshape(KV_LORA, MLA_HEADS, HP)[:, :, :QK_NOPE]
    dv = dwp[:, MLA_HEADS * HP:].reshape(KV_LORA, MLA_HEADS, V_HEAD)
    return jnp.concatenate([dk, dv], axis=-1).reshape(KV_LORA, MLA_HEADS * (QK_NOPE + V_HEAD))


def block_diag(w):
    h, d, _ = w.shape
    eye = jnp.eye(h, dtype=w.dtype)
    return (eye[:, None, :, None] * w[:, :, None, :]).reshape(h * d, h * d)


def block_diag_part(dense, h):
    d = dense.shape[0] // h
    x = dense.reshape(h, d, h, d)
    return jnp.stack([x[i, :, i, :] for i in range(h)], axis=0)


def rope_tables(tp):
    pos = jnp.arange(tp, dtype=F32)
    inv_freq = ROPE_BASE ** (-jnp.arange(0, QK_ROPE, 2, dtype=F32) / QK_ROPE)
    ang = pos[:, None] * inv_freq[None, :]
    cos2 = jnp.tile(jnp.cos(ang), (1, 2))
    sin2 = jnp.tile(jnp.sin(ang), (1, 2))
    tabq = jnp.concatenate([jnp.ones((tp, QK_NOPE), F32), cos2, sin2], axis=1)
    tabk = jnp.concatenate([jnp.zeros((tp, QK_NOPE), F32), cos2, sin2], axis=1)
    return tabq, tabk


class Dims:
    def __init__(self, nb, seq):
        self.nb = nb
        self.t_real = seq + N_META
        self.tp = _round_up(self.t_real, ATT_BLK)
        self.n = self.tp // 4
        assert self.n % 16 == 0
        self.rows = nb * self.tp


class NoComm:
    def advance(self, carried):
        return None


def even_fwd(h, p, dm, comm):
    (u, hn), _ = norm_matmul(h, 0, D_MODEL, p["norm"], p["w_in"], dm.n, EVEN_IN, F32, "ev_in")
    (y, ca, xc, a, hs), got = even_mid_fwd(u, p["conv_a"], p["conv_b"], p["conv_b_bias"], p["rw"], p["r_b"], p["iw"],
                                           p["i_b"], p["lam"], dm.nb, dm.tp, dm.n, "ev_mid", carry=comm.advance(None))
    comm.advance(got)
    out = matmul_res(y, p["w_out"].reshape(2, CONV_W, D_MODEL), h, dm.n, D_MODEL, "ev_out")
    return out, (h, u, hn, ca, xc, a, hs, y)


def even_bwd(dout, saved, p, dm, comm):
    h, u, hn, ca, xc, a, hs, y = saved
    g = {}
    dycat = matmul_nt(dout, p["w_out"], dm.n, D_MODEL, F32, "ev_dycat")
    g["w_out"], got = matmul_tn(y, dout, dm.n, "ev_dw_out", carry=comm.advance(None))
    outs, got = even_mid_bwd(u, dycat, ca, xc, a, hs, p["conv_a"], p["conv_b"], p["rw"], p["r_b"], p["iw"], p["i_b"],
                             p["lam"], dm.nb, dm.tp, dm.n, "ev_mid_bwd", carry=comm.advance(got))
    du, g["conv_a"], g["conv_b"], g["conv_b_bias"], drw, g["r_b"], diw, g["i_b"], g["lam"] = outs
    g["r_w"] = block_diag_part(drw, LRU_HEADS)
    g["i_w"] = block_diag_part(diw, LRU_HEADS)
    g["w_in"], got = matmul_tn(hn, du, dm.n, "ev_dw_in", carry=comm.advance(got))
    comm.advance(got)
    dx, g["norm"] = matmul_nt_normbwd(du, p["w_in"], h, 0, p["norm"], dout, dm.n, 512, F32, "ev_dx")
    return dx, g


def odd_fwd(h, p, tabq, tabk, dm, comm):
    nt = dm.tp // dm.n
    (u, hn), _ = norm_matmul(h, 0, D_MODEL, p["norm"], p["w_in_p"], dm.n, ODD_PAD, F32, "od_in")
    tab_spec = pl.BlockSpec((dm.n, HP), lambda i, j: (i % nt, 0))
    (q, cqn), _ = norm_matmul(u, 0, Q_LORA, p["q_norm"], p["w_uq_p"], dm.n, MLA_HEADS * HP, MXU_DT, "od_q",
                              epi=_q_rope_epi, epi_ops=(tabq,), epi_specs=(tab_spec,))
    kr_spec = pl.BlockSpec((dm.n, HP), lambda i, j: (i, ODD_KR_COL))
    (k, ckvn), _ = norm_matmul(u, ODD_CKV_COL, KV_LORA, p["kv_norm"], p["w_uk_p"], dm.n, MLA_HEADS * HP, MXU_DT, "od_k",
                               epi=_k_rope_epi, epi_ops=(u, tabk), epi_specs=(kr_spec, tab_spec))
    (v, _), _ = norm_matmul(u, ODD_CKV_COL, KV_LORA, p["kv_norm"], p["w_uv_p"], dm.n, MLA_HEADS * V_HEAD, MXU_DT,
                            "od_v")
    (o, lse), got = attn_fwd(q, k, v, dm.nb, dm.tp, "od_attn", carry=comm.advance(None))
    comm.advance(got)
    out = matmul_res(o[None], p["w_out"][None], h, dm.n, D_MODEL, "od_out")
    return out, (h, u, hn, cqn, ckvn, q, k, v, o, lse)


def odd_bwd(dout, saved, p, tabq, tabk, dm, comm):
    h, u, hn, cqn, ckvn, q, k, v, o, lse = saved
    g = {}
    do = matmul_nt(dout, p["w_out"], dm.n, D_MODEL, MXU_DT, "od_do")
    g["w_out"], got = matmul_tn(o, dout, dm.n, "od_dw_out", carry=comm.advance(None))
    (dq, dk, dv), got = attn_bwd(q, k, v, o, do, lse, dm.nb, dm.tp, "od_attn_bwd", carry=comm.advance(got))
    dqa, dkv, dkr = rope_bwd(dq, dk, dv, tabq, tabk, dm.tp, dm.n, "od_rope_bwd")
    g["w_uq_p"], got = matmul_tn(cqn, dqa, dm.n, "od_dw_uq", carry=comm.advance(got))
    comm.advance(got)
    g["w_ukv_p"], _ = matmul_tn(ckvn, dkv, dm.n, "od_dw_ukv")
    dcq, g["q_norm"] = matmul_nt_normbwd(dqa, p["w_uq_p"], u, 0, p["q_norm"], None, dm.n, mat_cols(dqa), MXU_DT,
                                         "od_dcq")
    dckv, g["kv_norm"] = matmul_nt_normbwd(dkv, p["w_ukv_p"], u, ODD_CKV_COL, p["kv_norm"], None, dm.n, mat_cols(dkv),
                                           MXU_DT, "od_dckv")
    du = jnp.concatenate([dcq, jnp.zeros((dm.rows, 128), MXU_DT), dckv, dkr], axis=1)
    g["w_in_p"], _ = matmul_tn(hn, du, dm.n, "od_dw_in")
    dx, g["norm"] = matmul_nt_normbwd(du, p["w_in_p"], h, 0, p["norm"], dout, dm.n, ODD_PAD, F32, "od_dx")
    return dx, g


def ffn_fwd(h, p, dm, comm):
    (up, hn), got = norm_matmul(h, 0, D_MODEL, p["norm"], p["w_up"], dm.n, D_FF // 2, MXU_DT, "ffn_up",
                                carry=comm.advance(None))
    comm.advance(got)
    u, y = ffn_mid_fwd(up, p["cw"], p["cb"], dm.nb, dm.tp, dm.n, "ffn_mid")
    out = matmul_res(y[None], p["w_down"][None], h, dm.n, D_MODEL, "ffn_down")
    return out, (h, up, hn, u, y)


def ffn_bwd(dout, saved, p, dm, comm):
    h, up, hn, u, y = saved
    g = {}
    dy = matmul_nt(dout, p["w_down"], dm.n, D_FF, MXU_DT, "ffn_dy")
    g["w_down"], got = matmul_tn(y, dout, dm.n, "ffn_dw_down", carry=comm.advance(None))
    (dup, g["cw"], g["cb"]), got = ffn_mid_bwd(dy, u, up, p["cw"], dm.nb, dm.tp, dm.n, "ffn_mid_bwd",
                                               carry=comm.advance(got))
    g["w_up"], got = matmul_tn(hn, dup, dm.n, "ffn_dw_up", carry=comm.advance(got), col_shards=4)
    comm.advance(got)
    dx, g["norm"] = matmul_nt_normbwd(dup, p["w_up"], h, 0, p["norm"], dout, dm.n, D_FF // 2, F32, "ffn_dx")
    return dx, g


def _row(v):
    return v.reshape(1, -1)


def even_params(wf, j):
    return dict(norm=_row(wf["ev_norm"][j]), w_in=wf["ev_w_in"], conv_a=wf["ev_conv_a"][j], conv_b=wf["ev_conv_b"][j],
                conv_b_bias=_row(wf["ev_conv_b_bias"][j]), rw=block_diag(wf["ev_gate_r_w"][j]).astype(MXU_DT),
                r_b=_row(wf["ev_gate_r_b"][j]), iw=block_diag(wf["ev_gate_i_w"][j]).astype(MXU_DT),
                i_b=_row(wf["ev_gate_i_b"][j]), lam=_row(wf["ev_lru_lambda"][j]), w_out=wf["ev_w_out"])


def odd_params(wf, j):
    wkv = ukv_padded(wf["od_w_ukv"])
    return dict(norm=_row(wf["od_norm"][j]), w_in_p=odd_w_in_padded(wf["od_w_in"]), q_norm=_row(wf["od_q_norm"][j]),
                kv_norm=_row(wf["od_kv_norm"][j]), w_uq_p=uq_padded(wf["od_w_uq"]), w_ukv_p=wkv,
                w_uk_p=wkv[:, :MLA_HEADS * HP], w_uv_p=wkv[:, MLA_HEADS * HP:], w_out=wf["od_w_out"])


def ffn_params(wf, layer):
    return dict(norm=_row(wf["ffn_norm"][layer]), w_up=wf["ffn_w_up"],
                cw=jnp.moveaxis(wf["ffn_conv_w"][layer].reshape(3, 2, D_FF), 1, 0),
                cb=wf["ffn_conv_b"][layer].reshape(2, 1, D_FF), w_down=wf["ffn_w_down"])


def even_grads(g):
    out = {"ev_" + k_: g[k_] for k_ in ("w_in", "conv_a", "conv_b", "w_out")}
    out.update({"ev_norm": g["norm"][0], "ev_conv_b_bias": g["conv_b_bias"][0], "ev_gate_r_w": g["r_w"],
                "ev_gate_r_b": g["r_b"][0], "ev_gate_i_w": g["i_w"], "ev_gate_i_b": g["i_b"][0],
                "ev_lru_lambda": g["lam"][0]})
    return out


def odd_grads(g):
    return {"od_norm": g["norm"][0], "od_q_norm": g["q_norm"][0], "od_kv_norm": g["kv_norm"][0],
            "od_w_in": odd_w_in_unpad(g["w_in_p"]), "od_w_uq": uq_unpad(g["w_uq_p"]),
            "od_w_ukv": ukv_unpad(g["w_ukv_p"]), "od_w_out": g["w_out"]}


def ffn_grads(g):
    return {"ffn_norm": g["norm"][0], "ffn_w_up": g["w_up"], "ffn_conv_w": jnp.moveaxis(g["cw"], 0, 1).reshape(3, 2 * D_FF),
            "ffn_conv_b": g["cb"].reshape(2 * D_FF), "ffn_w_down": g["w_down"]}


WEIGHTS = ["meta_tokens", "ev_norm", "ev_w_in", "ev_conv_a", "ev_conv_b", "ev_conv_b_bias", "ev_gate_r_w", "ev_gate_r_b",
           "ev_gate_i_w", "ev_gate_i_b", "ev_lru_lambda", "ev_w_out", "od_norm", "od_w_in", "od_q_norm", "od_kv_norm",
           "od_w_uq", "od_w_ukv", "od_w_out", "ffn_norm", "ffn_w_up", "ffn_conv_w", "ffn_conv_b", "ffn_w_down",
           "final_norm"]
SHARD_AXIS = {"meta_tokens": 1, "ev_w_in": 2, "ev_conv_a": 2, "ev_conv_b": 2, "ev_w_out": 1, "od_norm": 1, "od_w_in": 1,
              "od_q_norm": 1, "od_kv_norm": 1, "od_w_uq": 2, "od_w_ukv": 2, "od_w_out": 1, "ffn_w_up": 2,
              "ffn_conv_w": 2, "ffn_w_down": 1}
MATMUL_WEIGHTS = ["ev_w_in", "ev_w_out", "od_w_in", "od_w_uq", "od_w_ukv", "od_w_out", "ffn_w_up", "ffn_w_down"]


LAYER_ORDER = [("ev", 0), ("ffn", 0), ("od", 0), ("ffn", 1), ("ev", 1), ("ffn", 2), ("od", 1), ("ffn", 3)]
LAYER_MATMUL = {"ev": ["ev_w_in", "ev_w_out"], "od": ["od_w_in", "od_w_uq", "od_w_ukv", "od_w_out"],
                "ffn": ["ffn_w_up", "ffn_w_down"]}
LAYER_SHARDED = {"ev": ["ev_w_in", "ev_conv_a", "ev_conv_b", "ev_w_out"],
                 "od": ["od_norm", "od_w_in", "od_q_norm", "od_kv_norm", "od_w_uq", "od_w_ukv", "od_w_out"],
                 "ffn": ["ffn_w_up", "ffn_conv_w", "ffn_w_down"]}
STACKED_SHARDS = "ffn_w_up"


def gather_at_entry(w, names, first, name):
    buf, offs = pack_rows([w[n] for n in names], 0, 32)
    halves = buf.reshape(2, buf.shape[0] // 2, LANE)
    outs = run_stage(stage_gather_chips([halves] + first.halves), name)
    first.step, first.got = 2, outs[1:]
    got = own_block(outs[0], halves).reshape(4, buf.shape[0], LANE)
    return {n: unshard(unpack_rows(got, off, w[n].shape), SHARD_AXIS[n]) for n, off in zip(names, offs)}


def _halves(a):
    return a.reshape(2, a.shape[0] // 2, a.shape[1])


class GatherComm:
    def __init__(self, w, kind, idx):
        self.names = LAYER_MATMUL[kind]
        self.halves = [_halves(w[n][idx].astype(MXU_DT)) for n in self.names]
        self.stage = stage_gather_chips(self.halves)
        self.step, self.got = 0, None

    def advance(self, carried):
        self.step += 1
        if self.step == 1:
            return self.stage
        if self.step == 2:
            self.got = carried
        return None

    def weights(self):
        out = {}
        for n, got, own in zip(self.names, self.got, self.halves):
            stacked = own_block(got, own).reshape(4, 2 * own.shape[1], own.shape[2])
            out[n] = stacked if n == STACKED_SHARDS else unshard(stacked, SHARD_AXIS[n] - 1)
        return out


class ReduceComm:
    def __init__(self, grads, axes, c_idx, tag, tail=None):
        shards = {n: grads[n] if n == STACKED_SHARDS else to_shards(grads[n], axes[n]) for n in grads}
        self.big = [n for n in grads if n in MATMUL_WEIGHTS]
        self.small = [n for n in grads if n not in MATMUL_WEIGHTS]
        self.shapes = {n: shards[n].shape[1:] for n in grads}
        arrays = [shards[n].reshape(4, 2, shards[n].shape[1] // 2, shards[n].shape[2]) for n in self.big]
        gs, self.offs = pack_rows([shards[n] for n in self.small], 1, 16)
        self.rs = gs.shape[1] // 2
        parts = [gs.reshape(4, 2, self.rs, LANE)]
        self.rr = 0
        if tail is not None:
            self.rr = tail.shape[0] // 8
            parts.append(tail.reshape(4, 2, self.rr, LANE))
        arrays.append(jnp.concatenate(parts, axis=2) if len(parts) > 1 else parts[0])
        self.arrays, self.c_idx, self.tag, self.step = arrays, c_idx, tag, 0
        self.part = self.mine = self.theirs = None

    def advance(self, carried):
        self.step += 1
        if self.step == 1:
            return stage_pair_exchange(self.arrays)
        if self.step == 2:
            self.part = [pair_sum(g, land, self.c_idx, "grad_pair_sum_%s_%d" % (self.tag, i))
                         for i, (g, land) in enumerate(zip(self.arrays, carried))]
            return stage_chip_scatter(self.part)
        if self.step == 3:
            me_idx = chip_index().astype(jnp.int32).reshape(1)
            self.mine = [chip_sum(land, part, me_idx, "grad_chip_sum_%s_%d" % (self.tag, i))
                         for i, (land, part) in enumerate(zip(carried, self.part))]
            return stage_pair_gather(self.mine)
        if self.step == 4:
            self.theirs = carried
        return None

    def run_alone(self, name):
        stage = self.advance(None)
        while stage is not None:
            stage = self.advance(run_stage(stage, name + "_%d" % self.step))

    def results(self):
        south = self.c_idx[0] == 0
        boths = [jnp.stack([jnp.where(south, m, t), jnp.where(south, t, m)], axis=0)
                 for m, t in zip(self.mine, self.theirs)]
        out = {n: b.reshape(self.shapes[n]) for n, b in zip(self.big, boths)}
        packed = boths[-1]
        flat = packed[:, :self.rs].reshape(2 * self.rs, LANE)
        out.update({n: unpack_rows(flat, off, self.shapes[n]) for n, off in zip(self.small, self.offs)})
        return out, packed[:, self.rs:self.rs + self.rr]


def kernel(x, meta_tokens, ev_norm, ev_w_in, ev_conv_a, ev_conv_b, ev_conv_b_bias, ev_gate_r_w, ev_gate_r_b, ev_gate_i_w, ev_gate_i_b, ev_lru_lambda, ev_w_out, od_norm, od_w_in, od_q_norm, od_kv_norm, od_w_uq, od_w_ukv, od_w_out, ffn_norm, ffn_w_up, ffn_conv_w, ffn_conv_b, ffn_w_down, final_norm, loss_target, m_meta_tokens, m_ev_norm, m_ev_w_in, m_ev_conv_a, m_ev_conv_b, m_ev_conv_b_bias, m_ev_gate_r_w, m_ev_gate_r_b, m_ev_gate_i_w, m_ev_gate_i_b, m_ev_lru_lambda, m_ev_w_out, m_od_norm, m_od_w_in, m_od_q_norm, m_od_kv_norm, m_od_w_uq, m_od_w_ukv, m_od_w_out, m_ffn_norm, m_ffn_w_up, m_ffn_conv_w, m_ffn_conv_b, m_ffn_w_down, m_final_norm, v_meta_tokens, v_ev_norm, v_ev_w_in, v_ev_conv_a, v_ev_conv_b, v_ev_conv_b_bias, v_ev_gate_r_w, v_ev_gate_r_b, v_ev_gate_i_w, v_ev_gate_i_b, v_ev_lru_lambda, v_ev_w_out, v_od_norm, v_od_w_in, v_od_q_norm, v_od_kv_norm, v_od_w_uq, v_od_w_ukv, v_od_w_out, v_ffn_norm, v_ffn_w_up, v_ffn_conv_w, v_ffn_conv_b, v_ffn_w_down, v_final_norm):
    given = dict(locals())
    w = {n: given[n] for n in WEIGHTS}
    nb, seq, _ = x.shape
    dm = Dims(nb, seq)
    n_layers = len(LAYER_ORDER)

    wf = {n: w[n] for n in WEIGHTS if n not in SHARD_AXIS}
    gathers = [GatherComm(w, kind, idx) for kind, idx in LAYER_ORDER]
    wf.update(gather_at_entry(w, [n for n in SHARD_AXIS if n not in MATMUL_WEIGHTS], gathers[0], "gather_at_entry"))

    tail = dm.tp - dm.t_real
    meta = jnp.broadcast_to(wf["meta_tokens"][None], (nb, N_META, D_MODEL))
    h = jnp.concatenate([meta, x, jnp.zeros((nb, tail, D_MODEL), F32)], axis=1).reshape(dm.rows, D_MODEL)
    tgt = jnp.pad(loss_target, ((0, 0), (N_META, tail), (0, 0))).reshape(dm.rows, D_MODEL)
    tabq, tabk = rope_tables(dm.tp)

    params, saved = [], []
    for i, (kind, idx) in enumerate(LAYER_ORDER):
        wl = dict(wf)
        wl.update(gathers[i].weights())
        comm = gathers[i + 1] if i + 1 < n_layers else NoComm()
        if kind == "ev":
            p = even_params(wl, idx)
            h, sv = even_fwd(h, p, dm, comm)
        elif kind == "od":
            p = odd_params(wl, idx)
            h, sv = odd_fwd(h, p, tabq, tabk, dm, comm)
        else:
            p = ffn_params(wl, idx)
            h, sv = ffn_fwd(h, p, dm, comm)
        params.append(p)
        saved.append(sv)

    dh, loss, dfinal = loss_head(h, tgt, _row(wf["final_norm"]), dm.tp, dm.t_real, dm.n, "loss_head")
    loss = lax.psum(loss[0, 0], ("x", "y", "c"))

    c_idx = lax.axis_index("c").astype(jnp.int32).reshape(1)
    layer_grads = {n: {} for n in WEIGHTS}
    pending, reduces = NoComm(), []
    for i in reversed(range(n_layers)):
        kind, idx = LAYER_ORDER[i]
        if kind == "ev":
            dh, g = even_bwd(dh, saved[i], params[i], dm, pending)
            g = even_grads(g)
        elif kind == "od":
            dh, g = odd_bwd(dh, saved[i], params[i], tabq, tabk, dm, pending)
            g = odd_grads(g)
        else:
            dh, g = ffn_bwd(dh, saved[i], params[i], dm, pending)
            g = ffn_grads(g)
        for n in g:
            if n not in SHARD_AXIS:
                layer_grads[n][idx] = g[n]
        if i > 0:
            pending = ReduceComm({n: g[n] for n in LAYER_SHARDED[kind]}, {n: SHARD_AXIS[n] - 1 for n in SHARD_AXIS},
                                 c_idx, "%s%d" % (kind, idx))
            reduces.append((pending, idx))
    dh3 = dh.reshape(nb, dm.tp, D_MODEL)
    grad_x = dh3[:, N_META:dm.t_real]

    repl = [n for n in WEIGHTS if n not in SHARD_AXIS]
    layer_grads["final_norm"] = {0: dfinal[0]}
    repl_full = {n: (layer_grads[n][0] if n == "final_norm" else
                     jnp.stack([layer_grads[n][j] for j in range(w[n].shape[0])], axis=0)) for n in repl}
    tail_buf, tail_offs = pack_rows([repl_full[n] for n in repl], 0, 64)
    first = {n: g[n] for n in LAYER_SHARDED["ev"]}
    first["meta_tokens"] = jnp.sum(dh3[:, :N_META], axis=0)
    axes = {n: SHARD_AXIS[n] - 1 for n in SHARD_AXIS}
    axes["meta_tokens"] = SHARD_AXIS["meta_tokens"]
    last = ReduceComm(first, axes, c_idx, "first_layer", tail=tail_buf)
    last.run_alone("grad_first_layer")
    reduces.append((last, 0))

    red = {}
    for comm, idx in reduces:
        got, tail_piece = comm.results()
        for n, v_ in got.items():
            if n == "meta_tokens":
                red[n] = v_
            else:
                layer_grads[n][idx] = v_
    tails = own_block(run_stage(stage_gather_chips([tail_piece]), "grad_gather_replicated")[0], tail_piece)
    tails = tails.reshape(tail_buf.shape[0], LANE)
    for n, off in zip(repl, tail_offs):
        red[n] = unpack_rows(tails, off, w[n].shape)
    for n in SHARD_AXIS:
        if n != "meta_tokens":
            red[n] = jnp.stack([layer_grads[n][j] for j in range(w[n].shape[0])], axis=0)

    outs = [adamw(red[n], w[n], given["m_" + n], given["v_" + n], "adamw_" + n) for n in WEIGHTS]
    return (loss, grad_x, *[red[n] for n in WEIGHTS], *[o[0] for o in outs], *[o[1] for o in outs],
            *[o[2] for o in outs])
```
